```python
import math
import jax
import jax.numpy as jnp
from jax import lax
import numpy as np

D_MODEL = 1024
BATCH = 8
SEQ = 4096
DEPTH = 2

GRID_W = 64
CTX_LEN = 256
HEAD_DIM = 64
ROPE_BASE = 10000.0
SWA_Q_HEADS = 4
SWA_KV_HEADS = 2
SWA_WINDOW = 128
SWA_BLOCK = 128
NA_HEADS = 4
NA_KH = 8
NA_KW = 16
DIFF_HEADS = 4
DIFF_V_DIM = 2 * HEAD_DIM
DIFF_BLOCK = 128
Q_SIZES = [SWA_Q_HEADS * HEAD_DIM, NA_HEADS * HEAD_DIM, DIFF_HEADS * 2 * HEAD_DIM]
KV_SIZES = [SWA_KV_HEADS * HEAD_DIM, SWA_KV_HEADS * HEAD_DIM, NA_HEADS * HEAD_DIM, NA_HEADS * HEAD_DIM,
            DIFF_HEADS * 2 * HEAD_DIM, DIFF_HEADS * DIFF_V_DIM]
Q_COLS = sum(Q_SIZES)
KV_COLS = sum(KV_SIZES)
IN_COLS = Q_COLS + KV_COLS
MIX_WIDTH = (SWA_Q_HEADS + NA_HEADS) * HEAD_DIM + DIFF_HEADS * DIFF_V_DIM
N_EXPERTS = 256
TOP_K = 8
N_GROUPS = 8
TOPK_GROUPS = 4
EXPERT_FF = 256
SHARED_FF = 256
ROUTED_SCALE = 2.5
MOE_BLOCK = 128
N_MOD = 6
EPS = 1e-6
NEG_INF = -1e30

kernel_name = 'hybrid_dit_swa_na2d_diff_moe'


def rmsnorm(x, g):
    xf = x.astype(jnp.float32)
    y = xf * lax.rsqrt(jnp.mean(xf * xf, axis=-1, keepdims=True) + EPS)
    return (y * g.astype(jnp.float32)).astype(x.dtype)


def modulate(x, shift, scale):
    return x * (1.0 + scale) + shift


def axial_rope_tables(n_tokens):
    n_freq = HEAD_DIM // 4
    inv = 1.0 / (ROPE_BASE ** (jnp.arange(n_freq, dtype=jnp.float32) / n_freq))
    t = jnp.arange(n_tokens)
    row = (t // GRID_W).astype(jnp.float32)
    col = (t % GRID_W).astype(jnp.float32)
    ar = row[:, None] * inv
    ac = col[:, None] * inv
    return (jnp.cos(ar), jnp.sin(ar), jnp.cos(ac), jnp.sin(ac))


def _rotate(x, cos, sin):
    x1, x2 = jnp.split(x, 2, axis=-1)
    cos = cos[:, None, :].astype(x.dtype)
    sin = sin[:, None, :].astype(x.dtype)
    return jnp.concatenate([x1 * cos - x2 * sin, x2 * cos + x1 * sin], axis=-1)


def apply_axial_rope(x, rope):
    cr, sr, cc, sc = rope
    half = x.shape[-1] // 2
    return jnp.concatenate([_rotate(x[..., :half], cr, sr), _rotate(x[..., half:], cc, sc)], axis=-1)


def _split_cols(t, sizes):
    return jnp.split(t, [int(s) for s in np.cumsum(sizes)[:-1]], axis=-1)


def _heads(t, *shape):
    return t.reshape(t.shape[:-1] + shape)


def split_q(q):
    qa, qb, qd = _split_cols(q, Q_SIZES)
    return (_heads(qa, SWA_Q_HEADS, HEAD_DIM), _heads(qb, NA_HEADS, HEAD_DIM),
            _heads(qd, DIFF_HEADS, 2, HEAD_DIM))


def split_kv(kv):
    ka, va, kb, vb, kd, vd = _split_cols(kv, KV_SIZES)
    return (_heads(ka, SWA_KV_HEADS, HEAD_DIM), _heads(va, SWA_KV_HEADS, HEAD_DIM),
            _heads(kb, NA_HEADS, HEAD_DIM), _heads(vb, NA_HEADS, HEAD_DIM),
            _heads(kd, DIFF_HEADS, 2, HEAD_DIM), _heads(vd, DIFF_HEADS, DIFF_V_DIM))


def swa_latent(q, k, v, kc, vc, sink):
    B, S, H, D = q.shape
    KVH = k.shape[2]
    G = H // KVH
    nb = S // SWA_BLOCK
    scale = D ** -0.5
    qb = q.reshape(B, nb, SWA_BLOCK, KVH, G, D)

    def band(t):
        tp = jnp.pad(t, ((0, 0), (SWA_BLOCK, SWA_BLOCK), (0, 0), (0, 0)))
        tp = tp.reshape(B, nb + 2, SWA_BLOCK, KVH, D)
        return jnp.concatenate([tp[:, :-2], tp[:, 1:-1], tp[:, 2:]], axis=2)

    kb, vb = band(k), band(v)
    blk = jnp.arange(nb)[:, None]
    qpos = blk * SWA_BLOCK + jnp.arange(SWA_BLOCK)[None, :]
    kpos = blk * SWA_BLOCK - SWA_BLOCK + jnp.arange(3 * SWA_BLOCK)[None, :]
    valid = ((jnp.abs(qpos[:, :, None] - kpos[:, None, :]) <= SWA_WINDOW)
             & (kpos[:, None, :] >= 0) & (kpos[:, None, :] < S))
    s_loc = jnp.einsum('bnqhgd,bnkhd->bnhgqk', qb, kb).astype(jnp.float32) * scale
    s_loc = jnp.where(valid[None, :, None, None], s_loc, NEG_INF)
    s_ctx = jnp.einsum('bnqhgd,bchd->bnhgqc', qb, kc).astype(jnp.float32) * scale
    s_sink = jnp.broadcast_to(sink.astype(jnp.float32).reshape(KVH, G)[:, :, None, None],
                              s_loc.shape[:-1] + (1,))
    p = jax.nn.softmax(jnp.concatenate([s_loc, s_ctx, s_sink], axis=-1), axis=-1)
    L = 3 * SWA_BLOCK
    C = kc.shape[1]
    o = (jnp.einsum('bnhgqk,bnkhd->bnqhgd', p[..., :L].astype(v.dtype), vb)
         + jnp.einsum('bnhgqc,bchd->bnqhgd', p[..., L:L + C].astype(v.dtype), vc))
    return o.reshape(B, S, H * D)


def swa_context(q, k, v, sink):
    B, C, H, D = q.shape
    KVH = k.shape[2]
    G = H // KVH
    qg = q.reshape(B, C, KVH, G, D)
    s = jnp.einsum('bqhgd,bkhd->bhgqk', qg, k).astype(jnp.float32) * (D ** -0.5)
    s_sink = jnp.broadcast_to(sink.astype(jnp.float32).reshape(KVH, G)[None, :, :, None, None],
                              s.shape[:-1] + (1,))
    p = jax.nn.softmax(jnp.concatenate([s, s_sink], axis=-1), axis=-1)
    o = jnp.einsum('bhgqk,bkhd->bqhgd', p[..., :C].astype(v.dtype), v)
    return o.reshape(B, C, H * D)


def na2d_latent(q, k, v, kc, vc, rpb):
    B, S, H, D = q.shape
    rows = S // GRID_W
    kh = min(NA_KH, rows)
    ncb = GRID_W // NA_KW
    scale = D ** -0.5
    qg = q.reshape(B, rows, ncb, NA_KW, H, D)
    kg = k.reshape(B, rows, GRID_W, H, D)
    vg = v.reshape(B, rows, GRID_W, H, D)
    r = jnp.arange(rows)
    ridx = jnp.clip(r - kh // 2, 0, rows - kh)[:, None] + jnp.arange(kh)[None, :]
    cb = jnp.arange(ncb)
    cidx = jnp.clip(cb * NA_KW - NA_KW // 2, 0, GRID_W - 2 * NA_KW)[:, None] + jnp.arange(2 * NA_KW)[None, :]
    L = kh * 2 * NA_KW
    gi_r = ridx[:, None, :, None]
    gi_c = cidx[None, :, None, :]
    kn = kg[:, gi_r, gi_c].reshape(B, rows, ncb, L, H, D)
    vn = vg[:, gi_r, gi_c].reshape(B, rows, ncb, L, H, D)
    qcol = cb[:, None] * NA_KW + jnp.arange(NA_KW)[None, :]
    cstart = jnp.clip(qcol - NA_KW // 2, 0, GRID_W - NA_KW)
    col_ok = (cidx[:, None, :] >= cstart[:, :, None]) & (cidx[:, None, :] < cstart[:, :, None] + NA_KW)
    mask = jnp.broadcast_to(col_ok[:, :, None, :], (ncb, NA_KW, kh, 2 * NA_KW)).reshape(ncb, NA_KW, L)
    dr = ridx - r[:, None]
    dc = jnp.clip(cidx[:, None, :] - qcol[:, :, None], -(NA_KW - 1), NA_KW - 1)
    bias = rpb.astype(jnp.float32)[:, (dr + NA_KH - 1)[:, None, None, :, None], (dc + NA_KW - 1)[None, :, :, None, :]]
    bias = bias.reshape(H, rows, ncb, NA_KW, L)
    s_loc = jnp.einsum('brnqhd,brnkhd->bhrnqk', qg, kn).astype(jnp.float32) * scale + bias[None]
    s_loc = jnp.where(mask, s_loc, NEG_INF)
    s_ctx = jnp.einsum('brnqhd,bchd->bhrnqc', qg, kc).astype(jnp.float32) * scale
    p = jax.nn.softmax(jnp.concatenate([s_loc, s_ctx], axis=-1), axis=-1)
    o = (jnp.einsum('bhrnqk,brnkhd->brnqhd', p[..., :L].astype(v.dtype), vn)
         + jnp.einsum('bhrnqc,bchd->brnqhd', p[..., L:].astype(v.dtype), vc))
    return o.reshape(B, S, H * D)


def dense_context(q, k, v):
    B, C, H, D = q.shape
    s = jnp.einsum('bqhd,bkhd->bhqk', q, k).astype(jnp.float32) * (D ** -0.5)
    p = jax.nn.softmax(s, axis=-1)
    return jnp.einsum('bhqk,bkhd->bqhd', p.astype(v.dtype), v).reshape(B, C, H * D)


def diff_core(q1, q2, k1, k2, v, lam):
    scale = q1.shape[-1] ** -0.5
    s1 = jnp.einsum('bqhd,bkhd->bhqk', q1, k1).astype(jnp.float32) * scale
    s2 = jnp.einsum('bqhd,bkhd->bhqk', q2, k2).astype(jnp.float32) * scale
    p = jax.nn.softmax(s1, axis=-1) - lam * jax.nn.softmax(s2, axis=-1)
    return jnp.einsum('bhqk,bkhe->bqhe', p.astype(v.dtype), v)


def diff_latent(q1, q2, k1, k2, v, k1c, k2c, vc, lam):
    B, S, H, D = q1.shape
    nb = S // DIFF_BLOCK
    k1a = jnp.concatenate([k1, k1c], axis=1)
    k2a = jnp.concatenate([k2, k2c], axis=1)
    va = jnp.concatenate([v, vc], axis=1)

    def to_blocks(t):
        return t.reshape(B, nb, DIFF_BLOCK, H, D).transpose(1, 0, 2, 3, 4)

    o = lax.map(lambda qq: diff_core(qq[0], qq[1], k1a, k2a, va, lam), (to_blocks(q1), to_blocks(q2)))
    return o.transpose(1, 0, 2, 3, 4).reshape(B, S, H, DIFF_V_DIM)


def diff_post(o, subln_g, lam_init):
    return (rmsnorm(o, subln_g) * (1.0 - lam_init)).reshape(o.shape[0], o.shape[1], -1)


def moe_ffn(h, w_router, router_bias, w_gate, w_up, w_down, ws_gate, ws_up, ws_down):
    T, D = h.shape
    scores = jax.nn.sigmoid((h @ w_router).astype(jnp.float32))
    grouped = (scores + router_bias.astype(jnp.float32)).reshape(T, N_GROUPS, N_EXPERTS // N_GROUPS)
    group_score = jnp.sum(lax.top_k(grouped, 2)[0], axis=-1)
    _, gidx = lax.top_k(group_score, TOPK_GROUPS)
    gmask = jnp.any(gidx[:, :, None] == jnp.arange(N_GROUPS)[None, None, :], axis=1)
    masked = jnp.where(gmask[:, :, None], grouped, -jnp.inf).reshape(T, N_EXPERTS)
    _, eidx = lax.top_k(masked, TOP_K)
    w = jnp.take_along_axis(scores, eidx, axis=-1)
    w = w / jnp.sum(w, axis=-1, keepdims=True) * ROUTED_SCALE

    tk = T * TOP_K
    flat_e = eidx.reshape(-1)
    flat_tok = jnp.arange(tk, dtype=jnp.int32) // TOP_K
    order = jnp.argsort(flat_e)
    sorted_e = flat_e[order]
    counts = jnp.zeros((N_EXPERTS,), jnp.int32).at[flat_e].add(1)
    starts = jnp.cumsum(counts) - counts
    padded = (counts + MOE_BLOCK - 1) // MOE_BLOCK * MOE_BLOCK
    pend = jnp.cumsum(padded)
    pstart = pend - padded
    dest = pstart[sorted_e] + (jnp.arange(tk, dtype=jnp.int32) - starts[sorted_e])
    n_blk = -(-tk // MOE_BLOCK) + N_EXPERTS
    n_rows = n_blk * MOE_BLOCK
    row_tok = jnp.full((n_rows,), T, jnp.int32).at[dest].set(flat_tok[order])
    row_w = jnp.zeros((n_rows,), jnp.float32).at[dest].set(w.reshape(-1)[order])
    blk_e = jnp.minimum(jnp.searchsorted(pend, jnp.arange(n_blk, dtype=jnp.int32) * MOE_BLOCK, side='right'),
                        N_EXPERTS - 1)
    h_pad = jnp.concatenate([h, jnp.zeros((1, D), h.dtype)], axis=0)

    def expert_block(y, blk):
        toks, e, wt = blk
        xb = h_pad[toks]
        ob = (jax.nn.silu(xb @ w_gate[e]) * (xb @ w_up[e])) @ w_down[e]
        return y.at[toks].add(ob * wt[:, None].astype(ob.dtype)), None

    y, _ = lax.scan(expert_block, jnp.zeros((T + 1, D), h.dtype),
                    (row_tok.reshape(n_blk, MOE_BLOCK), blk_e, row_w.reshape(n_blk, MOE_BLOCK)))
    shared = (jax.nn.silu(h @ ws_gate) * (h @ ws_up)) @ ws_down
    return y[:T] + shared


def hybrid_layer(x, ctx, c, c_ctx, rope, layer_idx, last, w_mod, b_mod, g_mix, g_ffn, w_in, w_out,
                 sink, rpb, lam_q1, lam_k1, lam_q2, lam_k2, subln_g, w_router, router_bias,
                 w_gate, w_up, w_down, ws_gate, ws_up, ws_down):
    B, S, D = x.shape
    C = ctx.shape[1]
    mod = (jax.nn.silu(c) @ w_mod + b_mod)[:, None, :]
    sh_a, sc_a, g_a, sh_f, sc_f, g_f = jnp.split(mod, N_MOD, axis=-1)
    n_mod_c = 2 if last else N_MOD
    mod_c = jax.nn.silu(c_ctx) @ w_mod[:, :n_mod_c * D] + b_mod[:n_mod_c * D]
    mc = jnp.split(mod_c, n_mod_c)

    h = modulate(rmsnorm(x, g_mix), sh_a, sc_a)
    hc = modulate(rmsnorm(ctx, g_mix), mc[0], mc[1])
    proj = h @ w_in
    if last:
        kv_c = hc @ w_in[:, Q_COLS:]
    else:
        proj_c = hc @ w_in
        kv_c = proj_c[..., Q_COLS:]
    qa, qb, qd = split_q(proj[..., :Q_COLS])
    ka, va, kb, vb, kd, vd = split_kv(proj[..., Q_COLS:])
    kac, vac, kbc, vbc, kdc, vdc = split_kv(kv_c)

    lam_init = 0.8 - 0.6 * math.exp(-0.3 * layer_idx)
    lam = (jnp.exp(jnp.sum(lam_q1.astype(jnp.float32) * lam_k1.astype(jnp.float32)))
           - jnp.exp(jnp.sum(lam_q2.astype(jnp.float32) * lam_k2.astype(jnp.float32))) + lam_init)

    o_a = swa_latent(apply_axial_rope(qa, rope), apply_axial_rope(ka, rope), va, kac, vac, sink)
    o_b = na2d_latent(qb, kb, vb, kbc, vbc, rpb)
    o_d = diff_latent(apply_axial_rope(qd[..., 0, :], rope), apply_axial_rope(qd[..., 1, :], rope),
                      apply_axial_rope(kd[..., 0, :], rope), apply_axial_rope(kd[..., 1, :], rope), vd,
                      kdc[..., 0, :], kdc[..., 1, :], vdc, lam)
    o_d = diff_post(o_d, subln_g, lam_init)
    x = x + g_a * (jnp.concatenate([o_a, o_b, o_d], axis=-1) @ w_out)

    if not last:
        qac, qbc, qdc = split_q(proj_c[..., :Q_COLS])
        oc_a = swa_context(qac, kac, vac, sink)
        oc_b = dense_context(qbc, kbc, vbc)
        oc_d = diff_post(diff_core(qdc[..., 0, :], qdc[..., 1, :], kdc[..., 0, :], kdc[..., 1, :], vdc, lam),
                         subln_g, lam_init)
        ctx = ctx + mc[2] * (jnp.concatenate([oc_a, oc_b, oc_d], axis=-1) @ w_out)

    hf = modulate(rmsnorm(x, g_ffn), sh_f, sc_f)
    moe_w = (w_router, router_bias, w_gate, w_up, w_down, ws_gate, ws_up, ws_down)
    if last:
        x = x + g_f * moe_ffn(hf.reshape(B * S, D), *moe_w).reshape(B, S, D)
    else:
        hfc = modulate(rmsnorm(ctx, g_ffn), mc[3], mc[4])
        y = moe_ffn(jnp.concatenate([hf.reshape(B * S, D), hfc.reshape(B * C, D)], axis=0), *moe_w)
        x = x + g_f * y[:B * S].reshape(B, S, D)
        ctx = ctx + mc[5] * y[B * S:].reshape(B, C, D)
    return x, ctx


def setup_inputs(seed: int = 0) -> dict:
    key = jax.random.key(seed)
    ks = jax.random.split(key, 26)
    D = D_MODEL
    E = N_EXPERTS
    F = EXPERT_FF

    def nrm(k, shape, scale):
        return jax.random.normal(k, shape, jnp.float32) * scale

    return {
        'x': nrm(ks[0], (BATCH, SEQ, D), 1.0),
        'c': nrm(ks[1], (BATCH, D), 1.0),
        'ctx': nrm(ks[2], (BATCH, CTX_LEN, D), 1.0),
        'c_ctx': nrm(ks[3], (D,), 1.0),
        'w_mod': nrm(ks[4], (DEPTH, D, N_MOD * D), 0.5 * D ** -0.5),
        'b_mod': nrm(ks[5], (DEPTH, N_MOD * D), 0.02),
        'g_mix': 1.0 + nrm(ks[6], (DEPTH, D), 0.02),
        'g_ffn': 1.0 + nrm(ks[7], (DEPTH, D), 0.02),
        'w_in': nrm(ks[8], (DEPTH, D, IN_COLS), D ** -0.5),
        'w_out': nrm(ks[9], (DEPTH, MIX_WIDTH, D), MIX_WIDTH ** -0.5),
        'attn_sink': nrm(ks[10], (DEPTH, SWA_Q_HEADS), 0.5),
        'na_rpb': nrm(ks[11], (DEPTH, NA_HEADS, 2 * NA_KH - 1, 2 * NA_KW - 1), 0.1),
        'lam_q1': nrm(ks[12], (DEPTH, HEAD_DIM), 0.1),
        'lam_k1': nrm(ks[13], (DEPTH, HEAD_DIM), 0.1),
        'lam_q2': nrm(ks[14], (DEPTH, HEAD_DIM), 0.1),
        'lam_k2': nrm(ks[15], (DEPTH, HEAD_DIM), 0.1),
        'subln_g': 1.0 + nrm(ks[16], (DEPTH, DIFF_V_DIM), 0.02),
        'w_router': nrm(ks[17], (DEPTH, D, E), D ** -0.5),
        'router_bias': nrm(ks[18], (DEPTH, E), 0.01),
        'w_gate': nrm(ks[19], (DEPTH, E, D, F), D ** -0.5),
        'w_up': nrm(ks[20], (DEPTH, E, D, F), D ** -0.5),
        'w_down': nrm(ks[21], (DEPTH, E, F, D), F ** -0.5),
        'ws_gate': nrm(ks[22], (DEPTH, D, SHARED_FF), D ** -0.5),
        'ws_up': nrm(ks[23], (DEPTH, D, SHARED_FF), D ** -0.5),
        'ws_down': nrm(ks[24], (DEPTH, SHARED_FF, D), SHARED_FF ** -0.5),
        'g_final': 1.0 + nrm(ks[25], (D,), 0.02),
    }


def reference(x, c, ctx, c_ctx, w_mod, b_mod, g_mix, g_ffn, w_in, w_out, attn_sink, na_rpb,
              lam_q1, lam_k1, lam_q2, lam_k2, subln_g, w_router, router_bias, w_gate, w_up, w_down,
              ws_gate, ws_up, ws_down, g_final):
    rope = axial_rope_tables(x.shape[1])
    for i in range(DEPTH):
        x, ctx = hybrid_layer(x, ctx, c, c_ctx, rope, i, i == DEPTH - 1,
                              w_mod[i], b_mod[i], g_mix[i], g_ffn[i], w_in[i], w_out[i],
                              attn_sink[i], na_rpb[i], lam_q1[i], lam_k1[i], lam_q2[i], lam_k2[i],
                              subln_g[i], w_router[i], router_bias[i], w_gate[i], w_up[i], w_down[i],
                              ws_gate[i], ws_up[i], ws_down[i])
    return rmsnorm(x, g_final)
```

```python
import functools
import math

import numpy as np
import jax
import jax.numpy as jnp
from jax import lax
from jax.experimental import pallas as pl
from jax.experimental.pallas import tpu as pltpu

F32 = jnp.float32
BF16 = jnp.bfloat16
I32 = jnp.int32
U32 = jnp.uint32

GRID_W = 64
HEAD_DIM = 64
ROPE_BASE = 10000.0
SWA_WINDOW = 128
NA_KH = 8
NA_KW = 16
N_HEADS = 4
N_EXPERTS = 256
TOP_K = 8
N_GROUPS = 8
TOPK_GROUPS = 4
ROUTED_SCALE = 2.5
MOE_BLOCK = 128
EPS = 1e-6
NEG_INF = -1e30
Q_COLS = 1024
IN_COLS = 2816
MIX_WIDTH = 1024

LANES = 128
TOK_TILE = 256
ATT_TQ = 256
NA_QROWS = 4
NA_KROWS = NA_QROWS + NA_KH - 1
DIFF_KCHUNK = 512
VMEM_LIMIT = 48 * 1024 * 1024


def _params(sem):
    return pltpu.CompilerParams(dimension_semantics=sem, vmem_limit_bytes=VMEM_LIMIT)


def _dot(a, b):
    return jnp.dot(a, b, preferred_element_type=F32)


def _dot_nt(a, b):
    return lax.dot_general(a, b, (((1,), (1,)), ((), ())), preferred_element_type=F32)


def _split_bf16(a):
    hi = a.astype(BF16)
    lo = (a - hi.astype(F32)).astype(BF16)
    return hi, lo


def _sigmoid(x):
    return 1.0 / (1.0 + jnp.exp(-x))


def _rms(x, g):
    return x * lax.rsqrt(jnp.mean(x * x, axis=-1, keepdims=True) + EPS) * g


def _norm_mod(x, g, shift, scale):
    return _rms(x, g) * (1.0 + scale) + shift


def _mod_kernel(c_ref, w_ref, b_ref, o_ref):
    c = c_ref[...]
    a_hi, a_lo = _split_bf16(c * _sigmoid(c))
    w_hi, w_lo = _split_bf16(w_ref[...])
    o_ref[...] = _dot(a_hi, w_hi) + _dot(a_hi, w_lo) + _dot(a_lo, w_hi) + b_ref[...]


def _modulation(cs, w_mod, b_mod):
    n, d = cs.shape
    cols = w_mod.shape[1]
    tn = 1536
    return pl.pallas_call(
        _mod_kernel,
        grid=(cols // tn,),
        in_specs=[pl.BlockSpec((n, d), lambda j: (0, 0)),
                  pl.BlockSpec((d, tn), lambda j: (0, j)),
                  pl.BlockSpec((1, tn), lambda j: (0, j))],
        out_specs=pl.BlockSpec((n, tn), lambda j: (0, j)),
        out_shape=jax.ShapeDtypeStruct((n, cols), F32),
        compiler_params=_params(("arbitrary",)),
        name="modulation",
    )(cs, w_mod, b_mod.reshape(1, cols))


def _proj_kernel(x_ref, g_ref, mod_ref, cos_ref, sin_ref, w_ref,
                 qa_ref, qb_ref, qd_ref, ka_ref, va_ref, kb_ref, vb_ref, kd_ref, vd_ref, h_scr):
    h = _norm_mod(x_ref[0], g_ref[...], mod_ref[0, 0:1, :], mod_ref[0, 1:2, :])
    h_scr[...] = h.astype(BF16)
    cos = cos_ref[...]
    sin = sin_ref[...]
    tm = h.shape[0]
    lane = lax.broadcasted_iota(I32, (tm, LANES), 1)
    first16 = (lane & 16) == 0
    lo64 = lane < HEAD_DIM

    def rope(v):
        partner = jnp.where(first16, pltpu.roll(v, LANES - 16, 1), pltpu.roll(v, 16, 1))
        return v * cos + partner * sin

    def mm(c0):
        return _dot(h_scr[...], w_ref[:, c0:c0 + 2 * LANES])

    a = mm(0) * (HEAD_DIM ** -0.5)
    c0 = rope(a[:, :LANES])
    c1 = rope(a[:, LANES:])
    zero = jnp.zeros_like(c0)
    qa_ref[0, :, 0 * LANES:1 * LANES] = jnp.where(lo64, c0, zero).astype(BF16)
    qa_ref[0, :, 1 * LANES:2 * LANES] = jnp.where(lo64, pltpu.roll(c0, HEAD_DIM, 1), zero).astype(BF16)
    qa_ref[0, :, 2 * LANES:3 * LANES] = jnp.where(lo64, zero, pltpu.roll(c1, HEAD_DIM, 1)).astype(BF16)
    qa_ref[0, :, 3 * LANES:4 * LANES] = jnp.where(lo64, zero, c1).astype(BF16)
    qb_ref[0] = (mm(256) * (HEAD_DIM ** -0.5)).astype(BF16)
    for j in range(2):
        a = mm(512 + 256 * j) * (HEAD_DIM ** -0.5)
        qd_ref[0, :, 256 * j:256 * j + LANES] = rope(a[:, :LANES]).astype(BF16)
        qd_ref[0, :, 256 * j + LANES:256 * (j + 1)] = rope(a[:, LANES:]).astype(BF16)
    a = mm(1024)
    ka_ref[0] = rope(a[:, :LANES]).astype(BF16)
    va_ref[0] = a[:, LANES:].astype(BF16)
    kb_ref[0] = mm(1280).astype(BF16)
    vb_ref[0] = mm(1536).astype(BF16)
    for j in range(2):
        a = mm(1792 + 256 * j)
        kd_ref[0, :, 256 * j:256 * j + LANES] = rope(a[:, :LANES]).astype(BF16)
        kd_ref[0, :, 256 * j + LANES:256 * (j + 1)] = rope(a[:, LANES:]).astype(BF16)
        vd_ref[0, :, 256 * j:256 * (j + 1)] = mm(2304 + 256 * j).astype(BF16)


def _projection(xc, g, modall, cos, sin, w_in, s_lat):
    b, sall, d = xc.shape
    tm = TOK_TILE
    nlat = s_lat // tm
    row = lambda width: pl.BlockSpec((1, tm, width), lambda i, j: (i, j, 0))
    widths = (512, 256, 512, 128, 128, 256, 256, 512, 512)
    return pl.pallas_call(
        _proj_kernel,
        grid=(b, sall // tm),
        in_specs=[row(d),
                  pl.BlockSpec((1, d), lambda i, j: (0, 0)),
                  pl.BlockSpec((1, 8, d), lambda i, j: (2 * i + (j >= nlat).astype(I32), 0, 0)),
                  pl.BlockSpec((tm, LANES), lambda i, j: (j, 0)),
                  pl.BlockSpec((tm, LANES), lambda i, j: (j, 0)),
                  pl.BlockSpec((d, IN_COLS), lambda i, j: (0, 0))],
        out_specs=[row(w) for w in widths],
        out_shape=[jax.ShapeDtypeStruct((b, sall, w), BF16) for w in widths],
        scratch_shapes=[pltpu.VMEM((tm, d), BF16)],
        compiler_params=_params(("arbitrary", "arbitrary")),
        name="projection",
    )(xc, g, modall, cos, sin, w_in)


def _half_mask(q, half):
    lane = lax.broadcasted_iota(I32, q.shape, 1)
    keep = (lane < HEAD_DIM) if half == 0 else (lane >= HEAD_DIM)
    return jnp.where(keep, q, jnp.zeros_like(q))


def _merge_halves(lo_part, hi_part):
    lane = lax.broadcasted_iota(I32, lo_part.shape, 1)
    return jnp.where(lane < HEAD_DIM, lo_part, hi_part)


def _gqa_rows(q_ref, g):
    return jnp.concatenate([q_ref[0, :, (2 * g) * LANES:(2 * g + 1) * LANES],
                            q_ref[0, :, (2 * g + 1) * LANES:(2 * g + 2) * LANES]], axis=0)


def _gqa_sink(sink_ref, g, tq):
    rowi = lax.broadcasted_iota(I32, (2 * tq, 1), 0)
    return jnp.where(rowi < tq, sink_ref[2 * g], sink_ref[2 * g + 1])


def _gqa_store(o_ref, g, o, tq):
    top, bot = o[:tq], o[tq:]
    if g == 0:
        chunk = _merge_halves(top, pltpu.roll(bot, HEAD_DIM, 1))
    else:
        chunk = _merge_halves(pltpu.roll(top, HEAD_DIM, 1), bot)
    o_ref[0, :, g * LANES:(g + 1) * LANES] = chunk.astype(BF16)


def _lam(lamv_ref, lam_init):
    v = lamv_ref[...]
    a = jnp.sum(v[0:1] * v[1:2], axis=-1, keepdims=True)
    b = jnp.sum(v[2:3] * v[3:4], axis=-1, keepdims=True)
    return jnp.exp(a) - jnp.exp(b) + lam_init


def _diff_post(o, subg, lam_init):
    return _rms(o, subg) * (1.0 - lam_init)


def _swa_kernel(sink_ref, q_ref, k_ref, v_ref, o_ref, *, s_lat, c_len):
    tq = ATT_TQ
    tk = tq + 2 * SWA_WINDOW
    q0 = pl.program_id(1) * tq
    ks = pl.multiple_of(jnp.clip(q0 - SWA_WINDOW, 0, s_lat - tk), SWA_WINDOW)
    kwin = k_ref[0, pl.ds(ks, tk), :]
    vwin = v_ref[0, pl.ds(ks, tk), :]
    kc = k_ref[0, s_lat:s_lat + c_len, :]
    vc = v_ref[0, s_lat:s_lat + c_len, :]
    qpos = q0 + lax.broadcasted_iota(I32, (tq, tk), 0)
    kpos = ks + lax.broadcasted_iota(I32, (tq, tk), 1)
    valid = jnp.abs(qpos - kpos) <= SWA_WINDOW
    valid = jnp.concatenate([valid, valid], axis=0)
    for g in range(2):
        q2 = _gqa_rows(q_ref, g)
        s_loc = jnp.where(valid, _dot_nt(q2, kwin), NEG_INF)
        s_ctx = _dot_nt(q2, kc)
        snk = _gqa_sink(sink_ref, g, tq)
        m = jnp.maximum(jnp.maximum(jnp.max(s_loc, axis=-1, keepdims=True),
                                    jnp.max(s_ctx, axis=-1, keepdims=True)), snk)
        e_loc = jnp.exp(s_loc - m)
        e_ctx = jnp.exp(s_ctx - m)
        den = (jnp.sum(e_loc, axis=-1, keepdims=True) + jnp.sum(e_ctx, axis=-1, keepdims=True)
               + jnp.exp(snk - m))
        o = (_dot(e_loc.astype(BF16), vwin) + _dot(e_ctx.astype(BF16), vc)) / den
        _gqa_store(o_ref, g, o, tq)


def _swa(sink, qa, ka, va, s_lat):
    b, sall, _ = qa.shape
    kv = pl.BlockSpec((1, sall, LANES), lambda i, j: (i, 0, 0))
    return pl.pallas_call(
        functools.partial(_swa_kernel, s_lat=s_lat, c_len=sall - s_lat),
        grid=(b, s_lat // ATT_TQ),
        in_specs=[pl.BlockSpec(memory_space=pltpu.SMEM),
                  pl.BlockSpec((1, ATT_TQ, 512), lambda i, j: (i, j, 0)), kv, kv],
        out_specs=pl.BlockSpec((1, ATT_TQ, 256), lambda i, j: (i, j, 0)),
        out_shape=jax.ShapeDtypeStruct((b, s_lat, 256), BF16),
        compiler_params=_params(("arbitrary", "arbitrary")),
        name="swa",
    )(sink, qa, ka, va)


def _na_kernel(q_ref, k_ref, v_ref, bm_ref, o_ref, *, s_lat, c_len):
    rows = s_lat // GRID_W
    nk = NA_KROWS * GRID_W
    r0 = pl.program_id(1) * NA_QROWS
    ks = pl.multiple_of(jnp.clip(r0 - NA_KH // 2, 0, rows - NA_KROWS) * GRID_W, GRID_W)
    for c in range(2):
        sl = slice(c * LANES, (c + 1) * LANES)
        kwin = k_ref[0, pl.ds(ks, nk), sl]
        vwin = v_ref[0, pl.ds(ks, nk), sl]
        kc = k_ref[0, s_lat:s_lat + c_len, sl]
        vc = v_ref[0, s_lat:s_lat + c_len, sl]
        q = q_ref[0, :, sl]
        outs = []
        for half in range(2):
            qm = _half_mask(q, half)
            s_loc = _dot_nt(qm, kwin) + bm_ref[0, 2 * c + half]
            s_ctx = _dot_nt(qm, kc)
            m = jnp.maximum(jnp.max(s_loc, axis=-1, keepdims=True), jnp.max(s_ctx, axis=-1, keepdims=True))
            e_loc = jnp.exp(s_loc - m)
            e_ctx = jnp.exp(s_ctx - m)
            den = jnp.sum(e_loc, axis=-1, keepdims=True) + jnp.sum(e_ctx, axis=-1, keepdims=True)
            outs.append((_dot(e_loc.astype(BF16), vwin) + _dot(e_ctx.astype(BF16), vc)) / den)
        o_ref[0, :, sl] = _merge_halves(outs[0], outs[1]).astype(BF16)


def _na_bias_mask(rpb, rows):
    nq = NA_QROWS * GRID_W
    nk = NA_KROWS * GRID_W
    qi = np.arange(nq)
    kj = np.arange(nk)
    tabs = []
    for r0 in (0, NA_QROWS, rows - NA_QROWS):
        ksr = int(np.clip(r0 - NA_KH // 2, 0, rows - NA_KROWS))
        r = r0 + qi // GRID_W
        qc = qi % GRID_W
        kr = ksr + kj // GRID_W
        kc = kj % GRID_W
        rs = np.clip(r - NA_KH // 2, 0, rows - NA_KH)
        row_ok = (kr[None, :] >= rs[:, None]) & (kr[None, :] < rs[:, None] + NA_KH)
        cstart = np.clip(qc - NA_KW // 2, 0, GRID_W - NA_KW)
        col_ok = (kc[None, :] >= cstart[:, None]) & (kc[None, :] < cstart[:, None] + NA_KW)
        dr = np.clip(kr[None, :] - r[:, None], -(NA_KH - 1), NA_KH - 1) + NA_KH - 1
        dc = np.clip(kc[None, :] - qc[:, None], -(NA_KW - 1), NA_KW - 1) + NA_KW - 1
        bias = rpb.astype(F32)[:, dr, dc]
        tabs.append(jnp.where((row_ok & col_ok)[None], bias, NEG_INF))
    return jnp.stack(tabs)


def _na(qb, kb, vb, bm, s_lat):
    b, sall, _ = qb.shape
    nq = NA_QROWS * GRID_W
    nsteps = s_lat // nq
    kv = pl.BlockSpec((1, sall, 256), lambda i, j: (i, 0, 0))

    def bm_map(i, j):
        return (jnp.where(j == 0, 0, jnp.where(j == nsteps - 1, 2, 1)), 0, 0, 0)

    return pl.pallas_call(
        functools.partial(_na_kernel, s_lat=s_lat, c_len=sall - s_lat),
        grid=(b, nsteps),
        in_specs=[pl.BlockSpec((1, nq, 256), lambda i, j: (i, j, 0)), kv, kv,
                  pl.BlockSpec((1, N_HEADS, nq, NA_KROWS * GRID_W), bm_map)],
        out_specs=pl.BlockSpec((1, nq, 256), lambda i, j: (i, j, 0)),
        out_shape=jax.ShapeDtypeStruct((b, s_lat, 256), BF16),
        compiler_params=_params(("arbitrary", "arbitrary")),
        name="na2d",
    )(qb, kb, vb, bm)


def _diff_kernel(lamv_ref, subg_ref, q_ref, k_ref, v_ref, o_ref, s_scr, *, n_keys, lam_init):
    tq = ATT_TQ
    q = q_ref[0]
    q12 = jnp.concatenate([_half_mask(q, 0), _half_mask(q, 1)], axis=0)
    chunks = [(c0, min(DIFF_KCHUNK, n_keys - c0)) for c0 in range(0, n_keys, DIFF_KCHUNK)]
    m = jnp.full((2 * tq, 1), NEG_INF, F32)
    for c0, n in chunks:
        s = _dot_nt(q12, k_ref[0, c0:c0 + n, :])
        s_scr[:, c0:c0 + n] = s
        m = jnp.maximum(m, jnp.max(s, axis=-1, keepdims=True))
    den = jnp.zeros((2 * tq, 1), F32)
    for c0, n in chunks:
        e = jnp.exp(s_scr[:, c0:c0 + n] - m)
        s_scr[:, c0:c0 + n] = e
        den = den + jnp.sum(e, axis=-1, keepdims=True)
    r1 = 1.0 / den[:tq]
    r2 = _lam(lamv_ref, lam_init) / den[tq:]
    acc = jnp.zeros((tq, LANES), F32)
    for c0, n in chunks:
        p = s_scr[:tq, c0:c0 + n] * r1 - s_scr[tq:, c0:c0 + n] * r2
        acc = acc + _dot(p.astype(BF16), v_ref[0, c0:c0 + n, :])
    o_ref[0] = _diff_post(acc, subg_ref[...], lam_init).astype(BF16)


def _diff(lamv, subg, qd, kd, vd, s_lat, lam_init):
    b, sall, _ = qd.shape
    kv = pl.BlockSpec((1, sall, LANES), lambda i, h, j: (i, 0, h))
    return pl.pallas_call(
        functools.partial(_diff_kernel, n_keys=sall, lam_init=lam_init),
        grid=(b, N_HEADS, s_lat // ATT_TQ),
        in_specs=[pl.BlockSpec((8, LANES), lambda i, h, j: (0, 0)),
                  pl.BlockSpec((1, LANES), lambda i, h, j: (0, 0)),
                  pl.BlockSpec((1, ATT_TQ, LANES), lambda i, h, j: (i, j, h)), kv, kv],
        out_specs=pl.BlockSpec((1, ATT_TQ, LANES), lambda i, h, j: (i, j, h)),
        out_shape=jax.ShapeDtypeStruct((b, s_lat, 512), BF16),
        scratch_shapes=[pltpu.VMEM((2 * ATT_TQ, sall), F32)],
        compiler_params=_params(("arbitrary", "arbitrary", "arbitrary")),
        name="diff_attn",
    )(lamv, subg, qd, kd, vd)


def _ctx_kernel(sink_ref, lamv_ref, subg_ref, qa_ref, ka_ref, va_ref, qb_ref, kb_ref, vb_ref,
                qd_ref, kd_ref, vd_ref, oa_ref, ob_ref, od_ref, *, lam_init):
    cl = qa_ref.shape[1]
    ka = ka_ref[0]
    va = va_ref[0]
    for g in range(2):
        q2 = _gqa_rows(qa_ref, g)
        s = _dot_nt(q2, ka)
        snk = _gqa_sink(sink_ref, g, cl)
        m = jnp.maximum(jnp.max(s, axis=-1, keepdims=True), snk)
        e = jnp.exp(s - m)
        den = jnp.sum(e, axis=-1, keepdims=True) + jnp.exp(snk - m)
        _gqa_store(oa_ref, g, _dot(e.astype(BF16), va) / den, cl)
    for c in range(2):
        sl = slice(c * LANES, (c + 1) * LANES)
        outs = []
        for half in range(2):
            s = _dot_nt(_half_mask(qb_ref[0, :, sl], half), kb_ref[0, :, sl])
            e = jnp.exp(s - jnp.max(s, axis=-1, keepdims=True))
            outs.append(_dot(e.astype(BF16), vb_ref[0, :, sl]) / jnp.sum(e, axis=-1, keepdims=True))
        ob_ref[0, :, sl] = _merge_halves(outs[0], outs[1]).astype(BF16)
    lam = _lam(lamv_ref, lam_init)
    for h in range(N_HEADS):
        sl = slice(h * LANES, (h + 1) * LANES)
        q = qd_ref[0, :, sl]
        q12 = jnp.concatenate([_half_mask(q, 0), _half_mask(q, 1)], axis=0)
        s = _dot_nt(q12, kd_ref[0, :, sl])
        e = jnp.exp(s - jnp.max(s, axis=-1, keepdims=True))
        den = jnp.sum(e, axis=-1, keepdims=True)
        p = e[:cl] * (1.0 / den[:cl]) - e[cl:] * (lam / den[cl:])
        o = _dot(p.astype(BF16), vd_ref[0, :, sl])
        od_ref[0, :, sl] = _diff_post(o, subg_ref[...], lam_init).astype(BF16)


def _ctx_attention(sink, lamv, subg, qa, ka, va, qb, kb, vb, qd, kd, vd, s_lat, lam_init):
    b, sall, _ = qa.shape
    cl = sall - s_lat
    blk = s_lat // cl
    row = lambda width: pl.BlockSpec((1, cl, width), lambda i: (i, blk, 0))
    out = lambda width: pl.BlockSpec((1, cl, width), lambda i: (i, 0, 0))
    return pl.pallas_call(
        functools.partial(_ctx_kernel, lam_init=lam_init),
        grid=(b,),
        in_specs=[pl.BlockSpec(memory_space=pltpu.SMEM),
                  pl.BlockSpec((8, LANES), lambda i: (0, 0)),
                  pl.BlockSpec((1, LANES), lambda i: (0, 0)),
                  row(512), row(128), row(128), row(256), row(256), row(256), row(512), row(512), row(512)],
        out_specs=[out(256), out(256), out(512)],
        out_shape=[jax.ShapeDtypeStruct((b, cl, w), BF16) for w in (256, 256, 512)],
        compiler_params=_params(("arbitrary",)),
        name="ctx_attention",
    )(sink, lamv, subg, qa, ka, va, qb, kb, vb, qd, kd, vd)


def _out_kernel(x_ref, w_ref, mod_ref, *refs, nlat):
    xo_ref = refs[-1]

    def emit(oa_ref, ob_ref, od_ref):
        attn = (_dot(oa_ref[0], w_ref[0:256, :]) + _dot(ob_ref[0], w_ref[256:512, :])
                + _dot(od_ref[0], w_ref[512:1024, :]))
        xo_ref[0] = x_ref[0] + mod_ref[0, 2:3, :] * attn

    if len(refs) == 4:
        emit(*refs[:3])
    else:
        is_ctx = pl.program_id(1) >= nlat
        pl.when(jnp.logical_not(is_ctx))(lambda: emit(*refs[:3]))
        pl.when(is_ctx)(lambda: emit(*refs[3:6]))


def _out_projection(xc, lat, ctx, w_out, modall, s_lat):
    b, sall, d = xc.shape
    tm = TOK_TILE
    nlat = s_lat // tm
    n_rows = s_lat if ctx is None else sall
    row = lambda width: pl.BlockSpec((1, tm, width), lambda i, j: (i, j, 0))
    lat_row = lambda width: pl.BlockSpec((1, tm, width), lambda i, j: (i, jnp.minimum(j, nlat - 1), 0))
    ctx_row = lambda width: pl.BlockSpec((1, tm, width), lambda i, j: (i, 0, 0))
    widths = (256, 256, 512)
    specs = [lat_row(w) for w in widths] + ([] if ctx is None else [ctx_row(w) for w in widths])
    return pl.pallas_call(
        functools.partial(_out_kernel, nlat=nlat),
        grid=(b, n_rows // tm),
        in_specs=[row(d),
                  pl.BlockSpec((MIX_WIDTH, d), lambda i, j: (0, 0)),
                  pl.BlockSpec((1, 8, d), lambda i, j: (2 * i + (j >= nlat).astype(I32), 0, 0))] + specs,
        out_specs=row(d),
        out_shape=jax.ShapeDtypeStruct((b, n_rows, d), F32),
        compiler_params=_params(("arbitrary", "arbitrary")),
        name="out_projection",
    )(xc, w_out, modall, *lat, *(() if ctx is None else ctx))


def _first_argmax(v, iota, n):
    m = jnp.max(v, axis=0, keepdims=True)
    ix = jnp.min(jnp.where(v == m, iota, float(n)), axis=0, keepdims=True)
    return m, ix


def _router_kernel(x_ref, g_ref, mod_ref, whi_ref, wlo_ref, bias_ref, tri_ref,
                   eidx_ref, w_ref, pos_ref, cnt_ref):
    @pl.when(pl.program_id(0) == 0)
    def _():
        cnt_ref[...] = jnp.zeros_like(cnt_ref)

    hf = _norm_mod(x_ref[...], g_ref[...], mod_ref[0, 3:4, :], mod_ref[0, 4:5, :])
    tm = hf.shape[0]
    h_hi, h_lo = _split_bf16(hf)
    whi = whi_ref[...]
    logits = _dot_nt(whi, h_hi) + _dot_nt(whi, h_lo) + _dot_nt(wlo_ref[...], h_hi)
    scores = _sigmoid(logits)
    biased = scores + bias_ref[...]
    gsz = N_EXPERTS // N_GROUPS
    iota_g = lax.broadcasted_iota(I32, (gsz, tm), 0).astype(F32)
    gscore = []
    for g in range(N_GROUPS):
        v = biased[g * gsz:(g + 1) * gsz]
        m1, i1 = _first_argmax(v, iota_g, gsz)
        m2 = jnp.max(jnp.where(iota_g == i1, -jnp.inf, v), axis=0, keepdims=True)
        gscore.append(m1 + m2)
    cur = jnp.concatenate(gscore, axis=0)
    iota_n = lax.broadcasted_iota(I32, (N_GROUPS, tm), 0).astype(F32)
    gsel = jnp.zeros((N_GROUPS, tm), F32)
    for _ in range(TOPK_GROUPS):
        _, ix = _first_argmax(cur, iota_n, N_GROUPS)
        hit = iota_n == ix
        gsel = jnp.where(hit, 1.0, gsel)
        cur = jnp.where(hit, -jnp.inf, cur)
    masked = jnp.concatenate(
        [jnp.where(gsel[g:g + 1] > 0.5, biased[g * gsz:(g + 1) * gsz], -jnp.inf) for g in range(N_GROUPS)],
        axis=0)
    iota_e = lax.broadcasted_iota(I32, (N_EXPERTS, tm), 0).astype(F32)
    onehot = jnp.zeros((N_EXPERTS, tm), F32)
    idxs, ws = [], []
    for _ in range(TOP_K):
        _, ix = _first_argmax(masked, iota_e, N_EXPERTS)
        hit = iota_e == ix
        ws.append(jnp.sum(jnp.where(hit, scores, 0.0), axis=0, keepdims=True))
        masked = jnp.where(hit, -jnp.inf, masked)
        onehot = jnp.where(hit, 1.0, onehot)
        idxs.append(ix)
    cnt = cnt_ref[...]
    rank = _dot(onehot.astype(BF16), tri_ref[...]) + cnt
    wsum = ws[0]
    for wk in ws[1:]:
        wsum = wsum + wk
    for k in range(TOP_K):
        pos = jnp.sum(jnp.where(iota_e == idxs[k], rank, 0.0), axis=0, keepdims=True)
        eidx_ref[0, k:k + 1, :] = idxs[k].astype(I32)
        pos_ref[0, k:k + 1, :] = pos.astype(I32)
        w_ref[0, k:k + 1, :] = ws[k] / wsum * ROUTED_SCALE
    cnt_ref[...] = cnt + jnp.sum(onehot, axis=1, keepdims=True)


def _router(xflat, g, modall, whi, wlo, bias, tri, tiles_per_batch, nlat):
    t, d = xflat.shape
    tm = TOK_TILE
    nt = t // tm
    meta = pl.BlockSpec((1, TOP_K, tm), lambda i: (i, 0, 0))

    def mod_map(i):
        return (2 * (i // tiles_per_batch) + ((i % tiles_per_batch) >= nlat).astype(I32), 0, 0)

    return pl.pallas_call(
        _router_kernel,
        grid=(nt,),
        in_specs=[pl.BlockSpec((tm, d), lambda i: (i, 0)),
                  pl.BlockSpec((1, d), lambda i: (0, 0)),
                  pl.BlockSpec((1, 8, d), mod_map),
                  pl.BlockSpec((N_EXPERTS, d), lambda i: (0, 0)),
                  pl.BlockSpec((N_EXPERTS, d), lambda i: (0, 0)),
                  pl.BlockSpec((N_EXPERTS, 1), lambda i: (0, 0)),
                  pl.BlockSpec((tm, tm), lambda i: (0, 0))],
        out_specs=[meta, meta, meta, pl.BlockSpec((N_EXPERTS, 1), lambda i: (0, 0))],
        out_shape=[jax.ShapeDtypeStruct((nt, TOP_K, tm), I32),
                   jax.ShapeDtypeStruct((nt, TOP_K, tm), F32),
                   jax.ShapeDtypeStruct((nt, TOP_K, tm), I32),
                   jax.ShapeDtypeStruct((N_EXPERTS, 1), F32)],
        compiler_params=_params(("arbitrary",)),
        name="router",
    )(xflat, g, modall, whi, wlo, bias, tri)


def _row_copy(src, dst, sem):
    return pltpu.make_async_copy(src, dst, sem)


def _dispatch_kernel(eidx_ref, pos_ref, pstart_ref, x_ref, g_ref, mod_ref, xs_in, xs_ref, hp_scr, sem):
    del xs_in
    hf = _norm_mod(x_ref[...], g_ref[...], mod_ref[0, 3:4, :], mod_ref[0, 4:5, :])
    tm, d = hf.shape
    hb = hf.astype(BF16).astype(F32)
    lo = lax.shift_right_logical(lax.bitcast_convert_type(hb[:, :d // 2], U32), jnp.uint32(16))
    hi = lax.bitcast_convert_type(hb[:, d // 2:], U32) & jnp.uint32(0xFFFF0000)
    hp_scr[...] = lo | hi

    def start(t, carry):
        for k in range(TOP_K):
            dst = pstart_ref[eidx_ref[0, k, t]] + pos_ref[0, k, t]
            _row_copy(hp_scr.at[pl.ds(t, 1)], xs_ref.at[pl.ds(dst, 1)], sem).start()
        return carry

    lax.fori_loop(0, tm, start, 0)

    def wait(t, carry):
        for k in range(TOP_K):
            _row_copy(hp_scr.at[pl.ds(0, 1)], xs_ref.at[pl.ds(0, 1)], sem).wait()
        return carry

    lax.fori_loop(0, tm, wait, 0)


def _dispatch(eidx, pos, pstart, xflat, g, modall, xs0, tiles_per_batch, nlat):
    t, d = xflat.shape
    tm = TOK_TILE
    nt = t // tm
    meta = pl.BlockSpec((1, TOP_K, tm), lambda i: (i, 0, 0), memory_space=pltpu.SMEM)

    def mod_map(i):
        return (2 * (i // tiles_per_batch) + ((i % tiles_per_batch) >= nlat).astype(I32), 0, 0)

    return pl.pallas_call(
        _dispatch_kernel,
        grid=(nt,),
        in_specs=[meta, meta, pl.BlockSpec(memory_space=pltpu.SMEM),
                  pl.BlockSpec((tm, d), lambda i: (i, 0)),
                  pl.BlockSpec((1, d), lambda i: (0, 0)),
                  pl.BlockSpec((1, 8, d), mod_map),
                  pl.BlockSpec(memory_space=pl.ANY)],
        out_specs=pl.BlockSpec(memory_space=pl.ANY),
        out_shape=jax.ShapeDtypeStruct(xs0.shape, xs0.dtype),
        input_output_aliases={6: 0},
        scratch_shapes=[pltpu.VMEM((tm, d // 2), U32), pltpu.SemaphoreType.DMA(())],
        compiler_params=_params(("arbitrary",)),
        name="moe_dispatch",
    )(eidx, pos, pstart, xflat, g, modall, xs0)


def _expert_kernel(blk_e_ref, nact_ref, xs_ref, wg_ref, wu_ref, wd_ref, ys_ref, wg_s, wu_s, wd_s):
    b = pl.program_id(0)
    changed = jnp.logical_or(b == 0, blk_e_ref[b] != blk_e_ref[jnp.maximum(b - 1, 0)])

    @pl.when(changed)
    def _():
        wg_s[...] = wg_ref[0].astype(BF16)
        wu_s[...] = wu_ref[0].astype(BF16)
        wd_s[...] = wd_ref[0].astype(BF16)

    @pl.when(b < nact_ref[0])
    def _():
        xs = xs_ref[...]
        half = xs.shape[1]
        x_lo = lax.bitcast_convert_type(lax.shift_left(xs, jnp.uint32(16)), F32).astype(BF16)
        x_hi = lax.bitcast_convert_type(xs & jnp.uint32(0xFFFF0000), F32).astype(BF16)
        gate = _dot(x_lo, wg_s[0:half, :]) + _dot(x_hi, wg_s[half:, :])
        up = _dot(x_lo, wu_s[0:half, :]) + _dot(x_hi, wu_s[half:, :])
        mid = (gate * _sigmoid(gate)) * up
        ys_ref[...] = _dot(mid.astype(BF16), wd_s[...])

    @pl.when(b >= nact_ref[0])
    def _():
        ys_ref[...] = jnp.zeros_like(ys_ref)


def _experts(blk_e, nact, xs, w_gate, w_up, w_down):
    n_rows, half = xs.shape
    _, d, f = w_gate.shape
    n_blk = n_rows // MOE_BLOCK
    grid_spec = pltpu.PrefetchScalarGridSpec(
        num_scalar_prefetch=2,
        grid=(n_blk,),
        in_specs=[pl.BlockSpec((MOE_BLOCK, half), lambda i, be, na: (i, 0)),
                  pl.BlockSpec((1, d, f), lambda i, be, na: (be[i], 0, 0)),
                  pl.BlockSpec((1, d, f), lambda i, be, na: (be[i], 0, 0)),
                  pl.BlockSpec((1, f, d), lambda i, be, na: (be[i], 0, 0))],
        out_specs=pl.BlockSpec((MOE_BLOCK, d), lambda i, be, na: (i, 0)),
        scratch_shapes=[pltpu.VMEM((d, f), BF16), pltpu.VMEM((d, f), BF16), pltpu.VMEM((f, d), BF16)],
    )
    return pl.pallas_call(
        _expert_kernel,
        grid_spec=grid_spec,
        out_shape=jax.ShapeDtypeStruct((n_rows, d), F32),
        compiler_params=_params(("arbitrary",)),
        name="moe_experts",
    )(blk_e, nact, xs, w_gate, w_up, w_down)


def _combine_kernel(eidx_ref, pos_ref, pstart_ref, x_ref, g_ref, mod_ref, wt_ref, wsg_ref, wsu_ref, wsd_ref,
                    gfin_ref, ys_ref, o_ref, gbuf, sem, *, final_norm):
    tm = x_ref.shape[0]

    def start(t, carry):
        for k in range(TOP_K):
            src = pstart_ref[eidx_ref[0, k, t]] + pos_ref[0, k, t]
            _row_copy(ys_ref.at[pl.ds(src, 1)], gbuf.at[k, pl.ds(t, 1)], sem).start()
        return carry

    lax.fori_loop(0, tm, start, 0)

    x = x_ref[...]
    hb = _norm_mod(x, g_ref[...], mod_ref[0, 3:4, :], mod_ref[0, 4:5, :]).astype(BF16)
    gate = _dot(hb, wsg_ref[...])
    mid = (gate * _sigmoid(gate)) * _dot(hb, wsu_ref[...])
    shared = _dot(mid.astype(BF16), wsd_ref[...])

    def wait(t, carry):
        for k in range(TOP_K):
            _row_copy(ys_ref.at[pl.ds(0, 1)], gbuf.at[0, pl.ds(0, 1)], sem).wait()
        return carry

    lax.fori_loop(0, tm, wait, 0)

    y = gbuf[0] * wt_ref[0, :, 0:1]
    for k in range(1, TOP_K):
        y = y + gbuf[k] * wt_ref[0, :, k:k + 1]
    out = x + mod_ref[0, 5:6, :] * (y + shared)
    if final_norm:
        out = _rms(out, gfin_ref[...])
    o_ref[...] = out


def _combine(eidx, pos, pstart, xflat, g, modall, wt, wsg, wsu, wsd, gfin, ys, tiles_per_batch, nlat, final_norm):
    t, d = xflat.shape
    f = wsg.shape[1]
    tm = TOK_TILE
    nt = t // tm
    meta = pl.BlockSpec((1, TOP_K, tm), lambda i: (i, 0, 0), memory_space=pltpu.SMEM)

    def mod_map(i):
        return (2 * (i // tiles_per_batch) + ((i % tiles_per_batch) >= nlat).astype(I32), 0, 0)

    return pl.pallas_call(
        functools.partial(_combine_kernel, final_norm=final_norm),
        grid=(nt,),
        in_specs=[meta, meta, pl.BlockSpec(memory_space=pltpu.SMEM),
                  pl.BlockSpec((tm, d), lambda i: (i, 0)),
                  pl.BlockSpec((1, d), lambda i: (0, 0)),
                  pl.BlockSpec((1, 8, d), mod_map),
                  pl.BlockSpec((1, tm, TOP_K), lambda i: (i, 0, 0)),
                  pl.BlockSpec((d, f), lambda i: (0, 0)),
                  pl.BlockSpec((d, f), lambda i: (0, 0)),
                  pl.BlockSpec((f, d), lambda i: (0, 0)),
                  pl.BlockSpec((1, d), lambda i: (0, 0)),
                  pl.BlockSpec(memory_space=pl.ANY)],
        out_specs=pl.BlockSpec((tm, d), lambda i: (i, 0)),
        out_shape=jax.ShapeDtypeStruct((t, d), F32),
        scratch_shapes=[pltpu.VMEM((TOP_K, tm, d), F32), pltpu.SemaphoreType.DMA(())],
        compiler_params=_params(("arbitrary",)),
        name="moe_combine",
    )(eidx, pos, pstart, xflat, g, modall, wt, wsg, wsu, wsd, gfin, ys)


def _rope_tables(s_lat, c_len):
    n_freq = HEAD_DIM // 4
    inv = 1.0 / (ROPE_BASE ** (jnp.arange(n_freq, dtype=F32) / n_freq))
    t = jnp.arange(s_lat)
    pos = jnp.stack([(t // GRID_W).astype(F32), (t % GRID_W).astype(F32)], axis=1)
    lane = np.arange(LANES)
    ang = pos[:, (lane % HEAD_DIM) // 32] * inv[lane % 16][None, :]
    sign = jnp.asarray(np.where(lane % 32 < 16, -1.0, 1.0), F32)
    cos = jnp.concatenate([jnp.cos(ang), jnp.ones((c_len, LANES), F32)], axis=0)
    sin = jnp.concatenate([jnp.sin(ang) * sign, jnp.zeros((c_len, LANES), F32)], axis=0)
    return cos, sin


def _moe(xflat, lp, modall, gfin, tiles_per_batch, nlat, final_norm):
    t, d = xflat.shape
    nt = t // TOK_TILE
    g = lp["g_ffn"].reshape(1, d)
    w_r = lp["w_router"].T
    whi = w_r.astype(BF16)
    wlo = (w_r - whi.astype(F32)).astype(BF16)
    tri = jnp.asarray(np.triu(np.ones((TOK_TILE, TOK_TILE), np.float32), 1), BF16)
    eidx, w, pos, cnt = _router(xflat, g, modall, whi, wlo, lp["router_bias"].reshape(N_EXPERTS, 1), tri,
                                tiles_per_batch, nlat)
    counts = cnt[:, 0].astype(I32)
    padded = (counts + MOE_BLOCK - 1) // MOE_BLOCK * MOE_BLOCK
    pend = jnp.cumsum(padded)
    pstart = (pend - padded).astype(I32)
    n_blk = -(-(t * TOP_K) // MOE_BLOCK) + N_EXPERTS
    blk_e = jnp.minimum(jnp.searchsorted(pend, jnp.arange(n_blk, dtype=I32) * MOE_BLOCK, side="right"),
                        N_EXPERTS - 1).astype(I32)
    nact = (pend[-1] // MOE_BLOCK).astype(I32).reshape(1)
    xs0 = jnp.zeros((n_blk * MOE_BLOCK, d // 2), U32)
    xs = _dispatch(eidx, pos, pstart, xflat, g, modall, xs0, tiles_per_batch, nlat)
    ys = _experts(blk_e, nact, xs, lp["w_gate"], lp["w_up"], lp["w_down"])
    wt = jnp.transpose(w, (0, 2, 1))
    return _combine(eidx, pos, pstart, xflat, g, modall, wt, lp["ws_gate"].astype(BF16), lp["ws_up"].astype(BF16),
                    lp["ws_down"].astype(BF16), gfin, ys, tiles_per_batch, nlat, final_norm)


def _layer(xc, cs, cos, sin, lp, layer_idx, last, s_lat, gfin):
    b, sall, d = xc.shape
    c_len = sall - s_lat
    lam_init = 0.8 - 0.6 * math.exp(-0.3 * layer_idx)
    mod = _modulation(cs, lp["w_mod"], lp["b_mod"])
    mod_lat = mod[:b].reshape(b, 1, 6, d)
    mod_ctx = jnp.broadcast_to(mod[b].reshape(1, 1, 6, d), (b, 1, 6, d))
    modall = jnp.concatenate([mod_lat, mod_ctx], axis=1)
    modall = jnp.pad(modall, ((0, 0), (0, 0), (0, 2), (0, 0))).reshape(2 * b, 8, d)

    qa, qb, qd, ka, va, kb, vb, kd, vd = _projection(
        xc, lp["g_mix"].reshape(1, d), modall, cos, sin, lp["w_in"].astype(BF16), s_lat)
    sink = lp["attn_sink"].astype(F32)
    lamv = jnp.zeros((8, LANES), F32).at[0:4, 0:HEAD_DIM].set(
        jnp.stack([lp["lam_q1"], lp["lam_k1"], lp["lam_q2"], lp["lam_k2"]]).astype(F32))
    subg = lp["subln_g"].reshape(1, LANES).astype(F32)
    oa = _swa(sink, qa, ka, va, s_lat)
    ob = _na(qb, kb, vb, _na_bias_mask(lp["na_rpb"], s_lat // GRID_W), s_lat)
    od = _diff(lamv, subg, qd, kd, vd, s_lat, lam_init)
    ctx_out = None
    if not last:
        ctx_out = _ctx_attention(sink, lamv, subg, qa, ka, va, qb, kb, vb, qd, kd, vd, s_lat, lam_init)
    n_rows = s_lat if last else sall
    x2 = _out_projection(xc, (oa, ob, od), ctx_out, lp["w_out"].astype(BF16), modall, s_lat)
    tiles_per_batch = n_rows // TOK_TILE
    y = _moe(x2.reshape(b * n_rows, d), lp, modall, gfin, tiles_per_batch, s_lat // TOK_TILE, last)
    return y.reshape(b, n_rows, d)


_LAYER_KEYS = ("w_mod", "b_mod", "g_mix", "g_ffn", "w_in", "w_out", "attn_sink", "na_rpb", "lam_q1", "lam_k1",
               "lam_q2", "lam_k2", "subln_g", "w_router", "router_bias", "w_gate", "w_up", "w_down",
               "ws_gate", "ws_up", "ws_down")


def kernel(x, c, ctx, c_ctx, w_mod, b_mod, g_mix, g_ffn, w_in, w_out, attn_sink, na_rpb, lam_q1, lam_k1, lam_q2,
           lam_k2, subln_g, w_router, router_bias, w_gate, w_up, w_down, ws_gate, ws_up, ws_down, g_final):
    stacked = dict(zip(_LAYER_KEYS, (w_mod, b_mod, g_mix, g_ffn, w_in, w_out, attn_sink, na_rpb, lam_q1, lam_k1,
                                     lam_q2, lam_k2, subln_g, w_router, router_bias, w_gate, w_up, w_down,
                                     ws_gate, ws_up, ws_down)))
    b, s_lat, d = x.shape
    c_len = ctx.shape[1]
    depth = w_mod.shape[0]
    assert s_lat % (NA_QROWS * GRID_W) == 0 and s_lat % c_len == 0 and c_len == TOK_TILE
    cs = jnp.zeros((16, d), F32).at[:b].set(c).at[b].set(c_ctx)
    cos, sin = _rope_tables(s_lat, c_len)
    xc = jnp.concatenate([x, ctx], axis=1)
    gfin = g_final.reshape(1, d)
    for i in range(depth):
        lp = {k: v[i] for k, v in stacked.items()}
        xc = _layer(xc, cs, cos, sin, lp, i, i == depth - 1, s_lat, gfin)
    return xc
```

```python
import functools
import math

import numpy as np
import jax
import jax.numpy as jnp
from jax import lax
from jax.experimental import pallas as pl
from jax.experimental.pallas import tpu as pltpu

F32 = jnp.float32
BF16 = jnp.bfloat16
I32 = jnp.int32
U32 = jnp.uint32

GRID_W = 64
HEAD_DIM = 64
ROPE_BASE = 10000.0
SWA_WINDOW = 128
NA_KH = 8
NA_KW = 16
N_HEADS = 4
N_EXPERTS = 256
TOP_K = 8
N_GROUPS = 8
TOPK_GROUPS = 4
ROUTED_SCALE = 2.5
MOE_BLOCK = 256
EPS = 1e-6
NEG_INF = -1e30
Q_COLS = 1024
IN_COLS = 2816
MIX_WIDTH = 1024

LANES = 128
TOK_TILE = 256
ATT_TQ = 256
NA_QROWS = 4
NA_KROWS = NA_QROWS + NA_KH - 1
DIFF_KCHUNK = 512
VMEM_LIMIT = 48 * 1024 * 1024


def _params(sem):
    return pltpu.CompilerParams(dimension_semantics=sem, vmem_limit_bytes=VMEM_LIMIT)


def _dot(a, b):
    return jnp.dot(a, b, preferred_element_type=F32)


def _dot_nt(a, b):
    return lax.dot_general(a, b, (((1,), (1,)), ((), ())), preferred_element_type=F32)


def _split_bf16(a):
    hi = a.astype(BF16)
    lo = (a - hi.astype(F32)).astype(BF16)
    return hi, lo


def _sigmoid(x):
    return 1.0 / (1.0 + jnp.exp(-x))


def _rms(x, g):
    return x * lax.rsqrt(jnp.mean(x * x, axis=-1, keepdims=True) + EPS) * g


def _norm_mod(x, g, shift, scale):
    return _rms(x, g) * (1.0 + scale) + shift


def _mod_kernel(c_ref, w_ref, b_ref, o_ref):
    c = c_ref[...]
    a_hi, a_lo = _split_bf16(c * _sigmoid(c))
    w_hi, w_lo = _split_bf16(w_ref[...])
    o_ref[...] = _dot(a_hi, w_hi) + _dot(a_hi, w_lo) + _dot(a_lo, w_hi) + b_ref[...]


def _modulation(cs, w_mod, b_mod):
    n, d = cs.shape
    cols = w_mod.shape[1]
    tn = 1536
    return pl.pallas_call(
        _mod_kernel,
        grid=(cols // tn,),
        in_specs=[pl.BlockSpec((n, d), lambda j: (0, 0)),
                  pl.BlockSpec((d, tn), lambda j: (0, j)),
                  pl.BlockSpec((1, tn), lambda j: (0, j))],
        out_specs=pl.BlockSpec((n, tn), lambda j: (0, j)),
        out_shape=jax.ShapeDtypeStruct((n, cols), F32),
        compiler_params=_params(("arbitrary",)),
        name="modulation",
    )(cs, w_mod, b_mod.reshape(1, cols))


def _proj_kernel(x_ref, g_ref, mod_ref, cos_ref, sin_ref, w_ref,
                 qa_ref, qb_ref, qd_ref, ka_ref, va_ref, kb_ref, vb_ref, kd_ref, vd_ref, h_scr):
    h = _norm_mod(x_ref[0], g_ref[...], mod_ref[0, 0:1, :], mod_ref[0, 1:2, :])
    h_scr[...] = h.astype(BF16)
    cos = cos_ref[...]
    sin = sin_ref[...]
    tm = h.shape[0]
    lane = lax.broadcasted_iota(I32, (tm, LANES), 1)
    first16 = (lane & 16) == 0
    lo64 = lane < HEAD_DIM

    def rope(v):
        partner = jnp.where(first16, pltpu.roll(v, LANES - 16, 1), pltpu.roll(v, 16, 1))
        return v * cos + partner * sin

    def mm(c0):
        return _dot(h_scr[...], w_ref[:, c0:c0 + 2 * LANES])

    a = mm(0) * (HEAD_DIM ** -0.5)
    c0 = rope(a[:, :LANES])
    c1 = rope(a[:, LANES:])
    zero = jnp.zeros_like(c0)
    qa_ref[0, :, 0 * LANES:1 * LANES] = jnp.where(lo64, c0, zero).astype(BF16)
    qa_ref[0, :, 1 * LANES:2 * LANES] = jnp.where(lo64, pltpu.roll(c0, HEAD_DIM, 1), zero).astype(BF16)
    qa_ref[0, :, 2 * LANES:3 * LANES] = jnp.where(lo64, zero, pltpu.roll(c1, HEAD_DIM, 1)).astype(BF16)
    qa_ref[0, :, 3 * LANES:4 * LANES] = jnp.where(lo64, zero, c1).astype(BF16)
    qb_ref[0] = (mm(256) * (HEAD_DIM ** -0.5)).astype(BF16)
    for j in range(2):
        a = mm(512 + 256 * j) * (HEAD_DIM ** -0.5)
        qd_ref[0, :, 256 * j:256 * j + LANES] = rope(a[:, :LANES]).astype(BF16)
        qd_ref[0, :, 256 * j + LANES:256 * (j + 1)] = rope(a[:, LANES:]).astype(BF16)
    a = mm(1024)
    ka_ref[0] = rope(a[:, :LANES]).astype(BF16)
    va_ref[0] = a[:, LANES:].astype(BF16)
    kb_ref[0] = mm(1280).astype(BF16)
    vb_ref[0] = mm(1536).astype(BF16)
    for j in range(2):
        a = mm(1792 + 256 * j)
        kd_ref[0, :, 256 * j:256 * j + LANES] = rope(a[:, :LANES]).astype(BF16)
        kd_ref[0, :, 256 * j + LANES:256 * (j + 1)] = rope(a[:, LANES:]).astype(BF16)
        vd_ref[0, :, 256 * j:256 * (j + 1)] = mm(2304 + 256 * j).astype(BF16)


def _projection(xc, g, modall, cos, sin, w_in, s_lat):
    b, sall, d = xc.shape
    tm = TOK_TILE
    nlat = s_lat // tm
    row = lambda width: pl.BlockSpec((1, tm, width), lambda i, j: (i, j, 0))
    widths = (512, 256, 512, 128, 128, 256, 256, 512, 512)
    return pl.pallas_call(
        _proj_kernel,
        grid=(b, sall // tm),
        in_specs=[row(d),
                  pl.BlockSpec((1, d), lambda i, j: (0, 0)),
                  pl.BlockSpec((1, 8, d), lambda i, j: (2 * i + (j >= nlat).astype(I32), 0, 0)),
                  pl.BlockSpec((tm, LANES), lambda i, j: (j, 0)),
                  pl.BlockSpec((tm, LANES), lambda i, j: (j, 0)),
                  pl.BlockSpec((d, IN_COLS), lambda i, j: (0, 0))],
        out_specs=[row(w) for w in widths],
        out_shape=[jax.ShapeDtypeStruct((b, sall, w), BF16) for w in widths],
        scratch_shapes=[pltpu.VMEM((tm, d), BF16)],
        compiler_params=_params(("arbitrary", "arbitrary")),
        name="projection",
    )(xc, g, modall, cos, sin, w_in)


def _half_mask(q, half):
    lane = lax.broadcasted_iota(I32, q.shape, 1)
    keep = (lane < HEAD_DIM) if half == 0 else (lane >= HEAD_DIM)
    return jnp.where(keep, q, jnp.zeros_like(q))


def _merge_halves(lo_part, hi_part):
    lane = lax.broadcasted_iota(I32, lo_part.shape, 1)
    return jnp.where(lane < HEAD_DIM, lo_part, hi_part)


def _gqa_rows(q_ref, g):
    return jnp.concatenate([q_ref[0, :, (2 * g) * LANES:(2 * g + 1) * LANES],
                            q_ref[0, :, (2 * g + 1) * LANES:(2 * g + 2) * LANES]], axis=0)


def _gqa_sink(sink_ref, g, tq):
    rowi = lax.broadcasted_iota(I32, (2 * tq, 1), 0)
    return jnp.where(rowi < tq, sink_ref[2 * g], sink_ref[2 * g + 1])


def _gqa_store(o_ref, g, o, tq):
    top, bot = o[:tq], o[tq:]
    if g == 0:
        chunk = _merge_halves(top, pltpu.roll(bot, HEAD_DIM, 1))
    else:
        chunk = _merge_halves(pltpu.roll(top, HEAD_DIM, 1), bot)
    o_ref[0, :, g * LANES:(g + 1) * LANES] = chunk.astype(BF16)


def _lam(lamv_ref, lam_init):
    v = lamv_ref[...]
    a = jnp.sum(v[0:1] * v[1:2], axis=-1, keepdims=True)
    b = jnp.sum(v[2:3] * v[3:4], axis=-1, keepdims=True)
    return jnp.exp(a) - jnp.exp(b) + lam_init


def _diff_post(o, subg, lam_init):
    return _rms(o, subg) * (1.0 - lam_init)


def _swa_kernel(sink_ref, q_ref, k_ref, v_ref, o_ref, *, s_lat, c_len):
    tq = ATT_TQ
    tk = tq + 2 * SWA_WINDOW
    q0 = pl.program_id(1) * tq
    ks = pl.multiple_of(jnp.clip(q0 - SWA_WINDOW, 0, s_lat - tk), SWA_WINDOW)
    kwin = k_ref[0, pl.ds(ks, tk), :]
    vwin = v_ref[0, pl.ds(ks, tk), :]
    kc = k_ref[0, s_lat:s_lat + c_len, :]
    vc = v_ref[0, s_lat:s_lat + c_len, :]
    qpos = q0 + lax.broadcasted_iota(I32, (tq, tk), 0)
    kpos = ks + lax.broadcasted_iota(I32, (tq, tk), 1)
    valid = jnp.abs(qpos - kpos) <= SWA_WINDOW
    valid = jnp.concatenate([valid, valid], axis=0)
    for g in range(2):
        q2 = _gqa_rows(q_ref, g)
        s_loc = jnp.where(valid, _dot_nt(q2, kwin), NEG_INF)
        s_ctx = _dot_nt(q2, kc)
        snk = _gqa_sink(sink_ref, g, tq)
        m = jnp.maximum(jnp.maximum(jnp.max(s_loc, axis=-1, keepdims=True),
                                    jnp.max(s_ctx, axis=-1, keepdims=True)), snk)
        e_loc = jnp.exp(s_loc - m)
        e_ctx = jnp.exp(s_ctx - m)
        den = (jnp.sum(e_loc, axis=-1, keepdims=True) + jnp.sum(e_ctx, axis=-1, keepdims=True)
               + jnp.exp(snk - m))
        o = (_dot(e_loc.astype(BF16), vwin) + _dot(e_ctx.astype(BF16), vc)) / den
        _gqa_store(o_ref, g, o, tq)


def _swa(sink, qa, ka, va, s_lat):
    b, sall, _ = qa.shape
    kv = pl.BlockSpec((1, sall, LANES), lambda i, j: (i, 0, 0))
    return pl.pallas_call(
        functools.partial(_swa_kernel, s_lat=s_lat, c_len=sall - s_lat),
        grid=(b, s_lat // ATT_TQ),
        in_specs=[pl.BlockSpec(memory_space=pltpu.SMEM),
                  pl.BlockSpec((1, ATT_TQ, 512), lambda i, j: (i, j, 0)), kv, kv],
        out_specs=pl.BlockSpec((1, ATT_TQ, 256), lambda i, j: (i, j, 0)),
        out_shape=jax.ShapeDtypeStruct((b, s_lat, 256), BF16),
        compiler_params=_params(("arbitrary", "arbitrary")),
        name="swa",
    )(sink, qa, ka, va)


def _na_kernel(q_ref, k_ref, v_ref, bm_ref, o_ref, *, s_lat, c_len):
    rows = s_lat // GRID_W
    nk = NA_KROWS * GRID_W
    r0 = pl.program_id(1) * NA_QROWS
    ks = pl.multiple_of(jnp.clip(r0 - NA_KH // 2, 0, rows - NA_KROWS) * GRID_W, GRID_W)
    for c in range(2):
        sl = slice(c * LANES, (c + 1) * LANES)
        kwin = k_ref[0, pl.ds(ks, nk), sl]
        vwin = v_ref[0, pl.ds(ks, nk), sl]
        kc = k_ref[0, s_lat:s_lat + c_len, sl]
        vc = v_ref[0, s_lat:s_lat + c_len, sl]
        q = q_ref[0, :, sl]
        outs = []
        for half in range(2):
            qm = _half_mask(q, half)
            s_loc = _dot_nt(qm, kwin) + bm_ref[0, 2 * c + half]
            s_ctx = _dot_nt(qm, kc)
            m = jnp.maximum(jnp.max(s_loc, axis=-1, keepdims=True), jnp.max(s_ctx, axis=-1, keepdims=True))
            e_loc = jnp.exp(s_loc - m)
            e_ctx = jnp.exp(s_ctx - m)
            den = jnp.sum(e_loc, axis=-1, keepdims=True) + jnp.sum(e_ctx, axis=-1, keepdims=True)
            outs.append((_dot(e_loc.astype(BF16), vwin) + _dot(e_ctx.astype(BF16), vc)) / den)
        o_ref[0, :, sl] = _merge_halves(outs[0], outs[1]).astype(BF16)


def _na_bias_mask(rpb, rows):
    nq = NA_QROWS * GRID_W
    nk = NA_KROWS * GRID_W
    col = np.arange(GRID_W)
    cstart = np.clip(col - NA_KW // 2, 0, GRID_W - NA_KW)
    col_ok = (col[None, :] >= cstart[:, None]) & (col[None, :] < cstart[:, None] + NA_KW)
    dc = np.clip(col[None, :] - col[:, None], -(NA_KW - 1), NA_KW - 1) + NA_KW - 1
    sel_c = np.where(col_ok[:, :, None], np.eye(2 * NA_KW - 1)[dc], 0.0)
    tabs = []
    for r0 in (0, NA_QROWS, rows - NA_QROWS):
        ksr = int(np.clip(r0 - NA_KH // 2, 0, rows - NA_KROWS))
        r = r0 + np.arange(NA_QROWS)
        kr = ksr + np.arange(NA_KROWS)
        rs = np.clip(r - NA_KH // 2, 0, rows - NA_KH)
        row_ok = (kr[None, :] >= rs[:, None]) & (kr[None, :] < rs[:, None] + NA_KH)
        dr = np.clip(kr[None, :] - r[:, None], -(NA_KH - 1), NA_KH - 1) + NA_KH - 1
        sel_r = np.where(row_ok[:, :, None], np.eye(2 * NA_KH - 1)[dr], 0.0)
        ok = (row_ok[:, None, :, None] & col_ok[None, :, None, :]).reshape(nq, nk)
        bias = jnp.einsum("hdc,rjd,qkc->hrqjk", rpb.astype(F32), jnp.asarray(sel_r, F32), jnp.asarray(sel_c, F32),
                          precision=lax.Precision.HIGHEST).reshape(N_HEADS, nq, nk)
        tabs.append(jnp.where(ok[None], bias, NEG_INF))
    return jnp.stack(tabs)


def _na(qb, kb, vb, bm, s_lat):
    b, sall, _ = qb.shape
    nq = NA_QROWS * GRID_W
    nsteps = s_lat // nq
    kv = pl.BlockSpec((1, sall, 256), lambda i, j: (i, 0, 0))

    def bm_map(i, j):
        return (jnp.where(j == 0, 0, jnp.where(j == nsteps - 1, 2, 1)), 0, 0, 0)

    return pl.pallas_call(
        functools.partial(_na_kernel, s_lat=s_lat, c_len=sall - s_lat),
        grid=(b, nsteps),
        in_specs=[pl.BlockSpec((1, nq, 256), lambda i, j: (i, j, 0)), kv, kv,
                  pl.BlockSpec((1, N_HEADS, nq, NA_KROWS * GRID_W), bm_map)],
        out_specs=pl.BlockSpec((1, nq, 256), lambda i, j: (i, j, 0)),
        out_shape=jax.ShapeDtypeStruct((b, s_lat, 256), BF16),
        compiler_params=_params(("arbitrary", "arbitrary")),
        name="na2d",
    )(qb, kb, vb, bm)


def _diff_kernel(lamv_ref, subg_ref, q_ref, k_ref, v_ref, o_ref, s_scr, *, n_keys, lam_init):
    tq = ATT_TQ
    q = q_ref[0]
    q12 = jnp.concatenate([_half_mask(q, 0), _half_mask(q, 1)], axis=0)
    chunks = [(c0, min(DIFF_KCHUNK, n_keys - c0)) for c0 in range(0, n_keys, DIFF_KCHUNK)]
    m = jnp.full((2 * tq, 1), NEG_INF, F32)
    for c0, n in chunks:
        s = _dot_nt(q12, k_ref[0, c0:c0 + n, :])
        s_scr[:, c0:c0 + n] = s
        m = jnp.maximum(m, jnp.max(s, axis=-1, keepdims=True))
    den = jnp.zeros((2 * tq, 1), F32)
    for c0, n in chunks:
        e = jnp.exp(s_scr[:, c0:c0 + n] - m)
        s_scr[:, c0:c0 + n] = e
        den = den + jnp.sum(e, axis=-1, keepdims=True)
    r1 = 1.0 / den[:tq]
    r2 = _lam(lamv_ref, lam_init) / den[tq:]
    acc = jnp.zeros((tq, LANES), F32)
    for c0, n in chunks:
        p = s_scr[:tq, c0:c0 + n] * r1 - s_scr[tq:, c0:c0 + n] * r2
        acc = acc + _dot(p.astype(BF16), v_ref[0, c0:c0 + n, :])
    o_ref[0] = _diff_post(acc, subg_ref[...], lam_init).astype(BF16)


def _diff(lamv, subg, qd, kd, vd, s_lat, lam_init):
    b, sall, _ = qd.shape
    kv = pl.BlockSpec((1, sall, LANES), lambda i, h, j: (i, 0, h))
    return pl.pallas_call(
        functools.partial(_diff_kernel, n_keys=sall, lam_init=lam_init),
        grid=(b, N_HEADS, s_lat // ATT_TQ),
        in_specs=[pl.BlockSpec((8, LANES), lambda i, h, j: (0, 0)),
                  pl.BlockSpec((1, LANES), lambda i, h, j: (0, 0)),
                  pl.BlockSpec((1, ATT_TQ, LANES), lambda i, h, j: (i, j, h)), kv, kv],
        out_specs=pl.BlockSpec((1, ATT_TQ, LANES), lambda i, h, j: (i, j, h)),
        out_shape=jax.ShapeDtypeStruct((b, s_lat, 512), BF16),
        scratch_shapes=[pltpu.VMEM((2 * ATT_TQ, sall), F32)],
        compiler_params=_params(("arbitrary", "arbitrary", "arbitrary")),
        name="diff_attn",
    )(lamv, subg, qd, kd, vd)


def _ctx_kernel(sink_ref, lamv_ref, subg_ref, qa_ref, ka_ref, va_ref, qb_ref, kb_ref, vb_ref,
                qd_ref, kd_ref, vd_ref, oa_ref, ob_ref, od_ref, *, lam_init):
    cl = qa_ref.shape[1]
    ka = ka_ref[0]
    va = va_ref[0]
    for g in range(2):
        q2 = _gqa_rows(qa_ref, g)
        s = _dot_nt(q2, ka)
        snk = _gqa_sink(sink_ref, g, cl)
        m = jnp.maximum(jnp.max(s, axis=-1, keepdims=True), snk)
        e = jnp.exp(s - m)
        den = jnp.sum(e, axis=-1, keepdims=True) + jnp.exp(snk - m)
        _gqa_store(oa_ref, g, _dot(e.astype(BF16), va) / den, cl)
    for c in range(2):
        sl = slice(c * LANES, (c + 1) * LANES)
        outs = []
        for half in range(2):
            s = _dot_nt(_half_mask(qb_ref[0, :, sl], half), kb_ref[0, :, sl])
            e = jnp.exp(s - jnp.max(s, axis=-1, keepdims=True))
            outs.append(_dot(e.astype(BF16), vb_ref[0, :, sl]) / jnp.sum(e, axis=-1, keepdims=True))
        ob_ref[0, :, sl] = _merge_halves(outs[0], outs[1]).astype(BF16)
    lam = _lam(lamv_ref, lam_init)
    for h in range(N_HEADS):
        sl = slice(h * LANES, (h + 1) * LANES)
        q = qd_ref[0, :, sl]
        q12 = jnp.concatenate([_half_mask(q, 0), _half_mask(q, 1)], axis=0)
        s = _dot_nt(q12, kd_ref[0, :, sl])
        e = jnp.exp(s - jnp.max(s, axis=-1, keepdims=True))
        den = jnp.sum(e, axis=-1, keepdims=True)
        p = e[:cl] * (1.0 / den[:cl]) - e[cl:] * (lam / den[cl:])
        o = _dot(p.astype(BF16), vd_ref[0, :, sl])
        od_ref[0, :, sl] = _diff_post(o, subg_ref[...], lam_init).astype(BF16)


def _ctx_attention(sink, lamv, subg, qa, ka, va, qb, kb, vb, qd, kd, vd, s_lat, lam_init):
    b, sall, _ = qa.shape
    cl = sall - s_lat
    blk = s_lat // cl
    row = lambda width: pl.BlockSpec((1, cl, width), lambda i: (i, blk, 0))
    out = lambda width: pl.BlockSpec((1, cl, width), lambda i: (i, 0, 0))
    return pl.pallas_call(
        functools.partial(_ctx_kernel, lam_init=lam_init),
        grid=(b,),
        in_specs=[pl.BlockSpec(memory_space=pltpu.SMEM),
                  pl.BlockSpec((8, LANES), lambda i: (0, 0)),
                  pl.BlockSpec((1, LANES), lambda i: (0, 0)),
                  row(512), row(128), row(128), row(256), row(256), row(256), row(512), row(512), row(512)],
        out_specs=[out(256), out(256), out(512)],
        out_shape=[jax.ShapeDtypeStruct((b, cl, w), BF16) for w in (256, 256, 512)],
        compiler_params=_params(("arbitrary",)),
        name="ctx_attention",
    )(sink, lamv, subg, qa, ka, va, qb, kb, vb, qd, kd, vd)


def _out_kernel(x_ref, w_ref, mod_ref, *refs, nlat):
    xo_ref = refs[-1]

    def emit(oa_ref, ob_ref, od_ref):
        attn = (_dot(oa_ref[0], w_ref[0:256, :]) + _dot(ob_ref[0], w_ref[256:512, :])
                + _dot(od_ref[0], w_ref[512:1024, :]))
        xo_ref[0] = x_ref[0] + mod_ref[0, 2:3, :] * attn

    if len(refs) == 4:
        emit(*refs[:3])
    else:
        is_ctx = pl.program_id(1) >= nlat
        pl.when(jnp.logical_not(is_ctx))(lambda: emit(*refs[:3]))
        pl.when(is_ctx)(lambda: emit(*refs[3:6]))


def _out_projection(xc, lat, ctx, w_out, modall, s_lat):
    b, sall, d = xc.shape
    tm = TOK_TILE
    nlat = s_lat // tm
    n_rows = s_lat if ctx is None else sall
    row = lambda width: pl.BlockSpec((1, tm, width), lambda i, j: (i, j, 0))
    lat_row = lambda width: pl.BlockSpec((1, tm, width), lambda i, j: (i, jnp.minimum(j, nlat - 1), 0))
    ctx_row = lambda width: pl.BlockSpec((1, tm, width), lambda i, j: (i, 0, 0))
    widths = (256, 256, 512)
    specs = [lat_row(w) for w in widths] + ([] if ctx is None else [ctx_row(w) for w in widths])
    return pl.pallas_call(
        functools.partial(_out_kernel, nlat=nlat),
        grid=(b, n_rows // tm),
        in_specs=[row(d),
                  pl.BlockSpec((MIX_WIDTH, d), lambda i, j: (0, 0)),
                  pl.BlockSpec((1, 8, d), lambda i, j: (2 * i + (j >= nlat).astype(I32), 0, 0))] + specs,
        out_specs=row(d),
        out_shape=jax.ShapeDtypeStruct((b, n_rows, d), F32),
        compiler_params=_params(("arbitrary", "arbitrary")),
        name="out_projection",
    )(xc, w_out, modall, *lat, *(() if ctx is None else ctx))


def _first_argmax(v, iota, n):
    m = jnp.max(v, axis=0, keepdims=True)
    ix = jnp.min(jnp.where(v == m, iota, float(n)), axis=0, keepdims=True)
    return m, ix


def _router_kernel(x_ref, g_ref, mod_ref, whi_ref, wlo_ref, bias_ref, tri_ref,
                   eidx_ref, w_ref, pos_ref, cnt_ref):
    @pl.when(pl.program_id(0) == 0)
    def _():
        cnt_ref[...] = jnp.zeros_like(cnt_ref)

    hf = _norm_mod(x_ref[...], g_ref[...], mod_ref[0, 3:4, :], mod_ref[0, 4:5, :])
    tm = hf.shape[0]
    h_hi, h_lo = _split_bf16(hf)
    whi = whi_ref[...]
    logits = _dot_nt(whi, h_hi) + _dot_nt(whi, h_lo) + _dot_nt(wlo_ref[...], h_hi)
    scores = _sigmoid(logits)
    biased = scores + bias_ref[...]
    gsz = N_EXPERTS // N_GROUPS
    iota_g = lax.broadcasted_iota(I32, (gsz, tm), 0).astype(F32)
    gscore = []
    for g in range(N_GROUPS):
        v = biased[g * gsz:(g + 1) * gsz]
        m1, i1 = _first_argmax(v, iota_g, gsz)
        m2 = jnp.max(jnp.where(iota_g == i1, -jnp.inf, v), axis=0, keepdims=True)
        gscore.append(m1 + m2)
    cur = jnp.concatenate(gscore, axis=0)
    iota_n = lax.broadcasted_iota(I32, (N_GROUPS, tm), 0).astype(F32)
    gsel = jnp.zeros((N_GROUPS, tm), F32)
    for _ in range(TOPK_GROUPS):
        _, ix = _first_argmax(cur, iota_n, N_GROUPS)
        hit = iota_n == ix
        gsel = jnp.where(hit, 1.0, gsel)
        cur = jnp.where(hit, -jnp.inf, cur)
    masked = jnp.concatenate(
        [jnp.where(gsel[g:g + 1] > 0.5, biased[g * gsz:(g + 1) * gsz], -jnp.inf) for g in range(N_GROUPS)],
        axis=0)
    iota_e = lax.broadcasted_iota(I32, (N_EXPERTS, tm), 0).astype(F32)
    onehot = jnp.zeros((N_EXPERTS, tm), F32)
    idxs, ws = [], []
    for _ in range(TOP_K):
        _, ix = _first_argmax(masked, iota_e, N_EXPERTS)
        hit = iota_e == ix
        ws.append(jnp.sum(jnp.where(hit, scores, 0.0), axis=0, keepdims=True))
        masked = jnp.where(hit, -jnp.inf, masked)
        onehot = jnp.where(hit, 1.0, onehot)
        idxs.append(ix)
    cnt = cnt_ref[...]
    rank = _dot(onehot.astype(BF16), tri_ref[...]) + cnt
    wsum = ws[0]
    for wk in ws[1:]:
        wsum = wsum + wk
    for k in range(TOP_K):
        pos = jnp.sum(jnp.where(iota_e == idxs[k], rank, 0.0), axis=0, keepdims=True)
        eidx_ref[0, k:k + 1, :] = idxs[k].astype(I32)
        pos_ref[0, k:k + 1, :] = pos.astype(I32)
        w_ref[0, k:k + 1, :] = ws[k] / wsum * ROUTED_SCALE
    cnt_ref[...] = cnt + jnp.sum(onehot, axis=1, keepdims=True)


def _router(xflat, g, modall, whi, wlo, bias, tri, tiles_per_batch, nlat):
    t, d = xflat.shape
    tm = TOK_TILE
    nt = t // tm
    meta = pl.BlockSpec((1, TOP_K, tm), lambda i: (i, 0, 0))

    def mod_map(i):
        return (2 * (i // tiles_per_batch) + ((i % tiles_per_batch) >= nlat).astype(I32), 0, 0)

    return pl.pallas_call(
        _router_kernel,
        grid=(nt,),
        in_specs=[pl.BlockSpec((tm, d), lambda i: (i, 0)),
                  pl.BlockSpec((1, d), lambda i: (0, 0)),
                  pl.BlockSpec((1, 8, d), mod_map),
                  pl.BlockSpec((N_EXPERTS, d), lambda i: (0, 0)),
                  pl.BlockSpec((N_EXPERTS, d), lambda i: (0, 0)),
                  pl.BlockSpec((N_EXPERTS, 1), lambda i: (0, 0)),
                  pl.BlockSpec((tm, tm), lambda i: (0, 0))],
        out_specs=[meta, meta, meta, pl.BlockSpec((N_EXPERTS, 1), lambda i: (0, 0))],
        out_shape=[jax.ShapeDtypeStruct((nt, TOP_K, tm), I32),
                   jax.ShapeDtypeStruct((nt, TOP_K, tm), F32),
                   jax.ShapeDtypeStruct((nt, TOP_K, tm), I32),
                   jax.ShapeDtypeStruct((N_EXPERTS, 1), F32)],
        compiler_params=_params(("arbitrary",)),
        name="router",
    )(xflat, g, modall, whi, wlo, bias, tri)


def _row_copy(src, dst, sem):
    return pltpu.make_async_copy(src, dst, sem)


def _dispatch_kernel(eidx_ref, pos_ref, pstart_ref, x_ref, g_ref, mod_ref, xs_in, xs_ref, hp_scr, sem):
    del xs_in
    hf = _norm_mod(x_ref[...], g_ref[...], mod_ref[0, 3:4, :], mod_ref[0, 4:5, :])
    tm, d = hf.shape
    hb = hf.astype(BF16).astype(F32)
    lo = lax.shift_right_logical(lax.bitcast_convert_type(hb[:, :d // 2], U32), jnp.uint32(16))
    hi = lax.bitcast_convert_type(hb[:, d // 2:], U32) & jnp.uint32(0xFFFF0000)
    hp_scr[...] = lo | hi

    def start(t, carry):
        for k in range(TOP_K):
            dst = pstart_ref[eidx_ref[0, k, t]] + pos_ref[0, k, t]
            _row_copy(hp_scr.at[pl.ds(t, 1)], xs_ref.at[pl.ds(dst, 1)], sem).start()
        return carry

    lax.fori_loop(0, tm, start, 0)

    def wait(t, carry):
        for k in range(TOP_K):
            _row_copy(hp_scr.at[pl.ds(0, 1)], xs_ref.at[pl.ds(0, 1)], sem).wait()
        return carry

    lax.fori_loop(0, tm, wait, 0)


def _dispatch(eidx, pos, pstart, xflat, g, modall, xs0, tiles_per_batch, nlat):
    t, d = xflat.shape
    tm = TOK_TILE
    nt = t // tm
    meta = pl.BlockSpec((1, TOP_K, tm), lambda i: (i, 0, 0), memory_space=pltpu.SMEM)

    def mod_map(i):
        return (2 * (i // tiles_per_batch) + ((i % tiles_per_batch) >= nlat).astype(I32), 0, 0)

    return pl.pallas_call(
        _dispatch_kernel,
        grid=(nt,),
        in_specs=[meta, meta, pl.BlockSpec(memory_space=pltpu.SMEM),
                  pl.BlockSpec((tm, d), lambda i: (i, 0)),
                  pl.BlockSpec((1, d), lambda i: (0, 0)),
                  pl.BlockSpec((1, 8, d), mod_map),
                  pl.BlockSpec(memory_space=pl.ANY)],
        out_specs=pl.BlockSpec(memory_space=pl.ANY),
        out_shape=jax.ShapeDtypeStruct(xs0.shape, xs0.dtype),
        input_output_aliases={6: 0},
        scratch_shapes=[pltpu.VMEM((tm, d // 2), U32), pltpu.SemaphoreType.DMA(())],
        compiler_params=_params(("arbitrary",)),
        name="moe_dispatch",
    )(eidx, pos, pstart, xflat, g, modall, xs0)


def _expert_kernel(blk_e_ref, nact_ref, xs_ref, wg_ref, wu_ref, wd_ref, ys_ref, wg_s, wu_s, wd_s):
    b = pl.program_id(0)
    changed = jnp.logical_or(b == 0, blk_e_ref[b] != blk_e_ref[jnp.maximum(b - 1, 0)])

    @pl.when(changed)
    def _():
        wg_s[...] = wg_ref[0, 0].astype(BF16)
        wu_s[...] = wu_ref[0, 0].astype(BF16)
        wd_s[...] = wd_ref[0, 0].astype(BF16)

    @pl.when(b < nact_ref[0])
    def _():
        xs = xs_ref[...]
        half = xs.shape[1]
        x_lo = lax.bitcast_convert_type(lax.shift_left(xs, jnp.uint32(16)), F32).astype(BF16)
        x_hi = lax.bitcast_convert_type(xs & jnp.uint32(0xFFFF0000), F32).astype(BF16)
        gate = _dot(x_lo, wg_s[0:half, :]) + _dot(x_hi, wg_s[half:, :])
        up = _dot(x_lo, wu_s[0:half, :]) + _dot(x_hi, wu_s[half:, :])
        mid = (gate * _sigmoid(gate)) * up
        ys_ref[...] = _dot(mid.astype(BF16), wd_s[...])

    @pl.when(b >= nact_ref[0])
    def _():
        ys_ref[...] = jnp.zeros_like(ys_ref)


def _experts(blk_e, nact, xs, w_gate, w_up, w_down, layer):
    n_rows, half = xs.shape
    _, _, d, f = w_gate.shape
    n_blk = n_rows // MOE_BLOCK
    grid_spec = pltpu.PrefetchScalarGridSpec(
        num_scalar_prefetch=2,
        grid=(n_blk,),
        in_specs=[pl.BlockSpec((MOE_BLOCK, half), lambda i, be, na: (i, 0)),
                  pl.BlockSpec((1, 1, d, f), lambda i, be, na: (layer, be[i], 0, 0)),
                  pl.BlockSpec((1, 1, d, f), lambda i, be, na: (layer, be[i], 0, 0)),
                  pl.BlockSpec((1, 1, f, d), lambda i, be, na: (layer, be[i], 0, 0))],
        out_specs=pl.BlockSpec((MOE_BLOCK, d), lambda i, be, na: (i, 0)),
        scratch_shapes=[pltpu.VMEM((d, f), BF16), pltpu.VMEM((d, f), BF16), pltpu.VMEM((f, d), BF16)],
    )
    return pl.pallas_call(
        _expert_kernel,
        grid_spec=grid_spec,
        out_shape=jax.ShapeDtypeStruct((n_rows, d), F32),
        compiler_params=_params(("arbitrary",)),
        name="moe_experts",
    )(blk_e, nact, xs, w_gate, w_up, w_down)


def _combine_kernel(eidx_ref, pos_ref, pstart_ref, x_ref, g_ref, mod_ref, wt_ref, wsg_ref, wsu_ref, wsd_ref,
                    gfin_ref, ys_ref, o_ref, gbuf, sem, *, final_norm):
    tm = x_ref.shape[0]

    def start(t, carry):
        for k in range(TOP_K):
            src = pstart_ref[eidx_ref[0, k, t]] + pos_ref[0, k, t]
            _row_copy(ys_ref.at[pl.ds(src, 1)], gbuf.at[k, pl.ds(t, 1)], sem).start()
        return carry

    lax.fori_loop(0, tm, start, 0)

    x = x_ref[...]
    hb = _norm_mod(x, g_ref[...], mod_ref[0, 3:4, :], mod_ref[0, 4:5, :]).astype(BF16)
    gate = _dot(hb, wsg_ref[...])
    mid = (gate * _sigmoid(gate)) * _dot(hb, wsu_ref[...])
    shared = _dot(mid.astype(BF16), wsd_ref[...])

    def wait(t, carry):
        for k in range(TOP_K):
            _row_copy(ys_ref.at[pl.ds(0, 1)], gbuf.at[0, pl.ds(0, 1)], sem).wait()
        return carry

    lax.fori_loop(0, tm, wait, 0)

    y = gbuf[0] * wt_ref[0, :, 0:1]
    for k in range(1, TOP_K):
        y = y + gbuf[k] * wt_ref[0, :, k:k + 1]
    out = x + mod_ref[0, 5:6, :] * (y + shared)
    if final_norm:
        out = _rms(out, gfin_ref[...])
    o_ref[...] = out


def _combine(eidx, pos, pstart, xflat, g, modall, wt, wsg, wsu, wsd, gfin, ys, tiles_per_batch, nlat, final_norm):
    t, d = xflat.shape
    f = wsg.shape[1]
    tm = TOK_TILE
    nt = t // tm
    meta = pl.BlockSpec((1, TOP_K, tm), lambda i: (i, 0, 0), memory_space=pltpu.SMEM)

    def mod_map(i):
        return (2 * (i // tiles_per_batch) + ((i % tiles_per_batch) >= nlat).astype(I32), 0, 0)

    return pl.pallas_call(
        functools.partial(_combine_kernel, final_norm=final_norm),
        grid=(nt,),
        in_specs=[meta, meta, pl.BlockSpec(memory_space=pltpu.SMEM),
                  pl.BlockSpec((tm, d), lambda i: (i, 0)),
                  pl.BlockSpec((1, d), lambda i: (0, 0)),
                  pl.BlockSpec((1, 8, d), mod_map),
                  pl.BlockSpec((1, tm, TOP_K), lambda i: (i, 0, 0)),
                  pl.BlockSpec((d, f), lambda i: (0, 0)),
                  pl.BlockSpec((d, f), lambda i: (0, 0)),
                  pl.BlockSpec((f, d), lambda i: (0, 0)),
                  pl.BlockSpec((1, d), lambda i: (0, 0)),
                  pl.BlockSpec(memory_space=pl.ANY)],
        out_specs=pl.BlockSpec((tm, d), lambda i: (i, 0)),
        out_shape=jax.ShapeDtypeStruct((t, d), F32),
        scratch_shapes=[pltpu.VMEM((TOP_K, tm, d), F32), pltpu.SemaphoreType.DMA(())],
        compiler_params=_params(("arbitrary",)),
        name="moe_combine",
    )(eidx, pos, pstart, xflat, g, modall, wt, wsg, wsu, wsd, gfin, ys)


def _rope_tables(s_lat, c_len):
    n_freq = HEAD_DIM // 4
    inv = 1.0 / (ROPE_BASE ** (jnp.arange(n_freq, dtype=F32) / n_freq))
    t = jnp.arange(s_lat)
    pos = jnp.stack([(t // GRID_W).astype(F32), (t % GRID_W).astype(F32)], axis=1)
    lane = np.arange(LANES)
    ang = pos[:, (lane % HEAD_DIM) // 32] * inv[lane % 16][None, :]
    sign = jnp.asarray(np.where(lane % 32 < 16, -1.0, 1.0), F32)
    cos = jnp.concatenate([jnp.cos(ang), jnp.ones((c_len, LANES), F32)], axis=0)
    sin = jnp.concatenate([jnp.sin(ang) * sign, jnp.zeros((c_len, LANES), F32)], axis=0)
    return cos, sin


def _moe(xflat, lp, modall, gfin, tiles_per_batch, nlat, final_norm):
    t, d = xflat.shape
    nt = t // TOK_TILE
    g = lp["g_ffn"].reshape(1, d)
    w_r = lp["w_router"].T
    whi = w_r.astype(BF16)
    wlo = (w_r - whi.astype(F32)).astype(BF16)
    tri = jnp.asarray(np.triu(np.ones((TOK_TILE, TOK_TILE), np.float32), 1), BF16)
    eidx, w, pos, cnt = _router(xflat, g, modall, whi, wlo, lp["router_bias"].reshape(N_EXPERTS, 1), tri,
                                tiles_per_batch, nlat)
    counts = cnt[:, 0].astype(I32)
    padded = (counts + MOE_BLOCK - 1) // MOE_BLOCK * MOE_BLOCK
    pend = jnp.cumsum(padded)
    pstart = (pend - padded).astype(I32)
    n_blk = -(-(t * TOP_K) // MOE_BLOCK) + N_EXPERTS
    blk_row0 = jnp.arange(n_blk, dtype=I32) * MOE_BLOCK
    blk_e = jnp.minimum(jnp.sum((pend[None, :] <= blk_row0[:, None]).astype(I32), axis=1), N_EXPERTS - 1)
    nact = (pend[-1] // MOE_BLOCK).astype(I32).reshape(1)
    xs0 = jnp.zeros((n_blk * MOE_BLOCK, d // 2), U32)
    xs = _dispatch(eidx, pos, pstart, xflat, g, modall, xs0, tiles_per_batch, nlat)
    ys = _experts(blk_e, nact, xs, lp["w_gate"], lp["w_up"], lp["w_down"], lp["layer"])
    wt = jnp.transpose(w, (0, 2, 1))
    return _combine(eidx, pos, pstart, xflat, g, modall, wt, lp["ws_gate"].astype(BF16), lp["ws_up"].astype(BF16),
                    lp["ws_down"].astype(BF16), gfin, ys, tiles_per_batch, nlat, final_norm)


def _layer(xc, cs, cos, sin, lp, layer_idx, last, s_lat, gfin):
    b, sall, d = xc.shape
    c_len = sall - s_lat
    lam_init = 0.8 - 0.6 * math.exp(-0.3 * layer_idx)
    mod = _modulation(cs, lp["w_mod"], lp["b_mod"])
    mod_lat = mod[:b].reshape(b, 1, 6, d)
    mod_ctx = jnp.broadcast_to(mod[b].reshape(1, 1, 6, d), (b, 1, 6, d))
    modall = jnp.concatenate([mod_lat, mod_ctx], axis=1)
    modall = jnp.pad(modall, ((0, 0), (0, 0), (0, 2), (0, 0))).reshape(2 * b, 8, d)

    qa, qb, qd, ka, va, kb, vb, kd, vd = _projection(
        xc, lp["g_mix"].reshape(1, d), modall, cos, sin, lp["w_in"].astype(BF16), s_lat)
    sink = lp["attn_sink"].astype(F32)
    lamv = jnp.zeros((8, LANES), F32).at[0:4, 0:HEAD_DIM].set(
        jnp.stack([lp["lam_q1"], lp["lam_k1"], lp["lam_q2"], lp["lam_k2"]]).astype(F32))
    subg = lp["subln_g"].reshape(1, LANES).astype(F32)
    oa = _swa(sink, qa, ka, va, s_lat)
    ob = _na(qb, kb, vb, _na_bias_mask(lp["na_rpb"], s_lat // GRID_W), s_lat)
    od = _diff(lamv, subg, qd, kd, vd, s_lat, lam_init)
    ctx_out = None
    if not last:
        ctx_out = _ctx_attention(sink, lamv, subg, qa, ka, va, qb, kb, vb, qd, kd, vd, s_lat, lam_init)
    n_rows = s_lat if last else sall
    x2 = _out_projection(xc, (oa, ob, od), ctx_out, lp["w_out"].astype(BF16), modall, s_lat)
    tiles_per_batch = n_rows // TOK_TILE
    y = _moe(x2.reshape(b * n_rows, d), lp, modall, gfin, tiles_per_batch, s_lat // TOK_TILE, last)
    return y.reshape(b, n_rows, d)


_LAYER_KEYS = ("w_mod", "b_mod", "g_mix", "g_ffn", "w_in", "w_out", "attn_sink", "na_rpb", "lam_q1", "lam_k1",
               "lam_q2", "lam_k2", "subln_g", "w_router", "router_bias", "w_gate", "w_up", "w_down",
               "ws_gate", "ws_up", "ws_down")


def kernel(x, c, ctx, c_ctx, w_mod, b_mod, g_mix, g_ffn, w_in, w_out, attn_sink, na_rpb, lam_q1, lam_k1, lam_q2,
           lam_k2, subln_g, w_router, router_bias, w_gate, w_up, w_down, ws_gate, ws_up, ws_down, g_final):
    stacked = dict(zip(_LAYER_KEYS, (w_mod, b_mod, g_mix, g_ffn, w_in, w_out, attn_sink, na_rpb, lam_q1, lam_k1,
                                     lam_q2, lam_k2, subln_g, w_router, router_bias, w_gate, w_up, w_down,
                                     ws_gate, ws_up, ws_down)))
    b, s_lat, d = x.shape
    c_len = ctx.shape[1]
    depth = w_mod.shape[0]
    assert s_lat % (NA_QROWS * GRID_W) == 0 and s_lat % c_len == 0 and c_len == TOK_TILE
    cs = jnp.zeros((16, d), F32).at[:b].set(c).at[b].set(c_ctx)
    cos, sin = _rope_tables(s_lat, c_len)
    xc = jnp.concatenate([x, ctx], axis=1)
    gfin = g_final.reshape(1, d)
    for i in range(depth):
        big = ("w_gate", "w_up", "w_down")
        lp = {k: (v if k in big else v[i]) for k, v in stacked.items()}
        lp["layer"] = i
        xc = _layer(xc, cs, cos, sin, lp, i, i == depth - 1, s_lat, gfin)
    return xc
```

```python
import functools
import math

import numpy as np
import jax
import jax.numpy as jnp
from jax import lax
from jax.experimental import pallas as pl
from jax.experimental.pallas import tpu as pltpu

F32 = jnp.float32
BF16 = jnp.bfloat16
I32 = jnp.int32
U32 = jnp.uint32

GRID_W = 64
HEAD_DIM = 64
ROPE_BASE = 10000.0
SWA_WINDOW = 128
NA_KH = 8
NA_KW = 16
N_HEADS = 4
N_EXPERTS = 256
TOP_K = 8
N_GROUPS = 8
TOPK_GROUPS = 4
ROUTED_SCALE = 2.5
MOE_BLOCK = 256
EPS = 1e-6
NEG_INF = -1e30
LOG2E = math.log2(math.e)
Q_COLS = 1024
IN_COLS = 2816
MIX_WIDTH = 1024

LANES = 128
TOK_TILE = 256
ATT_TQ = 256
NA_QROWS = 4
NA_KROWS = NA_QROWS + NA_KH - 1
DIFF_KCHUNK = 256
DIFF_HEADS_PER_STEP = 2
VMEM_LIMIT = 48 * 1024 * 1024


def _params(sem):
    return pltpu.CompilerParams(dimension_semantics=sem, vmem_limit_bytes=VMEM_LIMIT)


def _dot(a, b):
    return jnp.dot(a, b, preferred_element_type=F32)


def _dot_nt(a, b):
    return lax.dot_general(a, b, (((1,), (1,)), ((), ())), preferred_element_type=F32)


def _split_bf16(a):
    hi = a.astype(BF16)
    lo = (a - hi.astype(F32)).astype(BF16)
    return hi, lo


def _sigmoid(x):
    return 1.0 / (1.0 + jnp.exp(-x))


def _rms(x, g):
    return x * lax.rsqrt(jnp.mean(x * x, axis=-1, keepdims=True) + EPS) * g


def _norm_mod(x, g, shift, scale):
    return _rms(x, g) * (1.0 + scale) + shift


def _mod_kernel(c_ref, w_ref, b_ref, o_ref):
    c = c_ref[...]
    a_hi, a_lo = _split_bf16(c * _sigmoid(c))
    w_hi, w_lo = _split_bf16(w_ref[...])
    o_ref[...] = _dot(a_hi, w_hi) + _dot(a_hi, w_lo) + _dot(a_lo, w_hi) + b_ref[...]


def _modulation(cs, w_mod, b_mod):
    n, d = cs.shape
    cols = w_mod.shape[1]
    tn = 1536
    return pl.pallas_call(
        _mod_kernel,
        grid=(cols // tn,),
        in_specs=[pl.BlockSpec((n, d), lambda j: (0, 0)),
                  pl.BlockSpec((d, tn), lambda j: (0, j)),
                  pl.BlockSpec((1, tn), lambda j: (0, j))],
        out_specs=pl.BlockSpec((n, tn), lambda j: (0, j)),
        out_shape=jax.ShapeDtypeStruct((n, cols), F32),
        compiler_params=_params(("arbitrary",)),
        name="modulation",
    )(cs, w_mod, b_mod.reshape(1, cols))


def _proj_kernel(x_ref, g_ref, mod_ref, cos_ref, sin_ref, w_ref,
                 qa_ref, qb_ref, qd_ref, ka_ref, va_ref, kb_ref, vb_ref, kd_ref, vd_ref, vdt_ref, h_scr):
    h = _norm_mod(x_ref[0], g_ref[...], mod_ref[0, 0:1, :], mod_ref[0, 1:2, :])
    h_scr[...] = h.astype(BF16)
    cos = cos_ref[...]
    sin = sin_ref[...]
    tm = h.shape[0]
    lane = lax.broadcasted_iota(I32, (tm, LANES), 1)
    first16 = (lane & 16) == 0
    lo64 = lane < HEAD_DIM

    def rope(v):
        partner = jnp.where(first16, pltpu.roll(v, LANES - 16, 1), pltpu.roll(v, 16, 1))
        return v * cos + partner * sin

    def mm(c0):
        return _dot(h_scr[...], w_ref[:, c0:c0 + 2 * LANES])

    a = mm(0) * (HEAD_DIM ** -0.5)
    c0 = rope(a[:, :LANES])
    c1 = rope(a[:, LANES:])
    zero = jnp.zeros_like(c0)
    qa_ref[0, :, 0 * LANES:1 * LANES] = jnp.where(lo64, c0, zero).astype(BF16)
    qa_ref[0, :, 1 * LANES:2 * LANES] = jnp.where(lo64, pltpu.roll(c0, HEAD_DIM, 1), zero).astype(BF16)
    qa_ref[0, :, 2 * LANES:3 * LANES] = jnp.where(lo64, zero, pltpu.roll(c1, HEAD_DIM, 1)).astype(BF16)
    qa_ref[0, :, 3 * LANES:4 * LANES] = jnp.where(lo64, zero, c1).astype(BF16)
    qb_ref[0] = (mm(256) * (HEAD_DIM ** -0.5)).astype(BF16)
    for j in range(2):
        a = mm(512 + 256 * j) * (HEAD_DIM ** -0.5 * LOG2E)
        qd_ref[0, :, 256 * j:256 * j + LANES] = rope(a[:, :LANES]).astype(BF16)
        qd_ref[0, :, 256 * j + LANES:256 * (j + 1)] = rope(a[:, LANES:]).astype(BF16)
    a = mm(1024)
    ka_ref[0] = rope(a[:, :LANES]).astype(BF16)
    va_ref[0] = a[:, LANES:].astype(BF16)
    kb_ref[0] = mm(1280).astype(BF16)
    vb_ref[0] = mm(1536).astype(BF16)
    for j in range(2):
        a = mm(1792 + 256 * j)
        kd_ref[0, :, 256 * j:256 * j + LANES] = rope(a[:, :LANES]).astype(BF16)
        kd_ref[0, :, 256 * j + LANES:256 * (j + 1)] = rope(a[:, LANES:]).astype(BF16)
        a = mm(2304 + 256 * j)
        vd_ref[0, :, 256 * j:256 * (j + 1)] = a.astype(BF16)
        vdt_ref[0, 256 * j:256 * j + LANES, :] = a[:, :LANES].T.astype(BF16)
        vdt_ref[0, 256 * j + LANES:256 * (j + 1), :] = a[:, LANES:].T.astype(BF16)


def _projection(xc, g, modall, cos, sin, w_in, s_lat):
    b, sall, d = xc.shape
    tm = TOK_TILE
    nlat = s_lat // tm
    row = lambda width: pl.BlockSpec((1, tm, width), lambda i, j: (i, j, 0))
    widths = (512, 256, 512, 128, 128, 256, 256, 512, 512)
    return pl.pallas_call(
        _proj_kernel,
        grid=(b, sall // tm),
        in_specs=[row(d),
                  pl.BlockSpec((1, d), lambda i, j: (0, 0)),
                  pl.BlockSpec((1, 8, d), lambda i, j: (2 * i + (j >= nlat).astype(I32), 0, 0)),
                  pl.BlockSpec((tm, LANES), lambda i, j: (j, 0)),
                  pl.BlockSpec((tm, LANES), lambda i, j: (j, 0)),
                  pl.BlockSpec((d, IN_COLS), lambda i, j: (0, 0))],
        out_specs=[row(w) for w in widths] + [pl.BlockSpec((1, 512, tm), lambda i, j: (i, 0, j))],
        out_shape=([jax.ShapeDtypeStruct((b, sall, w), BF16) for w in widths]
                   + [jax.ShapeDtypeStruct((b, 512, sall), BF16)]),
        scratch_shapes=[pltpu.VMEM((tm, d), BF16)],
        compiler_params=_params(("arbitrary", "arbitrary")),
        name="projection",
    )(xc, g, modall, cos, sin, w_in)


def _half_mask(q, half):
    lane = lax.broadcasted_iota(I32, q.shape, 1)
    keep = (lane < HEAD_DIM) if half == 0 else (lane >= HEAD_DIM)
    return jnp.where(keep, q, jnp.zeros_like(q))


def _merge_halves(lo_part, hi_part):
    lane = lax.broadcasted_iota(I32, lo_part.shape, 1)
    return jnp.where(lane < HEAD_DIM, lo_part, hi_part)


def _gqa_rows(q_ref, g):
    return jnp.concatenate([q_ref[0, :, (2 * g) * LANES:(2 * g + 1) * LANES],
                            q_ref[0, :, (2 * g + 1) * LANES:(2 * g + 2) * LANES]], axis=0)


def _gqa_sink(sink_ref, g, tq):
    rowi = lax.broadcasted_iota(I32, (2 * tq, 1), 0)
    return jnp.where(rowi < tq, sink_ref[2 * g], sink_ref[2 * g + 1])


def _gqa_store(o_ref, g, o, tq):
    top, bot = o[:tq], o[tq:]
    if g == 0:
        chunk = _merge_halves(top, pltpu.roll(bot, HEAD_DIM, 1))
    else:
        chunk = _merge_halves(pltpu.roll(top, HEAD_DIM, 1), bot)
    o_ref[0, :, g * LANES:(g + 1) * LANES] = chunk.astype(BF16)


def _lam(lamv_ref, lam_init):
    v = lamv_ref[...]
    a = jnp.sum(v[0:1] * v[1:2], axis=-1, keepdims=True)
    b = jnp.sum(v[2:3] * v[3:4], axis=-1, keepdims=True)
    return jnp.exp(a) - jnp.exp(b) + lam_init


def _diff_post(o, subg, lam_init):
    return _rms(o, subg) * (1.0 - lam_init)


def _swa_kernel(sink_ref, q_ref, k_ref, v_ref, o_ref, *, s_lat, c_len):
    tq = ATT_TQ
    tk = tq + 2 * SWA_WINDOW
    q0 = pl.program_id(1) * tq
    ks = pl.multiple_of(jnp.clip(q0 - SWA_WINDOW, 0, s_lat - tk), SWA_WINDOW)
    kwin = k_ref[0, pl.ds(ks, tk), :]
    vwin = v_ref[0, pl.ds(ks, tk), :]
    kc = k_ref[0, s_lat:s_lat + c_len, :]
    vc = v_ref[0, s_lat:s_lat + c_len, :]
    qpos = q0 + lax.broadcasted_iota(I32, (tq, tk), 0)
    kpos = ks + lax.broadcasted_iota(I32, (tq, tk), 1)
    valid = jnp.abs(qpos - kpos) <= SWA_WINDOW
    valid = jnp.concatenate([valid, valid], axis=0)
    for g in range(2):
        q2 = _gqa_rows(q_ref, g)
        s_loc = jnp.where(valid, _dot_nt(q2, kwin), NEG_INF)
        s_ctx = _dot_nt(q2, kc)
        snk = _gqa_sink(sink_ref, g, tq)
        m = jnp.maximum(jnp.maximum(jnp.max(s_loc, axis=-1, keepdims=True),
                                    jnp.max(s_ctx, axis=-1, keepdims=True)), snk)
        e_loc = jnp.exp(s_loc - m)
        e_ctx = jnp.exp(s_ctx - m)
        den = (jnp.sum(e_loc, axis=-1, keepdims=True) + jnp.sum(e_ctx, axis=-1, keepdims=True)
               + jnp.exp(snk - m))
        o = (_dot(e_loc.astype(BF16), vwin) + _dot(e_ctx.astype(BF16), vc)) / den
        _gqa_store(o_ref, g, o, tq)


def _swa(sink, qa, ka, va, s_lat):
    b, sall, _ = qa.shape
    kv = pl.BlockSpec((1, sall, LANES), lambda i, j: (i, 0, 0))
    return pl.pallas_call(
        functools.partial(_swa_kernel, s_lat=s_lat, c_len=sall - s_lat),
        grid=(b, s_lat // ATT_TQ),
        in_specs=[pl.BlockSpec(memory_space=pltpu.SMEM),
                  pl.BlockSpec((1, ATT_TQ, 512), lambda i, j: (i, j, 0)), kv, kv],
        out_specs=pl.BlockSpec((1, ATT_TQ, 256), lambda i, j: (i, j, 0)),
        out_shape=jax.ShapeDtypeStruct((b, s_lat, 256), BF16),
        compiler_params=_params(("arbitrary", "arbitrary")),
        name="swa",
    )(sink, qa, ka, va)


def _na_kernel(q_ref, k_ref, v_ref, bm_ref, o_ref, *, s_lat, c_len):
    rows = s_lat // GRID_W
    nk = NA_KROWS * GRID_W
    r0 = pl.program_id(1) * NA_QROWS
    ks = pl.multiple_of(jnp.clip(r0 - NA_KH // 2, 0, rows - NA_KROWS) * GRID_W, GRID_W)
    for c in range(2):
        sl = slice(c * LANES, (c + 1) * LANES)
        kwin = k_ref[0, pl.ds(ks, nk), sl]
        vwin = v_ref[0, pl.ds(ks, nk), sl]
        kc = k_ref[0, s_lat:s_lat + c_len, sl]
        vc = v_ref[0, s_lat:s_lat + c_len, sl]
        q = q_ref[0, :, sl]
        outs = []
        for half in range(2):
            qm = _half_mask(q, half)
            s_loc = _dot_nt(qm, kwin) + bm_ref[0, 2 * c + half]
            s_ctx = _dot_nt(qm, kc)
            m = jnp.maximum(jnp.max(s_loc, axis=-1, keepdims=True), jnp.max(s_ctx, axis=-1, keepdims=True))
            e_loc = jnp.exp(s_loc - m)
            e_ctx = jnp.exp(s_ctx - m)
            den = jnp.sum(e_loc, axis=-1, keepdims=True) + jnp.sum(e_ctx, axis=-1, keepdims=True)
            outs.append((_dot(e_loc.astype(BF16), vwin) + _dot(e_ctx.astype(BF16), vc)) / den)
        o_ref[0, :, sl] = _merge_halves(outs[0], outs[1]).astype(BF16)


def _na_bias_mask(rpb, rows):
    nq = NA_QROWS * GRID_W
    nk = NA_KROWS * GRID_W
    col = np.arange(GRID_W)
    cstart = np.clip(col - NA_KW // 2, 0, GRID_W - NA_KW)
    col_ok = (col[None, :] >= cstart[:, None]) & (col[None, :] < cstart[:, None] + NA_KW)
    dc = np.clip(col[None, :] - col[:, None], -(NA_KW - 1), NA_KW - 1) + NA_KW - 1
    sel_c = np.where(col_ok[:, :, None], np.eye(2 * NA_KW - 1)[dc], 0.0)
    tabs = []
    for r0 in (0, NA_QROWS, rows - NA_QROWS):
        ksr = int(np.clip(r0 - NA_KH // 2, 0, rows - NA_KROWS))
        r = r0 + np.arange(NA_QROWS)
        kr = ksr + np.arange(NA_KROWS)
        rs = np.clip(r - NA_KH // 2, 0, rows - NA_KH)
        row_ok = (kr[None, :] >= rs[:, None]) & (kr[None, :] < rs[:, None] + NA_KH)
        dr = np.clip(kr[None, :] - r[:, None], -(NA_KH - 1), NA_KH - 1) + NA_KH - 1
        sel_r = np.where(row_ok[:, :, None], np.eye(2 * NA_KH - 1)[dr], 0.0)
        ok = (row_ok[:, None, :, None] & col_ok[None, :, None, :]).reshape(nq, nk)
        bias = jnp.einsum("hdc,rjd,qkc->hrqjk", rpb.astype(F32), jnp.asarray(sel_r, F32), jnp.asarray(sel_c, F32),
                          precision=lax.Precision.HIGHEST).reshape(N_HEADS, nq, nk)
        tabs.append(jnp.where(ok[None], bias, NEG_INF))
    return jnp.stack(tabs)


def _na(qb, kb, vb, bm, s_lat):
    b, sall, _ = qb.shape
    nq = NA_QROWS * GRID_W
    nsteps = s_lat // nq
    kv = pl.BlockSpec((1, sall, 256), lambda i, j: (i, 0, 0))

    def bm_map(i, j):
        return (jnp.where(j == 0, 0, jnp.where(j == nsteps - 1, 2, 1)), 0, 0, 0)

    return pl.pallas_call(
        functools.partial(_na_kernel, s_lat=s_lat, c_len=sall - s_lat),
        grid=(b, nsteps),
        in_specs=[pl.BlockSpec((1, nq, 256), lambda i, j: (i, j, 0)), kv, kv,
                  pl.BlockSpec((1, N_HEADS, nq, NA_KROWS * GRID_W), bm_map)],
        out_specs=pl.BlockSpec((1, nq, 256), lambda i, j: (i, j, 0)),
        out_shape=jax.ShapeDtypeStruct((b, s_lat, 256), BF16),
        compiler_params=_params(("arbitrary", "arbitrary")),
        name="na2d",
    )(qb, kb, vb, bm)


def _diff_kernel(lamv_ref, subg_ref, q_ref, k_ref, vt_ref, o_ref, e_scr, m_scr, *, n_keys, lam_init):
    tq = ATT_TQ
    chunks = [(c0, min(DIFF_KCHUNK, n_keys - c0)) for c0 in range(0, n_keys, DIFF_KCHUNK)]
    lam = _lam(lamv_ref, lam_init)
    stats = []
    for hh in range(DIFF_HEADS_PER_STEP):
        sl = slice(hh * LANES, (hh + 1) * LANES)
        q = q_ref[0, :, sl]
        q12 = jnp.concatenate([_half_mask(q, 0), _half_mask(q, 1)], axis=0)
        m = jnp.full((1, 2 * tq), NEG_INF, F32)
        den = jnp.zeros((1, 2 * tq), F32)
        for i, (c0, n) in enumerate(chunks):
            s = _dot_nt(k_ref[0, c0:c0 + n, sl], q12)
            m_new = jnp.maximum(m, jnp.max(s, axis=0, keepdims=True))
            e = jnp.exp2(s - m_new)
            e_scr[hh, c0:c0 + n, :] = e
            m_scr[hh, i:i + 1, :] = m_new
            den = den * jnp.exp2(m - m_new) + jnp.sum(e, axis=0, keepdims=True)
            m = m_new
        stats.append((m, den))
    for hh in range(DIFF_HEADS_PER_STEP):
        sl = slice(hh * LANES, (hh + 1) * LANES)
        m, den = stats[hh]
        acc = jnp.zeros((LANES, tq), F32)
        for i, (c0, n) in enumerate(chunks):
            f = jnp.exp2(m_scr[hh, i:i + 1, :] - m) / den
            p = e_scr[hh, c0:c0 + n, :tq] * f[:, :tq] - e_scr[hh, c0:c0 + n, tq:] * (lam * f[:, tq:])
            acc = acc + _dot(vt_ref[0, sl, c0:c0 + n], p.astype(BF16))
        o_ref[0, :, sl] = _diff_post(acc.T, subg_ref[...], lam_init).astype(BF16)


def _diff(lamv, subg, qd, kd, vdt, s_lat, lam_init):
    b, sall, _ = qd.shape
    nchunk = -(-sall // DIFF_KCHUNK)
    hps = DIFF_HEADS_PER_STEP
    wid = hps * LANES
    return pl.pallas_call(
        functools.partial(_diff_kernel, n_keys=sall, lam_init=lam_init),
        grid=(b, N_HEADS // hps, s_lat // ATT_TQ),
        in_specs=[pl.BlockSpec((8, LANES), lambda i, h, j: (0, 0)),
                  pl.BlockSpec((1, LANES), lambda i, h, j: (0, 0)),
                  pl.BlockSpec((1, ATT_TQ, wid), lambda i, h, j: (i, j, h)),
                  pl.BlockSpec((1, sall, wid), lambda i, h, j: (i, 0, h)),
                  pl.BlockSpec((1, wid, sall), lambda i, h, j: (i, h, 0))],
        out_specs=pl.BlockSpec((1, ATT_TQ, wid), lambda i, h, j: (i, j, h)),
        out_shape=jax.ShapeDtypeStruct((b, s_lat, 512), BF16),
        scratch_shapes=[pltpu.VMEM((hps, sall, 2 * ATT_TQ), F32),
                        pltpu.VMEM((hps, -(-nchunk // 8) * 8, 2 * ATT_TQ), F32)],
        compiler_params=_params(("arbitrary", "arbitrary", "arbitrary")),
        name="diff_attn",
    )(lamv, subg, qd, kd, vdt)


def _ctx_kernel(sink_ref, lamv_ref, subg_ref, qa_ref, ka_ref, va_ref, qb_ref, kb_ref, vb_ref,
                qd_ref, kd_ref, vd_ref, oa_ref, ob_ref, od_ref, *, lam_init):
    cl = qa_ref.shape[1]
    ka = ka_ref[0]
    va = va_ref[0]
    for g in range(2):
        q2 = _gqa_rows(qa_ref, g)
        s = _dot_nt(q2, ka)
        snk = _gqa_sink(sink_ref, g, cl)
        m = jnp.maximum(jnp.max(s, axis=-1, keepdims=True), snk)
        e = jnp.exp(s - m)
        den = jnp.sum(e, axis=-1, keepdims=True) + jnp.exp(snk - m)
        _gqa_store(oa_ref, g, _dot(e.astype(BF16), va) / den, cl)
    for c in range(2):
        sl = slice(c * LANES, (c + 1) * LANES)
        outs = []
        for half in range(2):
            s = _dot_nt(_half_mask(qb_ref[0, :, sl], half), kb_ref[0, :, sl])
            e = jnp.exp(s - jnp.max(s, axis=-1, keepdims=True))
            outs.append(_dot(e.astype(BF16), vb_ref[0, :, sl]) / jnp.sum(e, axis=-1, keepdims=True))
        ob_ref[0, :, sl] = _merge_halves(outs[0], outs[1]).astype(BF16)
    lam = _lam(lamv_ref, lam_init)
    for h in range(N_HEADS):
        sl = slice(h * LANES, (h + 1) * LANES)
        q = qd_ref[0, :, sl]
        q12 = jnp.concatenate([_half_mask(q, 0), _half_mask(q, 1)], axis=0)
        s = _dot_nt(q12, kd_ref[0, :, sl])
        e = jnp.exp2(s - jnp.max(s, axis=-1, keepdims=True))
        den = jnp.sum(e, axis=-1, keepdims=True)
        p = e[:cl] * (1.0 / den[:cl]) - e[cl:] * (lam / den[cl:])
        o = _dot(p.astype(BF16), vd_ref[0, :, sl])
        od_ref[0, :, sl] = _diff_post(o, subg_ref[...], lam_init).astype(BF16)


def _ctx_attention(sink, lamv, subg, qa, ka, va, qb, kb, vb, qd, kd, vd, s_lat, lam_init):
    b, sall, _ = qa.shape
    cl = sall - s_lat
    blk = s_lat // cl
    row = lambda width: pl.BlockSpec((1, cl, width), lambda i: (i, blk, 0))
    out = lambda width: pl.BlockSpec((1, cl, width), lambda i: (i, 0, 0))
    return pl.pallas_call(
        functools.partial(_ctx_kernel, lam_init=lam_init),
        grid=(b,),
        in_specs=[pl.BlockSpec(memory_space=pltpu.SMEM),
                  pl.BlockSpec((8, LANES), lambda i: (0, 0)),
                  pl.BlockSpec((1, LANES), lambda i: (0, 0)),
                  row(512), row(128), row(128), row(256), row(256), row(256), row(512), row(512), row(512)],
        out_specs=[out(256), out(256), out(512)],
        out_shape=[jax.ShapeDtypeStruct((b, cl, w), BF16) for w in (256, 256, 512)],
        compiler_params=_params(("arbitrary",)),
        name="ctx_attention",
    )(sink, lamv, subg, qa, ka, va, qb, kb, vb, qd, kd, vd)


def _out_kernel(x_ref, w_ref, mod_ref, *refs, nlat):
    xo_ref = refs[-1]

    def emit(oa_ref, ob_ref, od_ref):
        attn = (_dot(oa_ref[0], w_ref[0:256, :]) + _dot(ob_ref[0], w_ref[256:512, :])
                + _dot(od_ref[0], w_ref[512:1024, :]))
        xo_ref[0] = x_ref[0] + mod_ref[0, 2:3, :] * attn

    if len(refs) == 4:
        emit(*refs[:3])
    else:
        is_ctx = pl.program_id(1) >= nlat
        pl.when(jnp.logical_not(is_ctx))(lambda: emit(*refs[:3]))
        pl.when(is_ctx)(lambda: emit(*refs[3:6]))


def _out_projection(xc, lat, ctx, w_out, modall, s_lat):
    b, sall, d = xc.shape
    tm = TOK_TILE
    nlat = s_lat // tm
    n_rows = s_lat if ctx is None else sall
    row = lambda width: pl.BlockSpec((1, tm, width), lambda i, j: (i, j, 0))
    lat_row = lambda width: pl.BlockSpec((1, tm, width), lambda i, j: (i, jnp.minimum(j, nlat - 1), 0))
    ctx_row = lambda width: pl.BlockSpec((1, tm, width), lambda i, j: (i, 0, 0))
    widths = (256, 256, 512)
    specs = [lat_row(w) for w in widths] + ([] if ctx is None else [ctx_row(w) for w in widths])
    return pl.pallas_call(
        functools.partial(_out_kernel, nlat=nlat),
        grid=(b, n_rows // tm),
        in_specs=[row(d),
                  pl.BlockSpec((MIX_WIDTH, d), lambda i, j: (0, 0)),
                  pl.BlockSpec((1, 8, d), lambda i, j: (2 * i + (j >= nlat).astype(I32), 0, 0))] + specs,
        out_specs=row(d),
        out_shape=jax.ShapeDtypeStruct((b, n_rows, d), F32),
        compiler_params=_params(("arbitrary", "arbitrary")),
        name="out_projection",
    )(xc, w_out, modall, *lat, *(() if ctx is None else ctx))


def _first_argmax(v, iota, n):
    m = jnp.max(v, axis=0, keepdims=True)
    ix = jnp.min(jnp.where(v == m, iota, float(n)), axis=0, keepdims=True)
    return m, ix


def _router_kernel(x_ref, g_ref, mod_ref, whi_ref, wlo_ref, bias_ref, tri_ref,
                   eidx_ref, w_ref, pos_ref, cnt_ref):
    @pl.when(pl.program_id(0) == 0)
    def _():
        cnt_ref[...] = jnp.zeros_like(cnt_ref)

    hf = _norm_mod(x_ref[...], g_ref[...], mod_ref[0, 3:4, :], mod_ref[0, 4:5, :])
    tm = hf.shape[0]
    h_hi, h_lo = _split_bf16(hf)
    whi = whi_ref[...]
    logits = _dot_nt(whi, h_hi) + _dot_nt(whi, h_lo) + _dot_nt(wlo_ref[...], h_hi)
    scores = _sigmoid(logits)
    biased = scores + bias_ref[...]
    gsz = N_EXPERTS // N_GROUPS
    iota_g = lax.broadcasted_iota(I32, (gsz, tm), 0).astype(F32)
    gscore = []
    for g in range(N_GROUPS):
        v = biased[g * gsz:(g + 1) * gsz]
        m1, i1 = _first_argmax(v, iota_g, gsz)
        m2 = jnp.max(jnp.where(iota_g == i1, -jnp.inf, v), axis=0, keepdims=True)
        gscore.append(m1 + m2)
    cur = jnp.concatenate(gscore, axis=0)
    iota_n = lax.broadcasted_iota(I32, (N_GROUPS, tm), 0).astype(F32)
    gsel = jnp.zeros((N_GROUPS, tm), F32)
    for _ in range(TOPK_GROUPS):
        _, ix = _first_argmax(cur, iota_n, N_GROUPS)
        hit = iota_n == ix
        gsel = jnp.where(hit, 1.0, gsel)
        cur = jnp.where(hit, -jnp.inf, cur)
    masked = jnp.concatenate(
        [jnp.where(gsel[g:g + 1] > 0.5, biased[g * gsz:(g + 1) * gsz], -jnp.inf) for g in range(N_GROUPS)],
        axis=0)
    iota_e = lax.broadcasted_iota(I32, (N_EXPERTS, tm), 0).astype(F32)
    onehot = jnp.zeros((N_EXPERTS, tm), F32)
    idxs, ws = [], []
    for _ in range(TOP_K):
        _, ix = _first_argmax(masked, iota_e, N_EXPERTS)
        hit = iota_e == ix
        ws.append(jnp.sum(jnp.where(hit, scores, 0.0), axis=0, keepdims=True))
        masked = jnp.where(hit, -jnp.inf, masked)
        onehot = jnp.where(hit, 1.0, onehot)
        idxs.append(ix)
    cnt = cnt_ref[...]
    rank = _dot(onehot.astype(BF16), tri_ref[...]) + cnt
    wsum = ws[0]
    for wk in ws[1:]:
        wsum = wsum + wk
    for k in range(TOP_K):
        pos = jnp.sum(jnp.where(iota_e == idxs[k], rank, 0.0), axis=0, keepdims=True)
        eidx_ref[0, k:k + 1, :] = idxs[k].astype(I32)
        pos_ref[0, k:k + 1, :] = pos.astype(I32)
        w_ref[0, k:k + 1, :] = ws[k] / wsum * ROUTED_SCALE
    cnt_ref[...] = cnt + jnp.sum(onehot, axis=1, keepdims=True)


def _router(xflat, g, modall, whi, wlo, bias, tri, tiles_per_batch, nlat):
    t, d = xflat.shape
    tm = TOK_TILE
    nt = t // tm
    meta = pl.BlockSpec((1, TOP_K, tm), lambda i: (i, 0, 0))

    def mod_map(i):
        return (2 * (i // tiles_per_batch) + ((i % tiles_per_batch) >= nlat).astype(I32), 0, 0)

    return pl.pallas_call(
        _router_kernel,
        grid=(nt,),
        in_specs=[pl.BlockSpec((tm, d), lambda i: (i, 0)),
                  pl.BlockSpec((1, d), lambda i: (0, 0)),
                  pl.BlockSpec((1, 8, d), mod_map),
                  pl.BlockSpec((N_EXPERTS, d), lambda i: (0, 0)),
                  pl.BlockSpec((N_EXPERTS, d), lambda i: (0, 0)),
                  pl.BlockSpec((N_EXPERTS, 1), lambda i: (0, 0)),
                  pl.BlockSpec((tm, tm), lambda i: (0, 0))],
        out_specs=[meta, meta, meta, pl.BlockSpec((N_EXPERTS, 1), lambda i: (0, 0))],
        out_shape=[jax.ShapeDtypeStruct((nt, TOP_K, tm), I32),
                   jax.ShapeDtypeStruct((nt, TOP_K, tm), F32),
                   jax.ShapeDtypeStruct((nt, TOP_K, tm), I32),
                   jax.ShapeDtypeStruct((N_EXPERTS, 1), F32)],
        compiler_params=_params(("arbitrary",)),
        name="router",
    )(xflat, g, modall, whi, wlo, bias, tri)


def _row_copy(src, dst, sem):
    return pltpu.make_async_copy(src, dst, sem)


def _dispatch_kernel(eidx_ref, pos_ref, pstart_ref, x_ref, g_ref, mod_ref, xs_in, xs_ref, hp_scr, sem):
    del xs_in
    hf = _norm_mod(x_ref[...], g_ref[...], mod_ref[0, 3:4, :], mod_ref[0, 4:5, :])
    tm, d = hf.shape
    hb = hf.astype(BF16).astype(F32)
    lo = lax.shift_right_logical(lax.bitcast_convert_type(hb[:, :d // 2], U32), jnp.uint32(16))
    hi = lax.bitcast_convert_type(hb[:, d // 2:], U32) & jnp.uint32(0xFFFF0000)
    hp_scr[...] = lo | hi

    def start(t, carry):
        for k in range(TOP_K):
            dst = pstart_ref[eidx_ref[0, k, t]] + pos_ref[0, k, t]
            _row_copy(hp_scr.at[pl.ds(t, 1)], xs_ref.at[pl.ds(dst, 1)], sem).start()
        return carry

    lax.fori_loop(0, tm, start, 0)

    def wait(t, carry):
        for k in range(TOP_K):
            _row_copy(hp_scr.at[pl.ds(0, 1)], xs_ref.at[pl.ds(0, 1)], sem).wait()
        return carry

    lax.fori_loop(0, tm, wait, 0)


def _dispatch(eidx, pos, pstart, xflat, g, modall, xs0, tiles_per_batch, nlat):
    t, d = xflat.shape
    tm = TOK_TILE
    nt = t // tm
    meta = pl.BlockSpec((1, TOP_K, tm), lambda i: (i, 0, 0), memory_space=pltpu.SMEM)

    def mod_map(i):
        return (2 * (i // tiles_per_batch) + ((i % tiles_per_batch) >= nlat).astype(I32), 0, 0)

    return pl.pallas_call(
        _dispatch_kernel,
        grid=(nt,),
        in_specs=[meta, meta, pl.BlockSpec(memory_space=pltpu.SMEM),
                  pl.BlockSpec((tm, d), lambda i: (i, 0)),
                  pl.BlockSpec((1, d), lambda i: (0, 0)),
                  pl.BlockSpec((1, 8, d), mod_map),
                  pl.BlockSpec(memory_space=pl.ANY)],
        out_specs=pl.BlockSpec(memory_space=pl.ANY),
        out_shape=jax.ShapeDtypeStruct(xs0.shape, xs0.dtype),
        input_output_aliases={6: 0},
        scratch_shapes=[pltpu.VMEM((tm, d // 2), U32), pltpu.SemaphoreType.DMA(())],
        compiler_params=_params(("arbitrary",)),
        name="moe_dispatch",
    )(eidx, pos, pstart, xflat, g, modall, xs0)


def _expert_kernel(blk_e_ref, nact_ref, xs_ref, wg_ref, wu_ref, wd_ref, ys_ref, wg_s, wu_s, wd_s):
    b = pl.program_id(0)
    changed = jnp.logical_or(b == 0, blk_e_ref[b] != blk_e_ref[jnp.maximum(b - 1, 0)])

    @pl.when(changed)
    def _():
        wg_s[...] = wg_ref[0, 0].astype(BF16)
        wu_s[...] = wu_ref[0, 0].astype(BF16)
        wd_s[...] = wd_ref[0, 0].astype(BF16)

    @pl.when(b < nact_ref[0])
    def _():
        xs = xs_ref[...]
        half = xs.shape[1]
        x_lo = lax.bitcast_convert_type(lax.shift_left(xs, jnp.uint32(16)), F32).astype(BF16)
        x_hi = lax.bitcast_convert_type(xs & jnp.uint32(0xFFFF0000), F32).astype(BF16)
        gate = _dot(x_lo, wg_s[0:half, :]) + _dot(x_hi, wg_s[half:, :])
        up = _dot(x_lo, wu_s[0:half, :]) + _dot(x_hi, wu_s[half:, :])
        mid = (gate * _sigmoid(gate)) * up
        ys_ref[...] = _dot(mid.astype(BF16), wd_s[...])

    @pl.when(b >= nact_ref[0])
    def _():
        ys_ref[...] = jnp.zeros_like(ys_ref)


def _experts(blk_e, nact, xs, w_gate, w_up, w_down, layer):
    n_rows, half = xs.shape
    _, _, d, f = w_gate.shape
    n_blk = n_rows // MOE_BLOCK
    grid_spec = pltpu.PrefetchScalarGridSpec(
        num_scalar_prefetch=2,
        grid=(n_blk,),
        in_specs=[pl.BlockSpec((MOE_BLOCK, half), lambda i, be, na: (i, 0)),
                  pl.BlockSpec((1, 1, d, f), lambda i, be, na: (layer, be[i], 0, 0)),
                  pl.BlockSpec((1, 1, d, f), lambda i, be, na: (layer, be[i], 0, 0)),
                  pl.BlockSpec((1, 1, f, d), lambda i, be, na: (layer, be[i], 0, 0))],
        out_specs=pl.BlockSpec((MOE_BLOCK, d), lambda i, be, na: (i, 0)),
        scratch_shapes=[pltpu.VMEM((d, f), BF16), pltpu.VMEM((d, f), BF16), pltpu.VMEM((f, d), BF16)],
    )
    return pl.pallas_call(
        _expert_kernel,
        grid_spec=grid_spec,
        out_shape=jax.ShapeDtypeStruct((n_rows, d), F32),
        compiler_params=_params(("arbitrary",)),
        name="moe_experts",
    )(blk_e, nact, xs, w_gate, w_up, w_down)


def _combine_kernel(eidx_ref, pos_ref, pstart_ref, x_ref, g_ref, mod_ref, wt_ref, wsg_ref, wsu_ref, wsd_ref,
                    gfin_ref, ys_ref, o_ref, gbuf, sem, *, final_norm):
    tm = x_ref.shape[0]

    def start(t, carry):
        for k in range(TOP_K):
            src = pstart_ref[eidx_ref[0, k, t]] + pos_ref[0, k, t]
            _row_copy(ys_ref.at[pl.ds(src, 1)], gbuf.at[k, pl.ds(t, 1)], sem).start()
        return carry

    lax.fori_loop(0, tm, start, 0)

    x = x_ref[...]
    hb = _norm_mod(x, g_ref[...], mod_ref[0, 3:4, :], mod_ref[0, 4:5, :]).astype(BF16)
    gate = _dot(hb, wsg_ref[...])
    mid = (gate * _sigmoid(gate)) * _dot(hb, wsu_ref[...])
    shared = _dot(mid.astype(BF16), wsd_ref[...])

    def wait(t, carry):
        for k in range(TOP_K):
            _row_copy(ys_ref.at[pl.ds(0, 1)], gbuf.at[0, pl.ds(0, 1)], sem).wait()
        return carry

    lax.fori_loop(0, tm, wait, 0)

    y = gbuf[0] * wt_ref[0, :, 0:1]
    for k in range(1, TOP_K):
        y = y + gbuf[k] * wt_ref[0, :, k:k + 1]
    out = x + mod_ref[0, 5:6, :] * (y + shared)
    if final_norm:
        out = _rms(out, gfin_ref[...])
    o_ref[...] = out


def _combine(eidx, pos, pstart, xflat, g, modall, wt, wsg, wsu, wsd, gfin, ys, tiles_per_batch, nlat, final_norm):
    t, d = xflat.shape
    f = wsg.shape[1]
    tm = TOK_TILE
    nt = t // tm
    meta = pl.BlockSpec((1, TOP_K, tm), lambda i: (i, 0, 0), memory_space=pltpu.SMEM)

    def mod_map(i):
        return (2 * (i // tiles_per_batch) + ((i % tiles_per_batch) >= nlat).astype(I32), 0, 0)

    return pl.pallas_call(
        functools.partial(_combine_kernel, final_norm=final_norm),
        grid=(nt,),
        in_specs=[meta, meta, pl.BlockSpec(memory_space=pltpu.SMEM),
                  pl.BlockSpec((tm, d), lambda i: (i, 0)),
                  pl.BlockSpec((1, d), lambda i: (0, 0)),
                  pl.BlockSpec((1, 8, d), mod_map),
                  pl.BlockSpec((1, tm, TOP_K), lambda i: (i, 0, 0)),
                  pl.BlockSpec((d, f), lambda i: (0, 0)),
                  pl.BlockSpec((d, f), lambda i: (0, 0)),
                  pl.BlockSpec((f, d), lambda i: (0, 0)),
                  pl.BlockSpec((1, d), lambda i: (0, 0)),
                  pl.BlockSpec(memory_space=pl.ANY)],
        out_specs=pl.BlockSpec((tm, d), lambda i: (i, 0)),
        out_shape=jax.ShapeDtypeStruct((t, d), F32),
        scratch_shapes=[pltpu.VMEM((TOP_K, tm, d), F32), pltpu.SemaphoreType.DMA(())],
        compiler_params=_params(("arbitrary",)),
        name="moe_combine",
    )(eidx, pos, pstart, xflat, g, modall, wt, wsg, wsu, wsd, gfin, ys)


def _rope_tables(s_lat, c_len):
    n_freq = HEAD_DIM // 4
    inv = 1.0 / (ROPE_BASE ** (jnp.arange(n_freq, dtype=F32) / n_freq))
    t = jnp.arange(s_lat)
    pos = jnp.stack([(t // GRID_W).astype(F32), (t % GRID_W).astype(F32)], axis=1)
    lane = np.arange(LANES)
    ang = pos[:, (lane % HEAD_DIM) // 32] * inv[lane % 16][None, :]
    sign = jnp.asarray(np.where(lane % 32 < 16, -1.0, 1.0), F32)
    cos = jnp.concatenate([jnp.cos(ang), jnp.ones((c_len, LANES), F32)], axis=0)
    sin = jnp.concatenate([jnp.sin(ang) * sign, jnp.zeros((c_len, LANES), F32)], axis=0)
    return cos, sin


def _moe(xflat, lp, modall, gfin, tiles_per_batch, nlat, final_norm):
    t, d = xflat.shape
    nt = t // TOK_TILE
    g = lp["g_ffn"].reshape(1, d)
    w_r = lp["w_router"].T
    whi = w_r.astype(BF16)
    wlo = (w_r - whi.astype(F32)).astype(BF16)
    tri = jnp.asarray(np.triu(np.ones((TOK_TILE, TOK_TILE), np.float32), 1), BF16)
    eidx, w, pos, cnt = _router(xflat, g, modall, whi, wlo, lp["router_bias"].reshape(N_EXPERTS, 1), tri,
                                tiles_per_batch, nlat)
    counts = cnt[:, 0].astype(I32)
    padded = (counts + MOE_BLOCK - 1) // MOE_BLOCK * MOE_BLOCK
    pend = jnp.cumsum(padded)
    pstart = (pend - padded).astype(I32)
    n_blk = -(-(t * TOP_K) // MOE_BLOCK) + N_EXPERTS
    blk_row0 = jnp.arange(n_blk, dtype=I32) * MOE_BLOCK
    blk_e = jnp.minimum(jnp.sum((pend[None, :] <= blk_row0[:, None]).astype(I32), axis=1), N_EXPERTS - 1)
    nact = (pend[-1] // MOE_BLOCK).astype(I32).reshape(1)
    xs0 = jnp.zeros((n_blk * MOE_BLOCK, d // 2), U32)
    xs = _dispatch(eidx, pos, pstart, xflat, g, modall, xs0, tiles_per_batch, nlat)
    ys = _experts(blk_e, nact, xs, lp["w_gate"], lp["w_up"], lp["w_down"], lp["layer"])
    wt = jnp.transpose(w, (0, 2, 1))
    return _combine(eidx, pos, pstart, xflat, g, modall, wt, lp["ws_gate"].astype(BF16), lp["ws_up"].astype(BF16),
                    lp["ws_down"].astype(BF16), gfin, ys, tiles_per_batch, nlat, final_norm)


def _layer(xc, cs, cos, sin, lp, layer_idx, last, s_lat, gfin):
    b, sall, d = xc.shape
    c_len = sall - s_lat
    lam_init = 0.8 - 0.6 * math.exp(-0.3 * layer_idx)
    mod = _modulation(cs, lp["w_mod"], lp["b_mod"])
    mod_lat = mod[:b].reshape(b, 1, 6, d)
    mod_ctx = jnp.broadcast_to(mod[b].reshape(1, 1, 6, d), (b, 1, 6, d))
    modall = jnp.concatenate([mod_lat, mod_ctx], axis=1)
    modall = jnp.pad(modall, ((0, 0), (0, 0), (0, 2), (0, 0))).reshape(2 * b, 8, d)

    qa, qb, qd, ka, va, kb, vb, kd, vd, vdt = _projection(
        xc, lp["g_mix"].reshape(1, d), modall, cos, sin, lp["w_in"].astype(BF16), s_lat)
    sink = lp["attn_sink"].astype(F32)
    lamv = jnp.zeros((8, LANES), F32).at[0:4, 0:HEAD_DIM].set(
        jnp.stack([lp["lam_q1"], lp["lam_k1"], lp["lam_q2"], lp["lam_k2"]]).astype(F32))
    subg = lp["subln_g"].reshape(1, LANES).astype(F32)
    oa = _swa(sink, qa, ka, va, s_lat)
    ob = _na(qb, kb, vb, _na_bias_mask(lp["na_rpb"], s_lat // GRID_W), s_lat)
    od = _diff(lamv, subg, qd, kd, vdt, s_lat, lam_init)
    ctx_out = None
    if not last:
        ctx_out = _ctx_attention(sink, lamv, subg, qa, ka, va, qb, kb, vb, qd, kd, vd, s_lat, lam_init)
    n_rows = s_lat if last else sall
    x2 = _out_projection(xc, (oa, ob, od), ctx_out, lp["w_out"].astype(BF16), modall, s_lat)
    tiles_per_batch = n_rows // TOK_TILE
    y = _moe(x2.reshape(b * n_rows, d), lp, modall, gfin, tiles_per_batch, s_lat // TOK_TILE, last)
    return y.reshape(b, n_rows, d)


_LAYER_KEYS = ("w_mod", "b_mod", "g_mix", "g_ffn", "w_in", "w_out", "attn_sink", "na_rpb", "lam_q1", "lam_k1",
               "lam_q2", "lam_k2", "subln_g", "w_router", "router_bias", "w_gate", "w_up", "w_down",
               "ws_gate", "ws_up", "ws_down")


def kernel(x, c, ctx, c_ctx, w_mod, b_mod, g_mix, g_ffn, w_in, w_out, attn_sink, na_rpb, lam_q1, lam_k1, lam_q2,
           lam_k2, subln_g, w_router, router_bias, w_gate, w_up, w_down, ws_gate, ws_up, ws_down, g_final):
    stacked = dict(zip(_LAYER_KEYS, (w_mod, b_mod, g_mix, g_ffn, w_in, w_out, attn_sink, na_rpb, lam_q1, lam_k1,
                                     lam_q2, lam_k2, subln_g, w_router, router_bias, w_gate, w_up, w_down,
                                     ws_gate, ws_up, ws_down)))
    b, s_lat, d = x.shape
    c_len = ctx.shape[1]
    depth = w_mod.shape[0]
    assert s_lat % (NA_QROWS * GRID_W) == 0 and s_lat % c_len == 0 and c_len == TOK_TILE
    cs = jnp.zeros((16, d), F32).at[:b].set(c).at[b].set(c_ctx)
    cos, sin = _rope_tables(s_lat, c_len)
    xc = jnp.concatenate([x, ctx], axis=1)
    gfin = g_final.reshape(1, d)
    for i in range(depth):
        big = ("w_gate", "w_up", "w_down")
        lp = {k: (v if k in big else v[i]) for k, v in stacked.items()}
        lp["layer"] = i
        xc = _layer(xc, cs, cos, sin, lp, i, i == depth - 1, s_lat, gfin)
    return xc
```

```python
import functools
import math

import numpy as np
import jax
import jax.numpy as jnp
from jax import lax
from jax.experimental import pallas as pl
from jax.experimental.pallas import tpu as pltpu

F32 = jnp.float32
BF16 = jnp.bfloat16
I32 = jnp.int32
U32 = jnp.uint32

GRID_W = 64
HEAD_DIM = 64
ROPE_BASE = 10000.0
SWA_WINDOW = 128
NA_KH = 8
NA_KW = 16
N_HEADS = 4
N_EXPERTS = 256
TOP_K = 8
N_GROUPS = 8
TOPK_GROUPS = 4
ROUTED_SCALE = 2.5
MOE_BLOCK = 256
ROW_CHUNK = 8
COMBINE_KCHUNK = 512
EPS = 1e-6
NEG_INF = -1e30
LOG2E = math.log2(math.e)
Q_COLS = 1024
IN_COLS = 2816
MIX_WIDTH = 1024

LANES = 128
TOK_TILE = 256
ATT_TQ = 256
NA_QROWS = 4
NA_KROWS = NA_QROWS + NA_KH - 1
DIFF_KCHUNK = 256
DIFF_HEADS_PER_STEP = 2
VMEM_LIMIT = 48 * 1024 * 1024


def _params(sem):
    return pltpu.CompilerParams(dimension_semantics=sem, vmem_limit_bytes=VMEM_LIMIT)


def _dot(a, b):
    return jnp.dot(a, b, preferred_element_type=F32)


def _dot_nt(a, b):
    return lax.dot_general(a, b, (((1,), (1,)), ((), ())), preferred_element_type=F32)


def _split_bf16(a):
    hi = a.astype(BF16)
    lo = (a - hi.astype(F32)).astype(BF16)
    return hi, lo


def _sigmoid(x):
    return 1.0 / (1.0 + jnp.exp(-x))


def _rms(x, g):
    return x * lax.rsqrt(jnp.mean(x * x, axis=-1, keepdims=True) + EPS) * g


def _norm_mod(x, g, shift, scale):
    return _rms(x, g) * (1.0 + scale) + shift


def _mod_kernel(c_ref, w_ref, b_ref, o_ref):
    c = c_ref[...]
    a_hi, a_lo = _split_bf16(c * _sigmoid(c))
    w_hi, w_lo = _split_bf16(w_ref[...])
    o_ref[...] = _dot(a_hi, w_hi) + _dot(a_hi, w_lo) + _dot(a_lo, w_hi) + b_ref[...]


def _modulation(cs, w_mod, b_mod):
    n, d = cs.shape
    cols = w_mod.shape[1]
    tn = 1536
    return pl.pallas_call(
        _mod_kernel,
        grid=(cols // tn,),
        in_specs=[pl.BlockSpec((n, d), lambda j: (0, 0)),
                  pl.BlockSpec((d, tn), lambda j: (0, j)),
                  pl.BlockSpec((1, tn), lambda j: (0, j))],
        out_specs=pl.BlockSpec((n, tn), lambda j: (0, j)),
        out_shape=jax.ShapeDtypeStruct((n, cols), F32),
        compiler_params=_params(("arbitrary",)),
        name="modulation",
    )(cs, w_mod, b_mod.reshape(1, cols))


def _proj_kernel(x_ref, g_ref, mod_ref, cos_ref, sin_ref, w_ref,
                 qa_ref, qb_ref, qd_ref, ka_ref, va_ref, kb_ref, vb_ref, kd_ref, vd_ref, vdt_ref, h_scr):
    h = _norm_mod(x_ref[0], g_ref[...], mod_ref[0, 0:1, :], mod_ref[0, 1:2, :])
    h_scr[...] = h.astype(BF16)
    cos = cos_ref[...]
    sin = sin_ref[...]
    tm = h.shape[0]
    lane = lax.broadcasted_iota(I32, (tm, LANES), 1)
    first16 = (lane & 16) == 0
    lo64 = lane < HEAD_DIM

    def rope(v):
        partner = jnp.where(first16, pltpu.roll(v, LANES - 16, 1), pltpu.roll(v, 16, 1))
        return v * cos + partner * sin

    def mm(c0):
        return _dot(h_scr[...], w_ref[:, c0:c0 + 2 * LANES])

    a = mm(0) * (HEAD_DIM ** -0.5)
    c0 = rope(a[:, :LANES])
    c1 = rope(a[:, LANES:])
    zero = jnp.zeros_like(c0)
    qa_ref[0, :, 0 * LANES:1 * LANES] = jnp.where(lo64, c0, zero).astype(BF16)
    qa_ref[0, :, 1 * LANES:2 * LANES] = jnp.where(lo64, pltpu.roll(c0, HEAD_DIM, 1), zero).astype(BF16)
    qa_ref[0, :, 2 * LANES:3 * LANES] = jnp.where(lo64, zero, pltpu.roll(c1, HEAD_DIM, 1)).astype(BF16)
    qa_ref[0, :, 3 * LANES:4 * LANES] = jnp.where(lo64, zero, c1).astype(BF16)
    qb_ref[0] = (mm(256) * (HEAD_DIM ** -0.5)).astype(BF16)
    for j in range(2):
        a = mm(512 + 256 * j) * (HEAD_DIM ** -0.5 * LOG2E)
        qd_ref[0, :, 256 * j:256 * j + LANES] = rope(a[:, :LANES]).astype(BF16)
        qd_ref[0, :, 256 * j + LANES:256 * (j + 1)] = rope(a[:, LANES:]).astype(BF16)
    a = mm(1024)
    ka_ref[0] = rope(a[:, :LANES]).astype(BF16)
    va_ref[0] = a[:, LANES:].astype(BF16)
    kb_ref[0] = mm(1280).astype(BF16)
    vb_ref[0] = mm(1536).astype(BF16)
    for j in range(2):
        a = mm(1792 + 256 * j)
        kd_ref[0, :, 256 * j:256 * j + LANES] = rope(a[:, :LANES]).astype(BF16)
        kd_ref[0, :, 256 * j + LANES:256 * (j + 1)] = rope(a[:, LANES:]).astype(BF16)
        a = mm(2304 + 256 * j)
        vd_ref[0, :, 256 * j:256 * (j + 1)] = a.astype(BF16)
        vdt_ref[0, 256 * j:256 * j + LANES, :] = a[:, :LANES].T.astype(BF16)
        vdt_ref[0, 256 * j + LANES:256 * (j + 1), :] = a[:, LANES:].T.astype(BF16)


def _projection(xc, g, modall, cos, sin, w_in, s_lat):
    b, sall, d = xc.shape
    tm = TOK_TILE
    nlat = s_lat // tm
    row = lambda width: pl.BlockSpec((1, tm, width), lambda i, j: (i, j, 0))
    widths = (512, 256, 512, 128, 128, 256, 256, 512, 512)
    return pl.pallas_call(
        _proj_kernel,
        grid=(b, sall // tm),
        in_specs=[row(d),
                  pl.BlockSpec((1, d), lambda i, j: (0, 0)),
                  pl.BlockSpec((1, 8, d), lambda i, j: (2 * i + (j >= nlat).astype(I32), 0, 0)),
                  pl.BlockSpec((tm, LANES), lambda i, j: (j, 0)),
                  pl.BlockSpec((tm, LANES), lambda i, j: (j, 0)),
                  pl.BlockSpec((d, IN_COLS), lambda i, j: (0, 0))],
        out_specs=[row(w) for w in widths] + [pl.BlockSpec((1, 512, tm), lambda i, j: (i, 0, j))],
        out_shape=([jax.ShapeDtypeStruct((b, sall, w), BF16) for w in widths]
                   + [jax.ShapeDtypeStruct((b, 512, sall), BF16)]),
        scratch_shapes=[pltpu.VMEM((tm, d), BF16)],
        compiler_params=_params(("arbitrary", "arbitrary")),
        name="projection",
    )(xc, g, modall, cos, sin, w_in)


def _half_mask(q, half):
    lane = lax.broadcasted_iota(I32, q.shape, 1)
    keep = (lane < HEAD_DIM) if half == 0 else (lane >= HEAD_DIM)
    return jnp.where(keep, q, jnp.zeros_like(q))


def _merge_halves(lo_part, hi_part):
    lane = lax.broadcasted_iota(I32, lo_part.shape, 1)
    return jnp.where(lane < HEAD_DIM, lo_part, hi_part)


def _gqa_rows(q_ref, g):
    return jnp.concatenate([q_ref[0, :, (2 * g) * LANES:(2 * g + 1) * LANES],
                            q_ref[0, :, (2 * g + 1) * LANES:(2 * g + 2) * LANES]], axis=0)


def _gqa_sink(sink_ref, g, tq):
    rowi = lax.broadcasted_iota(I32, (2 * tq, 1), 0)
    return jnp.where(rowi < tq, sink_ref[2 * g], sink_ref[2 * g + 1])


def _gqa_store(o_ref, g, o, tq):
    top, bot = o[:tq], o[tq:]
    if g == 0:
        chunk = _merge_halves(top, pltpu.roll(bot, HEAD_DIM, 1))
    else:
        chunk = _merge_halves(pltpu.roll(top, HEAD_DIM, 1), bot)
    o_ref[0, :, g * LANES:(g + 1) * LANES] = chunk.astype(BF16)


def _lam(lamv_ref, lam_init):
    v = lamv_ref[...]
    a = jnp.sum(v[0:1] * v[1:2], axis=-1, keepdims=True)
    b = jnp.sum(v[2:3] * v[3:4], axis=-1, keepdims=True)
    return jnp.exp(a) - jnp.exp(b) + lam_init


def _diff_post(o, subg, lam_init):
    return _rms(o, subg) * (1.0 - lam_init)


def _swa_kernel(sink_ref, q_ref, k_ref, v_ref, o_ref, *, s_lat, c_len):
    tq = ATT_TQ
    tk = tq + 2 * SWA_WINDOW
    q0 = pl.program_id(1) * tq
    ks = pl.multiple_of(jnp.clip(q0 - SWA_WINDOW, 0, s_lat - tk), SWA_WINDOW)
    kwin = k_ref[0, pl.ds(ks, tk), :]
    vwin = v_ref[0, pl.ds(ks, tk), :]
    kc = k_ref[0, s_lat:s_lat + c_len, :]
    vc = v_ref[0, s_lat:s_lat + c_len, :]
    qpos = q0 + lax.broadcasted_iota(I32, (tq, tk), 0)
    kpos = ks + lax.broadcasted_iota(I32, (tq, tk), 1)
    valid = jnp.abs(qpos - kpos) <= SWA_WINDOW
    valid = jnp.concatenate([valid, valid], axis=0)
    for g in range(2):
        q2 = _gqa_rows(q_ref, g)
        s_loc = jnp.where(valid, _dot_nt(q2, kwin), NEG_INF)
        s_ctx = _dot_nt(q2, kc)
        snk = _gqa_sink(sink_ref, g, tq)
        m = jnp.maximum(jnp.maximum(jnp.max(s_loc, axis=-1, keepdims=True),
                                    jnp.max(s_ctx, axis=-1, keepdims=True)), snk)
        e_loc = jnp.exp(s_loc - m)
        e_ctx = jnp.exp(s_ctx - m)
        den = (jnp.sum(e_loc, axis=-1, keepdims=True) + jnp.sum(e_ctx, axis=-1, keepdims=True)
               + jnp.exp(snk - m))
        o = (_dot(e_loc.astype(BF16), vwin) + _dot(e_ctx.astype(BF16), vc)) / den
        _gqa_store(o_ref, g, o, tq)


def _swa(sink, qa, ka, va, s_lat):
    b, sall, _ = qa.shape
    kv = pl.BlockSpec((1, sall, LANES), lambda i, j: (i, 0, 0))
    return pl.pallas_call(
        functools.partial(_swa_kernel, s_lat=s_lat, c_len=sall - s_lat),
        grid=(b, s_lat // ATT_TQ),
        in_specs=[pl.BlockSpec(memory_space=pltpu.SMEM),
                  pl.BlockSpec((1, ATT_TQ, 512), lambda i, j: (i, j, 0)), kv, kv],
        out_specs=pl.BlockSpec((1, ATT_TQ, 256), lambda i, j: (i, j, 0)),
        out_shape=jax.ShapeDtypeStruct((b, s_lat, 256), BF16),
        compiler_params=_params(("arbitrary", "arbitrary")),
        name="swa",
    )(sink, qa, ka, va)


def _na_kernel(q_ref, k_ref, v_ref, bm_ref, o_ref, *, s_lat, c_len):
    rows = s_lat // GRID_W
    nk = NA_KROWS * GRID_W
    r0 = pl.program_id(1) * NA_QROWS
    ks = pl.multiple_of(jnp.clip(r0 - NA_KH // 2, 0, rows - NA_KROWS) * GRID_W, GRID_W)
    for c in range(2):
        sl = slice(c * LANES, (c + 1) * LANES)
        kwin = k_ref[0, pl.ds(ks, nk), sl]
        vwin = v_ref[0, pl.ds(ks, nk), sl]
        kc = k_ref[0, s_lat:s_lat + c_len, sl]
        vc = v_ref[0, s_lat:s_lat + c_len, sl]
        q = q_ref[0, :, sl]
        outs = []
        for half in range(2):
            qm = _half_mask(q, half)
            s_loc = _dot_nt(qm, kwin) + bm_ref[0, 2 * c + half]
            s_ctx = _dot_nt(qm, kc)
            m = jnp.maximum(jnp.max(s_loc, axis=-1, keepdims=True), jnp.max(s_ctx, axis=-1, keepdims=True))
            e_loc = jnp.exp(s_loc - m)
            e_ctx = jnp.exp(s_ctx - m)
            den = jnp.sum(e_loc, axis=-1, keepdims=True) + jnp.sum(e_ctx, axis=-1, keepdims=True)
            outs.append((_dot(e_loc.astype(BF16), vwin) + _dot(e_ctx.astype(BF16), vc)) / den)
        o_ref[0, :, sl] = _merge_halves(outs[0], outs[1]).astype(BF16)


def _na_bias_mask(rpb, rows):
    nq = NA_QROWS * GRID_W
    nk = NA_KROWS * GRID_W
    col = np.arange(GRID_W)
    cstart = np.clip(col - NA_KW // 2, 0, GRID_W - NA_KW)
    col_ok = (col[None, :] >= cstart[:, None]) & (col[None, :] < cstart[:, None] + NA_KW)
    dc = np.clip(col[None, :] - col[:, None], -(NA_KW - 1), NA_KW - 1) + NA_KW - 1
    sel_c = np.where(col_ok[:, :, None], np.eye(2 * NA_KW - 1)[dc], 0.0)
    tabs = []
    for r0 in (0, NA_QROWS, rows - NA_QROWS):
        ksr = int(np.clip(r0 - NA_KH // 2, 0, rows - NA_KROWS))
        r = r0 + np.arange(NA_QROWS)
        kr = ksr + np.arange(NA_KROWS)
        rs = np.clip(r - NA_KH // 2, 0, rows - NA_KH)
        row_ok = (kr[None, :] >= rs[:, None]) & (kr[None, :] < rs[:, None] + NA_KH)
        dr = np.clip(kr[None, :] - r[:, None], -(NA_KH - 1), NA_KH - 1) + NA_KH - 1
        sel_r = np.where(row_ok[:, :, None], np.eye(2 * NA_KH - 1)[dr], 0.0)
        ok = (row_ok[:, None, :, None] & col_ok[None, :, None, :]).reshape(nq, nk)
        bias = jnp.einsum("hdc,rjd,qkc->hrqjk", rpb.astype(F32), jnp.asarray(sel_r, F32), jnp.asarray(sel_c, F32),
                          precision=lax.Precision.HIGHEST).reshape(N_HEADS, nq, nk)
        tabs.append(jnp.where(ok[None], bias, NEG_INF))
    return jnp.stack(tabs)


def _na(qb, kb, vb, bm, s_lat):
    b, sall, _ = qb.shape
    nq = NA_QROWS * GRID_W
    nsteps = s_lat // nq
    kv = pl.BlockSpec((1, sall, 256), lambda i, j: (i, 0, 0))

    def bm_map(i, j):
        return (jnp.where(j == 0, 0, jnp.where(j == nsteps - 1, 2, 1)), 0, 0, 0)

    return pl.pallas_call(
        functools.partial(_na_kernel, s_lat=s_lat, c_len=sall - s_lat),
        grid=(b, nsteps),
        in_specs=[pl.BlockSpec((1, nq, 256), lambda i, j: (i, j, 0)), kv, kv,
                  pl.BlockSpec((1, N_HEADS, nq, NA_KROWS * GRID_W), bm_map)],
        out_specs=pl.BlockSpec((1, nq, 256), lambda i, j: (i, j, 0)),
        out_shape=jax.ShapeDtypeStruct((b, s_lat, 256), BF16),
        compiler_params=_params(("arbitrary", "arbitrary")),
        name="na2d",
    )(qb, kb, vb, bm)


def _diff_kernel(lamv_ref, subg_ref, q_ref, k_ref, vt_ref, o_ref, e_scr, m_scr, *, n_keys, lam_init):
    tq = ATT_TQ
    chunks = [(c0, min(DIFF_KCHUNK, n_keys - c0)) for c0 in range(0, n_keys, DIFF_KCHUNK)]
    lam = _lam(lamv_ref, lam_init)
    stats = []
    for hh in range(DIFF_HEADS_PER_STEP):
        sl = slice(hh * LANES, (hh + 1) * LANES)
        q = q_ref[0, :, sl]
        q12 = jnp.concatenate([_half_mask(q, 0), _half_mask(q, 1)], axis=0)
        m = jnp.full((1, 2 * tq), NEG_INF, F32)
        den = jnp.zeros((1, 2 * tq), F32)
        for i, (c0, n) in enumerate(chunks):
            s = _dot_nt(k_ref[0, c0:c0 + n, sl], q12)
            m_new = jnp.maximum(m, jnp.max(s, axis=0, keepdims=True))
            e = jnp.exp2(s - m_new)
            e_scr[hh, c0:c0 + n, :] = e
            m_scr[hh, i:i + 1, :] = m_new
            den = den * jnp.exp2(m - m_new) + jnp.sum(e, axis=0, keepdims=True)
            m = m_new
        stats.append((m, den))
    for hh in range(DIFF_HEADS_PER_STEP):
        sl = slice(hh * LANES, (hh + 1) * LANES)
        m, den = stats[hh]
        acc = jnp.zeros((LANES, tq), F32)
        for i, (c0, n) in enumerate(chunks):
            f = jnp.exp2(m_scr[hh, i:i + 1, :] - m) / den
            p = e_scr[hh, c0:c0 + n, :tq] * f[:, :tq] - e_scr[hh, c0:c0 + n, tq:] * (lam * f[:, tq:])
            acc = acc + _dot(vt_ref[0, sl, c0:c0 + n], p.astype(BF16))
        o_ref[0, :, sl] = _diff_post(acc.T, subg_ref[...], lam_init).astype(BF16)


def _diff(lamv, subg, qd, kd, vdt, s_lat, lam_init):
    b, sall, _ = qd.shape
    nchunk = -(-sall // DIFF_KCHUNK)
    hps = DIFF_HEADS_PER_STEP
    wid = hps * LANES
    return pl.pallas_call(
        functools.partial(_diff_kernel, n_keys=sall, lam_init=lam_init),
        grid=(b, N_HEADS // hps, s_lat // ATT_TQ),
        in_specs=[pl.BlockSpec((8, LANES), lambda i, h, j: (0, 0)),
                  pl.BlockSpec((1, LANES), lambda i, h, j: (0, 0)),
                  pl.BlockSpec((1, ATT_TQ, wid), lambda i, h, j: (i, j, h)),
                  pl.BlockSpec((1, sall, wid), lambda i, h, j: (i, 0, h)),
                  pl.BlockSpec((1, wid, sall), lambda i, h, j: (i, h, 0))],
        out_specs=pl.BlockSpec((1, ATT_TQ, wid), lambda i, h, j: (i, j, h)),
        out_shape=jax.ShapeDtypeStruct((b, s_lat, 512), BF16),
        scratch_shapes=[pltpu.VMEM((hps, sall, 2 * ATT_TQ), F32),
                        pltpu.VMEM((hps, -(-nchunk // 8) * 8, 2 * ATT_TQ), F32)],
        compiler_params=_params(("arbitrary", "arbitrary", "arbitrary")),
        name="diff_attn",
    )(lamv, subg, qd, kd, vdt)


def _ctx_kernel(sink_ref, lamv_ref, subg_ref, qa_ref, ka_ref, va_ref, qb_ref, kb_ref, vb_ref,
                qd_ref, kd_ref, vd_ref, oa_ref, ob_ref, od_ref, *, lam_init):
    cl = qa_ref.shape[1]
    ka = ka_ref[0]
    va = va_ref[0]
    for g in range(2):
        q2 = _gqa_rows(qa_ref, g)
        s = _dot_nt(q2, ka)
        snk = _gqa_sink(sink_ref, g, cl)
        m = jnp.maximum(jnp.max(s, axis=-1, keepdims=True), snk)
        e = jnp.exp(s - m)
        den = jnp.sum(e, axis=-1, keepdims=True) + jnp.exp(snk - m)
        _gqa_store(oa_ref, g, _dot(e.astype(BF16), va) / den, cl)
    for c in range(2):
        sl = slice(c * LANES, (c + 1) * LANES)
        outs = []
        for half in range(2):
            s = _dot_nt(_half_mask(qb_ref[0, :, sl], half), kb_ref[0, :, sl])
            e = jnp.exp(s - jnp.max(s, axis=-1, keepdims=True))
            outs.append(_dot(e.astype(BF16), vb_ref[0, :, sl]) / jnp.sum(e, axis=-1, keepdims=True))
        ob_ref[0, :, sl] = _merge_halves(outs[0], outs[1]).astype(BF16)
    lam = _lam(lamv_ref, lam_init)
    for h in range(N_HEADS):
        sl = slice(h * LANES, (h + 1) * LANES)
        q = qd_ref[0, :, sl]
        q12 = jnp.concatenate([_half_mask(q, 0), _half_mask(q, 1)], axis=0)
        s = _dot_nt(q12, kd_ref[0, :, sl])
        e = jnp.exp2(s - jnp.max(s, axis=-1, keepdims=True))
        den = jnp.sum(e, axis=-1, keepdims=True)
        p = e[:cl] * (1.0 / den[:cl]) - e[cl:] * (lam / den[cl:])
        o = _dot(p.astype(BF16), vd_ref[0, :, sl])
        od_ref[0, :, sl] = _diff_post(o, subg_ref[...], lam_init).astype(BF16)


def _ctx_attention(sink, lamv, subg, qa, ka, va, qb, kb, vb, qd, kd, vd, s_lat, lam_init):
    b, sall, _ = qa.shape
    cl = sall - s_lat
    blk = s_lat // cl
    row = lambda width: pl.BlockSpec((1, cl, width), lambda i: (i, blk, 0))
    out = lambda width: pl.BlockSpec((1, cl, width), lambda i: (i, 0, 0))
    return pl.pallas_call(
        functools.partial(_ctx_kernel, lam_init=lam_init),
        grid=(b,),
        in_specs=[pl.BlockSpec(memory_space=pltpu.SMEM),
                  pl.BlockSpec((8, LANES), lambda i: (0, 0)),
                  pl.BlockSpec((1, LANES), lambda i: (0, 0)),
                  row(512), row(128), row(128), row(256), row(256), row(256), row(512), row(512), row(512)],
        out_specs=[out(256), out(256), out(512)],
        out_shape=[jax.ShapeDtypeStruct((b, cl, w), BF16) for w in (256, 256, 512)],
        compiler_params=_params(("arbitrary",)),
        name="ctx_attention",
    )(sink, lamv, subg, qa, ka, va, qb, kb, vb, qd, kd, vd)


def _out_kernel(x_ref, w_ref, mod_ref, *refs, nlat):
    xo_ref = refs[-1]

    def emit(oa_ref, ob_ref, od_ref):
        attn = (_dot(oa_ref[0], w_ref[0:256, :]) + _dot(ob_ref[0], w_ref[256:512, :])
                + _dot(od_ref[0], w_ref[512:1024, :]))
        xo_ref[0] = x_ref[0] + mod_ref[0, 2:3, :] * attn

    if len(refs) == 4:
        emit(*refs[:3])
    else:
        is_ctx = pl.program_id(1) >= nlat
        pl.when(jnp.logical_not(is_ctx))(lambda: emit(*refs[:3]))
        pl.when(is_ctx)(lambda: emit(*refs[3:6]))


def _out_projection(xc, lat, ctx, w_out, modall, s_lat):
    b, sall, d = xc.shape
    tm = TOK_TILE
    nlat = s_lat // tm
    n_rows = s_lat if ctx is None else sall
    row = lambda width: pl.BlockSpec((1, tm, width), lambda i, j: (i, j, 0))
    lat_row = lambda width: pl.BlockSpec((1, tm, width), lambda i, j: (i, jnp.minimum(j, nlat - 1), 0))
    ctx_row = lambda width: pl.BlockSpec((1, tm, width), lambda i, j: (i, 0, 0))
    widths = (256, 256, 512)
    specs = [lat_row(w) for w in widths] + ([] if ctx is None else [ctx_row(w) for w in widths])
    return pl.pallas_call(
        functools.partial(_out_kernel, nlat=nlat),
        grid=(b, n_rows // tm),
        in_specs=[row(d),
                  pl.BlockSpec((MIX_WIDTH, d), lambda i, j: (0, 0)),
                  pl.BlockSpec((1, 8, d), lambda i, j: (2 * i + (j >= nlat).astype(I32), 0, 0))] + specs,
        out_specs=row(d),
        out_shape=jax.ShapeDtypeStruct((b, n_rows, d), F32),
        compiler_params=_params(("arbitrary", "arbitrary")),
        name="out_projection",
    )(xc, w_out, modall, *lat, *(() if ctx is None else ctx))


def _first_argmax(v, iota, n):
    m = jnp.max(v, axis=0, keepdims=True)
    ix = jnp.min(jnp.where(v == m, iota, float(n)), axis=0, keepdims=True)
    return m, ix


def _router_kernel(x_ref, g_ref, mod_ref, whi_ref, wlo_ref, bias_ref, tri_ref, ltri_ref,
                   w_ref, slot_ref, meta_ref, cnt_ref):
    @pl.when(pl.program_id(0) == 0)
    def _():
        cnt_ref[...] = jnp.zeros_like(cnt_ref)

    hf = _norm_mod(x_ref[...], g_ref[...], mod_ref[0, 3:4, :], mod_ref[0, 4:5, :])
    tm = hf.shape[0]
    h_hi, h_lo = _split_bf16(hf)
    whi = whi_ref[...]
    logits = _dot_nt(whi, h_hi) + _dot_nt(whi, h_lo) + _dot_nt(wlo_ref[...], h_hi)
    scores = _sigmoid(logits)
    biased = scores + bias_ref[...]
    gsz = N_EXPERTS // N_GROUPS
    iota_g = lax.broadcasted_iota(I32, (gsz, tm), 0).astype(F32)
    gscore = []
    for g in range(N_GROUPS):
        v = biased[g * gsz:(g + 1) * gsz]
        m1, i1 = _first_argmax(v, iota_g, gsz)
        m2 = jnp.max(jnp.where(iota_g == i1, -jnp.inf, v), axis=0, keepdims=True)
        gscore.append(m1 + m2)
    cur = jnp.concatenate(gscore, axis=0)
    iota_n = lax.broadcasted_iota(I32, (N_GROUPS, tm), 0).astype(F32)
    gsel = jnp.zeros((N_GROUPS, tm), F32)
    for _ in range(TOPK_GROUPS):
        _, ix = _first_argmax(cur, iota_n, N_GROUPS)
        hit = iota_n == ix
        gsel = jnp.where(hit, 1.0, gsel)
        cur = jnp.where(hit, -jnp.inf, cur)
    masked = jnp.concatenate(
        [jnp.where(gsel[g:g + 1] > 0.5, biased[g * gsz:(g + 1) * gsz], -jnp.inf) for g in range(N_GROUPS)],
        axis=0)
    iota_e = lax.broadcasted_iota(I32, (N_EXPERTS, tm), 0).astype(F32)
    onehot = jnp.zeros((N_EXPERTS, tm), F32)
    idxs, ws = [], []
    for _ in range(TOP_K):
        _, ix = _first_argmax(masked, iota_e, N_EXPERTS)
        hit = iota_e == ix
        ws.append(jnp.sum(jnp.where(hit, scores, 0.0), axis=0, keepdims=True))
        masked = jnp.where(hit, -jnp.inf, masked)
        onehot = jnp.where(hit, 1.0, onehot)
        idxs.append(ix)
    base = cnt_ref[...]
    rank = _dot(onehot.astype(BF16), tri_ref[...])
    n_col = jnp.sum(onehot, axis=1, keepdims=True)
    n_pad = jnp.floor((n_col + (ROW_CHUNK - 1)) * (1.0 / ROW_CHUNK)) * ROW_CHUNK
    start = _dot(ltri_ref[...], jnp.broadcast_to(n_pad, (N_EXPERTS, LANES)).astype(BF16))
    wsum = ws[0]
    for wk in ws[1:]:
        wsum = wsum + wk
    for k in range(TOP_K):
        slot = jnp.sum(jnp.where(iota_e == idxs[k], rank + start[:, 0:1], 0.0), axis=0, keepdims=True)
        slot_ref[0, k:k + 1, :] = slot.astype(I32)
        w_ref[0, k:k + 1, :] = ws[k] / wsum * ROUTED_SCALE
    lane = lax.broadcasted_iota(I32, (N_EXPERTS, LANES), 1)
    meta_ref[0] = jnp.where(lane == 0, n_pad, jnp.where(lane == 1, base, start))
    cnt_ref[...] = base + n_pad


def _router(xflat, g, modall, whi, wlo, bias, tri, ltri, tiles_per_batch, nlat):
    t, d = xflat.shape
    tm = TOK_TILE
    nt = t // tm
    per_tok = pl.BlockSpec((1, TOP_K, tm), lambda i: (i, 0, 0))

    def mod_map(i):
        return (2 * (i // tiles_per_batch) + ((i % tiles_per_batch) >= nlat).astype(I32), 0, 0)

    return pl.pallas_call(
        _router_kernel,
        grid=(nt,),
        in_specs=[pl.BlockSpec((tm, d), lambda i: (i, 0)),
                  pl.BlockSpec((1, d), lambda i: (0, 0)),
                  pl.BlockSpec((1, 8, d), mod_map),
                  pl.BlockSpec((N_EXPERTS, d), lambda i: (0, 0)),
                  pl.BlockSpec((N_EXPERTS, d), lambda i: (0, 0)),
                  pl.BlockSpec((N_EXPERTS, 1), lambda i: (0, 0)),
                  pl.BlockSpec((tm, tm), lambda i: (0, 0)),
                  pl.BlockSpec((N_EXPERTS, N_EXPERTS), lambda i: (0, 0))],
        out_specs=[per_tok, per_tok,
                   pl.BlockSpec((1, N_EXPERTS, LANES), lambda i: (i, 0, 0)),
                   pl.BlockSpec((N_EXPERTS, 1), lambda i: (0, 0))],
        out_shape=[jax.ShapeDtypeStruct((nt, TOP_K, tm), F32),
                   jax.ShapeDtypeStruct((nt, TOP_K, tm), I32),
                   jax.ShapeDtypeStruct((nt, N_EXPERTS, LANES), F32),
                   jax.ShapeDtypeStruct((N_EXPERTS, 1), F32)],
        compiler_params=_params(("arbitrary",)),
        name="router",
    )(xflat, g, modall, whi, wlo, bias, tri, ltri)


def _pack_bf16_pairs(v):
    half = v.shape[1] // 2
    vb = v.astype(BF16).astype(F32)
    lo = lax.shift_right_logical(lax.bitcast_convert_type(vb[:, :half], U32), jnp.uint32(16))
    hi = lax.bitcast_convert_type(vb[:, half:], U32) & jnp.uint32(0xFFFF0000)
    return lo | hi


def _unpack_bf16_pairs(u):
    lo = lax.bitcast_convert_type(lax.shift_left(u, jnp.uint32(16)), F32).astype(BF16)
    hi = lax.bitcast_convert_type(u & jnp.uint32(0xFFFF0000), F32).astype(BF16)
    return lo, hi


def _expert_chunk_copies(meta_ref, make_copy):
    def per_expert(e, total):
        nch = lax.shift_right_logical(meta_ref[0, 0, e], ROW_CHUNK.bit_length() - 1)

        def per_chunk(c, carry):
            make_copy(e, c * ROW_CHUNK).start()
            return carry

        lax.fori_loop(0, nch, per_chunk, 0)
        return total + nch

    return lax.fori_loop(0, N_EXPERTS, per_expert, 0)


def _tile_slots(tm):
    return TOP_K * tm + ROW_CHUNK * N_EXPERTS


def _slots_used(meta_ref):
    last = N_EXPERTS - 1
    return meta_ref[0, 2, last] + meta_ref[0, 0, last]


def _dispatch_kernel(meta_ref, pstart_ref, x_ref, g_ref, mod_ref, slot_ref, xs_ref, xc_scr, sem):
    hb = _norm_mod(x_ref[...], g_ref[...], mod_ref[0, 3:4, :], mod_ref[0, 4:5, :]).astype(BF16)
    tm = hb.shape[0]
    slot = slot_ref[0]

    def select_rows(r, carry):
        r0 = pl.multiple_of(r * tm, tm)
        row = lax.broadcasted_iota(I32, (tm, tm), 0) + r0
        sel = jnp.zeros((tm, tm), F32)
        for k in range(TOP_K):
            sel = jnp.where(row == slot[k:k + 1, :], 1.0, sel)
        xc_scr[pl.ds(r0, tm), :] = _pack_bf16_pairs(_dot(sel.astype(BF16), hb))
        return carry

    lax.fori_loop(0, (_slots_used(meta_ref) + tm - 1) // tm, select_rows, 0)

    def make_copy(e, off):
        src = pl.multiple_of(meta_ref[0, 2, e] + off, ROW_CHUNK)
        dst = pl.multiple_of(pstart_ref[e] + meta_ref[0, 1, e] + off, ROW_CHUNK)
        return pltpu.make_async_copy(xc_scr.at[pl.ds(src, ROW_CHUNK)], xs_ref.at[pl.ds(dst, ROW_CHUNK)], sem)

    total = _expert_chunk_copies(meta_ref, make_copy)

    def wait(j, carry):
        pltpu.make_async_copy(xc_scr.at[pl.ds(0, ROW_CHUNK)], xs_ref.at[pl.ds(0, ROW_CHUNK)], sem).wait()
        return carry

    lax.fori_loop(0, total, wait, 0)


def _dispatch(meta, pstart, slot, xflat, g, modall, n_rows, tiles_per_batch, nlat):
    t, d = xflat.shape
    tm = TOK_TILE
    nt = t // tm

    def mod_map(i):
        return (2 * (i // tiles_per_batch) + ((i % tiles_per_batch) >= nlat).astype(I32), 0, 0)

    return pl.pallas_call(
        _dispatch_kernel,
        grid=(nt,),
        in_specs=[pl.BlockSpec((1, 3, N_EXPERTS), lambda i: (i, 0, 0), memory_space=pltpu.SMEM),
                  pl.BlockSpec(memory_space=pltpu.SMEM),
                  pl.BlockSpec((tm, d), lambda i: (i, 0)),
                  pl.BlockSpec((1, d), lambda i: (0, 0)),
                  pl.BlockSpec((1, 8, d), mod_map),
                  pl.BlockSpec((1, TOP_K, tm), lambda i: (i, 0, 0))],
        out_specs=pl.BlockSpec(memory_space=pl.ANY),
        out_shape=jax.ShapeDtypeStruct((n_rows, d // 2), U32),
        scratch_shapes=[pltpu.VMEM((_tile_slots(tm), d // 2), U32), pltpu.SemaphoreType.DMA(())],
        compiler_params=_params(("arbitrary",)),
        name="moe_dispatch",
    )(meta, pstart, xflat, g, modall, slot)


def _expert_kernel(blk_e_ref, nvalid_ref, nact_ref, xs_ref, wg_ref, wu_ref, wd_ref, ys_ref, wg_s, wu_s, wd_s):
    b = pl.program_id(0)
    changed = jnp.logical_or(b == 0, blk_e_ref[b] != blk_e_ref[jnp.maximum(b - 1, 0)])

    @pl.when(changed)
    def _():
        wg_s[...] = wg_ref[0, 0].astype(BF16)
        wu_s[...] = wu_ref[0, 0].astype(BF16)
        wd_s[...] = wd_ref[0, 0].astype(BF16)

    @pl.when(b < nact_ref[0])
    def _():
        half = xs_ref.shape[1]
        row = lax.broadcasted_iota(I32, xs_ref.shape, 0)
        xs = jnp.where(row < nvalid_ref[b], xs_ref[...], jnp.zeros(xs_ref.shape, U32))
        x_lo, x_hi = _unpack_bf16_pairs(xs)
        gate = _dot(x_lo, wg_s[0:half, :]) + _dot(x_hi, wg_s[half:, :])
        up = _dot(x_lo, wu_s[0:half, :]) + _dot(x_hi, wu_s[half:, :])
        mid = (gate * _sigmoid(gate)) * up
        ys_ref[...] = _pack_bf16_pairs(_dot(mid.astype(BF16), wd_s[...]))

    @pl.when(b >= nact_ref[0])
    def _():
        ys_ref[...] = jnp.zeros_like(ys_ref)


def _experts(blk_e, nvalid, nact, xs, w_gate, w_up, w_down, layer):
    n_rows, half = xs.shape
    _, _, d, f = w_gate.shape
    n_blk = n_rows // MOE_BLOCK
    grid_spec = pltpu.PrefetchScalarGridSpec(
        num_scalar_prefetch=3,
        grid=(n_blk,),
        in_specs=[pl.BlockSpec((MOE_BLOCK, half), lambda i, be, nv, na: (jnp.minimum(i, na[0] - 1), 0)),
                  pl.BlockSpec((1, 1, d, f), lambda i, be, nv, na: (layer, be[i], 0, 0)),
                  pl.BlockSpec((1, 1, d, f), lambda i, be, nv, na: (layer, be[i], 0, 0)),
                  pl.BlockSpec((1, 1, f, d), lambda i, be, nv, na: (layer, be[i], 0, 0))],
        out_specs=pl.BlockSpec((MOE_BLOCK, half), lambda i, be, nv, na: (i, 0)),
        scratch_shapes=[pltpu.VMEM((d, f), BF16), pltpu.VMEM((d, f), BF16), pltpu.VMEM((f, d), BF16)],
    )
    return pl.pallas_call(
        _expert_kernel,
        grid_spec=grid_spec,
        out_shape=jax.ShapeDtypeStruct((n_rows, half), U32),
        compiler_params=_params(("arbitrary",)),
        name="moe_experts",
    )(blk_e, nvalid, nact, xs, w_gate, w_up, w_down)


def _combine_kernel(meta_ref, pstart_ref, x_ref, g_ref, mod_ref, spos_ref, wt_ref, wsg_ref, wsu_ref, wsd_ref,
                    gfin_ref, ys_ref, o_ref, y_scr, acc_scr, sem, *, final_norm):
    tm, d = x_ref.shape
    half = d // 2

    @pl.when(pl.program_id(0) == 0)
    def _():
        y_scr[...] = jnp.zeros_like(y_scr)

    def make_copy(e, off):
        src = pl.multiple_of(pstart_ref[e] + meta_ref[0, 1, e] + off, ROW_CHUNK)
        dst = pl.multiple_of(meta_ref[0, 2, e] + off, ROW_CHUNK)
        return pltpu.make_async_copy(ys_ref.at[pl.ds(src, ROW_CHUNK)], y_scr.at[pl.ds(dst, ROW_CHUNK)], sem)

    total = _expert_chunk_copies(meta_ref, make_copy)

    x = x_ref[...]
    hb = _norm_mod(x, g_ref[...], mod_ref[0, 3:4, :], mod_ref[0, 4:5, :]).astype(BF16)
    gate = _dot(hb, wsg_ref[...])
    mid = (gate * _sigmoid(gate)) * _dot(hb, wsu_ref[...])
    acc_scr[...] = _dot(mid.astype(BF16), wsd_ref[...])

    def wait(j, carry):
        pltpu.make_async_copy(ys_ref.at[pl.ds(0, ROW_CHUNK)], y_scr.at[pl.ds(0, ROW_CHUNK)], sem).wait()
        return carry

    lax.fori_loop(0, total, wait, 0)

    spos = spos_ref[0]
    wt = wt_ref[0]
    n_used = _slots_used(meta_ref)

    def chunk(j, carry):
        r0 = pl.multiple_of(j * COMBINE_KCHUNK, COMBINE_KCHUNK)
        col = lax.broadcasted_iota(I32, (tm, COMBINE_KCHUNK), 1) + r0
        p = jnp.zeros((tm, COMBINE_KCHUNK), F32)
        for k in range(TOP_K):
            p = jnp.where(col == spos[:, k:k + 1], wt[:, k:k + 1], p)
        pb = p.astype(BF16)
        y_lo, y_hi = _unpack_bf16_pairs(y_scr[pl.ds(r0, COMBINE_KCHUNK), :])
        acc_scr[:, :half] += _dot(pb, y_lo)
        acc_scr[:, half:] += _dot(pb, y_hi)
        return carry

    lax.fori_loop(0, (n_used + COMBINE_KCHUNK - 1) // COMBINE_KCHUNK, chunk, 0)
    out = x + mod_ref[0, 5:6, :] * acc_scr[...]
    if final_norm:
        out = _rms(out, gfin_ref[...])
    o_ref[...] = out


def _combine(meta, pstart, spos_t, wt, xflat, g, modall, wsg, wsu, wsd, gfin, ys, tiles_per_batch, nlat, final_norm):
    t, d = xflat.shape
    f = wsg.shape[1]
    tm = TOK_TILE
    nt = t // tm
    y_rows = _tile_slots(tm)
    assert y_rows % COMBINE_KCHUNK == 0

    def mod_map(i):
        return (2 * (i // tiles_per_batch) + ((i % tiles_per_batch) >= nlat).astype(I32), 0, 0)

    return pl.pallas_call(
        functools.partial(_combine_kernel, final_norm=final_norm),
        grid=(nt,),
        in_specs=[pl.BlockSpec((1, 3, N_EXPERTS), lambda i: (i, 0, 0), memory_space=pltpu.SMEM),
                  pl.BlockSpec(memory_space=pltpu.SMEM),
                  pl.BlockSpec((tm, d), lambda i: (i, 0)),
                  pl.BlockSpec((1, d), lambda i: (0, 0)),
                  pl.BlockSpec((1, 8, d), mod_map),
                  pl.BlockSpec((1, tm, TOP_K), lambda i: (i, 0, 0)),
                  pl.BlockSpec((1, tm, TOP_K), lambda i: (i, 0, 0)),
                  pl.BlockSpec((d, f), lambda i: (0, 0)),
                  pl.BlockSpec((d, f), lambda i: (0, 0)),
                  pl.BlockSpec((f, d), lambda i: (0, 0)),
                  pl.BlockSpec((1, d), lambda i: (0, 0)),
                  pl.BlockSpec(memory_space=pl.ANY)],
        out_specs=pl.BlockSpec((tm, d), lambda i: (i, 0)),
        out_shape=jax.ShapeDtypeStruct((t, d), F32),
        scratch_shapes=[pltpu.VMEM((y_rows, d // 2), U32), pltpu.VMEM((tm, d), F32), pltpu.SemaphoreType.DMA(())],
        compiler_params=_params(("arbitrary",)),
        name="moe_combine",
    )(meta, pstart, xflat, g, modall, spos_t, wt, wsg, wsu, wsd, gfin, ys)


def _rope_tables(s_lat, c_len):
    n_freq = HEAD_DIM // 4
    inv = 1.0 / (ROPE_BASE ** (jnp.arange(n_freq, dtype=F32) / n_freq))
    t = jnp.arange(s_lat)
    pos = jnp.stack([(t // GRID_W).astype(F32), (t % GRID_W).astype(F32)], axis=1)
    lane = np.arange(LANES)
    ang = pos[:, (lane % HEAD_DIM) // 32] * inv[lane % 16][None, :]
    sign = jnp.asarray(np.where(lane % 32 < 16, -1.0, 1.0), F32)
    cos = jnp.concatenate([jnp.cos(ang), jnp.ones((c_len, LANES), F32)], axis=0)
    sin = jnp.concatenate([jnp.sin(ang) * sign, jnp.zeros((c_len, LANES), F32)], axis=0)
    return cos, sin


def _moe(xflat, lp, modall, gfin, tiles_per_batch, nlat, final_norm):
    t, d = xflat.shape
    nt = t // TOK_TILE
    g = lp["g_ffn"].reshape(1, d)
    w_r = lp["w_router"].T
    whi = w_r.astype(BF16)
    wlo = (w_r - whi.astype(F32)).astype(BF16)
    tri = jnp.asarray(np.triu(np.ones((TOK_TILE, TOK_TILE), np.float32), 1), BF16)
    ltri = jnp.asarray(np.tril(np.ones((N_EXPERTS, N_EXPERTS), np.float32), -1), BF16)
    w, slot, meta, cnt = _router(xflat, g, modall, whi, wlo, lp["router_bias"].reshape(N_EXPERTS, 1),
                                 tri, ltri, tiles_per_batch, nlat)
    meta = jnp.transpose(meta[:, :, :3].astype(I32), (0, 2, 1))
    rows_e = cnt[:, 0].astype(I32)
    seg = (rows_e + MOE_BLOCK - 1) // MOE_BLOCK * MOE_BLOCK
    pend = jnp.cumsum(seg)
    pstart = (pend - seg).astype(I32)
    n_blk = (t * TOP_K + ROW_CHUNK * N_EXPERTS * nt) // MOE_BLOCK + N_EXPERTS
    blk_row0 = jnp.arange(n_blk, dtype=I32) * MOE_BLOCK
    blk_e = jnp.minimum(jnp.sum((pend[None, :] <= blk_row0[:, None]).astype(I32), axis=1), N_EXPERTS - 1)
    of_blk = blk_e[:, None] == jnp.arange(N_EXPERTS, dtype=I32)[None, :]
    seg_end = jnp.sum(jnp.where(of_blk, (pstart + rows_e)[None, :], 0), axis=1)
    nvalid = jnp.clip(seg_end - blk_row0, 0, MOE_BLOCK).astype(I32)
    nact = (pend[-1] // MOE_BLOCK).astype(I32).reshape(1)
    xs = _dispatch(meta, pstart, slot, xflat, g, modall, n_blk * MOE_BLOCK, tiles_per_batch, nlat)
    ys = _experts(blk_e, nvalid, nact, xs, lp["w_gate"], lp["w_up"], lp["w_down"], lp["layer"])
    return _combine(meta, pstart, jnp.transpose(slot, (0, 2, 1)), jnp.transpose(w, (0, 2, 1)), xflat, g, modall,
                    lp["ws_gate"].astype(BF16), lp["ws_up"].astype(BF16), lp["ws_down"].astype(BF16), gfin, ys,
                    tiles_per_batch, nlat, final_norm)


def _layer(xc, cs, cos, sin, lp, layer_idx, last, s_lat, gfin):
    b, sall, d = xc.shape
    c_len = sall - s_lat
    lam_init = 0.8 - 0.6 * math.exp(-0.3 * layer_idx)
    mod = _modulation(cs, lp["w_mod"], lp["b_mod"])
    mod_lat = mod[:b].reshape(b, 1, 6, d)
    mod_ctx = jnp.broadcast_to(mod[b].reshape(1, 1, 6, d), (b, 1, 6, d))
    modall = jnp.concatenate([mod_lat, mod_ctx], axis=1)
    modall = jnp.pad(modall, ((0, 0), (0, 0), (0, 2), (0, 0))).reshape(2 * b, 8, d)

    qa, qb, qd, ka, va, kb, vb, kd, vd, vdt = _projection(
        xc, lp["g_mix"].reshape(1, d), modall, cos, sin, lp["w_in"].astype(BF16), s_lat)
    sink = lp["attn_sink"].astype(F32)
    lamv = jnp.zeros((8, LANES), F32).at[0:4, 0:HEAD_DIM].set(
        jnp.stack([lp["lam_q1"], lp["lam_k1"], lp["lam_q2"], lp["lam_k2"]]).astype(F32))
    subg = lp["subln_g"].reshape(1, LANES).astype(F32)
    oa = _swa(sink, qa, ka, va, s_lat)
    ob = _na(qb, kb, vb, _na_bias_mask(lp["na_rpb"], s_lat // GRID_W), s_lat)
    od = _diff(lamv, subg, qd, kd, vdt, s_lat, lam_init)
    ctx_out = None
    if not last:
        ctx_out = _ctx_attention(sink, lamv, subg, qa, ka, va, qb, kb, vb, qd, kd, vd, s_lat, lam_init)
    n_rows = s_lat if last else sall
    x2 = _out_projection(xc, (oa, ob, od), ctx_out, lp["w_out"].astype(BF16), modall, s_lat)
    tiles_per_batch = n_rows // TOK_TILE
    y = _moe(x2.reshape(b * n_rows, d), lp, modall, gfin, tiles_per_batch, s_lat // TOK_TILE, last)
    return y.reshape(b, n_rows, d)


_LAYER_KEYS = ("w_mod", "b_mod", "g_mix", "g_ffn", "w_in", "w_out", "attn_sink", "na_rpb", "lam_q1", "lam_k1",
               "lam_q2", "lam_k2", "subln_g", "w_router", "router_bias", "w_gate", "w_up", "w_down",
               "ws_gate", "ws_up", "ws_down")


def kernel(x, c, ctx, c_ctx, w_mod, b_mod, g_mix, g_ffn, w_in, w_out, attn_sink, na_rpb, lam_q1, lam_k1, lam_q2,
           lam_k2, subln_g, w_router, router_bias, w_gate, w_up, w_down, ws_gate, ws_up, ws_down, g_final):
    stacked = dict(zip(_LAYER_KEYS, (w_mod, b_mod, g_mix, g_ffn, w_in, w_out, attn_sink, na_rpb, lam_q1, lam_k1,
                                     lam_q2, lam_k2, subln_g, w_router, router_bias, w_gate, w_up, w_down,
                                     ws_gate, ws_up, ws_down)))
    b, s_lat, d = x.shape
    c_len = ctx.shape[1]
    depth = w_mod.shape[0]
    assert s_lat % (NA_QROWS * GRID_W) == 0 and s_lat % c_len == 0 and c_len == TOK_TILE
    cs = jnp.zeros((16, d), F32).at[:b].set(c).at[b].set(c_ctx)
    cos, sin = _rope_tables(s_lat, c_len)
    xc = jnp.concatenate([x, ctx], axis=1)
    gfin = g_final.reshape(1, d)
    for i in range(depth):
        big = ("w_gate", "w_up", "w_down")
        lp = {k: (v if k in big else v[i]) for k, v in stacked.items()}
        lp["layer"] = i
        xc = _layer(xc, cs, cos, sin, lp, i, i == depth - 1, s_lat, gfin)
    return xc
```

```python
import functools
import math

import numpy as np
import jax
import jax.numpy as jnp
from jax import lax
from jax.experimental import pallas as pl
from jax.experimental.pallas import tpu as pltpu

F32 = jnp.float32
BF16 = jnp.bfloat16
I32 = jnp.int32
U32 = jnp.uint32

GRID_W = 64
HEAD_DIM = 64
ROPE_BASE = 10000.0
SWA_WINDOW = 128
NA_KH = 8
NA_KW = 16
N_HEADS = 4
N_EXPERTS = 256
TOP_K = 8
N_GROUPS = 8
TOPK_GROUPS = 4
ROUTED_SCALE = 2.5
MOE_BLOCK = 256
ROW_CHUNK = 8
COMBINE_KCHUNK = 512
EPS = 1e-6
NEG_INF = -1e30
LOG2E = math.log2(math.e)
Q_COLS = 1024
IN_COLS = 2816
MIX_WIDTH = 1024

LANES = 128
TOK_TILE = 256
ATT_TQ = 256
NA_QROWS = 4
NA_KROWS = NA_QROWS + NA_KH - 1
DIFF_KCHUNK = 256
DIFF_HEADS_PER_STEP = 2
VMEM_LIMIT = 48 * 1024 * 1024


def _params(sem):
    return pltpu.CompilerParams(dimension_semantics=sem, vmem_limit_bytes=VMEM_LIMIT)


def _dot(a, b):
    return jnp.dot(a, b, preferred_element_type=F32)


def _dot_nt(a, b):
    return lax.dot_general(a, b, (((1,), (1,)), ((), ())), preferred_element_type=F32)


def _split_bf16(a):
    hi = a.astype(BF16)
    lo = (a - hi.astype(F32)).astype(BF16)
    return hi, lo


def _sigmoid(x):
    return 1.0 / (1.0 + jnp.exp(-x))


def _rms(x, g):
    return x * lax.rsqrt(jnp.mean(x * x, axis=-1, keepdims=True) + EPS) * g


def _norm_mod(x, g, shift, scale):
    return _rms(x, g) * (1.0 + scale) + shift


def _mod_kernel(c_ref, w_ref, b_ref, o_ref):
    c = c_ref[...]
    a_hi, a_lo = _split_bf16(c * _sigmoid(c))
    w_hi, w_lo = _split_bf16(w_ref[...])
    o_ref[...] = _dot(a_hi, w_hi) + _dot(a_hi, w_lo) + _dot(a_lo, w_hi) + b_ref[...]


def _modulation(cs, w_mod, b_mod):
    n, d = cs.shape
    cols = w_mod.shape[1]
    tn = 1536
    return pl.pallas_call(
        _mod_kernel,
        grid=(cols // tn,),
        in_specs=[pl.BlockSpec((n, d), lambda j: (0, 0)),
                  pl.BlockSpec((d, tn), lambda j: (0, j)),
                  pl.BlockSpec((1, tn), lambda j: (0, j))],
        out_specs=pl.BlockSpec((n, tn), lambda j: (0, j)),
        out_shape=jax.ShapeDtypeStruct((n, cols), F32),
        compiler_params=_params(("arbitrary",)),
        name="modulation",
    )(cs, w_mod, b_mod.reshape(1, cols))


def _proj_kernel(x_ref, g_ref, mod_ref, cos_ref, sin_ref, w_ref,
                 qa_ref, qb_ref, qd_ref, ka_ref, va_ref, kb_ref, vb_ref, kd_ref, vd_ref, vdt_ref, h_scr):
    h = _norm_mod(x_ref[0], g_ref[...], mod_ref[0, 0:1, :], mod_ref[0, 1:2, :])
    h_scr[...] = h.astype(BF16)
    cos = cos_ref[...]
    sin = sin_ref[...]
    tm = h.shape[0]
    lane = lax.broadcasted_iota(I32, (tm, LANES), 1)
    first16 = (lane & 16) == 0
    lo64 = lane < HEAD_DIM

    def rope(v):
        partner = jnp.where(first16, pltpu.roll(v, LANES - 16, 1), pltpu.roll(v, 16, 1))
        return v * cos + partner * sin

    def mm(c0):
        return _dot(h_scr[...], w_ref[:, c0:c0 + 2 * LANES])

    a = mm(0) * (HEAD_DIM ** -0.5)
    c0 = rope(a[:, :LANES])
    c1 = rope(a[:, LANES:])
    zero = jnp.zeros_like(c0)
    qa_ref[0, :, 0 * LANES:1 * LANES] = jnp.where(lo64, c0, zero).astype(BF16)
    qa_ref[0, :, 1 * LANES:2 * LANES] = jnp.where(lo64, pltpu.roll(c0, HEAD_DIM, 1), zero).astype(BF16)
    qa_ref[0, :, 2 * LANES:3 * LANES] = jnp.where(lo64, zero, pltpu.roll(c1, HEAD_DIM, 1)).astype(BF16)
    qa_ref[0, :, 3 * LANES:4 * LANES] = jnp.where(lo64, zero, c1).astype(BF16)
    qb_ref[0] = (mm(256) * (HEAD_DIM ** -0.5)).astype(BF16)
    for j in range(2):
        a = mm(512 + 256 * j) * (HEAD_DIM ** -0.5 * LOG2E)
        qd_ref[0, :, 256 * j:256 * j + LANES] = rope(a[:, :LANES]).astype(BF16)
        qd_ref[0, :, 256 * j + LANES:256 * (j + 1)] = rope(a[:, LANES:]).astype(BF16)
    a = mm(1024)
    ka_ref[0] = rope(a[:, :LANES]).astype(BF16)
    va_ref[0] = a[:, LANES:].astype(BF16)
    kb_ref[0] = mm(1280).astype(BF16)
    vb_ref[0] = mm(1536).astype(BF16)
    for j in range(2):
        a = mm(1792 + 256 * j)
        kd_ref[0, :, 256 * j:256 * j + LANES] = rope(a[:, :LANES]).astype(BF16)
        kd_ref[0, :, 256 * j + LANES:256 * (j + 1)] = rope(a[:, LANES:]).astype(BF16)
        a = mm(2304 + 256 * j)
        vd_ref[0, :, 256 * j:256 * (j + 1)] = a.astype(BF16)
        vdt_ref[0, 256 * j:256 * j + LANES, :] = a[:, :LANES].T.astype(BF16)
        vdt_ref[0, 256 * j + LANES:256 * (j + 1), :] = a[:, LANES:].T.astype(BF16)


def _projection(xc, g, modall, cos, sin, w_in, s_lat):
    b, sall, d = xc.shape
    tm = TOK_TILE
    nlat = s_lat // tm
    row = lambda width: pl.BlockSpec((1, tm, width), lambda i, j: (i, j, 0))
    widths = (512, 256, 512, 128, 128, 256, 256, 512, 512)
    return pl.pallas_call(
        _proj_kernel,
        grid=(b, sall // tm),
        in_specs=[row(d),
                  pl.BlockSpec((1, d), lambda i, j: (0, 0)),
                  pl.BlockSpec((1, 8, d), lambda i, j: (2 * i + (j >= nlat).astype(I32), 0, 0)),
                  pl.BlockSpec((tm, LANES), lambda i, j: (j, 0)),
                  pl.BlockSpec((tm, LANES), lambda i, j: (j, 0)),
                  pl.BlockSpec((d, IN_COLS), lambda i, j: (0, 0))],
        out_specs=[row(w) for w in widths] + [pl.BlockSpec((1, 512, tm), lambda i, j: (i, 0, j))],
        out_shape=([jax.ShapeDtypeStruct((b, sall, w), BF16) for w in widths]
                   + [jax.ShapeDtypeStruct((b, 512, sall), BF16)]),
        scratch_shapes=[pltpu.VMEM((tm, d), BF16)],
        compiler_params=_params(("arbitrary", "arbitrary")),
        name="projection",
    )(xc, g, modall, cos, sin, w_in)


def _half_mask(q, half):
    lane = lax.broadcasted_iota(I32, q.shape, 1)
    keep = (lane < HEAD_DIM) if half == 0 else (lane >= HEAD_DIM)
    return jnp.where(keep, q, jnp.zeros_like(q))


def _merge_halves(lo_part, hi_part):
    lane = lax.broadcasted_iota(I32, lo_part.shape, 1)
    return jnp.where(lane < HEAD_DIM, lo_part, hi_part)


def _gqa_rows(q_ref, g):
    return jnp.concatenate([q_ref[0, :, (2 * g) * LANES:(2 * g + 1) * LANES],
                            q_ref[0, :, (2 * g + 1) * LANES:(2 * g + 2) * LANES]], axis=0)


def _gqa_sink(sink_ref, g, tq):
    rowi = lax.broadcasted_iota(I32, (2 * tq, 1), 0)
    return jnp.where(rowi < tq, sink_ref[2 * g], sink_ref[2 * g + 1])


def _gqa_store(o_ref, g, o, tq):
    top, bot = o[:tq], o[tq:]
    if g == 0:
        chunk = _merge_halves(top, pltpu.roll(bot, HEAD_DIM, 1))
    else:
        chunk = _merge_halves(pltpu.roll(top, HEAD_DIM, 1), bot)
    o_ref[0, :, g * LANES:(g + 1) * LANES] = chunk.astype(BF16)


def _lam(lamv_ref, lam_init):
    v = lamv_ref[...]
    a = jnp.sum(v[0:1] * v[1:2], axis=-1, keepdims=True)
    b = jnp.sum(v[2:3] * v[3:4], axis=-1, keepdims=True)
    return jnp.exp(a) - jnp.exp(b) + lam_init


def _diff_post(o, subg, lam_init):
    return _rms(o, subg) * (1.0 - lam_init)


def _swa_kernel(sink_ref, q_ref, k_ref, v_ref, o_ref, *, s_lat, c_len):
    tq = ATT_TQ
    tk = tq + 2 * SWA_WINDOW
    q0 = pl.program_id(1) * tq
    ks = pl.multiple_of(jnp.clip(q0 - SWA_WINDOW, 0, s_lat - tk), SWA_WINDOW)
    kwin = k_ref[0, pl.ds(ks, tk), :]
    vwin = v_ref[0, pl.ds(ks, tk), :]
    kc = k_ref[0, s_lat:s_lat + c_len, :]
    vc = v_ref[0, s_lat:s_lat + c_len, :]
    qpos = q0 + lax.broadcasted_iota(I32, (tq, tk), 0)
    kpos = ks + lax.broadcasted_iota(I32, (tq, tk), 1)
    valid = jnp.abs(qpos - kpos) <= SWA_WINDOW
    valid = jnp.concatenate([valid, valid], axis=0)
    for g in range(2):
        q2 = _gqa_rows(q_ref, g)
        s_loc = jnp.where(valid, _dot_nt(q2, kwin), NEG_INF)
        s_ctx = _dot_nt(q2, kc)
        snk = _gqa_sink(sink_ref, g, tq)
        m = jnp.maximum(jnp.maximum(jnp.max(s_loc, axis=-1, keepdims=True),
                                    jnp.max(s_ctx, axis=-1, keepdims=True)), snk)
        e_loc = jnp.exp(s_loc - m)
        e_ctx = jnp.exp(s_ctx - m)
        den = (jnp.sum(e_loc, axis=-1, keepdims=True) + jnp.sum(e_ctx, axis=-1, keepdims=True)
               + jnp.exp(snk - m))
        o = (_dot(e_loc.astype(BF16), vwin) + _dot(e_ctx.astype(BF16), vc)) / den
        _gqa_store(o_ref, g, o, tq)


def _swa(sink, qa, ka, va, s_lat):
    b, sall, _ = qa.shape
    kv = pl.BlockSpec((1, sall, LANES), lambda i, j: (i, 0, 0))
    return pl.pallas_call(
        functools.partial(_swa_kernel, s_lat=s_lat, c_len=sall - s_lat),
        grid=(b, s_lat // ATT_TQ),
        in_specs=[pl.BlockSpec(memory_space=pltpu.SMEM),
                  pl.BlockSpec((1, ATT_TQ, 512), lambda i, j: (i, j, 0)), kv, kv],
        out_specs=pl.BlockSpec((1, ATT_TQ, 256), lambda i, j: (i, j, 0)),
        out_shape=jax.ShapeDtypeStruct((b, s_lat, 256), BF16),
        compiler_params=_params(("arbitrary", "arbitrary")),
        name="swa",
    )(sink, qa, ka, va)


def _na_kernel(q_ref, k_ref, v_ref, bm_ref, o_ref, *, s_lat, c_len):
    rows = s_lat // GRID_W
    nk = NA_KROWS * GRID_W
    r0 = pl.program_id(1) * NA_QROWS
    ks = pl.multiple_of(jnp.clip(r0 - NA_KH // 2, 0, rows - NA_KROWS) * GRID_W, GRID_W)
    for c in range(2):
        sl = slice(c * LANES, (c + 1) * LANES)
        kwin = k_ref[0, pl.ds(ks, nk), sl]
        vwin = v_ref[0, pl.ds(ks, nk), sl]
        kc = k_ref[0, s_lat:s_lat + c_len, sl]
        vc = v_ref[0, s_lat:s_lat + c_len, sl]
        q = q_ref[0, :, sl]
        outs = []
        for half in range(2):
            qm = _half_mask(q, half)
            s_loc = _dot_nt(qm, kwin) + bm_ref[0, 2 * c + half]
            s_ctx = _dot_nt(qm, kc)
            m = jnp.maximum(jnp.max(s_loc, axis=-1, keepdims=True), jnp.max(s_ctx, axis=-1, keepdims=True))
            e_loc = jnp.exp(s_loc - m)
            e_ctx = jnp.exp(s_ctx - m)
            den = jnp.sum(e_loc, axis=-1, keepdims=True) + jnp.sum(e_ctx, axis=-1, keepdims=True)
            outs.append((_dot(e_loc.astype(BF16), vwin) + _dot(e_ctx.astype(BF16), vc)) / den)
        o_ref[0, :, sl] = _merge_halves(outs[0], outs[1]).astype(BF16)


def _na_bias_mask(rpb, rows):
    nq = NA_QROWS * GRID_W
    nk = NA_KROWS * GRID_W
    col = np.arange(GRID_W)
    cstart = np.clip(col - NA_KW // 2, 0, GRID_W - NA_KW)
    col_ok = (col[None, :] >= cstart[:, None]) & (col[None, :] < cstart[:, None] + NA_KW)
    dc = np.clip(col[None, :] - col[:, None], -(NA_KW - 1), NA_KW - 1) + NA_KW - 1
    sel_c = np.where(col_ok[:, :, None], np.eye(2 * NA_KW - 1)[dc], 0.0)
    tabs = []
    for r0 in (0, NA_QROWS, rows - NA_QROWS):
        ksr = int(np.clip(r0 - NA_KH // 2, 0, rows - NA_KROWS))
        r = r0 + np.arange(NA_QROWS)
        kr = ksr + np.arange(NA_KROWS)
        rs = np.clip(r - NA_KH // 2, 0, rows - NA_KH)
        row_ok = (kr[None, :] >= rs[:, None]) & (kr[None, :] < rs[:, None] + NA_KH)
        dr = np.clip(kr[None, :] - r[:, None], -(NA_KH - 1), NA_KH - 1) + NA_KH - 1
        sel_r = np.where(row_ok[:, :, None], np.eye(2 * NA_KH - 1)[dr], 0.0)
        ok = (row_ok[:, None, :, None] & col_ok[None, :, None, :]).reshape(nq, nk)
        bias = jnp.einsum("hdc,rjd,qkc->hrqjk", rpb.astype(F32), jnp.asarray(sel_r, F32), jnp.asarray(sel_c, F32),
                          precision=lax.Precision.HIGHEST).reshape(N_HEADS, nq, nk)
        tabs.append(jnp.where(ok[None], bias, NEG_INF))
    return jnp.stack(tabs)


def _na(qb, kb, vb, bm, s_lat):
    b, sall, _ = qb.shape
    nq = NA_QROWS * GRID_W
    nsteps = s_lat // nq
    kv = pl.BlockSpec((1, sall, 256), lambda i, j: (i, 0, 0))

    def bm_map(i, j):
        return (jnp.where(j == 0, 0, jnp.where(j == nsteps - 1, 2, 1)), 0, 0, 0)

    return pl.pallas_call(
        functools.partial(_na_kernel, s_lat=s_lat, c_len=sall - s_lat),
        grid=(b, nsteps),
        in_specs=[pl.BlockSpec((1, nq, 256), lambda i, j: (i, j, 0)), kv, kv,
                  pl.BlockSpec((1, N_HEADS, nq, NA_KROWS * GRID_W), bm_map)],
        out_specs=pl.BlockSpec((1, nq, 256), lambda i, j: (i, j, 0)),
        out_shape=jax.ShapeDtypeStruct((b, s_lat, 256), BF16),
        compiler_params=_params(("arbitrary", "arbitrary")),
        name="na2d",
    )(qb, kb, vb, bm)


def _diff_kernel(lamv_ref, subg_ref, q_ref, k_ref, vt_ref, o_ref, e_scr, m_scr, *, n_keys, lam_init):
    tq = ATT_TQ
    chunks = [(c0, min(DIFF_KCHUNK, n_keys - c0)) for c0 in range(0, n_keys, DIFF_KCHUNK)]
    lam = _lam(lamv_ref, lam_init)
    stats = []
    for hh in range(DIFF_HEADS_PER_STEP):
        sl = slice(hh * LANES, (hh + 1) * LANES)
        q = q_ref[0, :, sl]
        q12 = jnp.concatenate([_half_mask(q, 0), _half_mask(q, 1)], axis=0)
        m = jnp.full((1, 2 * tq), NEG_INF, F32)
        den = jnp.zeros((1, 2 * tq), F32)
        for i, (c0, n) in enumerate(chunks):
            s = _dot_nt(k_ref[0, c0:c0 + n, sl], q12)
            m_new = jnp.maximum(m, jnp.max(s, axis=0, keepdims=True))
            e = jnp.exp2(s - m_new)
            e_scr[hh, c0:c0 + n, :] = e
            m_scr[hh, i:i + 1, :] = m_new
            den = den * jnp.exp2(m - m_new) + jnp.sum(e, axis=0, keepdims=True)
            m = m_new
        stats.append((m, den))
    for hh in range(DIFF_HEADS_PER_STEP):
        sl = slice(hh * LANES, (hh + 1) * LANES)
        m, den = stats[hh]
        acc = jnp.zeros((LANES, tq), F32)
        for i, (c0, n) in enumerate(chunks):
            f = jnp.exp2(m_scr[hh, i:i + 1, :] - m) / den
            p = e_scr[hh, c0:c0 + n, :tq] * f[:, :tq] - e_scr[hh, c0:c0 + n, tq:] * (lam * f[:, tq:])
            acc = acc + _dot(vt_ref[0, sl, c0:c0 + n], p.astype(BF16))
        o_ref[0, :, sl] = _diff_post(acc.T, subg_ref[...], lam_init).astype(BF16)


def _diff(lamv, subg, qd, kd, vdt, s_lat, lam_init):
    b, sall, _ = qd.shape
    nchunk = -(-sall // DIFF_KCHUNK)
    hps = DIFF_HEADS_PER_STEP
    wid = hps * LANES
    return pl.pallas_call(
        functools.partial(_diff_kernel, n_keys=sall, lam_init=lam_init),
        grid=(b, N_HEADS // hps, s_lat // ATT_TQ),
        in_specs=[pl.BlockSpec((8, LANES), lambda i, h, j: (0, 0)),
                  pl.BlockSpec((1, LANES), lambda i, h, j: (0, 0)),
                  pl.BlockSpec((1, ATT_TQ, wid), lambda i, h, j: (i, j, h)),
                  pl.BlockSpec((1, sall, wid), lambda i, h, j: (i, 0, h)),
                  pl.BlockSpec((1, wid, sall), lambda i, h, j: (i, h, 0))],
        out_specs=pl.BlockSpec((1, ATT_TQ, wid), lambda i, h, j: (i, j, h)),
        out_shape=jax.ShapeDtypeStruct((b, s_lat, 512), BF16),
        scratch_shapes=[pltpu.VMEM((hps, sall, 2 * ATT_TQ), F32),
                        pltpu.VMEM((hps, -(-nchunk // 8) * 8, 2 * ATT_TQ), F32)],
        compiler_params=_params(("arbitrary", "arbitrary", "arbitrary")),
        name="diff_attn",
    )(lamv, subg, qd, kd, vdt)


def _ctx_kernel(sink_ref, lamv_ref, subg_ref, qa_ref, ka_ref, va_ref, qb_ref, kb_ref, vb_ref,
                qd_ref, kd_ref, vd_ref, oa_ref, ob_ref, od_ref, *, lam_init):
    cl = qa_ref.shape[1]
    ka = ka_ref[0]
    va = va_ref[0]
    for g in range(2):
        q2 = _gqa_rows(qa_ref, g)
        s = _dot_nt(q2, ka)
        snk = _gqa_sink(sink_ref, g, cl)
        m = jnp.maximum(jnp.max(s, axis=-1, keepdims=True), snk)
        e = jnp.exp(s - m)
        den = jnp.sum(e, axis=-1, keepdims=True) + jnp.exp(snk - m)
        _gqa_store(oa_ref, g, _dot(e.astype(BF16), va) / den, cl)
    for c in range(2):
        sl = slice(c * LANES, (c + 1) * LANES)
        outs = []
        for half in range(2):
            s = _dot_nt(_half_mask(qb_ref[0, :, sl], half), kb_ref[0, :, sl])
            e = jnp.exp(s - jnp.max(s, axis=-1, keepdims=True))
            outs.append(_dot(e.astype(BF16), vb_ref[0, :, sl]) / jnp.sum(e, axis=-1, keepdims=True))
        ob_ref[0, :, sl] = _merge_halves(outs[0], outs[1]).astype(BF16)
    lam = _lam(lamv_ref, lam_init)
    for h in range(N_HEADS):
        sl = slice(h * LANES, (h + 1) * LANES)
        q = qd_ref[0, :, sl]
        q12 = jnp.concatenate([_half_mask(q, 0), _half_mask(q, 1)], axis=0)
        s = _dot_nt(q12, kd_ref[0, :, sl])
        e = jnp.exp2(s - jnp.max(s, axis=-1, keepdims=True))
        den = jnp.sum(e, axis=-1, keepdims=True)
        p = e[:cl] * (1.0 / den[:cl]) - e[cl:] * (lam / den[cl:])
        o = _dot(p.astype(BF16), vd_ref[0, :, sl])
        od_ref[0, :, sl] = _diff_post(o, subg_ref[...], lam_init).astype(BF16)


def _ctx_attention(sink, lamv, subg, qa, ka, va, qb, kb, vb, qd, kd, vd, s_lat, lam_init):
    b, sall, _ = qa.shape
    cl = sall - s_lat
    blk = s_lat // cl
    row = lambda width: pl.BlockSpec((1, cl, width), lambda i: (i, blk, 0))
    out = lambda width: pl.BlockSpec((1, cl, width), lambda i: (i, 0, 0))
    return pl.pallas_call(
        functools.partial(_ctx_kernel, lam_init=lam_init),
        grid=(b,),
        in_specs=[pl.BlockSpec(memory_space=pltpu.SMEM),
                  pl.BlockSpec((8, LANES), lambda i: (0, 0)),
                  pl.BlockSpec((1, LANES), lambda i: (0, 0)),
                  row(512), row(128), row(128), row(256), row(256), row(256), row(512), row(512), row(512)],
        out_specs=[out(256), out(256), out(512)],
        out_shape=[jax.ShapeDtypeStruct((b, cl, w), BF16) for w in (256, 256, 512)],
        compiler_params=_params(("arbitrary",)),
        name="ctx_attention",
    )(sink, lamv, subg, qa, ka, va, qb, kb, vb, qd, kd, vd)


def _out_kernel(x_ref, w_ref, mod_ref, *refs, nlat):
    xo_ref = refs[-1]

    def emit(oa_ref, ob_ref, od_ref):
        attn = (_dot(oa_ref[0], w_ref[0:256, :]) + _dot(ob_ref[0], w_ref[256:512, :])
                + _dot(od_ref[0], w_ref[512:1024, :]))
        xo_ref[0] = x_ref[0] + mod_ref[0, 2:3, :] * attn

    if len(refs) == 4:
        emit(*refs[:3])
    else:
        is_ctx = pl.program_id(1) >= nlat
        pl.when(jnp.logical_not(is_ctx))(lambda: emit(*refs[:3]))
        pl.when(is_ctx)(lambda: emit(*refs[3:6]))


def _out_projection(xc, lat, ctx, w_out, modall, s_lat):
    b, sall, d = xc.shape
    tm = TOK_TILE
    nlat = s_lat // tm
    n_rows = s_lat if ctx is None else sall
    row = lambda width: pl.BlockSpec((1, tm, width), lambda i, j: (i, j, 0))
    lat_row = lambda width: pl.BlockSpec((1, tm, width), lambda i, j: (i, jnp.minimum(j, nlat - 1), 0))
    ctx_row = lambda width: pl.BlockSpec((1, tm, width), lambda i, j: (i, 0, 0))
    widths = (256, 256, 512)
    specs = [lat_row(w) for w in widths] + ([] if ctx is None else [ctx_row(w) for w in widths])
    return pl.pallas_call(
        functools.partial(_out_kernel, nlat=nlat),
        grid=(b, n_rows // tm),
        in_specs=[row(d),
                  pl.BlockSpec((MIX_WIDTH, d), lambda i, j: (0, 0)),
                  pl.BlockSpec((1, 8, d), lambda i, j: (2 * i + (j >= nlat).astype(I32), 0, 0))] + specs,
        out_specs=row(d),
        out_shape=jax.ShapeDtypeStruct((b, n_rows, d), F32),
        compiler_params=_params(("arbitrary", "arbitrary")),
        name="out_projection",
    )(xc, w_out, modall, *lat, *(() if ctx is None else ctx))


def _first_argmax(v, iota, n):
    m = jnp.max(v, axis=0, keepdims=True)
    ix = jnp.min(jnp.where(v == m, iota, float(n)), axis=0, keepdims=True)
    return m, ix


def _router_kernel(x_ref, g_ref, mod_ref, whi_ref, wlo_ref, bias_ref, tri_ref, ltri_ref,
                   w_ref, slot_ref, meta_ref, cnt_ref):
    @pl.when(pl.program_id(0) == 0)
    def _():
        cnt_ref[...] = jnp.zeros_like(cnt_ref)

    hf = _norm_mod(x_ref[...], g_ref[...], mod_ref[0, 3:4, :], mod_ref[0, 4:5, :])
    tm = hf.shape[0]
    h_hi, h_lo = _split_bf16(hf)
    whi = whi_ref[...]
    logits = _dot_nt(whi, h_hi) + _dot_nt(whi, h_lo) + _dot_nt(wlo_ref[...], h_hi)
    scores = _sigmoid(logits)
    biased = scores + bias_ref[...]
    gsz = N_EXPERTS // N_GROUPS
    iota_g = lax.broadcasted_iota(I32, (gsz, tm), 0).astype(F32)
    gscore = []
    for g in range(N_GROUPS):
        v = biased[g * gsz:(g + 1) * gsz]
        m1, i1 = _first_argmax(v, iota_g, gsz)
        m2 = jnp.max(jnp.where(iota_g == i1, -jnp.inf, v), axis=0, keepdims=True)
        gscore.append(m1 + m2)
    cur = jnp.concatenate(gscore, axis=0)
    iota_n = lax.broadcasted_iota(I32, (N_GROUPS, tm), 0).astype(F32)
    gsel = jnp.zeros((N_GROUPS, tm), F32)
    for _ in range(TOPK_GROUPS):
        _, ix = _first_argmax(cur, iota_n, N_GROUPS)
        hit = iota_n == ix
        gsel = jnp.where(hit, 1.0, gsel)
        cur = jnp.where(hit, -jnp.inf, cur)
    masked = jnp.concatenate(
        [jnp.where(gsel[g:g + 1] > 0.5, biased[g * gsz:(g + 1) * gsz], -jnp.inf) for g in range(N_GROUPS)],
        axis=0)
    iota_e = lax.broadcasted_iota(I32, (N_EXPERTS, tm), 0).astype(F32)
    onehot = jnp.zeros((N_EXPERTS, tm), F32)
    idxs, ws = [], []
    for _ in range(TOP_K):
        _, ix = _first_argmax(masked, iota_e, N_EXPERTS)
        hit = iota_e == ix
        ws.append(jnp.sum(jnp.where(hit, scores, 0.0), axis=0, keepdims=True))
        masked = jnp.where(hit, -jnp.inf, masked)
        onehot = jnp.where(hit, 1.0, onehot)
        idxs.append(ix)
    base = cnt_ref[...]
    rank = _dot(onehot.astype(BF16), tri_ref[...])
    n_col = jnp.sum(onehot, axis=1, keepdims=True)
    n_pad = jnp.floor((n_col + (ROW_CHUNK - 1)) * (1.0 / ROW_CHUNK)) * ROW_CHUNK
    start = _dot(ltri_ref[...], jnp.broadcast_to(n_pad, (N_EXPERTS, LANES)).astype(BF16))
    wsum = ws[0]
    for wk in ws[1:]:
        wsum = wsum + wk
    for k in range(TOP_K):
        slot = jnp.sum(jnp.where(iota_e == idxs[k], rank + start[:, 0:1], 0.0), axis=0, keepdims=True)
        slot_ref[0, k:k + 1, :] = slot.astype(I32)
        w_ref[0, k:k + 1, :] = ws[k] / wsum * ROUTED_SCALE
    lane = lax.broadcasted_iota(I32, (N_EXPERTS, LANES), 1)
    meta_ref[0] = jnp.where(lane == 0, n_pad, jnp.where(lane == 1, base, start))
    cnt_ref[...] = base + n_pad


def _router(xflat, g, modall, whi, wlo, bias, tri, ltri, tiles_per_batch, nlat):
    t, d = xflat.shape
    tm = TOK_TILE
    nt = t // tm
    per_tok = pl.BlockSpec((1, TOP_K, tm), lambda i: (i, 0, 0))

    def mod_map(i):
        return (2 * (i // tiles_per_batch) + ((i % tiles_per_batch) >= nlat).astype(I32), 0, 0)

    return pl.pallas_call(
        _router_kernel,
        grid=(nt,),
        in_specs=[pl.BlockSpec((tm, d), lambda i: (i, 0)),
                  pl.BlockSpec((1, d), lambda i: (0, 0)),
                  pl.BlockSpec((1, 8, d), mod_map),
                  pl.BlockSpec((N_EXPERTS, d), lambda i: (0, 0)),
                  pl.BlockSpec((N_EXPERTS, d), lambda i: (0, 0)),
                  pl.BlockSpec((N_EXPERTS, 1), lambda i: (0, 0)),
                  pl.BlockSpec((tm, tm), lambda i: (0, 0)),
                  pl.BlockSpec((N_EXPERTS, N_EXPERTS), lambda i: (0, 0))],
        out_specs=[per_tok, per_tok,
                   pl.BlockSpec((1, N_EXPERTS, LANES), lambda i: (i, 0, 0)),
                   pl.BlockSpec((N_EXPERTS, 1), lambda i: (0, 0))],
        out_shape=[jax.ShapeDtypeStruct((nt, TOP_K, tm), F32),
                   jax.ShapeDtypeStruct((nt, TOP_K, tm), I32),
                   jax.ShapeDtypeStruct((nt, N_EXPERTS, LANES), F32),
                   jax.ShapeDtypeStruct((N_EXPERTS, 1), F32)],
        compiler_params=_params(("arbitrary",)),
        name="router",
    )(xflat, g, modall, whi, wlo, bias, tri, ltri)


def _pack_bf16_pairs(v):
    half = v.shape[1] // 2
    vb = v.astype(BF16).astype(F32)
    lo = lax.shift_right_logical(lax.bitcast_convert_type(vb[:, :half], U32), jnp.uint32(16))
    hi = lax.bitcast_convert_type(vb[:, half:], U32) & jnp.uint32(0xFFFF0000)
    return lo | hi


def _unpack_bf16_pairs(u):
    lo = lax.bitcast_convert_type(lax.shift_left(u, jnp.uint32(16)), F32).astype(BF16)
    hi = lax.bitcast_convert_type(u & jnp.uint32(0xFFFF0000), F32).astype(BF16)
    return lo, hi


def _chunk_copies(meta_ref, make_copy):
    total = lax.shift_right_logical(_slots_used(meta_ref), ROW_CHUNK.bit_length() - 1)

    def start(j, carry):
        make_copy(j).start()
        return carry

    lax.fori_loop(0, total, start, 0)
    return total


def _tile_slots(tm):
    return TOP_K * tm + ROW_CHUNK * N_EXPERTS


def _slots_used(meta_ref):
    last = N_EXPERTS - 1
    return meta_ref[0, 2, last] + meta_ref[0, 0, last]


def _dispatch_kernel(meta_ref, rows_ref, x_ref, g_ref, mod_ref, slot_ref, xs_ref, xc_scr, sem):
    hb = _norm_mod(x_ref[...], g_ref[...], mod_ref[0, 3:4, :], mod_ref[0, 4:5, :]).astype(BF16)
    tm = hb.shape[0]
    slot = slot_ref[0]

    def select_rows(r, carry):
        r0 = pl.multiple_of(r * tm, tm)
        row = lax.broadcasted_iota(I32, (tm, tm), 0) + r0
        sel = jnp.zeros((tm, tm), F32)
        for k in range(TOP_K):
            sel = jnp.where(row == slot[k:k + 1, :], 1.0, sel)
        xc_scr[pl.ds(r0, tm), :] = _pack_bf16_pairs(_dot(sel.astype(BF16), hb))
        return carry

    lax.fori_loop(0, (_slots_used(meta_ref) + tm - 1) // tm, select_rows, 0)

    def make_copy(j):
        src = pl.multiple_of(j * ROW_CHUNK, ROW_CHUNK)
        dst = pl.multiple_of(rows_ref[0, 0, j], ROW_CHUNK)
        return pltpu.make_async_copy(xc_scr.at[pl.ds(src, ROW_CHUNK)], xs_ref.at[pl.ds(dst, ROW_CHUNK)], sem)

    total = _chunk_copies(meta_ref, make_copy)

    def wait(j, carry):
        pltpu.make_async_copy(xc_scr.at[pl.ds(0, ROW_CHUNK)], xs_ref.at[pl.ds(0, ROW_CHUNK)], sem).wait()
        return carry

    lax.fori_loop(0, total, wait, 0)


def _dispatch(meta, chunk_rows, slot, xflat, g, modall, n_rows, tiles_per_batch, nlat):
    t, d = xflat.shape
    tm = TOK_TILE
    nt = t // tm

    def mod_map(i):
        return (2 * (i // tiles_per_batch) + ((i % tiles_per_batch) >= nlat).astype(I32), 0, 0)

    return pl.pallas_call(
        _dispatch_kernel,
        grid=(nt,),
        in_specs=[pl.BlockSpec((1, 3, N_EXPERTS), lambda i: (i, 0, 0), memory_space=pltpu.SMEM),
                  pl.BlockSpec((1, 1, chunk_rows.shape[2]), lambda i: (i, 0, 0), memory_space=pltpu.SMEM),
                  pl.BlockSpec((tm, d), lambda i: (i, 0)),
                  pl.BlockSpec((1, d), lambda i: (0, 0)),
                  pl.BlockSpec((1, 8, d), mod_map),
                  pl.BlockSpec((1, TOP_K, tm), lambda i: (i, 0, 0))],
        out_specs=pl.BlockSpec(memory_space=pl.ANY),
        out_shape=jax.ShapeDtypeStruct((n_rows, d // 2), U32),
        scratch_shapes=[pltpu.VMEM((_tile_slots(tm), d // 2), U32), pltpu.SemaphoreType.DMA(())],
        compiler_params=_params(("arbitrary",)),
        name="moe_dispatch",
    )(meta, chunk_rows, xflat, g, modall, slot)


def _expert_kernel(blk_e_ref, nvalid_ref, nact_ref, xs_ref, wg_ref, wu_ref, wd_ref, ys_ref, wg_s, wu_s, wd_s):
    b = pl.program_id(0)
    changed = jnp.logical_or(b == 0, blk_e_ref[b] != blk_e_ref[jnp.maximum(b - 1, 0)])

    @pl.when(changed)
    def _():
        wg_s[...] = wg_ref[0, 0].astype(BF16)
        wu_s[...] = wu_ref[0, 0].astype(BF16)
        wd_s[...] = wd_ref[0, 0].astype(BF16)

    @pl.when(b < nact_ref[0])
    def _():
        half = xs_ref.shape[1]
        row = lax.broadcasted_iota(I32, xs_ref.shape, 0)
        xs = jnp.where(row < nvalid_ref[b], xs_ref[...], jnp.zeros(xs_ref.shape, U32))
        x_lo, x_hi = _unpack_bf16_pairs(xs)
        gate = _dot(x_lo, wg_s[0:half, :]) + _dot(x_hi, wg_s[half:, :])
        up = _dot(x_lo, wu_s[0:half, :]) + _dot(x_hi, wu_s[half:, :])
        mid = (gate * _sigmoid(gate)) * up
        ys_ref[...] = _pack_bf16_pairs(_dot(mid.astype(BF16), wd_s[...]))

    @pl.when(b >= nact_ref[0])
    def _():
        ys_ref[...] = jnp.zeros_like(ys_ref)


def _experts(blk_e, nvalid, nact, xs, w_gate, w_up, w_down, layer):
    n_rows, half = xs.shape
    _, _, d, f = w_gate.shape
    n_blk = n_rows // MOE_BLOCK
    grid_spec = pltpu.PrefetchScalarGridSpec(
        num_scalar_prefetch=3,
        grid=(n_blk,),
        in_specs=[pl.BlockSpec((MOE_BLOCK, half), lambda i, be, nv, na: (jnp.minimum(i, na[0] - 1), 0)),
                  pl.BlockSpec((1, 1, d, f), lambda i, be, nv, na: (layer, be[i], 0, 0)),
                  pl.BlockSpec((1, 1, d, f), lambda i, be, nv, na: (layer, be[i], 0, 0)),
                  pl.BlockSpec((1, 1, f, d), lambda i, be, nv, na: (layer, be[i], 0, 0))],
        out_specs=pl.BlockSpec((MOE_BLOCK, half), lambda i, be, nv, na: (i, 0)),
        scratch_shapes=[pltpu.VMEM((d, f), BF16), pltpu.VMEM((d, f), BF16), pltpu.VMEM((f, d), BF16)],
    )
    return pl.pallas_call(
        _expert_kernel,
        grid_spec=grid_spec,
        out_shape=jax.ShapeDtypeStruct((n_rows, half), U32),
        compiler_params=_params(("arbitrary",)),
        name="moe_experts",
    )(blk_e, nvalid, nact, xs, w_gate, w_up, w_down)


def _combine_kernel(meta_ref, rows_ref, x_ref, g_ref, mod_ref, spos_ref, wt_ref, wsg_ref, wsu_ref, wsd_ref,
                    gfin_ref, ys_ref, o_ref, y_scr, acc_scr, sb_scr, wb_scr, sem, *, final_norm):
    tm, d = x_ref.shape
    half = d // 2

    @pl.when(pl.program_id(0) == 0)
    def _():
        y_scr[...] = jnp.zeros_like(y_scr)

    def make_copy(j):
        src = pl.multiple_of(rows_ref[0, 0, j], ROW_CHUNK)
        dst = pl.multiple_of(j * ROW_CHUNK, ROW_CHUNK)
        return pltpu.make_async_copy(ys_ref.at[pl.ds(src, ROW_CHUNK)], y_scr.at[pl.ds(dst, ROW_CHUNK)], sem)

    total = _chunk_copies(meta_ref, make_copy)

    x = x_ref[...]
    hb = _norm_mod(x, g_ref[...], mod_ref[0, 3:4, :], mod_ref[0, 4:5, :]).astype(BF16)
    gate = _dot(hb, wsg_ref[...])
    mid = (gate * _sigmoid(gate)) * _dot(hb, wsu_ref[...])
    acc_scr[...] = _dot(mid.astype(BF16), wsd_ref[...])

    def wait(j, carry):
        pltpu.make_async_copy(ys_ref.at[pl.ds(0, ROW_CHUNK)], y_scr.at[pl.ds(0, ROW_CHUNK)], sem).wait()
        return carry

    lax.fori_loop(0, total, wait, 0)

    n_used = _slots_used(meta_ref)
    for k in range(TOP_K):
        sb_scr[k] = jnp.broadcast_to(spos_ref[0, :, k:k + 1], (tm, LANES))
        wb_scr[k] = jnp.broadcast_to(wt_ref[0, :, k:k + 1], (tm, LANES))

    def chunk(j, carry):
        r0 = pl.multiple_of(j * COMBINE_KCHUNK, COMBINE_KCHUNK)
        parts = []
        for c in range(COMBINE_KCHUNK // LANES):
            col = lax.broadcasted_iota(I32, (tm, LANES), 1) + (r0 + c * LANES)
            p = jnp.zeros((tm, LANES), F32)
            for k in range(TOP_K):
                p = jnp.where(col == sb_scr[k], wb_scr[k], p)
            parts.append(p.astype(BF16))
        pb = jnp.concatenate(parts, axis=1)
        y_lo, y_hi = _unpack_bf16_pairs(y_scr[pl.ds(r0, COMBINE_KCHUNK), :])
        acc_scr[:, :half] += _dot(pb, y_lo)
        acc_scr[:, half:] += _dot(pb, y_hi)
        return carry

    lax.fori_loop(0, (n_used + COMBINE_KCHUNK - 1) // COMBINE_KCHUNK, chunk, 0)
    out = x + mod_ref[0, 5:6, :] * acc_scr[...]
    if final_norm:
        out = _rms(out, gfin_ref[...])
    o_ref[...] = out


def _combine(meta, chunk_rows, spos_t, wt, xflat, g, modall, wsg, wsu, wsd, gfin, ys, tiles_per_batch, nlat,
             final_norm):
    t, d = xflat.shape
    f = wsg.shape[1]
    tm = TOK_TILE
    nt = t // tm
    y_rows = _tile_slots(tm)
    assert y_rows % COMBINE_KCHUNK == 0

    def mod_map(i):
        return (2 * (i // tiles_per_batch) + ((i % tiles_per_batch) >= nlat).astype(I32), 0, 0)

    return pl.pallas_call(
        functools.partial(_combine_kernel, final_norm=final_norm),
        grid=(nt,),
        in_specs=[pl.BlockSpec((1, 3, N_EXPERTS), lambda i: (i, 0, 0), memory_space=pltpu.SMEM),
                  pl.BlockSpec((1, 1, chunk_rows.shape[2]), lambda i: (i, 0, 0), memory_space=pltpu.SMEM),
                  pl.BlockSpec((tm, d), lambda i: (i, 0)),
                  pl.BlockSpec((1, d), lambda i: (0, 0)),
                  pl.BlockSpec((1, 8, d), mod_map),
                  pl.BlockSpec((1, tm, TOP_K), lambda i: (i, 0, 0)),
                  pl.BlockSpec((1, tm, TOP_K), lambda i: (i, 0, 0)),
                  pl.BlockSpec((d, f), lambda i: (0, 0)),
                  pl.BlockSpec((d, f), lambda i: (0, 0)),
                  pl.BlockSpec((f, d), lambda i: (0, 0)),
                  pl.BlockSpec((1, d), lambda i: (0, 0)),
                  pl.BlockSpec(memory_space=pl.ANY)],
        out_specs=pl.BlockSpec((tm, d), lambda i: (i, 0)),
        out_shape=jax.ShapeDtypeStruct((t, d), F32),
        scratch_shapes=[pltpu.VMEM((y_rows, d // 2), U32), pltpu.VMEM((tm, d), F32),
                        pltpu.VMEM((TOP_K, tm, LANES), I32), pltpu.VMEM((TOP_K, tm, LANES), F32),
                        pltpu.SemaphoreType.DMA(())],
        compiler_params=_params(("arbitrary",)),
        name="moe_combine",
    )(meta, chunk_rows, xflat, g, modall, spos_t, wt, wsg, wsu, wsd, gfin, ys)


def _rope_tables(s_lat, c_len):
    n_freq = HEAD_DIM // 4
    inv = 1.0 / (ROPE_BASE ** (jnp.arange(n_freq, dtype=F32) / n_freq))
    t = jnp.arange(s_lat)
    pos = jnp.stack([(t // GRID_W).astype(F32), (t % GRID_W).astype(F32)], axis=1)
    lane = np.arange(LANES)
    ang = pos[:, (lane % HEAD_DIM) // 32] * inv[lane % 16][None, :]
    sign = jnp.asarray(np.where(lane % 32 < 16, -1.0, 1.0), F32)
    cos = jnp.concatenate([jnp.cos(ang), jnp.ones((c_len, LANES), F32)], axis=0)
    sin = jnp.concatenate([jnp.sin(ang) * sign, jnp.zeros((c_len, LANES), F32)], axis=0)
    return cos, sin


def _moe(xflat, lp, modall, gfin, tiles_per_batch, nlat, final_norm):
    t, d = xflat.shape
    nt = t // TOK_TILE
    g = lp["g_ffn"].reshape(1, d)
    w_r = lp["w_router"].T
    whi = w_r.astype(BF16)
    wlo = (w_r - whi.astype(F32)).astype(BF16)
    tri = jnp.asarray(np.triu(np.ones((TOK_TILE, TOK_TILE), np.float32), 1), BF16)
    ltri = jnp.asarray(np.tril(np.ones((N_EXPERTS, N_EXPERTS), np.float32), -1), BF16)
    w, slot, meta, cnt = _router(xflat, g, modall, whi, wlo, lp["router_bias"].reshape(N_EXPERTS, 1),
                                 tri, ltri, tiles_per_batch, nlat)
    meta = jnp.transpose(meta[:, :, :3].astype(I32), (0, 2, 1))
    rows_e = cnt[:, 0].astype(I32)
    seg = (rows_e + MOE_BLOCK - 1) // MOE_BLOCK * MOE_BLOCK
    pend = jnp.cumsum(seg)
    pstart = (pend - seg).astype(I32)
    n_blk = (t * TOP_K + ROW_CHUNK * N_EXPERTS * nt) // MOE_BLOCK + N_EXPERTS
    blk_row0 = jnp.arange(n_blk, dtype=I32) * MOE_BLOCK
    blk_e = jnp.minimum(jnp.sum((pend[None, :] <= blk_row0[:, None]).astype(I32), axis=1), N_EXPERTS - 1)
    of_blk = blk_e[:, None] == jnp.arange(N_EXPERTS, dtype=I32)[None, :]
    seg_end = jnp.sum(jnp.where(of_blk, (pstart + rows_e)[None, :], 0), axis=1)
    nvalid = jnp.clip(seg_end - blk_row0, 0, MOE_BLOCK).astype(I32)
    nact = (pend[-1] // MOE_BLOCK).astype(I32).reshape(1)
    n_pad, base, start = meta[:, 0, :], meta[:, 1, :], meta[:, 2, :]
    chunk_slot = jnp.arange(_tile_slots(TOK_TILE) // ROW_CHUNK, dtype=I32) * ROW_CHUNK
    chunk_e = jnp.minimum(jnp.sum(((start + n_pad)[:, None, :] <= chunk_slot[None, :, None]).astype(I32), axis=2),
                          N_EXPERTS - 1)
    row0 = pstart[None, :] + base - start
    chunk_rows = (jnp.take_along_axis(row0, chunk_e, axis=1) + chunk_slot[None, :])[:, None, :]
    xs = _dispatch(meta, chunk_rows, slot, xflat, g, modall, n_blk * MOE_BLOCK, tiles_per_batch, nlat)
    ys = _experts(blk_e, nvalid, nact, xs, lp["w_gate"], lp["w_up"], lp["w_down"], lp["layer"])
    return _combine(meta, chunk_rows, jnp.transpose(slot, (0, 2, 1)), jnp.transpose(w, (0, 2, 1)), xflat, g, modall,
                    lp["ws_gate"].astype(BF16), lp["ws_up"].astype(BF16), lp["ws_down"].astype(BF16), gfin, ys,
                    tiles_per_batch, nlat, final_norm)


def _layer(xc, cs, cos, sin, lp, layer_idx, last, s_lat, gfin):
    b, sall, d = xc.shape
    c_len = sall - s_lat
    lam_init = 0.8 - 0.6 * math.exp(-0.3 * layer_idx)
    mod = _modulation(cs, lp["w_mod"], lp["b_mod"])
    mod_lat = mod[:b].reshape(b, 1, 6, d)
    mod_ctx = jnp.broadcast_to(mod[b].reshape(1, 1, 6, d), (b, 1, 6, d))
    modall = jnp.concatenate([mod_lat, mod_ctx], axis=1)
    modall = jnp.pad(modall, ((0, 0), (0, 0), (0, 2), (0, 0))).reshape(2 * b, 8, d)

    qa, qb, qd, ka, va, kb, vb, kd, vd, vdt = _projection(
        xc, lp["g_mix"].reshape(1, d), modall, cos, sin, lp["w_in"].astype(BF16), s_lat)
    sink = lp["attn_sink"].astype(F32)
    lamv = jnp.zeros((8, LANES), F32).at[0:4, 0:HEAD_DIM].set(
        jnp.stack([lp["lam_q1"], lp["lam_k1"], lp["lam_q2"], lp["lam_k2"]]).astype(F32))
    subg = lp["subln_g"].reshape(1, LANES).astype(F32)
    oa = _swa(sink, qa, ka, va, s_lat)
    ob = _na(qb, kb, vb, _na_bias_mask(lp["na_rpb"], s_lat // GRID_W), s_lat)
    od = _diff(lamv, subg, qd, kd, vdt, s_lat, lam_init)
    ctx_out = None
    if not last:
        ctx_out = _ctx_attention(sink, lamv, subg, qa, ka, va, qb, kb, vb, qd, kd, vd, s_lat, lam_init)
    n_rows = s_lat if last else sall
    x2 = _out_projection(xc, (oa, ob, od), ctx_out, lp["w_out"].astype(BF16), modall, s_lat)
    tiles_per_batch = n_rows // TOK_TILE
    y = _moe(x2.reshape(b * n_rows, d), lp, modall, gfin, tiles_per_batch, s_lat // TOK_TILE, last)
    return y.reshape(b, n_rows, d)


_LAYER_KEYS = ("w_mod", "b_mod", "g_mix", "g_ffn", "w_in", "w_out", "attn_sink", "na_rpb", "lam_q1", "lam_k1",
               "lam_q2", "lam_k2", "subln_g", "w_router", "router_bias", "w_gate", "w_up", "w_down",
               "ws_gate", "ws_up", "ws_down")


def kernel(x, c, ctx, c_ctx, w_mod, b_mod, g_mix, g_ffn, w_in, w_out, attn_sink, na_rpb, lam_q1, lam_k1, lam_q2,
           lam_k2, subln_g, w_router, router_bias, w_gate, w_up, w_down, ws_gate, ws_up, ws_down, g_final):
    stacked = dict(zip(_LAYER_KEYS, (w_mod, b_mod, g_mix, g_ffn, w_in, w_out, attn_sink, na_rpb, lam_q1, lam_k1,
                                     lam_q2, lam_k2, subln_g, w_router, router_bias, w_gate, w_up, w_down,
                                     ws_gate, ws_up, ws_down)))
    b, s_lat, d = x.shape
    c_len = ctx.shape[1]
    depth = w_mod.shape[0]
    assert s_lat % (NA_QROWS * GRID_W) == 0 and s_lat % c_len == 0 and c_len == TOK_TILE
    cs = jnp.zeros((16, d), F32).at[:b].set(c).at[b].set(c_ctx)
    cos, sin = _rope_tables(s_lat, c_len)
    xc = jnp.concatenate([x, ctx], axis=1)
    gfin = g_final.reshape(1, d)
    for i in range(depth):
        big = ("w_gate", "w_up", "w_down")
        lp = {k: (v if k in big else v[i]) for k, v in stacked.items()}
        lp["layer"] = i
        xc = _layer(xc, cs, cos, sin, lp, i, i == depth - 1, s_lat, gfin)
    return xc
```

```python
import functools
import math

import numpy as np
import jax
import jax.numpy as jnp
from jax import lax
from jax.experimental import pallas as pl
from jax.experimental.pallas import tpu as pltpu

F32 = jnp.float32
BF16 = jnp.bfloat16
I32 = jnp.int32
U32 = jnp.uint32

GRID_W = 64
HEAD_DIM = 64
ROPE_BASE = 10000.0
SWA_WINDOW = 128
NA_KH = 8
NA_KW = 16
N_HEADS = 4
N_EXPERTS = 256
TOP_K = 8
N_GROUPS = 8
TOPK_GROUPS = 4
ROUTED_SCALE = 2.5
MOE_BLOCK = 512
ROW_CHUNK = 8
COMBINE_KCHUNK = 512
EPS = 1e-6
NEG_INF = -1e30
LOG2E = math.log2(math.e)
Q_COLS = 1024
IN_COLS = 2816
MIX_WIDTH = 1024

LANES = 128
TOK_TILE = 256
ATT_TQ = 256
NA_QROWS = 4
NA_KROWS = NA_QROWS + NA_KH - 1
DIFF_KCHUNK = 256
DIFF_HEADS_PER_STEP = 2
VMEM_LIMIT = 48 * 1024 * 1024


def _params(sem):
    return pltpu.CompilerParams(dimension_semantics=sem, vmem_limit_bytes=VMEM_LIMIT)


def _dot(a, b):
    return jnp.dot(a, b, preferred_element_type=F32)


def _dot_nt(a, b):
    return lax.dot_general(a, b, (((1,), (1,)), ((), ())), preferred_element_type=F32)


def _split_bf16(a):
    hi = a.astype(BF16)
    lo = (a - hi.astype(F32)).astype(BF16)
    return hi, lo


def _sigmoid(x):
    return 1.0 / (1.0 + jnp.exp(-x))


def _rms(x, g):
    return x * lax.rsqrt(jnp.mean(x * x, axis=-1, keepdims=True) + EPS) * g


def _norm_mod(x, g, shift, scale):
    return _rms(x, g) * (1.0 + scale) + shift


def _mod_kernel(c_ref, w_ref, b_ref, o_ref):
    c = c_ref[...]
    a_hi, a_lo = _split_bf16(c * _sigmoid(c))
    w_hi, w_lo = _split_bf16(w_ref[...])
    o_ref[...] = _dot(a_hi, w_hi) + _dot(a_hi, w_lo) + _dot(a_lo, w_hi) + b_ref[...]


def _modulation(cs, w_mod, b_mod):
    n, d = cs.shape
    cols = w_mod.shape[1]
    tn = 1536
    return pl.pallas_call(
        _mod_kernel,
        grid=(cols // tn,),
        in_specs=[pl.BlockSpec((n, d), lambda j: (0, 0)),
                  pl.BlockSpec((d, tn), lambda j: (0, j)),
                  pl.BlockSpec((1, tn), lambda j: (0, j))],
        out_specs=pl.BlockSpec((n, tn), lambda j: (0, j)),
        out_shape=jax.ShapeDtypeStruct((n, cols), F32),
        compiler_params=_params(("arbitrary",)),
        name="modulation",
    )(cs, w_mod, b_mod.reshape(1, cols))


def _proj_kernel(x_ref, g_ref, mod_ref, cos_ref, sin_ref, w_ref,
                 qa_ref, qb_ref, qd_ref, ka_ref, va_ref, kb_ref, vb_ref, kd_ref, vd_ref, vdt_ref, h_scr):
    h = _norm_mod(x_ref[0], g_ref[...], mod_ref[0, 0:1, :], mod_ref[0, 1:2, :])
    h_scr[...] = h.astype(BF16)
    cos = cos_ref[...]
    sin = sin_ref[...]
    tm = h.shape[0]
    lane = lax.broadcasted_iota(I32, (tm, LANES), 1)
    first16 = (lane & 16) == 0
    lo64 = lane < HEAD_DIM

    def rope(v):
        partner = jnp.where(first16, pltpu.roll(v, LANES - 16, 1), pltpu.roll(v, 16, 1))
        return v * cos + partner * sin

    def mm(c0):
        return _dot(h_scr[...], w_ref[:, c0:c0 + 2 * LANES])

    a = mm(0) * (HEAD_DIM ** -0.5)
    c0 = rope(a[:, :LANES])
    c1 = rope(a[:, LANES:])
    zero = jnp.zeros_like(c0)
    qa_ref[0, :, 0 * LANES:1 * LANES] = jnp.where(lo64, c0, zero).astype(BF16)
    qa_ref[0, :, 1 * LANES:2 * LANES] = jnp.where(lo64, pltpu.roll(c0, HEAD_DIM, 1), zero).astype(BF16)
    qa_ref[0, :, 2 * LANES:3 * LANES] = jnp.where(lo64, zero, pltpu.roll(c1, HEAD_DIM, 1)).astype(BF16)
    qa_ref[0, :, 3 * LANES:4 * LANES] = jnp.where(lo64, zero, c1).astype(BF16)
    qb_ref[0] = (mm(256) * (HEAD_DIM ** -0.5)).astype(BF16)
    for j in range(2):
        a = mm(512 + 256 * j) * (HEAD_DIM ** -0.5 * LOG2E)
        qd_ref[0, :, 256 * j:256 * j + LANES] = rope(a[:, :LANES]).astype(BF16)
        qd_ref[0, :, 256 * j + LANES:256 * (j + 1)] = rope(a[:, LANES:]).astype(BF16)
    a = mm(1024)
    ka_ref[0] = rope(a[:, :LANES]).astype(BF16)
    va_ref[0] = a[:, LANES:].astype(BF16)
    kb_ref[0] = mm(1280).astype(BF16)
    vb_ref[0] = mm(1536).astype(BF16)
    for j in range(2):
        a = mm(1792 + 256 * j)
        kd_ref[0, :, 256 * j:256 * j + LANES] = rope(a[:, :LANES]).astype(BF16)
        kd_ref[0, :, 256 * j + LANES:256 * (j + 1)] = rope(a[:, LANES:]).astype(BF16)
        a = mm(2304 + 256 * j)
        vd_ref[0, :, 256 * j:256 * (j + 1)] = a.astype(BF16)
        vdt_ref[0, 256 * j:256 * j + LANES, :] = a[:, :LANES].T.astype(BF16)
        vdt_ref[0, 256 * j + LANES:256 * (j + 1), :] = a[:, LANES:].T.astype(BF16)


def _projection(xc, g, modall, cos, sin, w_in, s_lat):
    b, sall, d = xc.shape
    tm = TOK_TILE
    nlat = s_lat // tm
    row = lambda width: pl.BlockSpec((1, tm, width), lambda i, j: (i, j, 0))
    widths = (512, 256, 512, 128, 128, 256, 256, 512, 512)
    return pl.pallas_call(
        _proj_kernel,
        grid=(b, sall // tm),
        in_specs=[row(d),
                  pl.BlockSpec((1, d), lambda i, j: (0, 0)),
                  pl.BlockSpec((1, 8, d), lambda i, j: (2 * i + (j >= nlat).astype(I32), 0, 0)),
                  pl.BlockSpec((tm, LANES), lambda i, j: (j, 0)),
                  pl.BlockSpec((tm, LANES), lambda i, j: (j, 0)),
                  pl.BlockSpec((d, IN_COLS), lambda i, j: (0, 0))],
        out_specs=[row(w) for w in widths] + [pl.BlockSpec((1, 512, tm), lambda i, j: (i, 0, j))],
        out_shape=([jax.ShapeDtypeStruct((b, sall, w), BF16) for w in widths]
                   + [jax.ShapeDtypeStruct((b, 512, sall), BF16)]),
        scratch_shapes=[pltpu.VMEM((tm, d), BF16)],
        compiler_params=_params(("arbitrary", "arbitrary")),
        name="projection",
    )(xc, g, modall, cos, sin, w_in)


def _half_mask(q, half):
    lane = lax.broadcasted_iota(I32, q.shape, 1)
    keep = (lane < HEAD_DIM) if half == 0 else (lane >= HEAD_DIM)
    return jnp.where(keep, q, jnp.zeros_like(q))


def _merge_halves(lo_part, hi_part):
    lane = lax.broadcasted_iota(I32, lo_part.shape, 1)
    return jnp.where(lane < HEAD_DIM, lo_part, hi_part)


def _gqa_rows(q_ref, g):
    return jnp.concatenate([q_ref[0, :, (2 * g) * LANES:(2 * g + 1) * LANES],
                            q_ref[0, :, (2 * g + 1) * LANES:(2 * g + 2) * LANES]], axis=0)


def _gqa_sink(sink_ref, g, tq):
    rowi = lax.broadcasted_iota(I32, (2 * tq, 1), 0)
    return jnp.where(rowi < tq, sink_ref[2 * g], sink_ref[2 * g + 1])


def _gqa_store(o_ref, g, o, tq):
    top, bot = o[:tq], o[tq:]
    if g == 0:
        chunk = _merge_halves(top, pltpu.roll(bot, HEAD_DIM, 1))
    else:
        chunk = _merge_halves(pltpu.roll(top, HEAD_DIM, 1), bot)
    o_ref[0, :, g * LANES:(g + 1) * LANES] = chunk.astype(BF16)


def _lam(lamv_ref, lam_init):
    v = lamv_ref[...]
    a = jnp.sum(v[0:1] * v[1:2], axis=-1, keepdims=True)
    b = jnp.sum(v[2:3] * v[3:4], axis=-1, keepdims=True)
    return jnp.exp(a) - jnp.exp(b) + lam_init


def _diff_post(o, subg, lam_init):
    return _rms(o, subg) * (1.0 - lam_init)


def _swa_kernel(sink_ref, q_ref, k_ref, v_ref, o_ref, *, s_lat, c_len):
    tq = ATT_TQ
    tk = tq + 2 * SWA_WINDOW
    q0 = pl.program_id(1) * tq
    ks = pl.multiple_of(jnp.clip(q0 - SWA_WINDOW, 0, s_lat - tk), SWA_WINDOW)
    kwin = k_ref[0, pl.ds(ks, tk), :]
    vwin = v_ref[0, pl.ds(ks, tk), :]
    kc = k_ref[0, s_lat:s_lat + c_len, :]
    vc = v_ref[0, s_lat:s_lat + c_len, :]
    qpos = q0 + lax.broadcasted_iota(I32, (tq, tk), 0)
    kpos = ks + lax.broadcasted_iota(I32, (tq, tk), 1)
    valid = jnp.abs(qpos - kpos) <= SWA_WINDOW
    valid = jnp.concatenate([valid, valid], axis=0)
    for g in range(2):
        q2 = _gqa_rows(q_ref, g)
        s_loc = jnp.where(valid, _dot_nt(q2, kwin), NEG_INF)
        s_ctx = _dot_nt(q2, kc)
        snk = _gqa_sink(sink_ref, g, tq)
        m = jnp.maximum(jnp.maximum(jnp.max(s_loc, axis=-1, keepdims=True),
                                    jnp.max(s_ctx, axis=-1, keepdims=True)), snk)
        e_loc = jnp.exp(s_loc - m)
        e_ctx = jnp.exp(s_ctx - m)
        den = (jnp.sum(e_loc, axis=-1, keepdims=True) + jnp.sum(e_ctx, axis=-1, keepdims=True)
               + jnp.exp(snk - m))
        o = (_dot(e_loc.astype(BF16), vwin) + _dot(e_ctx.astype(BF16), vc)) / den
        _gqa_store(o_ref, g, o, tq)


def _swa(sink, qa, ka, va, s_lat):
    b, sall, _ = qa.shape
    kv = pl.BlockSpec((1, sall, LANES), lambda i, j: (i, 0, 0))
    return pl.pallas_call(
        functools.partial(_swa_kernel, s_lat=s_lat, c_len=sall - s_lat),
        grid=(b, s_lat // ATT_TQ),
        in_specs=[pl.BlockSpec(memory_space=pltpu.SMEM),
                  pl.BlockSpec((1, ATT_TQ, 512), lambda i, j: (i, j, 0)), kv, kv],
        out_specs=pl.BlockSpec((1, ATT_TQ, 256), lambda i, j: (i, j, 0)),
        out_shape=jax.ShapeDtypeStruct((b, s_lat, 256), BF16),
        compiler_params=_params(("arbitrary", "arbitrary")),
        name="swa",
    )(sink, qa, ka, va)


def _na_kernel(q_ref, k_ref, v_ref, bm_ref, o_ref, *, s_lat, c_len):
    rows = s_lat // GRID_W
    nk = NA_KROWS * GRID_W
    r0 = pl.program_id(1) * NA_QROWS
    ks = pl.multiple_of(jnp.clip(r0 - NA_KH // 2, 0, rows - NA_KROWS) * GRID_W, GRID_W)
    for c in range(2):
        sl = slice(c * LANES, (c + 1) * LANES)
        kwin = k_ref[0, pl.ds(ks, nk), sl]
        vwin = v_ref[0, pl.ds(ks, nk), sl]
        kc = k_ref[0, s_lat:s_lat + c_len, sl]
        vc = v_ref[0, s_lat:s_lat + c_len, sl]
        q = q_ref[0, :, sl]
        outs = []
        for half in range(2):
            qm = _half_mask(q, half)
            s_loc = _dot_nt(qm, kwin) + bm_ref[0, 2 * c + half]
            s_ctx = _dot_nt(qm, kc)
            m = jnp.maximum(jnp.max(s_loc, axis=-1, keepdims=True), jnp.max(s_ctx, axis=-1, keepdims=True))
            e_loc = jnp.exp(s_loc - m)
            e_ctx = jnp.exp(s_ctx - m)
            den = jnp.sum(e_loc, axis=-1, keepdims=True) + jnp.sum(e_ctx, axis=-1, keepdims=True)
            outs.append((_dot(e_loc.astype(BF16), vwin) + _dot(e_ctx.astype(BF16), vc)) / den)
        o_ref[0, :, sl] = _merge_halves(outs[0], outs[1]).astype(BF16)


def _na_bias_mask(rpb, rows):
    nq = NA_QROWS * GRID_W
    nk = NA_KROWS * GRID_W
    col = np.arange(GRID_W)
    cstart = np.clip(col - NA_KW // 2, 0, GRID_W - NA_KW)
    col_ok = (col[None, :] >= cstart[:, None]) & (col[None, :] < cstart[:, None] + NA_KW)
    dc = np.clip(col[None, :] - col[:, None], -(NA_KW - 1), NA_KW - 1) + NA_KW - 1
    sel_c = np.where(col_ok[:, :, None], np.eye(2 * NA_KW - 1)[dc], 0.0)
    tabs = []
    for r0 in (0, NA_QROWS, rows - NA_QROWS):
        ksr = int(np.clip(r0 - NA_KH // 2, 0, rows - NA_KROWS))
        r = r0 + np.arange(NA_QROWS)
        kr = ksr + np.arange(NA_KROWS)
        rs = np.clip(r - NA_KH // 2, 0, rows - NA_KH)
        row_ok = (kr[None, :] >= rs[:, None]) & (kr[None, :] < rs[:, None] + NA_KH)
        dr = np.clip(kr[None, :] - r[:, None], -(NA_KH - 1), NA_KH - 1) + NA_KH - 1
        sel_r = np.where(row_ok[:, :, None], np.eye(2 * NA_KH - 1)[dr], 0.0)
        ok = (row_ok[:, None, :, None] & col_ok[None, :, None, :]).reshape(nq, nk)
        bias = jnp.einsum("hdc,rjd,qkc->hrqjk", rpb.astype(F32), jnp.asarray(sel_r, F32), jnp.asarray(sel_c, F32),
                          precision=lax.Precision.HIGHEST).reshape(N_HEADS, nq, nk)
        tabs.append(jnp.where(ok[None], bias, NEG_INF))
    return jnp.stack(tabs)


def _na(qb, kb, vb, bm, s_lat):
    b, sall, _ = qb.shape
    nq = NA_QROWS * GRID_W
    nsteps = s_lat // nq
    kv = pl.BlockSpec((1, sall, 256), lambda i, j: (i, 0, 0))

    def bm_map(i, j):
        return (jnp.where(j == 0, 0, jnp.where(j == nsteps - 1, 2, 1)), 0, 0, 0)

    return pl.pallas_call(
        functools.partial(_na_kernel, s_lat=s_lat, c_len=sall - s_lat),
        grid=(b, nsteps),
        in_specs=[pl.BlockSpec((1, nq, 256), lambda i, j: (i, j, 0)), kv, kv,
                  pl.BlockSpec((1, N_HEADS, nq, NA_KROWS * GRID_W), bm_map)],
        out_specs=pl.BlockSpec((1, nq, 256), lambda i, j: (i, j, 0)),
        out_shape=jax.ShapeDtypeStruct((b, s_lat, 256), BF16),
        compiler_params=_params(("arbitrary", "arbitrary")),
        name="na2d",
    )(qb, kb, vb, bm)


def _diff_kernel(lamv_ref, subg_ref, q_ref, k_ref, vt_ref, o_ref, e_scr, m_scr, *, n_keys, lam_init):
    tq = ATT_TQ
    chunks = [(c0, min(DIFF_KCHUNK, n_keys - c0)) for c0 in range(0, n_keys, DIFF_KCHUNK)]
    lam = _lam(lamv_ref, lam_init)
    stats = []
    for hh in range(DIFF_HEADS_PER_STEP):
        sl = slice(hh * LANES, (hh + 1) * LANES)
        q = q_ref[0, :, sl]
        q12 = jnp.concatenate([_half_mask(q, 0), _half_mask(q, 1)], axis=0)
        m = jnp.full((1, 2 * tq), NEG_INF, F32)
        den = jnp.zeros((1, 2 * tq), F32)
        for i, (c0, n) in enumerate(chunks):
            s = _dot_nt(k_ref[0, c0:c0 + n, sl], q12)
            m_new = jnp.maximum(m, jnp.max(s, axis=0, keepdims=True))
            e = jnp.exp2(s - m_new)
            e_scr[hh, c0:c0 + n, :] = e
            m_scr[hh, i:i + 1, :] = m_new
            den = den * jnp.exp2(m - m_new) + jnp.sum(e, axis=0, keepdims=True)
            m = m_new
        stats.append((m, den))
    for hh in range(DIFF_HEADS_PER_STEP):
        sl = slice(hh * LANES, (hh + 1) * LANES)
        m, den = stats[hh]
        acc = jnp.zeros((LANES, tq), F32)
        for i, (c0, n) in enumerate(chunks):
            f = jnp.exp2(m_scr[hh, i:i + 1, :] - m) / den
            p = e_scr[hh, c0:c0 + n, :tq] * f[:, :tq] - e_scr[hh, c0:c0 + n, tq:] * (lam * f[:, tq:])
            acc = acc + _dot(vt_ref[0, sl, c0:c0 + n], p.astype(BF16))
        o_ref[0, :, sl] = _diff_post(acc.T, subg_ref[...], lam_init).astype(BF16)


def _diff(lamv, subg, qd, kd, vdt, s_lat, lam_init):
    b, sall, _ = qd.shape
    nchunk = -(-sall // DIFF_KCHUNK)
    hps = DIFF_HEADS_PER_STEP
    wid = hps * LANES
    return pl.pallas_call(
        functools.partial(_diff_kernel, n_keys=sall, lam_init=lam_init),
        grid=(b, N_HEADS // hps, s_lat // ATT_TQ),
        in_specs=[pl.BlockSpec((8, LANES), lambda i, h, j: (0, 0)),
                  pl.BlockSpec((1, LANES), lambda i, h, j: (0, 0)),
                  pl.BlockSpec((1, ATT_TQ, wid), lambda i, h, j: (i, j, h)),
                  pl.BlockSpec((1, sall, wid), lambda i, h, j: (i, 0, h)),
                  pl.BlockSpec((1, wid, sall), lambda i, h, j: (i, h, 0))],
        out_specs=pl.BlockSpec((1, ATT_TQ, wid), lambda i, h, j: (i, j, h)),
        out_shape=jax.ShapeDtypeStruct((b, s_lat, 512), BF16),
        scratch_shapes=[pltpu.VMEM((hps, sall, 2 * ATT_TQ), F32),
                        pltpu.VMEM((hps, -(-nchunk // 8) * 8, 2 * ATT_TQ), F32)],
        compiler_params=_params(("arbitrary", "arbitrary", "arbitrary")),
        name="diff_attn",
    )(lamv, subg, qd, kd, vdt)


def _ctx_kernel(sink_ref, lamv_ref, subg_ref, qa_ref, ka_ref, va_ref, qb_ref, kb_ref, vb_ref,
                qd_ref, kd_ref, vd_ref, oa_ref, ob_ref, od_ref, *, lam_init):
    cl = qa_ref.shape[1]
    ka = ka_ref[0]
    va = va_ref[0]
    for g in range(2):
        q2 = _gqa_rows(qa_ref, g)
        s = _dot_nt(q2, ka)
        snk = _gqa_sink(sink_ref, g, cl)
        m = jnp.maximum(jnp.max(s, axis=-1, keepdims=True), snk)
        e = jnp.exp(s - m)
        den = jnp.sum(e, axis=-1, keepdims=True) + jnp.exp(snk - m)
        _gqa_store(oa_ref, g, _dot(e.astype(BF16), va) / den, cl)
    for c in range(2):
        sl = slice(c * LANES, (c + 1) * LANES)
        outs = []
        for half in range(2):
            s = _dot_nt(_half_mask(qb_ref[0, :, sl], half), kb_ref[0, :, sl])
            e = jnp.exp(s - jnp.max(s, axis=-1, keepdims=True))
            outs.append(_dot(e.astype(BF16), vb_ref[0, :, sl]) / jnp.sum(e, axis=-1, keepdims=True))
        ob_ref[0, :, sl] = _merge_halves(outs[0], outs[1]).astype(BF16)
    lam = _lam(lamv_ref, lam_init)
    for h in range(N_HEADS):
        sl = slice(h * LANES, (h + 1) * LANES)
        q = qd_ref[0, :, sl]
        q12 = jnp.concatenate([_half_mask(q, 0), _half_mask(q, 1)], axis=0)
        s = _dot_nt(q12, kd_ref[0, :, sl])
        e = jnp.exp2(s - jnp.max(s, axis=-1, keepdims=True))
        den = jnp.sum(e, axis=-1, keepdims=True)
        p = e[:cl] * (1.0 / den[:cl]) - e[cl:] * (lam / den[cl:])
        o = _dot(p.astype(BF16), vd_ref[0, :, sl])
        od_ref[0, :, sl] = _diff_post(o, subg_ref[...], lam_init).astype(BF16)


def _ctx_attention(sink, lamv, subg, qa, ka, va, qb, kb, vb, qd, kd, vd, s_lat, lam_init):
    b, sall, _ = qa.shape
    cl = sall - s_lat
    blk = s_lat // cl
    row = lambda width: pl.BlockSpec((1, cl, width), lambda i: (i, blk, 0))
    out = lambda width: pl.BlockSpec((1, cl, width), lambda i: (i, 0, 0))
    return pl.pallas_call(
        functools.partial(_ctx_kernel, lam_init=lam_init),
        grid=(b,),
        in_specs=[pl.BlockSpec(memory_space=pltpu.SMEM),
                  pl.BlockSpec((8, LANES), lambda i: (0, 0)),
                  pl.BlockSpec((1, LANES), lambda i: (0, 0)),
                  row(512), row(128), row(128), row(256), row(256), row(256), row(512), row(512), row(512)],
        out_specs=[out(256), out(256), out(512)],
        out_shape=[jax.ShapeDtypeStruct((b, cl, w), BF16) for w in (256, 256, 512)],
        compiler_params=_params(("arbitrary",)),
        name="ctx_attention",
    )(sink, lamv, subg, qa, ka, va, qb, kb, vb, qd, kd, vd)


def _out_kernel(x_ref, w_ref, mod_ref, *refs, nlat):
    xo_ref = refs[-1]

    def emit(oa_ref, ob_ref, od_ref):
        attn = (_dot(oa_ref[0], w_ref[0:256, :]) + _dot(ob_ref[0], w_ref[256:512, :])
                + _dot(od_ref[0], w_ref[512:1024, :]))
        xo_ref[0] = x_ref[0] + mod_ref[0, 2:3, :] * attn

    if len(refs) == 4:
        emit(*refs[:3])
    else:
        is_ctx = pl.program_id(1) >= nlat
        pl.when(jnp.logical_not(is_ctx))(lambda: emit(*refs[:3]))
        pl.when(is_ctx)(lambda: emit(*refs[3:6]))


def _out_projection(xc, lat, ctx, w_out, modall, s_lat):
    b, sall, d = xc.shape
    tm = TOK_TILE
    nlat = s_lat // tm
    n_rows = s_lat if ctx is None else sall
    row = lambda width: pl.BlockSpec((1, tm, width), lambda i, j: (i, j, 0))
    lat_row = lambda width: pl.BlockSpec((1, tm, width), lambda i, j: (i, jnp.minimum(j, nlat - 1), 0))
    ctx_row = lambda width: pl.BlockSpec((1, tm, width), lambda i, j: (i, 0, 0))
    widths = (256, 256, 512)
    specs = [lat_row(w) for w in widths] + ([] if ctx is None else [ctx_row(w) for w in widths])
    return pl.pallas_call(
        functools.partial(_out_kernel, nlat=nlat),
        grid=(b, n_rows // tm),
        in_specs=[row(d),
                  pl.BlockSpec((MIX_WIDTH, d), lambda i, j: (0, 0)),
                  pl.BlockSpec((1, 8, d), lambda i, j: (2 * i + (j >= nlat).astype(I32), 0, 0))] + specs,
        out_specs=row(d),
        out_shape=jax.ShapeDtypeStruct((b, n_rows, d), F32),
        compiler_params=_params(("arbitrary", "arbitrary")),
        name="out_projection",
    )(xc, w_out, modall, *lat, *(() if ctx is None else ctx))


def _first_argmax(v, iota, n):
    m = jnp.max(v, axis=0, keepdims=True)
    ix = jnp.min(jnp.where(v == m, iota, float(n)), axis=0, keepdims=True)
    return m, ix


def _router_kernel(x_ref, g_ref, mod_ref, whi_ref, wlo_ref, bias_ref, tri_ref, ltri_ref,
                   w_ref, slot_ref, meta_ref, cnt_ref):
    @pl.when(pl.program_id(0) == 0)
    def _():
        cnt_ref[...] = jnp.zeros_like(cnt_ref)

    hf = _norm_mod(x_ref[...], g_ref[...], mod_ref[0, 3:4, :], mod_ref[0, 4:5, :])
    tm = hf.shape[0]
    h_hi, h_lo = _split_bf16(hf)
    whi = whi_ref[...]
    logits = _dot_nt(whi, h_hi) + _dot_nt(whi, h_lo) + _dot_nt(wlo_ref[...], h_hi)
    scores = _sigmoid(logits)
    biased = scores + bias_ref[...]
    gsz = N_EXPERTS // N_GROUPS
    iota_g = lax.broadcasted_iota(I32, (gsz, tm), 0).astype(F32)
    gscore = []
    for g in range(N_GROUPS):
        v = biased[g * gsz:(g + 1) * gsz]
        m1, i1 = _first_argmax(v, iota_g, gsz)
        m2 = jnp.max(jnp.where(iota_g == i1, -jnp.inf, v), axis=0, keepdims=True)
        gscore.append(m1 + m2)
    cur = jnp.concatenate(gscore, axis=0)
    iota_n = lax.broadcasted_iota(I32, (N_GROUPS, tm), 0).astype(F32)
    gsel = jnp.zeros((N_GROUPS, tm), F32)
    for _ in range(TOPK_GROUPS):
        _, ix = _first_argmax(cur, iota_n, N_GROUPS)
        hit = iota_n == ix
        gsel = jnp.where(hit, 1.0, gsel)
        cur = jnp.where(hit, -jnp.inf, cur)
    masked = jnp.concatenate(
        [jnp.where(gsel[g:g + 1] > 0.5, biased[g * gsz:(g + 1) * gsz], -jnp.inf) for g in range(N_GROUPS)],
        axis=0)
    iota_e = lax.broadcasted_iota(I32, (N_EXPERTS, tm), 0).astype(F32)
    onehot = jnp.zeros((N_EXPERTS, tm), F32)
    idxs, ws = [], []
    for _ in range(TOP_K):
        _, ix = _first_argmax(masked, iota_e, N_EXPERTS)
        hit = iota_e == ix
        ws.append(jnp.sum(jnp.where(hit, scores, 0.0), axis=0, keepdims=True))
        masked = jnp.where(hit, -jnp.inf, masked)
        onehot = jnp.where(hit, 1.0, onehot)
        idxs.append(ix)
    base = cnt_ref[...]
    rank = _dot(onehot.astype(BF16), tri_ref[...])
    n_col = jnp.sum(onehot, axis=1, keepdims=True)
    n_pad = jnp.floor((n_col + (ROW_CHUNK - 1)) * (1.0 / ROW_CHUNK)) * ROW_CHUNK
    start = _dot(ltri_ref[...], jnp.broadcast_to(n_pad, (N_EXPERTS, LANES)).astype(BF16))
    wsum = ws[0]
    for wk in ws[1:]:
        wsum = wsum + wk
    for k in range(TOP_K):
        slot = jnp.sum(jnp.where(iota_e == idxs[k], rank + start[:, 0:1], 0.0), axis=0, keepdims=True)
        slot_ref[0, k:k + 1, :] = slot.astype(I32)
        w_ref[0, k:k + 1, :] = ws[k] / wsum * ROUTED_SCALE
    lane = lax.broadcasted_iota(I32, (N_EXPERTS, LANES), 1)
    meta_ref[0] = jnp.where(lane == 0, n_pad, jnp.where(lane == 1, base, start))
    cnt_ref[...] = base + n_pad


def _router(xflat, g, modall, whi, wlo, bias, tri, ltri, tiles_per_batch, nlat):
    t, d = xflat.shape
    tm = TOK_TILE
    nt = t // tm
    per_tok = pl.BlockSpec((1, TOP_K, tm), lambda i: (i, 0, 0))

    def mod_map(i):
        return (2 * (i // tiles_per_batch) + ((i % tiles_per_batch) >= nlat).astype(I32), 0, 0)

    return pl.pallas_call(
        _router_kernel,
        grid=(nt,),
        in_specs=[pl.BlockSpec((tm, d), lambda i: (i, 0)),
                  pl.BlockSpec((1, d), lambda i: (0, 0)),
                  pl.BlockSpec((1, 8, d), mod_map),
                  pl.BlockSpec((N_EXPERTS, d), lambda i: (0, 0)),
                  pl.BlockSpec((N_EXPERTS, d), lambda i: (0, 0)),
                  pl.BlockSpec((N_EXPERTS, 1), lambda i: (0, 0)),
                  pl.BlockSpec((tm, tm), lambda i: (0, 0)),
                  pl.BlockSpec((N_EXPERTS, N_EXPERTS), lambda i: (0, 0))],
        out_specs=[per_tok, per_tok,
                   pl.BlockSpec((1, N_EXPERTS, LANES), lambda i: (i, 0, 0)),
                   pl.BlockSpec((N_EXPERTS, 1), lambda i: (0, 0))],
        out_shape=[jax.ShapeDtypeStruct((nt, TOP_K, tm), F32),
                   jax.ShapeDtypeStruct((nt, TOP_K, tm), I32),
                   jax.ShapeDtypeStruct((nt, N_EXPERTS, LANES), F32),
                   jax.ShapeDtypeStruct((N_EXPERTS, 1), F32)],
        compiler_params=_params(("arbitrary",)),
        name="router",
    )(xflat, g, modall, whi, wlo, bias, tri, ltri)


def _pack_bf16_pairs(v):
    half = v.shape[1] // 2
    vb = v.astype(BF16).astype(F32)
    lo = lax.shift_right_logical(lax.bitcast_convert_type(vb[:, :half], U32), jnp.uint32(16))
    hi = lax.bitcast_convert_type(vb[:, half:], U32) & jnp.uint32(0xFFFF0000)
    return lo | hi


def _unpack_bf16_pairs(u):
    lo = lax.bitcast_convert_type(lax.shift_left(u, jnp.uint32(16)), F32).astype(BF16)
    hi = lax.bitcast_convert_type(u & jnp.uint32(0xFFFF0000), F32).astype(BF16)
    return lo, hi


def _chunk_copies(meta_ref, make_copy):
    total = lax.shift_right_logical(_slots_used(meta_ref), ROW_CHUNK.bit_length() - 1)

    def start(j, carry):
        make_copy(j).start()
        return carry

    lax.fori_loop(0, total, start, 0)
    return total


def _tile_slots(tm):
    return TOP_K * tm + ROW_CHUNK * N_EXPERTS


def _slots_used(meta_ref):
    last = N_EXPERTS - 1
    return meta_ref[0, 2, last] + meta_ref[0, 0, last]


def _dispatch_kernel(meta_ref, rows_ref, x_ref, g_ref, mod_ref, slot_ref, xs_ref, xc_scr, sem):
    hb = _norm_mod(x_ref[...], g_ref[...], mod_ref[0, 3:4, :], mod_ref[0, 4:5, :]).astype(BF16)
    tm = hb.shape[0]
    slot = slot_ref[0]

    def select_rows(r, carry):
        r0 = pl.multiple_of(r * tm, tm)
        row = lax.broadcasted_iota(I32, (tm, tm), 0) + r0
        sel = jnp.zeros((tm, tm), F32)
        for k in range(TOP_K):
            sel = jnp.where(row == slot[k:k + 1, :], 1.0, sel)
        xc_scr[pl.ds(r0, tm), :] = _pack_bf16_pairs(_dot(sel.astype(BF16), hb))
        return carry

    lax.fori_loop(0, (_slots_used(meta_ref) + tm - 1) // tm, select_rows, 0)

    def make_copy(j):
        src = pl.multiple_of(j * ROW_CHUNK, ROW_CHUNK)
        dst = pl.multiple_of(rows_ref[0, 0, j], ROW_CHUNK)
        return pltpu.make_async_copy(xc_scr.at[pl.ds(src, ROW_CHUNK)], xs_ref.at[pl.ds(dst, ROW_CHUNK)], sem)

    total = _chunk_copies(meta_ref, make_copy)

    def wait(j, carry):
        pltpu.make_async_copy(xc_scr.at[pl.ds(0, ROW_CHUNK)], xs_ref.at[pl.ds(0, ROW_CHUNK)], sem).wait()
        return carry

    lax.fori_loop(0, total, wait, 0)


def _dispatch(meta, chunk_rows, slot, xflat, g, modall, n_rows, tiles_per_batch, nlat):
    t, d = xflat.shape
    tm = TOK_TILE
    nt = t // tm

    def mod_map(i):
        return (2 * (i // tiles_per_batch) + ((i % tiles_per_batch) >= nlat).astype(I32), 0, 0)

    return pl.pallas_call(
        _dispatch_kernel,
        grid=(nt,),
        in_specs=[pl.BlockSpec((1, 3, N_EXPERTS), lambda i: (i, 0, 0), memory_space=pltpu.SMEM),
                  pl.BlockSpec((1, 1, chunk_rows.shape[2]), lambda i: (i, 0, 0), memory_space=pltpu.SMEM),
                  pl.BlockSpec((tm, d), lambda i: (i, 0)),
                  pl.BlockSpec((1, d), lambda i: (0, 0)),
                  pl.BlockSpec((1, 8, d), mod_map),
                  pl.BlockSpec((1, TOP_K, tm), lambda i: (i, 0, 0))],
        out_specs=pl.BlockSpec(memory_space=pl.ANY),
        out_shape=jax.ShapeDtypeStruct((n_rows, d // 2), U32),
        scratch_shapes=[pltpu.VMEM((_tile_slots(tm), d // 2), U32), pltpu.SemaphoreType.DMA(())],
        compiler_params=_params(("arbitrary",)),
        name="moe_dispatch",
    )(meta, chunk_rows, xflat, g, modall, slot)


def _expert_kernel(blk_e_ref, nvalid_ref, nact_ref, xs_ref, wg_ref, wu_ref, wd_ref, ys_ref, wg_s, wu_s, wd_s):
    b = pl.program_id(0)
    changed = jnp.logical_or(b == 0, blk_e_ref[b] != blk_e_ref[jnp.maximum(b - 1, 0)])

    @pl.when(changed)
    def _():
        wg_s[...] = wg_ref[0, 0].astype(BF16)
        wu_s[...] = wu_ref[0, 0].astype(BF16)
        wd_s[...] = wd_ref[0, 0].astype(BF16)

    @pl.when(b < nact_ref[0])
    def _():
        half = xs_ref.shape[1]
        row = lax.broadcasted_iota(I32, xs_ref.shape, 0)
        xs = jnp.where(row < nvalid_ref[b], xs_ref[...], jnp.zeros(xs_ref.shape, U32))
        x_lo, x_hi = _unpack_bf16_pairs(xs)
        gate = _dot(x_lo, wg_s[0:half, :]) + _dot(x_hi, wg_s[half:, :])
        up = _dot(x_lo, wu_s[0:half, :]) + _dot(x_hi, wu_s[half:, :])
        mid = (gate * _sigmoid(gate)) * up
        ys_ref[...] = _pack_bf16_pairs(_dot(mid.astype(BF16), wd_s[...]))

    @pl.when(b >= nact_ref[0])
    def _():
        ys_ref[...] = jnp.zeros_like(ys_ref)


def _experts(blk_e, nvalid, nact, xs, w_gate, w_up, w_down, layer):
    n_rows, half = xs.shape
    _, _, d, f = w_gate.shape
    n_blk = n_rows // MOE_BLOCK
    grid_spec = pltpu.PrefetchScalarGridSpec(
        num_scalar_prefetch=3,
        grid=(n_blk,),
        in_specs=[pl.BlockSpec((MOE_BLOCK, half), lambda i, be, nv, na: (jnp.minimum(i, na[0] - 1), 0)),
                  pl.BlockSpec((1, 1, d, f), lambda i, be, nv, na: (layer, be[i], 0, 0)),
                  pl.BlockSpec((1, 1, d, f), lambda i, be, nv, na: (layer, be[i], 0, 0)),
                  pl.BlockSpec((1, 1, f, d), lambda i, be, nv, na: (layer, be[i], 0, 0))],
        out_specs=pl.BlockSpec((MOE_BLOCK, half), lambda i, be, nv, na: (i, 0)),
        scratch_shapes=[pltpu.VMEM((d, f), BF16), pltpu.VMEM((d, f), BF16), pltpu.VMEM((f, d), BF16)],
    )
    return pl.pallas_call(
        _expert_kernel,
        grid_spec=grid_spec,
        out_shape=jax.ShapeDtypeStruct((n_rows, half), U32),
        compiler_params=_params(("arbitrary",)),
        name="moe_experts",
    )(blk_e, nvalid, nact, xs, w_gate, w_up, w_down)


def _combine_kernel(meta_ref, rows_ref, x_ref, g_ref, mod_ref, spos_ref, wt_ref, wsg_ref, wsu_ref, wsd_ref,
                    gfin_ref, ys_ref, o_ref, y_scr, acc_scr, sb_scr, wb_scr, sem, *, final_norm):
    tm, d = x_ref.shape
    half = d // 2

    @pl.when(pl.program_id(0) == 0)
    def _():
        y_scr[...] = jnp.zeros_like(y_scr)

    def make_copy(j):
        src = pl.multiple_of(rows_ref[0, 0, j], ROW_CHUNK)
        dst = pl.multiple_of(j * ROW_CHUNK, ROW_CHUNK)
        return pltpu.make_async_copy(ys_ref.at[pl.ds(src, ROW_CHUNK)], y_scr.at[pl.ds(dst, ROW_CHUNK)], sem)

    total = _chunk_copies(meta_ref, make_copy)

    x = x_ref[...]
    hb = _norm_mod(x, g_ref[...], mod_ref[0, 3:4, :], mod_ref[0, 4:5, :]).astype(BF16)
    gate = _dot(hb, wsg_ref[...])
    mid = (gate * _sigmoid(gate)) * _dot(hb, wsu_ref[...])
    acc_scr[...] = _dot(mid.astype(BF16), wsd_ref[...])

    def wait(j, carry):
        pltpu.make_async_copy(ys_ref.at[pl.ds(0, ROW_CHUNK)], y_scr.at[pl.ds(0, ROW_CHUNK)], sem).wait()
        return carry

    lax.fori_loop(0, total, wait, 0)

    n_used = _slots_used(meta_ref)
    for k in range(TOP_K):
        sb_scr[k] = jnp.broadcast_to(spos_ref[0, :, k:k + 1], (tm, LANES))
        wb_scr[k] = jnp.broadcast_to(wt_ref[0, :, k:k + 1], (tm, LANES))

    def chunk(j, carry):
        r0 = pl.multiple_of(j * COMBINE_KCHUNK, COMBINE_KCHUNK)
        parts = []
        for c in range(COMBINE_KCHUNK // LANES):
            col = lax.broadcasted_iota(I32, (tm, LANES), 1) + (r0 + c * LANES)
            p = jnp.zeros((tm, LANES), F32)
            for k in range(TOP_K):
                p = jnp.where(col == sb_scr[k], wb_scr[k], p)
            parts.append(p.astype(BF16))
        pb = jnp.concatenate(parts, axis=1)
        y_lo, y_hi = _unpack_bf16_pairs(y_scr[pl.ds(r0, COMBINE_KCHUNK), :])
        acc_scr[:, :half] += _dot(pb, y_lo)
        acc_scr[:, half:] += _dot(pb, y_hi)
        return carry

    lax.fori_loop(0, (n_used + COMBINE_KCHUNK - 1) // COMBINE_KCHUNK, chunk, 0)
    out = x + mod_ref[0, 5:6, :] * acc_scr[...]
    if final_norm:
        out = _rms(out, gfin_ref[...])
    o_ref[...] = out


def _combine(meta, chunk_rows, spos_t, wt, xflat, g, modall, wsg, wsu, wsd, gfin, ys, tiles_per_batch, nlat,
             final_norm):
    t, d = xflat.shape
    f = wsg.shape[1]
    tm = TOK_TILE
    nt = t // tm
    y_rows = _tile_slots(tm)
    assert y_rows % COMBINE_KCHUNK == 0

    def mod_map(i):
        return (2 * (i // tiles_per_batch) + ((i % tiles_per_batch) >= nlat).astype(I32), 0, 0)

    return pl.pallas_call(
        functools.partial(_combine_kernel, final_norm=final_norm),
        grid=(nt,),
        in_specs=[pl.BlockSpec((1, 3, N_EXPERTS), lambda i: (i, 0, 0), memory_space=pltpu.SMEM),
                  pl.BlockSpec((1, 1, chunk_rows.shape[2]), lambda i: (i, 0, 0), memory_space=pltpu.SMEM),
                  pl.BlockSpec((tm, d), lambda i: (i, 0)),
                  pl.BlockSpec((1, d), lambda i: (0, 0)),
                  pl.BlockSpec((1, 8, d), mod_map),
                  pl.BlockSpec((1, tm, TOP_K), lambda i: (i, 0, 0)),
                  pl.BlockSpec((1, tm, TOP_K), lambda i: (i, 0, 0)),
                  pl.BlockSpec((d, f), lambda i: (0, 0)),
                  pl.BlockSpec((d, f), lambda i: (0, 0)),
                  pl.BlockSpec((f, d), lambda i: (0, 0)),
                  pl.BlockSpec((1, d), lambda i: (0, 0)),
                  pl.BlockSpec(memory_space=pl.ANY)],
        out_specs=pl.BlockSpec((tm, d), lambda i: (i, 0)),
        out_shape=jax.ShapeDtypeStruct((t, d), F32),
        scratch_shapes=[pltpu.VMEM((y_rows, d // 2), U32), pltpu.VMEM((tm, d), F32),
                        pltpu.VMEM((TOP_K, tm, LANES), I32), pltpu.VMEM((TOP_K, tm, LANES), F32),
                        pltpu.SemaphoreType.DMA(())],
        compiler_params=_params(("arbitrary",)),
        name="moe_combine",
    )(meta, chunk_rows, xflat, g, modall, spos_t, wt, wsg, wsu, wsd, gfin, ys)


def _rope_tables(s_lat, c_len):
    n_freq = HEAD_DIM // 4
    inv = 1.0 / (ROPE_BASE ** (jnp.arange(n_freq, dtype=F32) / n_freq))
    t = jnp.arange(s_lat)
    pos = jnp.stack([(t // GRID_W).astype(F32), (t % GRID_W).astype(F32)], axis=1)
    lane = np.arange(LANES)
    ang = pos[:, (lane % HEAD_DIM) // 32] * inv[lane % 16][None, :]
    sign = jnp.asarray(np.where(lane % 32 < 16, -1.0, 1.0), F32)
    cos = jnp.concatenate([jnp.cos(ang), jnp.ones((c_len, LANES), F32)], axis=0)
    sin = jnp.concatenate([jnp.sin(ang) * sign, jnp.zeros((c_len, LANES), F32)], axis=0)
    return cos, sin


def _moe(xflat, lp, modall, gfin, tiles_per_batch, nlat, final_norm):
    t, d = xflat.shape
    nt = t // TOK_TILE
    g = lp["g_ffn"].reshape(1, d)
    w_r = lp["w_router"].T
    whi = w_r.astype(BF16)
    wlo = (w_r - whi.astype(F32)).astype(BF16)
    tri = jnp.asarray(np.triu(np.ones((TOK_TILE, TOK_TILE), np.float32), 1), BF16)
    ltri = jnp.asarray(np.tril(np.ones((N_EXPERTS, N_EXPERTS), np.float32), -1), BF16)
    w, slot, meta, cnt = _router(xflat, g, modall, whi, wlo, lp["router_bias"].reshape(N_EXPERTS, 1),
                                 tri, ltri, tiles_per_batch, nlat)
    meta = jnp.transpose(meta[:, :, :3].astype(I32), (0, 2, 1))
    rows_e = cnt[:, 0].astype(I32)
    seg = (rows_e + MOE_BLOCK - 1) // MOE_BLOCK * MOE_BLOCK
    pend = jnp.cumsum(seg)
    pstart = (pend - seg).astype(I32)
    n_blk = (t * TOP_K + ROW_CHUNK * N_EXPERTS * nt) // MOE_BLOCK + N_EXPERTS
    blk_row0 = jnp.arange(n_blk, dtype=I32) * MOE_BLOCK
    blk_e = jnp.minimum(jnp.sum((pend[None, :] <= blk_row0[:, None]).astype(I32), axis=1), N_EXPERTS - 1)
    of_blk = blk_e[:, None] == jnp.arange(N_EXPERTS, dtype=I32)[None, :]
    seg_end = jnp.sum(jnp.where(of_blk, (pstart + rows_e)[None, :], 0), axis=1)
    nvalid = jnp.clip(seg_end - blk_row0, 0, MOE_BLOCK).astype(I32)
    nact = (pend[-1] // MOE_BLOCK).astype(I32).reshape(1)
    n_pad, base, start = meta[:, 0, :], meta[:, 1, :], meta[:, 2, :]
    chunk_slot = jnp.arange(_tile_slots(TOK_TILE) // ROW_CHUNK, dtype=I32) * ROW_CHUNK
    s = chunk_slot[None, :, None]
    in_run = (start[:, None, :] <= s) & (s < (start + n_pad)[:, None, :])
    row0 = pstart[None, :] + base - start
    chunk_rows = (jnp.sum(jnp.where(in_run, row0[:, None, :], 0), axis=2) + chunk_slot[None, :])[:, None, :]
    xs = _dispatch(meta, chunk_rows, slot, xflat, g, modall, n_blk * MOE_BLOCK, tiles_per_batch, nlat)
    ys = _experts(blk_e, nvalid, nact, xs, lp["w_gate"], lp["w_up"], lp["w_down"], lp["layer"])
    return _combine(meta, chunk_rows, jnp.transpose(slot, (0, 2, 1)), jnp.transpose(w, (0, 2, 1)), xflat, g, modall,
                    lp["ws_gate"].astype(BF16), lp["ws_up"].astype(BF16), lp["ws_down"].astype(BF16), gfin, ys,
                    tiles_per_batch, nlat, final_norm)


def _layer(xc, cs, cos, sin, lp, layer_idx, last, s_lat, gfin):
    b, sall, d = xc.shape
    c_len = sall - s_lat
    lam_init = 0.8 - 0.6 * math.exp(-0.3 * layer_idx)
    mod = _modulation(cs, lp["w_mod"], lp["b_mod"])
    mod_lat = mod[:b].reshape(b, 1, 6, d)
    mod_ctx = jnp.broadcast_to(mod[b].reshape(1, 1, 6, d), (b, 1, 6, d))
    modall = jnp.concatenate([mod_lat, mod_ctx], axis=1)
    modall = jnp.pad(modall, ((0, 0), (0, 0), (0, 2), (0, 0))).reshape(2 * b, 8, d)

    qa, qb, qd, ka, va, kb, vb, kd, vd, vdt = _projection(
        xc, lp["g_mix"].reshape(1, d), modall, cos, sin, lp["w_in"].astype(BF16), s_lat)
    sink = lp["attn_sink"].astype(F32)
    lamv = jnp.zeros((8, LANES), F32).at[0:4, 0:HEAD_DIM].set(
        jnp.stack([lp["lam_q1"], lp["lam_k1"], lp["lam_q2"], lp["lam_k2"]]).astype(F32))
    subg = lp["subln_g"].reshape(1, LANES).astype(F32)
    oa = _swa(sink, qa, ka, va, s_lat)
    ob = _na(qb, kb, vb, _na_bias_mask(lp["na_rpb"], s_lat // GRID_W), s_lat)
    od = _diff(lamv, subg, qd, kd, vdt, s_lat, lam_init)
    ctx_out = None
    if not last:
        ctx_out = _ctx_attention(sink, lamv, subg, qa, ka, va, qb, kb, vb, qd, kd, vd, s_lat, lam_init)
    n_rows = s_lat if last else sall
    x2 = _out_projection(xc, (oa, ob, od), ctx_out, lp["w_out"].astype(BF16), modall, s_lat)
    tiles_per_batch = n_rows // TOK_TILE
    y = _moe(x2.reshape(b * n_rows, d), lp, modall, gfin, tiles_per_batch, s_lat // TOK_TILE, last)
    return y.reshape(b, n_rows, d)


_LAYER_KEYS = ("w_mod", "b_mod", "g_mix", "g_ffn", "w_in", "w_out", "attn_sink", "na_rpb", "lam_q1", "lam_k1",
               "lam_q2", "lam_k2", "subln_g", "w_router", "router_bias", "w_gate", "w_up", "w_down",
               "ws_gate", "ws_up", "ws_down")


def kernel(x, c, ctx, c_ctx, w_mod, b_mod, g_mix, g_ffn, w_in, w_out, attn_sink, na_rpb, lam_q1, lam_k1, lam_q2,
           lam_k2, subln_g, w_router, router_bias, w_gate, w_up, w_down, ws_gate, ws_up, ws_down, g_final):
    stacked = dict(zip(_LAYER_KEYS, (w_mod, b_mod, g_mix, g_ffn, w_in, w_out, attn_sink, na_rpb, lam_q1, lam_k1,
                                     lam_q2, lam_k2, subln_g, w_router, router_bias, w_gate, w_up, w_down,
                                     ws_gate, ws_up, ws_down)))
    b, s_lat, d = x.shape
    c_len = ctx.shape[1]
    depth = w_mod.shape[0]
    assert s_lat % (NA_QROWS * GRID_W) == 0 and s_lat % c_len == 0 and c_len == TOK_TILE
    cs = jnp.zeros((16, d), F32).at[:b].set(c).at[b].set(c_ctx)
    cos, sin = _rope_tables(s_lat, c_len)
    xc = jnp.concatenate([x, ctx], axis=1)
    gfin = g_final.reshape(1, d)
    for i in range(depth):
        big = ("w_gate", "w_up", "w_down")
        lp = {k: (v if k in big else v[i]) for k, v in stacked.items()}
        lp["layer"] = i
        xc = _layer(xc, cs, cos, sin, lp, i, i == depth - 1, s_lat, gfin)
    return xc
```

```python
import functools
import math

import numpy as np
import jax
import jax.numpy as jnp
from jax import lax
from jax.experimental import pallas as pl
from jax.experimental.pallas import tpu as pltpu

F32 = jnp.float32
BF16 = jnp.bfloat16
I32 = jnp.int32
U32 = jnp.uint32

GRID_W = 64
HEAD_DIM = 64
ROPE_BASE = 10000.0
SWA_WINDOW = 128
NA_KH = 8
NA_KW = 16
N_HEADS = 4
N_EXPERTS = 256
TOP_K = 8
N_GROUPS = 8
TOPK_GROUPS = 4
ROUTED_SCALE = 2.5
MOE_BLOCK = 1024
ROW_CHUNK = 8
COMBINE_KCHUNK = 512
CHUNK_UNROLL = 2
EPS = 1e-6
NEG_INF = -1e30
LOG2E = math.log2(math.e)
Q_COLS = 1024
IN_COLS = 2816
MIX_WIDTH = 1024

LANES = 128
TOK_TILE = 256
ATT_TQ = 256
NA_QROWS = 4
NA_KROWS = NA_QROWS + NA_KH - 1
DIFF_KCHUNK = 256
DIFF_HEADS_PER_STEP = 2
VMEM_LIMIT = 48 * 1024 * 1024


def _params(sem):
    return pltpu.CompilerParams(dimension_semantics=sem, vmem_limit_bytes=VMEM_LIMIT)


def _dot(a, b):
    return jnp.dot(a, b, preferred_element_type=F32)


def _dot_nt(a, b):
    return lax.dot_general(a, b, (((1,), (1,)), ((), ())), preferred_element_type=F32)


def _split_bf16(a):
    hi = a.astype(BF16)
    lo = (a - hi.astype(F32)).astype(BF16)
    return hi, lo


def _sigmoid(x):
    return 1.0 / (1.0 + jnp.exp(-x))


def _rms(x, g):
    return x * lax.rsqrt(jnp.mean(x * x, axis=-1, keepdims=True) + EPS) * g


def _norm_mod(x, g, shift, scale):
    return _rms(x, g) * (1.0 + scale) + shift


def _mod_kernel(c_ref, w_ref, b_ref, o_ref):
    c = c_ref[...]
    a_hi, a_lo = _split_bf16(c * _sigmoid(c))
    w_hi, w_lo = _split_bf16(w_ref[...])
    o_ref[...] = _dot(a_hi, w_hi) + _dot(a_hi, w_lo) + _dot(a_lo, w_hi) + b_ref[...]


def _modulation(cs, w_mod, b_mod):
    n, d = cs.shape
    cols = w_mod.shape[1]
    tn = 1536
    return pl.pallas_call(
        _mod_kernel,
        grid=(cols // tn,),
        in_specs=[pl.BlockSpec((n, d), lambda j: (0, 0)),
                  pl.BlockSpec((d, tn), lambda j: (0, j)),
                  pl.BlockSpec((1, tn), lambda j: (0, j))],
        out_specs=pl.BlockSpec((n, tn), lambda j: (0, j)),
        out_shape=jax.ShapeDtypeStruct((n, cols), F32),
        compiler_params=_params(("arbitrary",)),
        name="modulation",
    )(cs, w_mod, b_mod.reshape(1, cols))


def _proj_kernel(x_ref, g_ref, mod_ref, cos_ref, sin_ref, w_ref,
                 qa_ref, qb_ref, qd_ref, ka_ref, va_ref, kb_ref, vb_ref, kd_ref, vd_ref, vdt_ref, h_scr):
    h = _norm_mod(x_ref[0], g_ref[...], mod_ref[0, 0:1, :], mod_ref[0, 1:2, :])
    h_scr[...] = h.astype(BF16)
    cos = cos_ref[...]
    sin = sin_ref[...]
    tm = h.shape[0]
    lane = lax.broadcasted_iota(I32, (tm, LANES), 1)
    first16 = (lane & 16) == 0
    lo64 = lane < HEAD_DIM

    def rope(v):
        partner = jnp.where(first16, pltpu.roll(v, LANES - 16, 1), pltpu.roll(v, 16, 1))
        return v * cos + partner * sin

    def mm(c0):
        return _dot(h_scr[...], w_ref[:, c0:c0 + 2 * LANES])

    a = mm(0) * (HEAD_DIM ** -0.5)
    c0 = rope(a[:, :LANES])
    c1 = rope(a[:, LANES:])
    zero = jnp.zeros_like(c0)
    qa_ref[0, :, 0 * LANES:1 * LANES] = jnp.where(lo64, c0, zero).astype(BF16)
    qa_ref[0, :, 1 * LANES:2 * LANES] = jnp.where(lo64, pltpu.roll(c0, HEAD_DIM, 1), zero).astype(BF16)
    qa_ref[0, :, 2 * LANES:3 * LANES] = jnp.where(lo64, zero, pltpu.roll(c1, HEAD_DIM, 1)).astype(BF16)
    qa_ref[0, :, 3 * LANES:4 * LANES] = jnp.where(lo64, zero, c1).astype(BF16)
    qb_ref[0] = (mm(256) * (HEAD_DIM ** -0.5)).astype(BF16)
    for j in range(2):
        a = mm(512 + 256 * j) * (HEAD_DIM ** -0.5 * LOG2E)
        qd_ref[0, :, 256 * j:256 * j + LANES] = rope(a[:, :LANES]).astype(BF16)
        qd_ref[0, :, 256 * j + LANES:256 * (j + 1)] = rope(a[:, LANES:]).astype(BF16)
    a = mm(1024)
    ka_ref[0] = rope(a[:, :LANES]).astype(BF16)
    va_ref[0] = a[:, LANES:].astype(BF16)
    kb_ref[0] = mm(1280).astype(BF16)
    vb_ref[0] = mm(1536).astype(BF16)
    for j in range(2):
        a = mm(1792 + 256 * j)
        kd_ref[0, :, 256 * j:256 * j + LANES] = rope(a[:, :LANES]).astype(BF16)
        kd_ref[0, :, 256 * j + LANES:256 * (j + 1)] = rope(a[:, LANES:]).astype(BF16)
        a = mm(2304 + 256 * j)
        vd_ref[0, :, 256 * j:256 * (j + 1)] = a.astype(BF16)
        vdt_ref[0, 256 * j:256 * j + LANES, :] = a[:, :LANES].T.astype(BF16)
        vdt_ref[0, 256 * j + LANES:256 * (j + 1), :] = a[:, LANES:].T.astype(BF16)


def _projection(xc, g, modall, cos, sin, w_in, s_lat):
    b, sall, d = xc.shape
    tm = TOK_TILE
    nlat = s_lat // tm
    row = lambda width: pl.BlockSpec((1, tm, width), lambda i, j: (i, j, 0))
    widths = (512, 256, 512, 128, 128, 256, 256, 512, 512)
    return pl.pallas_call(
        _proj_kernel,
        grid=(b, sall // tm),
        in_specs=[row(d),
                  pl.BlockSpec((1, d), lambda i, j: (0, 0)),
                  pl.BlockSpec((1, 8, d), lambda i, j: (2 * i + (j >= nlat).astype(I32), 0, 0)),
                  pl.BlockSpec((tm, LANES), lambda i, j: (j, 0)),
                  pl.BlockSpec((tm, LANES), lambda i, j: (j, 0)),
                  pl.BlockSpec((d, IN_COLS), lambda i, j: (0, 0))],
        out_specs=[row(w) for w in widths] + [pl.BlockSpec((1, 512, tm), lambda i, j: (i, 0, j))],
        out_shape=([jax.ShapeDtypeStruct((b, sall, w), BF16) for w in widths]
                   + [jax.ShapeDtypeStruct((b, 512, sall), BF16)]),
        scratch_shapes=[pltpu.VMEM((tm, d), BF16)],
        compiler_params=_params(("arbitrary", "arbitrary")),
        name="projection",
    )(xc, g, modall, cos, sin, w_in)


def _half_mask(q, half):
    lane = lax.broadcasted_iota(I32, q.shape, 1)
    keep = (lane < HEAD_DIM) if half == 0 else (lane >= HEAD_DIM)
    return jnp.where(keep, q, jnp.zeros_like(q))


def _merge_halves(lo_part, hi_part):
    lane = lax.broadcasted_iota(I32, lo_part.shape, 1)
    return jnp.where(lane < HEAD_DIM, lo_part, hi_part)


def _gqa_rows(q_ref, g):
    return jnp.concatenate([q_ref[0, :, (2 * g) * LANES:(2 * g + 1) * LANES],
                            q_ref[0, :, (2 * g + 1) * LANES:(2 * g + 2) * LANES]], axis=0)


def _gqa_sink(sink_ref, g, tq):
    rowi = lax.broadcasted_iota(I32, (2 * tq, 1), 0)
    return jnp.where(rowi < tq, sink_ref[2 * g], sink_ref[2 * g + 1])


def _gqa_store(o_ref, g, o, tq):
    top, bot = o[:tq], o[tq:]
    if g == 0:
        chunk = _merge_halves(top, pltpu.roll(bot, HEAD_DIM, 1))
    else:
        chunk = _merge_halves(pltpu.roll(top, HEAD_DIM, 1), bot)
    o_ref[0, :, g * LANES:(g + 1) * LANES] = chunk.astype(BF16)


def _lam(lamv_ref, lam_init):
    v = lamv_ref[...]
    a = jnp.sum(v[0:1] * v[1:2], axis=-1, keepdims=True)
    b = jnp.sum(v[2:3] * v[3:4], axis=-1, keepdims=True)
    return jnp.exp(a) - jnp.exp(b) + lam_init


def _diff_post(o, subg, lam_init):
    return _rms(o, subg) * (1.0 - lam_init)


def _swa_kernel(sink_ref, q_ref, k_ref, v_ref, o_ref, *, s_lat, c_len):
    tq = ATT_TQ
    tk = tq + 2 * SWA_WINDOW
    q0 = pl.program_id(1) * tq
    ks = pl.multiple_of(jnp.clip(q0 - SWA_WINDOW, 0, s_lat - tk), SWA_WINDOW)
    kwin = k_ref[0, pl.ds(ks, tk), :]
    vwin = v_ref[0, pl.ds(ks, tk), :]
    kc = k_ref[0, s_lat:s_lat + c_len, :]
    vc = v_ref[0, s_lat:s_lat + c_len, :]
    qpos = q0 + lax.broadcasted_iota(I32, (tq, tk), 0)
    kpos = ks + lax.broadcasted_iota(I32, (tq, tk), 1)
    valid = jnp.abs(qpos - kpos) <= SWA_WINDOW
    valid = jnp.concatenate([valid, valid], axis=0)
    for g in range(2):
        q2 = _gqa_rows(q_ref, g)
        s_loc = jnp.where(valid, _dot_nt(q2, kwin), NEG_INF)
        s_ctx = _dot_nt(q2, kc)
        snk = _gqa_sink(sink_ref, g, tq)
        m = jnp.maximum(jnp.maximum(jnp.max(s_loc, axis=-1, keepdims=True),
                                    jnp.max(s_ctx, axis=-1, keepdims=True)), snk)
        e_loc = jnp.exp(s_loc - m)
        e_ctx = jnp.exp(s_ctx - m)
        den = (jnp.sum(e_loc, axis=-1, keepdims=True) + jnp.sum(e_ctx, axis=-1, keepdims=True)
               + jnp.exp(snk - m))
        o = (_dot(e_loc.astype(BF16), vwin) + _dot(e_ctx.astype(BF16), vc)) / den
        _gqa_store(o_ref, g, o, tq)


def _swa(sink, qa, ka, va, s_lat):
    b, sall, _ = qa.shape
    kv = pl.BlockSpec((1, sall, LANES), lambda i, j: (i, 0, 0))
    return pl.pallas_call(
        functools.partial(_swa_kernel, s_lat=s_lat, c_len=sall - s_lat),
        grid=(b, s_lat // ATT_TQ),
        in_specs=[pl.BlockSpec(memory_space=pltpu.SMEM),
                  pl.BlockSpec((1, ATT_TQ, 512), lambda i, j: (i, j, 0)), kv, kv],
        out_specs=pl.BlockSpec((1, ATT_TQ, 256), lambda i, j: (i, j, 0)),
        out_shape=jax.ShapeDtypeStruct((b, s_lat, 256), BF16),
        compiler_params=_params(("arbitrary", "arbitrary")),
        name="swa",
    )(sink, qa, ka, va)


def _na_kernel(q_ref, k_ref, v_ref, bm_ref, o_ref, *, s_lat, c_len):
    rows = s_lat // GRID_W
    nk = NA_KROWS * GRID_W
    r0 = pl.program_id(1) * NA_QROWS
    ks = pl.multiple_of(jnp.clip(r0 - NA_KH // 2, 0, rows - NA_KROWS) * GRID_W, GRID_W)
    for c in range(2):
        sl = slice(c * LANES, (c + 1) * LANES)
        kwin = k_ref[0, pl.ds(ks, nk), sl]
        vwin = v_ref[0, pl.ds(ks, nk), sl]
        kc = k_ref[0, s_lat:s_lat + c_len, sl]
        vc = v_ref[0, s_lat:s_lat + c_len, sl]
        q = q_ref[0, :, sl]
        outs = []
        for half in range(2):
            qm = _half_mask(q, half)
            s_loc = _dot_nt(qm, kwin) + bm_ref[0, 2 * c + half]
            s_ctx = _dot_nt(qm, kc)
            m = jnp.maximum(jnp.max(s_loc, axis=-1, keepdims=True), jnp.max(s_ctx, axis=-1, keepdims=True))
            e_loc = jnp.exp(s_loc - m)
            e_ctx = jnp.exp(s_ctx - m)
            den = jnp.sum(e_loc, axis=-1, keepdims=True) + jnp.sum(e_ctx, axis=-1, keepdims=True)
            outs.append((_dot(e_loc.astype(BF16), vwin) + _dot(e_ctx.astype(BF16), vc)) / den)
        o_ref[0, :, sl] = _merge_halves(outs[0], outs[1]).astype(BF16)


def _na_bias_mask(rpb, rows):
    nq = NA_QROWS * GRID_W
    nk = NA_KROWS * GRID_W
    col = np.arange(GRID_W)
    cstart = np.clip(col - NA_KW // 2, 0, GRID_W - NA_KW)
    col_ok = (col[None, :] >= cstart[:, None]) & (col[None, :] < cstart[:, None] + NA_KW)
    dc = np.clip(col[None, :] - col[:, None], -(NA_KW - 1), NA_KW - 1) + NA_KW - 1
    sel_c = np.where(col_ok[:, :, None], np.eye(2 * NA_KW - 1)[dc], 0.0)
    tabs = []
    for r0 in (0, NA_QROWS, rows - NA_QROWS):
        ksr = int(np.clip(r0 - NA_KH // 2, 0, rows - NA_KROWS))
        r = r0 + np.arange(NA_QROWS)
        kr = ksr + np.arange(NA_KROWS)
        rs = np.clip(r - NA_KH // 2, 0, rows - NA_KH)
        row_ok = (kr[None, :] >= rs[:, None]) & (kr[None, :] < rs[:, None] + NA_KH)
        dr = np.clip(kr[None, :] - r[:, None], -(NA_KH - 1), NA_KH - 1) + NA_KH - 1
        sel_r = np.where(row_ok[:, :, None], np.eye(2 * NA_KH - 1)[dr], 0.0)
        ok = (row_ok[:, None, :, None] & col_ok[None, :, None, :]).reshape(nq, nk)
        bias = jnp.einsum("hdc,rjd,qkc->hrqjk", rpb.astype(F32), jnp.asarray(sel_r, F32), jnp.asarray(sel_c, F32),
                          precision=lax.Precision.HIGHEST).reshape(N_HEADS, nq, nk)
        tabs.append(jnp.where(ok[None], bias, NEG_INF))
    return jnp.stack(tabs)


def _na(qb, kb, vb, bm, s_lat):
    b, sall, _ = qb.shape
    nq = NA_QROWS * GRID_W
    nsteps = s_lat // nq
    kv = pl.BlockSpec((1, sall, 256), lambda i, j: (i, 0, 0))

    def bm_map(i, j):
        return (jnp.where(j == 0, 0, jnp.where(j == nsteps - 1, 2, 1)), 0, 0, 0)

    return pl.pallas_call(
        functools.partial(_na_kernel, s_lat=s_lat, c_len=sall - s_lat),
        grid=(b, nsteps),
        in_specs=[pl.BlockSpec((1, nq, 256), lambda i, j: (i, j, 0)), kv, kv,
                  pl.BlockSpec((1, N_HEADS, nq, NA_KROWS * GRID_W), bm_map)],
        out_specs=pl.BlockSpec((1, nq, 256), lambda i, j: (i, j, 0)),
        out_shape=jax.ShapeDtypeStruct((b, s_lat, 256), BF16),
        compiler_params=_params(("arbitrary", "arbitrary")),
        name="na2d",
    )(qb, kb, vb, bm)


def _diff_kernel(lamv_ref, subg_ref, q_ref, k_ref, vt_ref, o_ref, e_scr, m_scr, *, n_keys, lam_init):
    tq = ATT_TQ
    chunks = [(c0, min(DIFF_KCHUNK, n_keys - c0)) for c0 in range(0, n_keys, DIFF_KCHUNK)]
    lam = _lam(lamv_ref, lam_init)
    stats = []
    for hh in range(DIFF_HEADS_PER_STEP):
        sl = slice(hh * LANES, (hh + 1) * LANES)
        q = q_ref[0, :, sl]
        q12 = jnp.concatenate([_half_mask(q, 0), _half_mask(q, 1)], axis=0)
        m = jnp.full((1, 2 * tq), NEG_INF, F32)
        den = jnp.zeros((1, 2 * tq), F32)
        for i, (c0, n) in enumerate(chunks):
            s = _dot_nt(k_ref[0, c0:c0 + n, sl], q12)
            m_new = jnp.maximum(m, jnp.max(s, axis=0, keepdims=True))
            e = jnp.exp2(s - m_new)
            e_scr[hh, c0:c0 + n, :] = e
            m_scr[hh, i:i + 1, :] = m_new
            den = den * jnp.exp2(m - m_new) + jnp.sum(e, axis=0, keepdims=True)
            m = m_new
        stats.append((m, den))
    for hh in range(DIFF_HEADS_PER_STEP):
        sl = slice(hh * LANES, (hh + 1) * LANES)
        m, den = stats[hh]
        acc = jnp.zeros((LANES, tq), F32)
        for i, (c0, n) in enumerate(chunks):
            f = jnp.exp2(m_scr[hh, i:i + 1, :] - m) / den
            p = e_scr[hh, c0:c0 + n, :tq] * f[:, :tq] - e_scr[hh, c0:c0 + n, tq:] * (lam * f[:, tq:])
            acc = acc + _dot(vt_ref[0, sl, c0:c0 + n], p.astype(BF16))
        o_ref[0, :, sl] = _diff_post(acc.T, subg_ref[...], lam_init).astype(BF16)


def _diff(lamv, subg, qd, kd, vdt, s_lat, lam_init):
    b, sall, _ = qd.shape
    nchunk = -(-sall // DIFF_KCHUNK)
    hps = DIFF_HEADS_PER_STEP
    wid = hps * LANES
    return pl.pallas_call(
        functools.partial(_diff_kernel, n_keys=sall, lam_init=lam_init),
        grid=(b, N_HEADS // hps, s_lat // ATT_TQ),
        in_specs=[pl.BlockSpec((8, LANES), lambda i, h, j: (0, 0)),
                  pl.BlockSpec((1, LANES), lambda i, h, j: (0, 0)),
                  pl.BlockSpec((1, ATT_TQ, wid), lambda i, h, j: (i, j, h)),
                  pl.BlockSpec((1, sall, wid), lambda i, h, j: (i, 0, h)),
                  pl.BlockSpec((1, wid, sall), lambda i, h, j: (i, h, 0))],
        out_specs=pl.BlockSpec((1, ATT_TQ, wid), lambda i, h, j: (i, j, h)),
        out_shape=jax.ShapeDtypeStruct((b, s_lat, 512), BF16),
        scratch_shapes=[pltpu.VMEM((hps, sall, 2 * ATT_TQ), F32),
                        pltpu.VMEM((hps, -(-nchunk // 8) * 8, 2 * ATT_TQ), F32)],
        compiler_params=_params(("arbitrary", "arbitrary", "arbitrary")),
        name="diff_attn",
    )(lamv, subg, qd, kd, vdt)


def _ctx_kernel(sink_ref, lamv_ref, subg_ref, qa_ref, ka_ref, va_ref, qb_ref, kb_ref, vb_ref,
                qd_ref, kd_ref, vd_ref, oa_ref, ob_ref, od_ref, *, lam_init):
    cl = qa_ref.shape[1]
    ka = ka_ref[0]
    va = va_ref[0]
    for g in range(2):
        q2 = _gqa_rows(qa_ref, g)
        s = _dot_nt(q2, ka)
        snk = _gqa_sink(sink_ref, g, cl)
        m = jnp.maximum(jnp.max(s, axis=-1, keepdims=True), snk)
        e = jnp.exp(s - m)
        den = jnp.sum(e, axis=-1, keepdims=True) + jnp.exp(snk - m)
        _gqa_store(oa_ref, g, _dot(e.astype(BF16), va) / den, cl)
    for c in range(2):
        sl = slice(c * LANES, (c + 1) * LANES)
        outs = []
        for half in range(2):
            s = _dot_nt(_half_mask(qb_ref[0, :, sl], half), kb_ref[0, :, sl])
            e = jnp.exp(s - jnp.max(s, axis=-1, keepdims=True))
            outs.append(_dot(e.astype(BF16), vb_ref[0, :, sl]) / jnp.sum(e, axis=-1, keepdims=True))
        ob_ref[0, :, sl] = _merge_halves(outs[0], outs[1]).astype(BF16)
    lam = _lam(lamv_ref, lam_init)
    for h in range(N_HEADS):
        sl = slice(h * LANES, (h + 1) * LANES)
        q = qd_ref[0, :, sl]
        q12 = jnp.concatenate([_half_mask(q, 0), _half_mask(q, 1)], axis=0)
        s = _dot_nt(q12, kd_ref[0, :, sl])
        e = jnp.exp2(s - jnp.max(s, axis=-1, keepdims=True))
        den = jnp.sum(e, axis=-1, keepdims=True)
        p = e[:cl] * (1.0 / den[:cl]) - e[cl:] * (lam / den[cl:])
        o = _dot(p.astype(BF16), vd_ref[0, :, sl])
        od_ref[0, :, sl] = _diff_post(o, subg_ref[...], lam_init).astype(BF16)


def _ctx_attention(sink, lamv, subg, qa, ka, va, qb, kb, vb, qd, kd, vd, s_lat, lam_init):
    b, sall, _ = qa.shape
    cl = sall - s_lat
    blk = s_lat // cl
    row = lambda width: pl.BlockSpec((1, cl, width), lambda i: (i, blk, 0))
    out = lambda width: pl.BlockSpec((1, cl, width), lambda i: (i, 0, 0))
    return pl.pallas_call(
        functools.partial(_ctx_kernel, lam_init=lam_init),
        grid=(b,),
        in_specs=[pl.BlockSpec(memory_space=pltpu.SMEM),
                  pl.BlockSpec((8, LANES), lambda i: (0, 0)),
                  pl.BlockSpec((1, LANES), lambda i: (0, 0)),
                  row(512), row(128), row(128), row(256), row(256), row(256), row(512), row(512), row(512)],
        out_specs=[out(256), out(256), out(512)],
        out_shape=[jax.ShapeDtypeStruct((b, cl, w), BF16) for w in (256, 256, 512)],
        compiler_params=_params(("arbitrary",)),
        name="ctx_attention",
    )(sink, lamv, subg, qa, ka, va, qb, kb, vb, qd, kd, vd)


def _out_kernel(x_ref, w_ref, mod_ref, *refs, nlat):
    xo_ref = refs[-1]

    def emit(oa_ref, ob_ref, od_ref):
        attn = (_dot(oa_ref[0], w_ref[0:256, :]) + _dot(ob_ref[0], w_ref[256:512, :])
                + _dot(od_ref[0], w_ref[512:1024, :]))
        xo_ref[0] = x_ref[0] + mod_ref[0, 2:3, :] * attn

    if len(refs) == 4:
        emit(*refs[:3])
    else:
        is_ctx = pl.program_id(1) >= nlat
        pl.when(jnp.logical_not(is_ctx))(lambda: emit(*refs[:3]))
        pl.when(is_ctx)(lambda: emit(*refs[3:6]))


def _out_projection(xc, lat, ctx, w_out, modall, s_lat):
    b, sall, d = xc.shape
    tm = TOK_TILE
    nlat = s_lat // tm
    n_rows = s_lat if ctx is None else sall
    row = lambda width: pl.BlockSpec((1, tm, width), lambda i, j: (i, j, 0))
    lat_row = lambda width: pl.BlockSpec((1, tm, width), lambda i, j: (i, jnp.minimum(j, nlat - 1), 0))
    ctx_row = lambda width: pl.BlockSpec((1, tm, width), lambda i, j: (i, 0, 0))
    widths = (256, 256, 512)
    specs = [lat_row(w) for w in widths] + ([] if ctx is None else [ctx_row(w) for w in widths])
    return pl.pallas_call(
        functools.partial(_out_kernel, nlat=nlat),
        grid=(b, n_rows // tm),
        in_specs=[row(d),
                  pl.BlockSpec((MIX_WIDTH, d), lambda i, j: (0, 0)),
                  pl.BlockSpec((1, 8, d), lambda i, j: (2 * i + (j >= nlat).astype(I32), 0, 0))] + specs,
        out_specs=row(d),
        out_shape=jax.ShapeDtypeStruct((b, n_rows, d), F32),
        compiler_params=_params(("arbitrary", "arbitrary")),
        name="out_projection",
    )(xc, w_out, modall, *lat, *(() if ctx is None else ctx))


def _first_argmax(v, iota, n):
    m = jnp.max(v, axis=0, keepdims=True)
    ix = jnp.min(jnp.where(v == m, iota, float(n)), axis=0, keepdims=True)
    return m, ix


def _router_kernel(x_ref, g_ref, mod_ref, whi_ref, wlo_ref, bias_ref, tri_ref, ltri_ref,
                   w_ref, slot_ref, meta_ref, cnt_ref):
    @pl.when(pl.program_id(0) == 0)
    def _():
        cnt_ref[...] = jnp.zeros_like(cnt_ref)

    hf = _norm_mod(x_ref[...], g_ref[...], mod_ref[0, 3:4, :], mod_ref[0, 4:5, :])
    tm = hf.shape[0]
    h_hi, h_lo = _split_bf16(hf)
    whi = whi_ref[...]
    logits = _dot_nt(whi, h_hi) + _dot_nt(whi, h_lo) + _dot_nt(wlo_ref[...], h_hi)
    scores = _sigmoid(logits)
    biased = scores + bias_ref[...]
    gsz = N_EXPERTS // N_GROUPS
    iota_g = lax.broadcasted_iota(I32, (gsz, tm), 0).astype(F32)
    gscore = []
    for g in range(N_GROUPS):
        v = biased[g * gsz:(g + 1) * gsz]
        m1, i1 = _first_argmax(v, iota_g, gsz)
        m2 = jnp.max(jnp.where(iota_g == i1, -jnp.inf, v), axis=0, keepdims=True)
        gscore.append(m1 + m2)
    cur = jnp.concatenate(gscore, axis=0)
    iota_n = lax.broadcasted_iota(I32, (N_GROUPS, tm), 0).astype(F32)
    gsel = jnp.zeros((N_GROUPS, tm), F32)
    for _ in range(TOPK_GROUPS):
        _, ix = _first_argmax(cur, iota_n, N_GROUPS)
        hit = iota_n == ix
        gsel = jnp.where(hit, 1.0, gsel)
        cur = jnp.where(hit, -jnp.inf, cur)
    masked = jnp.concatenate(
        [jnp.where(gsel[g:g + 1] > 0.5, biased[g * gsz:(g + 1) * gsz], -jnp.inf) for g in range(N_GROUPS)],
        axis=0)
    iota_e = lax.broadcasted_iota(I32, (N_EXPERTS, tm), 0).astype(F32)
    onehot = jnp.zeros((N_EXPERTS, tm), F32)
    idxs, ws = [], []
    for _ in range(TOP_K):
        _, ix = _first_argmax(masked, iota_e, N_EXPERTS)
        hit = iota_e == ix
        ws.append(jnp.sum(jnp.where(hit, scores, 0.0), axis=0, keepdims=True))
        masked = jnp.where(hit, -jnp.inf, masked)
        onehot = jnp.where(hit, 1.0, onehot)
        idxs.append(ix)
    base = cnt_ref[...]
    rank = _dot(onehot.astype(BF16), tri_ref[...])
    n_col = jnp.sum(onehot, axis=1, keepdims=True)
    n_pad = jnp.floor((n_col + (ROW_CHUNK - 1)) * (1.0 / ROW_CHUNK)) * ROW_CHUNK
    start = _dot(ltri_ref[...], jnp.broadcast_to(n_pad, (N_EXPERTS, LANES)).astype(BF16))
    wsum = ws[0]
    for wk in ws[1:]:
        wsum = wsum + wk
    for k in range(TOP_K):
        slot = jnp.sum(jnp.where(iota_e == idxs[k], rank + start[:, 0:1], 0.0), axis=0, keepdims=True)
        slot_ref[0, k:k + 1, :] = slot.astype(I32)
        w_ref[0, k:k + 1, :] = ws[k] / wsum * ROUTED_SCALE
    lane = lax.broadcasted_iota(I32, (N_EXPERTS, LANES), 1)
    meta_ref[0] = jnp.where(lane == 0, n_pad, jnp.where(lane == 1, base, start))
    cnt_ref[...] = base + n_pad


def _router(xflat, g, modall, whi, wlo, bias, tri, ltri, tiles_per_batch, nlat):
    t, d = xflat.shape
    tm = TOK_TILE
    nt = t // tm
    per_tok = pl.BlockSpec((1, TOP_K, tm), lambda i: (i, 0, 0))

    def mod_map(i):
        return (2 * (i // tiles_per_batch) + ((i % tiles_per_batch) >= nlat).astype(I32), 0, 0)

    return pl.pallas_call(
        _router_kernel,
        grid=(nt,),
        in_specs=[pl.BlockSpec((tm, d), lambda i: (i, 0)),
                  pl.BlockSpec((1, d), lambda i: (0, 0)),
                  pl.BlockSpec((1, 8, d), mod_map),
                  pl.BlockSpec((N_EXPERTS, d), lambda i: (0, 0)),
                  pl.BlockSpec((N_EXPERTS, d), lambda i: (0, 0)),
                  pl.BlockSpec((N_EXPERTS, 1), lambda i: (0, 0)),
                  pl.BlockSpec((tm, tm), lambda i: (0, 0)),
                  pl.BlockSpec((N_EXPERTS, N_EXPERTS), lambda i: (0, 0))],
        out_specs=[per_tok, per_tok,
                   pl.BlockSpec((1, N_EXPERTS, LANES), lambda i: (i, 0, 0)),
                   pl.BlockSpec((N_EXPERTS, 1), lambda i: (0, 0))],
        out_shape=[jax.ShapeDtypeStruct((nt, TOP_K, tm), F32),
                   jax.ShapeDtypeStruct((nt, TOP_K, tm), I32),
                   jax.ShapeDtypeStruct((nt, N_EXPERTS, LANES), F32),
                   jax.ShapeDtypeStruct((N_EXPERTS, 1), F32)],
        compiler_params=_params(("arbitrary",)),
        name="router",
    )(xflat, g, modall, whi, wlo, bias, tri, ltri)


def _pack_bf16_pairs(v, is_bf16_valued=False):
    half = v.shape[1] // 2
    if not is_bf16_valued:
        v = v.astype(BF16).astype(F32)
    lo = lax.shift_right_logical(lax.bitcast_convert_type(v[:, :half], U32), jnp.uint32(16))
    hi = lax.bitcast_convert_type(v[:, half:], U32) & jnp.uint32(0xFFFF0000)
    return lo | hi


def _unpack_bf16_pairs(u):
    lo = lax.bitcast_convert_type(lax.shift_left(u, jnp.uint32(16)), F32).astype(BF16)
    hi = lax.bitcast_convert_type(u & jnp.uint32(0xFFFF0000), F32).astype(BF16)
    return lo, hi


def _chunk_copies(meta_ref, make_copy):
    total = lax.shift_right_logical(_slots_used(meta_ref), ROW_CHUNK.bit_length() - 1)

    def start(j, carry):
        make_copy(j).start()
        return carry

    lax.fori_loop(0, total, start, 0)
    return total


def _tile_slots(tm):
    return TOP_K * tm + ROW_CHUNK * N_EXPERTS


def _slots_used(meta_ref):
    last = N_EXPERTS - 1
    return meta_ref[0, 2, last] + meta_ref[0, 0, last]


def _dispatch_kernel(meta_ref, rows_ref, x_ref, g_ref, mod_ref, slot_ref, xs_ref, xc_scr, sem):
    hb = _norm_mod(x_ref[...], g_ref[...], mod_ref[0, 3:4, :], mod_ref[0, 4:5, :]).astype(BF16)
    tm = hb.shape[0]
    slot = slot_ref[0]

    def select_rows(r2, carry):
        for u in range(CHUNK_UNROLL):
            r0 = pl.multiple_of((r2 * CHUNK_UNROLL + u) * tm, tm)
            row = lax.broadcasted_iota(I32, (tm, tm), 0) + r0
            sel = jnp.zeros((tm, tm), F32)
            for k in range(TOP_K):
                sel = jnp.where(row == slot[k:k + 1, :], 1.0, sel)
            xc_scr[pl.ds(r0, tm), :] = _pack_bf16_pairs(_dot(sel.astype(BF16), hb), is_bf16_valued=True)
        return carry

    step = CHUNK_UNROLL * tm
    lax.fori_loop(0, (_slots_used(meta_ref) + step - 1) // step, select_rows, 0)

    def make_copy(j):
        src = pl.multiple_of(j * ROW_CHUNK, ROW_CHUNK)
        dst = pl.multiple_of(rows_ref[0, 0, j], ROW_CHUNK)
        return pltpu.make_async_copy(xc_scr.at[pl.ds(src, ROW_CHUNK)], xs_ref.at[pl.ds(dst, ROW_CHUNK)], sem)

    total = _chunk_copies(meta_ref, make_copy)

    def wait(j, carry):
        pltpu.make_async_copy(xc_scr.at[pl.ds(0, ROW_CHUNK)], xs_ref.at[pl.ds(0, ROW_CHUNK)], sem).wait()
        return carry

    lax.fori_loop(0, total, wait, 0)


def _dispatch(meta, chunk_rows, slot, xflat, g, modall, n_rows, tiles_per_batch, nlat):
    t, d = xflat.shape
    tm = TOK_TILE
    nt = t // tm

    def mod_map(i):
        return (2 * (i // tiles_per_batch) + ((i % tiles_per_batch) >= nlat).astype(I32), 0, 0)

    return pl.pallas_call(
        _dispatch_kernel,
        grid=(nt,),
        in_specs=[pl.BlockSpec((1, 3, N_EXPERTS), lambda i: (i, 0, 0), memory_space=pltpu.SMEM),
                  pl.BlockSpec((1, 1, chunk_rows.shape[2]), lambda i: (i, 0, 0), memory_space=pltpu.SMEM),
                  pl.BlockSpec((tm, d), lambda i: (i, 0)),
                  pl.BlockSpec((1, d), lambda i: (0, 0)),
                  pl.BlockSpec((1, 8, d), mod_map),
                  pl.BlockSpec((1, TOP_K, tm), lambda i: (i, 0, 0))],
        out_specs=pl.BlockSpec(memory_space=pl.ANY),
        out_shape=jax.ShapeDtypeStruct((n_rows, d // 2), U32),
        scratch_shapes=[pltpu.VMEM((_tile_slots(tm), d // 2), U32), pltpu.SemaphoreType.DMA(())],
        compiler_params=_params(("arbitrary",)),
        name="moe_dispatch",
    )(meta, chunk_rows, xflat, g, modall, slot)


def _expert_kernel(blk_e_ref, nvalid_ref, nact_ref, xs_ref, wg_ref, wu_ref, wd_ref, ys_ref, wg_s, wu_s, wd_s):
    b = pl.program_id(0)
    changed = jnp.logical_or(b == 0, blk_e_ref[b] != blk_e_ref[jnp.maximum(b - 1, 0)])

    @pl.when(changed)
    def _():
        wg_s[...] = wg_ref[0, 0].astype(BF16)
        wu_s[...] = wu_ref[0, 0].astype(BF16)
        wd_s[...] = wd_ref[0, 0].astype(BF16)

    @pl.when(b < nact_ref[0])
    def _():
        half = xs_ref.shape[1]
        row = lax.broadcasted_iota(I32, xs_ref.shape, 0)
        xs = jnp.where(row < nvalid_ref[b], xs_ref[...], jnp.zeros(xs_ref.shape, U32))
        x_lo, x_hi = _unpack_bf16_pairs(xs)
        gate = _dot(x_lo, wg_s[0:half, :]) + _dot(x_hi, wg_s[half:, :])
        up = _dot(x_lo, wu_s[0:half, :]) + _dot(x_hi, wu_s[half:, :])
        mid = (gate * _sigmoid(gate)) * up
        ys_ref[...] = _pack_bf16_pairs(_dot(mid.astype(BF16), wd_s[...]))

    @pl.when(b >= nact_ref[0])
    def _():
        ys_ref[...] = jnp.zeros_like(ys_ref)


def _experts(blk_e, nvalid, nact, xs, w_gate, w_up, w_down, layer):
    n_rows, half = xs.shape
    _, _, d, f = w_gate.shape
    n_blk = n_rows // MOE_BLOCK
    grid_spec = pltpu.PrefetchScalarGridSpec(
        num_scalar_prefetch=3,
        grid=(n_blk,),
        in_specs=[pl.BlockSpec((MOE_BLOCK, half), lambda i, be, nv, na: (jnp.minimum(i, na[0] - 1), 0)),
                  pl.BlockSpec((1, 1, d, f), lambda i, be, nv, na: (layer, be[i], 0, 0)),
                  pl.BlockSpec((1, 1, d, f), lambda i, be, nv, na: (layer, be[i], 0, 0)),
                  pl.BlockSpec((1, 1, f, d), lambda i, be, nv, na: (layer, be[i], 0, 0))],
        out_specs=pl.BlockSpec((MOE_BLOCK, half), lambda i, be, nv, na: (i, 0)),
        scratch_shapes=[pltpu.VMEM((d, f), BF16), pltpu.VMEM((d, f), BF16), pltpu.VMEM((f, d), BF16)],
    )
    return pl.pallas_call(
        _expert_kernel,
        grid_spec=grid_spec,
        out_shape=jax.ShapeDtypeStruct((n_rows, half), U32),
        compiler_params=_params(("arbitrary",)),
        name="moe_experts",
    )(blk_e, nvalid, nact, xs, w_gate, w_up, w_down)


def _combine_kernel(meta_ref, rows_ref, x_ref, g_ref, mod_ref, spos_ref, wt_ref, wsg_ref, wsu_ref, wsd_ref,
                    gfin_ref, ys_ref, o_ref, y_scr, acc_scr, sb_scr, wb_scr, sem, *, final_norm):
    tm, d = x_ref.shape
    half = d // 2

    @pl.when(pl.program_id(0) == 0)
    def _():
        y_scr[...] = jnp.zeros_like(y_scr)

    def make_copy(j):
        src = pl.multiple_of(rows_ref[0, 0, j], ROW_CHUNK)
        dst = pl.multiple_of(j * ROW_CHUNK, ROW_CHUNK)
        return pltpu.make_async_copy(ys_ref.at[pl.ds(src, ROW_CHUNK)], y_scr.at[pl.ds(dst, ROW_CHUNK)], sem)

    total = _chunk_copies(meta_ref, make_copy)

    x = x_ref[...]
    hb = _norm_mod(x, g_ref[...], mod_ref[0, 3:4, :], mod_ref[0, 4:5, :]).astype(BF16)
    gate = _dot(hb, wsg_ref[...])
    mid = (gate * _sigmoid(gate)) * _dot(hb, wsu_ref[...])
    acc_scr[...] = _dot(mid.astype(BF16), wsd_ref[...])

    def wait(j, carry):
        pltpu.make_async_copy(ys_ref.at[pl.ds(0, ROW_CHUNK)], y_scr.at[pl.ds(0, ROW_CHUNK)], sem).wait()
        return carry

    lax.fori_loop(0, total, wait, 0)

    n_used = _slots_used(meta_ref)
    for k in range(TOP_K):
        sb_scr[k] = jnp.broadcast_to(spos_ref[0, :, k:k + 1], (tm, LANES))
        wb_scr[k] = jnp.broadcast_to(wt_ref[0, :, k:k + 1], (tm, LANES))

    def chunk(j2, carry):
        lo_sum = jnp.zeros((tm, half), F32)
        hi_sum = jnp.zeros((tm, half), F32)
        for u in range(CHUNK_UNROLL):
            r0 = pl.multiple_of((j2 * CHUNK_UNROLL + u) * COMBINE_KCHUNK, COMBINE_KCHUNK)
            parts = []
            for c in range(COMBINE_KCHUNK // LANES):
                col = lax.broadcasted_iota(I32, (tm, LANES), 1) + (r0 + c * LANES)
                p = jnp.zeros((tm, LANES), F32)
                for k in range(TOP_K):
                    p = jnp.where(col == sb_scr[k], wb_scr[k], p)
                parts.append(p.astype(BF16))
            pb = jnp.concatenate(parts, axis=1)
            y_lo, y_hi = _unpack_bf16_pairs(y_scr[pl.ds(r0, COMBINE_KCHUNK), :])
            lo_sum = lo_sum + _dot(pb, y_lo)
            hi_sum = hi_sum + _dot(pb, y_hi)
        acc_scr[:, :half] += lo_sum
        acc_scr[:, half:] += hi_sum
        return carry

    step = CHUNK_UNROLL * COMBINE_KCHUNK
    lax.fori_loop(0, (n_used + step - 1) // step, chunk, 0)
    out = x + mod_ref[0, 5:6, :] * acc_scr[...]
    if final_norm:
        out = _rms(out, gfin_ref[...])
    o_ref[...] = out


def _combine(meta, chunk_rows, spos_t, wt, xflat, g, modall, wsg, wsu, wsd, gfin, ys, tiles_per_batch, nlat,
             final_norm):
    t, d = xflat.shape
    f = wsg.shape[1]
    tm = TOK_TILE
    nt = t // tm
    y_rows = _tile_slots(tm)
    assert y_rows % COMBINE_KCHUNK == 0

    def mod_map(i):
        return (2 * (i // tiles_per_batch) + ((i % tiles_per_batch) >= nlat).astype(I32), 0, 0)

    return pl.pallas_call(
        functools.partial(_combine_kernel, final_norm=final_norm),
        grid=(nt,),
        in_specs=[pl.BlockSpec((1, 3, N_EXPERTS), lambda i: (i, 0, 0), memory_space=pltpu.SMEM),
                  pl.BlockSpec((1, 1, chunk_rows.shape[2]), lambda i: (i, 0, 0), memory_space=pltpu.SMEM),
                  pl.BlockSpec((tm, d), lambda i: (i, 0)),
                  pl.BlockSpec((1, d), lambda i: (0, 0)),
                  pl.BlockSpec((1, 8, d), mod_map),
                  pl.BlockSpec((1, tm, TOP_K), lambda i: (i, 0, 0)),
                  pl.BlockSpec((1, tm, TOP_K), lambda i: (i, 0, 0)),
                  pl.BlockSpec((d, f), lambda i: (0, 0)),
                  pl.BlockSpec((d, f), lambda i: (0, 0)),
                  pl.BlockSpec((f, d), lambda i: (0, 0)),
                  pl.BlockSpec((1, d), lambda i: (0, 0)),
                  pl.BlockSpec(memory_space=pl.ANY)],
        out_specs=pl.BlockSpec((tm, d), lambda i: (i, 0)),
        out_shape=jax.ShapeDtypeStruct((t, d), F32),
        scratch_shapes=[pltpu.VMEM((y_rows, d // 2), U32), pltpu.VMEM((tm, d), F32),
                        pltpu.VMEM((TOP_K, tm, LANES), I32), pltpu.VMEM((TOP_K, tm, LANES), F32),
                        pltpu.SemaphoreType.DMA(())],
        compiler_params=_params(("arbitrary",)),
        name="moe_combine",
    )(meta, chunk_rows, xflat, g, modall, spos_t, wt, wsg, wsu, wsd, gfin, ys)


def _rope_tables(s_lat, c_len):
    n_freq = HEAD_DIM // 4
    inv = 1.0 / (ROPE_BASE ** (jnp.arange(n_freq, dtype=F32) / n_freq))
    t = jnp.arange(s_lat)
    pos = jnp.stack([(t // GRID_W).astype(F32), (t % GRID_W).astype(F32)], axis=1)
    lane = np.arange(LANES)
    ang = pos[:, (lane % HEAD_DIM) // 32] * inv[lane % 16][None, :]
    sign = jnp.asarray(np.where(lane % 32 < 16, -1.0, 1.0), F32)
    cos = jnp.concatenate([jnp.cos(ang), jnp.ones((c_len, LANES), F32)], axis=0)
    sin = jnp.concatenate([jnp.sin(ang) * sign, jnp.zeros((c_len, LANES), F32)], axis=0)
    return cos, sin


def _moe(xflat, lp, modall, gfin, tiles_per_batch, nlat, final_norm):
    t, d = xflat.shape
    nt = t // TOK_TILE
    g = lp["g_ffn"].reshape(1, d)
    w_r = lp["w_router"].T
    whi = w_r.astype(BF16)
    wlo = (w_r - whi.astype(F32)).astype(BF16)
    tri = jnp.asarray(np.triu(np.ones((TOK_TILE, TOK_TILE), np.float32), 1), BF16)
    ltri = jnp.asarray(np.tril(np.ones((N_EXPERTS, N_EXPERTS), np.float32), -1), BF16)
    w, slot, meta, cnt = _router(xflat, g, modall, whi, wlo, lp["router_bias"].reshape(N_EXPERTS, 1),
                                 tri, ltri, tiles_per_batch, nlat)
    meta = jnp.transpose(meta[:, :, :3].astype(I32), (0, 2, 1))
    rows_e = cnt[:, 0].astype(I32)
    seg = (rows_e + MOE_BLOCK - 1) // MOE_BLOCK * MOE_BLOCK
    pend = jnp.cumsum(seg)
    pstart = (pend - seg).astype(I32)
    n_blk = (t * TOP_K + ROW_CHUNK * N_EXPERTS * nt) // MOE_BLOCK + N_EXPERTS
    blk_row0 = jnp.arange(n_blk, dtype=I32) * MOE_BLOCK
    blk_e = jnp.minimum(jnp.sum((pend[None, :] <= blk_row0[:, None]).astype(I32), axis=1), N_EXPERTS - 1)
    of_blk = blk_e[:, None] == jnp.arange(N_EXPERTS, dtype=I32)[None, :]
    seg_end = jnp.sum(jnp.where(of_blk, (pstart + rows_e)[None, :], 0), axis=1)
    nvalid = jnp.clip(seg_end - blk_row0, 0, MOE_BLOCK).astype(I32)
    nact = (pend[-1] // MOE_BLOCK).astype(I32).reshape(1)
    n_pad, base, start = meta[:, 0, :], meta[:, 1, :], meta[:, 2, :]
    chunk_slot = jnp.arange(_tile_slots(TOK_TILE) // ROW_CHUNK, dtype=I32) * ROW_CHUNK
    s = chunk_slot[None, :, None]
    in_run = (start[:, None, :] <= s) & (s < (start + n_pad)[:, None, :])
    row0 = pstart[None, :] + base - start
    chunk_rows = (jnp.sum(jnp.where(in_run, row0[:, None, :], 0), axis=2) + chunk_slot[None, :])[:, None, :]
    xs = _dispatch(meta, chunk_rows, slot, xflat, g, modall, n_blk * MOE_BLOCK, tiles_per_batch, nlat)
    ys = _experts(blk_e, nvalid, nact, xs, lp["w_gate"], lp["w_up"], lp["w_down"], lp["layer"])
    return _combine(meta, chunk_rows, jnp.transpose(slot, (0, 2, 1)), jnp.transpose(w, (0, 2, 1)), xflat, g, modall,
                    lp["ws_gate"].astype(BF16), lp["ws_up"].astype(BF16), lp["ws_down"].astype(BF16), gfin, ys,
                    tiles_per_batch, nlat, final_norm)


def _layer(xc, cs, cos, sin, lp, layer_idx, last, s_lat, gfin):
    b, sall, d = xc.shape
    c_len = sall - s_lat
    lam_init = 0.8 - 0.6 * math.exp(-0.3 * layer_idx)
    mod = _modulation(cs, lp["w_mod"], lp["b_mod"])
    mod_lat = mod[:b].reshape(b, 1, 6, d)
    mod_ctx = jnp.broadcast_to(mod[b].reshape(1, 1, 6, d), (b, 1, 6, d))
    modall = jnp.concatenate([mod_lat, mod_ctx], axis=1)
    modall = jnp.pad(modall, ((0, 0), (0, 0), (0, 2), (0, 0))).reshape(2 * b, 8, d)

    qa, qb, qd, ka, va, kb, vb, kd, vd, vdt = _projection(
        xc, lp["g_mix"].reshape(1, d), modall, cos, sin, lp["w_in"].astype(BF16), s_lat)
    sink = lp["attn_sink"].astype(F32)
    lamv = jnp.zeros((8, LANES), F32).at[0:4, 0:HEAD_DIM].set(
        jnp.stack([lp["lam_q1"], lp["lam_k1"], lp["lam_q2"], lp["lam_k2"]]).astype(F32))
    subg = lp["subln_g"].reshape(1, LANES).astype(F32)
    oa = _swa(sink, qa, ka, va, s_lat)
    ob = _na(qb, kb, vb, _na_bias_mask(lp["na_rpb"], s_lat // GRID_W), s_lat)
    od = _diff(lamv, subg, qd, kd, vdt, s_lat, lam_init)
    ctx_out = None
    if not last:
        ctx_out = _ctx_attention(sink, lamv, subg, qa, ka, va, qb, kb, vb, qd, kd, vd, s_lat, lam_init)
    n_rows = s_lat if last else sall
    x2 = _out_projection(xc, (oa, ob, od), ctx_out, lp["w_out"].astype(BF16), modall, s_lat)
    tiles_per_batch = n_rows // TOK_TILE
    y = _moe(x2.reshape(b * n_rows, d), lp, modall, gfin, tiles_per_batch, s_lat // TOK_TILE, last)
    return y.reshape(b, n_rows, d)


_LAYER_KEYS = ("w_mod", "b_mod", "g_mix", "g_ffn", "w_in", "w_out", "attn_sink", "na_rpb", "lam_q1", "lam_k1",
               "lam_q2", "lam_k2", "subln_g", "w_router", "router_bias", "w_gate", "w_up", "w_down",
               "ws_gate", "ws_up", "ws_down")


def kernel(x, c, ctx, c_ctx, w_mod, b_mod, g_mix, g_ffn, w_in, w_out, attn_sink, na_rpb, lam_q1, lam_k1, lam_q2,
           lam_k2, subln_g, w_router, router_bias, w_gate, w_up, w_down, ws_gate, ws_up, ws_down, g_final):
    stacked = dict(zip(_LAYER_KEYS, (w_mod, b_mod, g_mix, g_ffn, w_in, w_out, attn_sink, na_rpb, lam_q1, lam_k1,
                                     lam_q2, lam_k2, subln_g, w_router, router_bias, w_gate, w_up, w_down,
                                     ws_gate, ws_up, ws_down)))
    b, s_lat, d = x.shape
    c_len = ctx.shape[1]
    depth = w_mod.shape[0]
    assert s_lat % (NA_QROWS * GRID_W) == 0 and s_lat % c_len == 0 and c_len == TOK_TILE
    cs = jnp.zeros((16, d), F32).at[:b].set(c).at[b].set(c_ctx)
    cos, sin = _rope_tables(s_lat, c_len)
    xc = jnp.concatenate([x, ctx], axis=1)
    gfin = g_final.reshape(1, d)
    for i in range(depth):
        big = ("w_gate", "w_up", "w_down")
        lp = {k: (v if k in big else v[i]) for k, v in stacked.items()}
        lp["layer"] = i
        xc = _layer(xc, cs, cos, sin, lp, i, i == depth - 1, s_lat, gfin)
    return xc
```

```python
import functools
import math

import numpy as np
import jax
import jax.numpy as jnp
from jax import lax
from jax.experimental import pallas as pl
from jax.experimental.pallas import tpu as pltpu

F32 = jnp.float32
BF16 = jnp.bfloat16
I32 = jnp.int32
U32 = jnp.uint32

GRID_W = 64
HEAD_DIM = 64
ROPE_BASE = 10000.0
SWA_WINDOW = 128
NA_KH = 8
NA_KW = 16
N_HEADS = 4
N_EXPERTS = 256
TOP_K = 8
N_GROUPS = 8
TOPK_GROUPS = 4
ROUTED_SCALE = 2.5
MOE_BLOCK = 1024
EXPERT_SUB_ROWS = 256
ROW_CHUNK = 8
COMBINE_KCHUNK = 512
CHUNK_UNROLL = 2
EPS = 1e-6
NEG_INF = -1e30
LOG2E = math.log2(math.e)
Q_COLS = 1024
IN_COLS = 2816
MIX_WIDTH = 1024

LANES = 128
TOK_TILE = 256
ATT_TQ = 256
NA_QROWS = 4
NA_KROWS = NA_QROWS + NA_KH - 1
DIFF_KCHUNK = 256
DIFF_HEADS_PER_STEP = 2
VMEM_LIMIT = 48 * 1024 * 1024


def _params(sem):
    return pltpu.CompilerParams(dimension_semantics=sem, vmem_limit_bytes=VMEM_LIMIT)


def _dot(a, b):
    return jnp.dot(a, b, preferred_element_type=F32)


def _dot_nt(a, b):
    return lax.dot_general(a, b, (((1,), (1,)), ((), ())), preferred_element_type=F32)


def _split_bf16(a):
    hi = a.astype(BF16)
    lo = (a - hi.astype(F32)).astype(BF16)
    return hi, lo


def _sigmoid(x):
    return 1.0 / (1.0 + jnp.exp(-x))


def _rms(x, g):
    return x * lax.rsqrt(jnp.mean(x * x, axis=-1, keepdims=True) + EPS) * g


def _norm_mod(x, g, shift, scale):
    return _rms(x, g) * (1.0 + scale) + shift


def _mod_kernel(c_ref, w_ref, b_ref, o_ref):
    c = c_ref[...]
    a_hi, a_lo = _split_bf16(c * _sigmoid(c))
    w_hi, w_lo = _split_bf16(w_ref[...])
    o_ref[...] = _dot(a_hi, w_hi) + _dot(a_hi, w_lo) + _dot(a_lo, w_hi) + b_ref[...]


def _modulation(cs, w_mod, b_mod):
    n, d = cs.shape
    cols = w_mod.shape[1]
    tn = 1536
    return pl.pallas_call(
        _mod_kernel,
        grid=(cols // tn,),
        in_specs=[pl.BlockSpec((n, d), lambda j: (0, 0)),
                  pl.BlockSpec((d, tn), lambda j: (0, j)),
                  pl.BlockSpec((1, tn), lambda j: (0, j))],
        out_specs=pl.BlockSpec((n, tn), lambda j: (0, j)),
        out_shape=jax.ShapeDtypeStruct((n, cols), F32),
        compiler_params=_params(("arbitrary",)),
        name="modulation",
    )(cs, w_mod, b_mod.reshape(1, cols))


def _proj_kernel(x_ref, g_ref, mod_ref, cos_ref, sin_ref, w_ref,
                 qa_ref, qb_ref, qd_ref, ka_ref, va_ref, kb_ref, vb_ref, kd_ref, vd_ref, vdt_ref, h_scr):
    h = _norm_mod(x_ref[0], g_ref[...], mod_ref[0, 0:1, :], mod_ref[0, 1:2, :])
    h_scr[...] = h.astype(BF16)
    cos = cos_ref[...]
    sin = sin_ref[...]
    tm = h.shape[0]
    lane = lax.broadcasted_iota(I32, (tm, LANES), 1)
    first16 = (lane & 16) == 0
    lo64 = lane < HEAD_DIM

    def rope(v):
        partner = jnp.where(first16, pltpu.roll(v, LANES - 16, 1), pltpu.roll(v, 16, 1))
        return v * cos + partner * sin

    def mm(c0):
        return _dot(h_scr[...], w_ref[:, c0:c0 + 2 * LANES])

    a = mm(0) * (HEAD_DIM ** -0.5)
    c0 = rope(a[:, :LANES])
    c1 = rope(a[:, LANES:])
    zero = jnp.zeros_like(c0)
    qa_ref[0, :, 0 * LANES:1 * LANES] = jnp.where(lo64, c0, zero).astype(BF16)
    qa_ref[0, :, 1 * LANES:2 * LANES] = jnp.where(lo64, pltpu.roll(c0, HEAD_DIM, 1), zero).astype(BF16)
    qa_ref[0, :, 2 * LANES:3 * LANES] = jnp.where(lo64, zero, pltpu.roll(c1, HEAD_DIM, 1)).astype(BF16)
    qa_ref[0, :, 3 * LANES:4 * LANES] = jnp.where(lo64, zero, c1).astype(BF16)
    qb_ref[0] = (mm(256) * (HEAD_DIM ** -0.5)).astype(BF16)
    for j in range(2):
        a = mm(512 + 256 * j) * (HEAD_DIM ** -0.5 * LOG2E)
        qd_ref[0, :, 256 * j:256 * j + LANES] = rope(a[:, :LANES]).astype(BF16)
        qd_ref[0, :, 256 * j + LANES:256 * (j + 1)] = rope(a[:, LANES:]).astype(BF16)
    a = mm(1024)
    ka_ref[0] = rope(a[:, :LANES]).astype(BF16)
    va_ref[0] = a[:, LANES:].astype(BF16)
    kb_ref[0] = mm(1280).astype(BF16)
    vb_ref[0] = mm(1536).astype(BF16)
    for j in range(2):
        a = mm(1792 + 256 * j)
        kd_ref[0, :, 256 * j:256 * j + LANES] = rope(a[:, :LANES]).astype(BF16)
        kd_ref[0, :, 256 * j + LANES:256 * (j + 1)] = rope(a[:, LANES:]).astype(BF16)
        a = mm(2304 + 256 * j)
        vd_ref[0, :, 256 * j:256 * (j + 1)] = a.astype(BF16)
        vdt_ref[0, 256 * j:256 * j + LANES, :] = a[:, :LANES].T.astype(BF16)
        vdt_ref[0, 256 * j + LANES:256 * (j + 1), :] = a[:, LANES:].T.astype(BF16)


def _projection(xc, g, modall, cos, sin, w_in, s_lat):
    b, sall, d = xc.shape
    tm = TOK_TILE
    nlat = s_lat // tm
    row = lambda width: pl.BlockSpec((1, tm, width), lambda i, j: (i, j, 0))
    widths = (512, 256, 512, 128, 128, 256, 256, 512, 512)
    return pl.pallas_call(
        _proj_kernel,
        grid=(b, sall // tm),
        in_specs=[row(d),
                  pl.BlockSpec((1, d), lambda i, j: (0, 0)),
                  pl.BlockSpec((1, 8, d), lambda i, j: (2 * i + (j >= nlat).astype(I32), 0, 0)),
                  pl.BlockSpec((tm, LANES), lambda i, j: (j, 0)),
                  pl.BlockSpec((tm, LANES), lambda i, j: (j, 0)),
                  pl.BlockSpec((d, IN_COLS), lambda i, j: (0, 0))],
        out_specs=[row(w) for w in widths] + [pl.BlockSpec((1, 512, tm), lambda i, j: (i, 0, j))],
        out_shape=([jax.ShapeDtypeStruct((b, sall, w), BF16) for w in widths]
                   + [jax.ShapeDtypeStruct((b, 512, sall), BF16)]),
        scratch_shapes=[pltpu.VMEM((tm, d), BF16)],
        compiler_params=_params(("arbitrary", "arbitrary")),
        name="projection",
    )(xc, g, modall, cos, sin, w_in)


def _half_mask(q, half):
    lane = lax.broadcasted_iota(I32, q.shape, 1)
    keep = (lane < HEAD_DIM) if half == 0 else (lane >= HEAD_DIM)
    return jnp.where(keep, q, jnp.zeros_like(q))


def _merge_halves(lo_part, hi_part):
    lane = lax.broadcasted_iota(I32, lo_part.shape, 1)
    return jnp.where(lane < HEAD_DIM, lo_part, hi_part)


def _gqa_rows(q_ref, g):
    return jnp.concatenate([q_ref[0, :, (2 * g) * LANES:(2 * g + 1) * LANES],
                            q_ref[0, :, (2 * g + 1) * LANES:(2 * g + 2) * LANES]], axis=0)


def _gqa_sink(sink_ref, g, tq):
    rowi = lax.broadcasted_iota(I32, (2 * tq, 1), 0)
    return jnp.where(rowi < tq, sink_ref[2 * g], sink_ref[2 * g + 1])


def _gqa_store(o_ref, g, o, tq):
    top, bot = o[:tq], o[tq:]
    if g == 0:
        chunk = _merge_halves(top, pltpu.roll(bot, HEAD_DIM, 1))
    else:
        chunk = _merge_halves(pltpu.roll(top, HEAD_DIM, 1), bot)
    o_ref[0, :, g * LANES:(g + 1) * LANES] = chunk.astype(BF16)


def _lam(lamv_ref, lam_init):
    v = lamv_ref[...]
    a = jnp.sum(v[0:1] * v[1:2], axis=-1, keepdims=True)
    b = jnp.sum(v[2:3] * v[3:4], axis=-1, keepdims=True)
    return jnp.exp(a) - jnp.exp(b) + lam_init


def _diff_post(o, subg, lam_init):
    return _rms(o, subg) * (1.0 - lam_init)


def _swa_kernel(sink_ref, q_ref, k_ref, v_ref, o_ref, *, s_lat, c_len):
    tq = ATT_TQ
    tk = tq + 2 * SWA_WINDOW
    q0 = pl.program_id(1) * tq
    ks = pl.multiple_of(jnp.clip(q0 - SWA_WINDOW, 0, s_lat - tk), SWA_WINDOW)
    kwin = k_ref[0, pl.ds(ks, tk), :]
    vwin = v_ref[0, pl.ds(ks, tk), :]
    kc = k_ref[0, s_lat:s_lat + c_len, :]
    vc = v_ref[0, s_lat:s_lat + c_len, :]
    qpos = q0 + lax.broadcasted_iota(I32, (tq, tk), 0)
    kpos = ks + lax.broadcasted_iota(I32, (tq, tk), 1)
    valid = jnp.abs(qpos - kpos) <= SWA_WINDOW
    valid = jnp.concatenate([valid, valid], axis=0)
    for g in range(2):
        q2 = _gqa_rows(q_ref, g)
        s_loc = jnp.where(valid, _dot_nt(q2, kwin), NEG_INF)
        s_ctx = _dot_nt(q2, kc)
        snk = _gqa_sink(sink_ref, g, tq)
        m = jnp.maximum(jnp.maximum(jnp.max(s_loc, axis=-1, keepdims=True),
                                    jnp.max(s_ctx, axis=-1, keepdims=True)), snk)
        e_loc = jnp.exp(s_loc - m)
        e_ctx = jnp.exp(s_ctx - m)
        den = (jnp.sum(e_loc, axis=-1, keepdims=True) + jnp.sum(e_ctx, axis=-1, keepdims=True)
               + jnp.exp(snk - m))
        o = (_dot(e_loc.astype(BF16), vwin) + _dot(e_ctx.astype(BF16), vc)) / den
        _gqa_store(o_ref, g, o, tq)


def _swa(sink, qa, ka, va, s_lat):
    b, sall, _ = qa.shape
    kv = pl.BlockSpec((1, sall, LANES), lambda i, j: (i, 0, 0))
    return pl.pallas_call(
        functools.partial(_swa_kernel, s_lat=s_lat, c_len=sall - s_lat),
        grid=(b, s_lat // ATT_TQ),
        in_specs=[pl.BlockSpec(memory_space=pltpu.SMEM),
                  pl.BlockSpec((1, ATT_TQ, 512), lambda i, j: (i, j, 0)), kv, kv],
        out_specs=pl.BlockSpec((1, ATT_TQ, 256), lambda i, j: (i, j, 0)),
        out_shape=jax.ShapeDtypeStruct((b, s_lat, 256), BF16),
        compiler_params=_params(("arbitrary", "arbitrary")),
        name="swa",
    )(sink, qa, ka, va)


def _na_kernel(q_ref, k_ref, v_ref, bm_ref, o_ref, *, s_lat, c_len):
    rows = s_lat // GRID_W
    nk = NA_KROWS * GRID_W
    r0 = pl.program_id(1) * NA_QROWS
    ks = pl.multiple_of(jnp.clip(r0 - NA_KH // 2, 0, rows - NA_KROWS) * GRID_W, GRID_W)
    for c in range(2):
        sl = slice(c * LANES, (c + 1) * LANES)
        kwin = k_ref[0, pl.ds(ks, nk), sl]
        vwin = v_ref[0, pl.ds(ks, nk), sl]
        kc = k_ref[0, s_lat:s_lat + c_len, sl]
        vc = v_ref[0, s_lat:s_lat + c_len, sl]
        q = q_ref[0, :, sl]
        outs = []
        for half in range(2):
            qm = _half_mask(q, half)
            s_loc = _dot_nt(qm, kwin) + bm_ref[0, 2 * c + half]
            s_ctx = _dot_nt(qm, kc)
            m = jnp.maximum(jnp.max(s_loc, axis=-1, keepdims=True), jnp.max(s_ctx, axis=-1, keepdims=True))
            e_loc = jnp.exp(s_loc - m)
            e_ctx = jnp.exp(s_ctx - m)
            den = jnp.sum(e_loc, axis=-1, keepdims=True) + jnp.sum(e_ctx, axis=-1, keepdims=True)
            outs.append((_dot(e_loc.astype(BF16), vwin) + _dot(e_ctx.astype(BF16), vc)) / den)
        o_ref[0, :, sl] = _merge_halves(outs[0], outs[1]).astype(BF16)


def _na_bias_mask(rpb, rows):
    nq = NA_QROWS * GRID_W
    nk = NA_KROWS * GRID_W
    col = np.arange(GRID_W)
    cstart = np.clip(col - NA_KW // 2, 0, GRID_W - NA_KW)
    col_ok = (col[None, :] >= cstart[:, None]) & (col[None, :] < cstart[:, None] + NA_KW)
    dc = np.clip(col[None, :] - col[:, None], -(NA_KW - 1), NA_KW - 1) + NA_KW - 1
    sel_c = np.where(col_ok[:, :, None], np.eye(2 * NA_KW - 1)[dc], 0.0)
    tabs = []
    for r0 in (0, NA_QROWS, rows - NA_QROWS):
        ksr = int(np.clip(r0 - NA_KH // 2, 0, rows - NA_KROWS))
        r = r0 + np.arange(NA_QROWS)
        kr = ksr + np.arange(NA_KROWS)
        rs = np.clip(r - NA_KH // 2, 0, rows - NA_KH)
        row_ok = (kr[None, :] >= rs[:, None]) & (kr[None, :] < rs[:, None] + NA_KH)
        dr = np.clip(kr[None, :] - r[:, None], -(NA_KH - 1), NA_KH - 1) + NA_KH - 1
        sel_r = np.where(row_ok[:, :, None], np.eye(2 * NA_KH - 1)[dr], 0.0)
        ok = (row_ok[:, None, :, None] & col_ok[None, :, None, :]).reshape(nq, nk)
        bias = jnp.einsum("hdc,rjd,qkc->hrqjk", rpb.astype(F32), jnp.asarray(sel_r, F32), jnp.asarray(sel_c, F32),
                          precision=lax.Precision.HIGHEST).reshape(N_HEADS, nq, nk)
        tabs.append(jnp.where(ok[None], bias, NEG_INF))
    return jnp.stack(tabs)


def _na(qb, kb, vb, bm, s_lat):
    b, sall, _ = qb.shape
    nq = NA_QROWS * GRID_W
    nsteps = s_lat // nq
    kv = pl.BlockSpec((1, sall, 256), lambda i, j: (i, 0, 0))

    def bm_map(i, j):
        return (jnp.where(j == 0, 0, jnp.where(j == nsteps - 1, 2, 1)), 0, 0, 0)

    return pl.pallas_call(
        functools.partial(_na_kernel, s_lat=s_lat, c_len=sall - s_lat),
        grid=(b, nsteps),
        in_specs=[pl.BlockSpec((1, nq, 256), lambda i, j: (i, j, 0)), kv, kv,
                  pl.BlockSpec((1, N_HEADS, nq, NA_KROWS * GRID_W), bm_map)],
        out_specs=pl.BlockSpec((1, nq, 256), lambda i, j: (i, j, 0)),
        out_shape=jax.ShapeDtypeStruct((b, s_lat, 256), BF16),
        compiler_params=_params(("arbitrary", "arbitrary")),
        name="na2d",
    )(qb, kb, vb, bm)


def _diff_kernel(lamv_ref, subg_ref, q_ref, k_ref, vt_ref, o_ref, e_scr, m_scr, *, n_keys, lam_init):
    tq = ATT_TQ
    chunks = [(c0, min(DIFF_KCHUNK, n_keys - c0)) for c0 in range(0, n_keys, DIFF_KCHUNK)]
    lam = _lam(lamv_ref, lam_init)
    stats = []
    for hh in range(DIFF_HEADS_PER_STEP):
        sl = slice(hh * LANES, (hh + 1) * LANES)
        q = q_ref[0, :, sl]
        q12 = jnp.concatenate([_half_mask(q, 0), _half_mask(q, 1)], axis=0)
        m = jnp.full((1, 2 * tq), NEG_INF, F32)
        den = jnp.zeros((1, 2 * tq), F32)
        for i, (c0, n) in enumerate(chunks):
            s = _dot_nt(k_ref[0, c0:c0 + n, sl], q12)
            m_new = jnp.maximum(m, jnp.max(s, axis=0, keepdims=True))
            e = jnp.exp2(s - m_new)
            e_scr[hh, c0:c0 + n, :] = e
            m_scr[hh, i:i + 1, :] = m_new
            den = den * jnp.exp2(m - m_new) + jnp.sum(e, axis=0, keepdims=True)
            m = m_new
        stats.append((m, den))
    for hh in range(DIFF_HEADS_PER_STEP):
        sl = slice(hh * LANES, (hh + 1) * LANES)
        m, den = stats[hh]
        acc = jnp.zeros((LANES, tq), F32)
        for i, (c0, n) in enumerate(chunks):
            f = jnp.exp2(m_scr[hh, i:i + 1, :] - m) / den
            p = e_scr[hh, c0:c0 + n, :tq] * f[:, :tq] - e_scr[hh, c0:c0 + n, tq:] * (lam * f[:, tq:])
            acc = acc + _dot(vt_ref[0, sl, c0:c0 + n], p.astype(BF16))
        o_ref[0, :, sl] = _diff_post(acc.T, subg_ref[...], lam_init).astype(BF16)


def _diff(lamv, subg, qd, kd, vdt, s_lat, lam_init):
    b, sall, _ = qd.shape
    nchunk = -(-sall // DIFF_KCHUNK)
    hps = DIFF_HEADS_PER_STEP
    wid = hps * LANES
    return pl.pallas_call(
        functools.partial(_diff_kernel, n_keys=sall, lam_init=lam_init),
        grid=(b, N_HEADS // hps, s_lat // ATT_TQ),
        in_specs=[pl.BlockSpec((8, LANES), lambda i, h, j: (0, 0)),
                  pl.BlockSpec((1, LANES), lambda i, h, j: (0, 0)),
                  pl.BlockSpec((1, ATT_TQ, wid), lambda i, h, j: (i, j, h)),
                  pl.BlockSpec((1, sall, wid), lambda i, h, j: (i, 0, h)),
                  pl.BlockSpec((1, wid, sall), lambda i, h, j: (i, h, 0))],
        out_specs=pl.BlockSpec((1, ATT_TQ, wid), lambda i, h, j: (i, j, h)),
        out_shape=jax.ShapeDtypeStruct((b, s_lat, 512), BF16),
        scratch_shapes=[pltpu.VMEM((hps, sall, 2 * ATT_TQ), F32),
                        pltpu.VMEM((hps, -(-nchunk // 8) * 8, 2 * ATT_TQ), F32)],
        compiler_params=_params(("arbitrary", "arbitrary", "arbitrary")),
        name="diff_attn",
    )(lamv, subg, qd, kd, vdt)


def _ctx_kernel(sink_ref, lamv_ref, subg_ref, qa_ref, ka_ref, va_ref, qb_ref, kb_ref, vb_ref,
                qd_ref, kd_ref, vd_ref, oa_ref, ob_ref, od_ref, *, lam_init):
    cl = qa_ref.shape[1]
    ka = ka_ref[0]
    va = va_ref[0]
    for g in range(2):
        q2 = _gqa_rows(qa_ref, g)
        s = _dot_nt(q2, ka)
        snk = _gqa_sink(sink_ref, g, cl)
        m = jnp.maximum(jnp.max(s, axis=-1, keepdims=True), snk)
        e = jnp.exp(s - m)
        den = jnp.sum(e, axis=-1, keepdims=True) + jnp.exp(snk - m)
        _gqa_store(oa_ref, g, _dot(e.astype(BF16), va) / den, cl)
    for c in range(2):
        sl = slice(c * LANES, (c + 1) * LANES)
        outs = []
        for half in range(2):
            s = _dot_nt(_half_mask(qb_ref[0, :, sl], half), kb_ref[0, :, sl])
            e = jnp.exp(s - jnp.max(s, axis=-1, keepdims=True))
            outs.append(_dot(e.astype(BF16), vb_ref[0, :, sl]) / jnp.sum(e, axis=-1, keepdims=True))
        ob_ref[0, :, sl] = _merge_halves(outs[0], outs[1]).astype(BF16)
    lam = _lam(lamv_ref, lam_init)
    for h in range(N_HEADS):
        sl = slice(h * LANES, (h + 1) * LANES)
        q = qd_ref[0, :, sl]
        q12 = jnp.concatenate([_half_mask(q, 0), _half_mask(q, 1)], axis=0)
        s = _dot_nt(q12, kd_ref[0, :, sl])
        e = jnp.exp2(s - jnp.max(s, axis=-1, keepdims=True))
        den = jnp.sum(e, axis=-1, keepdims=True)
        p = e[:cl] * (1.0 / den[:cl]) - e[cl:] * (lam / den[cl:])
        o = _dot(p.astype(BF16), vd_ref[0, :, sl])
        od_ref[0, :, sl] = _diff_post(o, subg_ref[...], lam_init).astype(BF16)


def _ctx_attention(sink, lamv, subg, qa, ka, va, qb, kb, vb, qd, kd, vd, s_lat, lam_init):
    b, sall, _ = qa.shape
    cl = sall - s_lat
    blk = s_lat // cl
    row = lambda width: pl.BlockSpec((1, cl, width), lambda i: (i, blk, 0))
    out = lambda width: pl.BlockSpec((1, cl, width), lambda i: (i, 0, 0))
    return pl.pallas_call(
        functools.partial(_ctx_kernel, lam_init=lam_init),
        grid=(b,),
        in_specs=[pl.BlockSpec(memory_space=pltpu.SMEM),
                  pl.BlockSpec((8, LANES), lambda i: (0, 0)),
                  pl.BlockSpec((1, LANES), lambda i: (0, 0)),
                  row(512), row(128), row(128), row(256), row(256), row(256), row(512), row(512), row(512)],
        out_specs=[out(256), out(256), out(512)],
        out_shape=[jax.ShapeDtypeStruct((b, cl, w), BF16) for w in (256, 256, 512)],
        compiler_params=_params(("arbitrary",)),
        name="ctx_attention",
    )(sink, lamv, subg, qa, ka, va, qb, kb, vb, qd, kd, vd)


def _out_kernel(x_ref, w_ref, mod_ref, *refs, nlat):
    xo_ref = refs[-1]

    def emit(oa_ref, ob_ref, od_ref):
        attn = (_dot(oa_ref[0], w_ref[0:256, :]) + _dot(ob_ref[0], w_ref[256:512, :])
                + _dot(od_ref[0], w_ref[512:1024, :]))
        xo_ref[0] = x_ref[0] + mod_ref[0, 2:3, :] * attn

    if len(refs) == 4:
        emit(*refs[:3])
    else:
        is_ctx = pl.program_id(1) >= nlat
        pl.when(jnp.logical_not(is_ctx))(lambda: emit(*refs[:3]))
        pl.when(is_ctx)(lambda: emit(*refs[3:6]))


def _out_projection(xc, lat, ctx, w_out, modall, s_lat):
    b, sall, d = xc.shape
    tm = TOK_TILE
    nlat = s_lat // tm
    n_rows = s_lat if ctx is None else sall
    row = lambda width: pl.BlockSpec((1, tm, width), lambda i, j: (i, j, 0))
    lat_row = lambda width: pl.BlockSpec((1, tm, width), lambda i, j: (i, jnp.minimum(j, nlat - 1), 0))
    ctx_row = lambda width: pl.BlockSpec((1, tm, width), lambda i, j: (i, 0, 0))
    widths = (256, 256, 512)
    specs = [lat_row(w) for w in widths] + ([] if ctx is None else [ctx_row(w) for w in widths])
    return pl.pallas_call(
        functools.partial(_out_kernel, nlat=nlat),
        grid=(b, n_rows // tm),
        in_specs=[row(d),
                  pl.BlockSpec((MIX_WIDTH, d), lambda i, j: (0, 0)),
                  pl.BlockSpec((1, 8, d), lambda i, j: (2 * i + (j >= nlat).astype(I32), 0, 0))] + specs,
        out_specs=row(d),
        out_shape=jax.ShapeDtypeStruct((b, n_rows, d), F32),
        compiler_params=_params(("arbitrary", "arbitrary")),
        name="out_projection",
    )(xc, w_out, modall, *lat, *(() if ctx is None else ctx))


def _first_argmax(v, iota, n):
    m = jnp.max(v, axis=0, keepdims=True)
    ix = jnp.min(jnp.where(v == m, iota, float(n)), axis=0, keepdims=True)
    return m, ix


def _router_kernel(x_ref, g_ref, mod_ref, whi_ref, wlo_ref, bias_ref, tri_ref, ltri_ref,
                   w_ref, slot_ref, meta_ref, cnt_ref):
    @pl.when(pl.program_id(0) == 0)
    def _():
        cnt_ref[...] = jnp.zeros_like(cnt_ref)

    hf = _norm_mod(x_ref[...], g_ref[...], mod_ref[0, 3:4, :], mod_ref[0, 4:5, :])
    tm = hf.shape[0]
    h_hi, h_lo = _split_bf16(hf)
    whi = whi_ref[...]
    logits = _dot_nt(whi, h_hi) + _dot_nt(whi, h_lo) + _dot_nt(wlo_ref[...], h_hi)
    scores = _sigmoid(logits)
    biased = scores + bias_ref[...]
    gsz = N_EXPERTS // N_GROUPS
    iota_g = lax.broadcasted_iota(I32, (gsz, tm), 0).astype(F32)
    gscore = []
    for g in range(N_GROUPS):
        v = biased[g * gsz:(g + 1) * gsz]
        m1, i1 = _first_argmax(v, iota_g, gsz)
        m2 = jnp.max(jnp.where(iota_g == i1, -jnp.inf, v), axis=0, keepdims=True)
        gscore.append(m1 + m2)
    cur = jnp.concatenate(gscore, axis=0)
    iota_n = lax.broadcasted_iota(I32, (N_GROUPS, tm), 0).astype(F32)
    gsel = jnp.zeros((N_GROUPS, tm), F32)
    for _ in range(TOPK_GROUPS):
        _, ix = _first_argmax(cur, iota_n, N_GROUPS)
        hit = iota_n == ix
        gsel = jnp.where(hit, 1.0, gsel)
        cur = jnp.where(hit, -jnp.inf, cur)
    masked = jnp.concatenate(
        [jnp.where(gsel[g:g + 1] > 0.5, biased[g * gsz:(g + 1) * gsz], -jnp.inf) for g in range(N_GROUPS)],
        axis=0)
    iota_e = lax.broadcasted_iota(I32, (N_EXPERTS, tm), 0).astype(F32)
    onehot = jnp.zeros((N_EXPERTS, tm), F32)
    idxs, ws = [], []
    for _ in range(TOP_K):
        _, ix = _first_argmax(masked, iota_e, N_EXPERTS)
        hit = iota_e == ix
        ws.append(jnp.sum(jnp.where(hit, scores, 0.0), axis=0, keepdims=True))
        masked = jnp.where(hit, -jnp.inf, masked)
        onehot = jnp.where(hit, 1.0, onehot)
        idxs.append(ix)
    base = cnt_ref[...]
    rank = _dot(onehot.astype(BF16), tri_ref[...])
    n_col = jnp.sum(onehot, axis=1, keepdims=True)
    n_pad = jnp.floor((n_col + (ROW_CHUNK - 1)) * (1.0 / ROW_CHUNK)) * ROW_CHUNK
    start = _dot(ltri_ref[...], jnp.broadcast_to(n_pad, (N_EXPERTS, LANES)).astype(BF16))
    wsum = ws[0]
    for wk in ws[1:]:
        wsum = wsum + wk
    for k in range(TOP_K):
        slot = jnp.sum(jnp.where(iota_e == idxs[k], rank + start[:, 0:1], 0.0), axis=0, keepdims=True)
        slot_ref[0, k:k + 1, :] = slot.astype(I32)
        w_ref[0, k:k + 1, :] = ws[k] / wsum * ROUTED_SCALE
    lane = lax.broadcasted_iota(I32, (N_EXPERTS, LANES), 1)
    meta_ref[0] = jnp.where(lane == 0, n_pad, jnp.where(lane == 1, base, start))
    cnt_ref[...] = base + n_pad


def _router(xflat, g, modall, whi, wlo, bias, tri, ltri, tiles_per_batch, nlat):
    t, d = xflat.shape
    tm = TOK_TILE
    nt = t // tm
    per_tok = pl.BlockSpec((1, TOP_K, tm), lambda i: (i, 0, 0))

    def mod_map(i):
        return (2 * (i // tiles_per_batch) + ((i % tiles_per_batch) >= nlat).astype(I32), 0, 0)

    return pl.pallas_call(
        _router_kernel,
        grid=(nt,),
        in_specs=[pl.BlockSpec((tm, d), lambda i: (i, 0)),
                  pl.BlockSpec((1, d), lambda i: (0, 0)),
                  pl.BlockSpec((1, 8, d), mod_map),
                  pl.BlockSpec((N_EXPERTS, d), lambda i: (0, 0)),
                  pl.BlockSpec((N_EXPERTS, d), lambda i: (0, 0)),
                  pl.BlockSpec((N_EXPERTS, 1), lambda i: (0, 0)),
                  pl.BlockSpec((tm, tm), lambda i: (0, 0)),
                  pl.BlockSpec((N_EXPERTS, N_EXPERTS), lambda i: (0, 0))],
        out_specs=[per_tok, per_tok,
                   pl.BlockSpec((1, N_EXPERTS, LANES), lambda i: (i, 0, 0)),
                   pl.BlockSpec((N_EXPERTS, 1), lambda i: (0, 0))],
        out_shape=[jax.ShapeDtypeStruct((nt, TOP_K, tm), F32),
                   jax.ShapeDtypeStruct((nt, TOP_K, tm), I32),
                   jax.ShapeDtypeStruct((nt, N_EXPERTS, LANES), F32),
                   jax.ShapeDtypeStruct((N_EXPERTS, 1), F32)],
        compiler_params=_params(("arbitrary",)),
        name="router",
    )(xflat, g, modall, whi, wlo, bias, tri, ltri)


def _pack_bf16_pairs(v, is_bf16_valued=False):
    half = v.shape[1] // 2
    if not is_bf16_valued:
        v = v.astype(BF16).astype(F32)
    lo = lax.shift_right_logical(lax.bitcast_convert_type(v[:, :half], U32), jnp.uint32(16))
    hi = lax.bitcast_convert_type(v[:, half:], U32) & jnp.uint32(0xFFFF0000)
    return lo | hi


def _unpack_bf16_pairs(u):
    lo = lax.bitcast_convert_type(lax.shift_left(u, jnp.uint32(16)), F32).astype(BF16)
    hi = lax.bitcast_convert_type(u & jnp.uint32(0xFFFF0000), F32).astype(BF16)
    return lo, hi


def _chunk_copies(meta_ref, make_copy):
    total = lax.shift_right_logical(_slots_used(meta_ref), ROW_CHUNK.bit_length() - 1)

    def start(j, carry):
        make_copy(j).start()
        return carry

    lax.fori_loop(0, total, start, 0)
    return total


def _tile_slots(tm):
    return TOP_K * tm + ROW_CHUNK * N_EXPERTS


def _slots_used(meta_ref):
    last = N_EXPERTS - 1
    return meta_ref[0, 2, last] + meta_ref[0, 0, last]


def _dispatch_kernel(meta_ref, rows_ref, x_ref, g_ref, mod_ref, slot_ref, xs_ref, xc_scr, sem):
    hb = _norm_mod(x_ref[...], g_ref[...], mod_ref[0, 3:4, :], mod_ref[0, 4:5, :]).astype(BF16)
    tm = hb.shape[0]
    slot = slot_ref[0]

    def select_rows(r2, carry):
        for u in range(CHUNK_UNROLL):
            r0 = pl.multiple_of((r2 * CHUNK_UNROLL + u) * tm, tm)
            row = lax.broadcasted_iota(I32, (tm, tm), 0) + r0
            sel = jnp.zeros((tm, tm), F32)
            for k in range(TOP_K):
                sel = jnp.where(row == slot[k:k + 1, :], 1.0, sel)
            xc_scr[pl.ds(r0, tm), :] = _pack_bf16_pairs(_dot(sel.astype(BF16), hb), is_bf16_valued=True)
        return carry

    step = CHUNK_UNROLL * tm
    lax.fori_loop(0, (_slots_used(meta_ref) + step - 1) // step, select_rows, 0)

    def make_copy(j):
        src = pl.multiple_of(j * ROW_CHUNK, ROW_CHUNK)
        dst = pl.multiple_of(rows_ref[0, 0, j], ROW_CHUNK)
        return pltpu.make_async_copy(xc_scr.at[pl.ds(src, ROW_CHUNK)], xs_ref.at[pl.ds(dst, ROW_CHUNK)], sem)

    total = _chunk_copies(meta_ref, make_copy)

    def wait(j, carry):
        pltpu.make_async_copy(xc_scr.at[pl.ds(0, ROW_CHUNK)], xs_ref.at[pl.ds(0, ROW_CHUNK)], sem).wait()
        return carry

    lax.fori_loop(0, total, wait, 0)


def _dispatch(meta, chunk_rows, slot, xflat, g, modall, n_rows, tiles_per_batch, nlat):
    t, d = xflat.shape
    tm = TOK_TILE
    nt = t // tm

    def mod_map(i):
        return (2 * (i // tiles_per_batch) + ((i % tiles_per_batch) >= nlat).astype(I32), 0, 0)

    return pl.pallas_call(
        _dispatch_kernel,
        grid=(nt,),
        in_specs=[pl.BlockSpec((1, 3, N_EXPERTS), lambda i: (i, 0, 0), memory_space=pltpu.SMEM),
                  pl.BlockSpec((1, 1, chunk_rows.shape[2]), lambda i: (i, 0, 0), memory_space=pltpu.SMEM),
                  pl.BlockSpec((tm, d), lambda i: (i, 0)),
                  pl.BlockSpec((1, d), lambda i: (0, 0)),
                  pl.BlockSpec((1, 8, d), mod_map),
                  pl.BlockSpec((1, TOP_K, tm), lambda i: (i, 0, 0))],
        out_specs=pl.BlockSpec(memory_space=pl.ANY),
        out_shape=jax.ShapeDtypeStruct((n_rows, d // 2), U32),
        scratch_shapes=[pltpu.VMEM((_tile_slots(tm), d // 2), U32), pltpu.SemaphoreType.DMA(())],
        compiler_params=_params(("arbitrary",)),
        name="moe_dispatch",
    )(meta, chunk_rows, xflat, g, modall, slot)


def _expert_kernel(blk_e_ref, nvalid_ref, nact_ref, xs_ref, wg_ref, wu_ref, wd_ref, ys_ref, wg_s, wu_s, wd_s):
    b = pl.program_id(0)
    changed = jnp.logical_or(b == 0, blk_e_ref[b] != blk_e_ref[jnp.maximum(b - 1, 0)])

    @pl.when(changed)
    def _():
        wg_s[...] = wg_ref[0, 0].astype(BF16)
        wu_s[...] = wu_ref[0, 0].astype(BF16)
        wd_s[...] = wd_ref[0, 0].astype(BF16)

    half = xs_ref.shape[1]
    n_here = jnp.where(b < nact_ref[0], nvalid_ref[b], 0)
    for r0 in range(0, MOE_BLOCK, EXPERT_SUB_ROWS):
        rows = pl.ds(r0, EXPERT_SUB_ROWS)

        @pl.when(n_here > r0)
        def _():
            row = lax.broadcasted_iota(I32, (EXPERT_SUB_ROWS, half), 0) + r0
            xs = jnp.where(row < n_here, xs_ref[rows, :], jnp.zeros((EXPERT_SUB_ROWS, half), U32))
            x_lo, x_hi = _unpack_bf16_pairs(xs)
            gate = _dot(x_lo, wg_s[0:half, :]) + _dot(x_hi, wg_s[half:, :])
            up = _dot(x_lo, wu_s[0:half, :]) + _dot(x_hi, wu_s[half:, :])
            mid = (gate * _sigmoid(gate)) * up
            ys_ref[rows, :] = _pack_bf16_pairs(_dot(mid.astype(BF16), wd_s[...]))

        @pl.when(n_here <= r0)
        def _():
            ys_ref[rows, :] = jnp.zeros((EXPERT_SUB_ROWS, half), U32)


def _experts(blk_e, nvalid, nact, xs, w_gate, w_up, w_down, layer):
    n_rows, half = xs.shape
    _, _, d, f = w_gate.shape
    n_blk = n_rows // MOE_BLOCK
    grid_spec = pltpu.PrefetchScalarGridSpec(
        num_scalar_prefetch=3,
        grid=(n_blk,),
        in_specs=[pl.BlockSpec((MOE_BLOCK, half), lambda i, be, nv, na: (jnp.minimum(i, na[0] - 1), 0)),
                  pl.BlockSpec((1, 1, d, f), lambda i, be, nv, na: (layer, be[i], 0, 0)),
                  pl.BlockSpec((1, 1, d, f), lambda i, be, nv, na: (layer, be[i], 0, 0)),
                  pl.BlockSpec((1, 1, f, d), lambda i, be, nv, na: (layer, be[i], 0, 0))],
        out_specs=pl.BlockSpec((MOE_BLOCK, half), lambda i, be, nv, na: (i, 0)),
        scratch_shapes=[pltpu.VMEM((d, f), BF16), pltpu.VMEM((d, f), BF16), pltpu.VMEM((f, d), BF16)],
    )
    return pl.pallas_call(
        _expert_kernel,
        grid_spec=grid_spec,
        out_shape=jax.ShapeDtypeStruct((n_rows, half), U32),
        compiler_params=_params(("arbitrary",)),
        name="moe_experts",
    )(blk_e, nvalid, nact, xs, w_gate, w_up, w_down)


def _combine_kernel(meta_ref, rows_ref, x_ref, g_ref, mod_ref, spos_ref, wt_ref, wsg_ref, wsu_ref, wsd_ref,
                    gfin_ref, ys_ref, o_ref, y_scr, acc_scr, sb_scr, wb_scr, sem, *, final_norm):
    tm, d = x_ref.shape
    half = d // 2

    @pl.when(pl.program_id(0) == 0)
    def _():
        y_scr[...] = jnp.zeros_like(y_scr)

    def make_copy(j):
        src = pl.multiple_of(rows_ref[0, 0, j], ROW_CHUNK)
        dst = pl.multiple_of(j * ROW_CHUNK, ROW_CHUNK)
        return pltpu.make_async_copy(ys_ref.at[pl.ds(src, ROW_CHUNK)], y_scr.at[pl.ds(dst, ROW_CHUNK)], sem)

    total = _chunk_copies(meta_ref, make_copy)

    x = x_ref[...]
    hb = _norm_mod(x, g_ref[...], mod_ref[0, 3:4, :], mod_ref[0, 4:5, :]).astype(BF16)
    gate = _dot(hb, wsg_ref[...])
    mid = (gate * _sigmoid(gate)) * _dot(hb, wsu_ref[...])
    acc_scr[...] = _dot(mid.astype(BF16), wsd_ref[...])

    def wait(j, carry):
        pltpu.make_async_copy(ys_ref.at[pl.ds(0, ROW_CHUNK)], y_scr.at[pl.ds(0, ROW_CHUNK)], sem).wait()
        return carry

    lax.fori_loop(0, total, wait, 0)

    n_used = _slots_used(meta_ref)
    for k in range(TOP_K):
        sb_scr[k] = jnp.broadcast_to(spos_ref[0, :, k:k + 1], (tm, LANES))
        wb_scr[k] = jnp.broadcast_to(wt_ref[0, :, k:k + 1], (tm, LANES))

    def chunk(j2, carry):
        lo_sum = jnp.zeros((tm, half), F32)
        hi_sum = jnp.zeros((tm, half), F32)
        for u in range(CHUNK_UNROLL):
            r0 = pl.multiple_of((j2 * CHUNK_UNROLL + u) * COMBINE_KCHUNK, COMBINE_KCHUNK)
            parts = []
            for c in range(COMBINE_KCHUNK // LANES):
                col = lax.broadcasted_iota(I32, (tm, LANES), 1) + (r0 + c * LANES)
                p = jnp.zeros((tm, LANES), F32)
                for k in range(TOP_K):
                    p = jnp.where(col == sb_scr[k], wb_scr[k], p)
                parts.append(p.astype(BF16))
            pb = jnp.concatenate(parts, axis=1)
            y_lo, y_hi = _unpack_bf16_pairs(y_scr[pl.ds(r0, COMBINE_KCHUNK), :])
            lo_sum = lo_sum + _dot(pb, y_lo)
            hi_sum = hi_sum + _dot(pb, y_hi)
        acc_scr[:, :half] += lo_sum
        acc_scr[:, half:] += hi_sum
        return carry

    step = CHUNK_UNROLL * COMBINE_KCHUNK
    lax.fori_loop(0, (n_used + step - 1) // step, chunk, 0)
    out = x + mod_ref[0, 5:6, :] * acc_scr[...]
    if final_norm:
        out = _rms(out, gfin_ref[...])
    o_ref[...] = out


def _combine(meta, chunk_rows, spos_t, wt, xflat, g, modall, wsg, wsu, wsd, gfin, ys, tiles_per_batch, nlat,
             final_norm):
    t, d = xflat.shape
    f = wsg.shape[1]
    tm = TOK_TILE
    nt = t // tm
    y_rows = _tile_slots(tm)
    assert y_rows % COMBINE_KCHUNK == 0

    def mod_map(i):
        return (2 * (i // tiles_per_batch) + ((i % tiles_per_batch) >= nlat).astype(I32), 0, 0)

    return pl.pallas_call(
        functools.partial(_combine_kernel, final_norm=final_norm),
        grid=(nt,),
        in_specs=[pl.BlockSpec((1, 3, N_EXPERTS), lambda i: (i, 0, 0), memory_space=pltpu.SMEM),
                  pl.BlockSpec((1, 1, chunk_rows.shape[2]), lambda i: (i, 0, 0), memory_space=pltpu.SMEM),
                  pl.BlockSpec((tm, d), lambda i: (i, 0)),
                  pl.BlockSpec((1, d), lambda i: (0, 0)),
                  pl.BlockSpec((1, 8, d), mod_map),
                  pl.BlockSpec((1, tm, TOP_K), lambda i: (i, 0, 0)),
                  pl.BlockSpec((1, tm, TOP_K), lambda i: (i, 0, 0)),
                  pl.BlockSpec((d, f), lambda i: (0, 0)),
                  pl.BlockSpec((d, f), lambda i: (0, 0)),
                  pl.BlockSpec((f, d), lambda i: (0, 0)),
                  pl.BlockSpec((1, d), lambda i: (0, 0)),
                  pl.BlockSpec(memory_space=pl.ANY)],
        out_specs=pl.BlockSpec((tm, d), lambda i: (i, 0)),
        out_shape=jax.ShapeDtypeStruct((t, d), F32),
        scratch_shapes=[pltpu.VMEM((y_rows, d // 2), U32), pltpu.VMEM((tm, d), F32),
                        pltpu.VMEM((TOP_K, tm, LANES), I32), pltpu.VMEM((TOP_K, tm, LANES), F32),
                        pltpu.SemaphoreType.DMA(())],
        compiler_params=_params(("arbitrary",)),
        name="moe_combine",
    )(meta, chunk_rows, xflat, g, modall, spos_t, wt, wsg, wsu, wsd, gfin, ys)


def _rope_tables(s_lat, c_len):
    n_freq = HEAD_DIM // 4
    inv = 1.0 / (ROPE_BASE ** (jnp.arange(n_freq, dtype=F32) / n_freq))
    t = jnp.arange(s_lat)
    pos = jnp.stack([(t // GRID_W).astype(F32), (t % GRID_W).astype(F32)], axis=1)
    lane = np.arange(LANES)
    ang = pos[:, (lane % HEAD_DIM) // 32] * inv[lane % 16][None, :]
    sign = jnp.asarray(np.where(lane % 32 < 16, -1.0, 1.0), F32)
    cos = jnp.concatenate([jnp.cos(ang), jnp.ones((c_len, LANES), F32)], axis=0)
    sin = jnp.concatenate([jnp.sin(ang) * sign, jnp.zeros((c_len, LANES), F32)], axis=0)
    return cos, sin


def _moe(xflat, lp, modall, gfin, tiles_per_batch, nlat, final_norm):
    t, d = xflat.shape
    nt = t // TOK_TILE
    g = lp["g_ffn"].reshape(1, d)
    w_r = lp["w_router"].T
    whi = w_r.astype(BF16)
    wlo = (w_r - whi.astype(F32)).astype(BF16)
    tri = jnp.asarray(np.triu(np.ones((TOK_TILE, TOK_TILE), np.float32), 1), BF16)
    ltri = jnp.asarray(np.tril(np.ones((N_EXPERTS, N_EXPERTS), np.float32), -1), BF16)
    w, slot, meta, cnt = _router(xflat, g, modall, whi, wlo, lp["router_bias"].reshape(N_EXPERTS, 1),
                                 tri, ltri, tiles_per_batch, nlat)
    meta = jnp.transpose(meta[:, :, :3].astype(I32), (0, 2, 1))
    rows_e = cnt[:, 0].astype(I32)
    seg = (rows_e + MOE_BLOCK - 1) // MOE_BLOCK * MOE_BLOCK
    pend = jnp.cumsum(seg)
    pstart = (pend - seg).astype(I32)
    n_blk = (t * TOP_K + ROW_CHUNK * N_EXPERTS * nt) // MOE_BLOCK + N_EXPERTS
    blk_row0 = jnp.arange(n_blk, dtype=I32) * MOE_BLOCK
    blk_e = jnp.minimum(jnp.sum((pend[None, :] <= blk_row0[:, None]).astype(I32), axis=1), N_EXPERTS - 1)
    of_blk = blk_e[:, None] == jnp.arange(N_EXPERTS, dtype=I32)[None, :]
    seg_end = jnp.sum(jnp.where(of_blk, (pstart + rows_e)[None, :], 0), axis=1)
    nvalid = jnp.clip(seg_end - blk_row0, 0, MOE_BLOCK).astype(I32)
    nact = (pend[-1] // MOE_BLOCK).astype(I32).reshape(1)
    n_pad, base, start = meta[:, 0, :], meta[:, 1, :], meta[:, 2, :]
    chunk_slot = jnp.arange(_tile_slots(TOK_TILE) // ROW_CHUNK, dtype=I32) * ROW_CHUNK
    s = chunk_slot[None, :, None]
    in_run = (start[:, None, :] <= s) & (s < (start + n_pad)[:, None, :])
    row0 = pstart[None, :] + base - start
    chunk_rows = (jnp.sum(jnp.where(in_run, row0[:, None, :], 0), axis=2) + chunk_slot[None, :])[:, None, :]
    xs = _dispatch(meta, chunk_rows, slot, xflat, g, modall, n_blk * MOE_BLOCK, tiles_per_batch, nlat)
    ys = _experts(blk_e, nvalid, nact, xs, lp["w_gate"], lp["w_up"], lp["w_down"], lp["layer"])
    return _combine(meta, chunk_rows, jnp.transpose(slot, (0, 2, 1)), jnp.transpose(w, (0, 2, 1)), xflat, g, modall,
                    lp["ws_gate"].astype(BF16), lp["ws_up"].astype(BF16), lp["ws_down"].astype(BF16), gfin, ys,
                    tiles_per_batch, nlat, final_norm)


def _layer(xc, cs, cos, sin, lp, layer_idx, last, s_lat, gfin):
    b, sall, d = xc.shape
    c_len = sall - s_lat
    lam_init = 0.8 - 0.6 * math.exp(-0.3 * layer_idx)
    mod = _modulation(cs, lp["w_mod"], lp["b_mod"])
    mod_lat = mod[:b].reshape(b, 1, 6, d)
    mod_ctx = jnp.broadcast_to(mod[b].reshape(1, 1, 6, d), (b, 1, 6, d))
    modall = jnp.concatenate([mod_lat, mod_ctx], axis=1)
    modall = jnp.pad(modall, ((0, 0), (0, 0), (0, 2), (0, 0))).reshape(2 * b, 8, d)

    qa, qb, qd, ka, va, kb, vb, kd, vd, vdt = _projection(
        xc, lp["g_mix"].reshape(1, d), modall, cos, sin, lp["w_in"].astype(BF16), s_lat)
    sink = lp["attn_sink"].astype(F32)
    lamv = jnp.zeros((8, LANES), F32).at[0:4, 0:HEAD_DIM].set(
        jnp.stack([lp["lam_q1"], lp["lam_k1"], lp["lam_q2"], lp["lam_k2"]]).astype(F32))
    subg = lp["subln_g"].reshape(1, LANES).astype(F32)
    oa = _swa(sink, qa, ka, va, s_lat)
    ob = _na(qb, kb, vb, _na_bias_mask(lp["na_rpb"], s_lat // GRID_W), s_lat)
    od = _diff(lamv, subg, qd, kd, vdt, s_lat, lam_init)
    ctx_out = None
    if not last:
        ctx_out = _ctx_attention(sink, lamv, subg, qa, ka, va, qb, kb, vb, qd, kd, vd, s_lat, lam_init)
    n_rows = s_lat if last else sall
    x2 = _out_projection(xc, (oa, ob, od), ctx_out, lp["w_out"].astype(BF16), modall, s_lat)
    tiles_per_batch = n_rows // TOK_TILE
    y = _moe(x2.reshape(b * n_rows, d), lp, modall, gfin, tiles_per_batch, s_lat // TOK_TILE, last)
    return y.reshape(b, n_rows, d)


_LAYER_KEYS = ("w_mod", "b_mod", "g_mix", "g_ffn", "w_in", "w_out", "attn_sink", "na_rpb", "lam_q1", "lam_k1",
               "lam_q2", "lam_k2", "subln_g", "w_router", "router_bias", "w_gate", "w_up", "w_down",
               "ws_gate", "ws_up", "ws_down")


def kernel(x, c, ctx, c_ctx, w_mod, b_mod, g_mix, g_ffn, w_in, w_out, attn_sink, na_rpb, lam_q1, lam_k1, lam_q2,
           lam_k2, subln_g, w_router, router_bias, w_gate, w_up, w_down, ws_gate, ws_up, ws_down, g_final):
    stacked = dict(zip(_LAYER_KEYS, (w_mod, b_mod, g_mix, g_ffn, w_in, w_out, attn_sink, na_rpb, lam_q1, lam_k1,
                                     lam_q2, lam_k2, subln_g, w_router, router_bias, w_gate, w_up, w_down,
                                     ws_gate, ws_up, ws_down)))
    b, s_lat, d = x.shape
    c_len = ctx.shape[1]
    depth = w_mod.shape[0]
    assert s_lat % (NA_QROWS * GRID_W) == 0 and s_lat % c_len == 0 and c_len == TOK_TILE
    cs = jnp.zeros((16, d), F32).at[:b].set(c).at[b].set(c_ctx)
    cos, sin = _rope_tables(s_lat, c_len)
    xc = jnp.concatenate([x, ctx], axis=1)
    gfin = g_final.reshape(1, d)
    for i in range(depth):
        big = ("w_gate", "w_up", "w_down")
        lp = {k: (v if k in big else v[i]) for k, v in stacked.items()}
        lp["layer"] = i
        xc = _layer(xc, cs, cos, sin, lp, i, i == depth - 1, s_lat, gfin)
    return xc
```

```python
import functools
import math

import numpy as np
import jax
import jax.numpy as jnp
from jax import lax
from jax.experimental import pallas as pl
from jax.experimental.pallas import tpu as pltpu

F32 = jnp.float32
BF16 = jnp.bfloat16
I32 = jnp.int32
U32 = jnp.uint32

GRID_W = 64
HEAD_DIM = 64
ROPE_BASE = 10000.0
SWA_WINDOW = 128
NA_KH = 8
NA_KW = 16
N_HEADS = 4
N_EXPERTS = 256
TOP_K = 8
N_GROUPS = 8
TOPK_GROUPS = 4
ROUTED_SCALE = 2.5
MOE_BLOCK = 1024
ROW_CHUNK = 8
WAIT_GROUP = 16
COMBINE_KCHUNK = 512
CHUNK_UNROLL = 2
EPS = 1e-6
NEG_INF = -1e30
LOG2E = math.log2(math.e)
Q_COLS = 1024
IN_COLS = 2816
MIX_WIDTH = 1024

LANES = 128
TOK_TILE = 256
ATT_TQ = 256
NA_QROWS = 4
NA_KROWS = NA_QROWS + NA_KH - 1
DIFF_KCHUNK = 256
DIFF_HEADS_PER_STEP = 2
VMEM_LIMIT = 48 * 1024 * 1024


def _params(sem):
    return pltpu.CompilerParams(dimension_semantics=sem, vmem_limit_bytes=VMEM_LIMIT)


def _dot(a, b):
    return jnp.dot(a, b, preferred_element_type=F32)


def _dot_nt(a, b):
    return lax.dot_general(a, b, (((1,), (1,)), ((), ())), preferred_element_type=F32)


def _split_bf16(a):
    hi = a.astype(BF16)
    lo = (a - hi.astype(F32)).astype(BF16)
    return hi, lo


def _sigmoid(x):
    return 1.0 / (1.0 + jnp.exp(-x))


def _rms(x, g):
    return x * lax.rsqrt(jnp.mean(x * x, axis=-1, keepdims=True) + EPS) * g


def _norm_mod(x, g, shift, scale):
    return _rms(x, g) * (1.0 + scale) + shift


def _mod_kernel(c_ref, w_ref, b_ref, o_ref):
    c = c_ref[...]
    a_hi, a_lo = _split_bf16(c * _sigmoid(c))
    w_hi, w_lo = _split_bf16(w_ref[...])
    o_ref[...] = _dot(a_hi, w_hi) + _dot(a_hi, w_lo) + _dot(a_lo, w_hi) + b_ref[...]


def _modulation(cs, w_mod, b_mod):
    n, d = cs.shape
    cols = w_mod.shape[1]
    tn = 1536
    return pl.pallas_call(
        _mod_kernel,
        grid=(cols // tn,),
        in_specs=[pl.BlockSpec((n, d), lambda j: (0, 0)),
                  pl.BlockSpec((d, tn), lambda j: (0, j)),
                  pl.BlockSpec((1, tn), lambda j: (0, j))],
        out_specs=pl.BlockSpec((n, tn), lambda j: (0, j)),
        out_shape=jax.ShapeDtypeStruct((n, cols), F32),
        compiler_params=_params(("arbitrary",)),
        name="modulation",
    )(cs, w_mod, b_mod.reshape(1, cols))


def _proj_kernel(x_ref, g_ref, mod_ref, cos_ref, sin_ref, w_ref,
                 qa_ref, qb_ref, qd_ref, ka_ref, va_ref, kb_ref, vb_ref, kd_ref, vd_ref, vdt_ref, h_scr):
    h = _norm_mod(x_ref[0], g_ref[...], mod_ref[0, 0:1, :], mod_ref[0, 1:2, :])
    h_scr[...] = h.astype(BF16)
    cos = cos_ref[...]
    sin = sin_ref[...]
    tm = h.shape[0]
    lane = lax.broadcasted_iota(I32, (tm, LANES), 1)
    first16 = (lane & 16) == 0
    lo64 = lane < HEAD_DIM

    def rope(v):
        partner = jnp.where(first16, pltpu.roll(v, LANES - 16, 1), pltpu.roll(v, 16, 1))
        return v * cos + partner * sin

    def mm(c0):
        return _dot(h_scr[...], w_ref[:, c0:c0 + 2 * LANES])

    a = mm(0) * (HEAD_DIM ** -0.5)
    c0 = rope(a[:, :LANES])
    c1 = rope(a[:, LANES:])
    zero = jnp.zeros_like(c0)
    qa_ref[0, :, 0 * LANES:1 * LANES] = jnp.where(lo64, c0, zero).astype(BF16)
    qa_ref[0, :, 1 * LANES:2 * LANES] = jnp.where(lo64, pltpu.roll(c0, HEAD_DIM, 1), zero).astype(BF16)
    qa_ref[0, :, 2 * LANES:3 * LANES] = jnp.where(lo64, zero, pltpu.roll(c1, HEAD_DIM, 1)).astype(BF16)
    qa_ref[0, :, 3 * LANES:4 * LANES] = jnp.where(lo64, zero, c1).astype(BF16)
    qb_ref[0] = (mm(256) * (HEAD_DIM ** -0.5)).astype(BF16)
    for j in range(2):
        a = mm(512 + 256 * j) * (HEAD_DIM ** -0.5 * LOG2E)
        qd_ref[0, :, 256 * j:256 * j + LANES] = rope(a[:, :LANES]).astype(BF16)
        qd_ref[0, :, 256 * j + LANES:256 * (j + 1)] = rope(a[:, LANES:]).astype(BF16)
    a = mm(1024)
    ka_ref[0] = rope(a[:, :LANES]).astype(BF16)
    va_ref[0] = a[:, LANES:].astype(BF16)
    kb_ref[0] = mm(1280).astype(BF16)
    vb_ref[0] = mm(1536).astype(BF16)
    for j in range(2):
        a = mm(1792 + 256 * j)
        kd_ref[0, :, 256 * j:256 * j + LANES] = rope(a[:, :LANES]).astype(BF16)
        kd_ref[0, :, 256 * j + LANES:256 * (j + 1)] = rope(a[:, LANES:]).astype(BF16)
        a = mm(2304 + 256 * j)
        vd_ref[0, :, 256 * j:256 * (j + 1)] = a.astype(BF16)
        vdt_ref[0, 256 * j:256 * j + LANES, :] = a[:, :LANES].T.astype(BF16)
        vdt_ref[0, 256 * j + LANES:256 * (j + 1), :] = a[:, LANES:].T.astype(BF16)


def _projection(xc, g, modall, cos, sin, w_in, s_lat):
    b, sall, d = xc.shape
    tm = TOK_TILE
    nlat = s_lat // tm
    row = lambda width: pl.BlockSpec((1, tm, width), lambda i, j: (i, j, 0))
    widths = (512, 256, 512, 128, 128, 256, 256, 512, 512)
    return pl.pallas_call(
        _proj_kernel,
        grid=(b, sall // tm),
        in_specs=[row(d),
                  pl.BlockSpec((1, d), lambda i, j: (0, 0)),
                  pl.BlockSpec((1, 8, d), lambda i, j: (2 * i + (j >= nlat).astype(I32), 0, 0)),
                  pl.BlockSpec((tm, LANES), lambda i, j: (j, 0)),
                  pl.BlockSpec((tm, LANES), lambda i, j: (j, 0)),
                  pl.BlockSpec((d, IN_COLS), lambda i, j: (0, 0))],
        out_specs=[row(w) for w in widths] + [pl.BlockSpec((1, 512, tm), lambda i, j: (i, 0, j))],
        out_shape=([jax.ShapeDtypeStruct((b, sall, w), BF16) for w in widths]
                   + [jax.ShapeDtypeStruct((b, 512, sall), BF16)]),
        scratch_shapes=[pltpu.VMEM((tm, d), BF16)],
        compiler_params=_params(("arbitrary", "arbitrary")),
        name="projection",
    )(xc, g, modall, cos, sin, w_in)


def _half_mask(q, half):
    lane = lax.broadcasted_iota(I32, q.shape, 1)
    keep = (lane < HEAD_DIM) if half == 0 else (lane >= HEAD_DIM)
    return jnp.where(keep, q, jnp.zeros_like(q))


def _merge_halves(lo_part, hi_part):
    lane = lax.broadcasted_iota(I32, lo_part.shape, 1)
    return jnp.where(lane < HEAD_DIM, lo_part, hi_part)


def _gqa_rows(q_ref, g):
    return jnp.concatenate([q_ref[0, :, (2 * g) * LANES:(2 * g + 1) * LANES],
                            q_ref[0, :, (2 * g + 1) * LANES:(2 * g + 2) * LANES]], axis=0)


def _gqa_sink(sink_ref, g, tq):
    rowi = lax.broadcasted_iota(I32, (2 * tq, 1), 0)
    return jnp.where(rowi < tq, sink_ref[2 * g], sink_ref[2 * g + 1])


def _gqa_store(o_ref, g, o, tq):
    top, bot = o[:tq], o[tq:]
    if g == 0:
        chunk = _merge_halves(top, pltpu.roll(bot, HEAD_DIM, 1))
    else:
        chunk = _merge_halves(pltpu.roll(top, HEAD_DIM, 1), bot)
    o_ref[0, :, g * LANES:(g + 1) * LANES] = chunk.astype(BF16)


def _lam(lamv_ref, lam_init):
    v = lamv_ref[...]
    a = jnp.sum(v[0:1] * v[1:2], axis=-1, keepdims=True)
    b = jnp.sum(v[2:3] * v[3:4], axis=-1, keepdims=True)
    return jnp.exp(a) - jnp.exp(b) + lam_init


def _diff_post(o, subg, lam_init):
    return _rms(o, subg) * (1.0 - lam_init)


def _swa_kernel(sink_ref, q_ref, k_ref, v_ref, o_ref, *, s_lat, c_len):
    tq = ATT_TQ
    tk = tq + 2 * SWA_WINDOW
    q0 = pl.program_id(1) * tq
    ks = pl.multiple_of(jnp.clip(q0 - SWA_WINDOW, 0, s_lat - tk), SWA_WINDOW)
    kwin = k_ref[0, pl.ds(ks, tk), :]
    vwin = v_ref[0, pl.ds(ks, tk), :]
    kc = k_ref[0, s_lat:s_lat + c_len, :]
    vc = v_ref[0, s_lat:s_lat + c_len, :]
    qpos = q0 + lax.broadcasted_iota(I32, (tq, tk), 0)
    kpos = ks + lax.broadcasted_iota(I32, (tq, tk), 1)
    valid = jnp.abs(qpos - kpos) <= SWA_WINDOW
    valid = jnp.concatenate([valid, valid], axis=0)
    for g in range(2):
        q2 = _gqa_rows(q_ref, g)
        s_loc = jnp.where(valid, _dot_nt(q2, kwin), NEG_INF)
        s_ctx = _dot_nt(q2, kc)
        snk = _gqa_sink(sink_ref, g, tq)
        m = jnp.maximum(jnp.maximum(jnp.max(s_loc, axis=-1, keepdims=True),
                                    jnp.max(s_ctx, axis=-1, keepdims=True)), snk)
        e_loc = jnp.exp(s_loc - m)
        e_ctx = jnp.exp(s_ctx - m)
        den = (jnp.sum(e_loc, axis=-1, keepdims=True) + jnp.sum(e_ctx, axis=-1, keepdims=True)
               + jnp.exp(snk - m))
        o = (_dot(e_loc.astype(BF16), vwin) + _dot(e_ctx.astype(BF16), vc)) / den
        _gqa_store(o_ref, g, o, tq)


def _swa(sink, qa, ka, va, s_lat):
    b, sall, _ = qa.shape
    kv = pl.BlockSpec((1, sall, LANES), lambda i, j: (i, 0, 0))
    return pl.pallas_call(
        functools.partial(_swa_kernel, s_lat=s_lat, c_len=sall - s_lat),
        grid=(b, s_lat // ATT_TQ),
        in_specs=[pl.BlockSpec(memory_space=pltpu.SMEM),
                  pl.BlockSpec((1, ATT_TQ, 512), lambda i, j: (i, j, 0)), kv, kv],
        out_specs=pl.BlockSpec((1, ATT_TQ, 256), lambda i, j: (i, j, 0)),
        out_shape=jax.ShapeDtypeStruct((b, s_lat, 256), BF16),
        compiler_params=_params(("arbitrary", "arbitrary")),
        name="swa",
    )(sink, qa, ka, va)


def _na_kernel(q_ref, k_ref, v_ref, bm_ref, o_ref, *, s_lat, c_len):
    rows = s_lat // GRID_W
    nk = NA_KROWS * GRID_W
    r0 = pl.program_id(1) * NA_QROWS
    ks = pl.multiple_of(jnp.clip(r0 - NA_KH // 2, 0, rows - NA_KROWS) * GRID_W, GRID_W)
    for c in range(2):
        sl = slice(c * LANES, (c + 1) * LANES)
        kwin = k_ref[0, pl.ds(ks, nk), sl]
        vwin = v_ref[0, pl.ds(ks, nk), sl]
        kc = k_ref[0, s_lat:s_lat + c_len, sl]
        vc = v_ref[0, s_lat:s_lat + c_len, sl]
        q = q_ref[0, :, sl]
        outs = []
        for half in range(2):
            qm = _half_mask(q, half)
            s_loc = _dot_nt(qm, kwin) + bm_ref[0, 2 * c + half]
            s_ctx = _dot_nt(qm, kc)
            m = jnp.maximum(jnp.max(s_loc, axis=-1, keepdims=True), jnp.max(s_ctx, axis=-1, keepdims=True))
            e_loc = jnp.exp(s_loc - m)
            e_ctx = jnp.exp(s_ctx - m)
            den = jnp.sum(e_loc, axis=-1, keepdims=True) + jnp.sum(e_ctx, axis=-1, keepdims=True)
            outs.append((_dot(e_loc.astype(BF16), vwin) + _dot(e_ctx.astype(BF16), vc)) / den)
        o_ref[0, :, sl] = _merge_halves(outs[0], outs[1]).astype(BF16)


def _na_bias_mask(rpb, rows):
    nq = NA_QROWS * GRID_W
    nk = NA_KROWS * GRID_W
    col = np.arange(GRID_W)
    cstart = np.clip(col - NA_KW // 2, 0, GRID_W - NA_KW)
    col_ok = (col[None, :] >= cstart[:, None]) & (col[None, :] < cstart[:, None] + NA_KW)
    dc = np.clip(col[None, :] - col[:, None], -(NA_KW - 1), NA_KW - 1) + NA_KW - 1
    sel_c = np.where(col_ok[:, :, None], np.eye(2 * NA_KW - 1)[dc], 0.0)
    tabs = []
    for r0 in (0, NA_QROWS, rows - NA_QROWS):
        ksr = int(np.clip(r0 - NA_KH // 2, 0, rows - NA_KROWS))
        r = r0 + np.arange(NA_QROWS)
        kr = ksr + np.arange(NA_KROWS)
        rs = np.clip(r - NA_KH // 2, 0, rows - NA_KH)
        row_ok = (kr[None, :] >= rs[:, None]) & (kr[None, :] < rs[:, None] + NA_KH)
        dr = np.clip(kr[None, :] - r[:, None], -(NA_KH - 1), NA_KH - 1) + NA_KH - 1
        sel_r = np.where(row_ok[:, :, None], np.eye(2 * NA_KH - 1)[dr], 0.0)
        ok = (row_ok[:, None, :, None] & col_ok[None, :, None, :]).reshape(nq, nk)
        bias = jnp.einsum("hdc,rjd,qkc->hrqjk", rpb.astype(F32), jnp.asarray(sel_r, F32), jnp.asarray(sel_c, F32),
                          precision=lax.Precision.HIGHEST).reshape(N_HEADS, nq, nk)
        tabs.append(jnp.where(ok[None], bias, NEG_INF))
    return jnp.stack(tabs)


def _na(qb, kb, vb, bm, s_lat):
    b, sall, _ = qb.shape
    nq = NA_QROWS * GRID_W
    nsteps = s_lat // nq
    kv = pl.BlockSpec((1, sall, 256), lambda i, j: (i, 0, 0))

    def bm_map(i, j):
        return (jnp.where(j == 0, 0, jnp.where(j == nsteps - 1, 2, 1)), 0, 0, 0)

    return pl.pallas_call(
        functools.partial(_na_kernel, s_lat=s_lat, c_len=sall - s_lat),
        grid=(b, nsteps),
        in_specs=[pl.BlockSpec((1, nq, 256), lambda i, j: (i, j, 0)), kv, kv,
                  pl.BlockSpec((1, N_HEADS, nq, NA_KROWS * GRID_W), bm_map)],
        out_specs=pl.BlockSpec((1, nq, 256), lambda i, j: (i, j, 0)),
        out_shape=jax.ShapeDtypeStruct((b, s_lat, 256), BF16),
        compiler_params=_params(("arbitrary", "arbitrary")),
        name="na2d",
    )(qb, kb, vb, bm)


def _diff_kernel(lamv_ref, subg_ref, q_ref, k_ref, vt_ref, o_ref, e_scr, m_scr, *, n_keys, lam_init):
    tq = ATT_TQ
    chunks = [(c0, min(DIFF_KCHUNK, n_keys - c0)) for c0 in range(0, n_keys, DIFF_KCHUNK)]
    lam = _lam(lamv_ref, lam_init)
    stats = []
    for hh in range(DIFF_HEADS_PER_STEP):
        sl = slice(hh * LANES, (hh + 1) * LANES)
        q = q_ref[0, :, sl]
        q12 = jnp.concatenate([_half_mask(q, 0), _half_mask(q, 1)], axis=0)
        m = jnp.full((1, 2 * tq), NEG_INF, F32)
        den = jnp.zeros((1, 2 * tq), F32)
        for i, (c0, n) in enumerate(chunks):
            s = _dot_nt(k_ref[0, c0:c0 + n, sl], q12)
            m_new = jnp.maximum(m, jnp.max(s, axis=0, keepdims=True))
            e = jnp.exp2(s - m_new)
            e_scr[hh, c0:c0 + n, :] = e
            m_scr[hh, i:i + 1, :] = m_new
            den = den * jnp.exp2(m - m_new) + jnp.sum(e, axis=0, keepdims=True)
            m = m_new
        stats.append((m, den))
    for hh in range(DIFF_HEADS_PER_STEP):
        sl = slice(hh * LANES, (hh + 1) * LANES)
        m, den = stats[hh]
        acc = jnp.zeros((LANES, tq), F32)
        for i, (c0, n) in enumerate(chunks):
            f = jnp.exp2(m_scr[hh, i:i + 1, :] - m) / den
            p = e_scr[hh, c0:c0 + n, :tq] * f[:, :tq] - e_scr[hh, c0:c0 + n, tq:] * (lam * f[:, tq:])
            acc = acc + _dot(vt_ref[0, sl, c0:c0 + n], p.astype(BF16))
        o_ref[0, :, sl] = _diff_post(acc.T, subg_ref[...], lam_init).astype(BF16)


def _diff(lamv, subg, qd, kd, vdt, s_lat, lam_init):
    b, sall, _ = qd.shape
    nchunk = -(-sall // DIFF_KCHUNK)
    hps = DIFF_HEADS_PER_STEP
    wid = hps * LANES
    return pl.pallas_call(
        functools.partial(_diff_kernel, n_keys=sall, lam_init=lam_init),
        grid=(b, N_HEADS // hps, s_lat // ATT_TQ),
        in_specs=[pl.BlockSpec((8, LANES), lambda i, h, j: (0, 0)),
                  pl.BlockSpec((1, LANES), lambda i, h, j: (0, 0)),
                  pl.BlockSpec((1, ATT_TQ, wid), lambda i, h, j: (i, j, h)),
                  pl.BlockSpec((1, sall, wid), lambda i, h, j: (i, 0, h)),
                  pl.BlockSpec((1, wid, sall), lambda i, h, j: (i, h, 0))],
        out_specs=pl.BlockSpec((1, ATT_TQ, wid), lambda i, h, j: (i, j, h)),
        out_shape=jax.ShapeDtypeStruct((b, s_lat, 512), BF16),
        scratch_shapes=[pltpu.VMEM((hps, sall, 2 * ATT_TQ), F32),
                        pltpu.VMEM((hps, -(-nchunk // 8) * 8, 2 * ATT_TQ), F32)],
        compiler_params=_params(("arbitrary", "arbitrary", "arbitrary")),
        name="diff_attn",
    )(lamv, subg, qd, kd, vdt)


def _ctx_kernel(sink_ref, lamv_ref, subg_ref, qa_ref, ka_ref, va_ref, qb_ref, kb_ref, vb_ref,
                qd_ref, kd_ref, vd_ref, oa_ref, ob_ref, od_ref, *, lam_init):
    cl = qa_ref.shape[1]
    ka = ka_ref[0]
    va = va_ref[0]
    for g in range(2):
        q2 = _gqa_rows(qa_ref, g)
        s = _dot_nt(q2, ka)
        snk = _gqa_sink(sink_ref, g, cl)
        m = jnp.maximum(jnp.max(s, axis=-1, keepdims=True), snk)
        e = jnp.exp(s - m)
        den = jnp.sum(e, axis=-1, keepdims=True) + jnp.exp(snk - m)
        _gqa_store(oa_ref, g, _dot(e.astype(BF16), va) / den, cl)
    for c in range(2):
        sl = slice(c * LANES, (c + 1) * LANES)
        outs = []
        for half in range(2):
            s = _dot_nt(_half_mask(qb_ref[0, :, sl], half), kb_ref[0, :, sl])
            e = jnp.exp(s - jnp.max(s, axis=-1, keepdims=True))
            outs.append(_dot(e.astype(BF16), vb_ref[0, :, sl]) / jnp.sum(e, axis=-1, keepdims=True))
        ob_ref[0, :, sl] = _merge_halves(outs[0], outs[1]).astype(BF16)
    lam = _lam(lamv_ref, lam_init)
    for h in range(N_HEADS):
        sl = slice(h * LANES, (h + 1) * LANES)
        q = qd_ref[0, :, sl]
        q12 = jnp.concatenate([_half_mask(q, 0), _half_mask(q, 1)], axis=0)
        s = _dot_nt(q12, kd_ref[0, :, sl])
        e = jnp.exp2(s - jnp.max(s, axis=-1, keepdims=True))
        den = jnp.sum(e, axis=-1, keepdims=True)
        p = e[:cl] * (1.0 / den[:cl]) - e[cl:] * (lam / den[cl:])
        o = _dot(p.astype(BF16), vd_ref[0, :, sl])
        od_ref[0, :, sl] = _diff_post(o, subg_ref[...], lam_init).astype(BF16)


def _ctx_attention(sink, lamv, subg, qa, ka, va, qb, kb, vb, qd, kd, vd, s_lat, lam_init):
    b, sall, _ = qa.shape
    cl = sall - s_lat
    blk = s_lat // cl
    row = lambda width: pl.BlockSpec((1, cl, width), lambda i: (i, blk, 0))
    out = lambda width: pl.BlockSpec((1, cl, width), lambda i: (i, 0, 0))
    return pl.pallas_call(
        functools.partial(_ctx_kernel, lam_init=lam_init),
        grid=(b,),
        in_specs=[pl.BlockSpec(memory_space=pltpu.SMEM),
                  pl.BlockSpec((8, LANES), lambda i: (0, 0)),
                  pl.BlockSpec((1, LANES), lambda i: (0, 0)),
                  row(512), row(128), row(128), row(256), row(256), row(256), row(512), row(512), row(512)],
        out_specs=[out(256), out(256), out(512)],
        out_shape=[jax.ShapeDtypeStruct((b, cl, w), BF16) for w in (256, 256, 512)],
        compiler_params=_params(("arbitrary",)),
        name="ctx_attention",
    )(sink, lamv, subg, qa, ka, va, qb, kb, vb, qd, kd, vd)


def _out_kernel(x_ref, w_ref, mod_ref, *refs, nlat):
    xo_ref = refs[-1]

    def emit(oa_ref, ob_ref, od_ref):
        attn = (_dot(oa_ref[0], w_ref[0:256, :]) + _dot(ob_ref[0], w_ref[256:512, :])
                + _dot(od_ref[0], w_ref[512:1024, :]))
        xo_ref[0] = x_ref[0] + mod_ref[0, 2:3, :] * attn

    if len(refs) == 4:
        emit(*refs[:3])
    else:
        is_ctx = pl.program_id(1) >= nlat
        pl.when(jnp.logical_not(is_ctx))(lambda: emit(*refs[:3]))
        pl.when(is_ctx)(lambda: emit(*refs[3:6]))


def _out_projection(xc, lat, ctx, w_out, modall, s_lat):
    b, sall, d = xc.shape
    tm = TOK_TILE
    nlat = s_lat // tm
    n_rows = s_lat if ctx is None else sall
    row = lambda width: pl.BlockSpec((1, tm, width), lambda i, j: (i, j, 0))
    lat_row = lambda width: pl.BlockSpec((1, tm, width), lambda i, j: (i, jnp.minimum(j, nlat - 1), 0))
    ctx_row = lambda width: pl.BlockSpec((1, tm, width), lambda i, j: (i, 0, 0))
    widths = (256, 256, 512)
    specs = [lat_row(w) for w in widths] + ([] if ctx is None else [ctx_row(w) for w in widths])
    return pl.pallas_call(
        functools.partial(_out_kernel, nlat=nlat),
        grid=(b, n_rows // tm),
        in_specs=[row(d),
                  pl.BlockSpec((MIX_WIDTH, d), lambda i, j: (0, 0)),
                  pl.BlockSpec((1, 8, d), lambda i, j: (2 * i + (j >= nlat).astype(I32), 0, 0))] + specs,
        out_specs=row(d),
        out_shape=jax.ShapeDtypeStruct((b, n_rows, d), F32),
        compiler_params=_params(("arbitrary", "arbitrary")),
        name="out_projection",
    )(xc, w_out, modall, *lat, *(() if ctx is None else ctx))


def _first_argmax(v, iota, n):
    m = jnp.max(v, axis=0, keepdims=True)
    ix = jnp.min(jnp.where(v == m, iota, float(n)), axis=0, keepdims=True)
    return m, ix


def _router_kernel(x_ref, g_ref, mod_ref, whi_ref, wlo_ref, bias_ref, tri_ref, ltri_ref,
                   w_ref, slot_ref, meta_ref, cnt_ref):
    @pl.when(pl.program_id(0) == 0)
    def _():
        cnt_ref[...] = jnp.zeros_like(cnt_ref)

    hf = _norm_mod(x_ref[...], g_ref[...], mod_ref[0, 3:4, :], mod_ref[0, 4:5, :])
    tm = hf.shape[0]
    h_hi, h_lo = _split_bf16(hf)
    whi = whi_ref[...]
    logits = _dot_nt(whi, h_hi) + _dot_nt(whi, h_lo) + _dot_nt(wlo_ref[...], h_hi)
    scores = _sigmoid(logits)
    biased = scores + bias_ref[...]
    gsz = N_EXPERTS // N_GROUPS
    iota_g = lax.broadcasted_iota(I32, (gsz, tm), 0).astype(F32)
    gscore = []
    for g in range(N_GROUPS):
        v = biased[g * gsz:(g + 1) * gsz]
        m1, i1 = _first_argmax(v, iota_g, gsz)
        m2 = jnp.max(jnp.where(iota_g == i1, -jnp.inf, v), axis=0, keepdims=True)
        gscore.append(m1 + m2)
    cur = jnp.concatenate(gscore, axis=0)
    iota_n = lax.broadcasted_iota(I32, (N_GROUPS, tm), 0).astype(F32)
    gsel = jnp.zeros((N_GROUPS, tm), F32)
    for _ in range(TOPK_GROUPS):
        _, ix = _first_argmax(cur, iota_n, N_GROUPS)
        hit = iota_n == ix
        gsel = jnp.where(hit, 1.0, gsel)
        cur = jnp.where(hit, -jnp.inf, cur)
    masked = jnp.concatenate(
        [jnp.where(gsel[g:g + 1] > 0.5, biased[g * gsz:(g + 1) * gsz], -jnp.inf) for g in range(N_GROUPS)],
        axis=0)
    iota_e = lax.broadcasted_iota(I32, (N_EXPERTS, tm), 0).astype(F32)
    onehot = jnp.zeros((N_EXPERTS, tm), F32)
    idxs, ws = [], []
    for _ in range(TOP_K):
        _, ix = _first_argmax(masked, iota_e, N_EXPERTS)
        hit = iota_e == ix
        ws.append(jnp.sum(jnp.where(hit, scores, 0.0), axis=0, keepdims=True))
        masked = jnp.where(hit, -jnp.inf, masked)
        onehot = jnp.where(hit, 1.0, onehot)
        idxs.append(ix)
    base = cnt_ref[...]
    rank = _dot(onehot.astype(BF16), tri_ref[...])
    n_col = jnp.sum(onehot, axis=1, keepdims=True)
    n_pad = jnp.floor((n_col + (ROW_CHUNK - 1)) * (1.0 / ROW_CHUNK)) * ROW_CHUNK
    start = _dot(ltri_ref[...], jnp.broadcast_to(n_pad, (N_EXPERTS, LANES)).astype(BF16))
    wsum = ws[0]
    for wk in ws[1:]:
        wsum = wsum + wk
    for k in range(TOP_K):
        slot = jnp.sum(jnp.where(iota_e == idxs[k], rank + start[:, 0:1], 0.0), axis=0, keepdims=True)
        slot_ref[0, k:k + 1, :] = slot.astype(I32)
        w_ref[0, k:k + 1, :] = ws[k] / wsum * ROUTED_SCALE
    lane = lax.broadcasted_iota(I32, (N_EXPERTS, LANES), 1)
    meta_ref[0] = jnp.where(lane == 0, n_pad, jnp.where(lane == 1, base, start))
    cnt_ref[...] = base + n_pad


def _router(xflat, g, modall, whi, wlo, bias, tri, ltri, tiles_per_batch, nlat):
    t, d = xflat.shape
    tm = TOK_TILE
    nt = t // tm
    per_tok = pl.BlockSpec((1, TOP_K, tm), lambda i: (i, 0, 0))

    def mod_map(i):
        return (2 * (i // tiles_per_batch) + ((i % tiles_per_batch) >= nlat).astype(I32), 0, 0)

    return pl.pallas_call(
        _router_kernel,
        grid=(nt,),
        in_specs=[pl.BlockSpec((tm, d), lambda i: (i, 0)),
                  pl.BlockSpec((1, d), lambda i: (0, 0)),
                  pl.BlockSpec((1, 8, d), mod_map),
                  pl.BlockSpec((N_EXPERTS, d), lambda i: (0, 0)),
                  pl.BlockSpec((N_EXPERTS, d), lambda i: (0, 0)),
                  pl.BlockSpec((N_EXPERTS, 1), lambda i: (0, 0)),
                  pl.BlockSpec((tm, tm), lambda i: (0, 0)),
                  pl.BlockSpec((N_EXPERTS, N_EXPERTS), lambda i: (0, 0))],
        out_specs=[per_tok, per_tok,
                   pl.BlockSpec((1, N_EXPERTS, LANES), lambda i: (i, 0, 0)),
                   pl.BlockSpec((N_EXPERTS, 1), lambda i: (0, 0))],
        out_shape=[jax.ShapeDtypeStruct((nt, TOP_K, tm), F32),
                   jax.ShapeDtypeStruct((nt, TOP_K, tm), I32),
                   jax.ShapeDtypeStruct((nt, N_EXPERTS, LANES), F32),
                   jax.ShapeDtypeStruct((N_EXPERTS, 1), F32)],
        compiler_params=_params(("arbitrary",)),
        name="router",
    )(xflat, g, modall, whi, wlo, bias, tri, ltri)


def _pack_bf16_pairs(v, is_bf16_valued=False):
    half = v.shape[1] // 2
    if not is_bf16_valued:
        v = v.astype(BF16).astype(F32)
    lo = lax.shift_right_logical(lax.bitcast_convert_type(v[:, :half], U32), jnp.uint32(16))
    hi = lax.bitcast_convert_type(v[:, half:], U32) & jnp.uint32(0xFFFF0000)
    return lo | hi


def _unpack_bf16_pairs(u):
    lo = lax.bitcast_convert_type(lax.shift_left(u, jnp.uint32(16)), F32).astype(BF16)
    hi = lax.bitcast_convert_type(u & jnp.uint32(0xFFFF0000), F32).astype(BF16)
    return lo, hi


def _chunk_copies(meta_ref, make_copy):
    total = lax.shift_right_logical(_slots_used(meta_ref), ROW_CHUNK.bit_length() - 1)
    pairs = lax.shift_right_logical(total, 1)

    def start_pair(j, carry):
        make_copy(2 * j).start()
        make_copy(2 * j + 1).start()
        return carry

    lax.fori_loop(0, pairs, start_pair, 0)

    @pl.when(total > 2 * pairs)
    def _():
        make_copy(total - 1).start()

    return total


def _wait_chunk_copies(total, copy_of_rows):
    group = lax.shift_right_logical(total, WAIT_GROUP.bit_length() - 1)

    def wait_group(j, carry):
        copy_of_rows(WAIT_GROUP * ROW_CHUNK).wait()
        return carry

    def wait_one(j, carry):
        copy_of_rows(ROW_CHUNK).wait()
        return carry

    lax.fori_loop(0, group, wait_group, 0)
    lax.fori_loop(0, total - group * WAIT_GROUP, wait_one, 0)


def _tile_slots(tm):
    return TOP_K * tm + ROW_CHUNK * N_EXPERTS


def _slots_used(meta_ref):
    last = N_EXPERTS - 1
    return meta_ref[0, 2, last] + meta_ref[0, 0, last]


def _dispatch_kernel(meta_ref, rows_ref, x_ref, g_ref, mod_ref, slot_ref, xs_ref, xc_scr, sem):
    hb = _norm_mod(x_ref[...], g_ref[...], mod_ref[0, 3:4, :], mod_ref[0, 4:5, :]).astype(BF16)
    tm = hb.shape[0]
    slot = slot_ref[0]

    def select_rows(r2, carry):
        for u in range(CHUNK_UNROLL):
            r0 = pl.multiple_of((r2 * CHUNK_UNROLL + u) * tm, tm)
            row = lax.broadcasted_iota(I32, (tm, tm), 0) + r0
            sel = jnp.zeros((tm, tm), F32)
            for k in range(TOP_K):
                sel = jnp.where(row == slot[k:k + 1, :], 1.0, sel)
            xc_scr[pl.ds(r0, tm), :] = _pack_bf16_pairs(_dot(sel.astype(BF16), hb), is_bf16_valued=True)
        return carry

    step = CHUNK_UNROLL * tm
    lax.fori_loop(0, (_slots_used(meta_ref) + step - 1) // step, select_rows, 0)

    def make_copy(j):
        src = pl.multiple_of(j * ROW_CHUNK, ROW_CHUNK)
        dst = pl.multiple_of(rows_ref[0, 0, j], ROW_CHUNK)
        return pltpu.make_async_copy(xc_scr.at[pl.ds(src, ROW_CHUNK)], xs_ref.at[pl.ds(dst, ROW_CHUNK)], sem)

    total = _chunk_copies(meta_ref, make_copy)

    _wait_chunk_copies(total, lambda n: pltpu.make_async_copy(xc_scr.at[pl.ds(0, n)], xs_ref.at[pl.ds(0, n)], sem))


def _dispatch(meta, chunk_rows, slot, xflat, g, modall, n_rows, tiles_per_batch, nlat):
    t, d = xflat.shape
    tm = TOK_TILE
    nt = t // tm

    def mod_map(i):
        return (2 * (i // tiles_per_batch) + ((i % tiles_per_batch) >= nlat).astype(I32), 0, 0)

    return pl.pallas_call(
        _dispatch_kernel,
        grid=(nt,),
        in_specs=[pl.BlockSpec((1, 3, N_EXPERTS), lambda i: (i, 0, 0), memory_space=pltpu.SMEM),
                  pl.BlockSpec((1, 1, chunk_rows.shape[2]), lambda i: (i, 0, 0), memory_space=pltpu.SMEM),
                  pl.BlockSpec((tm, d), lambda i: (i, 0)),
                  pl.BlockSpec((1, d), lambda i: (0, 0)),
                  pl.BlockSpec((1, 8, d), mod_map),
                  pl.BlockSpec((1, TOP_K, tm), lambda i: (i, 0, 0))],
        out_specs=pl.BlockSpec(memory_space=pl.ANY),
        out_shape=jax.ShapeDtypeStruct((n_rows, d // 2), U32),
        scratch_shapes=[pltpu.VMEM((_tile_slots(tm), d // 2), U32), pltpu.SemaphoreType.DMA(())],
        compiler_params=_params(("arbitrary",)),
        name="moe_dispatch",
    )(meta, chunk_rows, xflat, g, modall, slot)


def _expert_kernel(blk_e_ref, nvalid_ref, nact_ref, xs_ref, wg_ref, wu_ref, wd_ref, ys_ref, wg_s, wu_s, wd_s):
    b = pl.program_id(0)
    changed = jnp.logical_or(b == 0, blk_e_ref[b] != blk_e_ref[jnp.maximum(b - 1, 0)])

    @pl.when(changed)
    def _():
        wg_s[...] = wg_ref[0, 0].astype(BF16)
        wu_s[...] = wu_ref[0, 0].astype(BF16)
        wd_s[...] = wd_ref[0, 0].astype(BF16)

    half = xs_ref.shape[1]
    n_here = jnp.where(b < nact_ref[0], nvalid_ref[b], 0)

    def ffn(n_rows):
        row = lax.broadcasted_iota(I32, (n_rows, half), 0)
        xs = jnp.where(row < n_here, xs_ref[0:n_rows, :], jnp.zeros((n_rows, half), U32))
        x_lo, x_hi = _unpack_bf16_pairs(xs)
        gate = _dot(x_lo, wg_s[0:half, :]) + _dot(x_hi, wg_s[half:, :])
        up = _dot(x_lo, wu_s[0:half, :]) + _dot(x_hi, wu_s[half:, :])
        mid = (gate * _sigmoid(gate)) * up
        ys_ref[0:n_rows, :] = _pack_bf16_pairs(_dot(mid.astype(BF16), wd_s[...]))
        if n_rows < MOE_BLOCK:
            ys_ref[n_rows:, :] = jnp.zeros((MOE_BLOCK - n_rows, half), U32)

    pl.when(n_here > MOE_BLOCK // 2)(lambda: ffn(MOE_BLOCK))
    pl.when(jnp.logical_and(n_here > 0, n_here <= MOE_BLOCK // 2))(lambda: ffn(MOE_BLOCK // 2))

    @pl.when(n_here == 0)
    def _():
        ys_ref[...] = jnp.zeros_like(ys_ref)


def _experts(blk_e, nvalid, nact, xs, w_gate, w_up, w_down, layer):
    n_rows, half = xs.shape
    _, _, d, f = w_gate.shape
    n_blk = n_rows // MOE_BLOCK
    grid_spec = pltpu.PrefetchScalarGridSpec(
        num_scalar_prefetch=3,
        grid=(n_blk,),
        in_specs=[pl.BlockSpec((MOE_BLOCK, half), lambda i, be, nv, na: (jnp.minimum(i, na[0] - 1), 0)),
                  pl.BlockSpec((1, 1, d, f), lambda i, be, nv, na: (layer, be[i], 0, 0)),
                  pl.BlockSpec((1, 1, d, f), lambda i, be, nv, na: (layer, be[i], 0, 0)),
                  pl.BlockSpec((1, 1, f, d), lambda i, be, nv, na: (layer, be[i], 0, 0))],
        out_specs=pl.BlockSpec((MOE_BLOCK, half), lambda i, be, nv, na: (i, 0)),
        scratch_shapes=[pltpu.VMEM((d, f), BF16), pltpu.VMEM((d, f), BF16), pltpu.VMEM((f, d), BF16)],
    )
    return pl.pallas_call(
        _expert_kernel,
        grid_spec=grid_spec,
        out_shape=jax.ShapeDtypeStruct((n_rows, half), U32),
        compiler_params=_params(("arbitrary",)),
        name="moe_experts",
    )(blk_e, nvalid, nact, xs, w_gate, w_up, w_down)


def _combine_kernel(meta_ref, rows_ref, x_ref, g_ref, mod_ref, spos_ref, wt_ref, wsg_ref, wsu_ref, wsd_ref,
                    gfin_ref, ys_ref, o_ref, y_scr, acc_scr, sb_scr, wb_scr, sem, *, final_norm):
    tm, d = x_ref.shape
    half = d // 2

    @pl.when(pl.program_id(0) == 0)
    def _():
        y_scr[...] = jnp.zeros_like(y_scr)

    def make_copy(j):
        src = pl.multiple_of(rows_ref[0, 0, j], ROW_CHUNK)
        dst = pl.multiple_of(j * ROW_CHUNK, ROW_CHUNK)
        return pltpu.make_async_copy(ys_ref.at[pl.ds(src, ROW_CHUNK)], y_scr.at[pl.ds(dst, ROW_CHUNK)], sem)

    total = _chunk_copies(meta_ref, make_copy)

    x = x_ref[...]
    hb = _norm_mod(x, g_ref[...], mod_ref[0, 3:4, :], mod_ref[0, 4:5, :]).astype(BF16)
    gate = _dot(hb, wsg_ref[...])
    mid = (gate * _sigmoid(gate)) * _dot(hb, wsu_ref[...])
    acc_scr[...] = _dot(mid.astype(BF16), wsd_ref[...])

    _wait_chunk_copies(total, lambda n: pltpu.make_async_copy(ys_ref.at[pl.ds(0, n)], y_scr.at[pl.ds(0, n)], sem))

    n_used = _slots_used(meta_ref)
    for k in range(TOP_K):
        sb_scr[k] = jnp.broadcast_to(spos_ref[0, :, k:k + 1], (tm, LANES))
        wb_scr[k] = jnp.broadcast_to(wt_ref[0, :, k:k + 1], (tm, LANES))

    def chunk(j2, carry):
        lo_sum = jnp.zeros((tm, half), F32)
        hi_sum = jnp.zeros((tm, half), F32)
        for u in range(CHUNK_UNROLL):
            r0 = pl.multiple_of((j2 * CHUNK_UNROLL + u) * COMBINE_KCHUNK, COMBINE_KCHUNK)
            parts = []
            for c in range(COMBINE_KCHUNK // LANES):
                col = lax.broadcasted_iota(I32, (tm, LANES), 1) + (r0 + c * LANES)
                p = jnp.zeros((tm, LANES), F32)
                for k in range(TOP_K):
                    p = jnp.where(col == sb_scr[k], wb_scr[k], p)
                parts.append(p.astype(BF16))
            pb = jnp.concatenate(parts, axis=1)
            y_lo, y_hi = _unpack_bf16_pairs(y_scr[pl.ds(r0, COMBINE_KCHUNK), :])
            lo_sum = lo_sum + _dot(pb, y_lo)
            hi_sum = hi_sum + _dot(pb, y_hi)
        acc_scr[:, :half] += lo_sum
        acc_scr[:, half:] += hi_sum
        return carry

    step = CHUNK_UNROLL * COMBINE_KCHUNK
    lax.fori_loop(0, (n_used + step - 1) // step, chunk, 0)
    out = x + mod_ref[0, 5:6, :] * acc_scr[...]
    if final_norm:
        out = _rms(out, gfin_ref[...])
    o_ref[...] = out


def _combine(meta, chunk_rows, spos_t, wt, xflat, g, modall, wsg, wsu, wsd, gfin, ys, tiles_per_batch, nlat,
             final_norm):
    t, d = xflat.shape
    f = wsg.shape[1]
    tm = TOK_TILE
    nt = t // tm
    y_rows = _tile_slots(tm)
    assert y_rows % COMBINE_KCHUNK == 0

    def mod_map(i):
        return (2 * (i // tiles_per_batch) + ((i % tiles_per_batch) >= nlat).astype(I32), 0, 0)

    return pl.pallas_call(
        functools.partial(_combine_kernel, final_norm=final_norm),
        grid=(nt,),
        in_specs=[pl.BlockSpec((1, 3, N_EXPERTS), lambda i: (i, 0, 0), memory_space=pltpu.SMEM),
                  pl.BlockSpec((1, 1, chunk_rows.shape[2]), lambda i: (i, 0, 0), memory_space=pltpu.SMEM),
                  pl.BlockSpec((tm, d), lambda i: (i, 0)),
                  pl.BlockSpec((1, d), lambda i: (0, 0)),
                  pl.BlockSpec((1, 8, d), mod_map),
                  pl.BlockSpec((1, tm, TOP_K), lambda i: (i, 0, 0)),
                  pl.BlockSpec((1, tm, TOP_K), lambda i: (i, 0, 0)),
                  pl.BlockSpec((d, f), lambda i: (0, 0)),
                  pl.BlockSpec((d, f), lambda i: (0, 0)),
                  pl.BlockSpec((f, d), lambda i: (0, 0)),
                  pl.BlockSpec((1, d), lambda i: (0, 0)),
                  pl.BlockSpec(memory_space=pl.ANY)],
        out_specs=pl.BlockSpec((tm, d), lambda i: (i, 0)),
        out_shape=jax.ShapeDtypeStruct((t, d), F32),
        scratch_shapes=[pltpu.VMEM((y_rows, d // 2), U32), pltpu.VMEM((tm, d), F32),
                        pltpu.VMEM((TOP_K, tm, LANES), I32), pltpu.VMEM((TOP_K, tm, LANES), F32),
                        pltpu.SemaphoreType.DMA(())],
        compiler_params=_params(("arbitrary",)),
        name="moe_combine",
    )(meta, chunk_rows, xflat, g, modall, spos_t, wt, wsg, wsu, wsd, gfin, ys)


def _rope_tables(s_lat, c_len):
    n_freq = HEAD_DIM // 4
    inv = 1.0 / (ROPE_BASE ** (jnp.arange(n_freq, dtype=F32) / n_freq))
    t = jnp.arange(s_lat)
    pos = jnp.stack([(t // GRID_W).astype(F32), (t % GRID_W).astype(F32)], axis=1)
    lane = np.arange(LANES)
    ang = pos[:, (lane % HEAD_DIM) // 32] * inv[lane % 16][None, :]
    sign = jnp.asarray(np.where(lane % 32 < 16, -1.0, 1.0), F32)
    cos = jnp.concatenate([jnp.cos(ang), jnp.ones((c_len, LANES), F32)], axis=0)
    sin = jnp.concatenate([jnp.sin(ang) * sign, jnp.zeros((c_len, LANES), F32)], axis=0)
    return cos, sin


def _moe(xflat, lp, modall, gfin, tiles_per_batch, nlat, final_norm):
    t, d = xflat.shape
    nt = t // TOK_TILE
    g = lp["g_ffn"].reshape(1, d)
    w_r = lp["w_router"].T
    whi = w_r.astype(BF16)
    wlo = (w_r - whi.astype(F32)).astype(BF16)
    tri = jnp.asarray(np.triu(np.ones((TOK_TILE, TOK_TILE), np.float32), 1), BF16)
    ltri = jnp.asarray(np.tril(np.ones((N_EXPERTS, N_EXPERTS), np.float32), -1), BF16)
    w, slot, meta, cnt = _router(xflat, g, modall, whi, wlo, lp["router_bias"].reshape(N_EXPERTS, 1),
                                 tri, ltri, tiles_per_batch, nlat)
    meta = jnp.transpose(meta[:, :, :3].astype(I32), (0, 2, 1))
    rows_e = cnt[:, 0].astype(I32)
    seg = (rows_e + MOE_BLOCK - 1) // MOE_BLOCK * MOE_BLOCK
    pend = jnp.cumsum(seg)
    pstart = (pend - seg).astype(I32)
    n_blk = (t * TOP_K + ROW_CHUNK * N_EXPERTS * nt) // MOE_BLOCK + N_EXPERTS
    blk_row0 = jnp.arange(n_blk, dtype=I32) * MOE_BLOCK
    blk_e = jnp.minimum(jnp.sum((pend[None, :] <= blk_row0[:, None]).astype(I32), axis=1), N_EXPERTS - 1)
    of_blk = blk_e[:, None] == jnp.arange(N_EXPERTS, dtype=I32)[None, :]
    seg_end = jnp.sum(jnp.where(of_blk, (pstart + rows_e)[None, :], 0), axis=1)
    nvalid = jnp.clip(seg_end - blk_row0, 0, MOE_BLOCK).astype(I32)
    nact = (pend[-1] // MOE_BLOCK).astype(I32).reshape(1)
    n_pad, base, start = meta[:, 0, :], meta[:, 1, :], meta[:, 2, :]
    chunk_slot = jnp.arange(_tile_slots(TOK_TILE) // ROW_CHUNK, dtype=I32) * ROW_CHUNK
    s = chunk_slot[None, :, None]
    in_run = (start[:, None, :] <= s) & (s < (start + n_pad)[:, None, :])
    row0 = pstart[None, :] + base - start
    chunk_rows = (jnp.sum(jnp.where(in_run, row0[:, None, :], 0), axis=2) + chunk_slot[None, :])[:, None, :]
    xs = _dispatch(meta, chunk_rows, slot, xflat, g, modall, n_blk * MOE_BLOCK, tiles_per_batch, nlat)
    ys = _experts(blk_e, nvalid, nact, xs, lp["w_gate"], lp["w_up"], lp["w_down"], lp["layer"])
    return _combine(meta, chunk_rows, jnp.transpose(slot, (0, 2, 1)), jnp.transpose(w, (0, 2, 1)), xflat, g, modall,
                    lp["ws_gate"].astype(BF16), lp["ws_up"].astype(BF16), lp["ws_down"].astype(BF16), gfin, ys,
                    tiles_per_batch, nlat, final_norm)


def _layer(xc, cs, cos, sin, lp, layer_idx, last, s_lat, gfin):
    b, sall, d = xc.shape
    c_len = sall - s_lat
    lam_init = 0.8 - 0.6 * math.exp(-0.3 * layer_idx)
    mod = _modulation(cs, lp["w_mod"], lp["b_mod"])
    mod_lat = mod[:b].reshape(b, 1, 6, d)
    mod_ctx = jnp.broadcast_to(mod[b].reshape(1, 1, 6, d), (b, 1, 6, d))
    modall = jnp.concatenate([mod_lat, mod_ctx], axis=1)
    modall = jnp.pad(modall, ((0, 0), (0, 0), (0, 2), (0, 0))).reshape(2 * b, 8, d)

    qa, qb, qd, ka, va, kb, vb, kd, vd, vdt = _projection(
        xc, lp["g_mix"].reshape(1, d), modall, cos, sin, lp["w_in"].astype(BF16), s_lat)
    sink = lp["attn_sink"].astype(F32)
    lamv = jnp.zeros((8, LANES), F32).at[0:4, 0:HEAD_DIM].set(
        jnp.stack([lp["lam_q1"], lp["lam_k1"], lp["lam_q2"], lp["lam_k2"]]).astype(F32))
    subg = lp["subln_g"].reshape(1, LANES).astype(F32)
    oa = _swa(sink, qa, ka, va, s_lat)
    ob = _na(qb, kb, vb, _na_bias_mask(lp["na_rpb"], s_lat // GRID_W), s_lat)
    od = _diff(lamv, subg, qd, kd, vdt, s_lat, lam_init)
    ctx_out = None
    if not last:
        ctx_out = _ctx_attention(sink, lamv, subg, qa, ka, va, qb, kb, vb, qd, kd, vd, s_lat, lam_init)
    n_rows = s_lat if last else sall
    x2 = _out_projection(xc, (oa, ob, od), ctx_out, lp["w_out"].astype(BF16), modall, s_lat)
    tiles_per_batch = n_rows // TOK_TILE
    y = _moe(x2.reshape(b * n_rows, d), lp, modall, gfin, tiles_per_batch, s_lat // TOK_TILE, last)
    return y.reshape(b, n_rows, d)


_LAYER_KEYS = ("w_mod", "b_mod", "g_mix", "g_ffn", "w_in", "w_out", "attn_sink", "na_rpb", "lam_q1", "lam_k1",
               "lam_q2", "lam_k2", "subln_g", "w_router", "router_bias", "w_gate", "w_up", "w_down",
               "ws_gate", "ws_up", "ws_down")


def kernel(x, c, ctx, c_ctx, w_mod, b_mod, g_mix, g_ffn, w_in, w_out, attn_sink, na_rpb, lam_q1, lam_k1, lam_q2,
           lam_k2, subln_g, w_router, router_bias, w_gate, w_up, w_down, ws_gate, ws_up, ws_down, g_final):
    stacked = dict(zip(_LAYER_KEYS, (w_mod, b_mod, g_mix, g_ffn, w_in, w_out, attn_sink, na_rpb, lam_q1, lam_k1,
                                     lam_q2, lam_k2, subln_g, w_router, router_bias, w_gate, w_up, w_down,
                                     ws_gate, ws_up, ws_down)))
    b, s_lat, d = x.shape
    c_len = ctx.shape[1]
    depth = w_mod.shape[0]
    assert s_lat % (NA_QROWS * GRID_W) == 0 and s_lat % c_len == 0 and c_len == TOK_TILE
    cs = jnp.zeros((16, d), F32).at[:b].set(c).at[b].set(c_ctx)
    cos, sin = _rope_tables(s_lat, c_len)
    xc = jnp.concatenate([x, ctx], axis=1)
    gfin = g_final.reshape(1, d)
    for i in range(depth):
        big = ("w_gate", "w_up", "w_down")
        lp = {k: (v if k in big else v[i]) for k, v in stacked.items()}
        lp["layer"] = i
        xc = _layer(xc, cs, cos, sin, lp, i, i == depth - 1, s_lat, gfin)
    return xc
```

```python
import functools
import math

import numpy as np
import jax
import jax.numpy as jnp
from jax import lax
from jax.experimental import pallas as pl
from jax.experimental.pallas import tpu as pltpu

F32 = jnp.float32
BF16 = jnp.bfloat16
I32 = jnp.int32
U32 = jnp.uint32

GRID_W = 64
HEAD_DIM = 64
ROPE_BASE = 10000.0
SWA_WINDOW = 128
NA_KH = 8
NA_KW = 16
N_HEADS = 4
N_EXPERTS = 256
TOP_K = 8
N_GROUPS = 8
TOPK_GROUPS = 4
ROUTED_SCALE = 2.5
MOE_BLOCK = 1024
ROW_CHUNK = 8
WAIT_GROUP = 16
COMBINE_KCHUNK = 512
CHUNK_UNROLL = 2
EPS = 1e-6
NEG_INF = -1e30
LOG2E = math.log2(math.e)
Q_COLS = 1024
IN_COLS = 2816
MIX_WIDTH = 1024

LANES = 128
TOK_TILE = 256
ATT_TQ = 256
NA_QROWS = 4
NA_KROWS = NA_QROWS + NA_KH - 1
DIFF_KCHUNK = 256
DIFF_HEADS_PER_STEP = 2
VMEM_LIMIT = 48 * 1024 * 1024


def _params(sem):
    return pltpu.CompilerParams(dimension_semantics=sem, vmem_limit_bytes=VMEM_LIMIT)


def _dot(a, b):
    return jnp.dot(a, b, preferred_element_type=F32)


def _dot_nt(a, b):
    return lax.dot_general(a, b, (((1,), (1,)), ((), ())), preferred_element_type=F32)


def _split_bf16(a):
    hi = a.astype(BF16)
    lo = (a - hi.astype(F32)).astype(BF16)
    return hi, lo


def _sigmoid(x):
    return 1.0 / (1.0 + jnp.exp(-x))


def _rms(x, g):
    return x * lax.rsqrt(jnp.mean(x * x, axis=-1, keepdims=True) + EPS) * g


def _norm_mod(x, g, shift, scale):
    return _rms(x, g) * (1.0 + scale) + shift


def _mod_kernel(c_ref, w_ref, b_ref, o_ref):
    c = c_ref[...]
    a_hi, a_lo = _split_bf16(c * _sigmoid(c))
    w_hi, w_lo = _split_bf16(w_ref[...])
    o_ref[...] = _dot(a_hi, w_hi) + _dot(a_hi, w_lo) + _dot(a_lo, w_hi) + b_ref[...]


def _modulation(cs, w_mod, b_mod):
    n, d = cs.shape
    cols = w_mod.shape[1]
    tn = 1536
    return pl.pallas_call(
        _mod_kernel,
        grid=(cols // tn,),
        in_specs=[pl.BlockSpec((n, d), lambda j: (0, 0)),
                  pl.BlockSpec((d, tn), lambda j: (0, j)),
                  pl.BlockSpec((1, tn), lambda j: (0, j))],
        out_specs=pl.BlockSpec((n, tn), lambda j: (0, j)),
        out_shape=jax.ShapeDtypeStruct((n, cols), F32),
        compiler_params=_params(("arbitrary",)),
        name="modulation",
    )(cs, w_mod, b_mod.reshape(1, cols))


def _proj_kernel(x_ref, g_ref, mod_ref, cos_ref, sin_ref, w_ref,
                 qa_ref, qb_ref, qd_ref, ka_ref, va_ref, kb_ref, vb_ref, kd_ref, vd_ref, vdt_ref, h_scr):
    h = _norm_mod(x_ref[0], g_ref[...], mod_ref[0, 0:1, :], mod_ref[0, 1:2, :])
    h_scr[...] = h.astype(BF16)
    cos = cos_ref[...]
    sin = sin_ref[...]
    tm = h.shape[0]
    lane = lax.broadcasted_iota(I32, (tm, LANES), 1)
    first16 = (lane & 16) == 0
    lo64 = lane < HEAD_DIM

    def rope(v):
        partner = jnp.where(first16, pltpu.roll(v, LANES - 16, 1), pltpu.roll(v, 16, 1))
        return v * cos + partner * sin

    def mm(c0):
        return _dot(h_scr[...], w_ref[:, c0:c0 + 2 * LANES])

    a = mm(0) * (HEAD_DIM ** -0.5)
    c0 = rope(a[:, :LANES])
    c1 = rope(a[:, LANES:])
    zero = jnp.zeros_like(c0)
    qa_ref[0, :, 0 * LANES:1 * LANES] = jnp.where(lo64, c0, zero).astype(BF16)
    qa_ref[0, :, 1 * LANES:2 * LANES] = jnp.where(lo64, pltpu.roll(c0, HEAD_DIM, 1), zero).astype(BF16)
    qa_ref[0, :, 2 * LANES:3 * LANES] = jnp.where(lo64, zero, pltpu.roll(c1, HEAD_DIM, 1)).astype(BF16)
    qa_ref[0, :, 3 * LANES:4 * LANES] = jnp.where(lo64, zero, c1).astype(BF16)
    qb_ref[0] = (mm(256) * (HEAD_DIM ** -0.5)).astype(BF16)
    for j in range(2):
        a = mm(512 + 256 * j) * (HEAD_DIM ** -0.5 * LOG2E)
        qd_ref[0, :, 256 * j:256 * j + LANES] = rope(a[:, :LANES]).astype(BF16)
        qd_ref[0, :, 256 * j + LANES:256 * (j + 1)] = rope(a[:, LANES:]).astype(BF16)
    a = mm(1024)
    ka_ref[0] = rope(a[:, :LANES]).astype(BF16)
    va_ref[0] = a[:, LANES:].astype(BF16)
    kb_ref[0] = mm(1280).astype(BF16)
    vb_ref[0] = mm(1536).astype(BF16)
    for j in range(2):
        a = mm(1792 + 256 * j)
        kd_ref[0, :, 256 * j:256 * j + LANES] = rope(a[:, :LANES]).astype(BF16)
        kd_ref[0, :, 256 * j + LANES:256 * (j + 1)] = rope(a[:, LANES:]).astype(BF16)
        a = mm(2304 + 256 * j)
        vd_ref[0, :, 256 * j:256 * (j + 1)] = a.astype(BF16)
        vdt_ref[0, 256 * j:256 * j + LANES, :] = a[:, :LANES].T.astype(BF16)
        vdt_ref[0, 256 * j + LANES:256 * (j + 1), :] = a[:, LANES:].T.astype(BF16)


def _projection(xc, g, modall, cos, sin, w_in, s_lat):
    b, sall, d = xc.shape
    tm = TOK_TILE
    nlat = s_lat // tm
    row = lambda width: pl.BlockSpec((1, tm, width), lambda i, j: (i, j, 0))
    widths = (512, 256, 512, 128, 128, 256, 256, 512, 512)
    return pl.pallas_call(
        _proj_kernel,
        grid=(b, sall // tm),
        in_specs=[row(d),
                  pl.BlockSpec((1, d), lambda i, j: (0, 0)),
                  pl.BlockSpec((1, 8, d), lambda i, j: (2 * i + (j >= nlat).astype(I32), 0, 0)),
                  pl.BlockSpec((tm, LANES), lambda i, j: (j, 0)),
                  pl.BlockSpec((tm, LANES), lambda i, j: (j, 0)),
                  pl.BlockSpec((d, IN_COLS), lambda i, j: (0, 0))],
        out_specs=[row(w) for w in widths] + [pl.BlockSpec((1, 512, tm), lambda i, j: (i, 0, j))],
        out_shape=([jax.ShapeDtypeStruct((b, sall, w), BF16) for w in widths]
                   + [jax.ShapeDtypeStruct((b, 512, sall), BF16)]),
        scratch_shapes=[pltpu.VMEM((tm, d), BF16)],
        compiler_params=_params(("arbitrary", "arbitrary")),
        name="projection",
    )(xc, g, modall, cos, sin, w_in)


def _half_mask(q, half):
    lane = lax.broadcasted_iota(I32, q.shape, 1)
    keep = (lane < HEAD_DIM) if half == 0 else (lane >= HEAD_DIM)
    return jnp.where(keep, q, jnp.zeros_like(q))


def _merge_halves(lo_part, hi_part):
    lane = lax.broadcasted_iota(I32, lo_part.shape, 1)
    return jnp.where(lane < HEAD_DIM, lo_part, hi_part)


def _gqa_rows(q_ref, g):
    return jnp.concatenate([q_ref[0, :, (2 * g) * LANES:(2 * g + 1) * LANES],
                            q_ref[0, :, (2 * g + 1) * LANES:(2 * g + 2) * LANES]], axis=0)


def _gqa_sink(sink_ref, g, tq):
    rowi = lax.broadcasted_iota(I32, (2 * tq, 1), 0)
    return jnp.where(rowi < tq, sink_ref[2 * g], sink_ref[2 * g + 1])


def _gqa_store(o_ref, g, o, tq):
    top, bot = o[:tq], o[tq:]
    if g == 0:
        chunk = _merge_halves(top, pltpu.roll(bot, HEAD_DIM, 1))
    else:
        chunk = _merge_halves(pltpu.roll(top, HEAD_DIM, 1), bot)
    o_ref[0, :, g * LANES:(g + 1) * LANES] = chunk.astype(BF16)


def _lam(lamv_ref, lam_init):
    v = lamv_ref[...]
    a = jnp.sum(v[0:1] * v[1:2], axis=-1, keepdims=True)
    b = jnp.sum(v[2:3] * v[3:4], axis=-1, keepdims=True)
    return jnp.exp(a) - jnp.exp(b) + lam_init


def _diff_post(o, subg, lam_init):
    return _rms(o, subg) * (1.0 - lam_init)


def _swa_kernel(sink_ref, q_ref, k_ref, v_ref, o_ref, *, s_lat, c_len):
    tq = ATT_TQ
    tk = tq + 2 * SWA_WINDOW
    q0 = pl.program_id(1) * tq
    ks = pl.multiple_of(jnp.clip(q0 - SWA_WINDOW, 0, s_lat - tk), SWA_WINDOW)
    kwin = k_ref[0, pl.ds(ks, tk), :]
    vwin = v_ref[0, pl.ds(ks, tk), :]
    kc = k_ref[0, s_lat:s_lat + c_len, :]
    vc = v_ref[0, s_lat:s_lat + c_len, :]
    qpos = q0 + lax.broadcasted_iota(I32, (tq, tk), 0)
    kpos = ks + lax.broadcasted_iota(I32, (tq, tk), 1)
    valid = jnp.abs(qpos - kpos) <= SWA_WINDOW
    valid = jnp.concatenate([valid, valid], axis=0)
    for g in range(2):
        q2 = _gqa_rows(q_ref, g)
        s_loc = jnp.where(valid, _dot_nt(q2, kwin), NEG_INF)
        s_ctx = _dot_nt(q2, kc)
        snk = _gqa_sink(sink_ref, g, tq)
        m = jnp.maximum(jnp.maximum(jnp.max(s_loc, axis=-1, keepdims=True),
                                    jnp.max(s_ctx, axis=-1, keepdims=True)), snk)
        e_loc = jnp.exp(s_loc - m)
        e_ctx = jnp.exp(s_ctx - m)
        den = (jnp.sum(e_loc, axis=-1, keepdims=True) + jnp.sum(e_ctx, axis=-1, keepdims=True)
               + jnp.exp(snk - m))
        o = (_dot(e_loc.astype(BF16), vwin) + _dot(e_ctx.astype(BF16), vc)) / den
        _gqa_store(o_ref, g, o, tq)


def _swa(sink, qa, ka, va, s_lat):
    b, sall, _ = qa.shape
    kv = pl.BlockSpec((1, sall, LANES), lambda i, j: (i, 0, 0))
    return pl.pallas_call(
        functools.partial(_swa_kernel, s_lat=s_lat, c_len=sall - s_lat),
        grid=(b, s_lat // ATT_TQ),
        in_specs=[pl.BlockSpec(memory_space=pltpu.SMEM),
                  pl.BlockSpec((1, ATT_TQ, 512), lambda i, j: (i, j, 0)), kv, kv],
        out_specs=pl.BlockSpec((1, ATT_TQ, 256), lambda i, j: (i, j, 0)),
        out_shape=jax.ShapeDtypeStruct((b, s_lat, 256), BF16),
        compiler_params=_params(("arbitrary", "arbitrary")),
        name="swa",
    )(sink, qa, ka, va)


def _na_kernel(q_ref, k_ref, v_ref, bm_ref, o_ref, *, s_lat, c_len):
    rows = s_lat // GRID_W
    nk = NA_KROWS * GRID_W
    r0 = pl.program_id(1) * NA_QROWS
    ks = pl.multiple_of(jnp.clip(r0 - NA_KH // 2, 0, rows - NA_KROWS) * GRID_W, GRID_W)
    for c in range(2):
        sl = slice(c * LANES, (c + 1) * LANES)
        kwin = k_ref[0, pl.ds(ks, nk), sl]
        vwin = v_ref[0, pl.ds(ks, nk), sl]
        kc = k_ref[0, s_lat:s_lat + c_len, sl]
        vc = v_ref[0, s_lat:s_lat + c_len, sl]
        q = q_ref[0, :, sl]
        outs = []
        for half in range(2):
            qm = _half_mask(q, half)
            s_loc = _dot_nt(qm, kwin) + bm_ref[0, 2 * c + half]
            s_ctx = _dot_nt(qm, kc)
            m = jnp.maximum(jnp.max(s_loc, axis=-1, keepdims=True), jnp.max(s_ctx, axis=-1, keepdims=True))
            e_loc = jnp.exp(s_loc - m)
            e_ctx = jnp.exp(s_ctx - m)
            den = jnp.sum(e_loc, axis=-1, keepdims=True) + jnp.sum(e_ctx, axis=-1, keepdims=True)
            outs.append((_dot(e_loc.astype(BF16), vwin) + _dot(e_ctx.astype(BF16), vc)) / den)
        o_ref[0, :, sl] = _merge_halves(outs[0], outs[1]).astype(BF16)


def _na_bias_mask(rpb, rows):
    nq = NA_QROWS * GRID_W
    nk = NA_KROWS * GRID_W
    col = np.arange(GRID_W)
    cstart = np.clip(col - NA_KW // 2, 0, GRID_W - NA_KW)
    col_ok = (col[None, :] >= cstart[:, None]) & (col[None, :] < cstart[:, None] + NA_KW)
    dc = np.clip(col[None, :] - col[:, None], -(NA_KW - 1), NA_KW - 1) + NA_KW - 1
    sel_c = np.where(col_ok[:, :, None], np.eye(2 * NA_KW - 1)[dc], 0.0)
    tabs = []
    for r0 in (0, NA_QROWS, rows - NA_QROWS):
        ksr = int(np.clip(r0 - NA_KH // 2, 0, rows - NA_KROWS))
        r = r0 + np.arange(NA_QROWS)
        kr = ksr + np.arange(NA_KROWS)
        rs = np.clip(r - NA_KH // 2, 0, rows - NA_KH)
        row_ok = (kr[None, :] >= rs[:, None]) & (kr[None, :] < rs[:, None] + NA_KH)
        dr = np.clip(kr[None, :] - r[:, None], -(NA_KH - 1), NA_KH - 1) + NA_KH - 1
        sel_r = np.where(row_ok[:, :, None], np.eye(2 * NA_KH - 1)[dr], 0.0)
        ok = (row_ok[:, None, :, None] & col_ok[None, :, None, :]).reshape(nq, nk)
        bias = jnp.einsum("hdc,rjd,qkc->hrqjk", rpb.astype(F32), jnp.asarray(sel_r, F32), jnp.asarray(sel_c, F32),
                          precision=lax.Precision.HIGHEST).reshape(N_HEADS, nq, nk)
        tabs.append(jnp.where(ok[None], bias, NEG_INF))
    return jnp.stack(tabs)


def _na(qb, kb, vb, bm, s_lat):
    b, sall, _ = qb.shape
    nq = NA_QROWS * GRID_W
    nsteps = s_lat // nq
    kv = pl.BlockSpec((1, sall, 256), lambda i, j: (i, 0, 0))

    def bm_map(i, j):
        return (jnp.where(j == 0, 0, jnp.where(j == nsteps - 1, 2, 1)), 0, 0, 0)

    return pl.pallas_call(
        functools.partial(_na_kernel, s_lat=s_lat, c_len=sall - s_lat),
        grid=(b, nsteps),
        in_specs=[pl.BlockSpec((1, nq, 256), lambda i, j: (i, j, 0)), kv, kv,
                  pl.BlockSpec((1, N_HEADS, nq, NA_KROWS * GRID_W), bm_map)],
        out_specs=pl.BlockSpec((1, nq, 256), lambda i, j: (i, j, 0)),
        out_shape=jax.ShapeDtypeStruct((b, s_lat, 256), BF16),
        compiler_params=_params(("arbitrary", "arbitrary")),
        name="na2d",
    )(qb, kb, vb, bm)


def _diff_kernel(lamv_ref, subg_ref, q_ref, k_ref, vt_ref, o_ref, e_scr, m_scr, *, n_keys, lam_init):
    tq = ATT_TQ
    chunks = [(c0, min(DIFF_KCHUNK, n_keys - c0)) for c0 in range(0, n_keys, DIFF_KCHUNK)]
    lam = _lam(lamv_ref, lam_init)
    stats = []
    for hh in range(DIFF_HEADS_PER_STEP):
        sl = slice(hh * LANES, (hh + 1) * LANES)
        q = q_ref[0, :, sl]
        q12 = jnp.concatenate([_half_mask(q, 0), _half_mask(q, 1)], axis=0)
        m = jnp.full((1, 2 * tq), NEG_INF, F32)
        den = jnp.zeros((1, 2 * tq), F32)
        for i, (c0, n) in enumerate(chunks):
            s = _dot_nt(k_ref[0, c0:c0 + n, sl], q12)
            m_new = jnp.maximum(m, jnp.max(s, axis=0, keepdims=True))
            e = jnp.exp2(s - m_new)
            e_scr[hh, c0:c0 + n, :] = e
            m_scr[hh, i:i + 1, :] = m_new
            den = den * jnp.exp2(m - m_new) + jnp.sum(e, axis=0, keepdims=True)
            m = m_new
        stats.append((m, den))
    for hh in range(DIFF_HEADS_PER_STEP):
        sl = slice(hh * LANES, (hh + 1) * LANES)
        m, den = stats[hh]
        acc = jnp.zeros((LANES, tq), F32)
        for i, (c0, n) in enumerate(chunks):
            f = jnp.exp2(m_scr[hh, i:i + 1, :] - m) / den
            p = e_scr[hh, c0:c0 + n, :tq] * f[:, :tq] - e_scr[hh, c0:c0 + n, tq:] * (lam * f[:, tq:])
            acc = acc + _dot(vt_ref[0, sl, c0:c0 + n], p.astype(BF16))
        o_ref[0, :, sl] = _diff_post(acc.T, subg_ref[...], lam_init).astype(BF16)


def _diff(lamv, subg, qd, kd, vdt, s_lat, lam_init):
    b, sall, _ = qd.shape
    nchunk = -(-sall // DIFF_KCHUNK)
    hps = DIFF_HEADS_PER_STEP
    wid = hps * LANES
    return pl.pallas_call(
        functools.partial(_diff_kernel, n_keys=sall, lam_init=lam_init),
        grid=(b, N_HEADS // hps, s_lat // ATT_TQ),
        in_specs=[pl.BlockSpec((8, LANES), lambda i, h, j: (0, 0)),
                  pl.BlockSpec((1, LANES), lambda i, h, j: (0, 0)),
                  pl.BlockSpec((1, ATT_TQ, wid), lambda i, h, j: (i, j, h)),
                  pl.BlockSpec((1, sall, wid), lambda i, h, j: (i, 0, h)),
                  pl.BlockSpec((1, wid, sall), lambda i, h, j: (i, h, 0))],
        out_specs=pl.BlockSpec((1, ATT_TQ, wid), lambda i, h, j: (i, j, h)),
        out_shape=jax.ShapeDtypeStruct((b, s_lat, 512), BF16),
        scratch_shapes=[pltpu.VMEM((hps, sall, 2 * ATT_TQ), F32),
                        pltpu.VMEM((hps, -(-nchunk // 8) * 8, 2 * ATT_TQ), F32)],
        compiler_params=_params(("arbitrary", "arbitrary", "arbitrary")),
        name="diff_attn",
    )(lamv, subg, qd, kd, vdt)


def _ctx_kernel(sink_ref, lamv_ref, subg_ref, qa_ref, ka_ref, va_ref, qb_ref, kb_ref, vb_ref,
                qd_ref, kd_ref, vd_ref, oa_ref, ob_ref, od_ref, *, lam_init):
    cl = qa_ref.shape[1]
    ka = ka_ref[0]
    va = va_ref[0]
    for g in range(2):
        q2 = _gqa_rows(qa_ref, g)
        s = _dot_nt(q2, ka)
        snk = _gqa_sink(sink_ref, g, cl)
        m = jnp.maximum(jnp.max(s, axis=-1, keepdims=True), snk)
        e = jnp.exp(s - m)
        den = jnp.sum(e, axis=-1, keepdims=True) + jnp.exp(snk - m)
        _gqa_store(oa_ref, g, _dot(e.astype(BF16), va) / den, cl)
    for c in range(2):
        sl = slice(c * LANES, (c + 1) * LANES)
        outs = []
        for half in range(2):
            s = _dot_nt(_half_mask(qb_ref[0, :, sl], half), kb_ref[0, :, sl])
            e = jnp.exp(s - jnp.max(s, axis=-1, keepdims=True))
            outs.append(_dot(e.astype(BF16), vb_ref[0, :, sl]) / jnp.sum(e, axis=-1, keepdims=True))
        ob_ref[0, :, sl] = _merge_halves(outs[0], outs[1]).astype(BF16)
    lam = _lam(lamv_ref, lam_init)
    for h in range(N_HEADS):
        sl = slice(h * LANES, (h + 1) * LANES)
        q = qd_ref[0, :, sl]
        q12 = jnp.concatenate([_half_mask(q, 0), _half_mask(q, 1)], axis=0)
        s = _dot_nt(q12, kd_ref[0, :, sl])
        e = jnp.exp2(s - jnp.max(s, axis=-1, keepdims=True))
        den = jnp.sum(e, axis=-1, keepdims=True)
        p = e[:cl] * (1.0 / den[:cl]) - e[cl:] * (lam / den[cl:])
        o = _dot(p.astype(BF16), vd_ref[0, :, sl])
        od_ref[0, :, sl] = _diff_post(o, subg_ref[...], lam_init).astype(BF16)


def _ctx_attention(sink, lamv, subg, qa, ka, va, qb, kb, vb, qd, kd, vd, s_lat, lam_init):
    b, sall, _ = qa.shape
    cl = sall - s_lat
    blk = s_lat // cl
    row = lambda width: pl.BlockSpec((1, cl, width), lambda i: (i, blk, 0))
    out = lambda width: pl.BlockSpec((1, cl, width), lambda i: (i, 0, 0))
    return pl.pallas_call(
        functools.partial(_ctx_kernel, lam_init=lam_init),
        grid=(b,),
        in_specs=[pl.BlockSpec(memory_space=pltpu.SMEM),
                  pl.BlockSpec((8, LANES), lambda i: (0, 0)),
                  pl.BlockSpec((1, LANES), lambda i: (0, 0)),
                  row(512), row(128), row(128), row(256), row(256), row(256), row(512), row(512), row(512)],
        out_specs=[out(256), out(256), out(512)],
        out_shape=[jax.ShapeDtypeStruct((b, cl, w), BF16) for w in (256, 256, 512)],
        compiler_params=_params(("arbitrary",)),
        name="ctx_attention",
    )(sink, lamv, subg, qa, ka, va, qb, kb, vb, qd, kd, vd)


def _out_kernel(x_ref, w_ref, mod_ref, *refs, nlat):
    xo_ref = refs[-1]

    def emit(oa_ref, ob_ref, od_ref):
        attn = (_dot(oa_ref[0], w_ref[0:256, :]) + _dot(ob_ref[0], w_ref[256:512, :])
                + _dot(od_ref[0], w_ref[512:1024, :]))
        xo_ref[0] = x_ref[0] + mod_ref[0, 2:3, :] * attn

    if len(refs) == 4:
        emit(*refs[:3])
    else:
        is_ctx = pl.program_id(1) >= nlat
        pl.when(jnp.logical_not(is_ctx))(lambda: emit(*refs[:3]))
        pl.when(is_ctx)(lambda: emit(*refs[3:6]))


def _out_projection(xc, lat, ctx, w_out, modall, s_lat):
    b, sall, d = xc.shape
    tm = TOK_TILE
    nlat = s_lat // tm
    n_rows = s_lat if ctx is None else sall
    row = lambda width: pl.BlockSpec((1, tm, width), lambda i, j: (i, j, 0))
    lat_row = lambda width: pl.BlockSpec((1, tm, width), lambda i, j: (i, jnp.minimum(j, nlat - 1), 0))
    ctx_row = lambda width: pl.BlockSpec((1, tm, width), lambda i, j: (i, 0, 0))
    widths = (256, 256, 512)
    specs = [lat_row(w) for w in widths] + ([] if ctx is None else [ctx_row(w) for w in widths])
    return pl.pallas_call(
        functools.partial(_out_kernel, nlat=nlat),
        grid=(b, n_rows // tm),
        in_specs=[row(d),
                  pl.BlockSpec((MIX_WIDTH, d), lambda i, j: (0, 0)),
                  pl.BlockSpec((1, 8, d), lambda i, j: (2 * i + (j >= nlat).astype(I32), 0, 0))] + specs,
        out_specs=row(d),
        out_shape=jax.ShapeDtypeStruct((b, n_rows, d), F32),
        compiler_params=_params(("arbitrary", "arbitrary")),
        name="out_projection",
    )(xc, w_out, modall, *lat, *(() if ctx is None else ctx))


def _first_argmax(v, iota, n):
    m = jnp.max(v, axis=0, keepdims=True)
    ix = jnp.min(jnp.where(v == m, iota, float(n)), axis=0, keepdims=True)
    return m, ix


def _router_kernel(x_ref, g_ref, mod_ref, whi_ref, wlo_ref, bias_ref, tri_ref, ltri_ref,
                   w_ref, slot_ref, meta_ref, cnt_ref):
    @pl.when(pl.program_id(0) == 0)
    def _():
        cnt_ref[...] = jnp.zeros_like(cnt_ref)

    hf = _norm_mod(x_ref[...], g_ref[...], mod_ref[0, 3:4, :], mod_ref[0, 4:5, :])
    tm = hf.shape[0]
    h_hi, h_lo = _split_bf16(hf)
    whi = whi_ref[...]
    logits = _dot_nt(whi, h_hi) + _dot_nt(whi, h_lo) + _dot_nt(wlo_ref[...], h_hi)
    scores = _sigmoid(logits)
    biased = scores + bias_ref[...]
    gsz = N_EXPERTS // N_GROUPS
    iota_g = lax.broadcasted_iota(I32, (gsz, tm), 0).astype(F32)
    gscore = []
    for g in range(N_GROUPS):
        v = biased[g * gsz:(g + 1) * gsz]
        m1, i1 = _first_argmax(v, iota_g, gsz)
        m2 = jnp.max(jnp.where(iota_g == i1, -jnp.inf, v), axis=0, keepdims=True)
        gscore.append(m1 + m2)
    cur = jnp.concatenate(gscore, axis=0)
    iota_n = lax.broadcasted_iota(I32, (N_GROUPS, tm), 0).astype(F32)
    gsel = jnp.zeros((N_GROUPS, tm), F32)
    for _ in range(TOPK_GROUPS):
        _, ix = _first_argmax(cur, iota_n, N_GROUPS)
        hit = iota_n == ix
        gsel = jnp.where(hit, 1.0, gsel)
        cur = jnp.where(hit, -jnp.inf, cur)
    masked = jnp.concatenate(
        [jnp.where(gsel[g:g + 1] > 0.5, biased[g * gsz:(g + 1) * gsz], -jnp.inf) for g in range(N_GROUPS)],
        axis=0)
    iota_e = lax.broadcasted_iota(I32, (N_EXPERTS, tm), 0).astype(F32)
    onehot = jnp.zeros((N_EXPERTS, tm), F32)
    idxs, ws = [], []
    for _ in range(TOP_K):
        _, ix = _first_argmax(masked, iota_e, N_EXPERTS)
        hit = iota_e == ix
        ws.append(jnp.sum(jnp.where(hit, scores, 0.0), axis=0, keepdims=True))
        masked = jnp.where(hit, -jnp.inf, masked)
        onehot = jnp.where(hit, 1.0, onehot)
        idxs.append(ix)
    base = cnt_ref[...]
    rank = _dot(onehot.astype(BF16), tri_ref[...])
    n_col = jnp.sum(onehot, axis=1, keepdims=True)
    n_pad = jnp.floor((n_col + (ROW_CHUNK - 1)) * (1.0 / ROW_CHUNK)) * ROW_CHUNK
    start = _dot(ltri_ref[...], jnp.broadcast_to(n_pad, (N_EXPERTS, LANES)).astype(BF16))
    wsum = ws[0]
    for wk in ws[1:]:
        wsum = wsum + wk
    for k in range(TOP_K):
        slot = jnp.sum(jnp.where(iota_e == idxs[k], rank + start[:, 0:1], 0.0), axis=0, keepdims=True)
        slot_ref[0, k:k + 1, :] = slot.astype(I32)
        w_ref[0, k:k + 1, :] = ws[k] / wsum * ROUTED_SCALE
    lane = lax.broadcasted_iota(I32, (N_EXPERTS, LANES), 1)
    meta_ref[0] = jnp.where(lane == 0, n_pad, jnp.where(lane == 1, base, start))
    cnt_ref[...] = base + n_pad


def _router(xflat, g, modall, whi, wlo, bias, tri, ltri, tiles_per_batch, nlat):
    t, d = xflat.shape
    tm = TOK_TILE
    nt = t // tm
    per_tok = pl.BlockSpec((1, TOP_K, tm), lambda i: (i, 0, 0))

    def mod_map(i):
        return (2 * (i // tiles_per_batch) + ((i % tiles_per_batch) >= nlat).astype(I32), 0, 0)

    return pl.pallas_call(
        _router_kernel,
        grid=(nt,),
        in_specs=[pl.BlockSpec((tm, d), lambda i: (i, 0)),
                  pl.BlockSpec((1, d), lambda i: (0, 0)),
                  pl.BlockSpec((1, 8, d), mod_map),
                  pl.BlockSpec((N_EXPERTS, d), lambda i: (0, 0)),
                  pl.BlockSpec((N_EXPERTS, d), lambda i: (0, 0)),
                  pl.BlockSpec((N_EXPERTS, 1), lambda i: (0, 0)),
                  pl.BlockSpec((tm, tm), lambda i: (0, 0)),
                  pl.BlockSpec((N_EXPERTS, N_EXPERTS), lambda i: (0, 0))],
        out_specs=[per_tok, per_tok,
                   pl.BlockSpec((1, N_EXPERTS, LANES), lambda i: (i, 0, 0)),
                   pl.BlockSpec((N_EXPERTS, 1), lambda i: (0, 0))],
        out_shape=[jax.ShapeDtypeStruct((nt, TOP_K, tm), F32),
                   jax.ShapeDtypeStruct((nt, TOP_K, tm), I32),
                   jax.ShapeDtypeStruct((nt, N_EXPERTS, LANES), F32),
                   jax.ShapeDtypeStruct((N_EXPERTS, 1), F32)],
        compiler_params=_params(("arbitrary",)),
        name="router",
    )(xflat, g, modall, whi, wlo, bias, tri, ltri)


def _pack_bf16_pairs(v, is_bf16_valued=False):
    half = v.shape[1] // 2
    if not is_bf16_valued:
        v = v.astype(BF16).astype(F32)
    lo = lax.shift_right_logical(lax.bitcast_convert_type(v[:, :half], U32), jnp.uint32(16))
    hi = lax.bitcast_convert_type(v[:, half:], U32) & jnp.uint32(0xFFFF0000)
    return lo | hi


def _unpack_bf16_pairs(u):
    lo = lax.bitcast_convert_type(lax.shift_left(u, jnp.uint32(16)), F32).astype(BF16)
    hi = lax.bitcast_convert_type(u & jnp.uint32(0xFFFF0000), F32).astype(BF16)
    return lo, hi


def _used_chunks(meta_ref):
    return lax.shift_right_logical(_slots_used(meta_ref), ROW_CHUNK.bit_length() - 1)


def _start_chunk_copies(first, count, make_copy):
    pairs = lax.shift_right_logical(count, 1)

    def start_pair(j, carry):
        make_copy(first + 2 * j).start()
        make_copy(first + 2 * j + 1).start()
        return carry

    lax.fori_loop(0, pairs, start_pair, 0)

    @pl.when(count > 2 * pairs)
    def _():
        make_copy(first + count - 1).start()


def _wait_chunk_copies(total, copy_of_rows):
    group = lax.shift_right_logical(total, WAIT_GROUP.bit_length() - 1)

    def wait_group(j, carry):
        copy_of_rows(WAIT_GROUP * ROW_CHUNK).wait()
        return carry

    def wait_one(j, carry):
        copy_of_rows(ROW_CHUNK).wait()
        return carry

    lax.fori_loop(0, group, wait_group, 0)
    lax.fori_loop(0, total - group * WAIT_GROUP, wait_one, 0)


def _tile_slots(tm):
    return TOP_K * tm + ROW_CHUNK * N_EXPERTS


def _slots_used(meta_ref):
    last = N_EXPERTS - 1
    return meta_ref[0, 2, last] + meta_ref[0, 0, last]


def _dispatch_kernel(meta_ref, rows_ref, x_ref, g_ref, mod_ref, slot_ref, xs_ref, xc_scr, sem):
    hb = _norm_mod(x_ref[...], g_ref[...], mod_ref[0, 3:4, :], mod_ref[0, 4:5, :]).astype(BF16)
    tm = hb.shape[0]
    slot = slot_ref[0]

    total = _used_chunks(meta_ref)
    step = CHUNK_UNROLL * tm
    chunks_per_step = step // ROW_CHUNK

    def make_copy(j):
        src = pl.multiple_of(j * ROW_CHUNK, ROW_CHUNK)
        dst = pl.multiple_of(rows_ref[0, 0, j], ROW_CHUNK)
        return pltpu.make_async_copy(xc_scr.at[pl.ds(src, ROW_CHUNK)], xs_ref.at[pl.ds(dst, ROW_CHUNK)], sem)

    def select_rows(r2, carry):
        for u in range(CHUNK_UNROLL):
            r0 = pl.multiple_of((r2 * CHUNK_UNROLL + u) * tm, tm)
            row = lax.broadcasted_iota(I32, (tm, tm), 0) + r0
            sel = jnp.zeros((tm, tm), F32)
            for k in range(TOP_K):
                sel = jnp.where(row == slot[k:k + 1, :], 1.0, sel)
            xc_scr[pl.ds(r0, tm), :] = _pack_bf16_pairs(_dot(sel.astype(BF16), hb), is_bf16_valued=True)
        first = r2 * chunks_per_step
        _start_chunk_copies(first, jnp.clip(total - first, 0, chunks_per_step), make_copy)
        return carry

    lax.fori_loop(0, (_slots_used(meta_ref) + step - 1) // step, select_rows, 0)
    _wait_chunk_copies(total, lambda n: pltpu.make_async_copy(xc_scr.at[pl.ds(0, n)], xs_ref.at[pl.ds(0, n)], sem))


def _dispatch(meta, chunk_rows, slot, xflat, g, modall, n_rows, tiles_per_batch, nlat):
    t, d = xflat.shape
    tm = TOK_TILE
    nt = t // tm

    def mod_map(i):
        return (2 * (i // tiles_per_batch) + ((i % tiles_per_batch) >= nlat).astype(I32), 0, 0)

    return pl.pallas_call(
        _dispatch_kernel,
        grid=(nt,),
        in_specs=[pl.BlockSpec((1, 3, N_EXPERTS), lambda i: (i, 0, 0), memory_space=pltpu.SMEM),
                  pl.BlockSpec((1, 1, chunk_rows.shape[2]), lambda i: (i, 0, 0), memory_space=pltpu.SMEM),
                  pl.BlockSpec((tm, d), lambda i: (i, 0)),
                  pl.BlockSpec((1, d), lambda i: (0, 0)),
                  pl.BlockSpec((1, 8, d), mod_map),
                  pl.BlockSpec((1, TOP_K, tm), lambda i: (i, 0, 0))],
        out_specs=pl.BlockSpec(memory_space=pl.ANY),
        out_shape=jax.ShapeDtypeStruct((n_rows, d // 2), U32),
        scratch_shapes=[pltpu.VMEM((_tile_slots(tm), d // 2), U32), pltpu.SemaphoreType.DMA(())],
        compiler_params=_params(("arbitrary",)),
        name="moe_dispatch",
    )(meta, chunk_rows, xflat, g, modall, slot)


def _expert_kernel(blk_e_ref, nvalid_ref, nact_ref, xs_ref, wg_ref, wu_ref, wd_ref, ys_ref, wg_s, wu_s, wd_s):
    b = pl.program_id(0)
    changed = jnp.logical_or(b == 0, blk_e_ref[b] != blk_e_ref[jnp.maximum(b - 1, 0)])

    @pl.when(changed)
    def _():
        wg_s[...] = wg_ref[0, 0].astype(BF16)
        wu_s[...] = wu_ref[0, 0].astype(BF16)
        wd_s[...] = wd_ref[0, 0].astype(BF16)

    half = xs_ref.shape[1]
    n_here = jnp.where(b < nact_ref[0], nvalid_ref[b], 0)

    @pl.when(n_here > 0)
    def _():
        row = lax.broadcasted_iota(I32, xs_ref.shape, 0)
        xs = jnp.where(row < n_here, xs_ref[...], jnp.zeros(xs_ref.shape, U32))
        x_lo, x_hi = _unpack_bf16_pairs(xs)
        gate = _dot(x_lo, wg_s[0:half, :]) + _dot(x_hi, wg_s[half:, :])
        up = _dot(x_lo, wu_s[0:half, :]) + _dot(x_hi, wu_s[half:, :])
        mid = (gate * _sigmoid(gate)) * up
        ys_ref[...] = _pack_bf16_pairs(_dot(mid.astype(BF16), wd_s[...]))

    @pl.when(n_here == 0)
    def _():
        ys_ref[...] = jnp.zeros_like(ys_ref)


def _experts(blk_e, nvalid, nact, xs, w_gate, w_up, w_down, layer):
    n_rows, half = xs.shape
    _, _, d, f = w_gate.shape
    n_blk = n_rows // MOE_BLOCK
    grid_spec = pltpu.PrefetchScalarGridSpec(
        num_scalar_prefetch=3,
        grid=(n_blk,),
        in_specs=[pl.BlockSpec((MOE_BLOCK, half), lambda i, be, nv, na: (jnp.minimum(i, na[0] - 1), 0)),
                  pl.BlockSpec((1, 1, d, f), lambda i, be, nv, na: (layer, be[i], 0, 0)),
                  pl.BlockSpec((1, 1, d, f), lambda i, be, nv, na: (layer, be[i], 0, 0)),
                  pl.BlockSpec((1, 1, f, d), lambda i, be, nv, na: (layer, be[i], 0, 0))],
        out_specs=pl.BlockSpec((MOE_BLOCK, half), lambda i, be, nv, na: (i, 0)),
        scratch_shapes=[pltpu.VMEM((d, f), BF16), pltpu.VMEM((d, f), BF16), pltpu.VMEM((f, d), BF16)],
    )
    return pl.pallas_call(
        _expert_kernel,
        grid_spec=grid_spec,
        out_shape=jax.ShapeDtypeStruct((n_rows, half), U32),
        compiler_params=_params(("arbitrary",)),
        name="moe_experts",
    )(blk_e, nvalid, nact, xs, w_gate, w_up, w_down)


def _combine_kernel(meta_ref, rows_ref, x_ref, g_ref, mod_ref, spos_ref, wt_ref, wsg_ref, wsu_ref, wsd_ref,
                    gfin_ref, ys_ref, o_ref, y_scr, acc_scr, sb_scr, wb_scr, sem, *, final_norm):
    tm, d = x_ref.shape
    half = d // 2

    @pl.when(pl.program_id(0) == 0)
    def _():
        y_scr[...] = jnp.zeros_like(y_scr)

    step = CHUNK_UNROLL * COMBINE_KCHUNK
    chunks_per_step = step // ROW_CHUNK
    group_shift = chunks_per_step.bit_length() - 1

    def make_copy(j):
        src = pl.multiple_of(rows_ref[0, 0, j], ROW_CHUNK)
        dst = pl.multiple_of(j * ROW_CHUNK, ROW_CHUNK)
        return pltpu.make_async_copy(ys_ref.at[pl.ds(src, ROW_CHUNK)], y_scr.at[pl.ds(dst, ROW_CHUNK)],
                                     sem.at[lax.shift_right_logical(j, group_shift)])

    total = _used_chunks(meta_ref)
    _start_chunk_copies(0, total, make_copy)

    x = x_ref[...]
    hb = _norm_mod(x, g_ref[...], mod_ref[0, 3:4, :], mod_ref[0, 4:5, :]).astype(BF16)
    gate = _dot(hb, wsg_ref[...])
    mid = (gate * _sigmoid(gate)) * _dot(hb, wsu_ref[...])
    acc_scr[...] = _dot(mid.astype(BF16), wsd_ref[...])

    n_used = _slots_used(meta_ref)
    for k in range(TOP_K):
        sb_scr[k] = jnp.broadcast_to(spos_ref[0, :, k:k + 1], (tm, LANES))
        wb_scr[k] = jnp.broadcast_to(wt_ref[0, :, k:k + 1], (tm, LANES))

    def chunk(j2, carry):
        _wait_chunk_copies(jnp.clip(total - j2 * chunks_per_step, 0, chunks_per_step),
                           lambda n: pltpu.make_async_copy(ys_ref.at[pl.ds(0, n)], y_scr.at[pl.ds(0, n)], sem.at[j2]))
        lo_sum = jnp.zeros((tm, half), F32)
        hi_sum = jnp.zeros((tm, half), F32)
        for u in range(CHUNK_UNROLL):
            r0 = pl.multiple_of((j2 * CHUNK_UNROLL + u) * COMBINE_KCHUNK, COMBINE_KCHUNK)
            parts = []
            for c in range(COMBINE_KCHUNK // LANES):
                col = lax.broadcasted_iota(I32, (tm, LANES), 1) + (r0 + c * LANES)
                p = jnp.zeros((tm, LANES), F32)
                for k in range(TOP_K):
                    p = jnp.where(col == sb_scr[k], wb_scr[k], p)
                parts.append(p.astype(BF16))
            pb = jnp.concatenate(parts, axis=1)
            y_lo, y_hi = _unpack_bf16_pairs(y_scr[pl.ds(r0, COMBINE_KCHUNK), :])
            lo_sum = lo_sum + _dot(pb, y_lo)
            hi_sum = hi_sum + _dot(pb, y_hi)
        acc_scr[:, :half] += lo_sum
        acc_scr[:, half:] += hi_sum
        return carry

    lax.fori_loop(0, (n_used + step - 1) // step, chunk, 0)
    out = x + mod_ref[0, 5:6, :] * acc_scr[...]
    if final_norm:
        out = _rms(out, gfin_ref[...])
    o_ref[...] = out


def _combine(meta, chunk_rows, spos_t, wt, xflat, g, modall, wsg, wsu, wsd, gfin, ys, tiles_per_batch, nlat,
             final_norm):
    t, d = xflat.shape
    f = wsg.shape[1]
    tm = TOK_TILE
    nt = t // tm
    y_rows = _tile_slots(tm)
    assert y_rows % COMBINE_KCHUNK == 0

    def mod_map(i):
        return (2 * (i // tiles_per_batch) + ((i % tiles_per_batch) >= nlat).astype(I32), 0, 0)

    return pl.pallas_call(
        functools.partial(_combine_kernel, final_norm=final_norm),
        grid=(nt,),
        in_specs=[pl.BlockSpec((1, 3, N_EXPERTS), lambda i: (i, 0, 0), memory_space=pltpu.SMEM),
                  pl.BlockSpec((1, 1, chunk_rows.shape[2]), lambda i: (i, 0, 0), memory_space=pltpu.SMEM),
                  pl.BlockSpec((tm, d), lambda i: (i, 0)),
                  pl.BlockSpec((1, d), lambda i: (0, 0)),
                  pl.BlockSpec((1, 8, d), mod_map),
                  pl.BlockSpec((1, tm, TOP_K), lambda i: (i, 0, 0)),
                  pl.BlockSpec((1, tm, TOP_K), lambda i: (i, 0, 0)),
                  pl.BlockSpec((d, f), lambda i: (0, 0)),
                  pl.BlockSpec((d, f), lambda i: (0, 0)),
                  pl.BlockSpec((f, d), lambda i: (0, 0)),
                  pl.BlockSpec((1, d), lambda i: (0, 0)),
                  pl.BlockSpec(memory_space=pl.ANY)],
        out_specs=pl.BlockSpec((tm, d), lambda i: (i, 0)),
        out_shape=jax.ShapeDtypeStruct((t, d), F32),
        scratch_shapes=[pltpu.VMEM((y_rows, d // 2), U32), pltpu.VMEM((tm, d), F32),
                        pltpu.VMEM((TOP_K, tm, LANES), I32), pltpu.VMEM((TOP_K, tm, LANES), F32),
                        pltpu.SemaphoreType.DMA((y_rows // (CHUNK_UNROLL * COMBINE_KCHUNK),))],
        compiler_params=_params(("arbitrary",)),
        name="moe_combine",
    )(meta, chunk_rows, xflat, g, modall, spos_t, wt, wsg, wsu, wsd, gfin, ys)


def _rope_tables(s_lat, c_len):
    n_freq = HEAD_DIM // 4
    inv = 1.0 / (ROPE_BASE ** (jnp.arange(n_freq, dtype=F32) / n_freq))
    t = jnp.arange(s_lat)
    pos = jnp.stack([(t // GRID_W).astype(F32), (t % GRID_W).astype(F32)], axis=1)
    lane = np.arange(LANES)
    ang = pos[:, (lane % HEAD_DIM) // 32] * inv[lane % 16][None, :]
    sign = jnp.asarray(np.where(lane % 32 < 16, -1.0, 1.0), F32)
    cos = jnp.concatenate([jnp.cos(ang), jnp.ones((c_len, LANES), F32)], axis=0)
    sin = jnp.concatenate([jnp.sin(ang) * sign, jnp.zeros((c_len, LANES), F32)], axis=0)
    return cos, sin


def _moe(xflat, lp, modall, gfin, tiles_per_batch, nlat, final_norm):
    t, d = xflat.shape
    nt = t // TOK_TILE
    g = lp["g_ffn"].reshape(1, d)
    w_r = lp["w_router"].T
    whi = w_r.astype(BF16)
    wlo = (w_r - whi.astype(F32)).astype(BF16)
    tri = jnp.asarray(np.triu(np.ones((TOK_TILE, TOK_TILE), np.float32), 1), BF16)
    ltri = jnp.asarray(np.tril(np.ones((N_EXPERTS, N_EXPERTS), np.float32), -1), BF16)
    w, slot, meta, cnt = _router(xflat, g, modall, whi, wlo, lp["router_bias"].reshape(N_EXPERTS, 1),
                                 tri, ltri, tiles_per_batch, nlat)
    meta = jnp.transpose(meta[:, :, :3].astype(I32), (0, 2, 1))
    rows_e = cnt[:, 0].astype(I32)
    seg = (rows_e + MOE_BLOCK - 1) // MOE_BLOCK * MOE_BLOCK
    pend = jnp.cumsum(seg)
    pstart = (pend - seg).astype(I32)
    n_blk = (t * TOP_K + ROW_CHUNK * N_EXPERTS * nt) // MOE_BLOCK + N_EXPERTS
    blk_row0 = jnp.arange(n_blk, dtype=I32) * MOE_BLOCK
    blk_e = jnp.minimum(jnp.sum((pend[None, :] <= blk_row0[:, None]).astype(I32), axis=1), N_EXPERTS - 1)
    of_blk = blk_e[:, None] == jnp.arange(N_EXPERTS, dtype=I32)[None, :]
    seg_end = jnp.sum(jnp.where(of_blk, (pstart + rows_e)[None, :], 0), axis=1)
    nvalid = jnp.clip(seg_end - blk_row0, 0, MOE_BLOCK).astype(I32)
    nact = (pend[-1] // MOE_BLOCK).astype(I32).reshape(1)
    n_pad, base, start = meta[:, 0, :], meta[:, 1, :], meta[:, 2, :]
    chunk_slot = jnp.arange(_tile_slots(TOK_TILE) // ROW_CHUNK, dtype=I32) * ROW_CHUNK
    s = chunk_slot[None, :, None]
    in_run = (start[:, None, :] <= s) & (s < (start + n_pad)[:, None, :])
    row0 = pstart[None, :] + base - start
    chunk_rows = (jnp.sum(jnp.where(in_run, row0[:, None, :], 0), axis=2) + chunk_slot[None, :])[:, None, :]
    xs = _dispatch(meta, chunk_rows, slot, xflat, g, modall, n_blk * MOE_BLOCK, tiles_per_batch, nlat)
    ys = _experts(blk_e, nvalid, nact, xs, lp["w_gate"], lp["w_up"], lp["w_down"], lp["layer"])
    return _combine(meta, chunk_rows, jnp.transpose(slot, (0, 2, 1)), jnp.transpose(w, (0, 2, 1)), xflat, g, modall,
                    lp["ws_gate"].astype(BF16), lp["ws_up"].astype(BF16), lp["ws_down"].astype(BF16), gfin, ys,
                    tiles_per_batch, nlat, final_norm)


def _layer(xc, cs, cos, sin, lp, layer_idx, last, s_lat, gfin):
    b, sall, d = xc.shape
    c_len = sall - s_lat
    lam_init = 0.8 - 0.6 * math.exp(-0.3 * layer_idx)
    mod = _modulation(cs, lp["w_mod"], lp["b_mod"])
    mod_lat = mod[:b].reshape(b, 1, 6, d)
    mod_ctx = jnp.broadcast_to(mod[b].reshape(1, 1, 6, d), (b, 1, 6, d))
    modall = jnp.concatenate([mod_lat, mod_ctx], axis=1)
    modall = jnp.pad(modall, ((0, 0), (0, 0), (0, 2), (0, 0))).reshape(2 * b, 8, d)

    qa, qb, qd, ka, va, kb, vb, kd, vd, vdt = _projection(
        xc, lp["g_mix"].reshape(1, d), modall, cos, sin, lp["w_in"].astype(BF16), s_lat)
    sink = lp["attn_sink"].astype(F32)
    lamv = jnp.zeros((8, LANES), F32).at[0:4, 0:HEAD_DIM].set(
        jnp.stack([lp["lam_q1"], lp["lam_k1"], lp["lam_q2"], lp["lam_k2"]]).astype(F32))
    subg = lp["subln_g"].reshape(1, LANES).astype(F32)
    oa = _swa(sink, qa, ka, va, s_lat)
    ob = _na(qb, kb, vb, _na_bias_mask(lp["na_rpb"], s_lat // GRID_W), s_lat)
    od = _diff(lamv, subg, qd, kd, vdt, s_lat, lam_init)
    ctx_out = None
    if not last:
        ctx_out = _ctx_attention(sink, lamv, subg, qa, ka, va, qb, kb, vb, qd, kd, vd, s_lat, lam_init)
    n_rows = s_lat if last else sall
    x2 = _out_projection(xc, (oa, ob, od), ctx_out, lp["w_out"].astype(BF16), modall, s_lat)
    tiles_per_batch = n_rows // TOK_TILE
    y = _moe(x2.reshape(b * n_rows, d), lp, modall, gfin, tiles_per_batch, s_lat // TOK_TILE, last)
    return y.reshape(b, n_rows, d)


_LAYER_KEYS = ("w_mod", "b_mod", "g_mix", "g_ffn", "w_in", "w_out", "attn_sink", "na_rpb", "lam_q1", "lam_k1",
               "lam_q2", "lam_k2", "subln_g", "w_router", "router_bias", "w_gate", "w_up", "w_down",
               "ws_gate", "ws_up", "ws_down")


def kernel(x, c, ctx, c_ctx, w_mod, b_mod, g_mix, g_ffn, w_in, w_out, attn_sink, na_rpb, lam_q1, lam_k1, lam_q2,
           lam_k2, subln_g, w_router, router_bias, w_gate, w_up, w_down, ws_gate, ws_up, ws_down, g_final):
    stacked = dict(zip(_LAYER_KEYS, (w_mod, b_mod, g_mix, g_ffn, w_in, w_out, attn_sink, na_rpb, lam_q1, lam_k1,
                                     lam_q2, lam_k2, subln_g, w_router, router_bias, w_gate, w_up, w_down,
                                     ws_gate, ws_up, ws_down)))
    b, s_lat, d = x.shape
    c_len = ctx.shape[1]
    depth = w_mod.shape[0]
    assert s_lat % (NA_QROWS * GRID_W) == 0 and s_lat % c_len == 0 and c_len == TOK_TILE
    cs = jnp.zeros((16, d), F32).at[:b].set(c).at[b].set(c_ctx)
    cos, sin = _rope_tables(s_lat, c_len)
    xc = jnp.concatenate([x, ctx], axis=1)
    gfin = g_final.reshape(1, d)
    for i in range(depth):
        big = ("w_gate", "w_up", "w_down")
        lp = {k: (v if k in big else v[i]) for k, v in stacked.items()}
        lp["layer"] = i
        xc = _layer(xc, cs, cos, sin, lp, i, i == depth - 1, s_lat, gfin)
    return xc
```

```python
import functools
import math

import numpy as np
import jax
import jax.numpy as jnp
from jax import lax
from jax.experimental import pallas as pl
from jax.experimental.pallas import tpu as pltpu

F32 = jnp.float32
BF16 = jnp.bfloat16
I32 = jnp.int32
U32 = jnp.uint32

GRID_W = 64
HEAD_DIM = 64
ROPE_BASE = 10000.0
SWA_WINDOW = 128
NA_KH = 8
NA_KW = 16
N_HEADS = 4
N_EXPERTS = 256
TOP_K = 8
N_GROUPS = 8
TOPK_GROUPS = 4
ROUTED_SCALE = 2.5
MOE_BLOCK = 1024
ROW_CHUNK = 8
WAIT_GROUP = 16
COMBINE_KCHUNK = 512
CHUNK_UNROLL = 2
EPS = 1e-6
NEG_INF = -1e30
LOG2E = math.log2(math.e)
Q_COLS = 1024
IN_COLS = 2816
MIX_WIDTH = 1024

LANES = 128
TOK_TILE = 256
ATT_TQ = 256
NA_QROWS = 4
NA_KROWS = NA_QROWS + NA_KH - 1
DIFF_KCHUNK = 256
DIFF_HEADS_PER_STEP = 2
VMEM_LIMIT = 48 * 1024 * 1024


def _params(sem):
    return pltpu.CompilerParams(dimension_semantics=sem, vmem_limit_bytes=VMEM_LIMIT)


def _dot(a, b):
    return jnp.dot(a, b, preferred_element_type=F32)


def _dot_nt(a, b):
    return lax.dot_general(a, b, (((1,), (1,)), ((), ())), preferred_element_type=F32)


def _split_bf16(a):
    hi = a.astype(BF16)
    lo = (a - hi.astype(F32)).astype(BF16)
    return hi, lo


def _sigmoid(x):
    return 1.0 / (1.0 + jnp.exp(-x))


def _rms(x, g):
    return x * lax.rsqrt(jnp.mean(x * x, axis=-1, keepdims=True) + EPS) * g


def _norm_mod(x, g, shift, scale):
    return _rms(x, g) * (1.0 + scale) + shift


def _mod_kernel(c_ref, w_ref, b_ref, o_ref):
    c = c_ref[...]
    a_hi, a_lo = _split_bf16(c * _sigmoid(c))
    w_hi, w_lo = _split_bf16(w_ref[...])
    o_ref[...] = _dot(a_hi, w_hi) + _dot(a_hi, w_lo) + _dot(a_lo, w_hi) + b_ref[...]


def _modulation(cs, w_mod, b_mod):
    n, d = cs.shape
    cols = w_mod.shape[1]
    tn = 1536
    return pl.pallas_call(
        _mod_kernel,
        grid=(cols // tn,),
        in_specs=[pl.BlockSpec((n, d), lambda j: (0, 0)),
                  pl.BlockSpec((d, tn), lambda j: (0, j)),
                  pl.BlockSpec((1, tn), lambda j: (0, j))],
        out_specs=pl.BlockSpec((n, tn), lambda j: (0, j)),
        out_shape=jax.ShapeDtypeStruct((n, cols), F32),
        compiler_params=_params(("arbitrary",)),
        name="modulation",
    )(cs, w_mod, b_mod.reshape(1, cols))


def _proj_kernel(x_ref, g_ref, mod_ref, cos_ref, sin_ref, w_ref,
                 qa_ref, qb_ref, qd_ref, ka_ref, va_ref, kb_ref, vb_ref, kd_ref, vd_ref, vdt_ref, h_scr):
    h = _norm_mod(x_ref[0], g_ref[...], mod_ref[0, 0:1, :], mod_ref[0, 1:2, :])
    h_scr[...] = h.astype(BF16)
    cos = cos_ref[...]
    sin = sin_ref[...]
    tm = h.shape[0]
    lane = lax.broadcasted_iota(I32, (tm, LANES), 1)
    first16 = (lane & 16) == 0
    lo64 = lane < HEAD_DIM

    def rope(v):
        partner = jnp.where(first16, pltpu.roll(v, LANES - 16, 1), pltpu.roll(v, 16, 1))
        return v * cos + partner * sin

    def mm(c0):
        return _dot(h_scr[...], w_ref[:, c0:c0 + 2 * LANES])

    a = mm(0) * (HEAD_DIM ** -0.5)
    c0 = rope(a[:, :LANES])
    c1 = rope(a[:, LANES:])
    zero = jnp.zeros_like(c0)
    qa_ref[0, :, 0 * LANES:1 * LANES] = jnp.where(lo64, c0, zero).astype(BF16)
    qa_ref[0, :, 1 * LANES:2 * LANES] = jnp.where(lo64, pltpu.roll(c0, HEAD_DIM, 1), zero).astype(BF16)
    qa_ref[0, :, 2 * LANES:3 * LANES] = jnp.where(lo64, zero, pltpu.roll(c1, HEAD_DIM, 1)).astype(BF16)
    qa_ref[0, :, 3 * LANES:4 * LANES] = jnp.where(lo64, zero, c1).astype(BF16)
    qb_ref[0] = (mm(256) * (HEAD_DIM ** -0.5)).astype(BF16)
    for j in range(2):
        a = mm(512 + 256 * j) * (HEAD_DIM ** -0.5 * LOG2E)
        qd_ref[0, :, 256 * j:256 * j + LANES] = rope(a[:, :LANES]).astype(BF16)
        qd_ref[0, :, 256 * j + LANES:256 * (j + 1)] = rope(a[:, LANES:]).astype(BF16)
    a = mm(1024)
    ka_ref[0] = rope(a[:, :LANES]).astype(BF16)
    va_ref[0] = a[:, LANES:].astype(BF16)
    kb_ref[0] = mm(1280).astype(BF16)
    vb_ref[0] = mm(1536).astype(BF16)
    for j in range(2):
        a = mm(1792 + 256 * j)
        kd_ref[0, :, 256 * j:256 * j + LANES] = rope(a[:, :LANES]).astype(BF16)
        kd_ref[0, :, 256 * j + LANES:256 * (j + 1)] = rope(a[:, LANES:]).astype(BF16)
        a = mm(2304 + 256 * j)
        vd_ref[0, :, 256 * j:256 * (j + 1)] = a.astype(BF16)
        vdt_ref[0, 256 * j:256 * j + LANES, :] = a[:, :LANES].T.astype(BF16)
        vdt_ref[0, 256 * j + LANES:256 * (j + 1), :] = a[:, LANES:].T.astype(BF16)


def _projection(xc, g, modall, cos, sin, w_in, s_lat):
    b, sall, d = xc.shape
    tm = TOK_TILE
    nlat = s_lat // tm
    row = lambda width: pl.BlockSpec((1, tm, width), lambda i, j: (i, j, 0))
    widths = (512, 256, 512, 128, 128, 256, 256, 512, 512)
    return pl.pallas_call(
        _proj_kernel,
        grid=(b, sall // tm),
        in_specs=[row(d),
                  pl.BlockSpec((1, d), lambda i, j: (0, 0)),
                  pl.BlockSpec((1, 8, d), lambda i, j: (2 * i + (j >= nlat).astype(I32), 0, 0)),
                  pl.BlockSpec((tm, LANES), lambda i, j: (j, 0)),
                  pl.BlockSpec((tm, LANES), lambda i, j: (j, 0)),
                  pl.BlockSpec((d, IN_COLS), lambda i, j: (0, 0))],
        out_specs=[row(w) for w in widths] + [pl.BlockSpec((1, 512, tm), lambda i, j: (i, 0, j))],
        out_shape=([jax.ShapeDtypeStruct((b, sall, w), BF16) for w in widths]
                   + [jax.ShapeDtypeStruct((b, 512, sall), BF16)]),
        scratch_shapes=[pltpu.VMEM((tm, d), BF16)],
        compiler_params=_params(("arbitrary", "arbitrary")),
        name="projection",
    )(xc, g, modall, cos, sin, w_in)


def _half_mask(q, half):
    lane = lax.broadcasted_iota(I32, q.shape, 1)
    keep = (lane < HEAD_DIM) if half == 0 else (lane >= HEAD_DIM)
    return jnp.where(keep, q, jnp.zeros_like(q))


def _merge_halves(lo_part, hi_part):
    lane = lax.broadcasted_iota(I32, lo_part.shape, 1)
    return jnp.where(lane < HEAD_DIM, lo_part, hi_part)


def _gqa_rows(q_ref, g):
    return jnp.concatenate([q_ref[0, :, (2 * g) * LANES:(2 * g + 1) * LANES],
                            q_ref[0, :, (2 * g + 1) * LANES:(2 * g + 2) * LANES]], axis=0)


def _gqa_sink(sink_ref, g, tq):
    rowi = lax.broadcasted_iota(I32, (2 * tq, 1), 0)
    return jnp.where(rowi < tq, sink_ref[2 * g], sink_ref[2 * g + 1])


def _gqa_store(o_ref, g, o, tq):
    top, bot = o[:tq], o[tq:]
    if g == 0:
        chunk = _merge_halves(top, pltpu.roll(bot, HEAD_DIM, 1))
    else:
        chunk = _merge_halves(pltpu.roll(top, HEAD_DIM, 1), bot)
    o_ref[0, :, g * LANES:(g + 1) * LANES] = chunk.astype(BF16)


def _lam(lamv_ref, lam_init):
    v = lamv_ref[...]
    a = jnp.sum(v[0:1] * v[1:2], axis=-1, keepdims=True)
    b = jnp.sum(v[2:3] * v[3:4], axis=-1, keepdims=True)
    return jnp.exp(a) - jnp.exp(b) + lam_init


def _diff_post(o, subg, lam_init):
    return _rms(o, subg) * (1.0 - lam_init)


def _swa_kernel(sink_ref, q_ref, k_ref, v_ref, o_ref, *, s_lat, c_len):
    tq = ATT_TQ
    tk = tq + 2 * SWA_WINDOW
    q0 = pl.program_id(1) * tq
    ks = pl.multiple_of(jnp.clip(q0 - SWA_WINDOW, 0, s_lat - tk), SWA_WINDOW)
    kwin = k_ref[0, pl.ds(ks, tk), :]
    vwin = v_ref[0, pl.ds(ks, tk), :]
    kc = k_ref[0, s_lat:s_lat + c_len, :]
    vc = v_ref[0, s_lat:s_lat + c_len, :]
    qpos = q0 + lax.broadcasted_iota(I32, (tq, tk), 0)
    kpos = ks + lax.broadcasted_iota(I32, (tq, tk), 1)
    valid = jnp.abs(qpos - kpos) <= SWA_WINDOW
    valid = jnp.concatenate([valid, valid], axis=0)
    for g in range(2):
        q2 = _gqa_rows(q_ref, g)
        s_loc = jnp.where(valid, _dot_nt(q2, kwin), NEG_INF)
        s_ctx = _dot_nt(q2, kc)
        snk = _gqa_sink(sink_ref, g, tq)
        m = jnp.maximum(jnp.maximum(jnp.max(s_loc, axis=-1, keepdims=True),
                                    jnp.max(s_ctx, axis=-1, keepdims=True)), snk)
        e_loc = jnp.exp(s_loc - m)
        e_ctx = jnp.exp(s_ctx - m)
        den = (jnp.sum(e_loc, axis=-1, keepdims=True) + jnp.sum(e_ctx, axis=-1, keepdims=True)
               + jnp.exp(snk - m))
        o = (_dot(e_loc.astype(BF16), vwin) + _dot(e_ctx.astype(BF16), vc)) / den
        _gqa_store(o_ref, g, o, tq)


def _swa(sink, qa, ka, va, s_lat):
    b, sall, _ = qa.shape
    kv = pl.BlockSpec((1, sall, LANES), lambda i, j: (i, 0, 0))
    return pl.pallas_call(
        functools.partial(_swa_kernel, s_lat=s_lat, c_len=sall - s_lat),
        grid=(b, s_lat // ATT_TQ),
        in_specs=[pl.BlockSpec(memory_space=pltpu.SMEM),
                  pl.BlockSpec((1, ATT_TQ, 512), lambda i, j: (i, j, 0)), kv, kv],
        out_specs=pl.BlockSpec((1, ATT_TQ, 256), lambda i, j: (i, j, 0)),
        out_shape=jax.ShapeDtypeStruct((b, s_lat, 256), BF16),
        compiler_params=_params(("arbitrary", "arbitrary")),
        name="swa",
    )(sink, qa, ka, va)


def _na_kernel(q_ref, k_ref, v_ref, bm_ref, o_ref, *, s_lat, c_len):
    rows = s_lat // GRID_W
    nk = NA_KROWS * GRID_W
    r0 = pl.program_id(1) * NA_QROWS
    ks = pl.multiple_of(jnp.clip(r0 - NA_KH // 2, 0, rows - NA_KROWS) * GRID_W, GRID_W)
    for c in range(2):
        sl = slice(c * LANES, (c + 1) * LANES)
        kwin = k_ref[0, pl.ds(ks, nk), sl]
        vwin = v_ref[0, pl.ds(ks, nk), sl]
        kc = k_ref[0, s_lat:s_lat + c_len, sl]
        vc = v_ref[0, s_lat:s_lat + c_len, sl]
        q = q_ref[0, :, sl]
        outs = []
        for half in range(2):
            qm = _half_mask(q, half)
            s_loc = _dot_nt(qm, kwin) + bm_ref[0, 2 * c + half]
            s_ctx = _dot_nt(qm, kc)
            m = jnp.maximum(jnp.max(s_loc, axis=-1, keepdims=True), jnp.max(s_ctx, axis=-1, keepdims=True))
            e_loc = jnp.exp(s_loc - m)
            e_ctx = jnp.exp(s_ctx - m)
            den = jnp.sum(e_loc, axis=-1, keepdims=True) + jnp.sum(e_ctx, axis=-1, keepdims=True)
            outs.append((_dot(e_loc.astype(BF16), vwin) + _dot(e_ctx.astype(BF16), vc)) / den)
        o_ref[0, :, sl] = _merge_halves(outs[0], outs[1]).astype(BF16)


def _na_bias_mask(rpb, rows):
    nq = NA_QROWS * GRID_W
    nk = NA_KROWS * GRID_W
    col = np.arange(GRID_W)
    cstart = np.clip(col - NA_KW // 2, 0, GRID_W - NA_KW)
    col_ok = (col[None, :] >= cstart[:, None]) & (col[None, :] < cstart[:, None] + NA_KW)
    dc = np.clip(col[None, :] - col[:, None], -(NA_KW - 1), NA_KW - 1) + NA_KW - 1
    sel_c = np.where(col_ok[:, :, None], np.eye(2 * NA_KW - 1)[dc], 0.0)
    tabs = []
    for r0 in (0, NA_QROWS, rows - NA_QROWS):
        ksr = int(np.clip(r0 - NA_KH // 2, 0, rows - NA_KROWS))
        r = r0 + np.arange(NA_QROWS)
        kr = ksr + np.arange(NA_KROWS)
        rs = np.clip(r - NA_KH // 2, 0, rows - NA_KH)
        row_ok = (kr[None, :] >= rs[:, None]) & (kr[None, :] < rs[:, None] + NA_KH)
        dr = np.clip(kr[None, :] - r[:, None], -(NA_KH - 1), NA_KH - 1) + NA_KH - 1
        sel_r = np.where(row_ok[:, :, None], np.eye(2 * NA_KH - 1)[dr], 0.0)
        ok = (row_ok[:, None, :, None] & col_ok[None, :, None, :]).reshape(nq, nk)
        bias = jnp.einsum("hdc,rjd,qkc->hrqjk", rpb.astype(F32), jnp.asarray(sel_r, F32), jnp.asarray(sel_c, F32),
                          precision=lax.Precision.HIGHEST).reshape(N_HEADS, nq, nk)
        tabs.append(jnp.where(ok[None], bias, NEG_INF))
    return jnp.stack(tabs)


def _na(qb, kb, vb, bm, s_lat):
    b, sall, _ = qb.shape
    nq = NA_QROWS * GRID_W
    nsteps = s_lat // nq
    kv = pl.BlockSpec((1, sall, 256), lambda i, j: (i, 0, 0))

    def bm_map(i, j):
        return (jnp.where(j == 0, 0, jnp.where(j == nsteps - 1, 2, 1)), 0, 0, 0)

    return pl.pallas_call(
        functools.partial(_na_kernel, s_lat=s_lat, c_len=sall - s_lat),
        grid=(b, nsteps),
        in_specs=[pl.BlockSpec((1, nq, 256), lambda i, j: (i, j, 0)), kv, kv,
                  pl.BlockSpec((1, N_HEADS, nq, NA_KROWS * GRID_W), bm_map)],
        out_specs=pl.BlockSpec((1, nq, 256), lambda i, j: (i, j, 0)),
        out_shape=jax.ShapeDtypeStruct((b, s_lat, 256), BF16),
        compiler_params=_params(("arbitrary", "arbitrary")),
        name="na2d",
    )(qb, kb, vb, bm)


def _diff_kernel(lamv_ref, subg_ref, q_ref, k_ref, vt_ref, o_ref, e_scr, m_scr, *, n_keys, lam_init):
    tq = ATT_TQ
    chunks = [(c0, min(DIFF_KCHUNK, n_keys - c0)) for c0 in range(0, n_keys, DIFF_KCHUNK)]
    lam = _lam(lamv_ref, lam_init)
    stats = []
    for hh in range(DIFF_HEADS_PER_STEP):
        sl = slice(hh * LANES, (hh + 1) * LANES)
        q = q_ref[0, :, sl]
        q12 = jnp.concatenate([_half_mask(q, 0), _half_mask(q, 1)], axis=0)
        m = jnp.full((1, 2 * tq), NEG_INF, F32)
        den = jnp.zeros((1, 2 * tq), F32)
        for i, (c0, n) in enumerate(chunks):
            s = _dot_nt(k_ref[0, c0:c0 + n, sl], q12)
            m_new = jnp.maximum(m, jnp.max(s, axis=0, keepdims=True))
            e = jnp.exp2(s - m_new)
            e_scr[hh, c0:c0 + n, :] = e
            m_scr[hh, i:i + 1, :] = m_new
            den = den * jnp.exp2(m - m_new) + jnp.sum(e, axis=0, keepdims=True)
            m = m_new
        stats.append((m, den))
    for hh in range(DIFF_HEADS_PER_STEP):
        sl = slice(hh * LANES, (hh + 1) * LANES)
        m, den = stats[hh]
        acc = jnp.zeros((LANES, tq), F32)
        for i, (c0, n) in enumerate(chunks):
            f = jnp.exp2(m_scr[hh, i:i + 1, :] - m) / den
            p = e_scr[hh, c0:c0 + n, :tq] * f[:, :tq] - e_scr[hh, c0:c0 + n, tq:] * (lam * f[:, tq:])
            acc = acc + _dot(vt_ref[0, sl, c0:c0 + n], p.astype(BF16))
        o_ref[0, :, sl] = _diff_post(acc.T, subg_ref[...], lam_init).astype(BF16)


def _diff(lamv, subg, qd, kd, vdt, s_lat, lam_init):
    b, sall, _ = qd.shape
    nchunk = -(-sall // DIFF_KCHUNK)
    hps = DIFF_HEADS_PER_STEP
    wid = hps * LANES
    return pl.pallas_call(
        functools.partial(_diff_kernel, n_keys=sall, lam_init=lam_init),
        grid=(b, N_HEADS // hps, s_lat // ATT_TQ),
        in_specs=[pl.BlockSpec((8, LANES), lambda i, h, j: (0, 0)),
                  pl.BlockSpec((1, LANES), lambda i, h, j: (0, 0)),
                  pl.BlockSpec((1, ATT_TQ, wid), lambda i, h, j: (i, j, h)),
                  pl.BlockSpec((1, sall, wid), lambda i, h, j: (i, 0, h)),
                  pl.BlockSpec((1, wid, sall), lambda i, h, j: (i, h, 0))],
        out_specs=pl.BlockSpec((1, ATT_TQ, wid), lambda i, h, j: (i, j, h)),
        out_shape=jax.ShapeDtypeStruct((b, s_lat, 512), BF16),
        scratch_shapes=[pltpu.VMEM((hps, sall, 2 * ATT_TQ), F32),
                        pltpu.VMEM((hps, -(-nchunk // 8) * 8, 2 * ATT_TQ), F32)],
        compiler_params=_params(("arbitrary", "arbitrary", "arbitrary")),
        name="diff_attn",
    )(lamv, subg, qd, kd, vdt)


def _ctx_kernel(sink_ref, lamv_ref, subg_ref, qa_ref, ka_ref, va_ref, qb_ref, kb_ref, vb_ref,
                qd_ref, kd_ref, vd_ref, oa_ref, ob_ref, od_ref, *, lam_init):
    cl = qa_ref.shape[1]
    ka = ka_ref[0]
    va = va_ref[0]
    for g in range(2):
        q2 = _gqa_rows(qa_ref, g)
        s = _dot_nt(q2, ka)
        snk = _gqa_sink(sink_ref, g, cl)
        m = jnp.maximum(jnp.max(s, axis=-1, keepdims=True), snk)
        e = jnp.exp(s - m)
        den = jnp.sum(e, axis=-1, keepdims=True) + jnp.exp(snk - m)
        _gqa_store(oa_ref, g, _dot(e.astype(BF16), va) / den, cl)
    for c in range(2):
        sl = slice(c * LANES, (c + 1) * LANES)
        outs = []
        for half in range(2):
            s = _dot_nt(_half_mask(qb_ref[0, :, sl], half), kb_ref[0, :, sl])
            e = jnp.exp(s - jnp.max(s, axis=-1, keepdims=True))
            outs.append(_dot(e.astype(BF16), vb_ref[0, :, sl]) / jnp.sum(e, axis=-1, keepdims=True))
        ob_ref[0, :, sl] = _merge_halves(outs[0], outs[1]).astype(BF16)
    lam = _lam(lamv_ref, lam_init)
    for h in range(N_HEADS):
        sl = slice(h * LANES, (h + 1) * LANES)
        q = qd_ref[0, :, sl]
        q12 = jnp.concatenate([_half_mask(q, 0), _half_mask(q, 1)], axis=0)
        s = _dot_nt(q12, kd_ref[0, :, sl])
        e = jnp.exp2(s - jnp.max(s, axis=-1, keepdims=True))
        den = jnp.sum(e, axis=-1, keepdims=True)
        p = e[:cl] * (1.0 / den[:cl]) - e[cl:] * (lam / den[cl:])
        o = _dot(p.astype(BF16), vd_ref[0, :, sl])
        od_ref[0, :, sl] = _diff_post(o, subg_ref[...], lam_init).astype(BF16)


def _ctx_attention(sink, lamv, subg, qa, ka, va, qb, kb, vb, qd, kd, vd, s_lat, lam_init):
    b, sall, _ = qa.shape
    cl = sall - s_lat
    blk = s_lat // cl
    row = lambda width: pl.BlockSpec((1, cl, width), lambda i: (i, blk, 0))
    out = lambda width: pl.BlockSpec((1, cl, width), lambda i: (i, 0, 0))
    return pl.pallas_call(
        functools.partial(_ctx_kernel, lam_init=lam_init),
        grid=(b,),
        in_specs=[pl.BlockSpec(memory_space=pltpu.SMEM),
                  pl.BlockSpec((8, LANES), lambda i: (0, 0)),
                  pl.BlockSpec((1, LANES), lambda i: (0, 0)),
                  row(512), row(128), row(128), row(256), row(256), row(256), row(512), row(512), row(512)],
        out_specs=[out(256), out(256), out(512)],
        out_shape=[jax.ShapeDtypeStruct((b, cl, w), BF16) for w in (256, 256, 512)],
        compiler_params=_params(("arbitrary",)),
        name="ctx_attention",
    )(sink, lamv, subg, qa, ka, va, qb, kb, vb, qd, kd, vd)


def _out_kernel(x_ref, w_ref, mod_ref, *refs, nlat):
    xo_ref = refs[-1]

    def emit(oa_ref, ob_ref, od_ref):
        attn = (_dot(oa_ref[0], w_ref[0:256, :]) + _dot(ob_ref[0], w_ref[256:512, :])
                + _dot(od_ref[0], w_ref[512:1024, :]))
        xo_ref[0] = x_ref[0] + mod_ref[0, 2:3, :] * attn

    if len(refs) == 4:
        emit(*refs[:3])
    else:
        is_ctx = pl.program_id(1) >= nlat
        pl.when(jnp.logical_not(is_ctx))(lambda: emit(*refs[:3]))
        pl.when(is_ctx)(lambda: emit(*refs[3:6]))


def _out_projection(xc, lat, ctx, w_out, modall, s_lat):
    b, sall, d = xc.shape
    tm = TOK_TILE
    nlat = s_lat // tm
    n_rows = s_lat if ctx is None else sall
    row = lambda width: pl.BlockSpec((1, tm, width), lambda i, j: (i, j, 0))
    lat_row = lambda width: pl.BlockSpec((1, tm, width), lambda i, j: (i, jnp.minimum(j, nlat - 1), 0))
    ctx_row = lambda width: pl.BlockSpec((1, tm, width), lambda i, j: (i, 0, 0))
    widths = (256, 256, 512)
    specs = [lat_row(w) for w in widths] + ([] if ctx is None else [ctx_row(w) for w in widths])
    return pl.pallas_call(
        functools.partial(_out_kernel, nlat=nlat),
        grid=(b, n_rows // tm),
        in_specs=[row(d),
                  pl.BlockSpec((MIX_WIDTH, d), lambda i, j: (0, 0)),
                  pl.BlockSpec((1, 8, d), lambda i, j: (2 * i + (j >= nlat).astype(I32), 0, 0))] + specs,
        out_specs=row(d),
        out_shape=jax.ShapeDtypeStruct((b, n_rows, d), F32),
        compiler_params=_params(("arbitrary", "arbitrary")),
        name="out_projection",
    )(xc, w_out, modall, *lat, *(() if ctx is None else ctx))


def _first_argmax(v, iota, n):
    m = jnp.max(v, axis=0, keepdims=True)
    ix = jnp.min(jnp.where(v == m, iota, float(n)), axis=0, keepdims=True)
    return m, ix


def _router_kernel(x_ref, g_ref, mod_ref, whi_ref, wlo_ref, bias_ref, tri_ref, ltri_ref,
                   w_ref, slot_ref, meta_ref, cnt_ref):
    @pl.when(pl.program_id(0) == 0)
    def _():
        cnt_ref[...] = jnp.zeros_like(cnt_ref)

    hf = _norm_mod(x_ref[...], g_ref[...], mod_ref[0, 3:4, :], mod_ref[0, 4:5, :])
    tm = hf.shape[0]
    h_hi, h_lo = _split_bf16(hf)
    whi = whi_ref[...]
    logits = _dot_nt(whi, h_hi) + _dot_nt(whi, h_lo) + _dot_nt(wlo_ref[...], h_hi)
    scores = _sigmoid(logits)
    biased = scores + bias_ref[...]
    gsz = N_EXPERTS // N_GROUPS
    iota_g = lax.broadcasted_iota(I32, (gsz, tm), 0).astype(F32)
    gscore = []
    for g in range(N_GROUPS):
        v = biased[g * gsz:(g + 1) * gsz]
        m1, i1 = _first_argmax(v, iota_g, gsz)
        m2 = jnp.max(jnp.where(iota_g == i1, -jnp.inf, v), axis=0, keepdims=True)
        gscore.append(m1 + m2)
    cur = jnp.concatenate(gscore, axis=0)
    iota_n = lax.broadcasted_iota(I32, (N_GROUPS, tm), 0).astype(F32)
    gsel = jnp.zeros((N_GROUPS, tm), F32)
    for _ in range(TOPK_GROUPS):
        _, ix = _first_argmax(cur, iota_n, N_GROUPS)
        hit = iota_n == ix
        gsel = jnp.where(hit, 1.0, gsel)
        cur = jnp.where(hit, -jnp.inf, cur)
    masked = jnp.concatenate(
        [jnp.where(gsel[g:g + 1] > 0.5, biased[g * gsz:(g + 1) * gsz], -jnp.inf) for g in range(N_GROUPS)],
        axis=0)
    iota_e = lax.broadcasted_iota(I32, (N_EXPERTS, tm), 0).astype(F32)
    onehot = jnp.zeros((N_EXPERTS, tm), F32)
    idxs, ws = [], []
    for _ in range(TOP_K):
        _, ix = _first_argmax(masked, iota_e, N_EXPERTS)
        hit = iota_e == ix
        ws.append(jnp.sum(jnp.where(hit, scores, 0.0), axis=0, keepdims=True))
        masked = jnp.where(hit, -jnp.inf, masked)
        onehot = jnp.where(hit, 1.0, onehot)
        idxs.append(ix)
    base = cnt_ref[...]
    rank = _dot(onehot.astype(BF16), tri_ref[...])
    n_col = jnp.sum(onehot, axis=1, keepdims=True)
    n_pad = jnp.floor((n_col + (ROW_CHUNK - 1)) * (1.0 / ROW_CHUNK)) * ROW_CHUNK
    start = _dot(ltri_ref[...], jnp.broadcast_to(n_pad, (N_EXPERTS, LANES)).astype(BF16))
    wsum = ws[0]
    for wk in ws[1:]:
        wsum = wsum + wk
    for k in range(TOP_K):
        slot = jnp.sum(jnp.where(iota_e == idxs[k], rank + start[:, 0:1], 0.0), axis=0, keepdims=True)
        slot_ref[0, k:k + 1, :] = slot.astype(I32)
        w_ref[0, k:k + 1, :] = ws[k] / wsum * ROUTED_SCALE
    lane = lax.broadcasted_iota(I32, (N_EXPERTS, LANES), 1)
    meta_ref[0] = jnp.where(lane == 0, n_pad, jnp.where(lane == 1, base, start))
    cnt_ref[...] = base + n_pad


def _router(xflat, g, modall, whi, wlo, bias, tri, ltri, tiles_per_batch, nlat):
    t, d = xflat.shape
    tm = TOK_TILE
    nt = t // tm
    per_tok = pl.BlockSpec((1, TOP_K, tm), lambda i: (i, 0, 0))

    def mod_map(i):
        return (2 * (i // tiles_per_batch) + ((i % tiles_per_batch) >= nlat).astype(I32), 0, 0)

    return pl.pallas_call(
        _router_kernel,
        grid=(nt,),
        in_specs=[pl.BlockSpec((tm, d), lambda i: (i, 0)),
                  pl.BlockSpec((1, d), lambda i: (0, 0)),
                  pl.BlockSpec((1, 8, d), mod_map),
                  pl.BlockSpec((N_EXPERTS, d), lambda i: (0, 0)),
                  pl.BlockSpec((N_EXPERTS, d), lambda i: (0, 0)),
                  pl.BlockSpec((N_EXPERTS, 1), lambda i: (0, 0)),
                  pl.BlockSpec((tm, tm), lambda i: (0, 0)),
                  pl.BlockSpec((N_EXPERTS, N_EXPERTS), lambda i: (0, 0))],
        out_specs=[per_tok, per_tok,
                   pl.BlockSpec((1, N_EXPERTS, LANES), lambda i: (i, 0, 0)),
                   pl.BlockSpec((N_EXPERTS, 1), lambda i: (0, 0))],
        out_shape=[jax.ShapeDtypeStruct((nt, TOP_K, tm), F32),
                   jax.ShapeDtypeStruct((nt, TOP_K, tm), I32),
                   jax.ShapeDtypeStruct((nt, N_EXPERTS, LANES), F32),
                   jax.ShapeDtypeStruct((N_EXPERTS, 1), F32)],
        compiler_params=_params(("arbitrary",)),
        name="router",
    )(xflat, g, modall, whi, wlo, bias, tri, ltri)


def _pack_bf16_pairs(v, is_bf16_valued=False):
    half = v.shape[1] // 2
    if not is_bf16_valued:
        v = v.astype(BF16).astype(F32)
    lo = lax.shift_right_logical(lax.bitcast_convert_type(v[:, :half], U32), jnp.uint32(16))
    hi = lax.bitcast_convert_type(v[:, half:], U32) & jnp.uint32(0xFFFF0000)
    return lo | hi


def _unpack_bf16_pairs(u):
    lo = lax.bitcast_convert_type(lax.shift_left(u, jnp.uint32(16)), F32).astype(BF16)
    hi = lax.bitcast_convert_type(u & jnp.uint32(0xFFFF0000), F32).astype(BF16)
    return lo, hi


def _used_chunks(meta_ref):
    return lax.shift_right_logical(_slots_used(meta_ref), ROW_CHUNK.bit_length() - 1)


def _start_chunk_copies(first, count, make_copy):
    pairs = lax.shift_right_logical(count, 1)

    def start_pair(j, carry):
        make_copy(first + 2 * j).start()
        make_copy(first + 2 * j + 1).start()
        return carry

    lax.fori_loop(0, pairs, start_pair, 0)

    @pl.when(count > 2 * pairs)
    def _():
        make_copy(first + count - 1).start()


def _wait_chunk_copies(total, copy_of_rows):
    group = lax.shift_right_logical(total, WAIT_GROUP.bit_length() - 1)

    def wait_group(j, carry):
        copy_of_rows(WAIT_GROUP * ROW_CHUNK).wait()
        return carry

    def wait_one(j, carry):
        copy_of_rows(ROW_CHUNK).wait()
        return carry

    lax.fori_loop(0, group, wait_group, 0)
    lax.fori_loop(0, total - group * WAIT_GROUP, wait_one, 0)


def _tile_slots(tm):
    return TOP_K * tm + ROW_CHUNK * N_EXPERTS


def _slots_used(meta_ref):
    last = N_EXPERTS - 1
    return meta_ref[0, 2, last] + meta_ref[0, 0, last]


def _dispatch_kernel(meta_ref, rows_ref, x_ref, g_ref, mod_ref, slot_ref, xs_ref, xc_scr, sem):
    hb = _norm_mod(x_ref[...], g_ref[...], mod_ref[0, 3:4, :], mod_ref[0, 4:5, :]).astype(BF16)
    tm = hb.shape[0]
    slot = slot_ref[0]

    total = _used_chunks(meta_ref)
    step = CHUNK_UNROLL * tm
    chunks_per_step = step // ROW_CHUNK

    def make_copy(j):
        src = pl.multiple_of(j * ROW_CHUNK, ROW_CHUNK)
        dst = pl.multiple_of(rows_ref[0, 0, j], ROW_CHUNK)
        return pltpu.make_async_copy(xc_scr.at[pl.ds(src, ROW_CHUNK)], xs_ref.at[pl.ds(dst, ROW_CHUNK)], sem)

    def select_rows(r2, carry):
        for u in range(CHUNK_UNROLL):
            r0 = pl.multiple_of((r2 * CHUNK_UNROLL + u) * tm, tm)
            row = lax.broadcasted_iota(I32, (tm, tm), 0) + r0
            sel = jnp.zeros((tm, tm), F32)
            for k in range(TOP_K):
                sel = jnp.where(row == slot[k:k + 1, :], 1.0, sel)
            xc_scr[pl.ds(r0, tm), :] = _pack_bf16_pairs(_dot(sel.astype(BF16), hb), is_bf16_valued=True)
        first = r2 * chunks_per_step
        _start_chunk_copies(first, jnp.clip(total - first, 0, chunks_per_step), make_copy)
        return carry

    lax.fori_loop(0, (_slots_used(meta_ref) + step - 1) // step, select_rows, 0)
    _wait_chunk_copies(total, lambda n: pltpu.make_async_copy(xc_scr.at[pl.ds(0, n)], xs_ref.at[pl.ds(0, n)], sem))


def _dispatch(meta, chunk_rows, slot, xflat, g, modall, n_rows, tiles_per_batch, nlat):
    t, d = xflat.shape
    tm = TOK_TILE
    nt = t // tm

    def mod_map(i):
        return (2 * (i // tiles_per_batch) + ((i % tiles_per_batch) >= nlat).astype(I32), 0, 0)

    return pl.pallas_call(
        _dispatch_kernel,
        grid=(nt,),
        in_specs=[pl.BlockSpec((1, 3, N_EXPERTS), lambda i: (i, 0, 0), memory_space=pltpu.SMEM),
                  pl.BlockSpec((1, 1, chunk_rows.shape[2]), lambda i: (i, 0, 0), memory_space=pltpu.SMEM),
                  pl.BlockSpec((tm, d), lambda i: (i, 0)),
                  pl.BlockSpec((1, d), lambda i: (0, 0)),
                  pl.BlockSpec((1, 8, d), mod_map),
                  pl.BlockSpec((1, TOP_K, tm), lambda i: (i, 0, 0))],
        out_specs=pl.BlockSpec(memory_space=pl.ANY),
        out_shape=jax.ShapeDtypeStruct((n_rows, d // 2), U32),
        scratch_shapes=[pltpu.VMEM((_tile_slots(tm), d // 2), U32), pltpu.SemaphoreType.DMA(())],
        compiler_params=_params(("arbitrary",)),
        name="moe_dispatch",
    )(meta, chunk_rows, xflat, g, modall, slot)


def _expert_kernel(blk_e_ref, nvalid_ref, nact_ref, next_e_ref, has_next_ref, parity_ref,
                   xs_ref, wg_any, wu_any, wd_any, ys_ref, wg_s, wu_s, wd_s, wg_buf, wu_buf, wd_buf, sem, *, layer):
    b = pl.program_id(0)
    e = blk_e_ref[b]
    changed = jnp.logical_or(b == 0, e != blk_e_ref[jnp.maximum(b - 1, 0)])
    par = parity_ref[b]

    def weight_copies(expert, p):
        return [pltpu.make_async_copy(src.at[layer, expert], dst.at[p], sem.at[p])
                for src, dst in ((wg_any, wg_buf), (wu_any, wu_buf), (wd_any, wd_buf))]

    @pl.when(b == 0)
    def _():
        for c in weight_copies(e, par):
            c.start()

    @pl.when(changed)
    def _():
        for c in weight_copies(e, par):
            c.wait()
        wg_s[...] = wg_buf[par].astype(BF16)
        wu_s[...] = wu_buf[par].astype(BF16)
        wd_s[...] = wd_buf[par].astype(BF16)

        @pl.when(has_next_ref[b] == 1)
        def _():
            for c in weight_copies(next_e_ref[b], 1 - par):
                c.start()

    half = xs_ref.shape[1]
    n_here = jnp.where(b < nact_ref[0], nvalid_ref[b], 0)

    @pl.when(n_here > 0)
    def _():
        row = lax.broadcasted_iota(I32, xs_ref.shape, 0)
        xs = jnp.where(row < n_here, xs_ref[...], jnp.zeros(xs_ref.shape, U32))
        x_lo, x_hi = _unpack_bf16_pairs(xs)
        gate = _dot(x_lo, wg_s[0:half, :]) + _dot(x_hi, wg_s[half:, :])
        up = _dot(x_lo, wu_s[0:half, :]) + _dot(x_hi, wu_s[half:, :])
        mid = (gate * _sigmoid(gate)) * up
        ys_ref[...] = _pack_bf16_pairs(_dot(mid.astype(BF16), wd_s[...]))

    @pl.when(n_here == 0)
    def _():
        ys_ref[...] = jnp.zeros_like(ys_ref)


def _experts(blk_e, nvalid, nact, xs, w_gate, w_up, w_down, layer):
    n_rows, half = xs.shape
    _, _, d, f = w_gate.shape
    n_blk = n_rows // MOE_BLOCK
    first_of_next = jnp.sum((blk_e[None, :] <= blk_e[:, None]).astype(I32), axis=1)
    has_next = (first_of_next < n_blk).astype(I32)
    next_e = blk_e[jnp.minimum(first_of_next, n_blk - 1)]
    starts = jnp.concatenate([jnp.zeros((1,), I32), (blk_e[1:] != blk_e[:-1]).astype(I32)])
    parity = jnp.cumsum(starts) % 2
    anyspec = pl.BlockSpec(memory_space=pl.ANY)
    grid_spec = pltpu.PrefetchScalarGridSpec(
        num_scalar_prefetch=6,
        grid=(n_blk,),
        in_specs=[pl.BlockSpec((MOE_BLOCK, half), lambda i, be, nv, na, *_: (jnp.minimum(i, na[0] - 1), 0)),
                  anyspec, anyspec, anyspec],
        out_specs=pl.BlockSpec((MOE_BLOCK, half), lambda i, be, nv, na, *_: (jnp.minimum(i, na[0]), 0)),
        scratch_shapes=[pltpu.VMEM((d, f), BF16), pltpu.VMEM((d, f), BF16), pltpu.VMEM((f, d), BF16),
                        pltpu.VMEM((2, d, f), F32), pltpu.VMEM((2, d, f), F32), pltpu.VMEM((2, f, d), F32),
                        pltpu.SemaphoreType.DMA((2,))],
    )
    return pl.pallas_call(
        functools.partial(_expert_kernel, layer=layer),
        grid_spec=grid_spec,
        out_shape=jax.ShapeDtypeStruct((n_rows, half), U32),
        compiler_params=_params(("arbitrary",)),
        name="moe_experts",
    )(blk_e, nvalid, nact, next_e, has_next, parity.astype(I32), xs, w_gate, w_up, w_down)


def _combine_kernel(meta_ref, rows_ref, x_ref, g_ref, mod_ref, spos_ref, wt_ref, wsg_ref, wsu_ref, wsd_ref,
                    gfin_ref, ys_ref, o_ref, y_scr, acc_scr, sb_scr, wb_scr, sem, *, final_norm):
    tm, d = x_ref.shape
    half = d // 2

    @pl.when(pl.program_id(0) == 0)
    def _():
        y_scr[...] = jnp.zeros_like(y_scr)

    step = CHUNK_UNROLL * COMBINE_KCHUNK
    chunks_per_step = step // ROW_CHUNK
    group_shift = chunks_per_step.bit_length() - 1

    def make_copy(j):
        src = pl.multiple_of(rows_ref[0, 0, j], ROW_CHUNK)
        dst = pl.multiple_of(j * ROW_CHUNK, ROW_CHUNK)
        return pltpu.make_async_copy(ys_ref.at[pl.ds(src, ROW_CHUNK)], y_scr.at[pl.ds(dst, ROW_CHUNK)],
                                     sem.at[lax.shift_right_logical(j, group_shift)])

    total = _used_chunks(meta_ref)
    _start_chunk_copies(0, total, make_copy)

    x = x_ref[...]
    hb = _norm_mod(x, g_ref[...], mod_ref[0, 3:4, :], mod_ref[0, 4:5, :]).astype(BF16)
    gate = _dot(hb, wsg_ref[...])
    mid = (gate * _sigmoid(gate)) * _dot(hb, wsu_ref[...])
    acc_scr[...] = _dot(mid.astype(BF16), wsd_ref[...])

    n_used = _slots_used(meta_ref)
    for k in range(TOP_K):
        sb_scr[k] = jnp.broadcast_to(spos_ref[0, :, k:k + 1], (tm, LANES))
        wb_scr[k] = jnp.broadcast_to(wt_ref[0, :, k:k + 1], (tm, LANES))

    def chunk(j2, carry):
        _wait_chunk_copies(jnp.clip(total - j2 * chunks_per_step, 0, chunks_per_step),
                           lambda n: pltpu.make_async_copy(ys_ref.at[pl.ds(0, n)], y_scr.at[pl.ds(0, n)], sem.at[j2]))
        lo_sum = jnp.zeros((tm, half), F32)
        hi_sum = jnp.zeros((tm, half), F32)
        for u in range(CHUNK_UNROLL):
            r0 = pl.multiple_of((j2 * CHUNK_UNROLL + u) * COMBINE_KCHUNK, COMBINE_KCHUNK)
            parts = []
            for c in range(COMBINE_KCHUNK // LANES):
                col = lax.broadcasted_iota(I32, (tm, LANES), 1) + (r0 + c * LANES)
                p = jnp.zeros((tm, LANES), F32)
                for k in range(TOP_K):
                    p = jnp.where(col == sb_scr[k], wb_scr[k], p)
                parts.append(p.astype(BF16))
            pb = jnp.concatenate(parts, axis=1)
            y_lo, y_hi = _unpack_bf16_pairs(y_scr[pl.ds(r0, COMBINE_KCHUNK), :])
            lo_sum = lo_sum + _dot(pb, y_lo)
            hi_sum = hi_sum + _dot(pb, y_hi)
        acc_scr[:, :half] += lo_sum
        acc_scr[:, half:] += hi_sum
        return carry

    lax.fori_loop(0, (n_used + step - 1) // step, chunk, 0)
    out = x + mod_ref[0, 5:6, :] * acc_scr[...]
    if final_norm:
        out = _rms(out, gfin_ref[...])
    o_ref[...] = out


def _combine(meta, chunk_rows, spos_t, wt, xflat, g, modall, wsg, wsu, wsd, gfin, ys, tiles_per_batch, nlat,
             final_norm):
    t, d = xflat.shape
    f = wsg.shape[1]
    tm = TOK_TILE
    nt = t // tm
    y_rows = _tile_slots(tm)
    assert y_rows % COMBINE_KCHUNK == 0

    def mod_map(i):
        return (2 * (i // tiles_per_batch) + ((i % tiles_per_batch) >= nlat).astype(I32), 0, 0)

    return pl.pallas_call(
        functools.partial(_combine_kernel, final_norm=final_norm),
        grid=(nt,),
        in_specs=[pl.BlockSpec((1, 3, N_EXPERTS), lambda i: (i, 0, 0), memory_space=pltpu.SMEM),
                  pl.BlockSpec((1, 1, chunk_rows.shape[2]), lambda i: (i, 0, 0), memory_space=pltpu.SMEM),
                  pl.BlockSpec((tm, d), lambda i: (i, 0)),
                  pl.BlockSpec((1, d), lambda i: (0, 0)),
                  pl.BlockSpec((1, 8, d), mod_map),
                  pl.BlockSpec((1, tm, TOP_K), lambda i: (i, 0, 0)),
                  pl.BlockSpec((1, tm, TOP_K), lambda i: (i, 0, 0)),
                  pl.BlockSpec((d, f), lambda i: (0, 0)),
                  pl.BlockSpec((d, f), lambda i: (0, 0)),
                  pl.BlockSpec((f, d), lambda i: (0, 0)),
                  pl.BlockSpec((1, d), lambda i: (0, 0)),
                  pl.BlockSpec(memory_space=pl.ANY)],
        out_specs=pl.BlockSpec((tm, d), lambda i: (i, 0)),
        out_shape=jax.ShapeDtypeStruct((t, d), F32),
        scratch_shapes=[pltpu.VMEM((y_rows, d // 2), U32), pltpu.VMEM((tm, d), F32),
                        pltpu.VMEM((TOP_K, tm, LANES), I32), pltpu.VMEM((TOP_K, tm, LANES), F32),
                        pltpu.SemaphoreType.DMA((y_rows // (CHUNK_UNROLL * COMBINE_KCHUNK),))],
        compiler_params=_params(("arbitrary",)),
        name="moe_combine",
    )(meta, chunk_rows, xflat, g, modall, spos_t, wt, wsg, wsu, wsd, gfin, ys)


def _rope_tables(s_lat, c_len):
    n_freq = HEAD_DIM // 4
    inv = 1.0 / (ROPE_BASE ** (jnp.arange(n_freq, dtype=F32) / n_freq))
    t = jnp.arange(s_lat)
    pos = jnp.stack([(t // GRID_W).astype(F32), (t % GRID_W).astype(F32)], axis=1)
    lane = np.arange(LANES)
    ang = pos[:, (lane % HEAD_DIM) // 32] * inv[lane % 16][None, :]
    sign = jnp.asarray(np.where(lane % 32 < 16, -1.0, 1.0), F32)
    cos = jnp.concatenate([jnp.cos(ang), jnp.ones((c_len, LANES), F32)], axis=0)
    sin = jnp.concatenate([jnp.sin(ang) * sign, jnp.zeros((c_len, LANES), F32)], axis=0)
    return cos, sin


def _moe(xflat, lp, modall, gfin, tiles_per_batch, nlat, final_norm):
    t, d = xflat.shape
    nt = t // TOK_TILE
    g = lp["g_ffn"].reshape(1, d)
    w_r = lp["w_router"].T
    whi = w_r.astype(BF16)
    wlo = (w_r - whi.astype(F32)).astype(BF16)
    tri = jnp.asarray(np.triu(np.ones((TOK_TILE, TOK_TILE), np.float32), 1), BF16)
    ltri = jnp.asarray(np.tril(np.ones((N_EXPERTS, N_EXPERTS), np.float32), -1), BF16)
    w, slot, meta, cnt = _router(xflat, g, modall, whi, wlo, lp["router_bias"].reshape(N_EXPERTS, 1),
                                 tri, ltri, tiles_per_batch, nlat)
    meta = jnp.transpose(meta[:, :, :3].astype(I32), (0, 2, 1))
    rows_e = cnt[:, 0].astype(I32)
    seg = (rows_e + MOE_BLOCK - 1) // MOE_BLOCK * MOE_BLOCK
    pend = jnp.cumsum(seg)
    pstart = (pend - seg).astype(I32)
    n_blk = (t * TOP_K + ROW_CHUNK * N_EXPERTS * nt) // MOE_BLOCK + N_EXPERTS
    blk_row0 = jnp.arange(n_blk, dtype=I32) * MOE_BLOCK
    blk_e = jnp.minimum(jnp.sum((pend[None, :] <= blk_row0[:, None]).astype(I32), axis=1), N_EXPERTS - 1)
    of_blk = blk_e[:, None] == jnp.arange(N_EXPERTS, dtype=I32)[None, :]
    seg_end = jnp.sum(jnp.where(of_blk, (pstart + rows_e)[None, :], 0), axis=1)
    nvalid = jnp.clip(seg_end - blk_row0, 0, MOE_BLOCK).astype(I32)
    nact = (pend[-1] // MOE_BLOCK).astype(I32).reshape(1)
    n_pad, base, start = meta[:, 0, :], meta[:, 1, :], meta[:, 2, :]
    chunk_slot = jnp.arange(_tile_slots(TOK_TILE) // ROW_CHUNK, dtype=I32) * ROW_CHUNK
    s = chunk_slot[None, :, None]
    in_run = (start[:, None, :] <= s) & (s < (start + n_pad)[:, None, :])
    row0 = pstart[None, :] + base - start
    chunk_rows = (jnp.sum(jnp.where(in_run, row0[:, None, :], 0), axis=2) + chunk_slot[None, :])[:, None, :]
    xs = _dispatch(meta, chunk_rows, slot, xflat, g, modall, n_blk * MOE_BLOCK, tiles_per_batch, nlat)
    ys = _experts(blk_e, nvalid, nact, xs, lp["w_gate"], lp["w_up"], lp["w_down"], lp["layer"])
    return _combine(meta, chunk_rows, jnp.transpose(slot, (0, 2, 1)), jnp.transpose(w, (0, 2, 1)), xflat, g, modall,
                    lp["ws_gate"].astype(BF16), lp["ws_up"].astype(BF16), lp["ws_down"].astype(BF16), gfin, ys,
                    tiles_per_batch, nlat, final_norm)


def _layer(xc, cs, cos, sin, lp, layer_idx, last, s_lat, gfin):
    b, sall, d = xc.shape
    c_len = sall - s_lat
    lam_init = 0.8 - 0.6 * math.exp(-0.3 * layer_idx)
    mod = _modulation(cs, lp["w_mod"], lp["b_mod"])
    mod_lat = mod[:b].reshape(b, 1, 6, d)
    mod_ctx = jnp.broadcast_to(mod[b].reshape(1, 1, 6, d), (b, 1, 6, d))
    modall = jnp.concatenate([mod_lat, mod_ctx], axis=1)
    modall = jnp.pad(modall, ((0, 0), (0, 0), (0, 2), (0, 0))).reshape(2 * b, 8, d)

    qa, qb, qd, ka, va, kb, vb, kd, vd, vdt = _projection(
        xc, lp["g_mix"].reshape(1, d), modall, cos, sin, lp["w_in"].astype(BF16), s_lat)
    sink = lp["attn_sink"].astype(F32)
    lamv = jnp.zeros((8, LANES), F32).at[0:4, 0:HEAD_DIM].set(
        jnp.stack([lp["lam_q1"], lp["lam_k1"], lp["lam_q2"], lp["lam_k2"]]).astype(F32))
    subg = lp["subln_g"].reshape(1, LANES).astype(F32)
    oa = _swa(sink, qa, ka, va, s_lat)
    ob = _na(qb, kb, vb, _na_bias_mask(lp["na_rpb"], s_lat // GRID_W), s_lat)
    od = _diff(lamv, subg, qd, kd, vdt, s_lat, lam_init)
    ctx_out = None
    if not last:
        ctx_out = _ctx_attention(sink, lamv, subg, qa, ka, va, qb, kb, vb, qd, kd, vd, s_lat, lam_init)
    n_rows = s_lat if last else sall
    x2 = _out_projection(xc, (oa, ob, od), ctx_out, lp["w_out"].astype(BF16), modall, s_lat)
    tiles_per_batch = n_rows // TOK_TILE
    y = _moe(x2.reshape(b * n_rows, d), lp, modall, gfin, tiles_per_batch, s_lat // TOK_TILE, last)
    return y.reshape(b, n_rows, d)


_LAYER_KEYS = ("w_mod", "b_mod", "g_mix", "g_ffn", "w_in", "w_out", "attn_sink", "na_rpb", "lam_q1", "lam_k1",
               "lam_q2", "lam_k2", "subln_g", "w_router", "router_bias", "w_gate", "w_up", "w_down",
               "ws_gate", "ws_up", "ws_down")


def kernel(x, c, ctx, c_ctx, w_mod, b_mod, g_mix, g_ffn, w_in, w_out, attn_sink, na_rpb, lam_q1, lam_k1, lam_q2,
           lam_k2, subln_g, w_router, router_bias, w_gate, w_up, w_down, ws_gate, ws_up, ws_down, g_final):
    stacked = dict(zip(_LAYER_KEYS, (w_mod, b_mod, g_mix, g_ffn, w_in, w_out, attn_sink, na_rpb, lam_q1, lam_k1,
                                     lam_q2, lam_k2, subln_g, w_router, router_bias, w_gate, w_up, w_down,
                                     ws_gate, ws_up, ws_down)))
    b, s_lat, d = x.shape
    c_len = ctx.shape[1]
    depth = w_mod.shape[0]
    assert s_lat % (NA_QROWS * GRID_W) == 0 and s_lat % c_len == 0 and c_len == TOK_TILE
    cs = jnp.zeros((16, d), F32).at[:b].set(c).at[b].set(c_ctx)
    cos, sin = _rope_tables(s_lat, c_len)
    xc = jnp.concatenate([x, ctx], axis=1)
    gfin = g_final.reshape(1, d)
    for i in range(depth):
        big = ("w_gate", "w_up", "w_down")
        lp = {k: (v if k in big else v[i]) for k, v in stacked.items()}
        lp["layer"] = i
        xc = _layer(xc, cs, cos, sin, lp, i, i == depth - 1, s_lat, gfin)
    return xc
```

```python
import functools
import math

import numpy as np
import jax
import jax.numpy as jnp
from jax import lax
from jax.experimental import pallas as pl
from jax.experimental.pallas import tpu as pltpu

F32 = jnp.float32
BF16 = jnp.bfloat16
I32 = jnp.int32
U32 = jnp.uint32

GRID_W = 64
HEAD_DIM = 64
ROPE_BASE = 10000.0
SWA_WINDOW = 128
NA_KH = 8
NA_KW = 16
N_HEADS = 4
N_EXPERTS = 256
TOP_K = 8
N_GROUPS = 8
TOPK_GROUPS = 4
ROUTED_SCALE = 2.5
MOE_BLOCK = 1024
SUBROWS = 4
ROW_CHUNK = 8
TOK_CHUNK = ROW_CHUNK // SUBROWS
WAIT_GROUP = 16
COMBINE_KCHUNK = 256
CHUNK_UNROLL = 2
EPS = 1e-6
NEG_INF = -1e30
LOG2E = math.log2(math.e)
Q_COLS = 1024
IN_COLS = 2816
MIX_WIDTH = 1024

LANES = 128
TOK_TILE = 256
ATT_TQ = 256
NA_QROWS = 4
NA_KROWS = NA_QROWS + NA_KH - 1
DIFF_KCHUNK = 256
DIFF_TQ = 256
DIFF_HEADS_PER_STEP = 2
VMEM_LIMIT = 48 * 1024 * 1024


def _params(sem):
    return pltpu.CompilerParams(dimension_semantics=sem, vmem_limit_bytes=VMEM_LIMIT)


def _dot(a, b):
    return jnp.dot(a, b, preferred_element_type=F32)


def _dot_nt(a, b):
    return lax.dot_general(a, b, (((1,), (1,)), ((), ())), preferred_element_type=F32)


def _split_bf16(a):
    hi = a.astype(BF16)
    lo = (a - hi.astype(F32)).astype(BF16)
    return hi, lo


def _sigmoid(x):
    return 1.0 / (1.0 + jnp.exp(-x))


def _rms(x, g):
    return x * lax.rsqrt(jnp.mean(x * x, axis=-1, keepdims=True) + EPS) * g


def _norm_mod(x, g, shift, scale):
    return _rms(x, g) * (1.0 + scale) + shift


def _mod_kernel(c_ref, w_ref, b_ref, o_ref):
    c = c_ref[...]
    a_hi, a_lo = _split_bf16(c * _sigmoid(c))
    w_hi, w_lo = _split_bf16(w_ref[...])
    o_ref[...] = _dot(a_hi, w_hi) + _dot(a_hi, w_lo) + _dot(a_lo, w_hi) + b_ref[...]


def _modulation(cs, w_mod, b_mod):
    n, d = cs.shape
    cols = w_mod.shape[1]
    tn = 1536
    return pl.pallas_call(
        _mod_kernel,
        grid=(cols // tn,),
        in_specs=[pl.BlockSpec((n, d), lambda j: (0, 0)),
                  pl.BlockSpec((d, tn), lambda j: (0, j)),
                  pl.BlockSpec((1, tn), lambda j: (0, j))],
        out_specs=pl.BlockSpec((n, tn), lambda j: (0, j)),
        out_shape=jax.ShapeDtypeStruct((n, cols), F32),
        compiler_params=_params(("arbitrary",)),
        name="modulation",
    )(cs, w_mod, b_mod.reshape(1, cols))


def _proj_kernel(x_ref, g_ref, mod_ref, cos_ref, sin_ref, w_ref,
                 qa_ref, qb_ref, qd_ref, ka_ref, va_ref, kb_ref, vb_ref, kd_ref, vd_ref, vdt_ref, h_scr):
    h = _norm_mod(x_ref[0], g_ref[...], mod_ref[0, 0:1, :], mod_ref[0, 1:2, :])
    h_scr[...] = h.astype(BF16)
    cos = cos_ref[...]
    sin = sin_ref[...]
    tm = h.shape[0]
    lane = lax.broadcasted_iota(I32, (tm, LANES), 1)
    first16 = (lane & 16) == 0
    lo64 = lane < HEAD_DIM

    def rope(v):
        partner = jnp.where(first16, pltpu.roll(v, LANES - 16, 1), pltpu.roll(v, 16, 1))
        return v * cos + partner * sin

    def mm(c0):
        return _dot(h_scr[...], w_ref[:, c0:c0 + 2 * LANES])

    a = mm(0) * (HEAD_DIM ** -0.5)
    c0 = rope(a[:, :LANES])
    c1 = rope(a[:, LANES:])
    zero = jnp.zeros_like(c0)
    qa_ref[0, :, 0 * LANES:1 * LANES] = jnp.where(lo64, c0, zero).astype(BF16)
    qa_ref[0, :, 1 * LANES:2 * LANES] = jnp.where(lo64, pltpu.roll(c0, HEAD_DIM, 1), zero).astype(BF16)
    qa_ref[0, :, 2 * LANES:3 * LANES] = jnp.where(lo64, zero, pltpu.roll(c1, HEAD_DIM, 1)).astype(BF16)
    qa_ref[0, :, 3 * LANES:4 * LANES] = jnp.where(lo64, zero, c1).astype(BF16)
    qb_ref[0] = (mm(256) * (HEAD_DIM ** -0.5)).astype(BF16)
    for j in range(2):
        a = mm(512 + 256 * j) * (HEAD_DIM ** -0.5 * LOG2E)
        qd_ref[0, :, 256 * j:256 * j + LANES] = rope(a[:, :LANES]).astype(BF16)
        qd_ref[0, :, 256 * j + LANES:256 * (j + 1)] = rope(a[:, LANES:]).astype(BF16)
    a = mm(1024)
    ka_ref[0] = rope(a[:, :LANES]).astype(BF16)
    va_ref[0] = a[:, LANES:].astype(BF16)
    kb_ref[0] = mm(1280).astype(BF16)
    vb_ref[0] = mm(1536).astype(BF16)
    for j in range(2):
        a = mm(1792 + 256 * j)
        kd_ref[0, :, 256 * j:256 * j + LANES] = rope(a[:, :LANES]).astype(BF16)
        kd_ref[0, :, 256 * j + LANES:256 * (j + 1)] = rope(a[:, LANES:]).astype(BF16)
        a = mm(2304 + 256 * j)
        vd_ref[0, :, 256 * j:256 * (j + 1)] = a.astype(BF16)
        vdt_ref[0, 256 * j:256 * j + LANES, :] = a[:, :LANES].T.astype(BF16)
        vdt_ref[0, 256 * j + LANES:256 * (j + 1), :] = a[:, LANES:].T.astype(BF16)


def _projection(xc, g, modall, cos, sin, w_in, s_lat):
    b, sall, d = xc.shape
    tm = TOK_TILE
    nlat = s_lat // tm
    row = lambda width: pl.BlockSpec((1, tm, width), lambda i, j: (i, j, 0))
    widths = (512, 256, 512, 128, 128, 256, 256, 512, 512)
    return pl.pallas_call(
        _proj_kernel,
        grid=(b, sall // tm),
        in_specs=[row(d),
                  pl.BlockSpec((1, d), lambda i, j: (0, 0)),
                  pl.BlockSpec((1, 8, d), lambda i, j: (2 * i + (j >= nlat).astype(I32), 0, 0)),
                  pl.BlockSpec((tm, LANES), lambda i, j: (j, 0)),
                  pl.BlockSpec((tm, LANES), lambda i, j: (j, 0)),
                  pl.BlockSpec((d, IN_COLS), lambda i, j: (0, 0))],
        out_specs=[row(w) for w in widths] + [pl.BlockSpec((1, 512, tm), lambda i, j: (i, 0, j))],
        out_shape=([jax.ShapeDtypeStruct((b, sall, w), BF16) for w in widths]
                   + [jax.ShapeDtypeStruct((b, 512, sall), BF16)]),
        scratch_shapes=[pltpu.VMEM((tm, d), BF16)],
        compiler_params=_params(("arbitrary", "arbitrary")),
        name="projection",
    )(xc, g, modall, cos, sin, w_in)


def _half_mask(q, half):
    lane = lax.broadcasted_iota(I32, q.shape, 1)
    keep = (lane < HEAD_DIM) if half == 0 else (lane >= HEAD_DIM)
    return jnp.where(keep, q, jnp.zeros_like(q))


def _merge_halves(lo_part, hi_part):
    lane = lax.broadcasted_iota(I32, lo_part.shape, 1)
    return jnp.where(lane < HEAD_DIM, lo_part, hi_part)


def _gqa_rows(q_ref, g):
    return jnp.concatenate([q_ref[0, :, (2 * g) * LANES:(2 * g + 1) * LANES],
                            q_ref[0, :, (2 * g + 1) * LANES:(2 * g + 2) * LANES]], axis=0)


def _gqa_sink(sink_ref, g, tq):
    rowi = lax.broadcasted_iota(I32, (2 * tq, 1), 0)
    return jnp.where(rowi < tq, sink_ref[2 * g], sink_ref[2 * g + 1])


def _gqa_store(o_ref, g, o, tq):
    top, bot = o[:tq], o[tq:]
    if g == 0:
        chunk = _merge_halves(top, pltpu.roll(bot, HEAD_DIM, 1))
    else:
        chunk = _merge_halves(pltpu.roll(top, HEAD_DIM, 1), bot)
    o_ref[0, :, g * LANES:(g + 1) * LANES] = chunk.astype(BF16)


def _lam(lamv_ref, lam_init):
    v = lamv_ref[...]
    a = jnp.sum(v[0:1] * v[1:2], axis=-1, keepdims=True)
    b = jnp.sum(v[2:3] * v[3:4], axis=-1, keepdims=True)
    return jnp.exp(a) - jnp.exp(b) + lam_init


def _diff_post(o, subg, lam_init):
    return _rms(o, subg) * (1.0 - lam_init)


def _swa_kernel(sink_ref, q_ref, k_ref, v_ref, o_ref, *, s_lat, c_len):
    tq = ATT_TQ
    tk = tq + 2 * SWA_WINDOW
    q0 = pl.program_id(1) * tq
    ks = pl.multiple_of(jnp.clip(q0 - SWA_WINDOW, 0, s_lat - tk), SWA_WINDOW)
    kwin = k_ref[0, pl.ds(ks, tk), :]
    vwin = v_ref[0, pl.ds(ks, tk), :]
    kc = k_ref[0, s_lat:s_lat + c_len, :]
    vc = v_ref[0, s_lat:s_lat + c_len, :]
    qpos = q0 + lax.broadcasted_iota(I32, (tq, tk), 0)
    kpos = ks + lax.broadcasted_iota(I32, (tq, tk), 1)
    valid = jnp.abs(qpos - kpos) <= SWA_WINDOW
    valid = jnp.concatenate([valid, valid], axis=0)
    for g in range(2):
        q2 = _gqa_rows(q_ref, g)
        s_loc = jnp.where(valid, _dot_nt(q2, kwin), NEG_INF)
        s_ctx = _dot_nt(q2, kc)
        snk = _gqa_sink(sink_ref, g, tq)
        m = jnp.maximum(jnp.maximum(jnp.max(s_loc, axis=-1, keepdims=True),
                                    jnp.max(s_ctx, axis=-1, keepdims=True)), snk)
        e_loc = jnp.exp(s_loc - m)
        e_ctx = jnp.exp(s_ctx - m)
        den = (jnp.sum(e_loc, axis=-1, keepdims=True) + jnp.sum(e_ctx, axis=-1, keepdims=True)
               + jnp.exp(snk - m))
        o = (_dot(e_loc.astype(BF16), vwin) + _dot(e_ctx.astype(BF16), vc)) / den
        _gqa_store(o_ref, g, o, tq)


def _swa(sink, qa, ka, va, s_lat):
    b, sall, _ = qa.shape
    kv = pl.BlockSpec((1, sall, LANES), lambda i, j: (i, 0, 0))
    return pl.pallas_call(
        functools.partial(_swa_kernel, s_lat=s_lat, c_len=sall - s_lat),
        grid=(b, s_lat // ATT_TQ),
        in_specs=[pl.BlockSpec(memory_space=pltpu.SMEM),
                  pl.BlockSpec((1, ATT_TQ, 512), lambda i, j: (i, j, 0)), kv, kv],
        out_specs=pl.BlockSpec((1, ATT_TQ, 256), lambda i, j: (i, j, 0)),
        out_shape=jax.ShapeDtypeStruct((b, s_lat, 256), BF16),
        compiler_params=_params(("arbitrary", "arbitrary")),
        name="swa",
    )(sink, qa, ka, va)


def _na_kernel(q_ref, k_ref, v_ref, bm_ref, o_ref, *, s_lat, c_len):
    rows = s_lat // GRID_W
    nk = NA_KROWS * GRID_W
    r0 = pl.program_id(1) * NA_QROWS
    ks = pl.multiple_of(jnp.clip(r0 - NA_KH // 2, 0, rows - NA_KROWS) * GRID_W, GRID_W)
    for c in range(2):
        sl = slice(c * LANES, (c + 1) * LANES)
        kwin = k_ref[0, pl.ds(ks, nk), sl]
        vwin = v_ref[0, pl.ds(ks, nk), sl]
        kc = k_ref[0, s_lat:s_lat + c_len, sl]
        vc = v_ref[0, s_lat:s_lat + c_len, sl]
        q = q_ref[0, :, sl]
        outs = []
        for half in range(2):
            qm = _half_mask(q, half)
            s_loc = _dot_nt(qm, kwin) + bm_ref[0, 2 * c + half]
            s_ctx = _dot_nt(qm, kc)
            m = jnp.maximum(jnp.max(s_loc, axis=-1, keepdims=True), jnp.max(s_ctx, axis=-1, keepdims=True))
            e_loc = jnp.exp(s_loc - m)
            e_ctx = jnp.exp(s_ctx - m)
            den = jnp.sum(e_loc, axis=-1, keepdims=True) + jnp.sum(e_ctx, axis=-1, keepdims=True)
            outs.append((_dot(e_loc.astype(BF16), vwin) + _dot(e_ctx.astype(BF16), vc)) / den)
        o_ref[0, :, sl] = _merge_halves(outs[0], outs[1]).astype(BF16)


def _na_bias_mask(rpb, rows):
    nq = NA_QROWS * GRID_W
    nk = NA_KROWS * GRID_W
    col = np.arange(GRID_W)
    cstart = np.clip(col - NA_KW // 2, 0, GRID_W - NA_KW)
    col_ok = (col[None, :] >= cstart[:, None]) & (col[None, :] < cstart[:, None] + NA_KW)
    dc = np.clip(col[None, :] - col[:, None], -(NA_KW - 1), NA_KW - 1) + NA_KW - 1
    sel_c = np.where(col_ok[:, :, None], np.eye(2 * NA_KW - 1)[dc], 0.0)
    tabs = []
    for r0 in (0, NA_QROWS, rows - NA_QROWS):
        ksr = int(np.clip(r0 - NA_KH // 2, 0, rows - NA_KROWS))
        r = r0 + np.arange(NA_QROWS)
        kr = ksr + np.arange(NA_KROWS)
        rs = np.clip(r - NA_KH // 2, 0, rows - NA_KH)
        row_ok = (kr[None, :] >= rs[:, None]) & (kr[None, :] < rs[:, None] + NA_KH)
        dr = np.clip(kr[None, :] - r[:, None], -(NA_KH - 1), NA_KH - 1) + NA_KH - 1
        sel_r = np.where(row_ok[:, :, None], np.eye(2 * NA_KH - 1)[dr], 0.0)
        ok = (row_ok[:, None, :, None] & col_ok[None, :, None, :]).reshape(nq, nk)
        bias = jnp.einsum("hdc,rjd,qkc->hrqjk", rpb.astype(F32), jnp.asarray(sel_r, F32), jnp.asarray(sel_c, F32),
                          precision=lax.Precision.HIGHEST).reshape(N_HEADS, nq, nk)
        tabs.append(jnp.where(ok[None], bias, NEG_INF))
    return jnp.stack(tabs)


def _na(qb, kb, vb, bm, s_lat):
    b, sall, _ = qb.shape
    nq = NA_QROWS * GRID_W
    nsteps = s_lat // nq
    kv = pl.BlockSpec((1, sall, 256), lambda i, j: (i, 0, 0))

    def bm_map(i, j):
        return (jnp.where(j == 0, 0, jnp.where(j == nsteps - 1, 2, 1)), 0, 0, 0)

    return pl.pallas_call(
        functools.partial(_na_kernel, s_lat=s_lat, c_len=sall - s_lat),
        grid=(b, nsteps),
        in_specs=[pl.BlockSpec((1, nq, 256), lambda i, j: (i, j, 0)), kv, kv,
                  pl.BlockSpec((1, N_HEADS, nq, NA_KROWS * GRID_W), bm_map)],
        out_specs=pl.BlockSpec((1, nq, 256), lambda i, j: (i, j, 0)),
        out_shape=jax.ShapeDtypeStruct((b, s_lat, 256), BF16),
        compiler_params=_params(("arbitrary", "arbitrary")),
        name="na2d",
    )(qb, kb, vb, bm)


def _diff_kernel(lamv_ref, subg_ref, q_ref, k_ref, vt_ref, o_ref, e_scr, m_scr, *, n_keys, lam_init):
    tq = DIFF_TQ
    chunks = [(c0, min(DIFF_KCHUNK, n_keys - c0)) for c0 in range(0, n_keys, DIFF_KCHUNK)]
    lam = _lam(lamv_ref, lam_init)
    def head_slice(hh):
        return slice(hh * LANES, (hh + 1) * LANES)

    def masked_queries(hh):
        q = q_ref[0, :, head_slice(hh)]
        return jnp.concatenate([_half_mask(q, 0), _half_mask(q, 1)], axis=0)

    def exp_chunk(hh, q12, i, m, den):
        c0, n = chunks[i]
        s = _dot_nt(k_ref[0, c0:c0 + n, head_slice(hh)], q12)
        m_new = jnp.maximum(m, jnp.max(s, axis=0, keepdims=True))
        e = jnp.exp2(s - m_new)
        e_scr[hh, c0:c0 + n, :] = e
        m_scr[hh, i:i + 1, :] = m_new
        return m_new, den * jnp.exp2(m - m_new) + jnp.sum(e, axis=0, keepdims=True)

    def pv_chunk(hh, i, m, den, acc):
        c0, n = chunks[i]
        f = jnp.exp2(m_scr[hh, i:i + 1, :] - m) / den
        p = e_scr[hh, c0:c0 + n, :tq] * f[:, :tq] - e_scr[hh, c0:c0 + n, tq:] * (lam * f[:, tq:])
        return acc + _dot(vt_ref[0, head_slice(hh), c0:c0 + n], p.astype(BF16))

    def finish(hh, acc):
        o_ref[0, :, head_slice(hh)] = _diff_post(acc.T, subg_ref[...], lam_init).astype(BF16)

    stats = []
    for hh in range(DIFF_HEADS_PER_STEP):
        q12 = masked_queries(hh)
        m, den = jnp.full((1, 2 * tq), NEG_INF, F32), jnp.zeros((1, 2 * tq), F32)
        for i in range(len(chunks)):
            m, den = exp_chunk(hh, q12, i, m, den)
        stats.append((m, den))
    for hh in range(DIFF_HEADS_PER_STEP):
        acc = jnp.zeros((LANES, tq), F32)
        for i in range(len(chunks)):
            acc = pv_chunk(hh, i, stats[hh][0], stats[hh][1], acc)
        finish(hh, acc)


def _diff(lamv, subg, qd, kd, vdt, s_lat, lam_init):
    b, sall, _ = qd.shape
    nchunk = -(-sall // DIFF_KCHUNK)
    hps = DIFF_HEADS_PER_STEP
    wid = hps * LANES
    return pl.pallas_call(
        functools.partial(_diff_kernel, n_keys=sall, lam_init=lam_init),
        grid=(b, N_HEADS // hps, s_lat // DIFF_TQ),
        in_specs=[pl.BlockSpec((8, LANES), lambda i, h, j: (0, 0)),
                  pl.BlockSpec((1, LANES), lambda i, h, j: (0, 0)),
                  pl.BlockSpec((1, DIFF_TQ, wid), lambda i, h, j: (i, j, h)),
                  pl.BlockSpec((1, sall, wid), lambda i, h, j: (i, 0, h)),
                  pl.BlockSpec((1, wid, sall), lambda i, h, j: (i, h, 0))],
        out_specs=pl.BlockSpec((1, DIFF_TQ, wid), lambda i, h, j: (i, j, h)),
        out_shape=jax.ShapeDtypeStruct((b, s_lat, 512), BF16),
        scratch_shapes=[pltpu.VMEM((hps, sall, 2 * DIFF_TQ), F32),
                        pltpu.VMEM((hps, -(-nchunk // 8) * 8, 2 * DIFF_TQ), F32)],
        compiler_params=_params(("arbitrary", "arbitrary", "arbitrary")),
        name="diff_attn",
    )(lamv, subg, qd, kd, vdt)


def _ctx_kernel(sink_ref, lamv_ref, subg_ref, qa_ref, ka_ref, va_ref, qb_ref, kb_ref, vb_ref,
                qd_ref, kd_ref, vd_ref, oa_ref, ob_ref, od_ref, *, lam_init):
    cl = qa_ref.shape[1]
    ka = ka_ref[0]
    va = va_ref[0]
    for g in range(2):
        q2 = _gqa_rows(qa_ref, g)
        s = _dot_nt(q2, ka)
        snk = _gqa_sink(sink_ref, g, cl)
        m = jnp.maximum(jnp.max(s, axis=-1, keepdims=True), snk)
        e = jnp.exp(s - m)
        den = jnp.sum(e, axis=-1, keepdims=True) + jnp.exp(snk - m)
        _gqa_store(oa_ref, g, _dot(e.astype(BF16), va) / den, cl)
    for c in range(2):
        sl = slice(c * LANES, (c + 1) * LANES)
        outs = []
        for half in range(2):
            s = _dot_nt(_half_mask(qb_ref[0, :, sl], half), kb_ref[0, :, sl])
            e = jnp.exp(s - jnp.max(s, axis=-1, keepdims=True))
            outs.append(_dot(e.astype(BF16), vb_ref[0, :, sl]) / jnp.sum(e, axis=-1, keepdims=True))
        ob_ref[0, :, sl] = _merge_halves(outs[0], outs[1]).astype(BF16)
    lam = _lam(lamv_ref, lam_init)
    for h in range(N_HEADS):
        sl = slice(h * LANES, (h + 1) * LANES)
        q = qd_ref[0, :, sl]
        q12 = jnp.concatenate([_half_mask(q, 0), _half_mask(q, 1)], axis=0)
        s = _dot_nt(q12, kd_ref[0, :, sl])
        e = jnp.exp2(s - jnp.max(s, axis=-1, keepdims=True))
        den = jnp.sum(e, axis=-1, keepdims=True)
        p = e[:cl] * (1.0 / den[:cl]) - e[cl:] * (lam / den[cl:])
        o = _dot(p.astype(BF16), vd_ref[0, :, sl])
        od_ref[0, :, sl] = _diff_post(o, subg_ref[...], lam_init).astype(BF16)


def _ctx_attention(sink, lamv, subg, qa, ka, va, qb, kb, vb, qd, kd, vd, s_lat, lam_init):
    b, sall, _ = qa.shape
    cl = sall - s_lat
    blk = s_lat // cl
    row = lambda width: pl.BlockSpec((1, cl, width), lambda i: (i, blk, 0))
    out = lambda width: pl.BlockSpec((1, cl, width), lambda i: (i, 0, 0))
    return pl.pallas_call(
        functools.partial(_ctx_kernel, lam_init=lam_init),
        grid=(b,),
        in_specs=[pl.BlockSpec(memory_space=pltpu.SMEM),
                  pl.BlockSpec((8, LANES), lambda i: (0, 0)),
                  pl.BlockSpec((1, LANES), lambda i: (0, 0)),
                  row(512), row(128), row(128), row(256), row(256), row(256), row(512), row(512), row(512)],
        out_specs=[out(256), out(256), out(512)],
        out_shape=[jax.ShapeDtypeStruct((b, cl, w), BF16) for w in (256, 256, 512)],
        compiler_params=_params(("arbitrary",)),
        name="ctx_attention",
    )(sink, lamv, subg, qa, ka, va, qb, kb, vb, qd, kd, vd)


def _out_kernel(x_ref, w_ref, mod_ref, *refs, nlat):
    xo_ref = refs[-1]

    def emit(oa_ref, ob_ref, od_ref):
        attn = (_dot(oa_ref[0], w_ref[0:256, :]) + _dot(ob_ref[0], w_ref[256:512, :])
                + _dot(od_ref[0], w_ref[512:1024, :]))
        xo_ref[0] = x_ref[0] + mod_ref[0, 2:3, :] * attn

    if len(refs) == 4:
        emit(*refs[:3])
    else:
        is_ctx = pl.program_id(1) >= nlat
        pl.when(jnp.logical_not(is_ctx))(lambda: emit(*refs[:3]))
        pl.when(is_ctx)(lambda: emit(*refs[3:6]))


def _out_projection(xc, lat, ctx, w_out, modall, s_lat):
    b, sall, d = xc.shape
    tm = TOK_TILE
    nlat = s_lat // tm
    n_rows = s_lat if ctx is None else sall
    row = lambda width: pl.BlockSpec((1, tm, width), lambda i, j: (i, j, 0))
    lat_row = lambda width: pl.BlockSpec((1, tm, width), lambda i, j: (i, jnp.minimum(j, nlat - 1), 0))
    ctx_row = lambda width: pl.BlockSpec((1, tm, width), lambda i, j: (i, 0, 0))
    widths = (256, 256, 512)
    specs = [lat_row(w) for w in widths] + ([] if ctx is None else [ctx_row(w) for w in widths])
    return pl.pallas_call(
        functools.partial(_out_kernel, nlat=nlat),
        grid=(b, n_rows // tm),
        in_specs=[row(d),
                  pl.BlockSpec((MIX_WIDTH, d), lambda i, j: (0, 0)),
                  pl.BlockSpec((1, 8, d), lambda i, j: (2 * i + (j >= nlat).astype(I32), 0, 0))] + specs,
        out_specs=row(d),
        out_shape=jax.ShapeDtypeStruct((b, n_rows, d), F32),
        compiler_params=_params(("arbitrary", "arbitrary")),
        name="out_projection",
    )(xc, w_out, modall, *lat, *(() if ctx is None else ctx))


def _first_argmax(v, iota, n):
    m = jnp.max(v, axis=0, keepdims=True)
    ix = jnp.min(jnp.where(v == m, iota, float(n)), axis=0, keepdims=True)
    return m, ix


def _router_kernel(x_ref, g_ref, mod_ref, whi_ref, wlo_ref, bias_ref, tri_ref, ltri_ref,
                   w_ref, slot_ref, meta_ref, cnt_ref):
    @pl.when(pl.program_id(0) == 0)
    def _():
        cnt_ref[...] = jnp.zeros_like(cnt_ref)

    hf = _norm_mod(x_ref[...], g_ref[...], mod_ref[0, 3:4, :], mod_ref[0, 4:5, :])
    tm = hf.shape[0]
    h_hi, h_lo = _split_bf16(hf)
    whi = whi_ref[...]
    logits = _dot_nt(whi, h_hi) + _dot_nt(whi, h_lo) + _dot_nt(wlo_ref[...], h_hi)
    scores = _sigmoid(logits)
    biased = scores + bias_ref[...]
    gsz = N_EXPERTS // N_GROUPS
    iota_g = lax.broadcasted_iota(I32, (gsz, tm), 0).astype(F32)
    gscore = []
    for g in range(N_GROUPS):
        v = biased[g * gsz:(g + 1) * gsz]
        m1, i1 = _first_argmax(v, iota_g, gsz)
        m2 = jnp.max(jnp.where(iota_g == i1, -jnp.inf, v), axis=0, keepdims=True)
        gscore.append(m1 + m2)
    cur = jnp.concatenate(gscore, axis=0)
    iota_n = lax.broadcasted_iota(I32, (N_GROUPS, tm), 0).astype(F32)
    gsel = jnp.zeros((N_GROUPS, tm), F32)
    for _ in range(TOPK_GROUPS):
        _, ix = _first_argmax(cur, iota_n, N_GROUPS)
        hit = iota_n == ix
        gsel = jnp.where(hit, 1.0, gsel)
        cur = jnp.where(hit, -jnp.inf, cur)
    masked = jnp.concatenate(
        [jnp.where(gsel[g:g + 1] > 0.5, biased[g * gsz:(g + 1) * gsz], -jnp.inf) for g in range(N_GROUPS)],
        axis=0)
    iota_e = lax.broadcasted_iota(I32, (N_EXPERTS, tm), 0).astype(F32)
    onehot = jnp.zeros((N_EXPERTS, tm), F32)
    idxs, ws = [], []
    for _ in range(TOP_K):
        _, ix = _first_argmax(masked, iota_e, N_EXPERTS)
        hit = iota_e == ix
        ws.append(jnp.sum(jnp.where(hit, scores, 0.0), axis=0, keepdims=True))
        masked = jnp.where(hit, -jnp.inf, masked)
        onehot = jnp.where(hit, 1.0, onehot)
        idxs.append(ix)
    base = cnt_ref[...]
    rank = _dot(onehot.astype(BF16), tri_ref[...])
    n_col = jnp.sum(onehot, axis=1, keepdims=True)
    n_pad = jnp.floor((n_col + (TOK_CHUNK - 1)) * (1.0 / TOK_CHUNK)) * TOK_CHUNK
    start = _dot(ltri_ref[...], jnp.broadcast_to(n_pad, (N_EXPERTS, LANES)).astype(BF16))
    wsum = ws[0]
    for wk in ws[1:]:
        wsum = wsum + wk
    for k in range(TOP_K):
        slot = jnp.sum(jnp.where(iota_e == idxs[k], rank + start[:, 0:1], 0.0), axis=0, keepdims=True)
        slot_ref[0, k:k + 1, :] = slot.astype(I32)
        w_ref[0, k:k + 1, :] = ws[k] / wsum * ROUTED_SCALE
    lane = lax.broadcasted_iota(I32, (N_EXPERTS, LANES), 1)
    meta_ref[0] = jnp.where(lane == 0, n_pad, jnp.where(lane == 1, base, start))
    cnt_ref[...] = base + n_pad


def _router(xflat, g, modall, whi, wlo, bias, tri, ltri, tiles_per_batch, nlat):
    t, d = xflat.shape
    tm = TOK_TILE
    nt = t // tm
    per_tok = pl.BlockSpec((1, TOP_K, tm), lambda i: (i, 0, 0))

    def mod_map(i):
        return (2 * (i // tiles_per_batch) + ((i % tiles_per_batch) >= nlat).astype(I32), 0, 0)

    return pl.pallas_call(
        _router_kernel,
        grid=(nt,),
        in_specs=[pl.BlockSpec((tm, d), lambda i: (i, 0)),
                  pl.BlockSpec((1, d), lambda i: (0, 0)),
                  pl.BlockSpec((1, 8, d), mod_map),
                  pl.BlockSpec((N_EXPERTS, d), lambda i: (0, 0)),
                  pl.BlockSpec((N_EXPERTS, d), lambda i: (0, 0)),
                  pl.BlockSpec((N_EXPERTS, 1), lambda i: (0, 0)),
                  pl.BlockSpec((tm, tm), lambda i: (0, 0)),
                  pl.BlockSpec((N_EXPERTS, N_EXPERTS), lambda i: (0, 0))],
        out_specs=[per_tok, per_tok,
                   pl.BlockSpec((1, N_EXPERTS, LANES), lambda i: (i, 0, 0)),
                   pl.BlockSpec((N_EXPERTS, 1), lambda i: (0, 0))],
        out_shape=[jax.ShapeDtypeStruct((nt, TOP_K, tm), F32),
                   jax.ShapeDtypeStruct((nt, TOP_K, tm), I32),
                   jax.ShapeDtypeStruct((nt, N_EXPERTS, LANES), F32),
                   jax.ShapeDtypeStruct((N_EXPERTS, 1), F32)],
        compiler_params=_params(("arbitrary",)),
        name="router",
    )(xflat, g, modall, whi, wlo, bias, tri, ltri)


def _pack_bf16_pairs(v, is_bf16_valued=False):
    half = v.shape[1] // 2
    if not is_bf16_valued:
        v = v.astype(BF16).astype(F32)
    lo = lax.shift_right_logical(lax.bitcast_convert_type(v[:, :half], U32), jnp.uint32(16))
    hi = lax.bitcast_convert_type(v[:, half:], U32) & jnp.uint32(0xFFFF0000)
    return lo | hi


def _unpack_bf16_pairs(u):
    lo = lax.bitcast_convert_type(lax.shift_left(u, jnp.uint32(16)), F32).astype(BF16)
    hi = lax.bitcast_convert_type(u & jnp.uint32(0xFFFF0000), F32).astype(BF16)
    return lo, hi


def _store_subrows(ref, slot0, v, is_bf16_valued=False):
    n, d = v.shape
    fw = d // SUBROWS
    for q in range(SUBROWS):
        packed = _pack_bf16_pairs(v[:, q * fw:(q + 1) * fw], is_bf16_valued)
        if SUBROWS == 1:
            ref[pl.ds(slot0, n), :] = packed
        else:
            ref[pl.ds(SUBROWS * slot0 + q, n, stride=SUBROWS), :] = packed


def _load_subrows(ref, slot0, n, keep=None):
    slabs = []
    for q in range(SUBROWS):
        u = ref[pl.ds(slot0, n), :] if SUBROWS == 1 else ref[pl.ds(SUBROWS * slot0 + q, n, stride=SUBROWS), :]
        if keep is not None:
            u = jnp.where(keep, u, jnp.zeros_like(u))
        lo, hi = _unpack_bf16_pairs(u)
        slabs.append(jnp.concatenate([lo, hi], axis=1))
    return slabs


def _used_chunks(meta_ref):
    return lax.shift_right_logical(_slots_used(meta_ref), TOK_CHUNK.bit_length() - 1)


def _start_chunk_copies(first, count, make_copy):
    pairs = lax.shift_right_logical(count, 1)

    def start_pair(j, carry):
        make_copy(first + 2 * j).start()
        make_copy(first + 2 * j + 1).start()
        return carry

    lax.fori_loop(0, pairs, start_pair, 0)

    @pl.when(count > 2 * pairs)
    def _():
        make_copy(first + count - 1).start()


def _wait_chunk_copies(total, copy_of_rows):
    group = lax.shift_right_logical(total, WAIT_GROUP.bit_length() - 1)

    def wait_group(j, carry):
        copy_of_rows(WAIT_GROUP * ROW_CHUNK).wait()
        return carry

    def wait_one(j, carry):
        copy_of_rows(ROW_CHUNK).wait()
        return carry

    lax.fori_loop(0, group, wait_group, 0)
    lax.fori_loop(0, total - group * WAIT_GROUP, wait_one, 0)


def _tile_slots(tm):
    trip = CHUNK_UNROLL * max(COMBINE_KCHUNK, tm)
    return -(-(TOP_K * tm + TOK_CHUNK * N_EXPERTS) // trip) * trip


def _slots_used(meta_ref):
    last = N_EXPERTS - 1
    return meta_ref[0, 2, last] + meta_ref[0, 0, last]


def _dispatch_kernel(meta_ref, rows_ref, x_ref, g_ref, mod_ref, slot_ref, xs_ref, xc_scr, sem):
    hb = _norm_mod(x_ref[...], g_ref[...], mod_ref[0, 3:4, :], mod_ref[0, 4:5, :]).astype(BF16)
    tm = hb.shape[0]
    slot = slot_ref[0]

    total = _used_chunks(meta_ref)
    step = CHUNK_UNROLL * tm
    chunks_per_step = step // TOK_CHUNK

    def make_copy(j):
        src = pl.multiple_of(j * ROW_CHUNK, ROW_CHUNK)
        dst = pl.multiple_of(rows_ref[0, 0, j], ROW_CHUNK)
        return pltpu.make_async_copy(xc_scr.at[pl.ds(src, ROW_CHUNK)], xs_ref.at[pl.ds(dst, ROW_CHUNK)], sem)

    def select_rows(r2, carry):
        for u in range(CHUNK_UNROLL):
            r0 = pl.multiple_of((r2 * CHUNK_UNROLL + u) * tm, tm)
            row = lax.broadcasted_iota(I32, (tm, tm), 0) + r0
            sel = jnp.zeros((tm, tm), F32)
            for k in range(TOP_K):
                sel = jnp.where(row == slot[k:k + 1, :], 1.0, sel)
            _store_subrows(xc_scr, r0, _dot(sel.astype(BF16), hb), is_bf16_valued=True)
        first = r2 * chunks_per_step
        _start_chunk_copies(first, jnp.clip(total - first, 0, chunks_per_step), make_copy)
        return carry

    lax.fori_loop(0, (_slots_used(meta_ref) + step - 1) // step, select_rows, 0)
    _wait_chunk_copies(total, lambda n: pltpu.make_async_copy(xc_scr.at[pl.ds(0, n)], xs_ref.at[pl.ds(0, n)], sem))


def _dispatch(meta, chunk_rows, slot, xflat, g, modall, n_rows, tiles_per_batch, nlat):
    t, d = xflat.shape
    tm = TOK_TILE
    nt = t // tm

    def mod_map(i):
        return (2 * (i // tiles_per_batch) + ((i % tiles_per_batch) >= nlat).astype(I32), 0, 0)

    return pl.pallas_call(
        _dispatch_kernel,
        grid=(nt,),
        in_specs=[pl.BlockSpec((1, 3, N_EXPERTS), lambda i: (i, 0, 0), memory_space=pltpu.SMEM),
                  pl.BlockSpec((1, 1, chunk_rows.shape[2]), lambda i: (i, 0, 0), memory_space=pltpu.SMEM),
                  pl.BlockSpec((tm, d), lambda i: (i, 0)),
                  pl.BlockSpec((1, d), lambda i: (0, 0)),
                  pl.BlockSpec((1, 8, d), mod_map),
                  pl.BlockSpec((1, TOP_K, tm), lambda i: (i, 0, 0))],
        out_specs=pl.BlockSpec(memory_space=pl.ANY),
        out_shape=jax.ShapeDtypeStruct((n_rows * SUBROWS, d // (2 * SUBROWS)), U32),
        scratch_shapes=[pltpu.VMEM((_tile_slots(tm) * SUBROWS, d // (2 * SUBROWS)), U32),
                        pltpu.SemaphoreType.DMA(())],
        compiler_params=_params(("arbitrary",)),
        name="moe_dispatch",
    )(meta, chunk_rows, xflat, g, modall, slot)


def _expert_kernel(blk_e_ref, nvalid_ref, nact_ref, next_e_ref, has_next_ref, parity_ref,
                   xs_ref, wg_any, wu_any, wd_any, ys_ref, wg_s, wu_s, wd_s, wg_buf, wu_buf, wd_buf, sem, *, layer):
    b = pl.program_id(0)
    e = blk_e_ref[b]
    changed = jnp.logical_or(b == 0, e != blk_e_ref[jnp.maximum(b - 1, 0)])
    par = parity_ref[b]

    def weight_copies(expert, p):
        return [pltpu.make_async_copy(src.at[layer, expert], dst.at[p], sem.at[p])
                for src, dst in ((wg_any, wg_buf), (wu_any, wu_buf), (wd_any, wd_buf))]

    @pl.when(b == 0)
    def _():
        for c in weight_copies(e, par):
            c.start()

    @pl.when(changed)
    def _():
        for c in weight_copies(e, par):
            c.wait()
        wg_s[...] = wg_buf[par].astype(BF16)
        wu_s[...] = wu_buf[par].astype(BF16)
        wd_s[...] = wd_buf[par].astype(BF16)

        @pl.when(has_next_ref[b] == 1)
        def _():
            for c in weight_copies(next_e_ref[b], 1 - par):
                c.start()

    n_here = jnp.where(b < nact_ref[0], nvalid_ref[b], 0)

    @pl.when(n_here > 0)
    def _():
        fw = wg_s.shape[0] // SUBROWS
        pick = lax.broadcasted_iota(I32, (MOE_BLOCK, xs_ref.shape[1]), 0)
        slabs = _load_subrows(xs_ref, 0, MOE_BLOCK, keep=pick < n_here)
        gate = _dot(slabs[0], wg_s[0:fw, :])
        up = _dot(slabs[0], wu_s[0:fw, :])
        for q in range(1, SUBROWS):
            gate = gate + _dot(slabs[q], wg_s[q * fw:(q + 1) * fw, :])
            up = up + _dot(slabs[q], wu_s[q * fw:(q + 1) * fw, :])
        mid = (gate * _sigmoid(gate)) * up
        _store_subrows(ys_ref, 0, _dot(mid.astype(BF16), wd_s[...]))

    @pl.when(n_here == 0)
    def _():
        ys_ref[...] = jnp.zeros_like(ys_ref)


def _experts(blk_e, nvalid, nact, xs, w_gate, w_up, w_down, layer):
    n_rows, width = xs.shape
    _, _, d, f = w_gate.shape
    blk_rows = MOE_BLOCK * SUBROWS
    n_blk = n_rows // blk_rows
    first_of_next = jnp.sum((blk_e[None, :] <= blk_e[:, None]).astype(I32), axis=1)
    has_next = (first_of_next < n_blk).astype(I32)
    next_e = blk_e[jnp.minimum(first_of_next, n_blk - 1)]
    starts = jnp.concatenate([jnp.zeros((1,), I32), (blk_e[1:] != blk_e[:-1]).astype(I32)])
    parity = jnp.cumsum(starts) % 2
    anyspec = pl.BlockSpec(memory_space=pl.ANY)
    grid_spec = pltpu.PrefetchScalarGridSpec(
        num_scalar_prefetch=6,
        grid=(n_blk,),
        in_specs=[pl.BlockSpec((blk_rows, width), lambda i, be, nv, na, *_: (jnp.minimum(i, na[0] - 1), 0)),
                  anyspec, anyspec, anyspec],
        out_specs=pl.BlockSpec((blk_rows, width), lambda i, be, nv, na, *_: (jnp.minimum(i, na[0]), 0)),
        scratch_shapes=[pltpu.VMEM((d, f), BF16), pltpu.VMEM((d, f), BF16), pltpu.VMEM((f, d), BF16),
                        pltpu.VMEM((2, d, f), F32), pltpu.VMEM((2, d, f), F32), pltpu.VMEM((2, f, d), F32),
                        pltpu.SemaphoreType.DMA((2,))],
    )
    return pl.pallas_call(
        functools.partial(_expert_kernel, layer=layer),
        grid_spec=grid_spec,
        out_shape=jax.ShapeDtypeStruct((n_rows, width), U32),
        compiler_params=_params(("arbitrary",)),
        name="moe_experts",
    )(blk_e, nvalid, nact, next_e, has_next, parity.astype(I32), xs, w_gate, w_up, w_down)


def _combine_kernel(meta_ref, rows_ref, x_ref, g_ref, mod_ref, spos_ref, wt_ref, wsg_ref, wsu_ref, wsd_ref,
                    gfin_ref, ys_ref, o_ref, y_scr, acc_scr, sb_scr, wb_scr, sem, *, final_norm):
    tm, d = x_ref.shape
    fw = d // SUBROWS

    @pl.when(pl.program_id(0) == 0)
    def _():
        y_scr[...] = jnp.zeros_like(y_scr)

    step = CHUNK_UNROLL * COMBINE_KCHUNK
    chunks_per_step = step // TOK_CHUNK
    group_shift = chunks_per_step.bit_length() - 1

    def make_copy(j):
        src = pl.multiple_of(rows_ref[0, 0, j], ROW_CHUNK)
        dst = pl.multiple_of(j * ROW_CHUNK, ROW_CHUNK)
        return pltpu.make_async_copy(ys_ref.at[pl.ds(src, ROW_CHUNK)], y_scr.at[pl.ds(dst, ROW_CHUNK)],
                                     sem.at[lax.shift_right_logical(j, group_shift)])

    total = _used_chunks(meta_ref)
    _start_chunk_copies(0, total, make_copy)

    x = x_ref[...]
    hb = _norm_mod(x, g_ref[...], mod_ref[0, 3:4, :], mod_ref[0, 4:5, :]).astype(BF16)
    gate = _dot(hb, wsg_ref[...])
    mid = (gate * _sigmoid(gate)) * _dot(hb, wsu_ref[...])
    acc_scr[...] = _dot(mid.astype(BF16), wsd_ref[...])

    n_used = _slots_used(meta_ref)
    for k in range(TOP_K):
        sb_scr[k] = jnp.broadcast_to(spos_ref[0, :, k:k + 1], (tm, LANES))
        wb_scr[k] = jnp.broadcast_to(wt_ref[0, :, k:k + 1], (tm, LANES))

    def chunk(j2, carry):
        _wait_chunk_copies(jnp.clip(total - j2 * chunks_per_step, 0, chunks_per_step),
                           lambda n: pltpu.make_async_copy(ys_ref.at[pl.ds(0, n)], y_scr.at[pl.ds(0, n)], sem.at[j2]))
        sums = [jnp.zeros((tm, fw), F32) for _ in range(SUBROWS)]
        for u in range(CHUNK_UNROLL):
            r0 = pl.multiple_of((j2 * CHUNK_UNROLL + u) * COMBINE_KCHUNK, COMBINE_KCHUNK)
            parts = []
            for c in range(COMBINE_KCHUNK // LANES):
                col = lax.broadcasted_iota(I32, (tm, LANES), 1) + (r0 + c * LANES)
                p = jnp.zeros((tm, LANES), F32)
                for k in range(TOP_K):
                    p = jnp.where(col == sb_scr[k], wb_scr[k], p)
                parts.append(p.astype(BF16))
            pb = jnp.concatenate(parts, axis=1)
            slabs = _load_subrows(y_scr, r0, COMBINE_KCHUNK)
            sums = [acc + _dot(pb, slab) for acc, slab in zip(sums, slabs)]
        for q in range(SUBROWS):
            acc_scr[:, q * fw:(q + 1) * fw] += sums[q]
        return carry

    lax.fori_loop(0, (n_used + step - 1) // step, chunk, 0)
    out = x + mod_ref[0, 5:6, :] * acc_scr[...]
    if final_norm:
        out = _rms(out, gfin_ref[...])
    o_ref[...] = out


def _combine(meta, chunk_rows, spos_t, wt, xflat, g, modall, wsg, wsu, wsd, gfin, ys, tiles_per_batch, nlat,
             final_norm):
    t, d = xflat.shape
    f = wsg.shape[1]
    tm = TOK_TILE
    nt = t // tm
    y_rows = _tile_slots(tm)
    assert y_rows % COMBINE_KCHUNK == 0

    def mod_map(i):
        return (2 * (i // tiles_per_batch) + ((i % tiles_per_batch) >= nlat).astype(I32), 0, 0)

    return pl.pallas_call(
        functools.partial(_combine_kernel, final_norm=final_norm),
        grid=(nt,),
        in_specs=[pl.BlockSpec((1, 3, N_EXPERTS), lambda i: (i, 0, 0), memory_space=pltpu.SMEM),
                  pl.BlockSpec((1, 1, chunk_rows.shape[2]), lambda i: (i, 0, 0), memory_space=pltpu.SMEM),
                  pl.BlockSpec((tm, d), lambda i: (i, 0)),
                  pl.BlockSpec((1, d), lambda i: (0, 0)),
                  pl.BlockSpec((1, 8, d), mod_map),
                  pl.BlockSpec((1, tm, TOP_K), lambda i: (i, 0, 0)),
                  pl.BlockSpec((1, tm, TOP_K), lambda i: (i, 0, 0)),
                  pl.BlockSpec((d, f), lambda i: (0, 0)),
                  pl.BlockSpec((d, f), lambda i: (0, 0)),
                  pl.BlockSpec((f, d), lambda i: (0, 0)),
                  pl.BlockSpec((1, d), lambda i: (0, 0)),
                  pl.BlockSpec(memory_space=pl.ANY)],
        out_specs=pl.BlockSpec((tm, d), lambda i: (i, 0)),
        out_shape=jax.ShapeDtypeStruct((t, d), F32),
        scratch_shapes=[pltpu.VMEM((y_rows * SUBROWS, d // (2 * SUBROWS)), U32), pltpu.VMEM((tm, d), F32),
                        pltpu.VMEM((TOP_K, tm, LANES), I32), pltpu.VMEM((TOP_K, tm, LANES), F32),
                        pltpu.SemaphoreType.DMA((y_rows // (CHUNK_UNROLL * COMBINE_KCHUNK),))],
        compiler_params=_params(("arbitrary",)),
        name="moe_combine",
    )(meta, chunk_rows, xflat, g, modall, spos_t, wt, wsg, wsu, wsd, gfin, ys)


def _rope_tables(s_lat, c_len):
    n_freq = HEAD_DIM // 4
    inv = 1.0 / (ROPE_BASE ** (jnp.arange(n_freq, dtype=F32) / n_freq))
    t = jnp.arange(s_lat)
    pos = jnp.stack([(t // GRID_W).astype(F32), (t % GRID_W).astype(F32)], axis=1)
    lane = np.arange(LANES)
    ang = pos[:, (lane % HEAD_DIM) // 32] * inv[lane % 16][None, :]
    sign = jnp.asarray(np.where(lane % 32 < 16, -1.0, 1.0), F32)
    cos = jnp.concatenate([jnp.cos(ang), jnp.ones((c_len, LANES), F32)], axis=0)
    sin = jnp.concatenate([jnp.sin(ang) * sign, jnp.zeros((c_len, LANES), F32)], axis=0)
    return cos, sin


def _moe(xflat, lp, modall, gfin, tiles_per_batch, nlat, final_norm):
    t, d = xflat.shape
    nt = t // TOK_TILE
    g = lp["g_ffn"].reshape(1, d)
    w_r = lp["w_router"].T
    whi = w_r.astype(BF16)
    wlo = (w_r - whi.astype(F32)).astype(BF16)
    tri = jnp.asarray(np.triu(np.ones((TOK_TILE, TOK_TILE), np.float32), 1), BF16)
    ltri = jnp.asarray(np.tril(np.ones((N_EXPERTS, N_EXPERTS), np.float32), -1), BF16)
    w, slot, meta, cnt = _router(xflat, g, modall, whi, wlo, lp["router_bias"].reshape(N_EXPERTS, 1),
                                 tri, ltri, tiles_per_batch, nlat)
    meta = jnp.transpose(meta[:, :, :3].astype(I32), (0, 2, 1))
    rows_e = cnt[:, 0].astype(I32)
    seg = (rows_e + MOE_BLOCK - 1) // MOE_BLOCK * MOE_BLOCK
    pend = jnp.cumsum(seg)
    pstart = (pend - seg).astype(I32)
    n_blk = (t * TOP_K + TOK_CHUNK * N_EXPERTS * nt) // MOE_BLOCK + N_EXPERTS
    blk_row0 = jnp.arange(n_blk, dtype=I32) * MOE_BLOCK
    blk_e = jnp.minimum(jnp.sum((pend[None, :] <= blk_row0[:, None]).astype(I32), axis=1), N_EXPERTS - 1)
    of_blk = blk_e[:, None] == jnp.arange(N_EXPERTS, dtype=I32)[None, :]
    seg_end = jnp.sum(jnp.where(of_blk, (pstart + rows_e)[None, :], 0), axis=1)
    nvalid = jnp.clip(seg_end - blk_row0, 0, MOE_BLOCK).astype(I32)
    nact = (pend[-1] // MOE_BLOCK).astype(I32).reshape(1)
    n_pad, base, start = meta[:, 0, :], meta[:, 1, :], meta[:, 2, :]
    chunk_slot = jnp.arange(_tile_slots(TOK_TILE) // TOK_CHUNK, dtype=I32) * TOK_CHUNK
    s = chunk_slot[None, :, None]
    in_run = (start[:, None, :] <= s) & (s < (start + n_pad)[:, None, :])
    pick0 = pstart[None, :] + base - start
    chunk_rows = ((jnp.sum(jnp.where(in_run, pick0[:, None, :], 0), axis=2) + chunk_slot[None, :])
                  * SUBROWS)[:, None, :]
    xs = _dispatch(meta, chunk_rows, slot, xflat, g, modall, n_blk * MOE_BLOCK, tiles_per_batch, nlat)
    ys = _experts(blk_e, nvalid, nact, xs, lp["w_gate"], lp["w_up"], lp["w_down"], lp["layer"])
    return _combine(meta, chunk_rows, jnp.transpose(slot, (0, 2, 1)), jnp.transpose(w, (0, 2, 1)), xflat, g, modall,
                    lp["ws_gate"].astype(BF16), lp["ws_up"].astype(BF16), lp["ws_down"].astype(BF16), gfin, ys,
                    tiles_per_batch, nlat, final_norm)


def _layer(xc, cs, cos, sin, lp, layer_idx, last, s_lat, gfin):
    b, sall, d = xc.shape
    c_len = sall - s_lat
    lam_init = 0.8 - 0.6 * math.exp(-0.3 * layer_idx)
    mod = _modulation(cs, lp["w_mod"], lp["b_mod"])
    mod_lat = mod[:b].reshape(b, 1, 6, d)
    mod_ctx = jnp.broadcast_to(mod[b].reshape(1, 1, 6, d), (b, 1, 6, d))
    modall = jnp.concatenate([mod_lat, mod_ctx], axis=1)
    modall = jnp.pad(modall, ((0, 0), (0, 0), (0, 2), (0, 0))).reshape(2 * b, 8, d)

    qa, qb, qd, ka, va, kb, vb, kd, vd, vdt = _projection(
        xc, lp["g_mix"].reshape(1, d), modall, cos, sin, lp["w_in"].astype(BF16), s_lat)
    sink = lp["attn_sink"].astype(F32)
    lamv = jnp.zeros((8, LANES), F32).at[0:4, 0:HEAD_DIM].set(
        jnp.stack([lp["lam_q1"], lp["lam_k1"], lp["lam_q2"], lp["lam_k2"]]).astype(F32))
    subg = lp["subln_g"].reshape(1, LANES).astype(F32)
    oa = _swa(sink, qa, ka, va, s_lat)
    ob = _na(qb, kb, vb, _na_bias_mask(lp["na_rpb"], s_lat // GRID_W), s_lat)
    od = _diff(lamv, subg, qd, kd, vdt, s_lat, lam_init)
    ctx_out = None
    if not last:
        ctx_out = _ctx_attention(sink, lamv, subg, qa, ka, va, qb, kb, vb, qd, kd, vd, s_lat, lam_init)
    n_rows = s_lat if last else sall
    x2 = _out_projection(xc, (oa, ob, od), ctx_out, lp["w_out"].astype(BF16), modall, s_lat)
    tiles_per_batch = n_rows // TOK_TILE
    y = _moe(x2.reshape(b * n_rows, d), lp, modall, gfin, tiles_per_batch, s_lat // TOK_TILE, last)
    return y.reshape(b, n_rows, d)


_LAYER_KEYS = ("w_mod", "b_mod", "g_mix", "g_ffn", "w_in", "w_out", "attn_sink", "na_rpb", "lam_q1", "lam_k1",
               "lam_q2", "lam_k2", "subln_g", "w_router", "router_bias", "w_gate", "w_up", "w_down",
               "ws_gate", "ws_up", "ws_down")


def kernel(x, c, ctx, c_ctx, w_mod, b_mod, g_mix, g_ffn, w_in, w_out, attn_sink, na_rpb, lam_q1, lam_k1, lam_q2,
           lam_k2, subln_g, w_router, router_bias, w_gate, w_up, w_down, ws_gate, ws_up, ws_down, g_final):
    stacked = dict(zip(_LAYER_KEYS, (w_mod, b_mod, g_mix, g_ffn, w_in, w_out, attn_sink, na_rpb, lam_q1, lam_k1,
                                     lam_q2, lam_k2, subln_g, w_router, router_bias, w_gate, w_up, w_down,
                                     ws_gate, ws_up, ws_down)))
    b, s_lat, d = x.shape
    c_len = ctx.shape[1]
    depth = w_mod.shape[0]
    assert s_lat % (NA_QROWS * GRID_W) == 0 and s_lat % c_len == 0 and c_len == TOK_TILE
    cs = jnp.zeros((16, d), F32).at[:b].set(c).at[b].set(c_ctx)
    cos, sin = _rope_tables(s_lat, c_len)
    xc = jnp.concatenate([x, ctx], axis=1)
    gfin = g_final.reshape(1, d)
    for i in range(depth):
        big = ("w_gate", "w_up", "w_down")
        lp = {k: (v if k in big else v[i]) for k, v in stacked.items()}
        lp["layer"] = i
        xc = _layer(xc, cs, cos, sin, lp, i, i == depth - 1, s_lat, gfin)
    return xc
```

```python
import functools
import math

import numpy as np
import jax
import jax.numpy as jnp
from jax import lax
from jax.experimental import pallas as pl
from jax.experimental.pallas import tpu as pltpu

F32 = jnp.float32
BF16 = jnp.bfloat16
I32 = jnp.int32
U32 = jnp.uint32

GRID_W = 64
HEAD_DIM = 64
ROPE_BASE = 10000.0
SWA_WINDOW = 128
NA_KH = 8
NA_KW = 16
N_HEADS = 4
N_EXPERTS = 256
TOP_K = 8
N_GROUPS = 8
TOPK_GROUPS = 4
ROUTED_SCALE = 2.5
MOE_BLOCK = 1024
ROW_CHUNK = 8
WAIT_GROUP = 16
COMBINE_KCHUNK = 512
CHUNK_UNROLL = 2
EPS = 1e-6
NEG_INF = -1e30
LOG2E = math.log2(math.e)
Q_COLS = 1024
IN_COLS = 2816
MIX_WIDTH = 1024

LANES = 128
TOK_TILE = 256
ATT_TQ = 256
NA_QROWS = 4
NA_KROWS = NA_QROWS + NA_KH - 1
DIFF_KCHUNK = 256
DIFF_HEADS_PER_STEP = 2
VMEM_LIMIT = 48 * 1024 * 1024


def _params(sem):
    return pltpu.CompilerParams(dimension_semantics=sem, vmem_limit_bytes=VMEM_LIMIT)


def _dot(a, b):
    return jnp.dot(a, b, preferred_element_type=F32)


def _dot_nt(a, b):
    return lax.dot_general(a, b, (((1,), (1,)), ((), ())), preferred_element_type=F32)


def _split_bf16(a):
    hi = a.astype(BF16)
    lo = (a - hi.astype(F32)).astype(BF16)
    return hi, lo


def _sigmoid(x):
    return 1.0 / (1.0 + jnp.exp(-x))


def _rms(x, g):
    return x * lax.rsqrt(jnp.mean(x * x, axis=-1, keepdims=True) + EPS) * g


def _norm_mod(x, g, shift, scale):
    return _rms(x, g) * (1.0 + scale) + shift


def _mod_kernel(c_ref, w_ref, b_ref, o_ref):
    c = c_ref[...]
    a_hi, a_lo = _split_bf16(c * _sigmoid(c))
    w_hi, w_lo = _split_bf16(w_ref[...])
    o_ref[...] = _dot(a_hi, w_hi) + _dot(a_hi, w_lo) + _dot(a_lo, w_hi) + b_ref[...]


def _modulation(cs, w_mod, b_mod):
    n, d = cs.shape
    cols = w_mod.shape[1]
    tn = 1536
    return pl.pallas_call(
        _mod_kernel,
        grid=(cols // tn,),
        in_specs=[pl.BlockSpec((n, d), lambda j: (0, 0)),
                  pl.BlockSpec((d, tn), lambda j: (0, j)),
                  pl.BlockSpec((1, tn), lambda j: (0, j))],
        out_specs=pl.BlockSpec((n, tn), lambda j: (0, j)),
        out_shape=jax.ShapeDtypeStruct((n, cols), F32),
        compiler_params=_params(("arbitrary",)),
        name="modulation",
    )(cs, w_mod, b_mod.reshape(1, cols))


def _proj_kernel(x_ref, g_ref, mod_ref, cos_ref, sin_ref, w_ref,
                 qa_ref, qb_ref, qd_ref, ka_ref, va_ref, kb_ref, vb_ref, kd_ref, vd_ref, vdt_ref, h_scr):
    h = _norm_mod(x_ref[0], g_ref[...], mod_ref[0, 0:1, :], mod_ref[0, 1:2, :])
    h_scr[...] = h.astype(BF16)
    cos = cos_ref[...]
    sin = sin_ref[...]
    tm = h.shape[0]
    lane = lax.broadcasted_iota(I32, (tm, LANES), 1)
    first16 = (lane & 16) == 0
    lo64 = lane < HEAD_DIM

    def rope(v):
        partner = jnp.where(first16, pltpu.roll(v, LANES - 16, 1), pltpu.roll(v, 16, 1))
        return v * cos + partner * sin

    def mm(c0):
        return _dot(h_scr[...], w_ref[:, c0:c0 + 2 * LANES])

    a = mm(0) * (HEAD_DIM ** -0.5)
    c0 = rope(a[:, :LANES])
    c1 = rope(a[:, LANES:])
    zero = jnp.zeros_like(c0)
    qa_ref[0, :, 0 * LANES:1 * LANES] = jnp.where(lo64, c0, zero).astype(BF16)
    qa_ref[0, :, 1 * LANES:2 * LANES] = jnp.where(lo64, pltpu.roll(c0, HEAD_DIM, 1), zero).astype(BF16)
    qa_ref[0, :, 2 * LANES:3 * LANES] = jnp.where(lo64, zero, pltpu.roll(c1, HEAD_DIM, 1)).astype(BF16)
    qa_ref[0, :, 3 * LANES:4 * LANES] = jnp.where(lo64, zero, c1).astype(BF16)
    qb_ref[0] = (mm(256) * (HEAD_DIM ** -0.5)).astype(BF16)
    for j in range(2):
        a = mm(512 + 256 * j) * (HEAD_DIM ** -0.5 * LOG2E)
        qd_ref[0, :, 256 * j:256 * j + LANES] = rope(a[:, :LANES]).astype(BF16)
        qd_ref[0, :, 256 * j + LANES:256 * (j + 1)] = rope(a[:, LANES:]).astype(BF16)
    a = mm(1024)
    ka_ref[0] = rope(a[:, :LANES]).astype(BF16)
    va_ref[0] = a[:, LANES:].astype(BF16)
    kb_ref[0] = mm(1280).astype(BF16)
    vb_ref[0] = mm(1536).astype(BF16)
    for j in range(2):
        a = mm(1792 + 256 * j)
        kd_ref[0, :, 256 * j:256 * j + LANES] = rope(a[:, :LANES]).astype(BF16)
        kd_ref[0, :, 256 * j + LANES:256 * (j + 1)] = rope(a[:, LANES:]).astype(BF16)
        a = mm(2304 + 256 * j)
        vd_ref[0, :, 256 * j:256 * (j + 1)] = a.astype(BF16)
        vdt_ref[0, 256 * j:256 * j + LANES, :] = a[:, :LANES].T.astype(BF16)
        vdt_ref[0, 256 * j + LANES:256 * (j + 1), :] = a[:, LANES:].T.astype(BF16)


def _projection(xc, g, modall, cos, sin, w_in, s_lat):
    b, sall, d = xc.shape
    tm = TOK_TILE
    nlat = s_lat // tm
    row = lambda width: pl.BlockSpec((1, tm, width), lambda i, j: (i, j, 0))
    widths = (512, 256, 512, 128, 128, 256, 256, 512, 512)
    return pl.pallas_call(
        _proj_kernel,
        grid=(b, sall // tm),
        in_specs=[row(d),
                  pl.BlockSpec((1, d), lambda i, j: (0, 0)),
                  pl.BlockSpec((1, 8, d), lambda i, j: (2 * i + (j >= nlat).astype(I32), 0, 0)),
                  pl.BlockSpec((tm, LANES), lambda i, j: (j, 0)),
                  pl.BlockSpec((tm, LANES), lambda i, j: (j, 0)),
                  pl.BlockSpec((d, IN_COLS), lambda i, j: (0, 0))],
        out_specs=[row(w) for w in widths] + [pl.BlockSpec((1, 512, tm), lambda i, j: (i, 0, j))],
        out_shape=([jax.ShapeDtypeStruct((b, sall, w), BF16) for w in widths]
                   + [jax.ShapeDtypeStruct((b, 512, sall), BF16)]),
        scratch_shapes=[pltpu.VMEM((tm, d), BF16)],
        compiler_params=_params(("arbitrary", "arbitrary")),
        name="projection",
    )(xc, g, modall, cos, sin, w_in)


def _half_mask(q, half):
    lane = lax.broadcasted_iota(I32, q.shape, 1)
    keep = (lane < HEAD_DIM) if half == 0 else (lane >= HEAD_DIM)
    return jnp.where(keep, q, jnp.zeros_like(q))


def _merge_halves(lo_part, hi_part):
    lane = lax.broadcasted_iota(I32, lo_part.shape, 1)
    return jnp.where(lane < HEAD_DIM, lo_part, hi_part)


def _gqa_rows(q_ref, g):
    return jnp.concatenate([q_ref[0, :, (2 * g) * LANES:(2 * g + 1) * LANES],
                            q_ref[0, :, (2 * g + 1) * LANES:(2 * g + 2) * LANES]], axis=0)


def _gqa_sink(sink_ref, g, tq):
    rowi = lax.broadcasted_iota(I32, (2 * tq, 1), 0)
    return jnp.where(rowi < tq, sink_ref[2 * g], sink_ref[2 * g + 1])


def _gqa_store(o_ref, g, o, tq):
    top, bot = o[:tq], o[tq:]
    if g == 0:
        chunk = _merge_halves(top, pltpu.roll(bot, HEAD_DIM, 1))
    else:
        chunk = _merge_halves(pltpu.roll(top, HEAD_DIM, 1), bot)
    o_ref[0, :, g * LANES:(g + 1) * LANES] = chunk.astype(BF16)


def _lam(lamv_ref, lam_init):
    v = lamv_ref[...]
    a = jnp.sum(v[0:1] * v[1:2], axis=-1, keepdims=True)
    b = jnp.sum(v[2:3] * v[3:4], axis=-1, keepdims=True)
    return jnp.exp(a) - jnp.exp(b) + lam_init


def _diff_post(o, subg, lam_init):
    return _rms(o, subg) * (1.0 - lam_init)


def _swa_kernel(sink_ref, q_ref, k_ref, v_ref, o_ref, *, s_lat, c_len):
    tq = ATT_TQ
    tk = tq + 2 * SWA_WINDOW
    q0 = pl.program_id(1) * tq
    ks = pl.multiple_of(jnp.clip(q0 - SWA_WINDOW, 0, s_lat - tk), SWA_WINDOW)
    kwin = k_ref[0, pl.ds(ks, tk), :]
    vwin = v_ref[0, pl.ds(ks, tk), :]
    kc = k_ref[0, s_lat:s_lat + c_len, :]
    vc = v_ref[0, s_lat:s_lat + c_len, :]
    qpos = q0 + lax.broadcasted_iota(I32, (tq, tk), 0)
    kpos = ks + lax.broadcasted_iota(I32, (tq, tk), 1)
    valid = jnp.abs(qpos - kpos) <= SWA_WINDOW
    valid = jnp.concatenate([valid, valid], axis=0)
    for g in range(2):
        q2 = _gqa_rows(q_ref, g)
        s_loc = jnp.where(valid, _dot_nt(q2, kwin), NEG_INF)
        s_ctx = _dot_nt(q2, kc)
        snk = _gqa_sink(sink_ref, g, tq)
        m = jnp.maximum(jnp.maximum(jnp.max(s_loc, axis=-1, keepdims=True),
                                    jnp.max(s_ctx, axis=-1, keepdims=True)), snk)
        e_loc = jnp.exp(s_loc - m)
        e_ctx = jnp.exp(s_ctx - m)
        den = (jnp.sum(e_loc, axis=-1, keepdims=True) + jnp.sum(e_ctx, axis=-1, keepdims=True)
               + jnp.exp(snk - m))
        o = (_dot(e_loc.astype(BF16), vwin) + _dot(e_ctx.astype(BF16), vc)) / den
        _gqa_store(o_ref, g, o, tq)


def _swa(sink, qa, ka, va, s_lat):
    b, sall, _ = qa.shape
    kv = pl.BlockSpec((1, sall, LANES), lambda i, j: (i, 0, 0))
    return pl.pallas_call(
        functools.partial(_swa_kernel, s_lat=s_lat, c_len=sall - s_lat),
        grid=(b, s_lat // ATT_TQ),
        in_specs=[pl.BlockSpec(memory_space=pltpu.SMEM),
                  pl.BlockSpec((1, ATT_TQ, 512), lambda i, j: (i, j, 0)), kv, kv],
        out_specs=pl.BlockSpec((1, ATT_TQ, 256), lambda i, j: (i, j, 0)),
        out_shape=jax.ShapeDtypeStruct((b, s_lat, 256), BF16),
        compiler_params=_params(("arbitrary", "arbitrary")),
        name="swa",
    )(sink, qa, ka, va)


def _na_kernel(q_ref, k_ref, v_ref, bm_ref, o_ref, *, s_lat, c_len):
    rows = s_lat // GRID_W
    nk = NA_KROWS * GRID_W
    r0 = pl.program_id(1) * NA_QROWS
    ks = pl.multiple_of(jnp.clip(r0 - NA_KH // 2, 0, rows - NA_KROWS) * GRID_W, GRID_W)
    for c in range(2):
        sl = slice(c * LANES, (c + 1) * LANES)
        kwin = k_ref[0, pl.ds(ks, nk), sl]
        vwin = v_ref[0, pl.ds(ks, nk), sl]
        kc = k_ref[0, s_lat:s_lat + c_len, sl]
        vc = v_ref[0, s_lat:s_lat + c_len, sl]
        q = q_ref[0, :, sl]
        outs = []
        for half in range(2):
            qm = _half_mask(q, half)
            s_loc = _dot_nt(qm, kwin) + bm_ref[0, 2 * c + half]
            s_ctx = _dot_nt(qm, kc)
            m = jnp.maximum(jnp.max(s_loc, axis=-1, keepdims=True), jnp.max(s_ctx, axis=-1, keepdims=True))
            e_loc = jnp.exp(s_loc - m)
            e_ctx = jnp.exp(s_ctx - m)
            den = jnp.sum(e_loc, axis=-1, keepdims=True) + jnp.sum(e_ctx, axis=-1, keepdims=True)
            outs.append((_dot(e_loc.astype(BF16), vwin) + _dot(e_ctx.astype(BF16), vc)) / den)
        o_ref[0, :, sl] = _merge_halves(outs[0], outs[1]).astype(BF16)


def _na_bias_mask(rpb, rows):
    nq = NA_QROWS * GRID_W
    nk = NA_KROWS * GRID_W
    col = np.arange(GRID_W)
    cstart = np.clip(col - NA_KW // 2, 0, GRID_W - NA_KW)
    col_ok = (col[None, :] >= cstart[:, None]) & (col[None, :] < cstart[:, None] + NA_KW)
    dc = np.clip(col[None, :] - col[:, None], -(NA_KW - 1), NA_KW - 1) + NA_KW - 1
    sel_c = np.where(col_ok[:, :, None], np.eye(2 * NA_KW - 1)[dc], 0.0)
    tabs = []
    for r0 in (0, NA_QROWS, rows - NA_QROWS):
        ksr = int(np.clip(r0 - NA_KH // 2, 0, rows - NA_KROWS))
        r = r0 + np.arange(NA_QROWS)
        kr = ksr + np.arange(NA_KROWS)
        rs = np.clip(r - NA_KH // 2, 0, rows - NA_KH)
        row_ok = (kr[None, :] >= rs[:, None]) & (kr[None, :] < rs[:, None] + NA_KH)
        dr = np.clip(kr[None, :] - r[:, None], -(NA_KH - 1), NA_KH - 1) + NA_KH - 1
        sel_r = np.where(row_ok[:, :, None], np.eye(2 * NA_KH - 1)[dr], 0.0)
        ok = (row_ok[:, None, :, None] & col_ok[None, :, None, :]).reshape(nq, nk)
        bias = jnp.einsum("hdc,rjd,qkc->hrqjk", rpb.astype(F32), jnp.asarray(sel_r, F32), jnp.asarray(sel_c, F32),
                          precision=lax.Precision.HIGHEST).reshape(N_HEADS, nq, nk)
        tabs.append(jnp.where(ok[None], bias, NEG_INF))
    return jnp.stack(tabs)


def _na(qb, kb, vb, bm, s_lat):
    b, sall, _ = qb.shape
    nq = NA_QROWS * GRID_W
    nsteps = s_lat // nq
    kv = pl.BlockSpec((1, sall, 256), lambda i, j: (i, 0, 0))

    def bm_map(i, j):
        return (jnp.where(j == 0, 0, jnp.where(j == nsteps - 1, 2, 1)), 0, 0, 0)

    return pl.pallas_call(
        functools.partial(_na_kernel, s_lat=s_lat, c_len=sall - s_lat),
        grid=(b, nsteps),
        in_specs=[pl.BlockSpec((1, nq, 256), lambda i, j: (i, j, 0)), kv, kv,
                  pl.BlockSpec((1, N_HEADS, nq, NA_KROWS * GRID_W), bm_map)],
        out_specs=pl.BlockSpec((1, nq, 256), lambda i, j: (i, j, 0)),
        out_shape=jax.ShapeDtypeStruct((b, s_lat, 256), BF16),
        compiler_params=_params(("arbitrary", "arbitrary")),
        name="na2d",
    )(qb, kb, vb, bm)


def _diff_kernel(lamv_ref, subg_ref, q_ref, k_ref, vt_ref, o_ref, e_scr, m_scr, *, n_keys, lam_init):
    tq = ATT_TQ
    chunks = [(c0, min(DIFF_KCHUNK, n_keys - c0)) for c0 in range(0, n_keys, DIFF_KCHUNK)]
    lam = _lam(lamv_ref, lam_init)
    stats = []
    for hh in range(DIFF_HEADS_PER_STEP):
        sl = slice(hh * LANES, (hh + 1) * LANES)
        q = q_ref[0, :, sl]
        q12 = jnp.concatenate([_half_mask(q, 0), _half_mask(q, 1)], axis=0)
        m = jnp.full((1, 2 * tq), NEG_INF, F32)
        den = jnp.zeros((1, 2 * tq), F32)
        for i, (c0, n) in enumerate(chunks):
            s = _dot_nt(k_ref[0, c0:c0 + n, sl], q12)
            m_new = jnp.maximum(m, jnp.max(s, axis=0, keepdims=True))
            e = jnp.exp2(s - m_new)
            e_scr[hh, c0:c0 + n, :] = e
            m_scr[hh, i:i + 1, :] = m_new
            den = den * jnp.exp2(m - m_new) + jnp.sum(e, axis=0, keepdims=True)
            m = m_new
        stats.append((m, den))
    for hh in range(DIFF_HEADS_PER_STEP):
        sl = slice(hh * LANES, (hh + 1) * LANES)
        m, den = stats[hh]
        acc = jnp.zeros((LANES, tq), F32)
        for i, (c0, n) in enumerate(chunks):
            f = jnp.exp2(m_scr[hh, i:i + 1, :] - m) / den
            p = e_scr[hh, c0:c0 + n, :tq] * f[:, :tq] - e_scr[hh, c0:c0 + n, tq:] * (lam * f[:, tq:])
            acc = acc + _dot(vt_ref[0, sl, c0:c0 + n], p.astype(BF16))
        o_ref[0, :, sl] = _diff_post(acc.T, subg_ref[...], lam_init).astype(BF16)


def _diff(lamv, subg, qd, kd, vdt, s_lat, lam_init):
    b, sall, _ = qd.shape
    nchunk = -(-sall // DIFF_KCHUNK)
    hps = DIFF_HEADS_PER_STEP
    wid = hps * LANES
    return pl.pallas_call(
        functools.partial(_diff_kernel, n_keys=sall, lam_init=lam_init),
        grid=(b, N_HEADS // hps, s_lat // ATT_TQ),
        in_specs=[pl.BlockSpec((8, LANES), lambda i, h, j: (0, 0)),
                  pl.BlockSpec((1, LANES), lambda i, h, j: (0, 0)),
                  pl.BlockSpec((1, ATT_TQ, wid), lambda i, h, j: (i, j, h)),
                  pl.BlockSpec((1, sall, wid), lambda i, h, j: (i, 0, h)),
                  pl.BlockSpec((1, wid, sall), lambda i, h, j: (i, h, 0))],
        out_specs=pl.BlockSpec((1, ATT_TQ, wid), lambda i, h, j: (i, j, h)),
        out_shape=jax.ShapeDtypeStruct((b, s_lat, 512), BF16),
        scratch_shapes=[pltpu.VMEM((hps, sall, 2 * ATT_TQ), F32),
                        pltpu.VMEM((hps, -(-nchunk // 8) * 8, 2 * ATT_TQ), F32)],
        compiler_params=_params(("arbitrary", "arbitrary", "arbitrary")),
        name="diff_attn",
    )(lamv, subg, qd, kd, vdt)


def _ctx_kernel(sink_ref, lamv_ref, subg_ref, qa_ref, ka_ref, va_ref, qb_ref, kb_ref, vb_ref,
                qd_ref, kd_ref, vd_ref, oa_ref, ob_ref, od_ref, *, lam_init):
    cl = qa_ref.shape[1]
    ka = ka_ref[0]
    va = va_ref[0]
    for g in range(2):
        q2 = _gqa_rows(qa_ref, g)
        s = _dot_nt(q2, ka)
        snk = _gqa_sink(sink_ref, g, cl)
        m = jnp.maximum(jnp.max(s, axis=-1, keepdims=True), snk)
        e = jnp.exp(s - m)
        den = jnp.sum(e, axis=-1, keepdims=True) + jnp.exp(snk - m)
        _gqa_store(oa_ref, g, _dot(e.astype(BF16), va) / den, cl)
    for c in range(2):
        sl = slice(c * LANES, (c + 1) * LANES)
        outs = []
        for half in range(2):
            s = _dot_nt(_half_mask(qb_ref[0, :, sl], half), kb_ref[0, :, sl])
            e = jnp.exp(s - jnp.max(s, axis=-1, keepdims=True))
            outs.append(_dot(e.astype(BF16), vb_ref[0, :, sl]) / jnp.sum(e, axis=-1, keepdims=True))
        ob_ref[0, :, sl] = _merge_halves(outs[0], outs[1]).astype(BF16)
    lam = _lam(lamv_ref, lam_init)
    for h in range(N_HEADS):
        sl = slice(h * LANES, (h + 1) * LANES)
        q = qd_ref[0, :, sl]
        q12 = jnp.concatenate([_half_mask(q, 0), _half_mask(q, 1)], axis=0)
        s = _dot_nt(q12, kd_ref[0, :, sl])
        e = jnp.exp2(s - jnp.max(s, axis=-1, keepdims=True))
        den = jnp.sum(e, axis=-1, keepdims=True)
        p = e[:cl] * (1.0 / den[:cl]) - e[cl:] * (lam / den[cl:])
        o = _dot(p.astype(BF16), vd_ref[0, :, sl])
        od_ref[0, :, sl] = _diff_post(o, subg_ref[...], lam_init).astype(BF16)


def _ctx_attention(sink, lamv, subg, qa, ka, va, qb, kb, vb, qd, kd, vd, s_lat, lam_init):
    b, sall, _ = qa.shape
    cl = sall - s_lat
    blk = s_lat // cl
    row = lambda width: pl.BlockSpec((1, cl, width), lambda i: (i, blk, 0))
    out = lambda width: pl.BlockSpec((1, cl, width), lambda i: (i, 0, 0))
    return pl.pallas_call(
        functools.partial(_ctx_kernel, lam_init=lam_init),
        grid=(b,),
        in_specs=[pl.BlockSpec(memory_space=pltpu.SMEM),
                  pl.BlockSpec((8, LANES), lambda i: (0, 0)),
                  pl.BlockSpec((1, LANES), lambda i: (0, 0)),
                  row(512), row(128), row(128), row(256), row(256), row(256), row(512), row(512), row(512)],
        out_specs=[out(256), out(256), out(512)],
        out_shape=[jax.ShapeDtypeStruct((b, cl, w), BF16) for w in (256, 256, 512)],
        compiler_params=_params(("arbitrary",)),
        name="ctx_attention",
    )(sink, lamv, subg, qa, ka, va, qb, kb, vb, qd, kd, vd)


def _out_kernel(x_ref, w_ref, mod_ref, *refs, nlat):
    xo_ref = refs[-1]

    def emit(oa_ref, ob_ref, od_ref):
        attn = (_dot(oa_ref[0], w_ref[0:256, :]) + _dot(ob_ref[0], w_ref[256:512, :])
                + _dot(od_ref[0], w_ref[512:1024, :]))
        xo_ref[0] = x_ref[0] + mod_ref[0, 2:3, :] * attn

    if len(refs) == 4:
        emit(*refs[:3])
    else:
        is_ctx = pl.program_id(1) >= nlat
        pl.when(jnp.logical_not(is_ctx))(lambda: emit(*refs[:3]))
        pl.when(is_ctx)(lambda: emit(*refs[3:6]))


def _out_projection(xc, lat, ctx, w_out, modall, s_lat):
    b, sall, d = xc.shape
    tm = TOK_TILE
    nlat = s_lat // tm
    n_rows = s_lat if ctx is None else sall
    row = lambda width: pl.BlockSpec((1, tm, width), lambda i, j: (i, j, 0))
    lat_row = lambda width: pl.BlockSpec((1, tm, width), lambda i, j: (i, jnp.minimum(j, nlat - 1), 0))
    ctx_row = lambda width: pl.BlockSpec((1, tm, width), lambda i, j: (i, 0, 0))
    widths = (256, 256, 512)
    specs = [lat_row(w) for w in widths] + ([] if ctx is None else [ctx_row(w) for w in widths])
    return pl.pallas_call(
        functools.partial(_out_kernel, nlat=nlat),
        grid=(b, n_rows // tm),
        in_specs=[row(d),
                  pl.BlockSpec((MIX_WIDTH, d), lambda i, j: (0, 0)),
                  pl.BlockSpec((1, 8, d), lambda i, j: (2 * i + (j >= nlat).astype(I32), 0, 0))] + specs,
        out_specs=row(d),
        out_shape=jax.ShapeDtypeStruct((b, n_rows, d), F32),
        compiler_params=_params(("arbitrary", "arbitrary")),
        name="out_projection",
    )(xc, w_out, modall, *lat, *(() if ctx is None else ctx))


def _first_argmax(v, iota, n):
    m = jnp.max(v, axis=0, keepdims=True)
    ix = jnp.min(jnp.where(v == m, iota, float(n)), axis=0, keepdims=True)
    return m, ix


def _router_kernel(x_ref, g_ref, mod_ref, whi_ref, wlo_ref, bias_ref, tri_ref, ltri_ref,
                   w_ref, slot_ref, meta_ref, cnt_ref):
    @pl.when(pl.program_id(0) == 0)
    def _():
        cnt_ref[...] = jnp.zeros_like(cnt_ref)

    hf = _norm_mod(x_ref[...], g_ref[...], mod_ref[0, 3:4, :], mod_ref[0, 4:5, :])
    tm = hf.shape[0]
    h_hi, h_lo = _split_bf16(hf)
    whi = whi_ref[...]
    logits = _dot_nt(whi, h_hi) + _dot_nt(whi, h_lo) + _dot_nt(wlo_ref[...], h_hi)
    scores = _sigmoid(logits)
    biased = scores + bias_ref[...]
    gsz = N_EXPERTS // N_GROUPS
    iota_g = lax.broadcasted_iota(I32, (gsz, tm), 0).astype(F32)
    gscore = []
    for g in range(N_GROUPS):
        v = biased[g * gsz:(g + 1) * gsz]
        m1, i1 = _first_argmax(v, iota_g, gsz)
        m2 = jnp.max(jnp.where(iota_g == i1, -jnp.inf, v), axis=0, keepdims=True)
        gscore.append(m1 + m2)
    cur = jnp.concatenate(gscore, axis=0)
    iota_n = lax.broadcasted_iota(I32, (N_GROUPS, tm), 0).astype(F32)
    gsel = jnp.zeros((N_GROUPS, tm), F32)
    for _ in range(TOPK_GROUPS):
        _, ix = _first_argmax(cur, iota_n, N_GROUPS)
        hit = iota_n == ix
        gsel = jnp.where(hit, 1.0, gsel)
        cur = jnp.where(hit, -jnp.inf, cur)
    masked = jnp.concatenate(
        [jnp.where(gsel[g:g + 1] > 0.5, biased[g * gsz:(g + 1) * gsz], -jnp.inf) for g in range(N_GROUPS)],
        axis=0)
    iota_e = lax.broadcasted_iota(I32, (N_EXPERTS, tm), 0).astype(F32)
    onehot = jnp.zeros((N_EXPERTS, tm), F32)
    idxs, ws = [], []
    for _ in range(TOP_K):
        _, ix = _first_argmax(masked, iota_e, N_EXPERTS)
        hit = iota_e == ix
        ws.append(jnp.sum(jnp.where(hit, scores, 0.0), axis=0, keepdims=True))
        masked = jnp.where(hit, -jnp.inf, masked)
        onehot = jnp.where(hit, 1.0, onehot)
        idxs.append(ix)
    base = cnt_ref[...]
    rank = _dot(onehot.astype(BF16), tri_ref[...])
    n_col = jnp.sum(onehot, axis=1, keepdims=True)
    n_pad = jnp.floor((n_col + (ROW_CHUNK - 1)) * (1.0 / ROW_CHUNK)) * ROW_CHUNK
    start = _dot(ltri_ref[...], jnp.broadcast_to(n_pad, (N_EXPERTS, LANES)).astype(BF16))
    wsum = ws[0]
    for wk in ws[1:]:
        wsum = wsum + wk
    for k in range(TOP_K):
        slot = jnp.sum(jnp.where(iota_e == idxs[k], rank + start[:, 0:1], 0.0), axis=0, keepdims=True)
        slot_ref[0, k:k + 1, :] = slot.astype(I32)
        w_ref[0, k:k + 1, :] = ws[k] / wsum * ROUTED_SCALE
    lane = lax.broadcasted_iota(I32, (N_EXPERTS, LANES), 1)
    meta_ref[0] = jnp.where(lane == 0, n_pad, jnp.where(lane == 1, base, start))
    cnt_ref[...] = base + n_pad


def _router(xflat, g, modall, whi, wlo, bias, tri, ltri, tiles_per_batch, nlat):
    t, d = xflat.shape
    tm = TOK_TILE
    nt = t // tm
    per_tok = pl.BlockSpec((1, TOP_K, tm), lambda i: (i, 0, 0))

    def mod_map(i):
        return (2 * (i // tiles_per_batch) + ((i % tiles_per_batch) >= nlat).astype(I32), 0, 0)

    return pl.pallas_call(
        _router_kernel,
        grid=(nt,),
        in_specs=[pl.BlockSpec((tm, d), lambda i: (i, 0)),
                  pl.BlockSpec((1, d), lambda i: (0, 0)),
                  pl.BlockSpec((1, 8, d), mod_map),
                  pl.BlockSpec((N_EXPERTS, d), lambda i: (0, 0)),
                  pl.BlockSpec((N_EXPERTS, d), lambda i: (0, 0)),
                  pl.BlockSpec((N_EXPERTS, 1), lambda i: (0, 0)),
                  pl.BlockSpec((tm, tm), lambda i: (0, 0)),
                  pl.BlockSpec((N_EXPERTS, N_EXPERTS), lambda i: (0, 0))],
        out_specs=[per_tok, per_tok,
                   pl.BlockSpec((1, N_EXPERTS, LANES), lambda i: (i, 0, 0)),
                   pl.BlockSpec((N_EXPERTS, 1), lambda i: (0, 0))],
        out_shape=[jax.ShapeDtypeStruct((nt, TOP_K, tm), F32),
                   jax.ShapeDtypeStruct((nt, TOP_K, tm), I32),
                   jax.ShapeDtypeStruct((nt, N_EXPERTS, LANES), F32),
                   jax.ShapeDtypeStruct((N_EXPERTS, 1), F32)],
        compiler_params=_params(("arbitrary",)),
        name="router",
    )(xflat, g, modall, whi, wlo, bias, tri, ltri)


def _pack_bf16_pairs(v, is_bf16_valued=False):
    half = v.shape[1] // 2
    if not is_bf16_valued:
        v = v.astype(BF16).astype(F32)
    lo = lax.shift_right_logical(lax.bitcast_convert_type(v[:, :half], U32), jnp.uint32(16))
    hi = lax.bitcast_convert_type(v[:, half:], U32) & jnp.uint32(0xFFFF0000)
    return lo | hi


def _unpack_bf16_pairs(u):
    lo = lax.bitcast_convert_type(lax.shift_left(u, jnp.uint32(16)), F32).astype(BF16)
    hi = lax.bitcast_convert_type(u & jnp.uint32(0xFFFF0000), F32).astype(BF16)
    return lo, hi


def _used_chunks(meta_ref):
    return lax.shift_right_logical(_slots_used(meta_ref), ROW_CHUNK.bit_length() - 1)


def _start_chunk_copies(first, count, make_copy):
    pairs = lax.shift_right_logical(count, 1)

    def start_pair(j, carry):
        make_copy(first + 2 * j).start()
        make_copy(first + 2 * j + 1).start(priority=1)
        return carry

    lax.fori_loop(0, pairs, start_pair, 0)

    @pl.when(count > 2 * pairs)
    def _():
        make_copy(first + count - 1).start()


def _wait_chunk_copies(total, copy_of_rows):
    group = lax.shift_right_logical(total, WAIT_GROUP.bit_length() - 1)

    def wait_group(j, carry):
        copy_of_rows(WAIT_GROUP * ROW_CHUNK).wait()
        return carry

    def wait_one(j, carry):
        copy_of_rows(ROW_CHUNK).wait()
        return carry

    lax.fori_loop(0, group, wait_group, 0)
    lax.fori_loop(0, total - group * WAIT_GROUP, wait_one, 0)


def _tile_slots(tm):
    return TOP_K * tm + ROW_CHUNK * N_EXPERTS


def _slots_used(meta_ref):
    last = N_EXPERTS - 1
    return meta_ref[0, 2, last] + meta_ref[0, 0, last]


def _dispatch_kernel(meta_ref, rows_ref, x_ref, g_ref, mod_ref, slot_ref, xs_ref, xc_scr, sem):
    hb = _norm_mod(x_ref[...], g_ref[...], mod_ref[0, 3:4, :], mod_ref[0, 4:5, :]).astype(BF16)
    tm = hb.shape[0]
    slot = slot_ref[0]

    total = _used_chunks(meta_ref)
    step = CHUNK_UNROLL * tm
    chunks_per_step = step // ROW_CHUNK

    def make_copy(j):
        src = pl.multiple_of(j * ROW_CHUNK, ROW_CHUNK)
        dst = pl.multiple_of(rows_ref[0, 0, j], ROW_CHUNK)
        return pltpu.make_async_copy(xc_scr.at[pl.ds(src, ROW_CHUNK)], xs_ref.at[pl.ds(dst, ROW_CHUNK)], sem)

    def select_rows(r2, carry):
        for u in range(CHUNK_UNROLL):
            r0 = pl.multiple_of((r2 * CHUNK_UNROLL + u) * tm, tm)
            row = lax.broadcasted_iota(I32, (tm, tm), 0) + r0
            sel = jnp.zeros((tm, tm), F32)
            for k in range(TOP_K):
                sel = jnp.where(row == slot[k:k + 1, :], 1.0, sel)
            xc_scr[pl.ds(r0, tm), :] = _pack_bf16_pairs(_dot(sel.astype(BF16), hb), is_bf16_valued=True)
        first = r2 * chunks_per_step
        _start_chunk_copies(first, jnp.clip(total - first, 0, chunks_per_step), make_copy)
        return carry

    lax.fori_loop(0, (_slots_used(meta_ref) + step - 1) // step, select_rows, 0)
    _wait_chunk_copies(total, lambda n: pltpu.make_async_copy(xc_scr.at[pl.ds(0, n)], xs_ref.at[pl.ds(0, n)], sem))


def _dispatch(meta, chunk_rows, slot, xflat, g, modall, n_rows, tiles_per_batch, nlat):
    t, d = xflat.shape
    tm = TOK_TILE
    nt = t // tm

    def mod_map(i):
        return (2 * (i // tiles_per_batch) + ((i % tiles_per_batch) >= nlat).astype(I32), 0, 0)

    return pl.pallas_call(
        _dispatch_kernel,
        grid=(nt,),
        in_specs=[pl.BlockSpec((1, 3, N_EXPERTS), lambda i: (i, 0, 0), memory_space=pltpu.SMEM),
                  pl.BlockSpec((1, 1, chunk_rows.shape[2]), lambda i: (i, 0, 0), memory_space=pltpu.SMEM),
                  pl.BlockSpec((tm, d), lambda i: (i, 0)),
                  pl.BlockSpec((1, d), lambda i: (0, 0)),
                  pl.BlockSpec((1, 8, d), mod_map),
                  pl.BlockSpec((1, TOP_K, tm), lambda i: (i, 0, 0))],
        out_specs=pl.BlockSpec(memory_space=pl.ANY),
        out_shape=jax.ShapeDtypeStruct((n_rows, d // 2), U32),
        scratch_shapes=[pltpu.VMEM((_tile_slots(tm), d // 2), U32), pltpu.SemaphoreType.DMA(())],
        compiler_params=_params(("arbitrary",)),
        name="moe_dispatch",
    )(meta, chunk_rows, xflat, g, modall, slot)


def _expert_kernel(blk_e_ref, nvalid_ref, nact_ref, next_e_ref, has_next_ref, parity_ref,
                   xs_ref, wg_any, wu_any, wd_any, ys_ref, wg_s, wu_s, wd_s, wg_buf, wu_buf, wd_buf, sem, *, layer):
    b = pl.program_id(0)
    e = blk_e_ref[b]
    changed = jnp.logical_or(b == 0, e != blk_e_ref[jnp.maximum(b - 1, 0)])
    par = parity_ref[b]

    def weight_copies(expert, p):
        return [pltpu.make_async_copy(src.at[layer, expert], dst.at[p], sem.at[p])
                for src, dst in ((wg_any, wg_buf), (wu_any, wu_buf), (wd_any, wd_buf))]

    @pl.when(b == 0)
    def _():
        for c in weight_copies(e, par):
            c.start()

    @pl.when(changed)
    def _():
        for c in weight_copies(e, par):
            c.wait()
        wg_s[...] = wg_buf[par].astype(BF16)
        wu_s[...] = wu_buf[par].astype(BF16)
        wd_s[...] = wd_buf[par].astype(BF16)

        @pl.when(has_next_ref[b] == 1)
        def _():
            for c in weight_copies(next_e_ref[b], 1 - par):
                c.start()

    half = xs_ref.shape[1]
    n_here = jnp.where(b < nact_ref[0], nvalid_ref[b], 0)

    @pl.when(n_here > 0)
    def _():
        row = lax.broadcasted_iota(I32, xs_ref.shape, 0)
        xs = jnp.where(row < n_here, xs_ref[...], jnp.zeros(xs_ref.shape, U32))
        x_lo, x_hi = _unpack_bf16_pairs(xs)
        gate = _dot(x_lo, wg_s[0:half, :]) + _dot(x_hi, wg_s[half:, :])
        up = _dot(x_lo, wu_s[0:half, :]) + _dot(x_hi, wu_s[half:, :])
        mid = (gate * _sigmoid(gate)) * up
        ys_ref[...] = _pack_bf16_pairs(_dot(mid.astype(BF16), wd_s[...]))

    @pl.when(n_here == 0)
    def _():
        ys_ref[...] = jnp.zeros_like(ys_ref)


def _experts(blk_e, nvalid, nact, xs, w_gate, w_up, w_down, layer):
    n_rows, half = xs.shape
    _, _, d, f = w_gate.shape
    n_blk = n_rows // MOE_BLOCK
    first_of_next = jnp.sum((blk_e[None, :] <= blk_e[:, None]).astype(I32), axis=1)
    has_next = (first_of_next < n_blk).astype(I32)
    next_e = blk_e[jnp.minimum(first_of_next, n_blk - 1)]
    starts = jnp.concatenate([jnp.zeros((1,), I32), (blk_e[1:] != blk_e[:-1]).astype(I32)])
    parity = jnp.cumsum(starts) % 2
    anyspec = pl.BlockSpec(memory_space=pl.ANY)
    grid_spec = pltpu.PrefetchScalarGridSpec(
        num_scalar_prefetch=6,
        grid=(n_blk,),
        in_specs=[pl.BlockSpec((MOE_BLOCK, half), lambda i, be, nv, na, *_: (jnp.minimum(i, na[0] - 1), 0)),
                  anyspec, anyspec, anyspec],
        out_specs=pl.BlockSpec((MOE_BLOCK, half), lambda i, be, nv, na, *_: (jnp.minimum(i, na[0]), 0)),
        scratch_shapes=[pltpu.VMEM((d, f), BF16), pltpu.VMEM((d, f), BF16), pltpu.VMEM((f, d), BF16),
                        pltpu.VMEM((2, d, f), F32), pltpu.VMEM((2, d, f), F32), pltpu.VMEM((2, f, d), F32),
                        pltpu.SemaphoreType.DMA((2,))],
    )
    return pl.pallas_call(
        functools.partial(_expert_kernel, layer=layer),
        grid_spec=grid_spec,
        out_shape=jax.ShapeDtypeStruct((n_rows, half), U32),
        compiler_params=_params(("arbitrary",)),
        name="moe_experts",
    )(blk_e, nvalid, nact, next_e, has_next, parity.astype(I32), xs, w_gate, w_up, w_down)


def _combine_kernel(meta_ref, rows_ref, x_ref, g_ref, mod_ref, spos_ref, wt_ref, wsg_ref, wsu_ref, wsd_ref,
                    gfin_ref, ys_ref, o_ref, y_scr, acc_scr, sb_scr, wb_scr, sem, *, final_norm):
    tm, d = x_ref.shape
    half = d // 2

    @pl.when(pl.program_id(0) == 0)
    def _():
        y_scr[...] = jnp.zeros_like(y_scr)

    step = CHUNK_UNROLL * COMBINE_KCHUNK
    chunks_per_step = step // ROW_CHUNK
    group_shift = chunks_per_step.bit_length() - 1

    def make_copy(j):
        src = pl.multiple_of(rows_ref[0, 0, j], ROW_CHUNK)
        dst = pl.multiple_of(j * ROW_CHUNK, ROW_CHUNK)
        return pltpu.make_async_copy(ys_ref.at[pl.ds(src, ROW_CHUNK)], y_scr.at[pl.ds(dst, ROW_CHUNK)],
                                     sem.at[lax.shift_right_logical(j, group_shift)])

    total = _used_chunks(meta_ref)
    _start_chunk_copies(0, total, make_copy)

    x = x_ref[...]
    hb = _norm_mod(x, g_ref[...], mod_ref[0, 3:4, :], mod_ref[0, 4:5, :]).astype(BF16)
    gate = _dot(hb, wsg_ref[...])
    mid = (gate * _sigmoid(gate)) * _dot(hb, wsu_ref[...])
    acc_scr[...] = _dot(mid.astype(BF16), wsd_ref[...])

    n_used = _slots_used(meta_ref)
    for k in range(TOP_K):
        sb_scr[k] = jnp.broadcast_to(spos_ref[0, :, k:k + 1], (tm, LANES))
        wb_scr[k] = jnp.broadcast_to(wt_ref[0, :, k:k + 1], (tm, LANES))

    def chunk(j2, carry):
        _wait_chunk_copies(jnp.clip(total - j2 * chunks_per_step, 0, chunks_per_step),
                           lambda n: pltpu.make_async_copy(ys_ref.at[pl.ds(0, n)], y_scr.at[pl.ds(0, n)], sem.at[j2]))
        lo_sum = jnp.zeros((tm, half), F32)
        hi_sum = jnp.zeros((tm, half), F32)
        for u in range(CHUNK_UNROLL):
            r0 = pl.multiple_of((j2 * CHUNK_UNROLL + u) * COMBINE_KCHUNK, COMBINE_KCHUNK)
            parts = []
            for c in range(COMBINE_KCHUNK // LANES):
                col = lax.broadcasted_iota(I32, (tm, LANES), 1) + (r0 + c * LANES)
                p = jnp.zeros((tm, LANES), F32)
                for k in range(TOP_K):
                    p = jnp.where(col == sb_scr[k], wb_scr[k], p)
                parts.append(p.astype(BF16))
            pb = jnp.concatenate(parts, axis=1)
            y_lo, y_hi = _unpack_bf16_pairs(y_scr[pl.ds(r0, COMBINE_KCHUNK), :])
            lo_sum = lo_sum + _dot(pb, y_lo)
            hi_sum = hi_sum + _dot(pb, y_hi)
        acc_scr[:, :half] += lo_sum
        acc_scr[:, half:] += hi_sum
        return carry

    lax.fori_loop(0, (n_used + step - 1) // step, chunk, 0)
    out = x + mod_ref[0, 5:6, :] * acc_scr[...]
    if final_norm:
        out = _rms(out, gfin_ref[...])
    o_ref[...] = out


def _combine(meta, chunk_rows, spos_t, wt, xflat, g, modall, wsg, wsu, wsd, gfin, ys, tiles_per_batch, nlat,
             final_norm):
    t, d = xflat.shape
    f = wsg.shape[1]
    tm = TOK_TILE
    nt = t // tm
    y_rows = _tile_slots(tm)
    assert y_rows % COMBINE_KCHUNK == 0

    def mod_map(i):
        return (2 * (i // tiles_per_batch) + ((i % tiles_per_batch) >= nlat).astype(I32), 0, 0)

    return pl.pallas_call(
        functools.partial(_combine_kernel, final_norm=final_norm),
        grid=(nt,),
        in_specs=[pl.BlockSpec((1, 3, N_EXPERTS), lambda i: (i, 0, 0), memory_space=pltpu.SMEM),
                  pl.BlockSpec((1, 1, chunk_rows.shape[2]), lambda i: (i, 0, 0), memory_space=pltpu.SMEM),
                  pl.BlockSpec((tm, d), lambda i: (i, 0)),
                  pl.BlockSpec((1, d), lambda i: (0, 0)),
                  pl.BlockSpec((1, 8, d), mod_map),
                  pl.BlockSpec((1, tm, TOP_K), lambda i: (i, 0, 0)),
                  pl.BlockSpec((1, tm, TOP_K), lambda i: (i, 0, 0)),
                  pl.BlockSpec((d, f), lambda i: (0, 0)),
                  pl.BlockSpec((d, f), lambda i: (0, 0)),
                  pl.BlockSpec((f, d), lambda i: (0, 0)),
                  pl.BlockSpec((1, d), lambda i: (0, 0)),
                  pl.BlockSpec(memory_space=pl.ANY)],
        out_specs=pl.BlockSpec((tm, d), lambda i: (i, 0)),
        out_shape=jax.ShapeDtypeStruct((t, d), F32),
        scratch_shapes=[pltpu.VMEM((y_rows, d // 2), U32), pltpu.VMEM((tm, d), F32),
                        pltpu.VMEM((TOP_K, tm, LANES), I32), pltpu.VMEM((TOP_K, tm, LANES), F32),
                        pltpu.SemaphoreType.DMA((y_rows // (CHUNK_UNROLL * COMBINE_KCHUNK),))],
        compiler_params=_params(("arbitrary",)),
        name="moe_combine",
    )(meta, chunk_rows, xflat, g, modall, spos_t, wt, wsg, wsu, wsd, gfin, ys)


def _rope_tables(s_lat, c_len):
    n_freq = HEAD_DIM // 4
    inv = 1.0 / (ROPE_BASE ** (jnp.arange(n_freq, dtype=F32) / n_freq))
    t = jnp.arange(s_lat)
    pos = jnp.stack([(t // GRID_W).astype(F32), (t % GRID_W).astype(F32)], axis=1)
    lane = np.arange(LANES)
    ang = pos[:, (lane % HEAD_DIM) // 32] * inv[lane % 16][None, :]
    sign = jnp.asarray(np.where(lane % 32 < 16, -1.0, 1.0), F32)
    cos = jnp.concatenate([jnp.cos(ang), jnp.ones((c_len, LANES), F32)], axis=0)
    sin = jnp.concatenate([jnp.sin(ang) * sign, jnp.zeros((c_len, LANES), F32)], axis=0)
    return cos, sin


def _moe(xflat, lp, modall, gfin, tiles_per_batch, nlat, final_norm):
    t, d = xflat.shape
    nt = t // TOK_TILE
    g = lp["g_ffn"].reshape(1, d)
    w_r = lp["w_router"].T
    whi = w_r.astype(BF16)
    wlo = (w_r - whi.astype(F32)).astype(BF16)
    tri = jnp.asarray(np.triu(np.ones((TOK_TILE, TOK_TILE), np.float32), 1), BF16)
    ltri = jnp.asarray(np.tril(np.ones((N_EXPERTS, N_EXPERTS), np.float32), -1), BF16)
    w, slot, meta, cnt = _router(xflat, g, modall, whi, wlo, lp["router_bias"].reshape(N_EXPERTS, 1),
                                 tri, ltri, tiles_per_batch, nlat)
    meta = jnp.transpose(meta[:, :, :3].astype(I32), (0, 2, 1))
    rows_e = cnt[:, 0].astype(I32)
    seg = (rows_e + MOE_BLOCK - 1) // MOE_BLOCK * MOE_BLOCK
    pend = jnp.cumsum(seg)
    pstart = (pend - seg).astype(I32)
    n_blk = (t * TOP_K + ROW_CHUNK * N_EXPERTS * nt) // MOE_BLOCK + N_EXPERTS
    blk_row0 = jnp.arange(n_blk, dtype=I32) * MOE_BLOCK
    blk_e = jnp.minimum(jnp.sum((pend[None, :] <= blk_row0[:, None]).astype(I32), axis=1), N_EXPERTS - 1)
    of_blk = blk_e[:, None] == jnp.arange(N_EXPERTS, dtype=I32)[None, :]
    seg_end = jnp.sum(jnp.where(of_blk, (pstart + rows_e)[None, :], 0), axis=1)
    nvalid = jnp.clip(seg_end - blk_row0, 0, MOE_BLOCK).astype(I32)
    nact = (pend[-1] // MOE_BLOCK).astype(I32).reshape(1)
    n_pad, base, start = meta[:, 0, :], meta[:, 1, :], meta[:, 2, :]
    chunk_slot = jnp.arange(_tile_slots(TOK_TILE) // ROW_CHUNK, dtype=I32) * ROW_CHUNK
    s = chunk_slot[None, :, None]
    in_run = (start[:, None, :] <= s) & (s < (start + n_pad)[:, None, :])
    row0 = pstart[None, :] + base - start
    chunk_rows = (jnp.sum(jnp.where(in_run, row0[:, None, :], 0), axis=2) + chunk_slot[None, :])[:, None, :]
    xs = _dispatch(meta, chunk_rows, slot, xflat, g, modall, n_blk * MOE_BLOCK, tiles_per_batch, nlat)
    ys = _experts(blk_e, nvalid, nact, xs, lp["w_gate"], lp["w_up"], lp["w_down"], lp["layer"])
    return _combine(meta, chunk_rows, jnp.transpose(slot, (0, 2, 1)), jnp.transpose(w, (0, 2, 1)), xflat, g, modall,
                    lp["ws_gate"].astype(BF16), lp["ws_up"].astype(BF16), lp["ws_down"].astype(BF16), gfin, ys,
                    tiles_per_batch, nlat, final_norm)


def _layer(xc, cs, cos, sin, lp, layer_idx, last, s_lat, gfin):
    b, sall, d = xc.shape
    c_len = sall - s_lat
    lam_init = 0.8 - 0.6 * math.exp(-0.3 * layer_idx)
    mod = _modulation(cs, lp["w_mod"], lp["b_mod"])
    mod_lat = mod[:b].reshape(b, 1, 6, d)
    mod_ctx = jnp.broadcast_to(mod[b].reshape(1, 1, 6, d), (b, 1, 6, d))
    modall = jnp.concatenate([mod_lat, mod_ctx], axis=1)
    modall = jnp.pad(modall, ((0, 0), (0, 0), (0, 2), (0, 0))).reshape(2 * b, 8, d)

    qa, qb, qd, ka, va, kb, vb, kd, vd, vdt = _projection(
        xc, lp["g_mix"].reshape(1, d), modall, cos, sin, lp["w_in"].astype(BF16), s_lat)
    sink = lp["attn_sink"].astype(F32)
    lamv = jnp.zeros((8, LANES), F32).at[0:4, 0:HEAD_DIM].set(
        jnp.stack([lp["lam_q1"], lp["lam_k1"], lp["lam_q2"], lp["lam_k2"]]).astype(F32))
    subg = lp["subln_g"].reshape(1, LANES).astype(F32)
    oa = _swa(sink, qa, ka, va, s_lat)
    ob = _na(qb, kb, vb, _na_bias_mask(lp["na_rpb"], s_lat // GRID_W), s_lat)
    od = _diff(lamv, subg, qd, kd, vdt, s_lat, lam_init)
    ctx_out = None
    if not last:
        ctx_out = _ctx_attention(sink, lamv, subg, qa, ka, va, qb, kb, vb, qd, kd, vd, s_lat, lam_init)
    n_rows = s_lat if last else sall
    x2 = _out_projection(xc, (oa, ob, od), ctx_out, lp["w_out"].astype(BF16), modall, s_lat)
    tiles_per_batch = n_rows // TOK_TILE
    y = _moe(x2.reshape(b * n_rows, d), lp, modall, gfin, tiles_per_batch, s_lat // TOK_TILE, last)
    return y.reshape(b, n_rows, d)


_LAYER_KEYS = ("w_mod", "b_mod", "g_mix", "g_ffn", "w_in", "w_out", "attn_sink", "na_rpb", "lam_q1", "lam_k1",
               "lam_q2", "lam_k2", "subln_g", "w_router", "router_bias", "w_gate", "w_up", "w_down",
               "ws_gate", "ws_up", "ws_down")


def kernel(x, c, ctx, c_ctx, w_mod, b_mod, g_mix, g_ffn, w_in, w_out, attn_sink, na_rpb, lam_q1, lam_k1, lam_q2,
           lam_k2, subln_g, w_router, router_bias, w_gate, w_up, w_down, ws_gate, ws_up, ws_down, g_final):
    stacked = dict(zip(_LAYER_KEYS, (w_mod, b_mod, g_mix, g_ffn, w_in, w_out, attn_sink, na_rpb, lam_q1, lam_k1,
                                     lam_q2, lam_k2, subln_g, w_router, router_bias, w_gate, w_up, w_down,
                                     ws_gate, ws_up, ws_down)))
    b, s_lat, d = x.shape
    c_len = ctx.shape[1]
    depth = w_mod.shape[0]
    assert s_lat % (NA_QROWS * GRID_W) == 0 and s_lat % c_len == 0 and c_len == TOK_TILE
    cs = jnp.zeros((16, d), F32).at[:b].set(c).at[b].set(c_ctx)
    cos, sin = _rope_tables(s_lat, c_len)
    xc = jnp.concatenate([x, ctx], axis=1)
    gfin = g_final.reshape(1, d)
    for i in range(depth):
        big = ("w_gate", "w_up", "w_down")
        lp = {k: (v if k in big else v[i]) for k, v in stacked.items()}
        lp["layer"] = i
        xc = _layer(xc, cs, cos, sin, lp, i, i == depth - 1, s_lat, gfin)
    return xc
```

```python
import functools
import math

import numpy as np
import jax
import jax.numpy as jnp
from jax import lax
from jax.experimental import pallas as pl
from jax.experimental.pallas import tpu as pltpu

F32 = jnp.float32
BF16 = jnp.bfloat16
I32 = jnp.int32
U32 = jnp.uint32

GRID_W = 64
HEAD_DIM = 64
ROPE_BASE = 10000.0
SWA_WINDOW = 128
NA_KH = 8
NA_KW = 16
N_HEADS = 4
N_EXPERTS = 256
TOP_K = 8
N_GROUPS = 8
TOPK_GROUPS = 4
ROUTED_SCALE = 2.5
MOE_BLOCK = 1024
ROW_CHUNK = 8
WAIT_GROUP = 16
COMBINE_KCHUNK = 512
CHUNK_UNROLL = 2
EPS = 1e-6
NEG_INF = -1e30
LOG2E = math.log2(math.e)
Q_COLS = 1024
IN_COLS = 2816
MIX_WIDTH = 1024

LANES = 128
TOK_TILE = 256
ATT_TQ = 256
NA_QROWS = 4
NA_KROWS = NA_QROWS + NA_KH - 1
DIFF_KCHUNK = 256
DIFF_HEADS_PER_STEP = 2
VMEM_LIMIT = 48 * 1024 * 1024


def _params(sem):
    return pltpu.CompilerParams(dimension_semantics=sem, vmem_limit_bytes=VMEM_LIMIT)


def _dot(a, b):
    return jnp.dot(a, b, preferred_element_type=F32)


def _dot_nt(a, b):
    return lax.dot_general(a, b, (((1,), (1,)), ((), ())), preferred_element_type=F32)


def _split_bf16(a):
    hi = a.astype(BF16)
    lo = (a - hi.astype(F32)).astype(BF16)
    return hi, lo


def _sigmoid(x):
    return 1.0 / (1.0 + jnp.exp(-x))


def _rms(x, g):
    return x * lax.rsqrt(jnp.mean(x * x, axis=-1, keepdims=True) + EPS) * g


def _norm_mod(x, g, shift, scale):
    return _rms(x, g) * (1.0 + scale) + shift


def _mod_kernel(c_ref, w_ref, b_ref, o_ref):
    c = c_ref[...]
    a_hi, a_lo = _split_bf16(c * _sigmoid(c))
    w_hi, w_lo = _split_bf16(w_ref[...])
    o_ref[...] = _dot(a_hi, w_hi) + _dot(a_hi, w_lo) + _dot(a_lo, w_hi) + b_ref[...]


def _modulation(cs, w_mod, b_mod):
    n, d = cs.shape
    cols = w_mod.shape[1]
    tn = 1536
    return pl.pallas_call(
        _mod_kernel,
        grid=(cols // tn,),
        in_specs=[pl.BlockSpec((n, d), lambda j: (0, 0)),
                  pl.BlockSpec((d, tn), lambda j: (0, j)),
                  pl.BlockSpec((1, tn), lambda j: (0, j))],
        out_specs=pl.BlockSpec((n, tn), lambda j: (0, j)),
        out_shape=jax.ShapeDtypeStruct((n, cols), F32),
        compiler_params=_params(("arbitrary",)),
        name="modulation",
    )(cs, w_mod, b_mod.reshape(1, cols))


def _proj_kernel(x_ref, g_ref, mod_ref, cos_ref, sin_ref, w_ref,
                 qa_ref, qb_ref, qd_ref, ka_ref, va_ref, kb_ref, vb_ref, kd_ref, vd_ref, vdt_ref, h_scr):
    h = _norm_mod(x_ref[0], g_ref[...], mod_ref[0, 0:1, :], mod_ref[0, 1:2, :])
    h_scr[...] = h.astype(BF16)
    cos = cos_ref[...]
    sin = sin_ref[...]
    tm = h.shape[0]
    lane = lax.broadcasted_iota(I32, (tm, LANES), 1)
    first16 = (lane & 16) == 0
    lo64 = lane < HEAD_DIM

    def rope(v):
        partner = jnp.where(first16, pltpu.roll(v, LANES - 16, 1), pltpu.roll(v, 16, 1))
        return v * cos + partner * sin

    def mm(c0):
        return _dot(h_scr[...], w_ref[:, c0:c0 + 2 * LANES])

    a = mm(0) * (HEAD_DIM ** -0.5)
    c0 = rope(a[:, :LANES])
    c1 = rope(a[:, LANES:])
    zero = jnp.zeros_like(c0)
    qa_ref[0, :, 0 * LANES:1 * LANES] = jnp.where(lo64, c0, zero).astype(BF16)
    qa_ref[0, :, 1 * LANES:2 * LANES] = jnp.where(lo64, pltpu.roll(c0, HEAD_DIM, 1), zero).astype(BF16)
    qa_ref[0, :, 2 * LANES:3 * LANES] = jnp.where(lo64, zero, pltpu.roll(c1, HEAD_DIM, 1)).astype(BF16)
    qa_ref[0, :, 3 * LANES:4 * LANES] = jnp.where(lo64, zero, c1).astype(BF16)
    qb_ref[0] = (mm(256) * (HEAD_DIM ** -0.5)).astype(BF16)
    for j in range(2):
        a = mm(512 + 256 * j) * (HEAD_DIM ** -0.5 * LOG2E)
        qd_ref[0, :, 256 * j:256 * j + LANES] = rope(a[:, :LANES]).astype(BF16)
        qd_ref[0, :, 256 * j + LANES:256 * (j + 1)] = rope(a[:, LANES:]).astype(BF16)
    a = mm(1024)
    ka_ref[0] = rope(a[:, :LANES]).astype(BF16)
    va_ref[0] = a[:, LANES:].astype(BF16)
    kb_ref[0] = mm(1280).astype(BF16)
    vb_ref[0] = mm(1536).astype(BF16)
    for j in range(2):
        a = mm(1792 + 256 * j)
        kd_ref[0, :, 256 * j:256 * j + LANES] = rope(a[:, :LANES]).astype(BF16)
        kd_ref[0, :, 256 * j + LANES:256 * (j + 1)] = rope(a[:, LANES:]).astype(BF16)
        a = mm(2304 + 256 * j)
        vd_ref[0, :, 256 * j:256 * (j + 1)] = a.astype(BF16)
        vdt_ref[0, 256 * j:256 * j + LANES, :] = a[:, :LANES].T.astype(BF16)
        vdt_ref[0, 256 * j + LANES:256 * (j + 1), :] = a[:, LANES:].T.astype(BF16)


def _projection(xc, g, modall, cos, sin, w_in, s_lat):
    b, sall, d = xc.shape
    tm = TOK_TILE
    nlat = s_lat // tm
    row = lambda width: pl.BlockSpec((1, tm, width), lambda i, j: (i, j, 0))
    widths = (512, 256, 512, 128, 128, 256, 256, 512, 512)
    return pl.pallas_call(
        _proj_kernel,
        grid=(b, sall // tm),
        in_specs=[row(d),
                  pl.BlockSpec((1, d), lambda i, j: (0, 0)),
                  pl.BlockSpec((1, 8, d), lambda i, j: (2 * i + (j >= nlat).astype(I32), 0, 0)),
                  pl.BlockSpec((tm, LANES), lambda i, j: (j, 0)),
                  pl.BlockSpec((tm, LANES), lambda i, j: (j, 0)),
                  pl.BlockSpec((d, IN_COLS), lambda i, j: (0, 0))],
        out_specs=[row(w) for w in widths] + [pl.BlockSpec((1, 512, tm), lambda i, j: (i, 0, j))],
        out_shape=([jax.ShapeDtypeStruct((b, sall, w), BF16) for w in widths]
                   + [jax.ShapeDtypeStruct((b, 512, sall), BF16)]),
        scratch_shapes=[pltpu.VMEM((tm, d), BF16)],
        compiler_params=_params(("arbitrary", "arbitrary")),
        name="projection",
    )(xc, g, modall, cos, sin, w_in)


def _half_mask(q, half):
    lane = lax.broadcasted_iota(I32, q.shape, 1)
    keep = (lane < HEAD_DIM) if half == 0 else (lane >= HEAD_DIM)
    return jnp.where(keep, q, jnp.zeros_like(q))


def _merge_halves(lo_part, hi_part):
    lane = lax.broadcasted_iota(I32, lo_part.shape, 1)
    return jnp.where(lane < HEAD_DIM, lo_part, hi_part)


def _gqa_rows(q_ref, g):
    return jnp.concatenate([q_ref[0, :, (2 * g) * LANES:(2 * g + 1) * LANES],
                            q_ref[0, :, (2 * g + 1) * LANES:(2 * g + 2) * LANES]], axis=0)


def _gqa_sink(sink_ref, g, tq):
    rowi = lax.broadcasted_iota(I32, (2 * tq, 1), 0)
    return jnp.where(rowi < tq, sink_ref[2 * g], sink_ref[2 * g + 1])


def _gqa_store(o_ref, g, o, tq):
    top, bot = o[:tq], o[tq:]
    if g == 0:
        chunk = _merge_halves(top, pltpu.roll(bot, HEAD_DIM, 1))
    else:
        chunk = _merge_halves(pltpu.roll(top, HEAD_DIM, 1), bot)
    o_ref[0, :, g * LANES:(g + 1) * LANES] = chunk.astype(BF16)


def _lam(lamv_ref, lam_init):
    v = lamv_ref[...]
    a = jnp.sum(v[0:1] * v[1:2], axis=-1, keepdims=True)
    b = jnp.sum(v[2:3] * v[3:4], axis=-1, keepdims=True)
    return jnp.exp(a) - jnp.exp(b) + lam_init


def _diff_post(o, subg, lam_init):
    return _rms(o, subg) * (1.0 - lam_init)


def _swa_kernel(sink_ref, q_ref, k_ref, v_ref, o_ref, *, s_lat, c_len):
    tq = ATT_TQ
    tk = tq + 2 * SWA_WINDOW
    q0 = pl.program_id(1) * tq
    ks = pl.multiple_of(jnp.clip(q0 - SWA_WINDOW, 0, s_lat - tk), SWA_WINDOW)
    kwin = k_ref[0, pl.ds(ks, tk), :]
    vwin = v_ref[0, pl.ds(ks, tk), :]
    kc = k_ref[0, s_lat:s_lat + c_len, :]
    vc = v_ref[0, s_lat:s_lat + c_len, :]
    qpos = q0 + lax.broadcasted_iota(I32, (tq, tk), 0)
    kpos = ks + lax.broadcasted_iota(I32, (tq, tk), 1)
    valid = jnp.abs(qpos - kpos) <= SWA_WINDOW
    valid = jnp.concatenate([valid, valid], axis=0)
    for g in range(2):
        q2 = _gqa_rows(q_ref, g)
        s_loc = jnp.where(valid, _dot_nt(q2, kwin), NEG_INF)
        s_ctx = _dot_nt(q2, kc)
        snk = _gqa_sink(sink_ref, g, tq)
        m = jnp.maximum(jnp.maximum(jnp.max(s_loc, axis=-1, keepdims=True),
                                    jnp.max(s_ctx, axis=-1, keepdims=True)), snk)
        e_loc = jnp.exp(s_loc - m)
        e_ctx = jnp.exp(s_ctx - m)
        den = (jnp.sum(e_loc, axis=-1, keepdims=True) + jnp.sum(e_ctx, axis=-1, keepdims=True)
               + jnp.exp(snk - m))
        o = (_dot(e_loc.astype(BF16), vwin) + _dot(e_ctx.astype(BF16), vc)) / den
        _gqa_store(o_ref, g, o, tq)


def _swa(sink, qa, ka, va, s_lat):
    b, sall, _ = qa.shape
    kv = pl.BlockSpec((1, sall, LANES), lambda i, j: (i, 0, 0))
    return pl.pallas_call(
        functools.partial(_swa_kernel, s_lat=s_lat, c_len=sall - s_lat),
        grid=(b, s_lat // ATT_TQ),
        in_specs=[pl.BlockSpec(memory_space=pltpu.SMEM),
                  pl.BlockSpec((1, ATT_TQ, 512), lambda i, j: (i, j, 0)), kv, kv],
        out_specs=pl.BlockSpec((1, ATT_TQ, 256), lambda i, j: (i, j, 0)),
        out_shape=jax.ShapeDtypeStruct((b, s_lat, 256), BF16),
        compiler_params=_params(("arbitrary", "arbitrary")),
        name="swa",
    )(sink, qa, ka, va)


def _na_kernel(q_ref, k_ref, v_ref, bm_ref, o_ref, *, s_lat, c_len):
    rows = s_lat // GRID_W
    nk = NA_KROWS * GRID_W
    r0 = pl.program_id(1) * NA_QROWS
    ks = pl.multiple_of(jnp.clip(r0 - NA_KH // 2, 0, rows - NA_KROWS) * GRID_W, GRID_W)
    for c in range(2):
        sl = slice(c * LANES, (c + 1) * LANES)
        kwin = k_ref[0, pl.ds(ks, nk), sl]
        vwin = v_ref[0, pl.ds(ks, nk), sl]
        kc = k_ref[0, s_lat:s_lat + c_len, sl]
        vc = v_ref[0, s_lat:s_lat + c_len, sl]
        q = q_ref[0, :, sl]
        outs = []
        for half in range(2):
            qm = _half_mask(q, half)
            s_loc = _dot_nt(qm, kwin) + bm_ref[0, 2 * c + half]
            s_ctx = _dot_nt(qm, kc)
            m = jnp.maximum(jnp.max(s_loc, axis=-1, keepdims=True), jnp.max(s_ctx, axis=-1, keepdims=True))
            e_loc = jnp.exp(s_loc - m)
            e_ctx = jnp.exp(s_ctx - m)
            den = jnp.sum(e_loc, axis=-1, keepdims=True) + jnp.sum(e_ctx, axis=-1, keepdims=True)
            outs.append((_dot(e_loc.astype(BF16), vwin) + _dot(e_ctx.astype(BF16), vc)) / den)
        o_ref[0, :, sl] = _merge_halves(outs[0], outs[1]).astype(BF16)


def _na_bias_mask(rpb, rows):
    nq = NA_QROWS * GRID_W
    nk = NA_KROWS * GRID_W
    col = np.arange(GRID_W)
    cstart = np.clip(col - NA_KW // 2, 0, GRID_W - NA_KW)
    col_ok = (col[None, :] >= cstart[:, None]) & (col[None, :] < cstart[:, None] + NA_KW)
    dc = np.clip(col[None, :] - col[:, None], -(NA_KW - 1), NA_KW - 1) + NA_KW - 1
    sel_c = np.where(col_ok[:, :, None], np.eye(2 * NA_KW - 1)[dc], 0.0)
    tabs = []
    for r0 in (0, NA_QROWS, rows - NA_QROWS):
        ksr = int(np.clip(r0 - NA_KH // 2, 0, rows - NA_KROWS))
        r = r0 + np.arange(NA_QROWS)
        kr = ksr + np.arange(NA_KROWS)
        rs = np.clip(r - NA_KH // 2, 0, rows - NA_KH)
        row_ok = (kr[None, :] >= rs[:, None]) & (kr[None, :] < rs[:, None] + NA_KH)
        dr = np.clip(kr[None, :] - r[:, None], -(NA_KH - 1), NA_KH - 1) + NA_KH - 1
        sel_r = np.where(row_ok[:, :, None], np.eye(2 * NA_KH - 1)[dr], 0.0)
        ok = (row_ok[:, None, :, None] & col_ok[None, :, None, :]).reshape(nq, nk)
        bias = jnp.einsum("hdc,rjd,qkc->hrqjk", rpb.astype(F32), jnp.asarray(sel_r, F32), jnp.asarray(sel_c, F32),
                          precision=lax.Precision.HIGHEST).reshape(N_HEADS, nq, nk)
        tabs.append(jnp.where(ok[None], bias, NEG_INF))
    return jnp.stack(tabs)


def _na(qb, kb, vb, bm, s_lat):
    b, sall, _ = qb.shape
    nq = NA_QROWS * GRID_W
    nsteps = s_lat // nq
    kv = pl.BlockSpec((1, sall, 256), lambda i, j: (i, 0, 0))

    def bm_map(i, j):
        return (jnp.where(j == 0, 0, jnp.where(j == nsteps - 1, 2, 1)), 0, 0, 0)

    return pl.pallas_call(
        functools.partial(_na_kernel, s_lat=s_lat, c_len=sall - s_lat),
        grid=(b, nsteps),
        in_specs=[pl.BlockSpec((1, nq, 256), lambda i, j: (i, j, 0)), kv, kv,
                  pl.BlockSpec((1, N_HEADS, nq, NA_KROWS * GRID_W), bm_map)],
        out_specs=pl.BlockSpec((1, nq, 256), lambda i, j: (i, j, 0)),
        out_shape=jax.ShapeDtypeStruct((b, s_lat, 256), BF16),
        compiler_params=_params(("arbitrary", "arbitrary")),
        name="na2d",
    )(qb, kb, vb, bm)


def _diff_kernel(lamv_ref, subg_ref, q_ref, k_ref, vt_ref, o_ref, e_scr, m_scr, *, n_keys, lam_init):
    tq = ATT_TQ
    chunks = [(c0, min(DIFF_KCHUNK, n_keys - c0)) for c0 in range(0, n_keys, DIFF_KCHUNK)]
    lam = _lam(lamv_ref, lam_init)
    stats = []
    for hh in range(DIFF_HEADS_PER_STEP):
        sl = slice(hh * LANES, (hh + 1) * LANES)
        q = q_ref[0, :, sl]
        q12 = jnp.concatenate([_half_mask(q, 0), _half_mask(q, 1)], axis=0)
        m = jnp.full((1, 2 * tq), NEG_INF, F32)
        den = jnp.zeros((1, 2 * tq), F32)
        for i, (c0, n) in enumerate(chunks):
            s = _dot_nt(k_ref[0, c0:c0 + n, sl], q12)
            m_new = jnp.maximum(m, jnp.max(s, axis=0, keepdims=True))
            e = jnp.exp2(s - m_new)
            e_scr[hh, c0:c0 + n, :] = e
            m_scr[hh, i:i + 1, :] = m_new
            den = den * jnp.exp2(m - m_new) + jnp.sum(e, axis=0, keepdims=True)
            m = m_new
        stats.append((m, den))
    for hh in range(DIFF_HEADS_PER_STEP):
        sl = slice(hh * LANES, (hh + 1) * LANES)
        m, den = stats[hh]
        acc = jnp.zeros((LANES, tq), F32)
        for i, (c0, n) in enumerate(chunks):
            f = jnp.exp2(m_scr[hh, i:i + 1, :] - m) / den
            p = e_scr[hh, c0:c0 + n, :tq] * f[:, :tq] - e_scr[hh, c0:c0 + n, tq:] * (lam * f[:, tq:])
            acc = acc + _dot(vt_ref[0, sl, c0:c0 + n], p.astype(BF16))
        o_ref[0, :, sl] = _diff_post(acc.T, subg_ref[...], lam_init).astype(BF16)


def _diff(lamv, subg, qd, kd, vdt, s_lat, lam_init):
    b, sall, _ = qd.shape
    nchunk = -(-sall // DIFF_KCHUNK)
    hps = DIFF_HEADS_PER_STEP
    wid = hps * LANES
    return pl.pallas_call(
        functools.partial(_diff_kernel, n_keys=sall, lam_init=lam_init),
        grid=(b, N_HEADS // hps, s_lat // ATT_TQ),
        in_specs=[pl.BlockSpec((8, LANES), lambda i, h, j: (0, 0)),
                  pl.BlockSpec((1, LANES), lambda i, h, j: (0, 0)),
                  pl.BlockSpec((1, ATT_TQ, wid), lambda i, h, j: (i, j, h)),
                  pl.BlockSpec((1, sall, wid), lambda i, h, j: (i, 0, h)),
                  pl.BlockSpec((1, wid, sall), lambda i, h, j: (i, h, 0))],
        out_specs=pl.BlockSpec((1, ATT_TQ, wid), lambda i, h, j: (i, j, h)),
        out_shape=jax.ShapeDtypeStruct((b, s_lat, 512), BF16),
        scratch_shapes=[pltpu.VMEM((hps, sall, 2 * ATT_TQ), F32),
                        pltpu.VMEM((hps, -(-nchunk // 8) * 8, 2 * ATT_TQ), F32)],
        compiler_params=_params(("arbitrary", "arbitrary", "arbitrary")),
        name="diff_attn",
    )(lamv, subg, qd, kd, vdt)


def _ctx_kernel(sink_ref, lamv_ref, subg_ref, qa_ref, ka_ref, va_ref, qb_ref, kb_ref, vb_ref,
                qd_ref, kd_ref, vd_ref, oa_ref, ob_ref, od_ref, *, lam_init):
    cl = qa_ref.shape[1]
    ka = ka_ref[0]
    va = va_ref[0]
    for g in range(2):
        q2 = _gqa_rows(qa_ref, g)
        s = _dot_nt(q2, ka)
        snk = _gqa_sink(sink_ref, g, cl)
        m = jnp.maximum(jnp.max(s, axis=-1, keepdims=True), snk)
        e = jnp.exp(s - m)
        den = jnp.sum(e, axis=-1, keepdims=True) + jnp.exp(snk - m)
        _gqa_store(oa_ref, g, _dot(e.astype(BF16), va) / den, cl)
    for c in range(2):
        sl = slice(c * LANES, (c + 1) * LANES)
        outs = []
        for half in range(2):
            s = _dot_nt(_half_mask(qb_ref[0, :, sl], half), kb_ref[0, :, sl])
            e = jnp.exp(s - jnp.max(s, axis=-1, keepdims=True))
            outs.append(_dot(e.astype(BF16), vb_ref[0, :, sl]) / jnp.sum(e, axis=-1, keepdims=True))
        ob_ref[0, :, sl] = _merge_halves(outs[0], outs[1]).astype(BF16)
    lam = _lam(lamv_ref, lam_init)
    for h in range(N_HEADS):
        sl = slice(h * LANES, (h + 1) * LANES)
        q = qd_ref[0, :, sl]
        q12 = jnp.concatenate([_half_mask(q, 0), _half_mask(q, 1)], axis=0)
        s = _dot_nt(q12, kd_ref[0, :, sl])
        e = jnp.exp2(s - jnp.max(s, axis=-1, keepdims=True))
        den = jnp.sum(e, axis=-1, keepdims=True)
        p = e[:cl] * (1.0 / den[:cl]) - e[cl:] * (lam / den[cl:])
        o = _dot(p.astype(BF16), vd_ref[0, :, sl])
        od_ref[0, :, sl] = _diff_post(o, subg_ref[...], lam_init).astype(BF16)


def _ctx_attention(sink, lamv, subg, qa, ka, va, qb, kb, vb, qd, kd, vd, s_lat, lam_init):
    b, sall, _ = qa.shape
    cl = sall - s_lat
    blk = s_lat // cl
    row = lambda width: pl.BlockSpec((1, cl, width), lambda i: (i, blk, 0))
    out = lambda width: pl.BlockSpec((1, cl, width), lambda i: (i, 0, 0))
    return pl.pallas_call(
        functools.partial(_ctx_kernel, lam_init=lam_init),
        grid=(b,),
        in_specs=[pl.BlockSpec(memory_space=pltpu.SMEM),
                  pl.BlockSpec((8, LANES), lambda i: (0, 0)),
                  pl.BlockSpec((1, LANES), lambda i: (0, 0)),
                  row(512), row(128), row(128), row(256), row(256), row(256), row(512), row(512), row(512)],
        out_specs=[out(256), out(256), out(512)],
        out_shape=[jax.ShapeDtypeStruct((b, cl, w), BF16) for w in (256, 256, 512)],
        compiler_params=_params(("arbitrary",)),
        name="ctx_attention",
    )(sink, lamv, subg, qa, ka, va, qb, kb, vb, qd, kd, vd)


def _out_kernel(x_ref, w_ref, mod_ref, *refs, nlat):
    xo_ref = refs[-1]

    def emit(oa_ref, ob_ref, od_ref):
        attn = (_dot(oa_ref[0], w_ref[0:256, :]) + _dot(ob_ref[0], w_ref[256:512, :])
                + _dot(od_ref[0], w_ref[512:1024, :]))
        xo_ref[0] = x_ref[0] + mod_ref[0, 2:3, :] * attn

    if len(refs) == 4:
        emit(*refs[:3])
    else:
        is_ctx = pl.program_id(1) >= nlat
        pl.when(jnp.logical_not(is_ctx))(lambda: emit(*refs[:3]))
        pl.when(is_ctx)(lambda: emit(*refs[3:6]))


def _out_projection(xc, lat, ctx, w_out, modall, s_lat):
    b, sall, d = xc.shape
    tm = TOK_TILE
    nlat = s_lat // tm
    n_rows = s_lat if ctx is None else sall
    row = lambda width: pl.BlockSpec((1, tm, width), lambda i, j: (i, j, 0))
    lat_row = lambda width: pl.BlockSpec((1, tm, width), lambda i, j: (i, jnp.minimum(j, nlat - 1), 0))
    ctx_row = lambda width: pl.BlockSpec((1, tm, width), lambda i, j: (i, 0, 0))
    widths = (256, 256, 512)
    specs = [lat_row(w) for w in widths] + ([] if ctx is None else [ctx_row(w) for w in widths])
    return pl.pallas_call(
        functools.partial(_out_kernel, nlat=nlat),
        grid=(b, n_rows // tm),
        in_specs=[row(d),
                  pl.BlockSpec((MIX_WIDTH, d), lambda i, j: (0, 0)),
                  pl.BlockSpec((1, 8, d), lambda i, j: (2 * i + (j >= nlat).astype(I32), 0, 0))] + specs,
        out_specs=row(d),
        out_shape=jax.ShapeDtypeStruct((b, n_rows, d), F32),
        compiler_params=_params(("arbitrary", "arbitrary")),
        name="out_projection",
    )(xc, w_out, modall, *lat, *(() if ctx is None else ctx))


def _first_argmax(v, iota, n):
    m = jnp.max(v, axis=0, keepdims=True)
    ix = jnp.min(jnp.where(v == m, iota, float(n)), axis=0, keepdims=True)
    return m, ix


def _router_kernel(x_ref, g_ref, mod_ref, whi_ref, wlo_ref, bias_ref, tri_ref, ltri_ref,
                   w_ref, slot_ref, meta_ref, cnt_ref):
    @pl.when(pl.program_id(0) == 0)
    def _():
        cnt_ref[...] = jnp.zeros_like(cnt_ref)

    hf = _norm_mod(x_ref[...], g_ref[...], mod_ref[0, 3:4, :], mod_ref[0, 4:5, :])
    tm = hf.shape[0]
    h_hi, h_lo = _split_bf16(hf)
    whi = whi_ref[...]
    logits = _dot_nt(whi, h_hi) + _dot_nt(whi, h_lo) + _dot_nt(wlo_ref[...], h_hi)
    scores = _sigmoid(logits)
    biased = scores + bias_ref[...]
    gsz = N_EXPERTS // N_GROUPS
    iota_g = lax.broadcasted_iota(I32, (gsz, tm), 0).astype(F32)
    gscore = []
    for g in range(N_GROUPS):
        v = biased[g * gsz:(g + 1) * gsz]
        m1, i1 = _first_argmax(v, iota_g, gsz)
        m2 = jnp.max(jnp.where(iota_g == i1, -jnp.inf, v), axis=0, keepdims=True)
        gscore.append(m1 + m2)
    cur = jnp.concatenate(gscore, axis=0)
    iota_n = lax.broadcasted_iota(I32, (N_GROUPS, tm), 0).astype(F32)
    gsel = jnp.zeros((N_GROUPS, tm), F32)
    for _ in range(TOPK_GROUPS):
        _, ix = _first_argmax(cur, iota_n, N_GROUPS)
        hit = iota_n == ix
        gsel = jnp.where(hit, 1.0, gsel)
        cur = jnp.where(hit, -jnp.inf, cur)
    masked = jnp.concatenate(
        [jnp.where(gsel[g:g + 1] > 0.5, biased[g * gsz:(g + 1) * gsz], -jnp.inf) for g in range(N_GROUPS)],
        axis=0)
    iota_e = lax.broadcasted_iota(I32, (N_EXPERTS, tm), 0).astype(F32)
    onehot = jnp.zeros((N_EXPERTS, tm), F32)
    idxs, ws = [], []
    for _ in range(TOP_K):
        _, ix = _first_argmax(masked, iota_e, N_EXPERTS)
        hit = iota_e == ix
        ws.append(jnp.sum(jnp.where(hit, scores, 0.0), axis=0, keepdims=True))
        masked = jnp.where(hit, -jnp.inf, masked)
        onehot = jnp.where(hit, 1.0, onehot)
        idxs.append(ix)
    base = cnt_ref[...]
    rank = _dot(onehot.astype(BF16), tri_ref[...])
    n_col = jnp.sum(onehot, axis=1, keepdims=True)
    n_pad = jnp.floor((n_col + (ROW_CHUNK - 1)) * (1.0 / ROW_CHUNK)) * ROW_CHUNK
    start = _dot(ltri_ref[...], jnp.broadcast_to(n_pad, (N_EXPERTS, LANES)).astype(BF16))
    wsum = ws[0]
    for wk in ws[1:]:
        wsum = wsum + wk
    for k in range(TOP_K):
        slot = jnp.sum(jnp.where(iota_e == idxs[k], rank + start[:, 0:1], 0.0), axis=0, keepdims=True)
        slot_ref[0, k:k + 1, :] = slot.astype(I32)
        w_ref[0, k:k + 1, :] = ws[k] / wsum * ROUTED_SCALE
    lane = lax.broadcasted_iota(I32, (N_EXPERTS, LANES), 1)
    meta_ref[0] = jnp.where(lane == 0, n_pad, jnp.where(lane == 1, base, start))
    cnt_ref[...] = base + n_pad


def _router(xflat, g, modall, whi, wlo, bias, tri, ltri, tiles_per_batch, nlat):
    t, d = xflat.shape
    tm = TOK_TILE
    nt = t // tm
    per_tok = pl.BlockSpec((1, TOP_K, tm), lambda i: (i, 0, 0))

    def mod_map(i):
        return (2 * (i // tiles_per_batch) + ((i % tiles_per_batch) >= nlat).astype(I32), 0, 0)

    return pl.pallas_call(
        _router_kernel,
        grid=(nt,),
        in_specs=[pl.BlockSpec((tm, d), lambda i: (i, 0)),
                  pl.BlockSpec((1, d), lambda i: (0, 0)),
                  pl.BlockSpec((1, 8, d), mod_map),
                  pl.BlockSpec((N_EXPERTS, d), lambda i: (0, 0)),
                  pl.BlockSpec((N_EXPERTS, d), lambda i: (0, 0)),
                  pl.BlockSpec((N_EXPERTS, 1), lambda i: (0, 0)),
                  pl.BlockSpec((tm, tm), lambda i: (0, 0)),
                  pl.BlockSpec((N_EXPERTS, N_EXPERTS), lambda i: (0, 0))],
        out_specs=[per_tok, per_tok,
                   pl.BlockSpec((1, N_EXPERTS, LANES), lambda i: (i, 0, 0)),
                   pl.BlockSpec((N_EXPERTS, 1), lambda i: (0, 0))],
        out_shape=[jax.ShapeDtypeStruct((nt, TOP_K, tm), F32),
                   jax.ShapeDtypeStruct((nt, TOP_K, tm), I32),
                   jax.ShapeDtypeStruct((nt, N_EXPERTS, LANES), F32),
                   jax.ShapeDtypeStruct((N_EXPERTS, 1), F32)],
        compiler_params=_params(("arbitrary",)),
        name="router",
    )(xflat, g, modall, whi, wlo, bias, tri, ltri)


def _pack_bf16_pairs(v, is_bf16_valued=False):
    half = v.shape[1] // 2
    if not is_bf16_valued:
        v = v.astype(BF16).astype(F32)
    lo = lax.shift_right_logical(lax.bitcast_convert_type(v[:, :half], U32), jnp.uint32(16))
    hi = lax.bitcast_convert_type(v[:, half:], U32) & jnp.uint32(0xFFFF0000)
    return lo | hi


def _unpack_bf16_pairs(u):
    lo = lax.bitcast_convert_type(lax.shift_left(u, jnp.uint32(16)), F32).astype(BF16)
    hi = lax.bitcast_convert_type(u & jnp.uint32(0xFFFF0000), F32).astype(BF16)
    return lo, hi


def _used_chunks(meta_ref):
    return lax.shift_right_logical(_slots_used(meta_ref), ROW_CHUNK.bit_length() - 1)


def _start_chunk_copies(first, count, make_copy):
    pairs = lax.shift_right_logical(count, 1)

    def start_pair(j, carry):
        make_copy(first + 2 * j).start()
        make_copy(first + 2 * j + 1).start()
        return carry

    lax.fori_loop(0, pairs, start_pair, 0)

    @pl.when(count > 2 * pairs)
    def _():
        make_copy(first + count - 1).start()


def _wait_chunk_copies(total, copy_of_rows):
    group = lax.shift_right_logical(total, WAIT_GROUP.bit_length() - 1)

    def wait_group(j, carry):
        copy_of_rows(WAIT_GROUP * ROW_CHUNK).wait()
        return carry

    def wait_one(j, carry):
        copy_of_rows(ROW_CHUNK).wait()
        return carry

    lax.fori_loop(0, group, wait_group, 0)
    lax.fori_loop(0, total - group * WAIT_GROUP, wait_one, 0)


def _tile_slots(tm):
    return TOP_K * tm + ROW_CHUNK * N_EXPERTS


def _slots_used(meta_ref):
    last = N_EXPERTS - 1
    return meta_ref[0, 2, last] + meta_ref[0, 0, last]


def _dispatch_kernel(meta_ref, rows_ref, x_ref, g_ref, mod_ref, slot_ref, xs_ref, xc_scr, sem):
    hb = _norm_mod(x_ref[...], g_ref[...], mod_ref[0, 3:4, :], mod_ref[0, 4:5, :]).astype(BF16)
    tm = hb.shape[0]
    slot16 = slot_ref[0].astype(jnp.int16)

    total = _used_chunks(meta_ref)
    step = CHUNK_UNROLL * tm
    chunks_per_step = step // ROW_CHUNK

    def make_copy(j):
        src = pl.multiple_of(j * ROW_CHUNK, ROW_CHUNK)
        dst = pl.multiple_of(rows_ref[0, 0, j], ROW_CHUNK)
        return pltpu.make_async_copy(xc_scr.at[pl.ds(src, ROW_CHUNK)], xs_ref.at[pl.ds(dst, ROW_CHUNK)], sem)

    def select_rows(r2, carry):
        for u in range(CHUNK_UNROLL):
            r0 = pl.multiple_of((r2 * CHUNK_UNROLL + u) * tm, tm)
            row = (lax.broadcasted_iota(I32, (tm, tm), 0) + r0).astype(jnp.int16)
            sel = jnp.zeros((tm, tm), BF16)
            for k in range(TOP_K):
                sel = jnp.where(row == slot16[k:k + 1, :], jnp.ones((), BF16), sel)
            xc_scr[pl.ds(r0, tm), :] = _pack_bf16_pairs(_dot(sel, hb), is_bf16_valued=True)
        first = r2 * chunks_per_step
        _start_chunk_copies(first, jnp.clip(total - first, 0, chunks_per_step), make_copy)
        return carry

    lax.fori_loop(0, (_slots_used(meta_ref) + step - 1) // step, select_rows, 0)
    _wait_chunk_copies(total, lambda n: pltpu.make_async_copy(xc_scr.at[pl.ds(0, n)], xs_ref.at[pl.ds(0, n)], sem))


def _dispatch(meta, chunk_rows, slot, xflat, g, modall, n_rows, tiles_per_batch, nlat):
    t, d = xflat.shape
    tm = TOK_TILE
    nt = t // tm

    def mod_map(i):
        return (2 * (i // tiles_per_batch) + ((i % tiles_per_batch) >= nlat).astype(I32), 0, 0)

    return pl.pallas_call(
        _dispatch_kernel,
        grid=(nt,),
        in_specs=[pl.BlockSpec((1, 3, N_EXPERTS), lambda i: (i, 0, 0), memory_space=pltpu.SMEM),
                  pl.BlockSpec((1, 1, chunk_rows.shape[2]), lambda i: (i, 0, 0), memory_space=pltpu.SMEM),
                  pl.BlockSpec((tm, d), lambda i: (i, 0)),
                  pl.BlockSpec((1, d), lambda i: (0, 0)),
                  pl.BlockSpec((1, 8, d), mod_map),
                  pl.BlockSpec((1, TOP_K, tm), lambda i: (i, 0, 0))],
        out_specs=pl.BlockSpec(memory_space=pl.ANY),
        out_shape=jax.ShapeDtypeStruct((n_rows, d // 2), U32),
        scratch_shapes=[pltpu.VMEM((_tile_slots(tm), d // 2), U32), pltpu.SemaphoreType.DMA(())],
        compiler_params=_params(("arbitrary",)),
        name="moe_dispatch",
    )(meta, chunk_rows, xflat, g, modall, slot)


def _expert_kernel(blk_e_ref, nvalid_ref, nact_ref, next_e_ref, has_next_ref, parity_ref,
                   xs_ref, wg_any, wu_any, wd_any, ys_ref, wg_s, wu_s, wd_s, wg_buf, wu_buf, wd_buf, sem, *, layer):
    b = pl.program_id(0)
    e = blk_e_ref[b]
    changed = jnp.logical_or(b == 0, e != blk_e_ref[jnp.maximum(b - 1, 0)])
    par = parity_ref[b]

    def weight_copies(expert, p):
        return [pltpu.make_async_copy(src.at[layer, expert], dst.at[p], sem.at[p])
                for src, dst in ((wg_any, wg_buf), (wu_any, wu_buf), (wd_any, wd_buf))]

    @pl.when(b == 0)
    def _():
        for c in weight_copies(e, par):
            c.start()

    @pl.when(changed)
    def _():
        for c in weight_copies(e, par):
            c.wait()
        wg_s[...] = wg_buf[par].astype(BF16)
        wu_s[...] = wu_buf[par].astype(BF16)
        wd_s[...] = wd_buf[par].astype(BF16)

        @pl.when(has_next_ref[b] == 1)
        def _():
            for c in weight_copies(next_e_ref[b], 1 - par):
                c.start()

    half = xs_ref.shape[1]
    n_here = jnp.where(b < nact_ref[0], nvalid_ref[b], 0)

    @pl.when(n_here > 0)
    def _():
        row = lax.broadcasted_iota(I32, xs_ref.shape, 0)
        xs = jnp.where(row < n_here, xs_ref[...], jnp.zeros(xs_ref.shape, U32))
        x_lo, x_hi = _unpack_bf16_pairs(xs)
        gate = _dot(x_lo, wg_s[0:half, :]) + _dot(x_hi, wg_s[half:, :])
        up = _dot(x_lo, wu_s[0:half, :]) + _dot(x_hi, wu_s[half:, :])
        mid = (gate * _sigmoid(gate)) * up
        ys_ref[...] = _pack_bf16_pairs(_dot(mid.astype(BF16), wd_s[...]))

    @pl.when(n_here == 0)
    def _():
        ys_ref[...] = jnp.zeros_like(ys_ref)


def _experts(blk_e, nvalid, nact, xs, w_gate, w_up, w_down, layer):
    n_rows, half = xs.shape
    _, _, d, f = w_gate.shape
    n_blk = n_rows // MOE_BLOCK
    first_of_next = jnp.sum((blk_e[None, :] <= blk_e[:, None]).astype(I32), axis=1)
    has_next = (first_of_next < n_blk).astype(I32)
    next_e = blk_e[jnp.minimum(first_of_next, n_blk - 1)]
    starts = jnp.concatenate([jnp.zeros((1,), I32), (blk_e[1:] != blk_e[:-1]).astype(I32)])
    parity = jnp.cumsum(starts) % 2
    anyspec = pl.BlockSpec(memory_space=pl.ANY)
    grid_spec = pltpu.PrefetchScalarGridSpec(
        num_scalar_prefetch=6,
        grid=(n_blk,),
        in_specs=[pl.BlockSpec((MOE_BLOCK, half), lambda i, be, nv, na, *_: (jnp.minimum(i, na[0] - 1), 0)),
                  anyspec, anyspec, anyspec],
        out_specs=pl.BlockSpec((MOE_BLOCK, half), lambda i, be, nv, na, *_: (jnp.minimum(i, na[0]), 0)),
        scratch_shapes=[pltpu.VMEM((d, f), BF16), pltpu.VMEM((d, f), BF16), pltpu.VMEM((f, d), BF16),
                        pltpu.VMEM((2, d, f), F32), pltpu.VMEM((2, d, f), F32), pltpu.VMEM((2, f, d), F32),
                        pltpu.SemaphoreType.DMA((2,))],
    )
    return pl.pallas_call(
        functools.partial(_expert_kernel, layer=layer),
        grid_spec=grid_spec,
        out_shape=jax.ShapeDtypeStruct((n_rows, half), U32),
        compiler_params=_params(("arbitrary",)),
        name="moe_experts",
    )(blk_e, nvalid, nact, next_e, has_next, parity.astype(I32), xs, w_gate, w_up, w_down)


def _combine_kernel(meta_ref, rows_ref, x_ref, g_ref, mod_ref, spos_ref, wt_ref, wsg_ref, wsu_ref, wsd_ref,
                    gfin_ref, ys_ref, o_ref, y_scr, acc_scr, sb_scr, wb_scr, sem, *, final_norm):
    tm, d = x_ref.shape
    half = d // 2

    @pl.when(pl.program_id(0) == 0)
    def _():
        y_scr[...] = jnp.zeros_like(y_scr)

    step = CHUNK_UNROLL * COMBINE_KCHUNK
    chunks_per_step = step // ROW_CHUNK
    group_shift = chunks_per_step.bit_length() - 1

    def make_copy(j):
        src = pl.multiple_of(rows_ref[0, 0, j], ROW_CHUNK)
        dst = pl.multiple_of(j * ROW_CHUNK, ROW_CHUNK)
        return pltpu.make_async_copy(ys_ref.at[pl.ds(src, ROW_CHUNK)], y_scr.at[pl.ds(dst, ROW_CHUNK)],
                                     sem.at[lax.shift_right_logical(j, group_shift)])

    total = _used_chunks(meta_ref)
    _start_chunk_copies(0, total, make_copy)

    x = x_ref[...]
    hb = _norm_mod(x, g_ref[...], mod_ref[0, 3:4, :], mod_ref[0, 4:5, :]).astype(BF16)
    gate = _dot(hb, wsg_ref[...])
    mid = (gate * _sigmoid(gate)) * _dot(hb, wsu_ref[...])
    acc_scr[...] = _dot(mid.astype(BF16), wsd_ref[...])

    n_used = _slots_used(meta_ref)
    for k in range(TOP_K):
        sb_scr[k] = jnp.broadcast_to(spos_ref[0, :, k:k + 1], (tm, LANES)).astype(jnp.int16)
        wb_scr[k] = jnp.broadcast_to(wt_ref[0, :, k:k + 1], (tm, LANES)).astype(BF16)

    def chunk(j2, carry):
        _wait_chunk_copies(jnp.clip(total - j2 * chunks_per_step, 0, chunks_per_step),
                           lambda n: pltpu.make_async_copy(ys_ref.at[pl.ds(0, n)], y_scr.at[pl.ds(0, n)], sem.at[j2]))
        lo_sum = jnp.zeros((tm, half), F32)
        hi_sum = jnp.zeros((tm, half), F32)
        for u in range(CHUNK_UNROLL):
            r0 = pl.multiple_of((j2 * CHUNK_UNROLL + u) * COMBINE_KCHUNK, COMBINE_KCHUNK)
            parts = []
            for c in range(COMBINE_KCHUNK // LANES):
                col = (lax.broadcasted_iota(I32, (tm, LANES), 1) + (r0 + c * LANES)).astype(jnp.int16)
                p = jnp.zeros((tm, LANES), BF16)
                for k in range(TOP_K):
                    p = jnp.where(col == sb_scr[k], wb_scr[k], p)
                parts.append(p)
            pb = jnp.concatenate(parts, axis=1)
            y_lo, y_hi = _unpack_bf16_pairs(y_scr[pl.ds(r0, COMBINE_KCHUNK), :])
            lo_sum = lo_sum + _dot(pb, y_lo)
            hi_sum = hi_sum + _dot(pb, y_hi)
        acc_scr[:, :half] += lo_sum
        acc_scr[:, half:] += hi_sum
        return carry

    lax.fori_loop(0, (n_used + step - 1) // step, chunk, 0)
    out = x + mod_ref[0, 5:6, :] * acc_scr[...]
    if final_norm:
        out = _rms(out, gfin_ref[...])
    o_ref[...] = out


def _combine(meta, chunk_rows, spos_t, wt, xflat, g, modall, wsg, wsu, wsd, gfin, ys, tiles_per_batch, nlat,
             final_norm):
    t, d = xflat.shape
    f = wsg.shape[1]
    tm = TOK_TILE
    nt = t // tm
    y_rows = _tile_slots(tm)
    assert y_rows % COMBINE_KCHUNK == 0

    def mod_map(i):
        return (2 * (i // tiles_per_batch) + ((i % tiles_per_batch) >= nlat).astype(I32), 0, 0)

    return pl.pallas_call(
        functools.partial(_combine_kernel, final_norm=final_norm),
        grid=(nt,),
        in_specs=[pl.BlockSpec((1, 3, N_EXPERTS), lambda i: (i, 0, 0), memory_space=pltpu.SMEM),
                  pl.BlockSpec((1, 1, chunk_rows.shape[2]), lambda i: (i, 0, 0), memory_space=pltpu.SMEM),
                  pl.BlockSpec((tm, d), lambda i: (i, 0)),
                  pl.BlockSpec((1, d), lambda i: (0, 0)),
                  pl.BlockSpec((1, 8, d), mod_map),
                  pl.BlockSpec((1, tm, TOP_K), lambda i: (i, 0, 0)),
                  pl.BlockSpec((1, tm, TOP_K), lambda i: (i, 0, 0)),
                  pl.BlockSpec((d, f), lambda i: (0, 0)),
                  pl.BlockSpec((d, f), lambda i: (0, 0)),
                  pl.BlockSpec((f, d), lambda i: (0, 0)),
                  pl.BlockSpec((1, d), lambda i: (0, 0)),
                  pl.BlockSpec(memory_space=pl.ANY)],
        out_specs=pl.BlockSpec((tm, d), lambda i: (i, 0)),
        out_shape=jax.ShapeDtypeStruct((t, d), F32),
        scratch_shapes=[pltpu.VMEM((y_rows, d // 2), U32), pltpu.VMEM((tm, d), F32),
                        pltpu.VMEM((TOP_K, tm, LANES), jnp.int16), pltpu.VMEM((TOP_K, tm, LANES), BF16),
                        pltpu.SemaphoreType.DMA((y_rows // (CHUNK_UNROLL * COMBINE_KCHUNK),))],
        compiler_params=_params(("arbitrary",)),
        name="moe_combine",
    )(meta, chunk_rows, xflat, g, modall, spos_t, wt, wsg, wsu, wsd, gfin, ys)


def _rope_tables(s_lat, c_len):
    n_freq = HEAD_DIM // 4
    inv = 1.0 / (ROPE_BASE ** (jnp.arange(n_freq, dtype=F32) / n_freq))
    t = jnp.arange(s_lat)
    pos = jnp.stack([(t // GRID_W).astype(F32), (t % GRID_W).astype(F32)], axis=1)
    lane = np.arange(LANES)
    ang = pos[:, (lane % HEAD_DIM) // 32] * inv[lane % 16][None, :]
    sign = jnp.asarray(np.where(lane % 32 < 16, -1.0, 1.0), F32)
    cos = jnp.concatenate([jnp.cos(ang), jnp.ones((c_len, LANES), F32)], axis=0)
    sin = jnp.concatenate([jnp.sin(ang) * sign, jnp.zeros((c_len, LANES), F32)], axis=0)
    return cos, sin


def _moe(xflat, lp, modall, gfin, tiles_per_batch, nlat, final_norm):
    t, d = xflat.shape
    nt = t // TOK_TILE
    g = lp["g_ffn"].reshape(1, d)
    w_r = lp["w_router"].T
    whi = w_r.astype(BF16)
    wlo = (w_r - whi.astype(F32)).astype(BF16)
    tri = jnp.asarray(np.triu(np.ones((TOK_TILE, TOK_TILE), np.float32), 1), BF16)
    ltri = jnp.asarray(np.tril(np.ones((N_EXPERTS, N_EXPERTS), np.float32), -1), BF16)
    w, slot, meta, cnt = _router(xflat, g, modall, whi, wlo, lp["router_bias"].reshape(N_EXPERTS, 1),
                                 tri, ltri, tiles_per_batch, nlat)
    meta = jnp.transpose(meta[:, :, :3].astype(I32), (0, 2, 1))
    rows_e = cnt[:, 0].astype(I32)
    seg = (rows_e + MOE_BLOCK - 1) // MOE_BLOCK * MOE_BLOCK
    pend = jnp.cumsum(seg)
    pstart = (pend - seg).astype(I32)
    n_blk = (t * TOP_K + ROW_CHUNK * N_EXPERTS * nt) // MOE_BLOCK + N_EXPERTS
    blk_row0 = jnp.arange(n_blk, dtype=I32) * MOE_BLOCK
    blk_e = jnp.minimum(jnp.sum((pend[None, :] <= blk_row0[:, None]).astype(I32), axis=1), N_EXPERTS - 1)
    of_blk = blk_e[:, None] == jnp.arange(N_EXPERTS, dtype=I32)[None, :]
    seg_end = jnp.sum(jnp.where(of_blk, (pstart + rows_e)[None, :], 0), axis=1)
    nvalid = jnp.clip(seg_end - blk_row0, 0, MOE_BLOCK).astype(I32)
    nact = (pend[-1] // MOE_BLOCK).astype(I32).reshape(1)
    n_pad, base, start = meta[:, 0, :], meta[:, 1, :], meta[:, 2, :]
    chunk_slot = jnp.arange(_tile_slots(TOK_TILE) // ROW_CHUNK, dtype=I32) * ROW_CHUNK
    s = chunk_slot[None, :, None]
    in_run = (start[:, None, :] <= s) & (s < (start + n_pad)[:, None, :])
    row0 = pstart[None, :] + base - start
    chunk_rows = (jnp.sum(jnp.where(in_run, row0[:, None, :], 0), axis=2) + chunk_slot[None, :])[:, None, :]
    xs = _dispatch(meta, chunk_rows, slot, xflat, g, modall, n_blk * MOE_BLOCK, tiles_per_batch, nlat)
    ys = _experts(blk_e, nvalid, nact, xs, lp["w_gate"], lp["w_up"], lp["w_down"], lp["layer"])
    return _combine(meta, chunk_rows, jnp.transpose(slot, (0, 2, 1)), jnp.transpose(w, (0, 2, 1)), xflat, g, modall,
                    lp["ws_gate"].astype(BF16), lp["ws_up"].astype(BF16), lp["ws_down"].astype(BF16), gfin, ys,
                    tiles_per_batch, nlat, final_norm)


def _layer(xc, cs, cos, sin, lp, layer_idx, last, s_lat, gfin):
    b, sall, d = xc.shape
    c_len = sall - s_lat
    lam_init = 0.8 - 0.6 * math.exp(-0.3 * layer_idx)
    mod = _modulation(cs, lp["w_mod"], lp["b_mod"])
    mod_lat = mod[:b].reshape(b, 1, 6, d)
    mod_ctx = jnp.broadcast_to(mod[b].reshape(1, 1, 6, d), (b, 1, 6, d))
    modall = jnp.concatenate([mod_lat, mod_ctx], axis=1)
    modall = jnp.pad(modall, ((0, 0), (0, 0), (0, 2), (0, 0))).reshape(2 * b, 8, d)

    qa, qb, qd, ka, va, kb, vb, kd, vd, vdt = _projection(
        xc, lp["g_mix"].reshape(1, d), modall, cos, sin, lp["w_in"].astype(BF16), s_lat)
    sink = lp["attn_sink"].astype(F32)
    lamv = jnp.zeros((8, LANES), F32).at[0:4, 0:HEAD_DIM].set(
        jnp.stack([lp["lam_q1"], lp["lam_k1"], lp["lam_q2"], lp["lam_k2"]]).astype(F32))
    subg = lp["subln_g"].reshape(1, LANES).astype(F32)
    oa = _swa(sink, qa, ka, va, s_lat)
    ob = _na(qb, kb, vb, _na_bias_mask(lp["na_rpb"], s_lat // GRID_W), s_lat)
    od = _diff(lamv, subg, qd, kd, vdt, s_lat, lam_init)
    ctx_out = None
    if not last:
        ctx_out = _ctx_attention(sink, lamv, subg, qa, ka, va, qb, kb, vb, qd, kd, vd, s_lat, lam_init)
    n_rows = s_lat if last else sall
    x2 = _out_projection(xc, (oa, ob, od), ctx_out, lp["w_out"].astype(BF16), modall, s_lat)
    tiles_per_batch = n_rows // TOK_TILE
    y = _moe(x2.reshape(b * n_rows, d), lp, modall, gfin, tiles_per_batch, s_lat // TOK_TILE, last)
    return y.reshape(b, n_rows, d)


_LAYER_KEYS = ("w_mod", "b_mod", "g_mix", "g_ffn", "w_in", "w_out", "attn_sink", "na_rpb", "lam_q1", "lam_k1",
               "lam_q2", "lam_k2", "subln_g", "w_router", "router_bias", "w_gate", "w_up", "w_down",
               "ws_gate", "ws_up", "ws_down")


def kernel(x, c, ctx, c_ctx, w_mod, b_mod, g_mix, g_ffn, w_in, w_out, attn_sink, na_rpb, lam_q1, lam_k1, lam_q2,
           lam_k2, subln_g, w_router, router_bias, w_gate, w_up, w_down, ws_gate, ws_up, ws_down, g_final):
    stacked = dict(zip(_LAYER_KEYS, (w_mod, b_mod, g_mix, g_ffn, w_in, w_out, attn_sink, na_rpb, lam_q1, lam_k1,
                                     lam_q2, lam_k2, subln_g, w_router, router_bias, w_gate, w_up, w_down,
                                     ws_gate, ws_up, ws_down)))
    b, s_lat, d = x.shape
    c_len = ctx.shape[1]
    depth = w_mod.shape[0]
    assert s_lat % (NA_QROWS * GRID_W) == 0 and s_lat % c_len == 0 and c_len == TOK_TILE
    cs = jnp.zeros((16, d), F32).at[:b].set(c).at[b].set(c_ctx)
    cos, sin = _rope_tables(s_lat, c_len)
    xc = jnp.concatenate([x, ctx], axis=1)
    gfin = g_final.reshape(1, d)
    for i in range(depth):
        big = ("w_gate", "w_up", "w_down")
        lp = {k: (v if k in big else v[i]) for k, v in stacked.items()}
        lp["layer"] = i
        xc = _layer(xc, cs, cos, sin, lp, i, i == depth - 1, s_lat, gfin)
    return xc
```

```python
import functools
import math

import numpy as np
import jax
import jax.numpy as jnp
from jax import lax
from jax.experimental import pallas as pl
from jax.experimental.pallas import tpu as pltpu

F32 = jnp.float32
BF16 = jnp.bfloat16
I32 = jnp.int32
U32 = jnp.uint32

GRID_W = 64
HEAD_DIM = 64
ROPE_BASE = 10000.0
SWA_WINDOW = 128
NA_KH = 8
NA_KW = 16
N_HEADS = 4
N_EXPERTS = 256
TOP_K = 8
N_GROUPS = 8
TOPK_GROUPS = 4
ROUTED_SCALE = 2.5
MOE_BLOCK = 1024
ROW_CHUNK = 8
WAIT_GROUP = 16
COMBINE_KCHUNK = 512
CHUNK_UNROLL = 2
EPS = 1e-6
NEG_INF = -1e30
LOG2E = math.log2(math.e)
Q_COLS = 1024
IN_COLS = 2816
MIX_WIDTH = 1024

LANES = 128
TOK_TILE = 256
ATT_TQ = 256
NA_QROWS = 4
NA_KROWS = NA_QROWS + NA_KH - 1
DIFF_KCHUNK = 256
DIFF_HEADS_PER_STEP = 2
VMEM_LIMIT = 48 * 1024 * 1024


def _params(sem):
    return pltpu.CompilerParams(dimension_semantics=sem, vmem_limit_bytes=VMEM_LIMIT)


def _dot(a, b):
    return jnp.dot(a, b, preferred_element_type=F32)


def _dot_nt(a, b):
    return lax.dot_general(a, b, (((1,), (1,)), ((), ())), preferred_element_type=F32)


def _split_bf16(a):
    hi = a.astype(BF16)
    lo = (a - hi.astype(F32)).astype(BF16)
    return hi, lo


def _sigmoid(x):
    return 1.0 / (1.0 + jnp.exp(-x))


def _rms(x, g):
    return x * lax.rsqrt(jnp.mean(x * x, axis=-1, keepdims=True) + EPS) * g


def _norm_mod(x, g, shift, scale):
    return _rms(x, g) * (1.0 + scale) + shift


def _mod_kernel(c_ref, w_ref, b_ref, o_ref):
    c = c_ref[...]
    a_hi, a_lo = _split_bf16(c * _sigmoid(c))
    w_hi, w_lo = _split_bf16(w_ref[...])
    o_ref[...] = _dot(a_hi, w_hi) + _dot(a_hi, w_lo) + _dot(a_lo, w_hi) + b_ref[...]


def _modulation(cs, w_mod, b_mod):
    n, d = cs.shape
    cols = w_mod.shape[1]
    tn = 1536
    return pl.pallas_call(
        _mod_kernel,
        grid=(cols // tn,),
        in_specs=[pl.BlockSpec((n, d), lambda j: (0, 0)),
                  pl.BlockSpec((d, tn), lambda j: (0, j)),
                  pl.BlockSpec((1, tn), lambda j: (0, j))],
        out_specs=pl.BlockSpec((n, tn), lambda j: (0, j)),
        out_shape=jax.ShapeDtypeStruct((n, cols), F32),
        compiler_params=_params(("arbitrary",)),
        name="modulation",
    )(cs, w_mod, b_mod.reshape(1, cols))


def _proj_kernel(x_ref, g_ref, mod_ref, cos_ref, sin_ref, w_ref,
                 qa_ref, qb_ref, qd_ref, ka_ref, va_ref, kb_ref, vb_ref, kd_ref, vd_ref, vdt_ref, h_scr):
    h = _norm_mod(x_ref[0], g_ref[...], mod_ref[0, 0:1, :], mod_ref[0, 1:2, :])
    h_scr[...] = h.astype(BF16)
    cos = cos_ref[...]
    sin = sin_ref[...]
    tm = h.shape[0]
    lane = lax.broadcasted_iota(I32, (tm, LANES), 1)
    first16 = (lane & 16) == 0
    lo64 = lane < HEAD_DIM

    def rope(v):
        partner = jnp.where(first16, pltpu.roll(v, LANES - 16, 1), pltpu.roll(v, 16, 1))
        return v * cos + partner * sin

    def mm(c0):
        return _dot(h_scr[...], w_ref[:, c0:c0 + 2 * LANES])

    a = mm(0) * (HEAD_DIM ** -0.5)
    c0 = rope(a[:, :LANES])
    c1 = rope(a[:, LANES:])
    zero = jnp.zeros_like(c0)
    qa_ref[0, :, 0 * LANES:1 * LANES] = jnp.where(lo64, c0, zero).astype(BF16)
    qa_ref[0, :, 1 * LANES:2 * LANES] = jnp.where(lo64, pltpu.roll(c0, HEAD_DIM, 1), zero).astype(BF16)
    qa_ref[0, :, 2 * LANES:3 * LANES] = jnp.where(lo64, zero, pltpu.roll(c1, HEAD_DIM, 1)).astype(BF16)
    qa_ref[0, :, 3 * LANES:4 * LANES] = jnp.where(lo64, zero, c1).astype(BF16)
    qb_ref[0] = (mm(256) * (HEAD_DIM ** -0.5)).astype(BF16)
    for j in range(2):
        a = mm(512 + 256 * j) * (HEAD_DIM ** -0.5 * LOG2E)
        qd_ref[0, :, 256 * j:256 * j + LANES] = rope(a[:, :LANES]).astype(BF16)
        qd_ref[0, :, 256 * j + LANES:256 * (j + 1)] = rope(a[:, LANES:]).astype(BF16)
    a = mm(1024)
    ka_ref[0] = rope(a[:, :LANES]).astype(BF16)
    va_ref[0] = a[:, LANES:].astype(BF16)
    kb_ref[0] = mm(1280).astype(BF16)
    vb_ref[0] = mm(1536).astype(BF16)
    for j in range(2):
        a = mm(1792 + 256 * j)
        kd_ref[0, :, 256 * j:256 * j + LANES] = rope(a[:, :LANES]).astype(BF16)
        kd_ref[0, :, 256 * j + LANES:256 * (j + 1)] = rope(a[:, LANES:]).astype(BF16)
        a = mm(2304 + 256 * j)
        vd_ref[0, :, 256 * j:256 * (j + 1)] = a.astype(BF16)
        vdt_ref[0, 256 * j:256 * j + LANES, :] = a[:, :LANES].T.astype(BF16)
        vdt_ref[0, 256 * j + LANES:256 * (j + 1), :] = a[:, LANES:].T.astype(BF16)


def _projection(xc, g, modall, cos, sin, w_in, s_lat):
    b, sall, d = xc.shape
    tm = TOK_TILE
    nlat = s_lat // tm
    row = lambda width: pl.BlockSpec((1, tm, width), lambda i, j: (i, j, 0))
    widths = (512, 256, 512, 128, 128, 256, 256, 512, 512)
    return pl.pallas_call(
        _proj_kernel,
        grid=(b, sall // tm),
        in_specs=[row(d),
                  pl.BlockSpec((1, d), lambda i, j: (0, 0)),
                  pl.BlockSpec((1, 8, d), lambda i, j: (2 * i + (j >= nlat).astype(I32), 0, 0)),
                  pl.BlockSpec((tm, LANES), lambda i, j: (j, 0)),
                  pl.BlockSpec((tm, LANES), lambda i, j: (j, 0)),
                  pl.BlockSpec((d, IN_COLS), lambda i, j: (0, 0))],
        out_specs=[row(w) for w in widths] + [pl.BlockSpec((1, 512, tm), lambda i, j: (i, 0, j))],
        out_shape=([jax.ShapeDtypeStruct((b, sall, w), BF16) for w in widths]
                   + [jax.ShapeDtypeStruct((b, 512, sall), BF16)]),
        scratch_shapes=[pltpu.VMEM((tm, d), BF16)],
        compiler_params=_params(("arbitrary", "arbitrary")),
        name="projection",
    )(xc, g, modall, cos, sin, w_in)


def _half_mask(q, half):
    lane = lax.broadcasted_iota(I32, q.shape, 1)
    keep = (lane < HEAD_DIM) if half == 0 else (lane >= HEAD_DIM)
    return jnp.where(keep, q, jnp.zeros_like(q))


def _merge_halves(lo_part, hi_part):
    lane = lax.broadcasted_iota(I32, lo_part.shape, 1)
    return jnp.where(lane < HEAD_DIM, lo_part, hi_part)


def _gqa_rows(q_ref, g):
    return jnp.concatenate([q_ref[0, :, (2 * g) * LANES:(2 * g + 1) * LANES],
                            q_ref[0, :, (2 * g + 1) * LANES:(2 * g + 2) * LANES]], axis=0)


def _gqa_sink(sink_ref, g, tq):
    rowi = lax.broadcasted_iota(I32, (2 * tq, 1), 0)
    return jnp.where(rowi < tq, sink_ref[2 * g], sink_ref[2 * g + 1])


def _gqa_store(o_ref, g, o, tq):
    top, bot = o[:tq], o[tq:]
    if g == 0:
        chunk = _merge_halves(top, pltpu.roll(bot, HEAD_DIM, 1))
    else:
        chunk = _merge_halves(pltpu.roll(top, HEAD_DIM, 1), bot)
    o_ref[0, :, g * LANES:(g + 1) * LANES] = chunk.astype(BF16)


def _lam(lamv_ref, lam_init):
    v = lamv_ref[...]
    a = jnp.sum(v[0:1] * v[1:2], axis=-1, keepdims=True)
    b = jnp.sum(v[2:3] * v[3:4], axis=-1, keepdims=True)
    return jnp.exp(a) - jnp.exp(b) + lam_init


def _diff_post(o, subg, lam_init):
    return _rms(o, subg) * (1.0 - lam_init)


def _swa_kernel(sink_ref, q_ref, k_ref, v_ref, o_ref, *, s_lat, c_len):
    tq = ATT_TQ
    tk = tq + 2 * SWA_WINDOW
    q0 = pl.program_id(1) * tq
    ks = pl.multiple_of(jnp.clip(q0 - SWA_WINDOW, 0, s_lat - tk), SWA_WINDOW)
    kwin = k_ref[0, pl.ds(ks, tk), :]
    vwin = v_ref[0, pl.ds(ks, tk), :]
    kc = k_ref[0, s_lat:s_lat + c_len, :]
    vc = v_ref[0, s_lat:s_lat + c_len, :]
    qpos = q0 + lax.broadcasted_iota(I32, (tq, tk), 0)
    kpos = ks + lax.broadcasted_iota(I32, (tq, tk), 1)
    valid = jnp.abs(qpos - kpos) <= SWA_WINDOW
    valid = jnp.concatenate([valid, valid], axis=0)
    for g in range(2):
        q2 = _gqa_rows(q_ref, g)
        s_loc = jnp.where(valid, _dot_nt(q2, kwin), NEG_INF)
        s_ctx = _dot_nt(q2, kc)
        snk = _gqa_sink(sink_ref, g, tq)
        m = jnp.maximum(jnp.maximum(jnp.max(s_loc, axis=-1, keepdims=True),
                                    jnp.max(s_ctx, axis=-1, keepdims=True)), snk)
        e_loc = jnp.exp(s_loc - m)
        e_ctx = jnp.exp(s_ctx - m)
        den = (jnp.sum(e_loc, axis=-1, keepdims=True) + jnp.sum(e_ctx, axis=-1, keepdims=True)
               + jnp.exp(snk - m))
        o = (_dot(e_loc.astype(BF16), vwin) + _dot(e_ctx.astype(BF16), vc)) / den
        _gqa_store(o_ref, g, o, tq)


def _swa(sink, qa, ka, va, s_lat):
    b, sall, _ = qa.shape
    kv = pl.BlockSpec((1, sall, LANES), lambda i, j: (i, 0, 0))
    return pl.pallas_call(
        functools.partial(_swa_kernel, s_lat=s_lat, c_len=sall - s_lat),
        grid=(b, s_lat // ATT_TQ),
        in_specs=[pl.BlockSpec(memory_space=pltpu.SMEM),
                  pl.BlockSpec((1, ATT_TQ, 512), lambda i, j: (i, j, 0)), kv, kv],
        out_specs=pl.BlockSpec((1, ATT_TQ, 256), lambda i, j: (i, j, 0)),
        out_shape=jax.ShapeDtypeStruct((b, s_lat, 256), BF16),
        compiler_params=_params(("arbitrary", "arbitrary")),
        name="swa",
    )(sink, qa, ka, va)


def _na_kernel(q_ref, k_ref, v_ref, bm_ref, o_ref, *, s_lat, c_len):
    rows = s_lat // GRID_W
    nk = NA_KROWS * GRID_W
    r0 = pl.program_id(1) * NA_QROWS
    ks = pl.multiple_of(jnp.clip(r0 - NA_KH // 2, 0, rows - NA_KROWS) * GRID_W, GRID_W)
    for c in range(2):
        sl = slice(c * LANES, (c + 1) * LANES)
        kwin = k_ref[0, pl.ds(ks, nk), sl]
        vwin = v_ref[0, pl.ds(ks, nk), sl]
        kc = k_ref[0, s_lat:s_lat + c_len, sl]
        vc = v_ref[0, s_lat:s_lat + c_len, sl]
        q = q_ref[0, :, sl]
        outs = []
        for half in range(2):
            qm = _half_mask(q, half)
            s_loc = _dot_nt(qm, kwin) + bm_ref[0, 2 * c + half]
            s_ctx = _dot_nt(qm, kc)
            m = jnp.maximum(jnp.max(s_loc, axis=-1, keepdims=True), jnp.max(s_ctx, axis=-1, keepdims=True))
            e_loc = jnp.exp(s_loc - m)
            e_ctx = jnp.exp(s_ctx - m)
            den = jnp.sum(e_loc, axis=-1, keepdims=True) + jnp.sum(e_ctx, axis=-1, keepdims=True)
            outs.append((_dot(e_loc.astype(BF16), vwin) + _dot(e_ctx.astype(BF16), vc)) / den)
        o_ref[0, :, sl] = _merge_halves(outs[0], outs[1]).astype(BF16)


def _na_bias_mask(rpb, rows):
    nq = NA_QROWS * GRID_W
    nk = NA_KROWS * GRID_W
    col = np.arange(GRID_W)
    cstart = np.clip(col - NA_KW // 2, 0, GRID_W - NA_KW)
    col_ok = (col[None, :] >= cstart[:, None]) & (col[None, :] < cstart[:, None] + NA_KW)
    dc = np.clip(col[None, :] - col[:, None], -(NA_KW - 1), NA_KW - 1) + NA_KW - 1
    sel_c = np.where(col_ok[:, :, None], np.eye(2 * NA_KW - 1)[dc], 0.0)
    tabs = []
    for r0 in (0, NA_QROWS, rows - NA_QROWS):
        ksr = int(np.clip(r0 - NA_KH // 2, 0, rows - NA_KROWS))
        r = r0 + np.arange(NA_QROWS)
        kr = ksr + np.arange(NA_KROWS)
        rs = np.clip(r - NA_KH // 2, 0, rows - NA_KH)
        row_ok = (kr[None, :] >= rs[:, None]) & (kr[None, :] < rs[:, None] + NA_KH)
        dr = np.clip(kr[None, :] - r[:, None], -(NA_KH - 1), NA_KH - 1) + NA_KH - 1
        sel_r = np.where(row_ok[:, :, None], np.eye(2 * NA_KH - 1)[dr], 0.0)
        ok = (row_ok[:, None, :, None] & col_ok[None, :, None, :]).reshape(nq, nk)
        bias = jnp.einsum("hdc,rjd,qkc->hrqjk", rpb.astype(F32), jnp.asarray(sel_r, F32), jnp.asarray(sel_c, F32),
                          precision=lax.Precision.HIGHEST).reshape(N_HEADS, nq, nk)
        tabs.append(jnp.where(ok[None], bias, NEG_INF))
    return jnp.stack(tabs)


def _na(qb, kb, vb, bm, s_lat):
    b, sall, _ = qb.shape
    nq = NA_QROWS * GRID_W
    nsteps = s_lat // nq
    kv = pl.BlockSpec((1, sall, 256), lambda i, j: (i, 0, 0))

    def bm_map(i, j):
        return (jnp.where(j == 0, 0, jnp.where(j == nsteps - 1, 2, 1)), 0, 0, 0)

    return pl.pallas_call(
        functools.partial(_na_kernel, s_lat=s_lat, c_len=sall - s_lat),
        grid=(b, nsteps),
        in_specs=[pl.BlockSpec((1, nq, 256), lambda i, j: (i, j, 0)), kv, kv,
                  pl.BlockSpec((1, N_HEADS, nq, NA_KROWS * GRID_W), bm_map)],
        out_specs=pl.BlockSpec((1, nq, 256), lambda i, j: (i, j, 0)),
        out_shape=jax.ShapeDtypeStruct((b, s_lat, 256), BF16),
        compiler_params=_params(("arbitrary", "arbitrary")),
        name="na2d",
    )(qb, kb, vb, bm)


def _diff_kernel(lamv_ref, subg_ref, q_ref, k_ref, vt_ref, o_ref, e_scr, m_scr, *, n_keys, lam_init):
    tq = ATT_TQ
    chunks = [(c0, min(DIFF_KCHUNK, n_keys - c0)) for c0 in range(0, n_keys, DIFF_KCHUNK)]
    lam = _lam(lamv_ref, lam_init)
    stats = []
    for hh in range(DIFF_HEADS_PER_STEP):
        sl = slice(hh * LANES, (hh + 1) * LANES)
        q = q_ref[0, :, sl]
        q12 = jnp.concatenate([_half_mask(q, 0), _half_mask(q, 1)], axis=0)
        m = jnp.full((1, 2 * tq), NEG_INF, F32)
        den = jnp.zeros((1, 2 * tq), F32)
        for i, (c0, n) in enumerate(chunks):
            s = _dot_nt(k_ref[0, c0:c0 + n, sl], q12)
            m_new = jnp.maximum(m, jnp.max(s, axis=0, keepdims=True))
            e = jnp.exp2(s - m_new)
            e_scr[hh, c0:c0 + n, :] = e
            m_scr[hh, i:i + 1, :] = m_new
            den = den * jnp.exp2(m - m_new) + jnp.sum(e, axis=0, keepdims=True)
            m = m_new
        stats.append((m, den))
    for hh in range(DIFF_HEADS_PER_STEP):
        sl = slice(hh * LANES, (hh + 1) * LANES)
        m, den = stats[hh]
        acc = jnp.zeros((LANES, tq), F32)
        for i, (c0, n) in enumerate(chunks):
            f = jnp.exp2(m_scr[hh, i:i + 1, :] - m) / den
            p = e_scr[hh, c0:c0 + n, :tq] * f[:, :tq] - e_scr[hh, c0:c0 + n, tq:] * (lam * f[:, tq:])
            acc = acc + _dot(vt_ref[0, sl, c0:c0 + n], p.astype(BF16))
        o_ref[0, :, sl] = _diff_post(acc.T, subg_ref[...], lam_init).astype(BF16)


def _diff(lamv, subg, qd, kd, vdt, s_lat, lam_init):
    b, sall, _ = qd.shape
    nchunk = -(-sall // DIFF_KCHUNK)
    hps = DIFF_HEADS_PER_STEP
    wid = hps * LANES
    return pl.pallas_call(
        functools.partial(_diff_kernel, n_keys=sall, lam_init=lam_init),
        grid=(b, N_HEADS // hps, s_lat // ATT_TQ),
        in_specs=[pl.BlockSpec((8, LANES), lambda i, h, j: (0, 0)),
                  pl.BlockSpec((1, LANES), lambda i, h, j: (0, 0)),
                  pl.BlockSpec((1, ATT_TQ, wid), lambda i, h, j: (i, j, h)),
                  pl.BlockSpec((1, sall, wid), lambda i, h, j: (i, 0, h)),
                  pl.BlockSpec((1, wid, sall), lambda i, h, j: (i, h, 0))],
        out_specs=pl.BlockSpec((1, ATT_TQ, wid), lambda i, h, j: (i, j, h)),
        out_shape=jax.ShapeDtypeStruct((b, s_lat, 512), BF16),
        scratch_shapes=[pltpu.VMEM((hps, sall, 2 * ATT_TQ), F32),
                        pltpu.VMEM((hps, -(-nchunk // 8) * 8, 2 * ATT_TQ), F32)],
        compiler_params=_params(("arbitrary", "arbitrary", "arbitrary")),
        name="diff_attn",
    )(lamv, subg, qd, kd, vdt)


def _ctx_kernel(sink_ref, lamv_ref, subg_ref, qa_ref, ka_ref, va_ref, qb_ref, kb_ref, vb_ref,
                qd_ref, kd_ref, vd_ref, oa_ref, ob_ref, od_ref, *, lam_init):
    cl = qa_ref.shape[1]
    ka = ka_ref[0]
    va = va_ref[0]
    for g in range(2):
        q2 = _gqa_rows(qa_ref, g)
        s = _dot_nt(q2, ka)
        snk = _gqa_sink(sink_ref, g, cl)
        m = jnp.maximum(jnp.max(s, axis=-1, keepdims=True), snk)
        e = jnp.exp(s - m)
        den = jnp.sum(e, axis=-1, keepdims=True) + jnp.exp(snk - m)
        _gqa_store(oa_ref, g, _dot(e.astype(BF16), va) / den, cl)
    for c in range(2):
        sl = slice(c * LANES, (c + 1) * LANES)
        outs = []
        for half in range(2):
            s = _dot_nt(_half_mask(qb_ref[0, :, sl], half), kb_ref[0, :, sl])
            e = jnp.exp(s - jnp.max(s, axis=-1, keepdims=True))
            outs.append(_dot(e.astype(BF16), vb_ref[0, :, sl]) / jnp.sum(e, axis=-1, keepdims=True))
        ob_ref[0, :, sl] = _merge_halves(outs[0], outs[1]).astype(BF16)
    lam = _lam(lamv_ref, lam_init)
    for h in range(N_HEADS):
        sl = slice(h * LANES, (h + 1) * LANES)
        q = qd_ref[0, :, sl]
        q12 = jnp.concatenate([_half_mask(q, 0), _half_mask(q, 1)], axis=0)
        s = _dot_nt(q12, kd_ref[0, :, sl])
        e = jnp.exp2(s - jnp.max(s, axis=-1, keepdims=True))
        den = jnp.sum(e, axis=-1, keepdims=True)
        p = e[:cl] * (1.0 / den[:cl]) - e[cl:] * (lam / den[cl:])
        o = _dot(p.astype(BF16), vd_ref[0, :, sl])
        od_ref[0, :, sl] = _diff_post(o, subg_ref[...], lam_init).astype(BF16)


def _ctx_attention(sink, lamv, subg, qa, ka, va, qb, kb, vb, qd, kd, vd, s_lat, lam_init):
    b, sall, _ = qa.shape
    cl = sall - s_lat
    blk = s_lat // cl
    row = lambda width: pl.BlockSpec((1, cl, width), lambda i: (i, blk, 0))
    out = lambda width: pl.BlockSpec((1, cl, width), lambda i: (i, 0, 0))
    return pl.pallas_call(
        functools.partial(_ctx_kernel, lam_init=lam_init),
        grid=(b,),
        in_specs=[pl.BlockSpec(memory_space=pltpu.SMEM),
                  pl.BlockSpec((8, LANES), lambda i: (0, 0)),
                  pl.BlockSpec((1, LANES), lambda i: (0, 0)),
                  row(512), row(128), row(128), row(256), row(256), row(256), row(512), row(512), row(512)],
        out_specs=[out(256), out(256), out(512)],
        out_shape=[jax.ShapeDtypeStruct((b, cl, w), BF16) for w in (256, 256, 512)],
        compiler_params=_params(("arbitrary",)),
        name="ctx_attention",
    )(sink, lamv, subg, qa, ka, va, qb, kb, vb, qd, kd, vd)


def _first_argmax(v, iota, n):
    m = jnp.max(v, axis=0, keepdims=True)
    ix = jnp.min(jnp.where(v == m, iota, float(n)), axis=0, keepdims=True)
    return m, ix


def _router_kernel(xin_ref, wout_ref, g_ref, mod_ref, whi_ref, wlo_ref, bias_ref, tri_ref, ltri_ref, *refs,
                   tiles_per_batch, nlat):
    attn_refs, (x_ref, w_ref, slot_ref, meta_ref, cnt_ref) = refs[:-5], refs[-5:]

    @pl.when(pl.program_id(0) == 0)
    def _():
        cnt_ref[...] = jnp.zeros_like(cnt_ref)

    def residual(oa_ref, ob_ref, od_ref):
        attn = (_dot(oa_ref[0], wout_ref[0:256, :]) + _dot(ob_ref[0], wout_ref[256:512, :])
                + _dot(od_ref[0], wout_ref[512:1024, :]))
        x_ref[...] = xin_ref[0] + mod_ref[0, 2:3, :] * attn

    if len(attn_refs) == 3:
        residual(*attn_refs)
    else:
        is_ctx = (pl.program_id(0) % tiles_per_batch) >= nlat
        pl.when(jnp.logical_not(is_ctx))(lambda: residual(*attn_refs[:3]))
        pl.when(is_ctx)(lambda: residual(*attn_refs[3:6]))

    hf = _norm_mod(x_ref[...], g_ref[...], mod_ref[0, 3:4, :], mod_ref[0, 4:5, :])
    tm = hf.shape[0]
    h_hi, h_lo = _split_bf16(hf)
    whi = whi_ref[...]
    logits = _dot_nt(whi, h_hi) + _dot_nt(whi, h_lo) + _dot_nt(wlo_ref[...], h_hi)
    scores = _sigmoid(logits)
    biased = scores + bias_ref[...]
    gsz = N_EXPERTS // N_GROUPS
    iota_g = lax.broadcasted_iota(I32, (gsz, tm), 0).astype(F32)
    gscore = []
    for g in range(N_GROUPS):
        v = biased[g * gsz:(g + 1) * gsz]
        m1, i1 = _first_argmax(v, iota_g, gsz)
        m2 = jnp.max(jnp.where(iota_g == i1, -jnp.inf, v), axis=0, keepdims=True)
        gscore.append(m1 + m2)
    cur = jnp.concatenate(gscore, axis=0)
    iota_n = lax.broadcasted_iota(I32, (N_GROUPS, tm), 0).astype(F32)
    gsel = jnp.zeros((N_GROUPS, tm), F32)
    for _ in range(TOPK_GROUPS):
        _, ix = _first_argmax(cur, iota_n, N_GROUPS)
        hit = iota_n == ix
        gsel = jnp.where(hit, 1.0, gsel)
        cur = jnp.where(hit, -jnp.inf, cur)
    masked = jnp.concatenate(
        [jnp.where(gsel[g:g + 1] > 0.5, biased[g * gsz:(g + 1) * gsz], -jnp.inf) for g in range(N_GROUPS)],
        axis=0)
    iota_e = lax.broadcasted_iota(I32, (N_EXPERTS, tm), 0).astype(F32)
    onehot = jnp.zeros((N_EXPERTS, tm), F32)
    idxs, ws = [], []
    for _ in range(TOP_K):
        _, ix = _first_argmax(masked, iota_e, N_EXPERTS)
        hit = iota_e == ix
        ws.append(jnp.sum(jnp.where(hit, scores, 0.0), axis=0, keepdims=True))
        masked = jnp.where(hit, -jnp.inf, masked)
        onehot = jnp.where(hit, 1.0, onehot)
        idxs.append(ix)
    base = cnt_ref[...]
    rank = _dot(onehot.astype(BF16), tri_ref[...])
    n_col = jnp.sum(onehot, axis=1, keepdims=True)
    n_pad = jnp.floor((n_col + (ROW_CHUNK - 1)) * (1.0 / ROW_CHUNK)) * ROW_CHUNK
    start = _dot(ltri_ref[...], jnp.broadcast_to(n_pad, (N_EXPERTS, LANES)).astype(BF16))
    wsum = ws[0]
    for wk in ws[1:]:
        wsum = wsum + wk
    for k in range(TOP_K):
        slot = jnp.sum(jnp.where(iota_e == idxs[k], rank + start[:, 0:1], 0.0), axis=0, keepdims=True)
        slot_ref[0, k:k + 1, :] = slot.astype(I32)
        w_ref[0, k:k + 1, :] = ws[k] / wsum * ROUTED_SCALE
    lane = lax.broadcasted_iota(I32, (N_EXPERTS, LANES), 1)
    meta_ref[0] = jnp.where(lane == 0, n_pad, jnp.where(lane == 1, base, start))
    cnt_ref[...] = base + n_pad


def _router(xc, lat, ctx, w_out, g, modall, whi, wlo, bias, tri, ltri, s_lat):
    b, sall, d = xc.shape
    tm = TOK_TILE
    nlat = s_lat // tm
    tpb = nlat if ctx is None else sall // tm
    nt = b * tpb
    per_tok = pl.BlockSpec((1, TOP_K, tm), lambda i: (i, 0, 0))
    lat_row = lambda width: pl.BlockSpec((1, tm, width), lambda i: (i // tpb, jnp.minimum(i % tpb, nlat - 1), 0))
    ctx_row = lambda width: pl.BlockSpec((1, tm, width), lambda i: (i // tpb, 0, 0))
    widths = (256, 256, 512)
    attn_specs = [lat_row(w) for w in widths] + ([] if ctx is None else [ctx_row(w) for w in widths])

    def mod_map(i):
        return (2 * (i // tpb) + ((i % tpb) >= nlat).astype(I32), 0, 0)

    return pl.pallas_call(
        functools.partial(_router_kernel, tiles_per_batch=tpb, nlat=nlat),
        grid=(nt,),
        in_specs=[pl.BlockSpec((1, tm, d), lambda i: (i // tpb, i % tpb, 0)),
                  pl.BlockSpec((MIX_WIDTH, d), lambda i: (0, 0)),
                  pl.BlockSpec((1, d), lambda i: (0, 0)),
                  pl.BlockSpec((1, 8, d), mod_map),
                  pl.BlockSpec((N_EXPERTS, d), lambda i: (0, 0)),
                  pl.BlockSpec((N_EXPERTS, d), lambda i: (0, 0)),
                  pl.BlockSpec((N_EXPERTS, 1), lambda i: (0, 0)),
                  pl.BlockSpec((tm, tm), lambda i: (0, 0)),
                  pl.BlockSpec((N_EXPERTS, N_EXPERTS), lambda i: (0, 0))] + attn_specs,
        out_specs=[pl.BlockSpec((tm, d), lambda i: (i, 0)), per_tok, per_tok,
                   pl.BlockSpec((1, N_EXPERTS, LANES), lambda i: (i, 0, 0)),
                   pl.BlockSpec((N_EXPERTS, 1), lambda i: (0, 0))],
        out_shape=[jax.ShapeDtypeStruct((nt * tm, d), F32),
                   jax.ShapeDtypeStruct((nt, TOP_K, tm), F32),
                   jax.ShapeDtypeStruct((nt, TOP_K, tm), I32),
                   jax.ShapeDtypeStruct((nt, N_EXPERTS, LANES), F32),
                   jax.ShapeDtypeStruct((N_EXPERTS, 1), F32)],
        compiler_params=_params(("arbitrary",)),
        name="router",
    )(xc, w_out, g, modall, whi, wlo, bias, tri, ltri, *lat, *(() if ctx is None else ctx))


def _pack_bf16_pairs(v, is_bf16_valued=False):
    half = v.shape[1] // 2
    if not is_bf16_valued:
        v = v.astype(BF16).astype(F32)
    lo = lax.shift_right_logical(lax.bitcast_convert_type(v[:, :half], U32), jnp.uint32(16))
    hi = lax.bitcast_convert_type(v[:, half:], U32) & jnp.uint32(0xFFFF0000)
    return lo | hi


def _unpack_bf16_pairs(u):
    lo = lax.bitcast_convert_type(lax.shift_left(u, jnp.uint32(16)), F32).astype(BF16)
    hi = lax.bitcast_convert_type(u & jnp.uint32(0xFFFF0000), F32).astype(BF16)
    return lo, hi


def _used_chunks(meta_ref):
    return lax.shift_right_logical(_slots_used(meta_ref), ROW_CHUNK.bit_length() - 1)


def _start_chunk_copies(first, count, make_copy):
    pairs = lax.shift_right_logical(count, 1)

    def start_pair(j, carry):
        make_copy(first + 2 * j).start()
        make_copy(first + 2 * j + 1).start()
        return carry

    lax.fori_loop(0, pairs, start_pair, 0)

    @pl.when(count > 2 * pairs)
    def _():
        make_copy(first + count - 1).start()


def _wait_chunk_copies(total, copy_of_rows):
    group = lax.shift_right_logical(total, WAIT_GROUP.bit_length() - 1)

    def wait_group(j, carry):
        copy_of_rows(WAIT_GROUP * ROW_CHUNK).wait()
        return carry

    def wait_one(j, carry):
        copy_of_rows(ROW_CHUNK).wait()
        return carry

    lax.fori_loop(0, group, wait_group, 0)
    lax.fori_loop(0, total - group * WAIT_GROUP, wait_one, 0)


def _tile_slots(tm):
    return TOP_K * tm + ROW_CHUNK * N_EXPERTS


def _slots_used(meta_ref):
    last = N_EXPERTS - 1
    return meta_ref[0, 2, last] + meta_ref[0, 0, last]


def _dispatch_kernel(meta_ref, rows_ref, x_ref, g_ref, mod_ref, slot_ref, xs_ref, xc_scr, sem):
    hb = _norm_mod(x_ref[...], g_ref[...], mod_ref[0, 3:4, :], mod_ref[0, 4:5, :]).astype(BF16)
    tm = hb.shape[0]
    slot16 = slot_ref[0].astype(jnp.int16)

    total = _used_chunks(meta_ref)
    step = CHUNK_UNROLL * tm
    chunks_per_step = step // ROW_CHUNK

    def make_copy(j):
        src = pl.multiple_of(j * ROW_CHUNK, ROW_CHUNK)
        dst = pl.multiple_of(rows_ref[0, 0, j], ROW_CHUNK)
        return pltpu.make_async_copy(xc_scr.at[pl.ds(src, ROW_CHUNK)], xs_ref.at[pl.ds(dst, ROW_CHUNK)], sem)

    def select_rows(r2, carry):
        for u in range(CHUNK_UNROLL):
            r0 = pl.multiple_of((r2 * CHUNK_UNROLL + u) * tm, tm)
            row = (lax.broadcasted_iota(I32, (tm, tm), 0) + r0).astype(jnp.int16)
            sel = jnp.zeros((tm, tm), BF16)
            for k in range(TOP_K):
                sel = jnp.where(row == slot16[k:k + 1, :], jnp.ones((), BF16), sel)
            xc_scr[pl.ds(r0, tm), :] = _pack_bf16_pairs(_dot(sel, hb), is_bf16_valued=True)
        first = r2 * chunks_per_step
        _start_chunk_copies(first, jnp.clip(total - first, 0, chunks_per_step), make_copy)
        return carry

    lax.fori_loop(0, (_slots_used(meta_ref) + step - 1) // step, select_rows, 0)
    _wait_chunk_copies(total, lambda n: pltpu.make_async_copy(xc_scr.at[pl.ds(0, n)], xs_ref.at[pl.ds(0, n)], sem))


def _dispatch(meta, chunk_rows, slot, xflat, g, modall, n_rows, tiles_per_batch, nlat):
    t, d = xflat.shape
    tm = TOK_TILE
    nt = t // tm

    def mod_map(i):
        return (2 * (i // tiles_per_batch) + ((i % tiles_per_batch) >= nlat).astype(I32), 0, 0)

    return pl.pallas_call(
        _dispatch_kernel,
        grid=(nt,),
        in_specs=[pl.BlockSpec((1, 3, N_EXPERTS), lambda i: (i, 0, 0), memory_space=pltpu.SMEM),
                  pl.BlockSpec((1, 1, chunk_rows.shape[2]), lambda i: (i, 0, 0), memory_space=pltpu.SMEM),
                  pl.BlockSpec((tm, d), lambda i: (i, 0)),
                  pl.BlockSpec((1, d), lambda i: (0, 0)),
                  pl.BlockSpec((1, 8, d), mod_map),
                  pl.BlockSpec((1, TOP_K, tm), lambda i: (i, 0, 0))],
        out_specs=pl.BlockSpec(memory_space=pl.ANY),
        out_shape=jax.ShapeDtypeStruct((n_rows, d // 2), U32),
        scratch_shapes=[pltpu.VMEM((_tile_slots(tm), d // 2), U32), pltpu.SemaphoreType.DMA(())],
        compiler_params=_params(("arbitrary",)),
        name="moe_dispatch",
    )(meta, chunk_rows, xflat, g, modall, slot)


def _expert_kernel(blk_e_ref, nvalid_ref, nact_ref, next_e_ref, has_next_ref, parity_ref,
                   xs_ref, wg_any, wu_any, wd_any, ys_ref, wg_s, wu_s, wd_s, wg_buf, wu_buf, wd_buf, sem, *, layer):
    b = pl.program_id(0)
    e = blk_e_ref[b]
    changed = jnp.logical_or(b == 0, e != blk_e_ref[jnp.maximum(b - 1, 0)])
    par = parity_ref[b]

    def weight_copies(expert, p):
        return [pltpu.make_async_copy(src.at[layer, expert], dst.at[p], sem.at[p])
                for src, dst in ((wg_any, wg_buf), (wu_any, wu_buf), (wd_any, wd_buf))]

    @pl.when(b == 0)
    def _():
        for c in weight_copies(e, par):
            c.start()

    @pl.when(changed)
    def _():
        for c in weight_copies(e, par):
            c.wait()
        wg_s[...] = wg_buf[par].astype(BF16)
        wu_s[...] = wu_buf[par].astype(BF16)
        wd_s[...] = wd_buf[par].astype(BF16)

        @pl.when(has_next_ref[b] == 1)
        def _():
            for c in weight_copies(next_e_ref[b], 1 - par):
                c.start()

    half = xs_ref.shape[1]
    n_here = jnp.where(b < nact_ref[0], nvalid_ref[b], 0)

    @pl.when(n_here > 0)
    def _():
        row = lax.broadcasted_iota(I32, xs_ref.shape, 0)
        xs = jnp.where(row < n_here, xs_ref[...], jnp.zeros(xs_ref.shape, U32))
        x_lo, x_hi = _unpack_bf16_pairs(xs)
        gate = _dot(x_lo, wg_s[0:half, :]) + _dot(x_hi, wg_s[half:, :])
        up = _dot(x_lo, wu_s[0:half, :]) + _dot(x_hi, wu_s[half:, :])
        mid = (gate * _sigmoid(gate)) * up
        ys_ref[...] = _pack_bf16_pairs(_dot(mid.astype(BF16), wd_s[...]))

    @pl.when(n_here == 0)
    def _():
        ys_ref[...] = jnp.zeros_like(ys_ref)


def _experts(blk_e, nvalid, nact, xs, w_gate, w_up, w_down, layer):
    n_rows, half = xs.shape
    _, _, d, f = w_gate.shape
    n_blk = n_rows // MOE_BLOCK
    first_of_next = jnp.sum((blk_e[None, :] <= blk_e[:, None]).astype(I32), axis=1)
    has_next = (first_of_next < n_blk).astype(I32)
    next_e = blk_e[jnp.minimum(first_of_next, n_blk - 1)]
    starts = jnp.concatenate([jnp.zeros((1,), I32), (blk_e[1:] != blk_e[:-1]).astype(I32)])
    parity = jnp.cumsum(starts) % 2
    anyspec = pl.BlockSpec(memory_space=pl.ANY)
    grid_spec = pltpu.PrefetchScalarGridSpec(
        num_scalar_prefetch=6,
        grid=(n_blk,),
        in_specs=[pl.BlockSpec((MOE_BLOCK, half), lambda i, be, nv, na, *_: (jnp.minimum(i, na[0] - 1), 0)),
                  anyspec, anyspec, anyspec],
        out_specs=pl.BlockSpec((MOE_BLOCK, half), lambda i, be, nv, na, *_: (jnp.minimum(i, na[0]), 0)),
        scratch_shapes=[pltpu.VMEM((d, f), BF16), pltpu.VMEM((d, f), BF16), pltpu.VMEM((f, d), BF16),
                        pltpu.VMEM((2, d, f), F32), pltpu.VMEM((2, d, f), F32), pltpu.VMEM((2, f, d), F32),
                        pltpu.SemaphoreType.DMA((2,))],
    )
    return pl.pallas_call(
        functools.partial(_expert_kernel, layer=layer),
        grid_spec=grid_spec,
        out_shape=jax.ShapeDtypeStruct((n_rows, half), U32),
        compiler_params=_params(("arbitrary",)),
        name="moe_experts",
    )(blk_e, nvalid, nact, next_e, has_next, parity.astype(I32), xs, w_gate, w_up, w_down)


def _combine_kernel(meta_ref, rows_ref, x_ref, g_ref, mod_ref, spos_ref, wt_ref, wsg_ref, wsu_ref, wsd_ref,
                    gfin_ref, ys_ref, o_ref, y_scr, acc_scr, sb_scr, wb_scr, sem, *, final_norm):
    tm, d = x_ref.shape
    half = d // 2

    @pl.when(pl.program_id(0) == 0)
    def _():
        y_scr[...] = jnp.zeros_like(y_scr)

    step = CHUNK_UNROLL * COMBINE_KCHUNK
    chunks_per_step = step // ROW_CHUNK
    group_shift = chunks_per_step.bit_length() - 1

    def make_copy(j):
        src = pl.multiple_of(rows_ref[0, 0, j], ROW_CHUNK)
        dst = pl.multiple_of(j * ROW_CHUNK, ROW_CHUNK)
        return pltpu.make_async_copy(ys_ref.at[pl.ds(src, ROW_CHUNK)], y_scr.at[pl.ds(dst, ROW_CHUNK)],
                                     sem.at[lax.shift_right_logical(j, group_shift)])

    total = _used_chunks(meta_ref)
    _start_chunk_copies(0, total, make_copy)

    x = x_ref[...]
    hb = _norm_mod(x, g_ref[...], mod_ref[0, 3:4, :], mod_ref[0, 4:5, :]).astype(BF16)
    gate = _dot(hb, wsg_ref[...])
    mid = (gate * _sigmoid(gate)) * _dot(hb, wsu_ref[...])
    acc_scr[...] = _dot(mid.astype(BF16), wsd_ref[...])

    n_used = _slots_used(meta_ref)
    for k in range(TOP_K):
        sb_scr[k] = jnp.broadcast_to(spos_ref[0, :, k:k + 1], (tm, LANES)).astype(jnp.int16)
        wb_scr[k] = jnp.broadcast_to(wt_ref[0, :, k:k + 1], (tm, LANES)).astype(BF16)

    def chunk(j2, carry):
        _wait_chunk_copies(jnp.clip(total - j2 * chunks_per_step, 0, chunks_per_step),
                           lambda n: pltpu.make_async_copy(ys_ref.at[pl.ds(0, n)], y_scr.at[pl.ds(0, n)], sem.at[j2]))
        lo_sum = jnp.zeros((tm, half), F32)
        hi_sum = jnp.zeros((tm, half), F32)
        for u in range(CHUNK_UNROLL):
            r0 = pl.multiple_of((j2 * CHUNK_UNROLL + u) * COMBINE_KCHUNK, COMBINE_KCHUNK)
            parts = []
            for c in range(COMBINE_KCHUNK // LANES):
                col = (lax.broadcasted_iota(I32, (tm, LANES), 1) + (r0 + c * LANES)).astype(jnp.int16)
                p = jnp.zeros((tm, LANES), BF16)
                for k in range(TOP_K):
                    p = jnp.where(col == sb_scr[k], wb_scr[k], p)
                parts.append(p)
            pb = jnp.concatenate(parts, axis=1)
            y_lo, y_hi = _unpack_bf16_pairs(y_scr[pl.ds(r0, COMBINE_KCHUNK), :])
            lo_sum = lo_sum + _dot(pb, y_lo)
            hi_sum = hi_sum + _dot(pb, y_hi)
        acc_scr[:, :half] += lo_sum
        acc_scr[:, half:] += hi_sum
        return carry

    lax.fori_loop(0, (n_used + step - 1) // step, chunk, 0)
    out = x + mod_ref[0, 5:6, :] * acc_scr[...]
    if final_norm:
        out = _rms(out, gfin_ref[...])
    o_ref[...] = out


def _combine(meta, chunk_rows, spos_t, wt, xflat, g, modall, wsg, wsu, wsd, gfin, ys, tiles_per_batch, nlat,
             final_norm):
    t, d = xflat.shape
    f = wsg.shape[1]
    tm = TOK_TILE
    nt = t // tm
    y_rows = _tile_slots(tm)
    assert y_rows % COMBINE_KCHUNK == 0

    def mod_map(i):
        return (2 * (i // tiles_per_batch) + ((i % tiles_per_batch) >= nlat).astype(I32), 0, 0)

    return pl.pallas_call(
        functools.partial(_combine_kernel, final_norm=final_norm),
        grid=(nt,),
        in_specs=[pl.BlockSpec((1, 3, N_EXPERTS), lambda i: (i, 0, 0), memory_space=pltpu.SMEM),
                  pl.BlockSpec((1, 1, chunk_rows.shape[2]), lambda i: (i, 0, 0), memory_space=pltpu.SMEM),
                  pl.BlockSpec((tm, d), lambda i: (i, 0)),
                  pl.BlockSpec((1, d), lambda i: (0, 0)),
                  pl.BlockSpec((1, 8, d), mod_map),
                  pl.BlockSpec((1, tm, TOP_K), lambda i: (i, 0, 0)),
                  pl.BlockSpec((1, tm, TOP_K), lambda i: (i, 0, 0)),
                  pl.BlockSpec((d, f), lambda i: (0, 0)),
                  pl.BlockSpec((d, f), lambda i: (0, 0)),
                  pl.BlockSpec((f, d), lambda i: (0, 0)),
                  pl.BlockSpec((1, d), lambda i: (0, 0)),
                  pl.BlockSpec(memory_space=pl.ANY)],
        out_specs=pl.BlockSpec((tm, d), lambda i: (i, 0)),
        out_shape=jax.ShapeDtypeStruct((t, d), F32),
        scratch_shapes=[pltpu.VMEM((y_rows, d // 2), U32), pltpu.VMEM((tm, d), F32),
                        pltpu.VMEM((TOP_K, tm, LANES), jnp.int16), pltpu.VMEM((TOP_K, tm, LANES), BF16),
                        pltpu.SemaphoreType.DMA((y_rows // (CHUNK_UNROLL * COMBINE_KCHUNK),))],
        compiler_params=_params(("arbitrary",)),
        name="moe_combine",
    )(meta, chunk_rows, xflat, g, modall, spos_t, wt, wsg, wsu, wsd, gfin, ys)


def _rope_tables(s_lat, c_len):
    n_freq = HEAD_DIM // 4
    inv = 1.0 / (ROPE_BASE ** (jnp.arange(n_freq, dtype=F32) / n_freq))
    t = jnp.arange(s_lat)
    pos = jnp.stack([(t // GRID_W).astype(F32), (t % GRID_W).astype(F32)], axis=1)
    lane = np.arange(LANES)
    ang = pos[:, (lane % HEAD_DIM) // 32] * inv[lane % 16][None, :]
    sign = jnp.asarray(np.where(lane % 32 < 16, -1.0, 1.0), F32)
    cos = jnp.concatenate([jnp.cos(ang), jnp.ones((c_len, LANES), F32)], axis=0)
    sin = jnp.concatenate([jnp.sin(ang) * sign, jnp.zeros((c_len, LANES), F32)], axis=0)
    return cos, sin


def _moe(xc, lat, ctx, lp, modall, gfin, s_lat, final_norm):
    d = xc.shape[2]
    nlat = s_lat // TOK_TILE
    tiles_per_batch = nlat if ctx is None else xc.shape[1] // TOK_TILE
    g = lp["g_ffn"].reshape(1, d)
    w_r = lp["w_router"].T
    whi = w_r.astype(BF16)
    wlo = (w_r - whi.astype(F32)).astype(BF16)
    tri = jnp.asarray(np.triu(np.ones((TOK_TILE, TOK_TILE), np.float32), 1), BF16)
    ltri = jnp.asarray(np.tril(np.ones((N_EXPERTS, N_EXPERTS), np.float32), -1), BF16)
    xflat, w, slot, meta, cnt = _router(xc, lat, ctx, lp["w_out"].astype(BF16), g, modall, whi, wlo,
                                        lp["router_bias"].reshape(N_EXPERTS, 1), tri, ltri, s_lat)
    t = xflat.shape[0]
    nt = t // TOK_TILE
    meta = jnp.transpose(meta[:, :, :3].astype(I32), (0, 2, 1))
    rows_e = cnt[:, 0].astype(I32)
    seg = (rows_e + MOE_BLOCK - 1) // MOE_BLOCK * MOE_BLOCK
    pend = jnp.cumsum(seg)
    pstart = (pend - seg).astype(I32)
    n_blk = (t * TOP_K + ROW_CHUNK * N_EXPERTS * nt) // MOE_BLOCK + N_EXPERTS
    blk_row0 = jnp.arange(n_blk, dtype=I32) * MOE_BLOCK
    blk_e = jnp.minimum(jnp.sum((pend[None, :] <= blk_row0[:, None]).astype(I32), axis=1), N_EXPERTS - 1)
    of_blk = blk_e[:, None] == jnp.arange(N_EXPERTS, dtype=I32)[None, :]
    seg_end = jnp.sum(jnp.where(of_blk, (pstart + rows_e)[None, :], 0), axis=1)
    nvalid = jnp.clip(seg_end - blk_row0, 0, MOE_BLOCK).astype(I32)
    nact = (pend[-1] // MOE_BLOCK).astype(I32).reshape(1)
    n_pad, base, start = meta[:, 0, :], meta[:, 1, :], meta[:, 2, :]
    chunk_slot = jnp.arange(_tile_slots(TOK_TILE) // ROW_CHUNK, dtype=I32) * ROW_CHUNK
    s = chunk_slot[None, :, None]
    in_run = (start[:, None, :] <= s) & (s < (start + n_pad)[:, None, :])
    row0 = pstart[None, :] + base - start
    chunk_rows = (jnp.sum(jnp.where(in_run, row0[:, None, :], 0), axis=2) + chunk_slot[None, :])[:, None, :]
    xs = _dispatch(meta, chunk_rows, slot, xflat, g, modall, n_blk * MOE_BLOCK, tiles_per_batch, nlat)
    ys = _experts(blk_e, nvalid, nact, xs, lp["w_gate"], lp["w_up"], lp["w_down"], lp["layer"])
    return _combine(meta, chunk_rows, jnp.transpose(slot, (0, 2, 1)), jnp.transpose(w, (0, 2, 1)), xflat, g, modall,
                    lp["ws_gate"].astype(BF16), lp["ws_up"].astype(BF16), lp["ws_down"].astype(BF16), gfin, ys,
                    tiles_per_batch, nlat, final_norm)


def _layer(xc, cs, cos, sin, lp, layer_idx, last, s_lat, gfin):
    b, sall, d = xc.shape
    c_len = sall - s_lat
    lam_init = 0.8 - 0.6 * math.exp(-0.3 * layer_idx)
    mod = _modulation(cs, lp["w_mod"], lp["b_mod"])
    mod_lat = mod[:b].reshape(b, 1, 6, d)
    mod_ctx = jnp.broadcast_to(mod[b].reshape(1, 1, 6, d), (b, 1, 6, d))
    modall = jnp.concatenate([mod_lat, mod_ctx], axis=1)
    modall = jnp.pad(modall, ((0, 0), (0, 0), (0, 2), (0, 0))).reshape(2 * b, 8, d)

    qa, qb, qd, ka, va, kb, vb, kd, vd, vdt = _projection(
        xc, lp["g_mix"].reshape(1, d), modall, cos, sin, lp["w_in"].astype(BF16), s_lat)
    sink = lp["attn_sink"].astype(F32)
    lamv = jnp.zeros((8, LANES), F32).at[0:4, 0:HEAD_DIM].set(
        jnp.stack([lp["lam_q1"], lp["lam_k1"], lp["lam_q2"], lp["lam_k2"]]).astype(F32))
    subg = lp["subln_g"].reshape(1, LANES).astype(F32)
    oa = _swa(sink, qa, ka, va, s_lat)
    ob = _na(qb, kb, vb, _na_bias_mask(lp["na_rpb"], s_lat // GRID_W), s_lat)
    od = _diff(lamv, subg, qd, kd, vdt, s_lat, lam_init)
    ctx_out = None
    if not last:
        ctx_out = _ctx_attention(sink, lamv, subg, qa, ka, va, qb, kb, vb, qd, kd, vd, s_lat, lam_init)
    n_rows = s_lat if last else sall
    y = _moe(xc, (oa, ob, od), ctx_out, lp, modall, gfin, s_lat, last)
    return y.reshape(b, n_rows, d)


_LAYER_KEYS = ("w_mod", "b_mod", "g_mix", "g_ffn", "w_in", "w_out", "attn_sink", "na_rpb", "lam_q1", "lam_k1",
               "lam_q2", "lam_k2", "subln_g", "w_router", "router_bias", "w_gate", "w_up", "w_down",
               "ws_gate", "ws_up", "ws_down")


def kernel(x, c, ctx, c_ctx, w_mod, b_mod, g_mix, g_ffn, w_in, w_out, attn_sink, na_rpb, lam_q1, lam_k1, lam_q2,
           lam_k2, subln_g, w_router, router_bias, w_gate, w_up, w_down, ws_gate, ws_up, ws_down, g_final):
    stacked = dict(zip(_LAYER_KEYS, (w_mod, b_mod, g_mix, g_ffn, w_in, w_out, attn_sink, na_rpb, lam_q1, lam_k1,
                                     lam_q2, lam_k2, subln_g, w_router, router_bias, w_gate, w_up, w_down,
                                     ws_gate, ws_up, ws_down)))
    b, s_lat, d = x.shape
    c_len = ctx.shape[1]
    depth = w_mod.shape[0]
    assert s_lat % (NA_QROWS * GRID_W) == 0 and s_lat % c_len == 0 and c_len == TOK_TILE
    cs = jnp.zeros((16, d), F32).at[:b].set(c).at[b].set(c_ctx)
    cos, sin = _rope_tables(s_lat, c_len)
    xc = jnp.concatenate([x, ctx], axis=1)
    gfin = g_final.reshape(1, d)
    for i in range(depth):
        big = ("w_gate", "w_up", "w_down")
        lp = {k: (v if k in big else v[i]) for k, v in stacked.items()}
        lp["layer"] = i
        xc = _layer(xc, cs, cos, sin, lp, i, i == depth - 1, s_lat, gfin)
    return xc
```

```python
import functools
import math

import numpy as np
import jax
import jax.numpy as jnp
from jax import lax
from jax.experimental import pallas as pl
from jax.experimental.pallas import tpu as pltpu

F32 = jnp.float32
BF16 = jnp.bfloat16
I32 = jnp.int32
U32 = jnp.uint32

GRID_W = 64
HEAD_DIM = 64
ROPE_BASE = 10000.0
SWA_WINDOW = 128
NA_KH = 8
NA_KW = 16
N_HEADS = 4
N_EXPERTS = 256
TOP_K = 8
N_GROUPS = 8
TOPK_GROUPS = 4
ROUTED_SCALE = 2.5
MOE_BLOCK = 1024
ROW_CHUNK = 8
WAIT_GROUP = 16
COMBINE_KCHUNK = 512
CHUNK_UNROLL = 2
EPS = 1e-6
NEG_INF = -1e30
LOG2E = math.log2(math.e)
Q_COLS = 1024
IN_COLS = 2816
MIX_WIDTH = 1024

LANES = 128
TOK_TILE = 256
ATT_TQ = 256
NA_QROWS = 4
NA_KROWS = NA_QROWS + NA_KH - 1
DIFF_KCHUNK = 256
DIFF_HEADS_PER_STEP = 2
VMEM_LIMIT = 48 * 1024 * 1024


def _params(sem):
    return pltpu.CompilerParams(dimension_semantics=sem, vmem_limit_bytes=VMEM_LIMIT)


def _dot(a, b):
    return jnp.dot(a, b, preferred_element_type=F32)


def _dot_nt(a, b):
    return lax.dot_general(a, b, (((1,), (1,)), ((), ())), preferred_element_type=F32)


def _split_bf16(a):
    hi = a.astype(BF16)
    lo = (a - hi.astype(F32)).astype(BF16)
    return hi, lo


def _sigmoid(x):
    return 1.0 / (1.0 + jnp.exp(-x))


def _rms(x, g):
    return x * lax.rsqrt(jnp.mean(x * x, axis=-1, keepdims=True) + EPS) * g


def _norm_mod(x, g, shift, scale):
    return _rms(x, g) * (1.0 + scale) + shift


def _mod_kernel(c_ref, w_ref, b_ref, o_ref):
    c = c_ref[...]
    a_hi, a_lo = _split_bf16(c * _sigmoid(c))
    w_hi, w_lo = _split_bf16(w_ref[...])
    o_ref[...] = _dot(a_hi, w_hi) + _dot(a_hi, w_lo) + _dot(a_lo, w_hi) + b_ref[...]


def _modulation(cs, w_mod, b_mod):
    n, d = cs.shape
    cols = w_mod.shape[1]
    tn = 1536
    return pl.pallas_call(
        _mod_kernel,
        grid=(cols // tn,),
        in_specs=[pl.BlockSpec((n, d), lambda j: (0, 0)),
                  pl.BlockSpec((d, tn), lambda j: (0, j)),
                  pl.BlockSpec((1, tn), lambda j: (0, j))],
        out_specs=pl.BlockSpec((n, tn), lambda j: (0, j)),
        out_shape=jax.ShapeDtypeStruct((n, cols), F32),
        compiler_params=_params(("arbitrary",)),
        name="modulation",
    )(cs, w_mod, b_mod.reshape(1, cols))


def _proj_kernel(x_ref, g_ref, mod_ref, cos_ref, sin_ref, w_ref,
                 qa_ref, qb_ref, qd_ref, ka_ref, va_ref, kb_ref, vb_ref, kd_ref, vd_ref, vdt_ref, h_scr):
    h = _norm_mod(x_ref[0], g_ref[...], mod_ref[0, 0:1, :], mod_ref[0, 1:2, :])
    h_scr[...] = h.astype(BF16)
    cos = cos_ref[...]
    sin = sin_ref[...]
    tm = h.shape[0]
    lane = lax.broadcasted_iota(I32, (tm, LANES), 1)
    first16 = (lane & 16) == 0
    lo64 = lane < HEAD_DIM

    def rope(v):
        partner = jnp.where(first16, pltpu.roll(v, LANES - 16, 1), pltpu.roll(v, 16, 1))
        return v * cos + partner * sin

    def mm(c0):
        return _dot(h_scr[...], w_ref[:, c0:c0 + 2 * LANES])

    a = mm(0) * (HEAD_DIM ** -0.5)
    c0 = rope(a[:, :LANES])
    c1 = rope(a[:, LANES:])
    zero = jnp.zeros_like(c0)
    qa_ref[0, :, 0 * LANES:1 * LANES] = jnp.where(lo64, c0, zero).astype(BF16)
    qa_ref[0, :, 1 * LANES:2 * LANES] = jnp.where(lo64, pltpu.roll(c0, HEAD_DIM, 1), zero).astype(BF16)
    qa_ref[0, :, 2 * LANES:3 * LANES] = jnp.where(lo64, zero, pltpu.roll(c1, HEAD_DIM, 1)).astype(BF16)
    qa_ref[0, :, 3 * LANES:4 * LANES] = jnp.where(lo64, zero, c1).astype(BF16)
    qb_ref[0] = (mm(256) * (HEAD_DIM ** -0.5)).astype(BF16)
    for j in range(2):
        a = mm(512 + 256 * j) * (HEAD_DIM ** -0.5 * LOG2E)
        qd_ref[0, :, 256 * j:256 * j + LANES] = rope(a[:, :LANES]).astype(BF16)
        qd_ref[0, :, 256 * j + LANES:256 * (j + 1)] = rope(a[:, LANES:]).astype(BF16)
    a = mm(1024)
    ka_ref[0] = rope(a[:, :LANES]).astype(BF16)
    va_ref[0] = a[:, LANES:].astype(BF16)
    kb_ref[0] = mm(1280).astype(BF16)
    vb_ref[0] = mm(1536).astype(BF16)
    for j in range(2):
        a = mm(1792 + 256 * j)
        kd_ref[0, :, 256 * j:256 * j + LANES] = rope(a[:, :LANES]).astype(BF16)
        kd_ref[0, :, 256 * j + LANES:256 * (j + 1)] = rope(a[:, LANES:]).astype(BF16)
        a = mm(2304 + 256 * j)
        vd_ref[0, :, 256 * j:256 * (j + 1)] = a.astype(BF16)
        vdt_ref[0, 256 * j:256 * j + LANES, :] = a[:, :LANES].T.astype(BF16)
        vdt_ref[0, 256 * j + LANES:256 * (j + 1), :] = a[:, LANES:].T.astype(BF16)


def _projection(xc, g, modall, cos, sin, w_in, s_lat):
    b, sall, d = xc.shape
    tm = TOK_TILE
    nlat = s_lat // tm
    row = lambda width: pl.BlockSpec((1, tm, width), lambda i, j: (i, j, 0))
    widths = (512, 256, 512, 128, 128, 256, 256, 512, 512)
    return pl.pallas_call(
        _proj_kernel,
        grid=(b, sall // tm),
        in_specs=[row(d),
                  pl.BlockSpec((1, d), lambda i, j: (0, 0)),
                  pl.BlockSpec((1, 8, d), lambda i, j: (2 * i + (j >= nlat).astype(I32), 0, 0)),
                  pl.BlockSpec((tm, LANES), lambda i, j: (j, 0)),
                  pl.BlockSpec((tm, LANES), lambda i, j: (j, 0)),
                  pl.BlockSpec((d, IN_COLS), lambda i, j: (0, 0))],
        out_specs=[row(w) for w in widths] + [pl.BlockSpec((1, 512, tm), lambda i, j: (i, 0, j))],
        out_shape=([jax.ShapeDtypeStruct((b, sall, w), BF16) for w in widths]
                   + [jax.ShapeDtypeStruct((b, 512, sall), BF16)]),
        scratch_shapes=[pltpu.VMEM((tm, d), BF16)],
        compiler_params=_params(("arbitrary", "arbitrary")),
        name="projection",
    )(xc, g, modall, cos, sin, w_in)


def _half_mask(q, half):
    lane = lax.broadcasted_iota(I32, q.shape, 1)
    keep = (lane < HEAD_DIM) if half == 0 else (lane >= HEAD_DIM)
    return jnp.where(keep, q, jnp.zeros_like(q))


def _merge_halves(lo_part, hi_part):
    lane = lax.broadcasted_iota(I32, lo_part.shape, 1)
    return jnp.where(lane < HEAD_DIM, lo_part, hi_part)


def _gqa_rows(q_ref, g):
    return jnp.concatenate([q_ref[0, :, (2 * g) * LANES:(2 * g + 1) * LANES],
                            q_ref[0, :, (2 * g + 1) * LANES:(2 * g + 2) * LANES]], axis=0)


def _gqa_sink(sink_ref, g, tq):
    rowi = lax.broadcasted_iota(I32, (2 * tq, 1), 0)
    return jnp.where(rowi < tq, sink_ref[2 * g], sink_ref[2 * g + 1])


def _gqa_store(o_ref, g, o, tq):
    top, bot = o[:tq], o[tq:]
    if g == 0:
        chunk = _merge_halves(top, pltpu.roll(bot, HEAD_DIM, 1))
    else:
        chunk = _merge_halves(pltpu.roll(top, HEAD_DIM, 1), bot)
    o_ref[0, :, g * LANES:(g + 1) * LANES] = chunk.astype(BF16)


def _lam(lamv_ref, lam_init):
    v = lamv_ref[...]
    a = jnp.sum(v[0:1] * v[1:2], axis=-1, keepdims=True)
    b = jnp.sum(v[2:3] * v[3:4], axis=-1, keepdims=True)
    return jnp.exp(a) - jnp.exp(b) + lam_init


def _diff_post(o, subg, lam_init):
    return _rms(o, subg) * (1.0 - lam_init)


def _swa_kernel(sink_ref, q_ref, k_ref, v_ref, o_ref, *, s_lat, c_len):
    tq = ATT_TQ
    tk = tq + 2 * SWA_WINDOW
    q0 = pl.program_id(1) * tq
    ks = pl.multiple_of(jnp.clip(q0 - SWA_WINDOW, 0, s_lat - tk), SWA_WINDOW)
    kwin = k_ref[0, pl.ds(ks, tk), :]
    vwin = v_ref[0, pl.ds(ks, tk), :]
    kc = k_ref[0, s_lat:s_lat + c_len, :]
    vc = v_ref[0, s_lat:s_lat + c_len, :]
    qpos = q0 + lax.broadcasted_iota(I32, (tq, tk), 0)
    kpos = ks + lax.broadcasted_iota(I32, (tq, tk), 1)
    valid = jnp.abs(qpos - kpos) <= SWA_WINDOW
    valid = jnp.concatenate([valid, valid], axis=0)
    for g in range(2):
        q2 = _gqa_rows(q_ref, g)
        s_loc = jnp.where(valid, _dot_nt(q2, kwin), NEG_INF)
        s_ctx = _dot_nt(q2, kc)
        snk = _gqa_sink(sink_ref, g, tq)
        m = jnp.maximum(jnp.maximum(jnp.max(s_loc, axis=-1, keepdims=True),
                                    jnp.max(s_ctx, axis=-1, keepdims=True)), snk)
        e_loc = jnp.exp(s_loc - m)
        e_ctx = jnp.exp(s_ctx - m)
        den = (jnp.sum(e_loc, axis=-1, keepdims=True) + jnp.sum(e_ctx, axis=-1, keepdims=True)
               + jnp.exp(snk - m))
        o = (_dot(e_loc.astype(BF16), vwin) + _dot(e_ctx.astype(BF16), vc)) / den
        _gqa_store(o_ref, g, o, tq)


def _na_kernel(q_ref, k_ref, v_ref, bm_ref, o_ref, *, s_lat, c_len):
    rows = s_lat // GRID_W
    nk = NA_KROWS * GRID_W
    r0 = pl.program_id(1) * NA_QROWS
    ks = pl.multiple_of(jnp.clip(r0 - NA_KH // 2, 0, rows - NA_KROWS) * GRID_W, GRID_W)
    for c in range(2):
        sl = slice(c * LANES, (c + 1) * LANES)
        kwin = k_ref[0, pl.ds(ks, nk), sl]
        vwin = v_ref[0, pl.ds(ks, nk), sl]
        kc = k_ref[0, s_lat:s_lat + c_len, sl]
        vc = v_ref[0, s_lat:s_lat + c_len, sl]
        q = q_ref[0, :, sl]
        outs = []
        for half in range(2):
            qm = _half_mask(q, half)
            s_loc = _dot_nt(qm, kwin) + bm_ref[0, 2 * c + half]
            s_ctx = _dot_nt(qm, kc)
            m = jnp.maximum(jnp.max(s_loc, axis=-1, keepdims=True), jnp.max(s_ctx, axis=-1, keepdims=True))
            e_loc = jnp.exp(s_loc - m)
            e_ctx = jnp.exp(s_ctx - m)
            den = jnp.sum(e_loc, axis=-1, keepdims=True) + jnp.sum(e_ctx, axis=-1, keepdims=True)
            outs.append((_dot(e_loc.astype(BF16), vwin) + _dot(e_ctx.astype(BF16), vc)) / den)
        o_ref[0, :, sl] = _merge_halves(outs[0], outs[1]).astype(BF16)


def _na_bias_mask(rpb, rows):
    nq = NA_QROWS * GRID_W
    nk = NA_KROWS * GRID_W
    col = np.arange(GRID_W)
    cstart = np.clip(col - NA_KW // 2, 0, GRID_W - NA_KW)
    col_ok = (col[None, :] >= cstart[:, None]) & (col[None, :] < cstart[:, None] + NA_KW)
    dc = np.clip(col[None, :] - col[:, None], -(NA_KW - 1), NA_KW - 1) + NA_KW - 1
    sel_c = np.where(col_ok[:, :, None], np.eye(2 * NA_KW - 1)[dc], 0.0)
    tabs = []
    for r0 in (0, NA_QROWS, rows - NA_QROWS):
        ksr = int(np.clip(r0 - NA_KH // 2, 0, rows - NA_KROWS))
        r = r0 + np.arange(NA_QROWS)
        kr = ksr + np.arange(NA_KROWS)
        rs = np.clip(r - NA_KH // 2, 0, rows - NA_KH)
        row_ok = (kr[None, :] >= rs[:, None]) & (kr[None, :] < rs[:, None] + NA_KH)
        dr = np.clip(kr[None, :] - r[:, None], -(NA_KH - 1), NA_KH - 1) + NA_KH - 1
        sel_r = np.where(row_ok[:, :, None], np.eye(2 * NA_KH - 1)[dr], 0.0)
        ok = (row_ok[:, None, :, None] & col_ok[None, :, None, :]).reshape(nq, nk)
        bias = jnp.einsum("hdc,rjd,qkc->hrqjk", rpb.astype(F32), jnp.asarray(sel_r, F32), jnp.asarray(sel_c, F32),
                          precision=lax.Precision.HIGHEST).reshape(N_HEADS, nq, nk)
        tabs.append(jnp.where(ok[None], bias, NEG_INF))
    return jnp.stack(tabs)


def _local_attn_kernel(sink_ref, qa_ref, ka_ref, va_ref, qb_ref, kb_ref, vb_ref, bm_ref, oa_ref, ob_ref, *,
                       s_lat, c_len):
    _swa_kernel(sink_ref, qa_ref, ka_ref, va_ref, oa_ref, s_lat=s_lat, c_len=c_len)
    _na_kernel(qb_ref, kb_ref, vb_ref, bm_ref, ob_ref, s_lat=s_lat, c_len=c_len)


def _local_attention(sink, qa, ka, va, qb, kb, vb, bm, s_lat):
    b, sall, _ = qa.shape
    nq = NA_QROWS * GRID_W
    assert nq == ATT_TQ
    nsteps = s_lat // nq
    kv = lambda width: pl.BlockSpec((1, sall, width), lambda i, j: (i, 0, 0))
    row = lambda width: pl.BlockSpec((1, nq, width), lambda i, j: (i, j, 0))

    def bm_map(i, j):
        return (jnp.where(j == 0, 0, jnp.where(j == nsteps - 1, 2, 1)), 0, 0, 0)

    return pl.pallas_call(
        functools.partial(_local_attn_kernel, s_lat=s_lat, c_len=sall - s_lat),
        grid=(b, nsteps),
        in_specs=[pl.BlockSpec(memory_space=pltpu.SMEM), row(512), kv(LANES), kv(LANES),
                  row(256), kv(256), kv(256),
                  pl.BlockSpec((1, N_HEADS, nq, NA_KROWS * GRID_W), bm_map)],
        out_specs=[row(256), row(256)],
        out_shape=[jax.ShapeDtypeStruct((b, s_lat, 256), BF16), jax.ShapeDtypeStruct((b, s_lat, 256), BF16)],
        compiler_params=_params(("arbitrary", "arbitrary")),
        name="local_attn",
    )(sink, qa, ka, va, qb, kb, vb, bm)


def _diff_kernel(lamv_ref, subg_ref, q_ref, k_ref, vt_ref, o_ref, e_scr, m_scr, *, n_keys, lam_init):
    tq = ATT_TQ
    chunks = [(c0, min(DIFF_KCHUNK, n_keys - c0)) for c0 in range(0, n_keys, DIFF_KCHUNK)]
    lam = _lam(lamv_ref, lam_init)
    stats = []
    for hh in range(DIFF_HEADS_PER_STEP):
        sl = slice(hh * LANES, (hh + 1) * LANES)
        q = q_ref[0, :, sl]
        q12 = jnp.concatenate([_half_mask(q, 0), _half_mask(q, 1)], axis=0)
        m = jnp.full((1, 2 * tq), NEG_INF, F32)
        den = jnp.zeros((1, 2 * tq), F32)
        for i, (c0, n) in enumerate(chunks):
            s = _dot_nt(k_ref[0, c0:c0 + n, sl], q12)
            m_new = jnp.maximum(m, jnp.max(s, axis=0, keepdims=True))
            e = jnp.exp2(s - m_new)
            e_scr[hh, c0:c0 + n, :] = e
            m_scr[hh, i:i + 1, :] = m_new
            den = den * jnp.exp2(m - m_new) + jnp.sum(e, axis=0, keepdims=True)
            m = m_new
        stats.append((m, den))
    for hh in range(DIFF_HEADS_PER_STEP):
        sl = slice(hh * LANES, (hh + 1) * LANES)
        m, den = stats[hh]
        acc = jnp.zeros((LANES, tq), F32)
        for i, (c0, n) in enumerate(chunks):
            f = jnp.exp2(m_scr[hh, i:i + 1, :] - m) / den
            p = e_scr[hh, c0:c0 + n, :tq] * f[:, :tq] - e_scr[hh, c0:c0 + n, tq:] * (lam * f[:, tq:])
            acc = acc + _dot(vt_ref[0, sl, c0:c0 + n], p.astype(BF16))
        o_ref[0, :, sl] = _diff_post(acc.T, subg_ref[...], lam_init).astype(BF16)


def _diff(lamv, subg, qd, kd, vdt, s_lat, lam_init):
    b, sall, _ = qd.shape
    nchunk = -(-sall // DIFF_KCHUNK)
    hps = DIFF_HEADS_PER_STEP
    wid = hps * LANES
    return pl.pallas_call(
        functools.partial(_diff_kernel, n_keys=sall, lam_init=lam_init),
        grid=(b, N_HEADS // hps, s_lat // ATT_TQ),
        in_specs=[pl.BlockSpec((8, LANES), lambda i, h, j: (0, 0)),
                  pl.BlockSpec((1, LANES), lambda i, h, j: (0, 0)),
                  pl.BlockSpec((1, ATT_TQ, wid), lambda i, h, j: (i, j, h)),
                  pl.BlockSpec((1, sall, wid), lambda i, h, j: (i, 0, h)),
                  pl.BlockSpec((1, wid, sall), lambda i, h, j: (i, h, 0))],
        out_specs=pl.BlockSpec((1, ATT_TQ, wid), lambda i, h, j: (i, j, h)),
        out_shape=jax.ShapeDtypeStruct((b, s_lat, 512), BF16),
        scratch_shapes=[pltpu.VMEM((hps, sall, 2 * ATT_TQ), F32),
                        pltpu.VMEM((hps, -(-nchunk // 8) * 8, 2 * ATT_TQ), F32)],
        compiler_params=_params(("arbitrary", "arbitrary", "arbitrary")),
        name="diff_attn",
    )(lamv, subg, qd, kd, vdt)


def _ctx_kernel(sink_ref, lamv_ref, subg_ref, qa_ref, ka_ref, va_ref, qb_ref, kb_ref, vb_ref,
                qd_ref, kd_ref, vd_ref, oa_ref, ob_ref, od_ref, *, lam_init):
    cl = qa_ref.shape[1]
    ka = ka_ref[0]
    va = va_ref[0]
    for g in range(2):
        q2 = _gqa_rows(qa_ref, g)
        s = _dot_nt(q2, ka)
        snk = _gqa_sink(sink_ref, g, cl)
        m = jnp.maximum(jnp.max(s, axis=-1, keepdims=True), snk)
        e = jnp.exp(s - m)
        den = jnp.sum(e, axis=-1, keepdims=True) + jnp.exp(snk - m)
        _gqa_store(oa_ref, g, _dot(e.astype(BF16), va) / den, cl)
    for c in range(2):
        sl = slice(c * LANES, (c + 1) * LANES)
        outs = []
        for half in range(2):
            s = _dot_nt(_half_mask(qb_ref[0, :, sl], half), kb_ref[0, :, sl])
            e = jnp.exp(s - jnp.max(s, axis=-1, keepdims=True))
            outs.append(_dot(e.astype(BF16), vb_ref[0, :, sl]) / jnp.sum(e, axis=-1, keepdims=True))
        ob_ref[0, :, sl] = _merge_halves(outs[0], outs[1]).astype(BF16)
    lam = _lam(lamv_ref, lam_init)
    for h in range(N_HEADS):
        sl = slice(h * LANES, (h + 1) * LANES)
        q = qd_ref[0, :, sl]
        q12 = jnp.concatenate([_half_mask(q, 0), _half_mask(q, 1)], axis=0)
        s = _dot_nt(q12, kd_ref[0, :, sl])
        e = jnp.exp2(s - jnp.max(s, axis=-1, keepdims=True))
        den = jnp.sum(e, axis=-1, keepdims=True)
        p = e[:cl] * (1.0 / den[:cl]) - e[cl:] * (lam / den[cl:])
        o = _dot(p.astype(BF16), vd_ref[0, :, sl])
        od_ref[0, :, sl] = _diff_post(o, subg_ref[...], lam_init).astype(BF16)


def _ctx_attention(sink, lamv, subg, qa, ka, va, qb, kb, vb, qd, kd, vd, s_lat, lam_init):
    b, sall, _ = qa.shape
    cl = sall - s_lat
    blk = s_lat // cl
    row = lambda width: pl.BlockSpec((1, cl, width), lambda i: (i, blk, 0))
    out = lambda width: pl.BlockSpec((1, cl, width), lambda i: (i, 0, 0))
    return pl.pallas_call(
        functools.partial(_ctx_kernel, lam_init=lam_init),
        grid=(b,),
        in_specs=[pl.BlockSpec(memory_space=pltpu.SMEM),
                  pl.BlockSpec((8, LANES), lambda i: (0, 0)),
                  pl.BlockSpec((1, LANES), lambda i: (0, 0)),
                  row(512), row(128), row(128), row(256), row(256), row(256), row(512), row(512), row(512)],
        out_specs=[out(256), out(256), out(512)],
        out_shape=[jax.ShapeDtypeStruct((b, cl, w), BF16) for w in (256, 256, 512)],
        compiler_params=_params(("arbitrary",)),
        name="ctx_attention",
    )(sink, lamv, subg, qa, ka, va, qb, kb, vb, qd, kd, vd)


def _first_argmax(v, iota, n):
    m = jnp.max(v, axis=0, keepdims=True)
    ix = jnp.min(jnp.where(v == m, iota, float(n)), axis=0, keepdims=True)
    return m, ix


def _router_kernel(xin_ref, wout_ref, g_ref, mod_ref, whi_ref, wlo_ref, bias_ref, tri_ref, ltri_ref, *refs,
                   tiles_per_batch, nlat):
    attn_refs, (x_ref, w_ref, slot_ref, meta_ref, cnt_ref) = refs[:-5], refs[-5:]

    @pl.when(pl.program_id(0) == 0)
    def _():
        cnt_ref[...] = jnp.zeros_like(cnt_ref)

    def residual(oa_ref, ob_ref, od_ref):
        attn = (_dot(oa_ref[0], wout_ref[0:256, :]) + _dot(ob_ref[0], wout_ref[256:512, :])
                + _dot(od_ref[0], wout_ref[512:1024, :]))
        x_ref[...] = xin_ref[0] + mod_ref[0, 2:3, :] * attn

    if len(attn_refs) == 3:
        residual(*attn_refs)
    else:
        is_ctx = (pl.program_id(0) % tiles_per_batch) >= nlat
        pl.when(jnp.logical_not(is_ctx))(lambda: residual(*attn_refs[:3]))
        pl.when(is_ctx)(lambda: residual(*attn_refs[3:6]))

    hf = _norm_mod(x_ref[...], g_ref[...], mod_ref[0, 3:4, :], mod_ref[0, 4:5, :])
    tm = hf.shape[0]
    h_hi, h_lo = _split_bf16(hf)
    whi = whi_ref[...]
    logits = _dot_nt(whi, h_hi) + _dot_nt(whi, h_lo) + _dot_nt(wlo_ref[...], h_hi)
    scores = _sigmoid(logits)
    biased = scores + bias_ref[...]
    gsz = N_EXPERTS // N_GROUPS
    iota_g = lax.broadcasted_iota(I32, (gsz, tm), 0).astype(F32)
    gscore = []
    for g in range(N_GROUPS):
        v = biased[g * gsz:(g + 1) * gsz]
        m1, i1 = _first_argmax(v, iota_g, gsz)
        m2 = jnp.max(jnp.where(iota_g == i1, -jnp.inf, v), axis=0, keepdims=True)
        gscore.append(m1 + m2)
    cur = jnp.concatenate(gscore, axis=0)
    iota_n = lax.broadcasted_iota(I32, (N_GROUPS, tm), 0).astype(F32)
    gsel = jnp.zeros((N_GROUPS, tm), F32)
    for _ in range(TOPK_GROUPS):
        _, ix = _first_argmax(cur, iota_n, N_GROUPS)
        hit = iota_n == ix
        gsel = jnp.where(hit, 1.0, gsel)
        cur = jnp.where(hit, -jnp.inf, cur)
    masked = jnp.concatenate(
        [jnp.where(gsel[g:g + 1] > 0.5, biased[g * gsz:(g + 1) * gsz], -jnp.inf) for g in range(N_GROUPS)],
        axis=0)
    iota_e = lax.broadcasted_iota(I32, (N_EXPERTS, tm), 0).astype(F32)
    onehot = jnp.zeros((N_EXPERTS, tm), F32)
    idxs, ws = [], []
    for _ in range(TOP_K):
        _, ix = _first_argmax(masked, iota_e, N_EXPERTS)
        hit = iota_e == ix
        ws.append(jnp.sum(jnp.where(hit, scores, 0.0), axis=0, keepdims=True))
        masked = jnp.where(hit, -jnp.inf, masked)
        onehot = jnp.where(hit, 1.0, onehot)
        idxs.append(ix)
    base = cnt_ref[...]
    rank = _dot(onehot.astype(BF16), tri_ref[...])
    n_col = jnp.sum(onehot, axis=1, keepdims=True)
    n_pad = jnp.floor((n_col + (ROW_CHUNK - 1)) * (1.0 / ROW_CHUNK)) * ROW_CHUNK
    start = _dot(ltri_ref[...], jnp.broadcast_to(n_pad, (N_EXPERTS, LANES)).astype(BF16))
    wsum = ws[0]
    for wk in ws[1:]:
        wsum = wsum + wk
    for k in range(TOP_K):
        slot = jnp.sum(jnp.where(iota_e == idxs[k], rank + start[:, 0:1], 0.0), axis=0, keepdims=True)
        slot_ref[0, k:k + 1, :] = slot.astype(I32)
        w_ref[0, k:k + 1, :] = ws[k] / wsum * ROUTED_SCALE
    lane = lax.broadcasted_iota(I32, (N_EXPERTS, LANES), 1)
    meta_ref[0] = jnp.where(lane == 0, n_pad, jnp.where(lane == 1, base, start))
    cnt_ref[...] = base + n_pad


def _router(xc, lat, ctx, w_out, g, modall, whi, wlo, bias, tri, ltri, s_lat):
    b, sall, d = xc.shape
    tm = TOK_TILE
    nlat = s_lat // tm
    tpb = nlat if ctx is None else sall // tm
    nt = b * tpb
    per_tok = pl.BlockSpec((1, TOP_K, tm), lambda i: (i, 0, 0))
    lat_row = lambda width: pl.BlockSpec((1, tm, width), lambda i: (i // tpb, jnp.minimum(i % tpb, nlat - 1), 0))
    ctx_row = lambda width: pl.BlockSpec((1, tm, width), lambda i: (i // tpb, 0, 0))
    widths = (256, 256, 512)
    attn_specs = [lat_row(w) for w in widths] + ([] if ctx is None else [ctx_row(w) for w in widths])

    def mod_map(i):
        return (2 * (i // tpb) + ((i % tpb) >= nlat).astype(I32), 0, 0)

    return pl.pallas_call(
        functools.partial(_router_kernel, tiles_per_batch=tpb, nlat=nlat),
        grid=(nt,),
        in_specs=[pl.BlockSpec((1, tm, d), lambda i: (i // tpb, i % tpb, 0)),
                  pl.BlockSpec((MIX_WIDTH, d), lambda i: (0, 0)),
                  pl.BlockSpec((1, d), lambda i: (0, 0)),
                  pl.BlockSpec((1, 8, d), mod_map),
                  pl.BlockSpec((N_EXPERTS, d), lambda i: (0, 0)),
                  pl.BlockSpec((N_EXPERTS, d), lambda i: (0, 0)),
                  pl.BlockSpec((N_EXPERTS, 1), lambda i: (0, 0)),
                  pl.BlockSpec((tm, tm), lambda i: (0, 0)),
                  pl.BlockSpec((N_EXPERTS, N_EXPERTS), lambda i: (0, 0))] + attn_specs,
        out_specs=[pl.BlockSpec((tm, d), lambda i: (i, 0)), per_tok, per_tok,
                   pl.BlockSpec((1, N_EXPERTS, LANES), lambda i: (i, 0, 0)),
                   pl.BlockSpec((N_EXPERTS, 1), lambda i: (0, 0))],
        out_shape=[jax.ShapeDtypeStruct((nt * tm, d), F32),
                   jax.ShapeDtypeStruct((nt, TOP_K, tm), F32),
                   jax.ShapeDtypeStruct((nt, TOP_K, tm), I32),
                   jax.ShapeDtypeStruct((nt, N_EXPERTS, LANES), F32),
                   jax.ShapeDtypeStruct((N_EXPERTS, 1), F32)],
        compiler_params=_params(("arbitrary",)),
        name="router",
    )(xc, w_out, g, modall, whi, wlo, bias, tri, ltri, *lat, *(() if ctx is None else ctx))


def _pack_bf16_pairs(v, is_bf16_valued=False):
    half = v.shape[1] // 2
    if not is_bf16_valued:
        v = v.astype(BF16).astype(F32)
    lo = lax.shift_right_logical(lax.bitcast_convert_type(v[:, :half], U32), jnp.uint32(16))
    hi = lax.bitcast_convert_type(v[:, half:], U32) & jnp.uint32(0xFFFF0000)
    return lo | hi


def _unpack_bf16_pairs(u):
    lo = lax.bitcast_convert_type(lax.shift_left(u, jnp.uint32(16)), F32).astype(BF16)
    hi = lax.bitcast_convert_type(u & jnp.uint32(0xFFFF0000), F32).astype(BF16)
    return lo, hi


def _used_chunks(meta_ref):
    return lax.shift_right_logical(_slots_used(meta_ref), ROW_CHUNK.bit_length() - 1)


def _start_chunk_copies(first, count, make_copy):
    pairs = lax.shift_right_logical(count, 1)

    def start_pair(j, carry):
        make_copy(first + 2 * j).start()
        make_copy(first + 2 * j + 1).start()
        return carry

    lax.fori_loop(0, pairs, start_pair, 0)

    @pl.when(count > 2 * pairs)
    def _():
        make_copy(first + count - 1).start()


def _wait_chunk_copies(total, copy_of_rows):
    group = lax.shift_right_logical(total, WAIT_GROUP.bit_length() - 1)

    def wait_group(j, carry):
        copy_of_rows(WAIT_GROUP * ROW_CHUNK).wait()
        return carry

    def wait_one(j, carry):
        copy_of_rows(ROW_CHUNK).wait()
        return carry

    lax.fori_loop(0, group, wait_group, 0)
    lax.fori_loop(0, total - group * WAIT_GROUP, wait_one, 0)


def _tile_slots(tm):
    return TOP_K * tm + ROW_CHUNK * N_EXPERTS


def _slots_used(meta_ref):
    last = N_EXPERTS - 1
    return meta_ref[0, 2, last] + meta_ref[0, 0, last]


def _dispatch_kernel(meta_ref, rows_ref, x_ref, g_ref, mod_ref, slot_ref, xs_ref, xc_scr, sem):
    hb = _norm_mod(x_ref[...], g_ref[...], mod_ref[0, 3:4, :], mod_ref[0, 4:5, :]).astype(BF16)
    tm = hb.shape[0]
    slot16 = slot_ref[0].astype(jnp.int16)

    total = _used_chunks(meta_ref)
    step = CHUNK_UNROLL * tm
    chunks_per_step = step // ROW_CHUNK

    def make_copy(j):
        src = pl.multiple_of(j * ROW_CHUNK, ROW_CHUNK)
        dst = pl.multiple_of(rows_ref[0, 0, j], ROW_CHUNK)
        return pltpu.make_async_copy(xc_scr.at[pl.ds(src, ROW_CHUNK)], xs_ref.at[pl.ds(dst, ROW_CHUNK)], sem)

    def select_rows(r2, carry):
        for u in range(CHUNK_UNROLL):
            r0 = pl.multiple_of((r2 * CHUNK_UNROLL + u) * tm, tm)
            row = (lax.broadcasted_iota(I32, (tm, tm), 0) + r0).astype(jnp.int16)
            sel = jnp.zeros((tm, tm), BF16)
            for k in range(TOP_K):
                sel = jnp.where(row == slot16[k:k + 1, :], jnp.ones((), BF16), sel)
            xc_scr[pl.ds(r0, tm), :] = _pack_bf16_pairs(_dot(sel, hb), is_bf16_valued=True)
        first = r2 * chunks_per_step
        _start_chunk_copies(first, jnp.clip(total - first, 0, chunks_per_step), make_copy)
        return carry

    lax.fori_loop(0, (_slots_used(meta_ref) + step - 1) // step, select_rows, 0)
    _wait_chunk_copies(total, lambda n: pltpu.make_async_copy(xc_scr.at[pl.ds(0, n)], xs_ref.at[pl.ds(0, n)], sem))


def _dispatch(meta, chunk_rows, slot, xflat, g, modall, n_rows, tiles_per_batch, nlat):
    t, d = xflat.shape
    tm = TOK_TILE
    nt = t // tm

    def mod_map(i):
        return (2 * (i // tiles_per_batch) + ((i % tiles_per_batch) >= nlat).astype(I32), 0, 0)

    return pl.pallas_call(
        _dispatch_kernel,
        grid=(nt,),
        in_specs=[pl.BlockSpec((1, 3, N_EXPERTS), lambda i: (i, 0, 0), memory_space=pltpu.SMEM),
                  pl.BlockSpec((1, 1, chunk_rows.shape[2]), lambda i: (i, 0, 0), memory_space=pltpu.SMEM),
                  pl.BlockSpec((tm, d), lambda i: (i, 0)),
                  pl.BlockSpec((1, d), lambda i: (0, 0)),
                  pl.BlockSpec((1, 8, d), mod_map),
                  pl.BlockSpec((1, TOP_K, tm), lambda i: (i, 0, 0))],
        out_specs=pl.BlockSpec(memory_space=pl.ANY),
        out_shape=jax.ShapeDtypeStruct((n_rows, d // 2), U32),
        scratch_shapes=[pltpu.VMEM((_tile_slots(tm), d // 2), U32), pltpu.SemaphoreType.DMA(())],
        compiler_params=_params(("arbitrary",)),
        name="moe_dispatch",
    )(meta, chunk_rows, xflat, g, modall, slot)


def _expert_kernel(blk_e_ref, nvalid_ref, nact_ref, next_e_ref, has_next_ref, parity_ref,
                   xs_ref, wg_any, wu_any, wd_any, ys_ref, wg_s, wu_s, wd_s, wg_buf, wu_buf, wd_buf, sem, *, layer):
    b = pl.program_id(0)
    e = blk_e_ref[b]
    changed = jnp.logical_or(b == 0, e != blk_e_ref[jnp.maximum(b - 1, 0)])
    par = parity_ref[b]

    def weight_copies(expert, p):
        return [pltpu.make_async_copy(src.at[layer, expert], dst.at[p], sem.at[p])
                for src, dst in ((wg_any, wg_buf), (wu_any, wu_buf), (wd_any, wd_buf))]

    @pl.when(b == 0)
    def _():
        for c in weight_copies(e, par):
            c.start()

    @pl.when(changed)
    def _():
        for c in weight_copies(e, par):
            c.wait()
        wg_s[...] = wg_buf[par].astype(BF16)
        wu_s[...] = wu_buf[par].astype(BF16)
        wd_s[...] = wd_buf[par].astype(BF16)

        @pl.when(has_next_ref[b] == 1)
        def _():
            for c in weight_copies(next_e_ref[b], 1 - par):
                c.start()

    half = xs_ref.shape[1]
    n_here = jnp.where(b < nact_ref[0], nvalid_ref[b], 0)

    @pl.when(n_here > 0)
    def _():
        row = lax.broadcasted_iota(I32, xs_ref.shape, 0)
        xs = jnp.where(row < n_here, xs_ref[...], jnp.zeros(xs_ref.shape, U32))
        x_lo, x_hi = _unpack_bf16_pairs(xs)
        gate = _dot(x_lo, wg_s[0:half, :]) + _dot(x_hi, wg_s[half:, :])
        up = _dot(x_lo, wu_s[0:half, :]) + _dot(x_hi, wu_s[half:, :])
        mid = (gate * _sigmoid(gate)) * up
        ys_ref[...] = _pack_bf16_pairs(_dot(mid.astype(BF16), wd_s[...]))

    @pl.when(n_here == 0)
    def _():
        ys_ref[...] = jnp.zeros_like(ys_ref)


def _experts(blk_e, nvalid, nact, xs, w_gate, w_up, w_down, layer):
    n_rows, half = xs.shape
    _, _, d, f = w_gate.shape
    n_blk = n_rows // MOE_BLOCK
    first_of_next = jnp.sum((blk_e[None, :] <= blk_e[:, None]).astype(I32), axis=1)
    has_next = (first_of_next < n_blk).astype(I32)
    next_e = blk_e[jnp.minimum(first_of_next, n_blk - 1)]
    starts = jnp.concatenate([jnp.zeros((1,), I32), (blk_e[1:] != blk_e[:-1]).astype(I32)])
    parity = jnp.cumsum(starts) % 2
    anyspec = pl.BlockSpec(memory_space=pl.ANY)
    grid_spec = pltpu.PrefetchScalarGridSpec(
        num_scalar_prefetch=6,
        grid=(n_blk,),
        in_specs=[pl.BlockSpec((MOE_BLOCK, half), lambda i, be, nv, na, *_: (jnp.minimum(i, na[0] - 1), 0)),
                  anyspec, anyspec, anyspec],
        out_specs=pl.BlockSpec((MOE_BLOCK, half), lambda i, be, nv, na, *_: (jnp.minimum(i, na[0]), 0)),
        scratch_shapes=[pltpu.VMEM((d, f), BF16), pltpu.VMEM((d, f), BF16), pltpu.VMEM((f, d), BF16),
                        pltpu.VMEM((2, d, f), F32), pltpu.VMEM((2, d, f), F32), pltpu.VMEM((2, f, d), F32),
                        pltpu.SemaphoreType.DMA((2,))],
    )
    return pl.pallas_call(
        functools.partial(_expert_kernel, layer=layer),
        grid_spec=grid_spec,
        out_shape=jax.ShapeDtypeStruct((n_rows, half), U32),
        compiler_params=_params(("arbitrary",)),
        name="moe_experts",
    )(blk_e, nvalid, nact, next_e, has_next, parity.astype(I32), xs, w_gate, w_up, w_down)


def _combine_kernel(meta_ref, rows_ref, x_ref, g_ref, mod_ref, spos_ref, wt_ref, wsg_ref, wsu_ref, wsd_ref,
                    gfin_ref, ys_ref, o_ref, y_scr, acc_scr, sb_scr, wb_scr, sem, *, final_norm):
    tm, d = x_ref.shape
    half = d // 2

    @pl.when(pl.program_id(0) == 0)
    def _():
        y_scr[...] = jnp.zeros_like(y_scr)

    step = CHUNK_UNROLL * COMBINE_KCHUNK
    chunks_per_step = step // ROW_CHUNK
    group_shift = chunks_per_step.bit_length() - 1

    def make_copy(j):
        src = pl.multiple_of(rows_ref[0, 0, j], ROW_CHUNK)
        dst = pl.multiple_of(j * ROW_CHUNK, ROW_CHUNK)
        return pltpu.make_async_copy(ys_ref.at[pl.ds(src, ROW_CHUNK)], y_scr.at[pl.ds(dst, ROW_CHUNK)],
                                     sem.at[lax.shift_right_logical(j, group_shift)])

    total = _used_chunks(meta_ref)
    _start_chunk_copies(0, total, make_copy)

    x = x_ref[...]
    hb = _norm_mod(x, g_ref[...], mod_ref[0, 3:4, :], mod_ref[0, 4:5, :]).astype(BF16)
    gate = _dot(hb, wsg_ref[...])
    mid = (gate * _sigmoid(gate)) * _dot(hb, wsu_ref[...])
    acc_scr[...] = _dot(mid.astype(BF16), wsd_ref[...])

    n_used = _slots_used(meta_ref)
    for k in range(TOP_K):
        sb_scr[k] = jnp.broadcast_to(spos_ref[0, :, k:k + 1], (tm, LANES)).astype(jnp.int16)
        wb_scr[k] = jnp.broadcast_to(wt_ref[0, :, k:k + 1], (tm, LANES)).astype(BF16)

    def chunk(j2, carry):
        _wait_chunk_copies(jnp.clip(total - j2 * chunks_per_step, 0, chunks_per_step),
                           lambda n: pltpu.make_async_copy(ys_ref.at[pl.ds(0, n)], y_scr.at[pl.ds(0, n)], sem.at[j2]))
        lo_sum = jnp.zeros((tm, half), F32)
        hi_sum = jnp.zeros((tm, half), F32)
        for u in range(CHUNK_UNROLL):
            r0 = pl.multiple_of((j2 * CHUNK_UNROLL + u) * COMBINE_KCHUNK, COMBINE_KCHUNK)
            parts = []
            for c in range(COMBINE_KCHUNK // LANES):
                col = (lax.broadcasted_iota(I32, (tm, LANES), 1) + (r0 + c * LANES)).astype(jnp.int16)
                p = jnp.zeros((tm, LANES), BF16)
                for k in range(TOP_K):
                    p = jnp.where(col == sb_scr[k], wb_scr[k], p)
                parts.append(p)
            pb = jnp.concatenate(parts, axis=1)
            y_lo, y_hi = _unpack_bf16_pairs(y_scr[pl.ds(r0, COMBINE_KCHUNK), :])
            lo_sum = lo_sum + _dot(pb, y_lo)
            hi_sum = hi_sum + _dot(pb, y_hi)
        acc_scr[:, :half] += lo_sum
        acc_scr[:, half:] += hi_sum
        return carry

    lax.fori_loop(0, (n_used + step - 1) // step, chunk, 0)
    out = x + mod_ref[0, 5:6, :] * acc_scr[...]
    if final_norm:
        out = _rms(out, gfin_ref[...])
    o_ref[...] = out


def _combine(meta, chunk_rows, spos_t, wt, xflat, g, modall, wsg, wsu, wsd, gfin, ys, tiles_per_batch, nlat,
             final_norm):
    t, d = xflat.shape
    f = wsg.shape[1]
    tm = TOK_TILE
    nt = t // tm
    y_rows = _tile_slots(tm)
    assert y_rows % COMBINE_KCHUNK == 0

    def mod_map(i):
        return (2 * (i // tiles_per_batch) + ((i % tiles_per_batch) >= nlat).astype(I32), 0, 0)

    return pl.pallas_call(
        functools.partial(_combine_kernel, final_norm=final_norm),
        grid=(nt,),
        in_specs=[pl.BlockSpec((1, 3, N_EXPERTS), lambda i: (i, 0, 0), memory_space=pltpu.SMEM),
                  pl.BlockSpec((1, 1, chunk_rows.shape[2]), lambda i: (i, 0, 0), memory_space=pltpu.SMEM),
                  pl.BlockSpec((tm, d), lambda i: (i, 0)),
                  pl.BlockSpec((1, d), lambda i: (0, 0)),
                  pl.BlockSpec((1, 8, d), mod_map),
                  pl.BlockSpec((1, tm, TOP_K), lambda i: (i, 0, 0)),
                  pl.BlockSpec((1, tm, TOP_K), lambda i: (i, 0, 0)),
                  pl.BlockSpec((d, f), lambda i: (0, 0)),
                  pl.BlockSpec((d, f), lambda i: (0, 0)),
                  pl.BlockSpec((f, d), lambda i: (0, 0)),
                  pl.BlockSpec((1, d), lambda i: (0, 0)),
                  pl.BlockSpec(memory_space=pl.ANY)],
        out_specs=pl.BlockSpec((tm, d), lambda i: (i, 0)),
        out_shape=jax.ShapeDtypeStruct((t, d), F32),
        scratch_shapes=[pltpu.VMEM((y_rows, d // 2), U32), pltpu.VMEM((tm, d), F32),
                        pltpu.VMEM((TOP_K, tm, LANES), jnp.int16), pltpu.VMEM((TOP_K, tm, LANES), BF16),
                        pltpu.SemaphoreType.DMA((y_rows // (CHUNK_UNROLL * COMBINE_KCHUNK),))],
        compiler_params=_params(("arbitrary",)),
        name="moe_combine",
    )(meta, chunk_rows, xflat, g, modall, spos_t, wt, wsg, wsu, wsd, gfin, ys)


def _rope_tables(s_lat, c_len):
    n_freq = HEAD_DIM // 4
    inv = 1.0 / (ROPE_BASE ** (jnp.arange(n_freq, dtype=F32) / n_freq))
    t = jnp.arange(s_lat)
    pos = jnp.stack([(t // GRID_W).astype(F32), (t % GRID_W).astype(F32)], axis=1)
    lane = np.arange(LANES)
    ang = pos[:, (lane % HEAD_DIM) // 32] * inv[lane % 16][None, :]
    sign = jnp.asarray(np.where(lane % 32 < 16, -1.0, 1.0), F32)
    cos = jnp.concatenate([jnp.cos(ang), jnp.ones((c_len, LANES), F32)], axis=0)
    sin = jnp.concatenate([jnp.sin(ang) * sign, jnp.zeros((c_len, LANES), F32)], axis=0)
    return cos, sin


def _moe(xc, lat, ctx, lp, modall, gfin, s_lat, final_norm):
    d = xc.shape[2]
    nlat = s_lat // TOK_TILE
    tiles_per_batch = nlat if ctx is None else xc.shape[1] // TOK_TILE
    g = lp["g_ffn"].reshape(1, d)
    w_r = lp["w_router"].T
    whi = w_r.astype(BF16)
    wlo = (w_r - whi.astype(F32)).astype(BF16)
    tri = jnp.asarray(np.triu(np.ones((TOK_TILE, TOK_TILE), np.float32), 1), BF16)
    ltri = jnp.asarray(np.tril(np.ones((N_EXPERTS, N_EXPERTS), np.float32), -1), BF16)
    xflat, w, slot, meta, cnt = _router(xc, lat, ctx, lp["w_out"].astype(BF16), g, modall, whi, wlo,
                                        lp["router_bias"].reshape(N_EXPERTS, 1), tri, ltri, s_lat)
    t = xflat.shape[0]
    nt = t // TOK_TILE
    meta = jnp.transpose(meta[:, :, :3].astype(I32), (0, 2, 1))
    rows_e = cnt[:, 0].astype(I32)
    seg = (rows_e + MOE_BLOCK - 1) // MOE_BLOCK * MOE_BLOCK
    pend = jnp.cumsum(seg)
    pstart = (pend - seg).astype(I32)
    n_blk = (t * TOP_K + ROW_CHUNK * N_EXPERTS * nt) // MOE_BLOCK + N_EXPERTS
    blk_row0 = jnp.arange(n_blk, dtype=I32) * MOE_BLOCK
    blk_e = jnp.minimum(jnp.sum((pend[None, :] <= blk_row0[:, None]).astype(I32), axis=1), N_EXPERTS - 1)
    of_blk = blk_e[:, None] == jnp.arange(N_EXPERTS, dtype=I32)[None, :]
    seg_end = jnp.sum(jnp.where(of_blk, (pstart + rows_e)[None, :], 0), axis=1)
    nvalid = jnp.clip(seg_end - blk_row0, 0, MOE_BLOCK).astype(I32)
    nact = (pend[-1] // MOE_BLOCK).astype(I32).reshape(1)
    n_pad, base, start = meta[:, 0, :], meta[:, 1, :], meta[:, 2, :]
    chunk_slot = jnp.arange(_tile_slots(TOK_TILE) // ROW_CHUNK, dtype=I32) * ROW_CHUNK
    s = chunk_slot[None, :, None]
    in_run = (start[:, None, :] <= s) & (s < (start + n_pad)[:, None, :])
    row0 = pstart[None, :] + base - start
    chunk_rows = (jnp.sum(jnp.where(in_run, row0[:, None, :], 0), axis=2) + chunk_slot[None, :])[:, None, :]
    xs = _dispatch(meta, chunk_rows, slot, xflat, g, modall, n_blk * MOE_BLOCK, tiles_per_batch, nlat)
    ys = _experts(blk_e, nvalid, nact, xs, lp["w_gate"], lp["w_up"], lp["w_down"], lp["layer"])
    return _combine(meta, chunk_rows, jnp.transpose(slot, (0, 2, 1)), jnp.transpose(w, (0, 2, 1)), xflat, g, modall,
                    lp["ws_gate"].astype(BF16), lp["ws_up"].astype(BF16), lp["ws_down"].astype(BF16), gfin, ys,
                    tiles_per_batch, nlat, final_norm)


def _layer(xc, cs, cos, sin, lp, layer_idx, last, s_lat, gfin):
    b, sall, d = xc.shape
    c_len = sall - s_lat
    lam_init = 0.8 - 0.6 * math.exp(-0.3 * layer_idx)
    mod = _modulation(cs, lp["w_mod"], lp["b_mod"])
    mod_lat = mod[:b].reshape(b, 1, 6, d)
    mod_ctx = jnp.broadcast_to(mod[b].reshape(1, 1, 6, d), (b, 1, 6, d))
    modall = jnp.concatenate([mod_lat, mod_ctx], axis=1)
    modall = jnp.pad(modall, ((0, 0), (0, 0), (0, 2), (0, 0))).reshape(2 * b, 8, d)

    qa, qb, qd, ka, va, kb, vb, kd, vd, vdt = _projection(
        xc, lp["g_mix"].reshape(1, d), modall, cos, sin, lp["w_in"].astype(BF16), s_lat)
    sink = lp["attn_sink"].astype(F32)
    lamv = jnp.zeros((8, LANES), F32).at[0:4, 0:HEAD_DIM].set(
        jnp.stack([lp["lam_q1"], lp["lam_k1"], lp["lam_q2"], lp["lam_k2"]]).astype(F32))
    subg = lp["subln_g"].reshape(1, LANES).astype(F32)
    oa, ob = _local_attention(sink, qa, ka, va, qb, kb, vb, _na_bias_mask(lp["na_rpb"], s_lat // GRID_W), s_lat)
    od = _diff(lamv, subg, qd, kd, vdt, s_lat, lam_init)
    ctx_out = None
    if not last:
        ctx_out = _ctx_attention(sink, lamv, subg, qa, ka, va, qb, kb, vb, qd, kd, vd, s_lat, lam_init)
    n_rows = s_lat if last else sall
    y = _moe(xc, (oa, ob, od), ctx_out, lp, modall, gfin, s_lat, last)
    return y.reshape(b, n_rows, d)


_LAYER_KEYS = ("w_mod", "b_mod", "g_mix", "g_ffn", "w_in", "w_out", "attn_sink", "na_rpb", "lam_q1", "lam_k1",
               "lam_q2", "lam_k2", "subln_g", "w_router", "router_bias", "w_gate", "w_up", "w_down",
               "ws_gate", "ws_up", "ws_down")


def kernel(x, c, ctx, c_ctx, w_mod, b_mod, g_mix, g_ffn, w_in, w_out, attn_sink, na_rpb, lam_q1, lam_k1, lam_q2,
           lam_k2, subln_g, w_router, router_bias, w_gate, w_up, w_down, ws_gate, ws_up, ws_down, g_final):
    stacked = dict(zip(_LAYER_KEYS, (w_mod, b_mod, g_mix, g_ffn, w_in, w_out, attn_sink, na_rpb, lam_q1, lam_k1,
                                     lam_q2, lam_k2, subln_g, w_router, router_bias, w_gate, w_up, w_down,
                                     ws_gate, ws_up, ws_down)))
    b, s_lat, d = x.shape
    c_len = ctx.shape[1]
    depth = w_mod.shape[0]
    assert s_lat % (NA_QROWS * GRID_W) == 0 and s_lat % c_len == 0 and c_len == TOK_TILE
    cs = jnp.zeros((16, d), F32).at[:b].set(c).at[b].set(c_ctx)
    cos, sin = _rope_tables(s_lat, c_len)
    xc = jnp.concatenate([x, ctx], axis=1)
    gfin = g_final.reshape(1, d)
    for i in range(depth):
        big = ("w_gate", "w_up", "w_down")
        lp = {k: (v if k in big else v[i]) for k, v in stacked.items()}
        lp["layer"] = i
        xc = _layer(xc, cs, cos, sin, lp, i, i == depth - 1, s_lat, gfin)
    return xc
```

```python
import functools
import math

import numpy as np
import jax
import jax.numpy as jnp
from jax import lax
from jax.experimental import pallas as pl
from jax.experimental.pallas import tpu as pltpu

F32 = jnp.float32
BF16 = jnp.bfloat16
I32 = jnp.int32
U32 = jnp.uint32

GRID_W = 64
HEAD_DIM = 64
ROPE_BASE = 10000.0
SWA_WINDOW = 128
NA_KH = 8
NA_KW = 16
N_HEADS = 4
N_EXPERTS = 256
TOP_K = 8
N_GROUPS = 8
TOPK_GROUPS = 4
ROUTED_SCALE = 2.5
MOE_BLOCK = 1024
ROW_CHUNK = 8
WAIT_GROUP = 16
COMBINE_KCHUNK = 512
CHUNK_UNROLL = 2
EPS = 1e-6
NEG_INF = -1e30
LOG2E = math.log2(math.e)
Q_COLS = 1024
IN_COLS = 2816
MIX_WIDTH = 1024

LANES = 128
TOK_TILE = 256
ATT_TQ = 256
NA_QROWS = 4
NA_KROWS = NA_QROWS + NA_KH - 1
DIFF_KCHUNK = 256
DIFF_HEADS_PER_STEP = 2
VMEM_LIMIT = 48 * 1024 * 1024


def _params(sem):
    return pltpu.CompilerParams(dimension_semantics=sem, vmem_limit_bytes=VMEM_LIMIT)


def _dot(a, b):
    return jnp.dot(a, b, preferred_element_type=F32)


def _dot_nt(a, b):
    return lax.dot_general(a, b, (((1,), (1,)), ((), ())), preferred_element_type=F32)


def _split_bf16(a):
    hi = a.astype(BF16)
    lo = (a - hi.astype(F32)).astype(BF16)
    return hi, lo


def _sigmoid(x):
    return 1.0 / (1.0 + jnp.exp(-x))


def _rms(x, g):
    return x * lax.rsqrt(jnp.mean(x * x, axis=-1, keepdims=True) + EPS) * g


def _norm_mod(x, g, shift, scale):
    return _rms(x, g) * (1.0 + scale) + shift


def _mod_kernel(c_ref, w_ref, b_ref, o_ref):
    c = c_ref[...]
    a_hi, a_lo = _split_bf16(c * _sigmoid(c))
    w_hi, w_lo = _split_bf16(w_ref[...])
    o_ref[...] = _dot(a_hi, w_hi) + _dot(a_hi, w_lo) + _dot(a_lo, w_hi) + b_ref[...]


def _modulation(cs, w_mod, b_mod):
    n, d = cs.shape
    cols = w_mod.shape[1]
    tn = 1536
    return pl.pallas_call(
        _mod_kernel,
        grid=(cols // tn,),
        in_specs=[pl.BlockSpec((n, d), lambda j: (0, 0)),
                  pl.BlockSpec((d, tn), lambda j: (0, j)),
                  pl.BlockSpec((1, tn), lambda j: (0, j))],
        out_specs=pl.BlockSpec((n, tn), lambda j: (0, j)),
        out_shape=jax.ShapeDtypeStruct((n, cols), F32),
        compiler_params=_params(("arbitrary",)),
        name="modulation",
    )(cs, w_mod, b_mod.reshape(1, cols))


def _proj_kernel(x_ref, g_ref, mod_ref, cos_ref, sin_ref, w_ref,
                 qa_ref, qb_ref, qd_ref, ka_ref, va_ref, kb_ref, vb_ref, kd_ref, vd_ref, vdt_ref, h_scr):
    h = _norm_mod(x_ref[0], g_ref[...], mod_ref[0, 0:1, :], mod_ref[0, 1:2, :])
    h_scr[...] = h.astype(BF16)
    cos = cos_ref[...]
    sin = sin_ref[...]
    tm = h.shape[0]
    lane = lax.broadcasted_iota(I32, (tm, LANES), 1)
    first16 = (lane & 16) == 0
    lo64 = lane < HEAD_DIM

    def rope(v):
        partner = jnp.where(first16, pltpu.roll(v, LANES - 16, 1), pltpu.roll(v, 16, 1))
        return v * cos + partner * sin

    def mm(c0):
        return _dot(h_scr[...], w_ref[:, c0:c0 + 2 * LANES])

    a = mm(0) * (HEAD_DIM ** -0.5)
    c0 = rope(a[:, :LANES])
    c1 = rope(a[:, LANES:])
    zero = jnp.zeros_like(c0)
    qa_ref[0, :, 0 * LANES:1 * LANES] = jnp.where(lo64, c0, zero).astype(BF16)
    qa_ref[0, :, 1 * LANES:2 * LANES] = jnp.where(lo64, pltpu.roll(c0, HEAD_DIM, 1), zero).astype(BF16)
    qa_ref[0, :, 2 * LANES:3 * LANES] = jnp.where(lo64, zero, pltpu.roll(c1, HEAD_DIM, 1)).astype(BF16)
    qa_ref[0, :, 3 * LANES:4 * LANES] = jnp.where(lo64, zero, c1).astype(BF16)
    qb_ref[0] = (mm(256) * (HEAD_DIM ** -0.5)).astype(BF16)
    for j in range(2):
        a = mm(512 + 256 * j) * (HEAD_DIM ** -0.5 * LOG2E)
        qd_ref[0, :, 256 * j:256 * j + LANES] = rope(a[:, :LANES]).astype(BF16)
        qd_ref[0, :, 256 * j + LANES:256 * (j + 1)] = rope(a[:, LANES:]).astype(BF16)
    a = mm(1024)
    ka_ref[0] = rope(a[:, :LANES]).astype(BF16)
    va_ref[0] = a[:, LANES:].astype(BF16)
    kb_ref[0] = mm(1280).astype(BF16)
    vb_ref[0] = mm(1536).astype(BF16)
    for j in range(2):
        a = mm(1792 + 256 * j)
        kd_ref[0, :, 256 * j:256 * j + LANES] = rope(a[:, :LANES]).astype(BF16)
        kd_ref[0, :, 256 * j + LANES:256 * (j + 1)] = rope(a[:, LANES:]).astype(BF16)
        a = mm(2304 + 256 * j)
        vd_ref[0, :, 256 * j:256 * (j + 1)] = a.astype(BF16)
        vdt_ref[0, 256 * j:256 * j + LANES, :] = a[:, :LANES].T.astype(BF16)
        vdt_ref[0, 256 * j + LANES:256 * (j + 1), :] = a[:, LANES:].T.astype(BF16)


def _projection(xc, g, modall, cos, sin, w_in, s_lat):
    b, sall, d = xc.shape
    tm = TOK_TILE
    nlat = s_lat // tm
    row = lambda width: pl.BlockSpec((1, tm, width), lambda i, j: (i, j, 0))
    widths = (512, 256, 512, 128, 128, 256, 256, 512, 512)
    return pl.pallas_call(
        _proj_kernel,
        grid=(b, sall // tm),
        in_specs=[row(d),
                  pl.BlockSpec((1, d), lambda i, j: (0, 0)),
                  pl.BlockSpec((1, 8, d), lambda i, j: (2 * i + (j >= nlat).astype(I32), 0, 0)),
                  pl.BlockSpec((tm, LANES), lambda i, j: (j, 0)),
                  pl.BlockSpec((tm, LANES), lambda i, j: (j, 0)),
                  pl.BlockSpec((d, IN_COLS), lambda i, j: (0, 0))],
        out_specs=[row(w) for w in widths] + [pl.BlockSpec((1, 512, tm), lambda i, j: (i, 0, j))],
        out_shape=([jax.ShapeDtypeStruct((b, sall, w), BF16) for w in widths]
                   + [jax.ShapeDtypeStruct((b, 512, sall), BF16)]),
        scratch_shapes=[pltpu.VMEM((tm, d), BF16)],
        compiler_params=_params(("arbitrary", "arbitrary")),
        name="projection",
    )(xc, g, modall, cos, sin, w_in)


def _half_mask(q, half):
    lane = lax.broadcasted_iota(I32, q.shape, 1)
    keep = (lane < HEAD_DIM) if half == 0 else (lane >= HEAD_DIM)
    return jnp.where(keep, q, jnp.zeros_like(q))


def _merge_halves(lo_part, hi_part):
    lane = lax.broadcasted_iota(I32, lo_part.shape, 1)
    return jnp.where(lane < HEAD_DIM, lo_part, hi_part)


def _gqa_rows(q_ref, g):
    return jnp.concatenate([q_ref[0, :, (2 * g) * LANES:(2 * g + 1) * LANES],
                            q_ref[0, :, (2 * g + 1) * LANES:(2 * g + 2) * LANES]], axis=0)


def _gqa_sink(sink_ref, g, tq):
    rowi = lax.broadcasted_iota(I32, (2 * tq, 1), 0)
    return jnp.where(rowi < tq, sink_ref[2 * g], sink_ref[2 * g + 1])


def _gqa_store(o_ref, g, o, tq):
    top, bot = o[:tq], o[tq:]
    if g == 0:
        chunk = _merge_halves(top, pltpu.roll(bot, HEAD_DIM, 1))
    else:
        chunk = _merge_halves(pltpu.roll(top, HEAD_DIM, 1), bot)
    o_ref[0, :, g * LANES:(g + 1) * LANES] = chunk.astype(BF16)


def _lam(lamv_ref, lam_init):
    v = lamv_ref[...]
    a = jnp.sum(v[0:1] * v[1:2], axis=-1, keepdims=True)
    b = jnp.sum(v[2:3] * v[3:4], axis=-1, keepdims=True)
    return jnp.exp(a) - jnp.exp(b) + lam_init


def _diff_post(o, subg, lam_init):
    return _rms(o, subg) * (1.0 - lam_init)


def _swa_kernel(sink_ref, q_ref, k_ref, v_ref, o_ref, *, s_lat, c_len):
    tq = ATT_TQ
    tk = tq + 2 * SWA_WINDOW
    q0 = pl.program_id(1) * tq
    ks = pl.multiple_of(jnp.clip(q0 - SWA_WINDOW, 0, s_lat - tk), SWA_WINDOW)
    kwin = k_ref[0, pl.ds(ks, tk), :]
    vwin = v_ref[0, pl.ds(ks, tk), :]
    kc = k_ref[0, s_lat:s_lat + c_len, :]
    vc = v_ref[0, s_lat:s_lat + c_len, :]
    qpos = q0 + lax.broadcasted_iota(I32, (tq, tk), 0)
    kpos = ks + lax.broadcasted_iota(I32, (tq, tk), 1)
    valid = jnp.abs(qpos - kpos) <= SWA_WINDOW
    valid = jnp.concatenate([valid, valid], axis=0)
    for g in range(2):
        q2 = _gqa_rows(q_ref, g)
        s_loc = jnp.where(valid, _dot_nt(q2, kwin), NEG_INF)
        s_ctx = _dot_nt(q2, kc)
        snk = _gqa_sink(sink_ref, g, tq)
        m = jnp.maximum(jnp.maximum(jnp.max(s_loc, axis=-1, keepdims=True),
                                    jnp.max(s_ctx, axis=-1, keepdims=True)), snk)
        e_loc = jnp.exp(s_loc - m)
        e_ctx = jnp.exp(s_ctx - m)
        den = (jnp.sum(e_loc, axis=-1, keepdims=True) + jnp.sum(e_ctx, axis=-1, keepdims=True)
               + jnp.exp(snk - m))
        o = (_dot(e_loc.astype(BF16), vwin) + _dot(e_ctx.astype(BF16), vc)) / den
        _gqa_store(o_ref, g, o, tq)


def _na_kernel(q_ref, k_ref, v_ref, bm_ref, o_ref, *, s_lat, c_len):
    rows = s_lat // GRID_W
    nk = NA_KROWS * GRID_W
    r0 = pl.program_id(1) * NA_QROWS
    ks = pl.multiple_of(jnp.clip(r0 - NA_KH // 2, 0, rows - NA_KROWS) * GRID_W, GRID_W)
    for c in range(2):
        sl = slice(c * LANES, (c + 1) * LANES)
        kwin = k_ref[0, pl.ds(ks, nk), sl]
        vwin = v_ref[0, pl.ds(ks, nk), sl]
        kc = k_ref[0, s_lat:s_lat + c_len, sl]
        vc = v_ref[0, s_lat:s_lat + c_len, sl]
        q = q_ref[0, :, sl]
        outs = []
        for half in range(2):
            qm = _half_mask(q, half)
            s_loc = _dot_nt(qm, kwin) + bm_ref[0, 2 * c + half]
            s_ctx = _dot_nt(qm, kc)
            m = jnp.maximum(jnp.max(s_loc, axis=-1, keepdims=True), jnp.max(s_ctx, axis=-1, keepdims=True))
            e_loc = jnp.exp(s_loc - m)
            e_ctx = jnp.exp(s_ctx - m)
            den = jnp.sum(e_loc, axis=-1, keepdims=True) + jnp.sum(e_ctx, axis=-1, keepdims=True)
            outs.append((_dot(e_loc.astype(BF16), vwin) + _dot(e_ctx.astype(BF16), vc)) / den)
        o_ref[0, :, sl] = _merge_halves(outs[0], outs[1]).astype(BF16)


def _na_bias_mask(rpb, rows):
    nq = NA_QROWS * GRID_W
    nk = NA_KROWS * GRID_W
    col = np.arange(GRID_W)
    cstart = np.clip(col - NA_KW // 2, 0, GRID_W - NA_KW)
    col_ok = (col[None, :] >= cstart[:, None]) & (col[None, :] < cstart[:, None] + NA_KW)
    dc = np.clip(col[None, :] - col[:, None], -(NA_KW - 1), NA_KW - 1) + NA_KW - 1
    sel_c = np.where(col_ok[:, :, None], np.eye(2 * NA_KW - 1)[dc], 0.0)
    tabs = []
    for r0 in (0, NA_QROWS, rows - NA_QROWS):
        ksr = int(np.clip(r0 - NA_KH // 2, 0, rows - NA_KROWS))
        r = r0 + np.arange(NA_QROWS)
        kr = ksr + np.arange(NA_KROWS)
        rs = np.clip(r - NA_KH // 2, 0, rows - NA_KH)
        row_ok = (kr[None, :] >= rs[:, None]) & (kr[None, :] < rs[:, None] + NA_KH)
        dr = np.clip(kr[None, :] - r[:, None], -(NA_KH - 1), NA_KH - 1) + NA_KH - 1
        sel_r = np.where(row_ok[:, :, None], np.eye(2 * NA_KH - 1)[dr], 0.0)
        ok = (row_ok[:, None, :, None] & col_ok[None, :, None, :]).reshape(nq, nk)
        bias = jnp.einsum("hdc,rjd,qkc->hrqjk", rpb.astype(F32), jnp.asarray(sel_r, F32), jnp.asarray(sel_c, F32),
                          precision=lax.Precision.HIGHEST).reshape(N_HEADS, nq, nk)
        tabs.append(jnp.where(ok[None], bias, NEG_INF))
    return jnp.stack(tabs)


def _local_attn_kernel(sink_ref, qa_ref, ka_ref, va_ref, qb_ref, kb_ref, vb_ref, bm_ref, oa_ref, ob_ref, *,
                       s_lat, c_len):
    _swa_kernel(sink_ref, qa_ref, ka_ref, va_ref, oa_ref, s_lat=s_lat, c_len=c_len)
    _na_kernel(qb_ref, kb_ref, vb_ref, bm_ref, ob_ref, s_lat=s_lat, c_len=c_len)


def _local_attention(sink, qa, ka, va, qb, kb, vb, bm, s_lat):
    b, sall, _ = qa.shape
    nq = NA_QROWS * GRID_W
    assert nq == ATT_TQ
    nsteps = s_lat // nq
    kv = lambda width: pl.BlockSpec((1, sall, width), lambda i, j: (i, 0, 0))
    row = lambda width: pl.BlockSpec((1, nq, width), lambda i, j: (i, j, 0))

    def bm_map(i, j):
        return (jnp.where(j == 0, 0, jnp.where(j == nsteps - 1, 2, 1)), 0, 0, 0)

    return pl.pallas_call(
        functools.partial(_local_attn_kernel, s_lat=s_lat, c_len=sall - s_lat),
        grid=(b, nsteps),
        in_specs=[pl.BlockSpec(memory_space=pltpu.SMEM), row(512), kv(LANES), kv(LANES),
                  row(256), kv(256), kv(256),
                  pl.BlockSpec((1, N_HEADS, nq, NA_KROWS * GRID_W), bm_map)],
        out_specs=[row(256), row(256)],
        out_shape=[jax.ShapeDtypeStruct((b, s_lat, 256), BF16), jax.ShapeDtypeStruct((b, s_lat, 256), BF16)],
        compiler_params=_params(("arbitrary", "arbitrary")),
        name="local_attn",
    )(sink, qa, ka, va, qb, kb, vb, bm)


def _diff_kernel(lamv_ref, subg_ref, q_ref, k_ref, vt_ref, o_ref, e_scr, m_scr, *, n_keys, lam_init):
    tq = ATT_TQ
    chunks = [(c0, min(DIFF_KCHUNK, n_keys - c0)) for c0 in range(0, n_keys, DIFF_KCHUNK)]
    lam = _lam(lamv_ref, lam_init)
    stats = []
    for hh in range(DIFF_HEADS_PER_STEP):
        sl = slice(hh * LANES, (hh + 1) * LANES)
        q = q_ref[0, :, sl]
        q12 = jnp.concatenate([_half_mask(q, 0), _half_mask(q, 1)], axis=0)
        m = jnp.full((1, 2 * tq), NEG_INF, F32)
        den = jnp.zeros((1, 2 * tq), F32)
        for i, (c0, n) in enumerate(chunks):
            s = _dot_nt(k_ref[0, c0:c0 + n, sl], q12)
            m_new = jnp.maximum(m, jnp.max(s, axis=0, keepdims=True))
            e = jnp.exp2(s - m_new)
            e_scr[hh, c0:c0 + n, :] = e
            m_scr[hh, i:i + 1, :] = m_new
            den = den * jnp.exp2(m - m_new) + jnp.sum(e, axis=0, keepdims=True)
            m = m_new
        stats.append((m, den))
    for hh in range(DIFF_HEADS_PER_STEP):
        sl = slice(hh * LANES, (hh + 1) * LANES)
        m, den = stats[hh]
        acc = jnp.zeros((LANES, tq), F32)
        for i, (c0, n) in enumerate(chunks):
            f = jnp.exp2(m_scr[hh, i:i + 1, :] - m) / den
            p = e_scr[hh, c0:c0 + n, :tq] * f[:, :tq] - e_scr[hh, c0:c0 + n, tq:] * (lam * f[:, tq:])
            acc = acc + _dot(vt_ref[0, sl, c0:c0 + n], p.astype(BF16))
        o_ref[0, :, sl] = _diff_post(acc.T, subg_ref[...], lam_init).astype(BF16)


def _diff(lamv, subg, qd, kd, vdt, s_lat, lam_init):
    b, sall, _ = qd.shape
    nchunk = -(-sall // DIFF_KCHUNK)
    hps = DIFF_HEADS_PER_STEP
    wid = hps * LANES
    return pl.pallas_call(
        functools.partial(_diff_kernel, n_keys=sall, lam_init=lam_init),
        grid=(b, N_HEADS // hps, s_lat // ATT_TQ),
        in_specs=[pl.BlockSpec((8, LANES), lambda i, h, j: (0, 0)),
                  pl.BlockSpec((1, LANES), lambda i, h, j: (0, 0)),
                  pl.BlockSpec((1, ATT_TQ, wid), lambda i, h, j: (i, j, h)),
                  pl.BlockSpec((1, sall, wid), lambda i, h, j: (i, 0, h)),
                  pl.BlockSpec((1, wid, sall), lambda i, h, j: (i, h, 0))],
        out_specs=pl.BlockSpec((1, ATT_TQ, wid), lambda i, h, j: (i, j, h)),
        out_shape=jax.ShapeDtypeStruct((b, s_lat, 512), BF16),
        scratch_shapes=[pltpu.VMEM((hps, sall, 2 * ATT_TQ), F32),
                        pltpu.VMEM((hps, -(-nchunk // 8) * 8, 2 * ATT_TQ), F32)],
        compiler_params=_params(("arbitrary", "arbitrary", "arbitrary")),
        name="diff_attn",
    )(lamv, subg, qd, kd, vdt)


def _ctx_kernel(sink_ref, lamv_ref, subg_ref, qa_ref, ka_ref, va_ref, qb_ref, kb_ref, vb_ref,
                qd_ref, kd_ref, vd_ref, oa_ref, ob_ref, od_ref, *, lam_init):
    cl = qa_ref.shape[1]
    ka = ka_ref[0]
    va = va_ref[0]
    for g in range(2):
        q2 = _gqa_rows(qa_ref, g)
        s = _dot_nt(q2, ka)
        snk = _gqa_sink(sink_ref, g, cl)
        m = jnp.maximum(jnp.max(s, axis=-1, keepdims=True), snk)
        e = jnp.exp(s - m)
        den = jnp.sum(e, axis=-1, keepdims=True) + jnp.exp(snk - m)
        _gqa_store(oa_ref, g, _dot(e.astype(BF16), va) / den, cl)
    for c in range(2):
        sl = slice(c * LANES, (c + 1) * LANES)
        outs = []
        for half in range(2):
            s = _dot_nt(_half_mask(qb_ref[0, :, sl], half), kb_ref[0, :, sl])
            e = jnp.exp(s - jnp.max(s, axis=-1, keepdims=True))
            outs.append(_dot(e.astype(BF16), vb_ref[0, :, sl]) / jnp.sum(e, axis=-1, keepdims=True))
        ob_ref[0, :, sl] = _merge_halves(outs[0], outs[1]).astype(BF16)
    lam = _lam(lamv_ref, lam_init)
    for h in range(N_HEADS):
        sl = slice(h * LANES, (h + 1) * LANES)
        q = qd_ref[0, :, sl]
        q12 = jnp.concatenate([_half_mask(q, 0), _half_mask(q, 1)], axis=0)
        s = _dot_nt(q12, kd_ref[0, :, sl])
        e = jnp.exp2(s - jnp.max(s, axis=-1, keepdims=True))
        den = jnp.sum(e, axis=-1, keepdims=True)
        p = e[:cl] * (1.0 / den[:cl]) - e[cl:] * (lam / den[cl:])
        o = _dot(p.astype(BF16), vd_ref[0, :, sl])
        od_ref[0, :, sl] = _diff_post(o, subg_ref[...], lam_init).astype(BF16)


def _ctx_attention(sink, lamv, subg, qa, ka, va, qb, kb, vb, qd, kd, vd, s_lat, lam_init):
    b, sall, _ = qa.shape
    cl = sall - s_lat
    blk = s_lat // cl
    row = lambda width: pl.BlockSpec((1, cl, width), lambda i: (i, blk, 0))
    out = lambda width: pl.BlockSpec((1, cl, width), lambda i: (i, 0, 0))
    return pl.pallas_call(
        functools.partial(_ctx_kernel, lam_init=lam_init),
        grid=(b,),
        in_specs=[pl.BlockSpec(memory_space=pltpu.SMEM),
                  pl.BlockSpec((8, LANES), lambda i: (0, 0)),
                  pl.BlockSpec((1, LANES), lambda i: (0, 0)),
                  row(512), row(128), row(128), row(256), row(256), row(256), row(512), row(512), row(512)],
        out_specs=[out(256), out(256), out(512)],
        out_shape=[jax.ShapeDtypeStruct((b, cl, w), BF16) for w in (256, 256, 512)],
        compiler_params=_params(("arbitrary",)),
        name="ctx_attention",
    )(sink, lamv, subg, qa, ka, va, qb, kb, vb, qd, kd, vd)


def _first_argmax(v, iota, n):
    m = jnp.max(v, axis=0, keepdims=True)
    ix = jnp.min(jnp.where(v == m, iota, float(n)), axis=0, keepdims=True)
    return m, ix


def _router_kernel(xin_ref, wout_ref, g_ref, mod_ref, whi_ref, wlo_ref, bias_ref, tri_ref, ltri_ref, *refs,
                   tiles_per_batch, nlat):
    attn_refs, (x_ref, hb_ref, w_ref, slot_ref, meta_ref, cnt_ref) = refs[:-6], refs[-6:]

    @pl.when(pl.program_id(0) == 0)
    def _():
        cnt_ref[...] = jnp.zeros_like(cnt_ref)

    def residual(oa_ref, ob_ref, od_ref):
        attn = (_dot(oa_ref[0], wout_ref[0:256, :]) + _dot(ob_ref[0], wout_ref[256:512, :])
                + _dot(od_ref[0], wout_ref[512:1024, :]))
        x_ref[...] = xin_ref[0] + mod_ref[0, 2:3, :] * attn

    if len(attn_refs) == 3:
        residual(*attn_refs)
    else:
        is_ctx = (pl.program_id(0) % tiles_per_batch) >= nlat
        pl.when(jnp.logical_not(is_ctx))(lambda: residual(*attn_refs[:3]))
        pl.when(is_ctx)(lambda: residual(*attn_refs[3:6]))

    hf = _norm_mod(x_ref[...], g_ref[...], mod_ref[0, 3:4, :], mod_ref[0, 4:5, :])
    tm = hf.shape[0]
    h_hi, h_lo = _split_bf16(hf)
    hb_ref[...] = h_hi
    whi = whi_ref[...]
    logits = _dot_nt(whi, h_hi) + _dot_nt(whi, h_lo) + _dot_nt(wlo_ref[...], h_hi)
    scores = _sigmoid(logits)
    biased = scores + bias_ref[...]
    gsz = N_EXPERTS // N_GROUPS
    iota_g = lax.broadcasted_iota(I32, (gsz, tm), 0).astype(F32)
    gscore = []
    for g in range(N_GROUPS):
        v = biased[g * gsz:(g + 1) * gsz]
        m1, i1 = _first_argmax(v, iota_g, gsz)
        m2 = jnp.max(jnp.where(iota_g == i1, -jnp.inf, v), axis=0, keepdims=True)
        gscore.append(m1 + m2)
    cur = jnp.concatenate(gscore, axis=0)
    iota_n = lax.broadcasted_iota(I32, (N_GROUPS, tm), 0).astype(F32)
    gsel = jnp.zeros((N_GROUPS, tm), F32)
    for _ in range(TOPK_GROUPS):
        _, ix = _first_argmax(cur, iota_n, N_GROUPS)
        hit = iota_n == ix
        gsel = jnp.where(hit, 1.0, gsel)
        cur = jnp.where(hit, -jnp.inf, cur)
    masked = jnp.concatenate(
        [jnp.where(gsel[g:g + 1] > 0.5, biased[g * gsz:(g + 1) * gsz], -jnp.inf) for g in range(N_GROUPS)],
        axis=0)
    iota_e = lax.broadcasted_iota(I32, (N_EXPERTS, tm), 0).astype(F32)
    onehot = jnp.zeros((N_EXPERTS, tm), F32)
    idxs, ws = [], []
    for _ in range(TOP_K):
        _, ix = _first_argmax(masked, iota_e, N_EXPERTS)
        hit = iota_e == ix
        ws.append(jnp.sum(jnp.where(hit, scores, 0.0), axis=0, keepdims=True))
        masked = jnp.where(hit, -jnp.inf, masked)
        onehot = jnp.where(hit, 1.0, onehot)
        idxs.append(ix)
    base = cnt_ref[...]
    rank = _dot(onehot.astype(BF16), tri_ref[...])
    n_col = jnp.sum(onehot, axis=1, keepdims=True)
    n_pad = jnp.floor((n_col + (ROW_CHUNK - 1)) * (1.0 / ROW_CHUNK)) * ROW_CHUNK
    start = _dot(ltri_ref[...], jnp.broadcast_to(n_pad, (N_EXPERTS, LANES)).astype(BF16))
    wsum = ws[0]
    for wk in ws[1:]:
        wsum = wsum + wk
    for k in range(TOP_K):
        slot = jnp.sum(jnp.where(iota_e == idxs[k], rank + start[:, 0:1], 0.0), axis=0, keepdims=True)
        slot_ref[0, k:k + 1, :] = slot.astype(I32)
        w_ref[0, k:k + 1, :] = ws[k] / wsum * ROUTED_SCALE
    lane = lax.broadcasted_iota(I32, (N_EXPERTS, LANES), 1)
    meta_ref[0] = jnp.where(lane == 0, n_pad, jnp.where(lane == 1, base, start))
    cnt_ref[...] = base + n_pad


def _router(xc, lat, ctx, w_out, g, modall, whi, wlo, bias, tri, ltri, s_lat):
    b, sall, d = xc.shape
    tm = TOK_TILE
    nlat = s_lat // tm
    tpb = nlat if ctx is None else sall // tm
    nt = b * tpb
    per_tok = pl.BlockSpec((1, TOP_K, tm), lambda i: (i, 0, 0))
    lat_row = lambda width: pl.BlockSpec((1, tm, width), lambda i: (i // tpb, jnp.minimum(i % tpb, nlat - 1), 0))
    ctx_row = lambda width: pl.BlockSpec((1, tm, width), lambda i: (i // tpb, 0, 0))
    widths = (256, 256, 512)
    attn_specs = [lat_row(w) for w in widths] + ([] if ctx is None else [ctx_row(w) for w in widths])

    def mod_map(i):
        return (2 * (i // tpb) + ((i % tpb) >= nlat).astype(I32), 0, 0)

    return pl.pallas_call(
        functools.partial(_router_kernel, tiles_per_batch=tpb, nlat=nlat),
        grid=(nt,),
        in_specs=[pl.BlockSpec((1, tm, d), lambda i: (i // tpb, i % tpb, 0)),
                  pl.BlockSpec((MIX_WIDTH, d), lambda i: (0, 0)),
                  pl.BlockSpec((1, d), lambda i: (0, 0)),
                  pl.BlockSpec((1, 8, d), mod_map),
                  pl.BlockSpec((N_EXPERTS, d), lambda i: (0, 0)),
                  pl.BlockSpec((N_EXPERTS, d), lambda i: (0, 0)),
                  pl.BlockSpec((N_EXPERTS, 1), lambda i: (0, 0)),
                  pl.BlockSpec((tm, tm), lambda i: (0, 0)),
                  pl.BlockSpec((N_EXPERTS, N_EXPERTS), lambda i: (0, 0))] + attn_specs,
        out_specs=[pl.BlockSpec((tm, d), lambda i: (i, 0)), pl.BlockSpec((tm, d), lambda i: (i, 0)), per_tok, per_tok,
                   pl.BlockSpec((1, N_EXPERTS, LANES), lambda i: (i, 0, 0)),
                   pl.BlockSpec((N_EXPERTS, 1), lambda i: (0, 0))],
        out_shape=[jax.ShapeDtypeStruct((nt * tm, d), F32),
                   jax.ShapeDtypeStruct((nt * tm, d), BF16),
                   jax.ShapeDtypeStruct((nt, TOP_K, tm), F32),
                   jax.ShapeDtypeStruct((nt, TOP_K, tm), I32),
                   jax.ShapeDtypeStruct((nt, N_EXPERTS, LANES), F32),
                   jax.ShapeDtypeStruct((N_EXPERTS, 1), F32)],
        compiler_params=_params(("arbitrary",)),
        name="router",
    )(xc, w_out, g, modall, whi, wlo, bias, tri, ltri, *lat, *(() if ctx is None else ctx))


def _pack_bf16_pairs(v, is_bf16_valued=False):
    half = v.shape[1] // 2
    if not is_bf16_valued:
        v = v.astype(BF16).astype(F32)
    lo = lax.shift_right_logical(lax.bitcast_convert_type(v[:, :half], U32), jnp.uint32(16))
    hi = lax.bitcast_convert_type(v[:, half:], U32) & jnp.uint32(0xFFFF0000)
    return lo | hi


def _unpack_bf16_pairs(u):
    lo = lax.bitcast_convert_type(lax.shift_left(u, jnp.uint32(16)), F32).astype(BF16)
    hi = lax.bitcast_convert_type(u & jnp.uint32(0xFFFF0000), F32).astype(BF16)
    return lo, hi


def _used_chunks(meta_ref):
    return lax.shift_right_logical(_slots_used(meta_ref), ROW_CHUNK.bit_length() - 1)


def _start_chunk_copies(first, count, make_copy):
    pairs = lax.shift_right_logical(count, 1)

    def start_pair(j, carry):
        make_copy(first + 2 * j).start()
        make_copy(first + 2 * j + 1).start()
        return carry

    lax.fori_loop(0, pairs, start_pair, 0)

    @pl.when(count > 2 * pairs)
    def _():
        make_copy(first + count - 1).start()


def _wait_chunk_copies(total, copy_of_rows):
    group = lax.shift_right_logical(total, WAIT_GROUP.bit_length() - 1)

    def wait_group(j, carry):
        copy_of_rows(WAIT_GROUP * ROW_CHUNK).wait()
        return carry

    def wait_one(j, carry):
        copy_of_rows(ROW_CHUNK).wait()
        return carry

    lax.fori_loop(0, group, wait_group, 0)
    lax.fori_loop(0, total - group * WAIT_GROUP, wait_one, 0)


def _tile_slots(tm):
    return TOP_K * tm + ROW_CHUNK * N_EXPERTS


def _slots_used(meta_ref):
    last = N_EXPERTS - 1
    return meta_ref[0, 2, last] + meta_ref[0, 0, last]


def _dispatch_kernel(meta_ref, rows_ref, hb_ref, slot_ref, xs_ref, xc_scr, sem):
    hb = hb_ref[...]
    tm = hb.shape[0]
    slot16 = slot_ref[0].astype(jnp.int16)

    total = _used_chunks(meta_ref)
    step = CHUNK_UNROLL * tm
    chunks_per_step = step // ROW_CHUNK

    def make_copy(j):
        src = pl.multiple_of(j * ROW_CHUNK, ROW_CHUNK)
        dst = pl.multiple_of(rows_ref[0, 0, j], ROW_CHUNK)
        return pltpu.make_async_copy(xc_scr.at[pl.ds(src, ROW_CHUNK)], xs_ref.at[pl.ds(dst, ROW_CHUNK)], sem)

    def select_rows(r2, carry):
        for u in range(CHUNK_UNROLL):
            r0 = pl.multiple_of((r2 * CHUNK_UNROLL + u) * tm, tm)
            row = (lax.broadcasted_iota(I32, (tm, tm), 0) + r0).astype(jnp.int16)
            sel = jnp.zeros((tm, tm), BF16)
            for k in range(TOP_K):
                sel = jnp.where(row == slot16[k:k + 1, :], jnp.ones((), BF16), sel)
            xc_scr[pl.ds(r0, tm), :] = _pack_bf16_pairs(_dot(sel, hb), is_bf16_valued=True)
        first = r2 * chunks_per_step
        _start_chunk_copies(first, jnp.clip(total - first, 0, chunks_per_step), make_copy)
        return carry

    lax.fori_loop(0, (_slots_used(meta_ref) + step - 1) // step, select_rows, 0)
    _wait_chunk_copies(total, lambda n: pltpu.make_async_copy(xc_scr.at[pl.ds(0, n)], xs_ref.at[pl.ds(0, n)], sem))


def _dispatch(meta, chunk_rows, slot, hb, n_rows):
    t, d = hb.shape
    tm = TOK_TILE
    nt = t // tm
    return pl.pallas_call(
        _dispatch_kernel,
        grid=(nt,),
        in_specs=[pl.BlockSpec((1, 3, N_EXPERTS), lambda i: (i, 0, 0), memory_space=pltpu.SMEM),
                  pl.BlockSpec((1, 1, chunk_rows.shape[2]), lambda i: (i, 0, 0), memory_space=pltpu.SMEM),
                  pl.BlockSpec((tm, d), lambda i: (i, 0)),
                  pl.BlockSpec((1, TOP_K, tm), lambda i: (i, 0, 0))],
        out_specs=pl.BlockSpec(memory_space=pl.ANY),
        out_shape=jax.ShapeDtypeStruct((n_rows, d // 2), U32),
        scratch_shapes=[pltpu.VMEM((_tile_slots(tm), d // 2), U32), pltpu.SemaphoreType.DMA(())],
        compiler_params=_params(("arbitrary",)),
        name="moe_dispatch",
    )(meta, chunk_rows, hb, slot)


def _expert_kernel(blk_e_ref, nvalid_ref, nact_ref, next_e_ref, has_next_ref, parity_ref,
                   xs_ref, wg_any, wu_any, wd_any, ys_ref, wg_s, wu_s, wd_s, wg_buf, wu_buf, wd_buf, sem, *, layer):
    b = pl.program_id(0)
    e = blk_e_ref[b]
    changed = jnp.logical_or(b == 0, e != blk_e_ref[jnp.maximum(b - 1, 0)])
    par = parity_ref[b]

    def weight_copies(expert, p):
        return [pltpu.make_async_copy(src.at[layer, expert], dst.at[p], sem.at[p])
                for src, dst in ((wg_any, wg_buf), (wu_any, wu_buf), (wd_any, wd_buf))]

    @pl.when(b == 0)
    def _():
        for c in weight_copies(e, par):
            c.start()

    @pl.when(changed)
    def _():
        for c in weight_copies(e, par):
            c.wait()
        wg_s[...] = wg_buf[par].astype(BF16)
        wu_s[...] = wu_buf[par].astype(BF16)
        wd_s[...] = wd_buf[par].astype(BF16)

        @pl.when(has_next_ref[b] == 1)
        def _():
            for c in weight_copies(next_e_ref[b], 1 - par):
                c.start()

    half = xs_ref.shape[1]
    n_here = jnp.where(b < nact_ref[0], nvalid_ref[b], 0)

    @pl.when(n_here > 0)
    def _():
        row = lax.broadcasted_iota(I32, xs_ref.shape, 0)
        xs = jnp.where(row < n_here, xs_ref[...], jnp.zeros(xs_ref.shape, U32))
        x_lo, x_hi = _unpack_bf16_pairs(xs)
        gate = _dot(x_lo, wg_s[0:half, :]) + _dot(x_hi, wg_s[half:, :])
        up = _dot(x_lo, wu_s[0:half, :]) + _dot(x_hi, wu_s[half:, :])
        mid = (gate * _sigmoid(gate)) * up
        ys_ref[...] = _pack_bf16_pairs(_dot(mid.astype(BF16), wd_s[...]))

    @pl.when(n_here == 0)
    def _():
        ys_ref[...] = jnp.zeros_like(ys_ref)


def _experts(blk_e, nvalid, nact, xs, w_gate, w_up, w_down, layer):
    n_rows, half = xs.shape
    _, _, d, f = w_gate.shape
    n_blk = n_rows // MOE_BLOCK
    first_of_next = jnp.sum((blk_e[None, :] <= blk_e[:, None]).astype(I32), axis=1)
    has_next = (first_of_next < n_blk).astype(I32)
    next_e = blk_e[jnp.minimum(first_of_next, n_blk - 1)]
    starts = jnp.concatenate([jnp.zeros((1,), I32), (blk_e[1:] != blk_e[:-1]).astype(I32)])
    parity = jnp.cumsum(starts) % 2
    anyspec = pl.BlockSpec(memory_space=pl.ANY)
    grid_spec = pltpu.PrefetchScalarGridSpec(
        num_scalar_prefetch=6,
        grid=(n_blk,),
        in_specs=[pl.BlockSpec((MOE_BLOCK, half), lambda i, be, nv, na, *_: (jnp.minimum(i, na[0] - 1), 0)),
                  anyspec, anyspec, anyspec],
        out_specs=pl.BlockSpec((MOE_BLOCK, half), lambda i, be, nv, na, *_: (jnp.minimum(i, na[0]), 0)),
        scratch_shapes=[pltpu.VMEM((d, f), BF16), pltpu.VMEM((d, f), BF16), pltpu.VMEM((f, d), BF16),
                        pltpu.VMEM((2, d, f), F32), pltpu.VMEM((2, d, f), F32), pltpu.VMEM((2, f, d), F32),
                        pltpu.SemaphoreType.DMA((2,))],
    )
    return pl.pallas_call(
        functools.partial(_expert_kernel, layer=layer),
        grid_spec=grid_spec,
        out_shape=jax.ShapeDtypeStruct((n_rows, half), U32),
        compiler_params=_params(("arbitrary",)),
        name="moe_experts",
    )(blk_e, nvalid, nact, next_e, has_next, parity.astype(I32), xs, w_gate, w_up, w_down)


def _combine_kernel(meta_ref, rows_ref, x_ref, hb_ref, mod_ref, spos_ref, wt_ref, wsg_ref, wsu_ref, wsd_ref,
                    gfin_ref, ys_ref, o_ref, y_scr, acc_scr, sb_scr, wb_scr, sem, *, final_norm):
    tm, d = x_ref.shape
    half = d // 2

    @pl.when(pl.program_id(0) == 0)
    def _():
        y_scr[...] = jnp.zeros_like(y_scr)

    step = CHUNK_UNROLL * COMBINE_KCHUNK
    chunks_per_step = step // ROW_CHUNK
    group_shift = chunks_per_step.bit_length() - 1

    def make_copy(j):
        src = pl.multiple_of(rows_ref[0, 0, j], ROW_CHUNK)
        dst = pl.multiple_of(j * ROW_CHUNK, ROW_CHUNK)
        return pltpu.make_async_copy(ys_ref.at[pl.ds(src, ROW_CHUNK)], y_scr.at[pl.ds(dst, ROW_CHUNK)],
                                     sem.at[lax.shift_right_logical(j, group_shift)])

    total = _used_chunks(meta_ref)
    _start_chunk_copies(0, total, make_copy)

    x = x_ref[...]
    hb = hb_ref[...]
    gate = _dot(hb, wsg_ref[...])
    mid = (gate * _sigmoid(gate)) * _dot(hb, wsu_ref[...])
    acc_scr[...] = _dot(mid.astype(BF16), wsd_ref[...])

    n_used = _slots_used(meta_ref)
    for k in range(TOP_K):
        sb_scr[k] = jnp.broadcast_to(spos_ref[0, :, k:k + 1], (tm, LANES)).astype(jnp.int16)
        wb_scr[k] = jnp.broadcast_to(wt_ref[0, :, k:k + 1], (tm, LANES)).astype(BF16)

    def chunk(j2, carry):
        _wait_chunk_copies(jnp.clip(total - j2 * chunks_per_step, 0, chunks_per_step),
                           lambda n: pltpu.make_async_copy(ys_ref.at[pl.ds(0, n)], y_scr.at[pl.ds(0, n)], sem.at[j2]))
        lo_sum = jnp.zeros((tm, half), F32)
        hi_sum = jnp.zeros((tm, half), F32)
        for u in range(CHUNK_UNROLL):
            r0 = pl.multiple_of((j2 * CHUNK_UNROLL + u) * COMBINE_KCHUNK, COMBINE_KCHUNK)
            parts = []
            for c in range(COMBINE_KCHUNK // LANES):
                col = (lax.broadcasted_iota(I32, (tm, LANES), 1) + (r0 + c * LANES)).astype(jnp.int16)
                p = jnp.zeros((tm, LANES), BF16)
                for k in range(TOP_K):
                    p = jnp.where(col == sb_scr[k], wb_scr[k], p)
                parts.append(p)
            pb = jnp.concatenate(parts, axis=1)
            y_lo, y_hi = _unpack_bf16_pairs(y_scr[pl.ds(r0, COMBINE_KCHUNK), :])
            lo_sum = lo_sum + _dot(pb, y_lo)
            hi_sum = hi_sum + _dot(pb, y_hi)
        acc_scr[:, :half] += lo_sum
        acc_scr[:, half:] += hi_sum
        return carry

    lax.fori_loop(0, (n_used + step - 1) // step, chunk, 0)
    out = x + mod_ref[0, 5:6, :] * acc_scr[...]
    if final_norm:
        out = _rms(out, gfin_ref[...])
    o_ref[...] = out


def _combine(meta, chunk_rows, spos_t, wt, xflat, hb, modall, wsg, wsu, wsd, gfin, ys, tiles_per_batch, nlat,
             final_norm):
    t, d = xflat.shape
    f = wsg.shape[1]
    tm = TOK_TILE
    nt = t // tm
    y_rows = _tile_slots(tm)
    assert y_rows % COMBINE_KCHUNK == 0

    def mod_map(i):
        return (2 * (i // tiles_per_batch) + ((i % tiles_per_batch) >= nlat).astype(I32), 0, 0)

    return pl.pallas_call(
        functools.partial(_combine_kernel, final_norm=final_norm),
        grid=(nt,),
        in_specs=[pl.BlockSpec((1, 3, N_EXPERTS), lambda i: (i, 0, 0), memory_space=pltpu.SMEM),
                  pl.BlockSpec((1, 1, chunk_rows.shape[2]), lambda i: (i, 0, 0), memory_space=pltpu.SMEM),
                  pl.BlockSpec((tm, d), lambda i: (i, 0)),
                  pl.BlockSpec((tm, d), lambda i: (i, 0)),
                  pl.BlockSpec((1, 8, d), mod_map),
                  pl.BlockSpec((1, tm, TOP_K), lambda i: (i, 0, 0)),
                  pl.BlockSpec((1, tm, TOP_K), lambda i: (i, 0, 0)),
                  pl.BlockSpec((d, f), lambda i: (0, 0)),
                  pl.BlockSpec((d, f), lambda i: (0, 0)),
                  pl.BlockSpec((f, d), lambda i: (0, 0)),
                  pl.BlockSpec((1, d), lambda i: (0, 0)),
                  pl.BlockSpec(memory_space=pl.ANY)],
        out_specs=pl.BlockSpec((tm, d), lambda i: (i, 0)),
        out_shape=jax.ShapeDtypeStruct((t, d), F32),
        scratch_shapes=[pltpu.VMEM((y_rows, d // 2), U32), pltpu.VMEM((tm, d), F32),
                        pltpu.VMEM((TOP_K, tm, LANES), jnp.int16), pltpu.VMEM((TOP_K, tm, LANES), BF16),
                        pltpu.SemaphoreType.DMA((y_rows // (CHUNK_UNROLL * COMBINE_KCHUNK),))],
        compiler_params=_params(("arbitrary",)),
        name="moe_combine",
    )(meta, chunk_rows, xflat, hb, modall, spos_t, wt, wsg, wsu, wsd, gfin, ys)


def _rope_tables(s_lat, c_len):
    n_freq = HEAD_DIM // 4
    inv = 1.0 / (ROPE_BASE ** (jnp.arange(n_freq, dtype=F32) / n_freq))
    t = jnp.arange(s_lat)
    pos = jnp.stack([(t // GRID_W).astype(F32), (t % GRID_W).astype(F32)], axis=1)
    lane = np.arange(LANES)
    ang = pos[:, (lane % HEAD_DIM) // 32] * inv[lane % 16][None, :]
    sign = jnp.asarray(np.where(lane % 32 < 16, -1.0, 1.0), F32)
    cos = jnp.concatenate([jnp.cos(ang), jnp.ones((c_len, LANES), F32)], axis=0)
    sin = jnp.concatenate([jnp.sin(ang) * sign, jnp.zeros((c_len, LANES), F32)], axis=0)
    return cos, sin


def _moe(xc, lat, ctx, lp, modall, gfin, s_lat, final_norm):
    d = xc.shape[2]
    nlat = s_lat // TOK_TILE
    tiles_per_batch = nlat if ctx is None else xc.shape[1] // TOK_TILE
    g = lp["g_ffn"].reshape(1, d)
    w_r = lp["w_router"].T
    whi = w_r.astype(BF16)
    wlo = (w_r - whi.astype(F32)).astype(BF16)
    tri = jnp.asarray(np.triu(np.ones((TOK_TILE, TOK_TILE), np.float32), 1), BF16)
    ltri = jnp.asarray(np.tril(np.ones((N_EXPERTS, N_EXPERTS), np.float32), -1), BF16)
    xflat, hb, w, slot, meta, cnt = _router(xc, lat, ctx, lp["w_out"].astype(BF16), g, modall, whi, wlo,
                                            lp["router_bias"].reshape(N_EXPERTS, 1), tri, ltri, s_lat)
    t = xflat.shape[0]
    nt = t // TOK_TILE
    meta = jnp.transpose(meta[:, :, :3].astype(I32), (0, 2, 1))
    rows_e = cnt[:, 0].astype(I32)
    seg = (rows_e + MOE_BLOCK - 1) // MOE_BLOCK * MOE_BLOCK
    pend = jnp.cumsum(seg)
    pstart = (pend - seg).astype(I32)
    n_blk = (t * TOP_K + ROW_CHUNK * N_EXPERTS * nt) // MOE_BLOCK + N_EXPERTS
    blk_row0 = jnp.arange(n_blk, dtype=I32) * MOE_BLOCK
    blk_e = jnp.minimum(jnp.sum((pend[None, :] <= blk_row0[:, None]).astype(I32), axis=1), N_EXPERTS - 1)
    of_blk = blk_e[:, None] == jnp.arange(N_EXPERTS, dtype=I32)[None, :]
    seg_end = jnp.sum(jnp.where(of_blk, (pstart + rows_e)[None, :], 0), axis=1)
    nvalid = jnp.clip(seg_end - blk_row0, 0, MOE_BLOCK).astype(I32)
    nact = (pend[-1] // MOE_BLOCK).astype(I32).reshape(1)
    n_pad, base, start = meta[:, 0, :], meta[:, 1, :], meta[:, 2, :]
    chunk_slot = jnp.arange(_tile_slots(TOK_TILE) // ROW_CHUNK, dtype=I32) * ROW_CHUNK
    s = chunk_slot[None, :, None]
    in_run = (start[:, None, :] <= s) & (s < (start + n_pad)[:, None, :])
    row0 = pstart[None, :] + base - start
    chunk_rows = (jnp.sum(jnp.where(in_run, row0[:, None, :], 0), axis=2) + chunk_slot[None, :])[:, None, :]
    xs = _dispatch(meta, chunk_rows, slot, hb, n_blk * MOE_BLOCK)
    ys = _experts(blk_e, nvalid, nact, xs, lp["w_gate"], lp["w_up"], lp["w_down"], lp["layer"])
    return _combine(meta, chunk_rows, jnp.transpose(slot, (0, 2, 1)), jnp.transpose(w, (0, 2, 1)), xflat, hb, modall,
                    lp["ws_gate"].astype(BF16), lp["ws_up"].astype(BF16), lp["ws_down"].astype(BF16), gfin, ys,
                    tiles_per_batch, nlat, final_norm)


def _layer(xc, cs, cos, sin, lp, layer_idx, last, s_lat, gfin):
    b, sall, d = xc.shape
    c_len = sall - s_lat
    lam_init = 0.8 - 0.6 * math.exp(-0.3 * layer_idx)
    mod = _modulation(cs, lp["w_mod"], lp["b_mod"])
    mod_lat = mod[:b].reshape(b, 1, 6, d)
    mod_ctx = jnp.broadcast_to(mod[b].reshape(1, 1, 6, d), (b, 1, 6, d))
    modall = jnp.concatenate([mod_lat, mod_ctx], axis=1)
    modall = jnp.pad(modall, ((0, 0), (0, 0), (0, 2), (0, 0))).reshape(2 * b, 8, d)

    qa, qb, qd, ka, va, kb, vb, kd, vd, vdt = _projection(
        xc, lp["g_mix"].reshape(1, d), modall, cos, sin, lp["w_in"].astype(BF16), s_lat)
    sink = lp["attn_sink"].astype(F32)
    lamv = jnp.zeros((8, LANES), F32).at[0:4, 0:HEAD_DIM].set(
        jnp.stack([lp["lam_q1"], lp["lam_k1"], lp["lam_q2"], lp["lam_k2"]]).astype(F32))
    subg = lp["subln_g"].reshape(1, LANES).astype(F32)
    oa, ob = _local_attention(sink, qa, ka, va, qb, kb, vb, _na_bias_mask(lp["na_rpb"], s_lat // GRID_W), s_lat)
    od = _diff(lamv, subg, qd, kd, vdt, s_lat, lam_init)
    ctx_out = None
    if not last:
        ctx_out = _ctx_attention(sink, lamv, subg, qa, ka, va, qb, kb, vb, qd, kd, vd, s_lat, lam_init)
    n_rows = s_lat if last else sall
    y = _moe(xc, (oa, ob, od), ctx_out, lp, modall, gfin, s_lat, last)
    return y.reshape(b, n_rows, d)


_LAYER_KEYS = ("w_mod", "b_mod", "g_mix", "g_ffn", "w_in", "w_out", "attn_sink", "na_rpb", "lam_q1", "lam_k1",
               "lam_q2", "lam_k2", "subln_g", "w_router", "router_bias", "w_gate", "w_up", "w_down",
               "ws_gate", "ws_up", "ws_down")


def kernel(x, c, ctx, c_ctx, w_mod, b_mod, g_mix, g_ffn, w_in, w_out, attn_sink, na_rpb, lam_q1, lam_k1, lam_q2,
           lam_k2, subln_g, w_router, router_bias, w_gate, w_up, w_down, ws_gate, ws_up, ws_down, g_final):
    stacked = dict(zip(_LAYER_KEYS, (w_mod, b_mod, g_mix, g_ffn, w_in, w_out, attn_sink, na_rpb, lam_q1, lam_k1,
                                     lam_q2, lam_k2, subln_g, w_router, router_bias, w_gate, w_up, w_down,
                                     ws_gate, ws_up, ws_down)))
    b, s_lat, d = x.shape
    c_len = ctx.shape[1]
    depth = w_mod.shape[0]
    assert s_lat % (NA_QROWS * GRID_W) == 0 and s_lat % c_len == 0 and c_len == TOK_TILE
    cs = jnp.zeros((16, d), F32).at[:b].set(c).at[b].set(c_ctx)
    cos, sin = _rope_tables(s_lat, c_len)
    xc = jnp.concatenate([x, ctx], axis=1)
    gfin = g_final.reshape(1, d)
    for i in range(depth):
        big = ("w_gate", "w_up", "w_down")
        lp = {k: (v if k in big else v[i]) for k, v in stacked.items()}
        lp["layer"] = i
        xc = _layer(xc, cs, cos, sin, lp, i, i == depth - 1, s_lat, gfin)
    return xc
```

```python
import functools
import math

import numpy as np
import jax
import jax.numpy as jnp
from jax import lax
from jax.experimental import pallas as pl
from jax.experimental.pallas import tpu as pltpu

F32 = jnp.float32
BF16 = jnp.bfloat16
I32 = jnp.int32
U32 = jnp.uint32

GRID_W = 64
HEAD_DIM = 64
ROPE_BASE = 10000.0
SWA_WINDOW = 128
NA_KH = 8
NA_KW = 16
N_HEADS = 4
N_EXPERTS = 256
TOP_K = 8
N_GROUPS = 8
TOPK_GROUPS = 4
ROUTED_SCALE = 2.5
MOE_BLOCK = 1024
ROW_CHUNK = 8
WAIT_GROUP = 16
COMBINE_KCHUNK = 512
CHUNK_UNROLL = 2
EPS = 1e-6
NEG_INF = -1e30
LOG2E = math.log2(math.e)
Q_COLS = 1024
IN_COLS = 2816
MIX_WIDTH = 1024

LANES = 128
TOK_TILE = 256
ATT_TQ = 256
NA_QROWS = 4
NA_KROWS = NA_QROWS + NA_KH - 1
DIFF_KCHUNK = 256
DIFF_HEADS_PER_STEP = 2
VMEM_LIMIT = 48 * 1024 * 1024


def _params(sem):
    return pltpu.CompilerParams(dimension_semantics=sem, vmem_limit_bytes=VMEM_LIMIT)


def _dot(a, b):
    return jnp.dot(a, b, preferred_element_type=F32)


def _dot_nt(a, b):
    return lax.dot_general(a, b, (((1,), (1,)), ((), ())), preferred_element_type=F32)


def _split_bf16(a):
    hi = a.astype(BF16)
    lo = (a - hi.astype(F32)).astype(BF16)
    return hi, lo


def _sigmoid(x):
    return 1.0 / (1.0 + jnp.exp(-x))


def _rms(x, g):
    return x * lax.rsqrt(jnp.mean(x * x, axis=-1, keepdims=True) + EPS) * g


def _norm_mod(x, g, shift, scale):
    return _rms(x, g) * (1.0 + scale) + shift


def _mod_kernel(c_ref, w_ref, b_ref, o_ref):
    c = c_ref[...]
    a_hi, a_lo = _split_bf16(c * _sigmoid(c))
    w_hi, w_lo = _split_bf16(w_ref[...])
    o_ref[...] = _dot(a_hi, w_hi) + _dot(a_hi, w_lo) + _dot(a_lo, w_hi) + b_ref[...]


def _modulation(cs, w_mod, b_mod):
    n, d = cs.shape
    cols = w_mod.shape[1]
    tn = 1536
    return pl.pallas_call(
        _mod_kernel,
        grid=(cols // tn,),
        in_specs=[pl.BlockSpec((n, d), lambda j: (0, 0)),
                  pl.BlockSpec((d, tn), lambda j: (0, j)),
                  pl.BlockSpec((1, tn), lambda j: (0, j))],
        out_specs=pl.BlockSpec((n, tn), lambda j: (0, j)),
        out_shape=jax.ShapeDtypeStruct((n, cols), F32),
        compiler_params=_params(("arbitrary",)),
        name="modulation",
    )(cs, w_mod, b_mod.reshape(1, cols))


def _proj_kernel(x_ref, g_ref, mod_ref, cos_ref, sin_ref, w_ref,
                 qa_ref, qb_ref, qd_ref, ka_ref, va_ref, kb_ref, vb_ref, kd_ref, vd_ref, vdt_ref, h_scr):
    h = _norm_mod(x_ref[0], g_ref[...], mod_ref[0, 0:1, :], mod_ref[0, 1:2, :])
    h_scr[...] = h.astype(BF16)
    cos = cos_ref[...]
    sin = sin_ref[...]
    tm = h.shape[0]
    lane = lax.broadcasted_iota(I32, (tm, LANES), 1)
    first16 = (lane & 16) == 0
    lo64 = lane < HEAD_DIM

    def rope(v):
        partner = jnp.where(first16, pltpu.roll(v, LANES - 16, 1), pltpu.roll(v, 16, 1))
        return v * cos + partner * sin

    def mm(c0):
        return _dot(h_scr[...], w_ref[:, c0:c0 + 2 * LANES])

    a = mm(0) * (HEAD_DIM ** -0.5)
    c0 = rope(a[:, :LANES])
    c1 = rope(a[:, LANES:])
    zero = jnp.zeros_like(c0)
    qa_ref[0, :, 0 * LANES:1 * LANES] = jnp.where(lo64, c0, zero).astype(BF16)
    qa_ref[0, :, 1 * LANES:2 * LANES] = jnp.where(lo64, pltpu.roll(c0, HEAD_DIM, 1), zero).astype(BF16)
    qa_ref[0, :, 2 * LANES:3 * LANES] = jnp.where(lo64, zero, pltpu.roll(c1, HEAD_DIM, 1)).astype(BF16)
    qa_ref[0, :, 3 * LANES:4 * LANES] = jnp.where(lo64, zero, c1).astype(BF16)
    qb_ref[0] = (mm(256) * (HEAD_DIM ** -0.5)).astype(BF16)
    for j in range(2):
        a = mm(512 + 256 * j) * (HEAD_DIM ** -0.5 * LOG2E)
        qd_ref[0, :, 256 * j:256 * j + LANES] = rope(a[:, :LANES]).astype(BF16)
        qd_ref[0, :, 256 * j + LANES:256 * (j + 1)] = rope(a[:, LANES:]).astype(BF16)
    a = mm(1024)
    ka_ref[0] = rope(a[:, :LANES]).astype(BF16)
    va_ref[0] = a[:, LANES:].astype(BF16)
    kb_ref[0] = mm(1280).astype(BF16)
    vb_ref[0] = mm(1536).astype(BF16)
    for j in range(2):
        a = mm(1792 + 256 * j)
        kd_ref[0, :, 256 * j:256 * j + LANES] = rope(a[:, :LANES]).astype(BF16)
        kd_ref[0, :, 256 * j + LANES:256 * (j + 1)] = rope(a[:, LANES:]).astype(BF16)
        a = mm(2304 + 256 * j)
        vd_ref[0, :, 256 * j:256 * (j + 1)] = a.astype(BF16)
        vdt_ref[0, 256 * j:256 * j + LANES, :] = a[:, :LANES].T.astype(BF16)
        vdt_ref[0, 256 * j + LANES:256 * (j + 1), :] = a[:, LANES:].T.astype(BF16)


def _projection(xc, g, modall, cos, sin, w_in, s_lat):
    b, sall, d = xc.shape
    tm = TOK_TILE
    nlat = s_lat // tm
    row = lambda width: pl.BlockSpec((1, tm, width), lambda i, j: (i, j, 0))
    widths = (512, 256, 512, 128, 128, 256, 256, 512, 512)
    return pl.pallas_call(
        _proj_kernel,
        grid=(b, sall // tm),
        in_specs=[row(d),
                  pl.BlockSpec((1, d), lambda i, j: (0, 0)),
                  pl.BlockSpec((1, 8, d), lambda i, j: (2 * i + (j >= nlat).astype(I32), 0, 0)),
                  pl.BlockSpec((tm, LANES), lambda i, j: (j, 0)),
                  pl.BlockSpec((tm, LANES), lambda i, j: (j, 0)),
                  pl.BlockSpec((d, IN_COLS), lambda i, j: (0, 0))],
        out_specs=[row(w) for w in widths] + [pl.BlockSpec((1, 512, tm), lambda i, j: (i, 0, j))],
        out_shape=([jax.ShapeDtypeStruct((b, sall, w), BF16) for w in widths]
                   + [jax.ShapeDtypeStruct((b, 512, sall), BF16)]),
        scratch_shapes=[pltpu.VMEM((tm, d), BF16)],
        compiler_params=_params(("arbitrary", "arbitrary")),
        name="projection",
    )(xc, g, modall, cos, sin, w_in)


def _half_mask(q, half):
    lane = lax.broadcasted_iota(I32, q.shape, 1)
    keep = (lane < HEAD_DIM) if half == 0 else (lane >= HEAD_DIM)
    return jnp.where(keep, q, jnp.zeros_like(q))


def _merge_halves(lo_part, hi_part):
    lane = lax.broadcasted_iota(I32, lo_part.shape, 1)
    return jnp.where(lane < HEAD_DIM, lo_part, hi_part)


def _gqa_rows(q_ref, g):
    return jnp.concatenate([q_ref[0, :, (2 * g) * LANES:(2 * g + 1) * LANES],
                            q_ref[0, :, (2 * g + 1) * LANES:(2 * g + 2) * LANES]], axis=0)


def _gqa_sink(sink_ref, g, tq):
    rowi = lax.broadcasted_iota(I32, (2 * tq, 1), 0)
    return jnp.where(rowi < tq, sink_ref[2 * g], sink_ref[2 * g + 1])


def _gqa_store(o_ref, g, o, tq):
    top, bot = o[:tq], o[tq:]
    if g == 0:
        chunk = _merge_halves(top, pltpu.roll(bot, HEAD_DIM, 1))
    else:
        chunk = _merge_halves(pltpu.roll(top, HEAD_DIM, 1), bot)
    o_ref[0, :, g * LANES:(g + 1) * LANES] = chunk.astype(BF16)


def _lam(lamv_ref, lam_init):
    v = lamv_ref[...]
    a = jnp.sum(v[0:1] * v[1:2], axis=-1, keepdims=True)
    b = jnp.sum(v[2:3] * v[3:4], axis=-1, keepdims=True)
    return jnp.exp(a) - jnp.exp(b) + lam_init


def _diff_post(o, subg, lam_init):
    return _rms(o, subg) * (1.0 - lam_init)


def _swa_kernel(sink_ref, q_ref, k_ref, v_ref, o_ref, *, s_lat, c_len):
    tq = ATT_TQ
    tk = tq + 2 * SWA_WINDOW
    q0 = pl.program_id(1) * tq
    ks = pl.multiple_of(jnp.clip(q0 - SWA_WINDOW, 0, s_lat - tk), SWA_WINDOW)
    kwin = k_ref[0, pl.ds(ks, tk), :]
    vwin = v_ref[0, pl.ds(ks, tk), :]
    kc = k_ref[0, s_lat:s_lat + c_len, :]
    vc = v_ref[0, s_lat:s_lat + c_len, :]
    qpos = q0 + lax.broadcasted_iota(I32, (tq, tk), 0)
    kpos = ks + lax.broadcasted_iota(I32, (tq, tk), 1)
    valid = jnp.abs(qpos - kpos) <= SWA_WINDOW
    valid = jnp.concatenate([valid, valid], axis=0)
    for g in range(2):
        q2 = _gqa_rows(q_ref, g)
        s_loc = jnp.where(valid, _dot_nt(q2, kwin), NEG_INF)
        s_ctx = _dot_nt(q2, kc)
        snk = _gqa_sink(sink_ref, g, tq)
        m = jnp.maximum(jnp.maximum(jnp.max(s_loc, axis=-1, keepdims=True),
                                    jnp.max(s_ctx, axis=-1, keepdims=True)), snk)
        e_loc = jnp.exp(s_loc - m)
        e_ctx = jnp.exp(s_ctx - m)
        den = (jnp.sum(e_loc, axis=-1, keepdims=True) + jnp.sum(e_ctx, axis=-1, keepdims=True)
               + jnp.exp(snk - m))
        o = (_dot(e_loc.astype(BF16), vwin) + _dot(e_ctx.astype(BF16), vc)) / den
        _gqa_store(o_ref, g, o, tq)


def _na_kernel(q_ref, k_ref, v_ref, bm_ref, o_ref, *, s_lat, c_len):
    rows = s_lat // GRID_W
    nk = NA_KROWS * GRID_W
    r0 = pl.program_id(1) * NA_QROWS
    ks = pl.multiple_of(jnp.clip(r0 - NA_KH // 2, 0, rows - NA_KROWS) * GRID_W, GRID_W)
    for c in range(2):
        sl = slice(c * LANES, (c + 1) * LANES)
        kwin = k_ref[0, pl.ds(ks, nk), sl]
        vwin = v_ref[0, pl.ds(ks, nk), sl]
        kc = k_ref[0, s_lat:s_lat + c_len, sl]
        vc = v_ref[0, s_lat:s_lat + c_len, sl]
        q = q_ref[0, :, sl]
        outs = []
        for half in range(2):
            qm = _half_mask(q, half)
            s_loc = _dot_nt(qm, kwin) + bm_ref[0, 2 * c + half]
            s_ctx = _dot_nt(qm, kc)
            m = jnp.maximum(jnp.max(s_loc, axis=-1, keepdims=True), jnp.max(s_ctx, axis=-1, keepdims=True))
            e_loc = jnp.exp(s_loc - m)
            e_ctx = jnp.exp(s_ctx - m)
            den = jnp.sum(e_loc, axis=-1, keepdims=True) + jnp.sum(e_ctx, axis=-1, keepdims=True)
            outs.append((_dot(e_loc.astype(BF16), vwin) + _dot(e_ctx.astype(BF16), vc)) / den)
        o_ref[0, :, sl] = _merge_halves(outs[0], outs[1]).astype(BF16)


def _na_bias_mask(rpb, rows):
    nq = NA_QROWS * GRID_W
    nk = NA_KROWS * GRID_W
    col = np.arange(GRID_W)
    cstart = np.clip(col - NA_KW // 2, 0, GRID_W - NA_KW)
    col_ok = (col[None, :] >= cstart[:, None]) & (col[None, :] < cstart[:, None] + NA_KW)
    dc = np.clip(col[None, :] - col[:, None], -(NA_KW - 1), NA_KW - 1) + NA_KW - 1
    sel_c = np.where(col_ok[:, :, None], np.eye(2 * NA_KW - 1)[dc], 0.0)
    tabs = []
    for r0 in (0, NA_QROWS, rows - NA_QROWS):
        ksr = int(np.clip(r0 - NA_KH // 2, 0, rows - NA_KROWS))
        r = r0 + np.arange(NA_QROWS)
        kr = ksr + np.arange(NA_KROWS)
        rs = np.clip(r - NA_KH // 2, 0, rows - NA_KH)
        row_ok = (kr[None, :] >= rs[:, None]) & (kr[None, :] < rs[:, None] + NA_KH)
        dr = np.clip(kr[None, :] - r[:, None], -(NA_KH - 1), NA_KH - 1) + NA_KH - 1
        sel_r = np.where(row_ok[:, :, None], np.eye(2 * NA_KH - 1)[dr], 0.0)
        ok = (row_ok[:, None, :, None] & col_ok[None, :, None, :]).reshape(nq, nk)
        bias = jnp.einsum("hdc,rjd,qkc->hrqjk", rpb.astype(F32), jnp.asarray(sel_r, F32), jnp.asarray(sel_c, F32),
                          precision=lax.Precision.HIGHEST).reshape(N_HEADS, nq, nk)
        tabs.append(jnp.where(ok[None], bias, NEG_INF))
    return jnp.stack(tabs)


def _local_attn_kernel(sink_ref, qa_ref, ka_ref, va_ref, qb_ref, kb_ref, vb_ref, bm_ref, oa_ref, ob_ref, *,
                       s_lat, c_len):
    _swa_kernel(sink_ref, qa_ref, ka_ref, va_ref, oa_ref, s_lat=s_lat, c_len=c_len)
    _na_kernel(qb_ref, kb_ref, vb_ref, bm_ref, ob_ref, s_lat=s_lat, c_len=c_len)


def _local_attention(sink, qa, ka, va, qb, kb, vb, bm, s_lat):
    b, sall, _ = qa.shape
    nq = NA_QROWS * GRID_W
    assert nq == ATT_TQ
    nsteps = s_lat // nq
    kv = lambda width: pl.BlockSpec((1, sall, width), lambda i, j: (i, 0, 0))
    row = lambda width: pl.BlockSpec((1, nq, width), lambda i, j: (i, j, 0))

    def bm_map(i, j):
        return (jnp.where(j == 0, 0, jnp.where(j == nsteps - 1, 2, 1)), 0, 0, 0)

    return pl.pallas_call(
        functools.partial(_local_attn_kernel, s_lat=s_lat, c_len=sall - s_lat),
        grid=(b, nsteps),
        in_specs=[pl.BlockSpec(memory_space=pltpu.SMEM), row(512), kv(LANES), kv(LANES),
                  row(256), kv(256), kv(256),
                  pl.BlockSpec((1, N_HEADS, nq, NA_KROWS * GRID_W), bm_map)],
        out_specs=[row(256), row(256)],
        out_shape=[jax.ShapeDtypeStruct((b, s_lat, 256), BF16), jax.ShapeDtypeStruct((b, s_lat, 256), BF16)],
        compiler_params=_params(("arbitrary", "arbitrary")),
        name="local_attn",
    )(sink, qa, ka, va, qb, kb, vb, bm)


def _diff_kernel(lamv_ref, subg_ref, q_ref, k_ref, vt_ref, o_ref, e_scr, m_scr, *, n_keys, lam_init):
    tq = ATT_TQ
    chunks = [(c0, min(DIFF_KCHUNK, n_keys - c0)) for c0 in range(0, n_keys, DIFF_KCHUNK)]
    lam = _lam(lamv_ref, lam_init)
    stats = []
    for hh in range(DIFF_HEADS_PER_STEP):
        sl = slice(hh * LANES, (hh + 1) * LANES)
        q = q_ref[0, :, sl]
        q12 = jnp.concatenate([_half_mask(q, 0), _half_mask(q, 1)], axis=0)
        m = jnp.full((1, 2 * tq), NEG_INF, F32)
        den = jnp.zeros((1, 2 * tq), F32)
        for i, (c0, n) in enumerate(chunks):
            s = _dot_nt(k_ref[0, c0:c0 + n, sl], q12)
            m_new = jnp.maximum(m, jnp.max(s, axis=0, keepdims=True))
            e = jnp.exp2(s - m_new)
            e_scr[hh, c0:c0 + n, :] = e
            m_scr[hh, i:i + 1, :] = m_new
            den = den * jnp.exp2(m - m_new) + jnp.sum(e, axis=0, keepdims=True)
            m = m_new
        stats.append((m, den))
    for hh in range(DIFF_HEADS_PER_STEP):
        sl = slice(hh * LANES, (hh + 1) * LANES)
        m, den = stats[hh]
        acc = jnp.zeros((LANES, tq), F32)
        for i, (c0, n) in enumerate(chunks):
            f = jnp.exp2(m_scr[hh, i:i + 1, :] - m) / den
            p = e_scr[hh, c0:c0 + n, :tq] * f[:, :tq] - e_scr[hh, c0:c0 + n, tq:] * (lam * f[:, tq:])
            acc = acc + _dot(vt_ref[0, sl, c0:c0 + n], p.astype(BF16))
        o_ref[0, :, sl] = _diff_post(acc.T, subg_ref[...], lam_init).astype(BF16)


def _diff(lamv, subg, qd, kd, vdt, s_lat, lam_init):
    b, sall, _ = qd.shape
    nchunk = -(-sall // DIFF_KCHUNK)
    hps = DIFF_HEADS_PER_STEP
    wid = hps * LANES
    return pl.pallas_call(
        functools.partial(_diff_kernel, n_keys=sall, lam_init=lam_init),
        grid=(b, N_HEADS // hps, s_lat // ATT_TQ),
        in_specs=[pl.BlockSpec((8, LANES), lambda i, h, j: (0, 0)),
                  pl.BlockSpec((1, LANES), lambda i, h, j: (0, 0)),
                  pl.BlockSpec((1, ATT_TQ, wid), lambda i, h, j: (i, j, h)),
                  pl.BlockSpec((1, sall, wid), lambda i, h, j: (i, 0, h)),
                  pl.BlockSpec((1, wid, sall), lambda i, h, j: (i, h, 0))],
        out_specs=pl.BlockSpec((1, ATT_TQ, wid), lambda i, h, j: (i, j, h)),
        out_shape=jax.ShapeDtypeStruct((b, s_lat, 512), BF16),
        scratch_shapes=[pltpu.VMEM((hps, sall, 2 * ATT_TQ), F32),
                        pltpu.VMEM((hps, -(-nchunk // 8) * 8, 2 * ATT_TQ), F32)],
        compiler_params=_params(("arbitrary", "arbitrary", "arbitrary")),
        name="diff_attn",
    )(lamv, subg, qd, kd, vdt)


def _ctx_kernel(sink_ref, lamv_ref, subg_ref, qa_ref, ka_ref, va_ref, qb_ref, kb_ref, vb_ref,
                qd_ref, kd_ref, vd_ref, oa_ref, ob_ref, od_ref, *, lam_init):
    cl = qa_ref.shape[1]
    ka = ka_ref[0]
    va = va_ref[0]
    for g in range(2):
        q2 = _gqa_rows(qa_ref, g)
        s = _dot_nt(q2, ka)
        snk = _gqa_sink(sink_ref, g, cl)
        m = jnp.maximum(jnp.max(s, axis=-1, keepdims=True), snk)
        e = jnp.exp(s - m)
        den = jnp.sum(e, axis=-1, keepdims=True) + jnp.exp(snk - m)
        _gqa_store(oa_ref, g, _dot(e.astype(BF16), va) / den, cl)
    for c in range(2):
        sl = slice(c * LANES, (c + 1) * LANES)
        outs = []
        for half in range(2):
            s = _dot_nt(_half_mask(qb_ref[0, :, sl], half), kb_ref[0, :, sl])
            e = jnp.exp(s - jnp.max(s, axis=-1, keepdims=True))
            outs.append(_dot(e.astype(BF16), vb_ref[0, :, sl]) / jnp.sum(e, axis=-1, keepdims=True))
        ob_ref[0, :, sl] = _merge_halves(outs[0], outs[1]).astype(BF16)
    lam = _lam(lamv_ref, lam_init)
    for h in range(N_HEADS):
        sl = slice(h * LANES, (h + 1) * LANES)
        q = qd_ref[0, :, sl]
        q12 = jnp.concatenate([_half_mask(q, 0), _half_mask(q, 1)], axis=0)
        s = _dot_nt(q12, kd_ref[0, :, sl])
        e = jnp.exp2(s - jnp.max(s, axis=-1, keepdims=True))
        den = jnp.sum(e, axis=-1, keepdims=True)
        p = e[:cl] * (1.0 / den[:cl]) - e[cl:] * (lam / den[cl:])
        o = _dot(p.astype(BF16), vd_ref[0, :, sl])
        od_ref[0, :, sl] = _diff_post(o, subg_ref[...], lam_init).astype(BF16)


def _ctx_attention(sink, lamv, subg, qa, ka, va, qb, kb, vb, qd, kd, vd, s_lat, lam_init):
    b, sall, _ = qa.shape
    cl = sall - s_lat
    blk = s_lat // cl
    row = lambda width: pl.BlockSpec((1, cl, width), lambda i: (i, blk, 0))
    out = lambda width: pl.BlockSpec((1, cl, width), lambda i: (i, 0, 0))
    return pl.pallas_call(
        functools.partial(_ctx_kernel, lam_init=lam_init),
        grid=(b,),
        in_specs=[pl.BlockSpec(memory_space=pltpu.SMEM),
                  pl.BlockSpec((8, LANES), lambda i: (0, 0)),
                  pl.BlockSpec((1, LANES), lambda i: (0, 0)),
                  row(512), row(128), row(128), row(256), row(256), row(256), row(512), row(512), row(512)],
        out_specs=[out(256), out(256), out(512)],
        out_shape=[jax.ShapeDtypeStruct((b, cl, w), BF16) for w in (256, 256, 512)],
        compiler_params=_params(("arbitrary",)),
        name="ctx_attention",
    )(sink, lamv, subg, qa, ka, va, qb, kb, vb, qd, kd, vd)


def _first_argmax(v, iota, n):
    m = jnp.max(v, axis=0, keepdims=True)
    ix = jnp.min(jnp.where(v == m, iota, float(n)), axis=0, keepdims=True)
    return m, ix


def _router_kernel(xin_ref, wout_ref, g_ref, mod_ref, whi_ref, wlo_ref, bias_ref, tri_ref, ltri_ref, *refs,
                   tiles_per_batch, nlat):
    attn_refs, (x_ref, hb_ref, w_ref, slot_ref, meta_ref, cnt_ref) = refs[:-6], refs[-6:]

    @pl.when(pl.program_id(0) == 0)
    def _():
        cnt_ref[...] = jnp.zeros_like(cnt_ref)

    def residual(oa_ref, ob_ref, od_ref):
        attn = (_dot(oa_ref[0], wout_ref[0:256, :]) + _dot(ob_ref[0], wout_ref[256:512, :])
                + _dot(od_ref[0], wout_ref[512:1024, :]))
        x_ref[...] = xin_ref[0] + mod_ref[0, 2:3, :] * attn

    if len(attn_refs) == 3:
        residual(*attn_refs)
    else:
        is_ctx = (pl.program_id(0) % tiles_per_batch) >= nlat
        pl.when(jnp.logical_not(is_ctx))(lambda: residual(*attn_refs[:3]))
        pl.when(is_ctx)(lambda: residual(*attn_refs[3:6]))

    hf = _norm_mod(x_ref[...], g_ref[...], mod_ref[0, 3:4, :], mod_ref[0, 4:5, :])
    tm = hf.shape[0]
    h_hi, h_lo = _split_bf16(hf)
    hb_ref[...] = h_hi
    whi = whi_ref[...]
    logits = _dot_nt(whi, h_hi) + _dot_nt(whi, h_lo) + _dot_nt(wlo_ref[...], h_hi)
    scores = _sigmoid(logits)
    biased = scores + bias_ref[...]
    gsz = N_EXPERTS // N_GROUPS
    iota_g = lax.broadcasted_iota(I32, (gsz, tm), 0).astype(F32)
    gscore = []
    for g in range(N_GROUPS):
        v = biased[g * gsz:(g + 1) * gsz]
        m1, i1 = _first_argmax(v, iota_g, gsz)
        m2 = jnp.max(jnp.where(iota_g == i1, -jnp.inf, v), axis=0, keepdims=True)
        gscore.append(m1 + m2)
    cur = jnp.concatenate(gscore, axis=0)
    iota_n = lax.broadcasted_iota(I32, (N_GROUPS, tm), 0).astype(F32)
    gsel = jnp.zeros((N_GROUPS, tm), F32)
    for _ in range(TOPK_GROUPS):
        _, ix = _first_argmax(cur, iota_n, N_GROUPS)
        hit = iota_n == ix
        gsel = jnp.where(hit, 1.0, gsel)
        cur = jnp.where(hit, -jnp.inf, cur)
    masked = jnp.concatenate(
        [jnp.where(gsel[g:g + 1] > 0.5, biased[g * gsz:(g + 1) * gsz], -jnp.inf) for g in range(N_GROUPS)],
        axis=0)
    iota_e = lax.broadcasted_iota(I32, (N_EXPERTS, tm), 0).astype(F32)
    onehot = jnp.zeros((N_EXPERTS, tm), F32)
    idxs, ws = [], []
    for _ in range(TOP_K):
        _, ix = _first_argmax(masked, iota_e, N_EXPERTS)
        hit = iota_e == ix
        ws.append(jnp.sum(jnp.where(hit, scores, 0.0), axis=0, keepdims=True))
        masked = jnp.where(hit, -jnp.inf, masked)
        onehot = jnp.where(hit, 1.0, onehot)
        idxs.append(ix)
    base = cnt_ref[...]
    rank = _dot(onehot.astype(BF16), tri_ref[...])
    n_col = jnp.sum(onehot, axis=1, keepdims=True)
    n_pad = jnp.floor((n_col + (ROW_CHUNK - 1)) * (1.0 / ROW_CHUNK)) * ROW_CHUNK
    start = _dot(ltri_ref[...], jnp.broadcast_to(n_pad, (N_EXPERTS, LANES)).astype(BF16))
    wsum = ws[0]
    for wk in ws[1:]:
        wsum = wsum + wk
    for k in range(TOP_K):
        slot = jnp.sum(jnp.where(iota_e == idxs[k], rank + start[:, 0:1], 0.0), axis=0, keepdims=True)
        slot_ref[0, k:k + 1, :] = slot.astype(I32)
        w_ref[0, k:k + 1, :] = ws[k] / wsum * ROUTED_SCALE
    lane = lax.broadcasted_iota(I32, (N_EXPERTS, LANES), 1)
    meta_ref[0] = jnp.where(lane == 0, n_pad, jnp.where(lane == 1, base, start))
    cnt_ref[...] = base + n_pad


def _router(xc, lat, ctx, w_out, g, modall, whi, wlo, bias, tri, ltri, s_lat):
    b, sall, d = xc.shape
    tm = TOK_TILE
    nlat = s_lat // tm
    tpb = nlat if ctx is None else sall // tm
    nt = b * tpb
    per_tok = pl.BlockSpec((1, TOP_K, tm), lambda i: (i, 0, 0))
    lat_row = lambda width: pl.BlockSpec((1, tm, width), lambda i: (i // tpb, jnp.minimum(i % tpb, nlat - 1), 0))
    ctx_row = lambda width: pl.BlockSpec((1, tm, width), lambda i: (i // tpb, 0, 0))
    widths = (256, 256, 512)
    attn_specs = [lat_row(w) for w in widths] + ([] if ctx is None else [ctx_row(w) for w in widths])

    def mod_map(i):
        return (2 * (i // tpb) + ((i % tpb) >= nlat).astype(I32), 0, 0)

    return pl.pallas_call(
        functools.partial(_router_kernel, tiles_per_batch=tpb, nlat=nlat),
        grid=(nt,),
        in_specs=[pl.BlockSpec((1, tm, d), lambda i: (i // tpb, i % tpb, 0)),
                  pl.BlockSpec((MIX_WIDTH, d), lambda i: (0, 0)),
                  pl.BlockSpec((1, d), lambda i: (0, 0)),
                  pl.BlockSpec((1, 8, d), mod_map),
                  pl.BlockSpec((N_EXPERTS, d), lambda i: (0, 0)),
                  pl.BlockSpec((N_EXPERTS, d), lambda i: (0, 0)),
                  pl.BlockSpec((N_EXPERTS, 1), lambda i: (0, 0)),
                  pl.BlockSpec((tm, tm), lambda i: (0, 0)),
                  pl.BlockSpec((N_EXPERTS, N_EXPERTS), lambda i: (0, 0))] + attn_specs,
        out_specs=[pl.BlockSpec((tm, d), lambda i: (i, 0)), pl.BlockSpec((tm, d), lambda i: (i, 0)), per_tok, per_tok,
                   pl.BlockSpec((1, N_EXPERTS, LANES), lambda i: (i, 0, 0)),
                   pl.BlockSpec((N_EXPERTS, 1), lambda i: (0, 0))],
        out_shape=[jax.ShapeDtypeStruct((nt * tm, d), F32),
                   jax.ShapeDtypeStruct((nt * tm, d), BF16),
                   jax.ShapeDtypeStruct((nt, TOP_K, tm), F32),
                   jax.ShapeDtypeStruct((nt, TOP_K, tm), I32),
                   jax.ShapeDtypeStruct((nt, N_EXPERTS, LANES), F32),
                   jax.ShapeDtypeStruct((N_EXPERTS, 1), F32)],
        compiler_params=_params(("arbitrary",)),
        name="router",
    )(xc, w_out, g, modall, whi, wlo, bias, tri, ltri, *lat, *(() if ctx is None else ctx))


def _pack_bf16_pairs(v, is_bf16_valued=False):
    half = v.shape[1] // 2
    if not is_bf16_valued:
        v = v.astype(BF16).astype(F32)
    lo = lax.shift_right_logical(lax.bitcast_convert_type(v[:, :half], U32), jnp.uint32(16))
    hi = lax.bitcast_convert_type(v[:, half:], U32) & jnp.uint32(0xFFFF0000)
    return lo | hi


def _unpack_bf16_pairs(u):
    lo = lax.bitcast_convert_type(lax.shift_left(u, jnp.uint32(16)), F32).astype(BF16)
    hi = lax.bitcast_convert_type(u & jnp.uint32(0xFFFF0000), F32).astype(BF16)
    return lo, hi


def _used_chunks(meta_ref):
    return lax.shift_right_logical(_slots_used(meta_ref), ROW_CHUNK.bit_length() - 1)


def _start_chunk_copies(first, count, make_copy):
    pairs = lax.shift_right_logical(count, 1)

    def start_pair(j, carry):
        make_copy(first + 2 * j).start()
        make_copy(first + 2 * j + 1).start()
        return carry

    lax.fori_loop(0, pairs, start_pair, 0)

    @pl.when(count > 2 * pairs)
    def _():
        make_copy(first + count - 1).start()


def _wait_chunk_copies(total, copy_of_rows):
    group = lax.shift_right_logical(total, WAIT_GROUP.bit_length() - 1)

    def wait_group(j, carry):
        copy_of_rows(WAIT_GROUP * ROW_CHUNK).wait()
        return carry

    def wait_one(j, carry):
        copy_of_rows(ROW_CHUNK).wait()
        return carry

    lax.fori_loop(0, group, wait_group, 0)
    lax.fori_loop(0, total - group * WAIT_GROUP, wait_one, 0)


def _tile_slots(tm):
    return TOP_K * tm + ROW_CHUNK * N_EXPERTS


def _slots_used(meta_ref):
    last = N_EXPERTS - 1
    return meta_ref[0, 2, last] + meta_ref[0, 0, last]


def _dispatch_kernel(meta_ref, rows_ref, hb_ref, slot_ref, xs_ref, xc_scr, sem):
    hb = hb_ref[...]
    tm = hb.shape[0]
    slot16 = slot_ref[0].astype(jnp.int16)

    total = _used_chunks(meta_ref)
    step = CHUNK_UNROLL * tm
    chunks_per_step = step // ROW_CHUNK

    def make_copy(j):
        src = pl.multiple_of(j * ROW_CHUNK, ROW_CHUNK)
        dst = pl.multiple_of(rows_ref[0, 0, j], ROW_CHUNK)
        return pltpu.make_async_copy(xc_scr.at[pl.ds(src, ROW_CHUNK)], xs_ref.at[pl.ds(dst, ROW_CHUNK)], sem)

    def select_rows(r2, carry):
        for u in range(CHUNK_UNROLL):
            r0 = pl.multiple_of((r2 * CHUNK_UNROLL + u) * tm, tm)
            row = (lax.broadcasted_iota(I32, (tm, tm), 0) + r0).astype(jnp.int16)
            sel = jnp.zeros((tm, tm), BF16)
            for k in range(TOP_K):
                sel = jnp.where(row == slot16[k:k + 1, :], jnp.ones((), BF16), sel)
            xc_scr[pl.ds(r0, tm), :] = _pack_bf16_pairs(_dot(sel, hb), is_bf16_valued=True)
        first = r2 * chunks_per_step
        _start_chunk_copies(first, jnp.clip(total - first, 0, chunks_per_step), make_copy)
        return carry

    lax.fori_loop(0, (_slots_used(meta_ref) + step - 1) // step, select_rows, 0)
    _wait_chunk_copies(total, lambda n: pltpu.make_async_copy(xc_scr.at[pl.ds(0, n)], xs_ref.at[pl.ds(0, n)], sem))


def _dispatch(meta, chunk_rows, slot, hb, n_rows):
    t, d = hb.shape
    tm = TOK_TILE
    nt = t // tm
    return pl.pallas_call(
        _dispatch_kernel,
        grid=(nt,),
        in_specs=[pl.BlockSpec((1, 3, N_EXPERTS), lambda i: (i, 0, 0), memory_space=pltpu.SMEM),
                  pl.BlockSpec((1, 1, chunk_rows.shape[2]), lambda i: (i, 0, 0), memory_space=pltpu.SMEM),
                  pl.BlockSpec((tm, d), lambda i: (i, 0)),
                  pl.BlockSpec((1, TOP_K, tm), lambda i: (i, 0, 0))],
        out_specs=pl.BlockSpec(memory_space=pl.ANY),
        out_shape=jax.ShapeDtypeStruct((n_rows, d // 2), U32),
        scratch_shapes=[pltpu.VMEM((_tile_slots(tm), d // 2), U32), pltpu.SemaphoreType.DMA(())],
        compiler_params=_params(("arbitrary",)),
        name="moe_dispatch",
    )(meta, chunk_rows, hb, slot)


def _expert_kernel(blk_e_ref, nvalid_ref, nact_ref, next_e_ref, has_next_ref, parity_ref,
                   xs_ref, wg_any, wu_any, wd_any, ys_ref, wg_s, wu_s, wd_s, wg_buf, wu_buf, wd_buf, sem, *, layer):
    b = pl.program_id(0)
    e = blk_e_ref[b]
    changed = jnp.logical_or(b == 0, e != blk_e_ref[jnp.maximum(b - 1, 0)])
    par = parity_ref[b]

    def weight_copies(expert, p):
        return [pltpu.make_async_copy(src.at[layer, expert], dst.at[p], sem.at[p])
                for src, dst in ((wg_any, wg_buf), (wu_any, wu_buf), (wd_any, wd_buf))]

    @pl.when(b == 0)
    def _():
        for c in weight_copies(e, par):
            c.start()

    @pl.when(changed)
    def _():
        for c in weight_copies(e, par):
            c.wait()
        wg_s[...] = wg_buf[par].astype(BF16)
        wu_s[...] = wu_buf[par].astype(BF16)
        wd_s[...] = wd_buf[par].astype(BF16)

        @pl.when(has_next_ref[b] == 1)
        def _():
            for c in weight_copies(next_e_ref[b], 1 - par):
                c.start()

    half = xs_ref.shape[1]
    n_here = jnp.where(b < nact_ref[0], nvalid_ref[b], 0)

    @pl.when(n_here > 0)
    def _():
        row = lax.broadcasted_iota(I32, xs_ref.shape, 0)
        xs = jnp.where(row < n_here, xs_ref[...], jnp.zeros(xs_ref.shape, U32))
        x_lo, x_hi = _unpack_bf16_pairs(xs)
        gate = _dot(x_lo, wg_s[0:half, :]) + _dot(x_hi, wg_s[half:, :])
        up = _dot(x_lo, wu_s[0:half, :]) + _dot(x_hi, wu_s[half:, :])
        mid = (gate * _sigmoid(gate)) * up
        ys_ref[...] = _pack_bf16_pairs(_dot(mid.astype(BF16), wd_s[...]))

    @pl.when(b == nact_ref[0])
    def _():
        ys_ref[...] = jnp.zeros_like(ys_ref)


def _experts(blk_e, nvalid, nact, xs, w_gate, w_up, w_down, layer):
    n_rows, half = xs.shape
    _, _, d, f = w_gate.shape
    n_blk = n_rows // MOE_BLOCK
    first_of_next = jnp.sum((blk_e[None, :] <= blk_e[:, None]).astype(I32), axis=1)
    has_next = (first_of_next < n_blk).astype(I32)
    next_e = blk_e[jnp.minimum(first_of_next, n_blk - 1)]
    starts = jnp.concatenate([jnp.zeros((1,), I32), (blk_e[1:] != blk_e[:-1]).astype(I32)])
    parity = jnp.cumsum(starts) % 2
    anyspec = pl.BlockSpec(memory_space=pl.ANY)
    grid_spec = pltpu.PrefetchScalarGridSpec(
        num_scalar_prefetch=6,
        grid=(n_blk,),
        in_specs=[pl.BlockSpec((MOE_BLOCK, half), lambda i, be, nv, na, *_: (jnp.minimum(i, na[0] - 1), 0)),
                  anyspec, anyspec, anyspec],
        out_specs=pl.BlockSpec((MOE_BLOCK, half), lambda i, be, nv, na, *_: (jnp.minimum(i, na[0]), 0)),
        scratch_shapes=[pltpu.VMEM((d, f), BF16), pltpu.VMEM((d, f), BF16), pltpu.VMEM((f, d), BF16),
                        pltpu.VMEM((2, d, f), F32), pltpu.VMEM((2, d, f), F32), pltpu.VMEM((2, f, d), F32),
                        pltpu.SemaphoreType.DMA((2,))],
    )
    return pl.pallas_call(
        functools.partial(_expert_kernel, layer=layer),
        grid_spec=grid_spec,
        out_shape=jax.ShapeDtypeStruct((n_rows, half), U32),
        compiler_params=_params(("arbitrary",)),
        name="moe_experts",
    )(blk_e, nvalid, nact, next_e, has_next, parity.astype(I32), xs, w_gate, w_up, w_down)


def _combine_kernel(meta_ref, rows_ref, x_ref, hb_ref, mod_ref, spos_ref, wt_ref, wsg_ref, wsu_ref, wsd_ref,
                    gfin_ref, ys_ref, o_ref, y_scr, acc_scr, sb_scr, wb_scr, sem, *, final_norm):
    tm, d = x_ref.shape
    half = d // 2

    @pl.when(pl.program_id(0) == 0)
    def _():
        y_scr[...] = jnp.zeros_like(y_scr)

    step = CHUNK_UNROLL * COMBINE_KCHUNK
    chunks_per_step = step // ROW_CHUNK
    group_shift = chunks_per_step.bit_length() - 1

    def make_copy(j):
        src = pl.multiple_of(rows_ref[0, 0, j], ROW_CHUNK)
        dst = pl.multiple_of(j * ROW_CHUNK, ROW_CHUNK)
        return pltpu.make_async_copy(ys_ref.at[pl.ds(src, ROW_CHUNK)], y_scr.at[pl.ds(dst, ROW_CHUNK)],
                                     sem.at[lax.shift_right_logical(j, group_shift)])

    total = _used_chunks(meta_ref)
    _start_chunk_copies(0, total, make_copy)

    x = x_ref[...]
    hb = hb_ref[...]
    gate = _dot(hb, wsg_ref[...])
    mid = (gate * _sigmoid(gate)) * _dot(hb, wsu_ref[...])
    acc_scr[...] = _dot(mid.astype(BF16), wsd_ref[...])

    n_used = _slots_used(meta_ref)
    for k in range(TOP_K):
        sb_scr[k] = jnp.broadcast_to(spos_ref[0, :, k:k + 1], (tm, LANES)).astype(jnp.int16)
        wb_scr[k] = jnp.broadcast_to(wt_ref[0, :, k:k + 1], (tm, LANES)).astype(BF16)

    def chunk(j2, carry):
        _wait_chunk_copies(jnp.clip(total - j2 * chunks_per_step, 0, chunks_per_step),
                           lambda n: pltpu.make_async_copy(ys_ref.at[pl.ds(0, n)], y_scr.at[pl.ds(0, n)], sem.at[j2]))
        lo_sum = jnp.zeros((tm, half), F32)
        hi_sum = jnp.zeros((tm, half), F32)
        for u in range(CHUNK_UNROLL):
            r0 = pl.multiple_of((j2 * CHUNK_UNROLL + u) * COMBINE_KCHUNK, COMBINE_KCHUNK)
            parts = []
            for c in range(COMBINE_KCHUNK // LANES):
                col = (lax.broadcasted_iota(I32, (tm, LANES), 1) + (r0 + c * LANES)).astype(jnp.int16)
                p = jnp.zeros((tm, LANES), BF16)
                for k in range(TOP_K):
                    p = jnp.where(col == sb_scr[k], wb_scr[k], p)
                parts.append(p)
            pb = jnp.concatenate(parts, axis=1)
            y_lo, y_hi = _unpack_bf16_pairs(y_scr[pl.ds(r0, COMBINE_KCHUNK), :])
            lo_sum = lo_sum + _dot(pb, y_lo)
            hi_sum = hi_sum + _dot(pb, y_hi)
        acc_scr[:, :half] += lo_sum
        acc_scr[:, half:] += hi_sum
        return carry

    lax.fori_loop(0, (n_used + step - 1) // step, chunk, 0)
    out = x + mod_ref[0, 5:6, :] * acc_scr[...]
    if final_norm:
        out = _rms(out, gfin_ref[...])
    o_ref[...] = out


def _combine(meta, chunk_rows, spos_t, wt, xflat, hb, modall, wsg, wsu, wsd, gfin, ys, tiles_per_batch, nlat,
             final_norm):
    t, d = xflat.shape
    f = wsg.shape[1]
    tm = TOK_TILE
    nt = t // tm
    y_rows = _tile_slots(tm)
    assert y_rows % COMBINE_KCHUNK == 0

    def mod_map(i):
        return (2 * (i // tiles_per_batch) + ((i % tiles_per_batch) >= nlat).astype(I32), 0, 0)

    return pl.pallas_call(
        functools.partial(_combine_kernel, final_norm=final_norm),
        grid=(nt,),
        in_specs=[pl.BlockSpec((1, 3, N_EXPERTS), lambda i: (i, 0, 0), memory_space=pltpu.SMEM),
                  pl.BlockSpec((1, 1, chunk_rows.shape[2]), lambda i: (i, 0, 0), memory_space=pltpu.SMEM),
                  pl.BlockSpec((tm, d), lambda i: (i, 0)),
                  pl.BlockSpec((tm, d), lambda i: (i, 0)),
                  pl.BlockSpec((1, 8, d), mod_map),
                  pl.BlockSpec((1, tm, TOP_K), lambda i: (i, 0, 0)),
                  pl.BlockSpec((1, tm, TOP_K), lambda i: (i, 0, 0)),
                  pl.BlockSpec((d, f), lambda i: (0, 0)),
                  pl.BlockSpec((d, f), lambda i: (0, 0)),
                  pl.BlockSpec((f, d), lambda i: (0, 0)),
                  pl.BlockSpec((1, d), lambda i: (0, 0)),
                  pl.BlockSpec(memory_space=pl.ANY)],
        out_specs=pl.BlockSpec((tm, d), lambda i: (i, 0)),
        out_shape=jax.ShapeDtypeStruct((t, d), F32),
        scratch_shapes=[pltpu.VMEM((y_rows, d // 2), U32), pltpu.VMEM((tm, d), F32),
                        pltpu.VMEM((TOP_K, tm, LANES), jnp.int16), pltpu.VMEM((TOP_K, tm, LANES), BF16),
                        pltpu.SemaphoreType.DMA((y_rows // (CHUNK_UNROLL * COMBINE_KCHUNK),))],
        compiler_params=_params(("arbitrary",)),
        name="moe_combine",
    )(meta, chunk_rows, xflat, hb, modall, spos_t, wt, wsg, wsu, wsd, gfin, ys)


def _rope_tables(s_lat, c_len):
    n_freq = HEAD_DIM // 4
    inv = 1.0 / (ROPE_BASE ** (jnp.arange(n_freq, dtype=F32) / n_freq))
    t = jnp.arange(s_lat)
    pos = jnp.stack([(t // GRID_W).astype(F32), (t % GRID_W).astype(F32)], axis=1)
    lane = np.arange(LANES)
    ang = pos[:, (lane % HEAD_DIM) // 32] * inv[lane % 16][None, :]
    sign = jnp.asarray(np.where(lane % 32 < 16, -1.0, 1.0), F32)
    cos = jnp.concatenate([jnp.cos(ang), jnp.ones((c_len, LANES), F32)], axis=0)
    sin = jnp.concatenate([jnp.sin(ang) * sign, jnp.zeros((c_len, LANES), F32)], axis=0)
    return cos, sin


def _moe(xc, lat, ctx, lp, modall, gfin, s_lat, final_norm):
    d = xc.shape[2]
    nlat = s_lat // TOK_TILE
    tiles_per_batch = nlat if ctx is None else xc.shape[1] // TOK_TILE
    g = lp["g_ffn"].reshape(1, d)
    w_r = lp["w_router"].T
    whi = w_r.astype(BF16)
    wlo = (w_r - whi.astype(F32)).astype(BF16)
    tri = jnp.asarray(np.triu(np.ones((TOK_TILE, TOK_TILE), np.float32), 1), BF16)
    ltri = jnp.asarray(np.tril(np.ones((N_EXPERTS, N_EXPERTS), np.float32), -1), BF16)
    xflat, hb, w, slot, meta, cnt = _router(xc, lat, ctx, lp["w_out"].astype(BF16), g, modall, whi, wlo,
                                            lp["router_bias"].reshape(N_EXPERTS, 1), tri, ltri, s_lat)
    t = xflat.shape[0]
    nt = t // TOK_TILE
    meta = jnp.transpose(meta[:, :, :3].astype(I32), (0, 2, 1))
    rows_e = cnt[:, 0].astype(I32)
    seg = (rows_e + MOE_BLOCK - 1) // MOE_BLOCK * MOE_BLOCK
    pend = jnp.cumsum(seg)
    pstart = (pend - seg).astype(I32)
    n_blk = (t * TOP_K + ROW_CHUNK * N_EXPERTS * nt) // MOE_BLOCK + N_EXPERTS
    blk_row0 = jnp.arange(n_blk, dtype=I32) * MOE_BLOCK
    blk_e = jnp.minimum(jnp.sum((pend[None, :] <= blk_row0[:, None]).astype(I32), axis=1), N_EXPERTS - 1)
    of_blk = blk_e[:, None] == jnp.arange(N_EXPERTS, dtype=I32)[None, :]
    seg_end = jnp.sum(jnp.where(of_blk, (pstart + rows_e)[None, :], 0), axis=1)
    nvalid = jnp.clip(seg_end - blk_row0, 0, MOE_BLOCK).astype(I32)
    nact = (pend[-1] // MOE_BLOCK).astype(I32).reshape(1)
    n_pad, base, start = meta[:, 0, :], meta[:, 1, :], meta[:, 2, :]
    chunk_slot = jnp.arange(_tile_slots(TOK_TILE) // ROW_CHUNK, dtype=I32) * ROW_CHUNK
    s = chunk_slot[None, :, None]
    in_run = (start[:, None, :] <= s) & (s < (start + n_pad)[:, None, :])
    row0 = pstart[None, :] + base - start
    chunk_rows = (jnp.sum(jnp.where(in_run, row0[:, None, :], 0), axis=2) + chunk_slot[None, :])[:, None, :]
    xs = _dispatch(meta, chunk_rows, slot, hb, n_blk * MOE_BLOCK)
    ys = _experts(blk_e, nvalid, nact, xs, lp["w_gate"], lp["w_up"], lp["w_down"], lp["layer"])
    return _combine(meta, chunk_rows, jnp.transpose(slot, (0, 2, 1)), jnp.transpose(w, (0, 2, 1)), xflat, hb, modall,
                    lp["ws_gate"].astype(BF16), lp["ws_up"].astype(BF16), lp["ws_down"].astype(BF16), gfin, ys,
                    tiles_per_batch, nlat, final_norm)


def _layer(xc, cs, cos, sin, lp, layer_idx, last, s_lat, gfin):
    b, sall, d = xc.shape
    c_len = sall - s_lat
    lam_init = 0.8 - 0.6 * math.exp(-0.3 * layer_idx)
    mod = _modulation(cs, lp["w_mod"], lp["b_mod"])
    mod_lat = mod[:b].reshape(b, 1, 6, d)
    mod_ctx = jnp.broadcast_to(mod[b].reshape(1, 1, 6, d), (b, 1, 6, d))
    modall = jnp.concatenate([mod_lat, mod_ctx], axis=1)
    modall = jnp.pad(modall, ((0, 0), (0, 0), (0, 2), (0, 0))).reshape(2 * b, 8, d)

    qa, qb, qd, ka, va, kb, vb, kd, vd, vdt = _projection(
        xc, lp["g_mix"].reshape(1, d), modall, cos, sin, lp["w_in"].astype(BF16), s_lat)
    sink = lp["attn_sink"].astype(F32)
    lamv = jnp.zeros((8, LANES), F32).at[0:4, 0:HEAD_DIM].set(
        jnp.stack([lp["lam_q1"], lp["lam_k1"], lp["lam_q2"], lp["lam_k2"]]).astype(F32))
    subg = lp["subln_g"].reshape(1, LANES).astype(F32)
    oa, ob = _local_attention(sink, qa, ka, va, qb, kb, vb, _na_bias_mask(lp["na_rpb"], s_lat // GRID_W), s_lat)
    od = _diff(lamv, subg, qd, kd, vdt, s_lat, lam_init)
    ctx_out = None
    if not last:
        ctx_out = _ctx_attention(sink, lamv, subg, qa, ka, va, qb, kb, vb, qd, kd, vd, s_lat, lam_init)
    n_rows = s_lat if last else sall
    y = _moe(xc, (oa, ob, od), ctx_out, lp, modall, gfin, s_lat, last)
    return y.reshape(b, n_rows, d)


_LAYER_KEYS = ("w_mod", "b_mod", "g_mix", "g_ffn", "w_in", "w_out", "attn_sink", "na_rpb", "lam_q1", "lam_k1",
               "lam_q2", "lam_k2", "subln_g", "w_router", "router_bias", "w_gate", "w_up", "w_down",
               "ws_gate", "ws_up", "ws_down")


def kernel(x, c, ctx, c_ctx, w_mod, b_mod, g_mix, g_ffn, w_in, w_out, attn_sink, na_rpb, lam_q1, lam_k1, lam_q2,
           lam_k2, subln_g, w_router, router_bias, w_gate, w_up, w_down, ws_gate, ws_up, ws_down, g_final):
    stacked = dict(zip(_LAYER_KEYS, (w_mod, b_mod, g_mix, g_ffn, w_in, w_out, attn_sink, na_rpb, lam_q1, lam_k1,
                                     lam_q2, lam_k2, subln_g, w_router, router_bias, w_gate, w_up, w_down,
                                     ws_gate, ws_up, ws_down)))
    b, s_lat, d = x.shape
    c_len = ctx.shape[1]
    depth = w_mod.shape[0]
    assert s_lat % (NA_QROWS * GRID_W) == 0 and s_lat % c_len == 0 and c_len == TOK_TILE
    cs = jnp.zeros((16, d), F32).at[:b].set(c).at[b].set(c_ctx)
    cos, sin = _rope_tables(s_lat, c_len)
    xc = jnp.concatenate([x, ctx], axis=1)
    gfin = g_final.reshape(1, d)
    for i in range(depth):
        big = ("w_gate", "w_up", "w_down")
        lp = {k: (v if k in big else v[i]) for k, v in stacked.items()}
        lp["layer"] = i
        xc = _layer(xc, cs, cos, sin, lp, i, i == depth - 1, s_lat, gfin)
    return xc
```
